```python
import jax, jax.numpy as jnp
from jax import lax
import numpy as np

D_MODEL = 2048
BATCH = 1
SEQ = 8192
DEPTH = 1
DEC_BATCH = 8
DEC_SEQ = 4096
PAST_LEN = 128

MIX_WIDTH = D_MODEL
RET_WIDTH = MIX_WIDTH // 2
GLA_WIDTH = MIX_WIDTH - RET_WIDTH
RET_HEADS = 4
RET_DK = RET_WIDTH // RET_HEADS
RET_DV = RET_WIDTH // RET_HEADS
GLA_HEADS = 4
GLA_DK = GLA_WIDTH // (2 * GLA_HEADS)
GLA_DV = GLA_WIDTH // GLA_HEADS
GLA_KEY_WIDTH = GLA_HEADS * GLA_DK
GLA_RANK = 16
GLA_TAU = 16.0
RET_CHUNK = 128
GLA_CHUNK = 64
ROPE_BASE = 10000.0
N_EXPERTS = 16
CAPACITY_FACTOR = 2
D_FF_EXPERT = D_MODEL
EPS = 1e-6
IN_WIDTH = 4 * RET_WIDTH + 2 * GLA_KEY_WIDTH + 2 * GLA_WIDTH + 2 * GLA_RANK

kernel_name = 'hybrid_retention_gla_ec_moe_encoder'


def rmsnorm(x, w):
    xf = x.astype(jnp.float32)
    y = xf * lax.rsqrt(jnp.mean(xf * xf, axis=-1, keepdims=True) + EPS)
    return (y * w.astype(jnp.float32)).astype(x.dtype)


def head_rmsnorm(y, w):
    H, dv = y.shape[-2], y.shape[-1]
    yn = y * lax.rsqrt(jnp.mean(y * y, axis=-1, keepdims=True) + EPS)
    return yn * w.astype(jnp.float32).reshape(H, dv)


def rope(x):
    L, d = x.shape[1], x.shape[-1]
    inv = ROPE_BASE ** (-jnp.arange(0, d, 2, dtype=jnp.float32) / d)
    ang = jnp.arange(L, dtype=jnp.float32)[:, None] * inv[None, :]
    cos = jnp.cos(ang)[None, :, None, :]
    sin = jnp.sin(ang)[None, :, None, :]
    x1, x2 = x[..., : d // 2], x[..., d // 2:]
    return jnp.concatenate([x1 * cos - x2 * sin, x1 * sin + x2 * cos], axis=-1)


def _chunks(t, c):
    B, L, H, d = t.shape
    return t.reshape(B, L // c, c, H, d).transpose(1, 0, 3, 2, 4)


def _unchunk(t):
    N, B, H, C, d = t.shape
    return t.transpose(1, 0, 3, 2, 4).reshape(B, N * C, H, d)


def retention_scan(q, k, v, log_gamma):
    B, L, H, dk = q.shape
    dv = v.shape[-1]
    C = RET_CHUNK
    pos = jnp.arange(C, dtype=jnp.float32)
    diff = pos[:, None] - pos[None, :]
    lg = log_gamma[:, None, None]
    intra = jnp.exp(jnp.where(diff >= 0, lg * diff, -jnp.inf))
    q_dec = jnp.exp(log_gamma[:, None] * (pos + 1.0))[..., None]
    k_dec = jnp.exp(log_gamma[:, None] * (C - 1.0 - pos))[..., None]
    c_dec = jnp.exp(log_gamma * C)[:, None, None]

    def step(S, inp):
        qi, ki, vi = inp
        scores = jnp.einsum('bhid,bhjd->bhij', qi, ki) * intra
        o = (jnp.einsum('bhij,bhjv->bhiv', scores, vi)
             + jnp.einsum('bhid,bhdv->bhiv', qi, S) * q_dec)
        S = S * c_dec + jnp.einsum('bhjd,bhjv->bhdv', ki * k_dec, vi)
        return S, o

    S0 = jnp.zeros((B, H, dk, dv), jnp.float32)
    _, o = lax.scan(step, S0, (_chunks(q, C), _chunks(k, C), _chunks(v, C)))
    return _unchunk(o)


def gla_scan(q, k, v, log_a):
    B, L, H, dk = q.shape
    dv = v.shape[-1]
    C = GLA_CHUNK
    causal = jnp.tril(jnp.ones((C, C), dtype=bool))[:, :, None]

    def step(S, inp):
        qi, ki, vi, ai = inp
        b = jnp.cumsum(ai, axis=2)
        expo = b[:, :, :, None, :] - b[:, :, None, :, :]
        w = jnp.exp(jnp.where(causal, expo, -jnp.inf))
        scores = jnp.einsum('bhid,bhijd->bhij', qi, ki[:, :, None, :, :] * w)
        b_last = b[:, :, -1:, :]
        o = (jnp.einsum('bhij,bhjv->bhiv', scores, vi)
             + jnp.einsum('bhid,bhdv->bhiv', qi * jnp.exp(b), S))
        S = (jnp.exp(b_last[:, :, 0, :])[..., None] * S
             + jnp.einsum('bhjd,bhjv->bhdv', ki * jnp.exp(b_last - b), vi))
        return S, o

    S0 = jnp.zeros((B, H, dk, dv), jnp.float32)
    _, o = lax.scan(step, S0, (_chunks(q, C), _chunks(k, C), _chunks(v, C), _chunks(log_a, C)))
    return _unchunk(o)


def _flip(t):
    return jnp.flip(t, axis=1)


def token_mixers(xn, w_in, ret_decay_logit, ret_gn_w, gla_gate_up, gla_gate_bias, gla_gn_w, w_out):
    B, L, _ = xn.shape
    f32 = jnp.float32
    proj = jnp.einsum('bld,dn->bln', xn, w_in).astype(f32)
    sizes = [RET_WIDTH] * 4 + [GLA_KEY_WIDTH, GLA_KEY_WIDTH, GLA_WIDTH, GLA_WIDTH, 2 * GLA_RANK]
    points = [int(p) for p in np.cumsum(sizes)[:-1]]
    rq, rk, rv, rg, gq, gk, gv, gg, ga = jnp.split(proj, points, axis=-1)

    rq = rope(rq.reshape(B, L, RET_HEADS, RET_DK)) * (RET_DK ** -0.5)
    rk = rope(rk.reshape(B, L, RET_HEADS, RET_DK))
    rv = rv.reshape(B, L, RET_HEADS, RET_DV)
    log_gamma = jax.nn.log_sigmoid(ret_decay_logit.astype(f32))
    ret = (retention_scan(rq, rk, rv, log_gamma[0])
           + _flip(retention_scan(_flip(rq), _flip(rk), _flip(rv), log_gamma[1])))
    ret_out = head_rmsnorm(ret, ret_gn_w).reshape(B, L, RET_WIDTH) * jax.nn.silu(rg)

    gq = gq.reshape(B, L, GLA_HEADS, GLA_DK) * (GLA_DK ** -0.5)
    gk = gk.reshape(B, L, GLA_HEADS, GLA_DK)
    gv = gv.reshape(B, L, GLA_HEADS, GLA_DV)
    up = gla_gate_up.astype(f32)
    bias = gla_gate_bias.astype(f32)
    la_f = jax.nn.log_sigmoid(jnp.einsum('blr,rk->blk', ga[..., :GLA_RANK], up[0]) + bias[0]) / GLA_TAU
    la_b = jax.nn.log_sigmoid(jnp.einsum('blr,rk->blk', ga[..., GLA_RANK:], up[1]) + bias[1]) / GLA_TAU
    la_f = la_f.reshape(B, L, GLA_HEADS, GLA_DK)
    la_b = la_b.reshape(B, L, GLA_HEADS, GLA_DK)
    gla = (gla_scan(gq, gk, gv, la_f)
           + _flip(gla_scan(_flip(gq), _flip(gk), _flip(gv), _flip(la_b))))
    gla_out = head_rmsnorm(gla, gla_gn_w).reshape(B, L, GLA_WIDTH) * jax.nn.silu(gg)

    mix = jnp.concatenate([ret_out, gla_out], axis=-1).astype(xn.dtype)
    return jnp.einsum('bln,nd->bld', mix, w_out)


def expert_choice_ffn(xn, router_w, w_gate, w_up, w_down):
    B, L, D = xn.shape
    T = B * L
    cap = CAPACITY_FACTOR * T // N_EXPERTS
    xt = xn.reshape(T, D)
    logits = jnp.einsum('td,de->te', xt, router_w).astype(jnp.float32)
    affinity = jax.nn.softmax(logits, axis=-1)
    gate, idx = lax.top_k(affinity.T, cap)
    xe = xt[idx]
    hdn = (jax.nn.silu(jnp.einsum('ecd,edf->ecf', xe, w_gate))
           * jnp.einsum('ecd,edf->ecf', xe, w_up))
    ye = jnp.einsum('ecf,efd->ecd', hdn, w_down) * gate[..., None].astype(xn.dtype)
    out = jnp.zeros((T, D), ye.dtype).at[idx.reshape(-1)].add(ye.reshape(-1, D))
    return out.reshape(B, L, D).astype(xn.dtype)


def encoder_trunk(x, norm1_w, w_in, ret_decay_logit, ret_gn_w, gla_gate_up, gla_gate_bias,
                  gla_gn_w, w_out, norm2_w, router_w, w_gate, w_up, w_down, normf_w):
    for layer in range(DEPTH):
        x = x + token_mixers(rmsnorm(x, norm1_w[layer]), w_in[layer], ret_decay_logit[layer],
                             ret_gn_w[layer], gla_gate_up[layer], gla_gate_bias[layer],
                             gla_gn_w[layer], w_out[layer])
        x = x + expert_choice_ffn(rmsnorm(x, norm2_w[layer]), router_w[layer],
                                  w_gate[layer], w_up[layer], w_down[layer])
    return rmsnorm(x, normf_w)


def setup_inputs(seed: int = 0) -> dict:
    key = jax.random.key(seed)
    ks = jax.random.split(key, 18)
    f32 = jnp.float32
    nrm = lambda k, shape: jax.random.normal(k, shape, f32)
    base_logit = jnp.log(2.0 ** (5.0 + jnp.arange(RET_HEADS, dtype=f32)) - 1.0)
    return {
        'x_prompt': nrm(ks[0], (BATCH, SEQ, D_MODEL)),
        'x_sample': nrm(ks[1], (DEC_BATCH, DEC_SEQ, D_MODEL)),
        'norm1_w': 1.0 + 0.02 * nrm(ks[2], (DEPTH, D_MODEL)),
        'w_in': nrm(ks[3], (DEPTH, D_MODEL, IN_WIDTH)) * D_MODEL ** -0.5,
        'ret_decay_logit': base_logit + 0.1 * nrm(ks[4], (DEPTH, 2, RET_HEADS)),
        'ret_gn_w': 1.0 + 0.02 * nrm(ks[5], (DEPTH, RET_WIDTH)),
        'gla_gate_up': nrm(ks[6], (DEPTH, 2, GLA_RANK, GLA_KEY_WIDTH)) * GLA_RANK ** -0.5,
        'gla_gate_bias': 0.1 * nrm(ks[7], (DEPTH, 2, GLA_KEY_WIDTH)),
        'gla_gn_w': 1.0 + 0.02 * nrm(ks[8], (DEPTH, GLA_WIDTH)),
        'w_out': nrm(ks[9], (DEPTH, MIX_WIDTH, D_MODEL)) * MIX_WIDTH ** -0.5,
        'norm2_w': 1.0 + 0.02 * nrm(ks[10], (DEPTH, D_MODEL)),
        'router_w': nrm(ks[11], (DEPTH, D_MODEL, N_EXPERTS)) * D_MODEL ** -0.5,
        'w_gate': nrm(ks[12], (DEPTH, N_EXPERTS, D_MODEL, D_FF_EXPERT)) * D_MODEL ** -0.5,
        'w_up': nrm(ks[13], (DEPTH, N_EXPERTS, D_MODEL, D_FF_EXPERT)) * D_MODEL ** -0.5,
        'w_down': nrm(ks[14], (DEPTH, N_EXPERTS, D_FF_EXPERT, D_MODEL)) * D_FF_EXPERT ** -0.5,
        'normf_w': 1.0 + 0.02 * nrm(ks[15], (D_MODEL,)),
    }


def reference(x_prompt, x_sample, norm1_w, w_in, ret_decay_logit, ret_gn_w, gla_gate_up,
              gla_gate_bias, gla_gn_w, w_out, norm2_w, router_w, w_gate, w_up, w_down, normf_w):
    y_prompt = encoder_trunk(x_prompt, norm1_w, w_in, ret_decay_logit, ret_gn_w, gla_gate_up,
                             gla_gate_bias, gla_gn_w, w_out, norm2_w, router_w, w_gate, w_up,
                             w_down, normf_w)
    y_sample = encoder_trunk(x_sample, norm1_w, w_in, ret_decay_logit, ret_gn_w, gla_gate_up,
                             gla_gate_bias, gla_gn_w, w_out, norm2_w, router_w, w_gate, w_up,
                             w_down, normf_w)
    return (y_prompt, y_sample)
```

```python
import functools

import numpy as np
import jax
import jax.numpy as jnp
from jax import lax
from jax.experimental import pallas as pl
from jax.experimental.pallas import tpu as pltpu

F32, BF16, I32 = jnp.float32, jnp.bfloat16, jnp.int32

D_MODEL = 2048
RET_WIDTH = 1024
RET_HEADS = 4
RET_DK = 256
RET_DV = 256
GLA_WIDTH = 1024
GLA_HEADS = 4
GLA_DK = 128
GLA_DV = 256
GLA_KEY_WIDTH = 512
GLA_RANK = 16
GLA_TAU = 16.0
RET_CHUNK = 128
GLA_CHUNK = 64
GLA_SUB = 16
ROPE_BASE = 10000.0
N_EXPERTS = 16
CAPACITY_FACTOR = 2
D_FF = 2048
EPS = 1e-6
IN_MAIN = 4 * RET_WIDTH + 2 * GLA_KEY_WIDTH + 2 * GLA_WIDTH

_RQ, _RK, _RV, _RG = 0, 1024, 2048, 3072
_GQ, _GK, _GV, _GG = 4096, 4608, 5120, 6144

LANE = 128
MOE_TT = 256
MOE_TS = 256
V7X_VMEM_BYTES = 64 * 1024 * 1024


def _cparams(sem, vmem_mb):
    return pltpu.CompilerParams(dimension_semantics=sem, vmem_limit_bytes=vmem_mb * 1024 * 1024)


def _log_sigmoid(z):
    return jnp.minimum(z, 0.0) - jnp.log1p(jnp.exp(-jnp.abs(z)))


def _silu(g):
    return g * (1.0 / (1.0 + jnp.exp(-g)))


def _dot_nt(a, b):
    return lax.dot_general(a, b, (((1,), (1,)), ((), ())), preferred_element_type=F32)


def _dot_tn(a, b):
    return lax.dot_general(a, b, (((0,), (0,)), ((), ())), preferred_element_type=F32)


def _dot(a, b):
    return jnp.dot(a, b, preferred_element_type=F32)


IP_TM = 1024
IP_TN = 1024


def _in_proj_kernel(x_ref, n1_ref, w_ref, wga_ref, cs_ref, cos_ref, sin_ref, o_ref, ga_ref, xn_ref):
    j = pl.program_id(1)

    @pl.when(j == 0)
    def _():
        x = x_ref[...]
        ms = jnp.mean(x * x, axis=-1, keepdims=True)
        xn = (x * lax.rsqrt(ms + EPS) * n1_ref[...]).astype(BF16)
        xn_ref[...] = xn
        ga_ref[...] = _dot(xn, wga_ref[...])

    acc = _dot(xn_ref[...], w_ref[...]) * cs_ref[...]
    n_rope_blocks = 2 * RET_WIDTH // IP_TN

    @pl.when(j < n_rope_blocks)
    def _():
        cos = cos_ref[...]
        sin = sin_ref[...]
        half = RET_DK // 2
        for h in range(IP_TN // RET_DK):
            lo = h * RET_DK
            x1 = acc[:, lo:lo + half]
            x2 = acc[:, lo + half:lo + RET_DK]
            o_ref[:, lo:lo + half] = (x1 * cos - x2 * sin).astype(BF16)
            o_ref[:, lo + half:lo + RET_DK] = (x1 * sin + x2 * cos).astype(BF16)

    @pl.when(j >= n_rope_blocks)
    def _():
        o_ref[...] = acc.astype(BF16)


def _in_proj(x2d, n1w, w_main, w_ga, colscale, cos, sin, seq_len):
    T = x2d.shape[0]
    tm, tn = IP_TM, IP_TN
    nlb = seq_len // tm
    return pl.pallas_call(
        _in_proj_kernel,
        grid=(T // tm, IN_MAIN // tn),
        in_specs=[
            pl.BlockSpec((tm, D_MODEL), lambda i, j: (i, 0)),
            pl.BlockSpec((1, D_MODEL), lambda i, j: (0, 0)),
            pl.BlockSpec((D_MODEL, tn), lambda i, j: (0, j)),
            pl.BlockSpec((D_MODEL, LANE), lambda i, j: (0, 0)),
            pl.BlockSpec((1, tn), lambda i, j: (0, j)),
            pl.BlockSpec((tm, LANE), lambda i, j: (i % nlb, 0)),
            pl.BlockSpec((tm, LANE), lambda i, j: (i % nlb, 0)),
        ],
        out_specs=[
            pl.BlockSpec((tm, tn), lambda i, j: (i, j)),
            pl.BlockSpec((tm, LANE), lambda i, j: (i, 0)),
        ],
        out_shape=[
            jax.ShapeDtypeStruct((T, IN_MAIN), BF16),
            jax.ShapeDtypeStruct((T, LANE), F32),
        ],
        scratch_shapes=[pltpu.VMEM((tm, D_MODEL), BF16)],
        compiler_params=_cparams(("parallel", "arbitrary"), 48),
        name="in_proj",
    )(x2d, n1w, w_main, w_ga, colscale, cos, sin)


GATE_TM = 256


def _gates_kernel(ga_ref, up_ref, bias_ref, lf_ref, lb_ref, bf_ref, bb_ref):
    z = _dot(ga_ref[...].astype(BF16), up_ref[...]) + bias_ref[...]
    la = _log_sigmoid(z) * (1.0 / GLA_TAU)
    hi = la.astype(BF16)
    lo = (la - hi.astype(F32)).astype(BF16)
    kw = GLA_KEY_WIDTH
    bf_ref[...] = _dot(lf_ref[...], hi[:, :kw]) + _dot(lf_ref[...], lo[:, :kw])
    bb_ref[...] = _dot(lb_ref[...], hi[:, kw:]) + _dot(lb_ref[...], lo[:, kw:])


def _gla_gates(ga, up_pad, bias, lf, lb):
    T = ga.shape[0]
    tm = GATE_TM
    kw = GLA_KEY_WIDTH
    return pl.pallas_call(
        _gates_kernel,
        grid=(T // tm,),
        in_specs=[
            pl.BlockSpec((tm, LANE), lambda i: (i, 0)),
            pl.BlockSpec((LANE, 2 * kw), lambda i: (0, 0)),
            pl.BlockSpec((1, 2 * kw), lambda i: (0, 0)),
            pl.BlockSpec((tm, tm), lambda i: (0, 0)),
            pl.BlockSpec((tm, tm), lambda i: (0, 0)),
        ],
        out_specs=[pl.BlockSpec((tm, kw), lambda i: (i, 0)), pl.BlockSpec((tm, kw), lambda i: (i, 0))],
        out_shape=[jax.ShapeDtypeStruct((T, kw), F32), jax.ShapeDtypeStruct((T, kw), F32)],
        compiler_params=_cparams(("parallel",), 32),
        name="gla_gates",
    )(ga, up_pad, bias, lf, lb)


def _finish_heads(tot, gn, gate):
    ms = jnp.mean(tot * tot, axis=-1, keepdims=True)
    yn = tot * lax.rsqrt(ms + EPS) * gn
    return (yn * _silu(gate.astype(F32))).astype(BF16)


RET_TB = 512


def _ret_kernel(dl_ref, q_ref, k_ref, v_ref, *rest, reverse):
    if reverse:
        g_ref, of_ref, gn_ref, o_ref, s_ref, intra_ref, qd_ref, kd_ref, cd_ref = rest
    else:
        o_ref, s_ref, intra_ref, qd_ref, kd_ref, cd_ref = rest
    h = pl.program_id(1)
    n = pl.program_id(2)
    C = RET_CHUNK

    @pl.when(n == 0)
    def _():
        s_ref[...] = jnp.zeros_like(s_ref)
        logit = dl_ref[1 if reverse else 0, h]
        lg = _log_sigmoid(jnp.full((C, RET_DV), logit, F32))
        lg_c = _log_sigmoid(jnp.full((C, C), logit, F32))
        lg_r = _log_sigmoid(jnp.full((1, RET_DV), logit, F32))
        ri = lax.broadcasted_iota(I32, (C, RET_DV), 0).astype(F32)
        rc = lax.broadcasted_iota(I32, (C, C), 0).astype(F32)
        cc = lax.broadcasted_iota(I32, (C, C), 1).astype(F32)
        diff = (cc - rc) if reverse else (rc - cc)
        intra_ref[...] = jnp.where(diff >= 0, jnp.exp(lg_c * diff), 0.0)
        if reverse:
            qd_ref[...] = jnp.exp(lg * (C - ri))
            kd_ref[...] = jnp.exp(lg * ri)
        else:
            qd_ref[...] = jnp.exp(lg * (ri + 1.0))
            kd_ref[...] = jnp.exp(lg * (C - 1.0 - ri))
        cd_ref[...] = jnp.exp(lg_r * C)

    nchunks = q_ref.shape[0] // C
    order = range(nchunks - 1, -1, -1) if reverse else range(nchunks)
    for c in order:
        rows = slice(c * C, (c + 1) * C)
        q = q_ref[rows, :]
        k = k_ref[rows, :]
        v = v_ref[rows, :]
        s = _dot_nt(q, k) * intra_ref[...]
        state = s_ref[...]
        o = _dot(s.astype(BF16), v) + _dot(q, state.astype(BF16)) * qd_ref[...]
        kd = (k.astype(F32) * kd_ref[...]).astype(BF16)
        s_ref[...] = state * cd_ref[...] + _dot_tn(kd, v)
        if reverse:
            o_ref[rows, :] = _finish_heads(of_ref[rows, :] + o, gn_ref[...], g_ref[rows, :])
        else:
            o_ref[rows, :] = o


def _ret_scan(proj, decay_logit, batch, seq_len, reverse, o_fwd=None, gn_w=None):
    T = proj.shape[0]
    tb = RET_TB
    nb = seq_len // tb
    dk, dv, C = RET_DK, RET_DV, RET_CHUNK

    def rb(b, n):
        return b * nb + ((nb - 1 - n) if reverse else n)

    def col(base):
        return lambda b, h, n: (rb(b, n), base // dk + h)

    in_specs = [
        pl.BlockSpec(memory_space=pltpu.SMEM),
        pl.BlockSpec((tb, dk), col(_RQ)),
        pl.BlockSpec((tb, dk), col(_RK)),
        pl.BlockSpec((tb, dv), col(_RV)),
    ]
    args = [decay_logit, proj, proj, proj]
    if reverse:
        in_specs += [
            pl.BlockSpec((tb, dv), col(_RG)),
            pl.BlockSpec((tb, dv), lambda b, h, n: (rb(b, n), h)),
            pl.BlockSpec((1, dv), lambda b, h, n: (0, h)),
        ]
        args += [proj, o_fwd, gn_w]
    out_dtype = BF16 if reverse else F32
    return pl.pallas_call(
        functools.partial(_ret_kernel, reverse=reverse),
        grid=(batch, RET_HEADS, nb),
        in_specs=in_specs,
        out_specs=pl.BlockSpec((tb, dv), lambda b, h, n: (rb(b, n), h)),
        out_shape=jax.ShapeDtypeStruct((T, RET_WIDTH), out_dtype),
        scratch_shapes=[
            pltpu.VMEM((dk, dv), F32),
            pltpu.VMEM((C, C), F32),
            pltpu.VMEM((C, dv), F32),
            pltpu.VMEM((C, dk), F32),
            pltpu.VMEM((1, dv), F32),
        ],
        compiler_params=_cparams(("parallel", "parallel", "arbitrary"), 32),
        name="ret_bwd" if reverse else "ret_fwd",
    )(*args)


GLA_TB = 512


def _gla_kernel(q_ref, k_ref, v_ref, b_ref, *rest, reverse):
    if reverse:
        g_ref, of_ref, gn_ref, o_ref, st_ref = rest
    else:
        o_ref, st_ref = rest
    n = pl.program_id(2)
    C, SUB = GLA_CHUNK, GLA_SUB
    NS = C // SUB

    @pl.when(n == 0)
    def _():
        st_ref[...] = jnp.zeros_like(st_ref)

    nchunks = q_ref.shape[0] // C
    row_c = lax.broadcasted_iota(I32, (C, GLA_DK), 0)
    lane_s = lax.broadcasted_iota(I32, (SUB, C), 1)
    row_s = lax.broadcasted_iota(I32, (SUB, C), 0)

    def chunk(ci, carry):
        c = (nchunks - 1 - ci) if reverse else ci
        c0 = pl.multiple_of(c * C, C)
        q = q_ref[pl.ds(c0, C), :].astype(F32)
        k = k_ref[pl.ds(c0, C), :].astype(F32)
        v = v_ref[pl.ds(c0, C), :]
        b = b_ref[pl.ds(c0, C), :]
        b_end = b[0:1, :] if reverse else b[C - 1:C, :]

        st = st_ref[...]
        o = _dot_nt((q * jnp.exp(b)).astype(BF16), st.astype(BF16))
        ke = (k * jnp.exp(b_end - b)).astype(BF16)
        st_ref[...] = st * jnp.exp(b_end) + _dot_tn(v, ke)

        prows = []
        for si in range(NS):
            r0 = si * SUB
            b_i = b[r0:r0 + SUB, :]
            q_i = q[r0:r0 + SUB, :]
            sd = jnp.zeros((SUB, C), F32)
            for jj in range(SUB):
                b_j = b[r0 + jj:r0 + jj + 1, :]
                k_j = k[r0 + jj:r0 + jj + 1, :]
                w = jnp.exp(jnp.minimum(b_i - b_j, 0.0))
                col = jnp.sum(q_i * k_j * w, axis=-1, keepdims=True)
                sd = jnp.where(lane_s == r0 + jj, col, sd)
            if reverse:
                causal = (row_s + r0) <= lane_s
                ref_row = b[r0 + SUB - 1:r0 + SUB, :]
                has_off = si < NS - 1
                off_rows = row_c >= r0 + SUB
            else:
                causal = (row_s + r0) >= lane_s
                ref_row = b[r0:r0 + 1, :]
                has_off = si > 0
                off_rows = row_c < r0
            s_i = jnp.where(causal, sd, 0.0)
            if has_off:
                qs = (q_i * jnp.exp(b_i - ref_row)).astype(BF16)
                kk = jnp.where(off_rows, k * jnp.exp(jnp.minimum(ref_row - b, 0.0)), 0.0).astype(BF16)
                s_i = s_i + _dot_nt(qs, kk)
            prows.append(s_i)
        p = jnp.concatenate(prows, axis=0).astype(BF16)
        o = o + _dot(p, v)
        if reverse:
            tot = of_ref[pl.ds(c0, C), :] + o
            o_ref[pl.ds(c0, C), :] = _finish_heads(tot, gn_ref[...], g_ref[pl.ds(c0, C), :])
        else:
            o_ref[pl.ds(c0, C), :] = o
        return carry

    lax.fori_loop(0, nchunks, chunk, 0)


def _gla_scan(proj, bcum, batch, seq_len, reverse, o_fwd=None, gn_w=None):
    T = proj.shape[0]
    tb = GLA_TB
    nb = seq_len // tb
    dk, dv = GLA_DK, GLA_DV

    def rb(b, n):
        return b * nb + ((nb - 1 - n) if reverse else n)

    in_specs = [
        pl.BlockSpec((tb, dk), lambda b, h, n: (rb(b, n), _GQ // dk + h)),
        pl.BlockSpec((tb, dk), lambda b, h, n: (rb(b, n), _GK // dk + h)),
        pl.BlockSpec((tb, dv), lambda b, h, n: (rb(b, n), _GV // dv + h)),
        pl.BlockSpec((tb, dk), lambda b, h, n: (rb(b, n), h)),
    ]
    args = [proj, proj, proj, bcum]
    if reverse:
        in_specs += [
            pl.BlockSpec((tb, dv), lambda b, h, n: (rb(b, n), _GG // dv + h)),
            pl.BlockSpec((tb, dv), lambda b, h, n: (rb(b, n), h)),
            pl.BlockSpec((1, dv), lambda b, h, n: (0, h)),
        ]
        args += [proj, o_fwd, gn_w]
    out_dtype = BF16 if reverse else F32
    return pl.pallas_call(
        functools.partial(_gla_kernel, reverse=reverse),
        grid=(batch, GLA_HEADS, nb),
        in_specs=in_specs,
        out_specs=pl.BlockSpec((tb, dv), lambda b, h, n: (rb(b, n), h)),
        out_shape=jax.ShapeDtypeStruct((T, GLA_WIDTH), out_dtype),
        scratch_shapes=[pltpu.VMEM((dv, dk), F32)],
        compiler_params=_cparams(("parallel", "parallel", "arbitrary"), 32),
        name="gla_bwd" if reverse else "gla_fwd",
    )(*args)


OP_TM = 256


def _out_proj_kernel(mr_ref, mg_ref, w0_ref, w1_ref, x_ref, n2_ref, rh_ref, rl_ref, h_ref, xn_ref, aff_ref):
    h = x_ref[...] + _dot(mr_ref[...], w0_ref[...]) + _dot(mg_ref[...], w1_ref[...])
    h_ref[...] = h
    ms = jnp.mean(h * h, axis=-1, keepdims=True)
    xn = h * lax.rsqrt(ms + EPS) * n2_ref[...]
    xh = xn.astype(BF16)
    xn_ref[...] = xh
    xl = (xn - xh.astype(F32)).astype(BF16)
    lt = _dot_nt(rh_ref[...], xh) + _dot_nt(rh_ref[...], xl) + _dot_nt(rl_ref[...], xh)
    m = jnp.max(lt, axis=0, keepdims=True)
    e = jnp.exp(lt - m)
    aff_ref[...] = e / jnp.sum(e, axis=0, keepdims=True)


def _out_proj(mix_r, mix_g, w_out, x2d, n2w, r_hi, r_lo):
    T = x2d.shape[0]
    tm = OP_TM
    half = RET_WIDTH
    return pl.pallas_call(
        _out_proj_kernel,
        grid=(T // tm,),
        in_specs=[
            pl.BlockSpec((tm, half), lambda i: (i, 0)),
            pl.BlockSpec((tm, half), lambda i: (i, 0)),
            pl.BlockSpec((half, D_MODEL), lambda i: (0, 0)),
            pl.BlockSpec((half, D_MODEL), lambda i: (1, 0)),
            pl.BlockSpec((tm, D_MODEL), lambda i: (i, 0)),
            pl.BlockSpec((1, D_MODEL), lambda i: (0, 0)),
            pl.BlockSpec((N_EXPERTS, D_MODEL), lambda i: (0, 0)),
            pl.BlockSpec((N_EXPERTS, D_MODEL), lambda i: (0, 0)),
        ],
        out_specs=[
            pl.BlockSpec((tm, D_MODEL), lambda i: (i, 0)),
            pl.BlockSpec((tm, D_MODEL), lambda i: (i, 0)),
            pl.BlockSpec((N_EXPERTS, tm), lambda i: (0, i)),
        ],
        out_shape=[
            jax.ShapeDtypeStruct((T, D_MODEL), F32),
            jax.ShapeDtypeStruct((T, D_MODEL), BF16),
            jax.ShapeDtypeStruct((N_EXPERTS, T), F32),
        ],
        compiler_params=_cparams(("parallel",), 48),
        name="out_proj",
    )(mix_r, mix_g, w_out, w_out, x2d, n2w, r_hi, r_lo)


def _select_kernel(a_ref, pos_ref, *, cap):
    E, T = a_ref.shape
    tt = MOE_TT

    def count(pred):
        return jnp.sum(pred.astype(F32), axis=1, keepdims=True)

    def bisect(i, tau):
        cand = tau | jnp.left_shift(jnp.int32(1), 30 - i)
        bits = pltpu.bitcast(a_ref[...], I32)
        return jnp.where(count(bits >= cand) >= cap, cand, tau)

    tau = lax.fori_loop(0, 31, bisect, jnp.zeros((E, 1), I32))
    bits_all = pltpu.bitcast(a_ref[...], I32)
    quota = cap - count(bits_all > tau)

    before = (lax.broadcasted_iota(I32, (tt, tt), 0) < lax.broadcasted_iota(I32, (tt, tt), 1)).astype(BF16)

    def block(j, carry):
        c_eq, c_sel = carry
        off = pl.multiple_of(j * tt, tt)
        bits = pltpu.bitcast(a_ref[:, pl.ds(off, tt)], I32)
        eq = bits == tau
        eqf = eq.astype(F32)
        rank_eq = _dot(eqf.astype(BF16), before) + c_eq
        sel = (bits > tau) | (eq & (rank_eq < quota))
        self_ = sel.astype(F32)
        slot = _dot(self_.astype(BF16), before) + c_sel
        pos_ref[:, pl.ds(off, tt)] = jnp.where(sel, slot, -1.0).astype(I32)
        return (c_eq + jnp.sum(eqf, axis=1, keepdims=True), c_sel + jnp.sum(self_, axis=1, keepdims=True))

    zero = jnp.zeros((E, 1), F32)
    lax.fori_loop(0, T // tt, block, (zero, zero))


def _select(aff, cap):
    E, T = aff.shape
    return pl.pallas_call(
        functools.partial(_select_kernel, cap=cap),
        out_shape=jax.ShapeDtypeStruct((E, T), I32),
        compiler_params=pltpu.CompilerParams(vmem_limit_bytes=32 * 1024 * 1024),
        name="select",
    )(aff)


def _pair_lists(pos, cap):
    E, T = pos.shape
    tt, ts = MOE_TT, MOE_TS
    nb, nsb = T // tt, cap // ts
    ncand = tt // ts + 1
    n = jnp.sum((pos >= 0).reshape(E, nb, tt), axis=-1).astype(I32)
    c1 = jnp.cumsum(n, axis=1)
    c0 = c1 - n
    sb0 = jnp.minimum(c0 // ts, nsb - 1)
    sb1 = jnp.where(n > 0, (c1 - 1) // ts, sb0)
    cand = jnp.arange(ncand, dtype=I32)
    sb = sb0[..., None] + cand
    valid = (sb <= sb1[..., None]) & (n[..., None] > 0)
    sb = jnp.minimum(sb, nsb - 1)
    e_idx = jnp.broadcast_to(jnp.arange(E, dtype=I32)[:, None, None], sb.shape)
    j_idx = jnp.broadcast_to(jnp.arange(nb, dtype=I32)[None, :, None], sb.shape)

    def compact(order, valid_o, key_fn, pmax):
        ev, jv, sv = (jnp.transpose(a, order).reshape(-1) for a in (e_idx, j_idx, sb))
        vv = valid_o.reshape(-1)
        total = jnp.sum(vv.astype(I32))
        dst = jnp.where(vv, jnp.cumsum(vv.astype(I32)) - 1, pmax)
        outs = [jnp.zeros((pmax,), I32).at[dst].set(a, mode="drop") for a in (ev, jv, sv)]
        real = jnp.arange(pmax, dtype=I32) < total
        outs = [jnp.where(real, a, a[total - 1]) for a in outs]
        key = key_fn(*outs)
        first = jnp.concatenate([jnp.ones((1,), bool), key[1:] != key[:-1]])
        last = jnp.concatenate([key[1:] != key[:-1], jnp.ones((1,), bool)])
        last = last | (jnp.arange(pmax, dtype=I32) == total - 1)
        flag = jnp.where(real, first.astype(I32) + 2 * last.astype(I32) + 4, 0)
        return outs + [flag]

    d_lists = compact((0, 1, 2), valid, lambda e, j, s: e * nsb + s, E * (nb + nsb))
    forced = (e_idx == 0) & (cand == 0)
    valid_c = jnp.transpose(valid | forced, (1, 0, 2))
    c_lists = compact((1, 0, 2), valid_c, lambda e, j, s: j, E * (nb + nsb) + nb)
    return d_lists, c_lists


def _dispatch_kernel(pe_ref, pj_ref, ps_ref, pf_ref, pos_ref, x_ref, o_ref, acc_ref):
    p = pl.program_id(0)
    flag = pf_ref[p]
    ts, tt = MOE_TS, MOE_TT

    @pl.when((flag & 1) != 0)
    def _():
        acc_ref[...] = jnp.zeros_like(acc_ref)

    @pl.when((flag & 4) != 0)
    def _():
        slot = lax.broadcasted_iota(I32, (ts, tt), 0) + ps_ref[p] * ts
        onehot = jnp.where(pos_ref[...] == slot, 1.0, 0.0).astype(BF16)
        acc_ref[...] += _dot(onehot, x_ref[...])

    @pl.when((flag & 2) != 0)
    def _():
        o_ref[...] = acc_ref[...].astype(BF16)


def _dispatch(lists, pos3, xn2, cap):
    pe, pj, ps, pf = lists
    E = pos3.shape[0]
    ts, tt = MOE_TS, MOE_TT
    grid_spec = pltpu.PrefetchScalarGridSpec(
        num_scalar_prefetch=4,
        grid=(pe.shape[0],),
        in_specs=[
            pl.BlockSpec((None, 1, tt), lambda p, pe, pj, ps, pf: (pe[p], 0, pj[p])),
            pl.BlockSpec((tt, D_MODEL), lambda p, pe, pj, ps, pf: (pj[p], 0)),
        ],
        out_specs=pl.BlockSpec((None, ts, D_MODEL), lambda p, pe, pj, ps, pf: (pe[p], ps[p], 0)),
        scratch_shapes=[pltpu.VMEM((ts, D_MODEL), F32)],
    )
    return pl.pallas_call(
        _dispatch_kernel,
        grid_spec=grid_spec,
        out_shape=jax.ShapeDtypeStruct((E, cap, D_MODEL), BF16),
        compiler_params=_cparams(("arbitrary",), 32),
        name="dispatch",
    )(pe, pj, ps, pf, pos3, xn2)


FFN_TM = 1024
FFN_TF = 256


def _ffn_kernel(x_ref, wg_ref, wu_ref, wd_ref, o_ref, acc_ref):
    f = pl.program_id(2)
    x = x_ref[...]
    g = _dot(x, wg_ref[...].astype(BF16))
    u = _dot(x, wu_ref[...].astype(BF16))
    hid = (_silu(g) * u).astype(BF16)
    part = _dot(hid, wd_ref[...].astype(BF16))

    @pl.when(f == 0)
    def _():
        acc_ref[...] = part

    @pl.when(f > 0)
    def _():
        acc_ref[...] += part

    @pl.when(f == pl.num_programs(2) - 1)
    def _():
        o_ref[...] = acc_ref[...].astype(BF16)


def _ffn(xe, w_gate, w_up, w_down):
    E, cap, _ = xe.shape
    tm, tf = min(FFN_TM, cap), FFN_TF
    return pl.pallas_call(
        _ffn_kernel,
        grid=(E, cap // tm, D_FF // tf),
        in_specs=[
            pl.BlockSpec((None, tm, D_MODEL), lambda e, m, f: (e, m, 0)),
            pl.BlockSpec((None, D_MODEL, tf), lambda e, m, f: (e, 0, f)),
            pl.BlockSpec((None, D_MODEL, tf), lambda e, m, f: (e, 0, f)),
            pl.BlockSpec((None, tf, D_MODEL), lambda e, m, f: (e, f, 0)),
        ],
        out_specs=pl.BlockSpec((None, tm, D_MODEL), lambda e, m, f: (e, m, 0)),
        out_shape=jax.ShapeDtypeStruct((E, cap, D_MODEL), BF16),
        scratch_shapes=[pltpu.VMEM((tm, D_MODEL), F32)],
        compiler_params=_cparams(("parallel", "parallel", "arbitrary"), 56),
        name="ffn",
    )(xe, w_gate, w_up, w_down)


def _combine_kernel(pe_ref, pj_ref, ps_ref, pf_ref, pos_ref, gate_ref, ye_ref, h_ref, nf_ref, o_ref, acc_ref):
    p = pl.program_id(0)
    flag = pf_ref[p]
    ts, tt = MOE_TS, MOE_TT

    @pl.when((flag & 1) != 0)
    def _():
        acc_ref[...] = h_ref[...]

    @pl.when((flag & 4) != 0)
    def _():
        slot = lax.broadcasted_iota(I32, (ts, tt), 0) + ps_ref[p] * ts
        weights = jnp.where(pos_ref[...] == slot, gate_ref[...], 0.0).astype(BF16)
        acc_ref[...] += _dot_tn(weights, ye_ref[...])

    @pl.when((flag & 2) != 0)
    def _():
        y = acc_ref[...]
        ms = jnp.mean(y * y, axis=-1, keepdims=True)
        o_ref[...] = y * lax.rsqrt(ms + EPS) * nf_ref[...]


def _combine(lists, pos3, aff3, ye, h, nfw):
    pe, pj, ps, pf = lists
    T = h.shape[0]
    ts, tt = MOE_TS, MOE_TT
    grid_spec = pltpu.PrefetchScalarGridSpec(
        num_scalar_prefetch=4,
        grid=(pe.shape[0],),
        in_specs=[
            pl.BlockSpec((None, 1, tt), lambda p, pe, pj, ps, pf: (pe[p], 0, pj[p])),
            pl.BlockSpec((None, 1, tt), lambda p, pe, pj, ps, pf: (pe[p], 0, pj[p])),
            pl.BlockSpec((None, ts, D_MODEL), lambda p, pe, pj, ps, pf: (pe[p], ps[p], 0)),
            pl.BlockSpec((tt, D_MODEL), lambda p, pe, pj, ps, pf: (pj[p], 0)),
            pl.BlockSpec((1, D_MODEL), lambda p, pe, pj, ps, pf: (0, 0)),
        ],
        out_specs=pl.BlockSpec((tt, D_MODEL), lambda p, pe, pj, ps, pf: (pj[p], 0)),
        scratch_shapes=[pltpu.VMEM((tt, D_MODEL), F32)],
    )
    return pl.pallas_call(
        _combine_kernel,
        grid_spec=grid_spec,
        out_shape=jax.ShapeDtypeStruct((T, D_MODEL), F32),
        compiler_params=_cparams(("arbitrary",), 32),
        name="combine",
    )(pe, pj, ps, pf, pos3, aff3, ye, h, nfw)


def _rope_tables(seq_len):
    d = RET_DK
    inv = ROPE_BASE ** (-jnp.arange(0, d, 2, dtype=F32) / d)
    ang = jnp.arange(seq_len, dtype=F32)[:, None] * inv[None, :]
    return jnp.cos(ang), jnp.sin(ang)


def _chunk_tri(n, chunk, upper):
    r = np.arange(n)
    same = (r[:, None] // chunk) == (r[None, :] // chunk)
    tri = (r[:, None] <= r[None, :]) if upper else (r[:, None] >= r[None, :])
    return jnp.asarray(same & tri, BF16)


def _prep_params(norm1_w, w_in, ret_gn_w, gla_gate_up, gla_gate_bias, gla_gn_w, w_out, norm2_w, router_w,
                 normf_w):
    w = w_in[0]
    w_main = w[:, :IN_MAIN].astype(BF16)
    w_ga = jnp.pad(w[:, IN_MAIN:], ((0, 0), (0, LANE - 2 * GLA_RANK))).astype(BF16)
    cs = np.ones((1, IN_MAIN), np.float32)
    cs[:, _RQ:_RQ + RET_WIDTH] = RET_DK ** -0.5
    cs[:, _GQ:_GQ + GLA_KEY_WIDTH] = GLA_DK ** -0.5
    up = gla_gate_up[0].astype(F32)
    up_pad = jnp.zeros((LANE, 2 * GLA_KEY_WIDTH), F32)
    up_pad = up_pad.at[:GLA_RANK, :GLA_KEY_WIDTH].set(up[0])
    up_pad = up_pad.at[GLA_RANK:2 * GLA_RANK, GLA_KEY_WIDTH:].set(up[1])
    rt = router_w[0].T.astype(F32)
    r_hi = rt.astype(BF16)
    r_lo = (rt - r_hi.astype(F32)).astype(BF16)
    return dict(
        n1w=norm1_w[0].reshape(1, D_MODEL).astype(F32),
        w_main=w_main, w_ga=w_ga, colscale=jnp.asarray(cs),
        up_pad=up_pad.astype(BF16),
        bias=gla_gate_bias[0].reshape(1, 2 * GLA_KEY_WIDTH).astype(F32),
        lf=_chunk_tri(GATE_TM, GLA_CHUNK, upper=False),
        lb=_chunk_tri(GATE_TM, GLA_CHUNK, upper=True),
        ret_gn=ret_gn_w[0].reshape(1, RET_WIDTH).astype(F32),
        gla_gn=gla_gn_w[0].reshape(1, GLA_WIDTH).astype(F32),
        w_out=w_out[0].astype(BF16),
        n2w=norm2_w[0].reshape(1, D_MODEL).astype(F32),
        r_hi=r_hi, r_lo=r_lo,
        nfw=normf_w.reshape(1, D_MODEL).astype(F32),
    )


def _trunk(x, pp, decay_logit, w_gate, w_up, w_down):
    B, L, _ = x.shape
    T = B * L
    x2d = x.reshape(T, D_MODEL)
    cos, sin = _rope_tables(L)
    proj, ga = _in_proj(x2d, pp["n1w"], pp["w_main"], pp["w_ga"], pp["colscale"], cos, sin, L)
    b_f, b_b = _gla_gates(ga, pp["up_pad"], pp["bias"], pp["lf"], pp["lb"])

    ret_f = _ret_scan(proj, decay_logit, B, L, reverse=False)
    mix_r = _ret_scan(proj, decay_logit, B, L, reverse=True, o_fwd=ret_f, gn_w=pp["ret_gn"])
    gla_f = _gla_scan(proj, b_f, B, L, reverse=False)
    mix_g = _gla_scan(proj, b_b, B, L, reverse=True, o_fwd=gla_f, gn_w=pp["gla_gn"])

    h, xn2, aff = _out_proj(mix_r, mix_g, pp["w_out"], x2d, pp["n2w"], pp["r_hi"], pp["r_lo"])

    cap = CAPACITY_FACTOR * T // N_EXPERTS
    pos = _select(aff, cap)
    d_lists, c_lists = _pair_lists(pos, cap)
    pos3 = pos.reshape(N_EXPERTS, 1, T)
    xe = _dispatch(d_lists, pos3, xn2, cap)
    ye = _ffn(xe, w_gate, w_up, w_down)
    y = _combine(c_lists, pos3, aff.reshape(N_EXPERTS, 1, T), ye, h, pp["nfw"])
    return y.reshape(B, L, D_MODEL)


def kernel(x_prompt, x_sample, norm1_w, w_in, ret_decay_logit, ret_gn_w, gla_gate_up, gla_gate_bias,
           gla_gn_w, w_out, norm2_w, router_w, w_gate, w_up, w_down, normf_w):
    pp = _prep_params(norm1_w, w_in, ret_gn_w, gla_gate_up, gla_gate_bias, gla_gn_w, w_out, norm2_w,
                      router_w, normf_w)
    decay_logit = ret_decay_logit[0].astype(F32)
    args = (pp, decay_logit, w_gate[0], w_up[0], w_down[0])
    return (_trunk(x_prompt, *args), _trunk(x_sample, *args))
```

```python
import functools

import numpy as np
import jax
import jax.numpy as jnp
from jax import lax
from jax.experimental import pallas as pl
from jax.experimental.pallas import tpu as pltpu
from jax.experimental.pallas import tpu_sc as plsc

F32, BF16, I32 = jnp.float32, jnp.bfloat16, jnp.int32

D_MODEL = 2048
RET_WIDTH = 1024
RET_HEADS = 4
RET_DK = 256
RET_DV = 256
GLA_WIDTH = 1024
GLA_HEADS = 4
GLA_DK = 128
GLA_DV = 256
GLA_KEY_WIDTH = 512
GLA_RANK = 16
GLA_TAU = 16.0
RET_CHUNK = 128
GLA_CHUNK = 64
GLA_SUB = 16
ROPE_BASE = 10000.0
N_EXPERTS = 16
CAPACITY_FACTOR = 2
D_FF = 2048
EPS = 1e-6
IN_MAIN = 4 * RET_WIDTH + 2 * GLA_KEY_WIDTH + 2 * GLA_WIDTH

_RQ, _RK, _RV, _RG = 0, 1024, 2048, 3072
_GQ, _GK, _GV, _GG = 4096, 4608, 5120, 6144

LANE = 128
MOE_TT = 256
MOE_TS = 256
V7X_VMEM_BYTES = 64 * 1024 * 1024


def _cparams(sem, vmem_mb):
    return pltpu.CompilerParams(dimension_semantics=sem, vmem_limit_bytes=vmem_mb * 1024 * 1024)


def _log_sigmoid(z):
    return jnp.minimum(z, 0.0) - jnp.log1p(jnp.exp(-jnp.abs(z)))


def _silu(g):
    return g * (1.0 / (1.0 + jnp.exp(-g)))


def _dot_nt(a, b):
    return lax.dot_general(a, b, (((1,), (1,)), ((), ())), preferred_element_type=F32)


def _dot_tn(a, b):
    return lax.dot_general(a, b, (((0,), (0,)), ((), ())), preferred_element_type=F32)


def _dot(a, b):
    return jnp.dot(a, b, preferred_element_type=F32)


IP_TM = 1024
IP_TN = 1024


def _in_proj_kernel(x_ref, n1_ref, w_ref, wga_ref, cs_ref, cos_ref, sin_ref, o_ref, ga_ref, xn_ref):
    j = pl.program_id(1)

    @pl.when(j == 0)
    def _():
        x = x_ref[...]
        ms = jnp.mean(x * x, axis=-1, keepdims=True)
        xn = (x * lax.rsqrt(ms + EPS) * n1_ref[...]).astype(BF16)
        xn_ref[...] = xn
        ga_ref[...] = _dot(xn, wga_ref[...])

    acc = _dot(xn_ref[...], w_ref[...]) * cs_ref[...]
    n_rope_blocks = 2 * RET_WIDTH // IP_TN

    @pl.when(j < n_rope_blocks)
    def _():
        cos = cos_ref[...]
        sin = sin_ref[...]
        half = RET_DK // 2
        for h in range(IP_TN // RET_DK):
            lo = h * RET_DK
            x1 = acc[:, lo:lo + half]
            x2 = acc[:, lo + half:lo + RET_DK]
            o_ref[:, lo:lo + half] = (x1 * cos - x2 * sin).astype(BF16)
            o_ref[:, lo + half:lo + RET_DK] = (x1 * sin + x2 * cos).astype(BF16)

    @pl.when(j >= n_rope_blocks)
    def _():
        o_ref[...] = acc.astype(BF16)


def _in_proj(x2d, n1w, w_main, w_ga, colscale, cos, sin, seq_len):
    T = x2d.shape[0]
    tm, tn = IP_TM, IP_TN
    nlb = seq_len // tm
    return pl.pallas_call(
        _in_proj_kernel,
        grid=(T // tm, IN_MAIN // tn),
        in_specs=[
            pl.BlockSpec((tm, D_MODEL), lambda i, j: (i, 0)),
            pl.BlockSpec((1, D_MODEL), lambda i, j: (0, 0)),
            pl.BlockSpec((D_MODEL, tn), lambda i, j: (0, j)),
            pl.BlockSpec((D_MODEL, LANE), lambda i, j: (0, 0)),
            pl.BlockSpec((1, tn), lambda i, j: (0, j)),
            pl.BlockSpec((tm, LANE), lambda i, j: (i % nlb, 0)),
            pl.BlockSpec((tm, LANE), lambda i, j: (i % nlb, 0)),
        ],
        out_specs=[
            pl.BlockSpec((tm, tn), lambda i, j: (i, j)),
            pl.BlockSpec((tm, LANE), lambda i, j: (i, 0)),
        ],
        out_shape=[
            jax.ShapeDtypeStruct((T, IN_MAIN), BF16),
            jax.ShapeDtypeStruct((T, LANE), F32),
        ],
        scratch_shapes=[pltpu.VMEM((tm, D_MODEL), BF16)],
        compiler_params=_cparams(("parallel", "arbitrary"), 48),
        name="in_proj",
    )(x2d, n1w, w_main, w_ga, colscale, cos, sin)


GATE_TM = 256


def _gates_kernel(ga_ref, up_ref, bias_ref, lf_ref, lb_ref, bf_ref, bb_ref):
    z = _dot(ga_ref[...].astype(BF16), up_ref[...]) + bias_ref[...]
    la = _log_sigmoid(z) * (1.0 / GLA_TAU)
    hi = la.astype(BF16)
    lo = (la - hi.astype(F32)).astype(BF16)
    kw = GLA_KEY_WIDTH
    bf_ref[...] = _dot(lf_ref[...], hi[:, :kw]) + _dot(lf_ref[...], lo[:, :kw])
    bb_ref[...] = _dot(lb_ref[...], hi[:, kw:]) + _dot(lb_ref[...], lo[:, kw:])


def _gla_gates(ga, up_pad, bias, lf, lb):
    T = ga.shape[0]
    tm = GATE_TM
    kw = GLA_KEY_WIDTH
    return pl.pallas_call(
        _gates_kernel,
        grid=(T // tm,),
        in_specs=[
            pl.BlockSpec((tm, LANE), lambda i: (i, 0)),
            pl.BlockSpec((LANE, 2 * kw), lambda i: (0, 0)),
            pl.BlockSpec((1, 2 * kw), lambda i: (0, 0)),
            pl.BlockSpec((tm, tm), lambda i: (0, 0)),
            pl.BlockSpec((tm, tm), lambda i: (0, 0)),
        ],
        out_specs=[pl.BlockSpec((tm, kw), lambda i: (i, 0)), pl.BlockSpec((tm, kw), lambda i: (i, 0))],
        out_shape=[jax.ShapeDtypeStruct((T, kw), F32), jax.ShapeDtypeStruct((T, kw), F32)],
        compiler_params=_cparams(("parallel",), 32),
        name="gla_gates",
    )(ga, up_pad, bias, lf, lb)


def _finish_heads(tot, gn, gate):
    ms = jnp.mean(tot * tot, axis=-1, keepdims=True)
    yn = tot * lax.rsqrt(ms + EPS) * gn
    return (yn * _silu(gate.astype(F32))).astype(BF16)


RET_TB = 512


def _ret_kernel(dl_ref, q_ref, k_ref, v_ref, *rest, reverse):
    if reverse:
        g_ref, of_ref, gn_ref, o_ref, s_ref, intra_ref, qd_ref, kd_ref, cd_ref = rest
    else:
        o_ref, s_ref, intra_ref, qd_ref, kd_ref, cd_ref = rest
    h = pl.program_id(1)
    n = pl.program_id(2)
    C = RET_CHUNK

    @pl.when(n == 0)
    def _():
        s_ref[...] = jnp.zeros_like(s_ref)
        logit = dl_ref[1 if reverse else 0, h]
        lg = _log_sigmoid(jnp.full((C, RET_DV), logit, F32))
        lg_c = _log_sigmoid(jnp.full((C, C), logit, F32))
        lg_r = _log_sigmoid(jnp.full((1, RET_DV), logit, F32))
        ri = lax.broadcasted_iota(I32, (C, RET_DV), 0).astype(F32)
        rc = lax.broadcasted_iota(I32, (C, C), 0).astype(F32)
        cc = lax.broadcasted_iota(I32, (C, C), 1).astype(F32)
        diff = (cc - rc) if reverse else (rc - cc)
        intra_ref[...] = jnp.where(diff >= 0, jnp.exp(lg_c * diff), 0.0)
        if reverse:
            qd_ref[...] = jnp.exp(lg * (C - ri))
            kd_ref[...] = jnp.exp(lg * ri)
        else:
            qd_ref[...] = jnp.exp(lg * (ri + 1.0))
            kd_ref[...] = jnp.exp(lg * (C - 1.0 - ri))
        cd_ref[...] = jnp.exp(lg_r * C)

    nchunks = q_ref.shape[0] // C
    order = range(nchunks - 1, -1, -1) if reverse else range(nchunks)
    for c in order:
        rows = slice(c * C, (c + 1) * C)
        q = q_ref[rows, :]
        k = k_ref[rows, :]
        v = v_ref[rows, :]
        s = _dot_nt(q, k) * intra_ref[...]
        state = s_ref[...]
        o = _dot(s.astype(BF16), v) + _dot(q, state.astype(BF16)) * qd_ref[...]
        kd = (k.astype(F32) * kd_ref[...]).astype(BF16)
        s_ref[...] = state * cd_ref[...] + _dot_tn(kd, v)
        if reverse:
            o_ref[rows, :] = _finish_heads(of_ref[rows, :] + o, gn_ref[...], g_ref[rows, :])
        else:
            o_ref[rows, :] = o


def _ret_scan(proj, decay_logit, batch, seq_len, reverse, o_fwd=None, gn_w=None):
    T = proj.shape[0]
    tb = RET_TB
    nb = seq_len // tb
    dk, dv, C = RET_DK, RET_DV, RET_CHUNK

    def rb(b, n):
        return b * nb + ((nb - 1 - n) if reverse else n)

    def col(base):
        return lambda b, h, n: (rb(b, n), base // dk + h)

    in_specs = [
        pl.BlockSpec(memory_space=pltpu.SMEM),
        pl.BlockSpec((tb, dk), col(_RQ)),
        pl.BlockSpec((tb, dk), col(_RK)),
        pl.BlockSpec((tb, dv), col(_RV)),
    ]
    args = [decay_logit, proj, proj, proj]
    if reverse:
        in_specs += [
            pl.BlockSpec((tb, dv), col(_RG)),
            pl.BlockSpec((tb, dv), lambda b, h, n: (rb(b, n), h)),
            pl.BlockSpec((1, dv), lambda b, h, n: (0, h)),
        ]
        args += [proj, o_fwd, gn_w]
    out_dtype = BF16 if reverse else F32
    return pl.pallas_call(
        functools.partial(_ret_kernel, reverse=reverse),
        grid=(batch, RET_HEADS, nb),
        in_specs=in_specs,
        out_specs=pl.BlockSpec((tb, dv), lambda b, h, n: (rb(b, n), h)),
        out_shape=jax.ShapeDtypeStruct((T, RET_WIDTH), out_dtype),
        scratch_shapes=[
            pltpu.VMEM((dk, dv), F32),
            pltpu.VMEM((C, C), F32),
            pltpu.VMEM((C, dv), F32),
            pltpu.VMEM((C, dk), F32),
            pltpu.VMEM((1, dv), F32),
        ],
        compiler_params=_cparams(("parallel", "parallel", "arbitrary"), 32),
        name="ret_bwd" if reverse else "ret_fwd",
    )(*args)


GLA_TB = 512


def _gla_kernel(q_ref, k_ref, v_ref, b_ref, *rest, reverse):
    if reverse:
        g_ref, of_ref, gn_ref, o_ref, st_ref = rest
    else:
        o_ref, st_ref = rest
    n = pl.program_id(2)
    C, SUB = GLA_CHUNK, GLA_SUB
    NS = C // SUB

    @pl.when(n == 0)
    def _():
        st_ref[...] = jnp.zeros_like(st_ref)

    nchunks = q_ref.shape[0] // C
    row_c = lax.broadcasted_iota(I32, (C, GLA_DK), 0)
    lane_s = lax.broadcasted_iota(I32, (SUB, C), 1)
    row_s = lax.broadcasted_iota(I32, (SUB, C), 0)

    def chunk(ci, carry):
        c = (nchunks - 1 - ci) if reverse else ci
        c0 = pl.multiple_of(c * C, C)
        q = q_ref[pl.ds(c0, C), :].astype(F32)
        k = k_ref[pl.ds(c0, C), :].astype(F32)
        v = v_ref[pl.ds(c0, C), :]
        b = b_ref[pl.ds(c0, C), :]
        b_end = b[0:1, :] if reverse else b[C - 1:C, :]

        st = st_ref[...]
        o = _dot_nt((q * jnp.exp(b)).astype(BF16), st.astype(BF16))
        ke = (k * jnp.exp(b_end - b)).astype(BF16)
        st_ref[...] = st * jnp.exp(b_end) + _dot_tn(v, ke)

        prows = []
        for si in range(NS):
            r0 = si * SUB
            b_i = b[r0:r0 + SUB, :]
            q_i = q[r0:r0 + SUB, :]
            sd = jnp.zeros((SUB, C), F32)
            for jj in range(SUB):
                b_j = b[r0 + jj:r0 + jj + 1, :]
                k_j = k[r0 + jj:r0 + jj + 1, :]
                w = jnp.exp(jnp.minimum(b_i - b_j, 0.0))
                col = jnp.sum(q_i * k_j * w, axis=-1, keepdims=True)
                sd = jnp.where(lane_s == r0 + jj, col, sd)
            if reverse:
                causal = (row_s + r0) <= lane_s
                ref_row = b[r0 + SUB - 1:r0 + SUB, :]
                has_off = si < NS - 1
                off_rows = row_c >= r0 + SUB
            else:
                causal = (row_s + r0) >= lane_s
                ref_row = b[r0:r0 + 1, :]
                has_off = si > 0
                off_rows = row_c < r0
            s_i = jnp.where(causal, sd, 0.0)
            if has_off:
                qs = (q_i * jnp.exp(b_i - ref_row)).astype(BF16)
                kk = jnp.where(off_rows, k * jnp.exp(jnp.minimum(ref_row - b, 0.0)), 0.0).astype(BF16)
                s_i = s_i + _dot_nt(qs, kk)
            prows.append(s_i)
        p = jnp.concatenate(prows, axis=0).astype(BF16)
        o = o + _dot(p, v)
        if reverse:
            tot = of_ref[pl.ds(c0, C), :] + o
            o_ref[pl.ds(c0, C), :] = _finish_heads(tot, gn_ref[...], g_ref[pl.ds(c0, C), :])
        else:
            o_ref[pl.ds(c0, C), :] = o
        return carry

    lax.fori_loop(0, nchunks, chunk, 0)


def _gla_scan(proj, bcum, batch, seq_len, reverse, o_fwd=None, gn_w=None):
    T = proj.shape[0]
    tb = GLA_TB
    nb = seq_len // tb
    dk, dv = GLA_DK, GLA_DV

    def rb(b, n):
        return b * nb + ((nb - 1 - n) if reverse else n)

    in_specs = [
        pl.BlockSpec((tb, dk), lambda b, h, n: (rb(b, n), _GQ // dk + h)),
        pl.BlockSpec((tb, dk), lambda b, h, n: (rb(b, n), _GK // dk + h)),
        pl.BlockSpec((tb, dv), lambda b, h, n: (rb(b, n), _GV // dv + h)),
        pl.BlockSpec((tb, dk), lambda b, h, n: (rb(b, n), h)),
    ]
    args = [proj, proj, proj, bcum]
    if reverse:
        in_specs += [
            pl.BlockSpec((tb, dv), lambda b, h, n: (rb(b, n), _GG // dv + h)),
            pl.BlockSpec((tb, dv), lambda b, h, n: (rb(b, n), h)),
            pl.BlockSpec((1, dv), lambda b, h, n: (0, h)),
        ]
        args += [proj, o_fwd, gn_w]
    out_dtype = BF16 if reverse else F32
    return pl.pallas_call(
        functools.partial(_gla_kernel, reverse=reverse),
        grid=(batch, GLA_HEADS, nb),
        in_specs=in_specs,
        out_specs=pl.BlockSpec((tb, dv), lambda b, h, n: (rb(b, n), h)),
        out_shape=jax.ShapeDtypeStruct((T, GLA_WIDTH), out_dtype),
        scratch_shapes=[pltpu.VMEM((dv, dk), F32)],
        compiler_params=_cparams(("parallel", "parallel", "arbitrary"), 32),
        name="gla_bwd" if reverse else "gla_fwd",
    )(*args)


OP_TM = 256


def _out_proj_kernel(mr_ref, mg_ref, w0_ref, w1_ref, x_ref, n2_ref, rh_ref, rl_ref, h_ref, xn_ref, aff_ref):
    h = x_ref[...] + _dot(mr_ref[...], w0_ref[...]) + _dot(mg_ref[...], w1_ref[...])
    h_ref[...] = h
    ms = jnp.mean(h * h, axis=-1, keepdims=True)
    xn = h * lax.rsqrt(ms + EPS) * n2_ref[...]
    xh = xn.astype(BF16)
    xr = xh.astype(F32)
    bits = pltpu.bitcast(xr, I32)
    half = D_MODEL // 2
    xn_ref[...] = bits[:, half:] | lax.shift_right_logical(bits[:, :half], 16)
    xl = (xn - xr).astype(BF16)
    lt = _dot_nt(rh_ref[...], xh) + _dot_nt(rh_ref[...], xl) + _dot_nt(rl_ref[...], xh)
    m = jnp.max(lt, axis=0, keepdims=True)
    e = jnp.exp(lt - m)
    aff_ref[...] = e / jnp.sum(e, axis=0, keepdims=True)


def _out_proj(mix_r, mix_g, w_out, x2d, n2w, r_hi, r_lo):
    T = x2d.shape[0]
    tm = OP_TM
    half = RET_WIDTH
    return pl.pallas_call(
        _out_proj_kernel,
        grid=(T // tm,),
        in_specs=[
            pl.BlockSpec((tm, half), lambda i: (i, 0)),
            pl.BlockSpec((tm, half), lambda i: (i, 0)),
            pl.BlockSpec((half, D_MODEL), lambda i: (0, 0)),
            pl.BlockSpec((half, D_MODEL), lambda i: (1, 0)),
            pl.BlockSpec((tm, D_MODEL), lambda i: (i, 0)),
            pl.BlockSpec((1, D_MODEL), lambda i: (0, 0)),
            pl.BlockSpec((N_EXPERTS, D_MODEL), lambda i: (0, 0)),
            pl.BlockSpec((N_EXPERTS, D_MODEL), lambda i: (0, 0)),
        ],
        out_specs=[
            pl.BlockSpec((tm, D_MODEL), lambda i: (i, 0)),
            pl.BlockSpec((tm, D_MODEL // 2), lambda i: (i, 0)),
            pl.BlockSpec((N_EXPERTS, tm), lambda i: (0, i)),
        ],
        out_shape=[
            jax.ShapeDtypeStruct((T, D_MODEL), F32),
            jax.ShapeDtypeStruct((T, D_MODEL // 2), I32),
            jax.ShapeDtypeStruct((N_EXPERTS, T), F32),
        ],
        compiler_params=_cparams(("parallel",), 48),
        name="out_proj",
    )(mix_r, mix_g, w_out, w_out, x2d, n2w, r_hi, r_lo)


def _select_kernel(a_ref, pos_ref, *, cap):
    E, T = a_ref.shape
    tt = MOE_TT

    def count(pred):
        return jnp.sum(pred.astype(F32), axis=1, keepdims=True)

    def bisect(i, tau):
        cand = tau | jnp.left_shift(jnp.int32(1), 30 - i)
        bits = pltpu.bitcast(a_ref[...], I32)
        return jnp.where(count(bits >= cand) >= cap, cand, tau)

    tau = lax.fori_loop(0, 31, bisect, jnp.zeros((E, 1), I32))
    bits_all = pltpu.bitcast(a_ref[...], I32)
    quota = cap - count(bits_all > tau)

    before = (lax.broadcasted_iota(I32, (tt, tt), 0) < lax.broadcasted_iota(I32, (tt, tt), 1)).astype(BF16)

    def block(j, carry):
        c_eq, c_sel = carry
        off = pl.multiple_of(j * tt, tt)
        bits = pltpu.bitcast(a_ref[:, pl.ds(off, tt)], I32)
        eq = bits == tau
        eqf = eq.astype(F32)
        rank_eq = _dot(eqf.astype(BF16), before) + c_eq
        sel = (bits > tau) | (eq & (rank_eq < quota))
        self_ = sel.astype(F32)
        slot = _dot(self_.astype(BF16), before) + c_sel
        pos_ref[:, pl.ds(off, tt)] = jnp.where(sel, slot, -1.0).astype(I32)
        return (c_eq + jnp.sum(eqf, axis=1, keepdims=True), c_sel + jnp.sum(self_, axis=1, keepdims=True))

    zero = jnp.zeros((E, 1), F32)
    lax.fori_loop(0, T // tt, block, (zero, zero))


def _select(aff, cap):
    E, T = aff.shape
    return pl.pallas_call(
        functools.partial(_select_kernel, cap=cap),
        out_shape=jax.ShapeDtypeStruct((E, T), I32),
        compiler_params=pltpu.CompilerParams(vmem_limit_bytes=32 * 1024 * 1024),
        name="select",
    )(aff)


def _pair_lists(pos, cap):
    E, T = pos.shape
    tt, ts = MOE_TT, MOE_TS
    nb, nsb = T // tt, cap // ts
    ncand = tt // ts + 1
    n = jnp.sum((pos >= 0).reshape(E, nb, tt), axis=-1).astype(I32)
    c1 = jnp.cumsum(n, axis=1)
    c0 = c1 - n
    sb0 = jnp.minimum(c0 // ts, nsb - 1)
    sb1 = jnp.where(n > 0, (c1 - 1) // ts, sb0)
    cand = jnp.arange(ncand, dtype=I32)
    sb = sb0[..., None] + cand
    valid = (sb <= sb1[..., None]) & (n[..., None] > 0)
    sb = jnp.minimum(sb, nsb - 1)
    e_idx = jnp.broadcast_to(jnp.arange(E, dtype=I32)[:, None, None], sb.shape)
    j_idx = jnp.broadcast_to(jnp.arange(nb, dtype=I32)[None, :, None], sb.shape)

    def compact(order, valid_o, key_fn, pmax):
        ev, jv, sv = (jnp.transpose(a, order).reshape(-1) for a in (e_idx, j_idx, sb))
        vv = valid_o.reshape(-1)
        total = jnp.sum(vv.astype(I32))
        dst = jnp.where(vv, jnp.cumsum(vv.astype(I32)) - 1, pmax)
        outs = [jnp.zeros((pmax,), I32).at[dst].set(a, mode="drop") for a in (ev, jv, sv)]
        real = jnp.arange(pmax, dtype=I32) < total
        outs = [jnp.where(real, a, a[total - 1]) for a in outs]
        key = key_fn(*outs)
        first = jnp.concatenate([jnp.ones((1,), bool), key[1:] != key[:-1]])
        last = jnp.concatenate([key[1:] != key[:-1], jnp.ones((1,), bool)])
        last = last | (jnp.arange(pmax, dtype=I32) == total - 1)
        flag = jnp.where(real, first.astype(I32) + 2 * last.astype(I32) + 4, 0)
        return outs + [flag]

    d_lists = compact((0, 1, 2), valid, lambda e, j, s: e * nsb + s, E * (nb + nsb))
    forced = (e_idx == 0) & (cand == 0)
    valid_c = jnp.transpose(valid | forced, (1, 0, 2))
    c_lists = compact((1, 0, 2), valid_c, lambda e, j, s: j, E * (nb + nsb) + nb)
    return d_lists, c_lists


SC_LANES = 16
SC_CORES = 2
SC_GATHER_ROWS = 32


def _dispatch(pos, xw, cap):
    E, T = pos.shape
    W = xw.shape[1]
    G = SC_GATHER_ROWS
    part_rows = cap // SC_CORES
    mesh = plsc.VectorSubcoreMesh(core_axis_name="c", subcore_axis_name="s")

    @pl.kernel(
        out_type=jax.ShapeDtypeStruct((E * cap, W), I32),
        mesh=mesh,
        scratch_types=[pltpu.VMEM((T,), I32), pltpu.VMEM((cap,), I32), pltpu.VMEM((G, W), I32)],
        compiler_params=pltpu.CompilerParams(needs_layout_passes=False),
        name="sc_dispatch",
    )
    def run(pos_hbm, x_hbm, xe_hbm, pos_v, idx_v, buf):
        e = lax.axis_index("s")
        part = lax.axis_index("c")
        pltpu.sync_copy(pos_hbm.at[e], pos_v)
        lane = lax.iota(I32, SC_LANES)

        @pl.loop(0, T // SC_LANES)
        def _(i):
            off = pl.multiple_of(i * SC_LANES, SC_LANES)
            p = pos_v[pl.ds(off, SC_LANES)]
            plsc.store_scatter(idx_v, [p], lane + off, mask=p >= 0)

        @pl.loop(0, part_rows // G)
        def _(g):
            o = pl.multiple_of(part * part_rows + g * G, G)
            pltpu.sync_copy(x_hbm.at[idx_v.at[pl.ds(o, G)]], buf)
            pltpu.sync_copy(buf, xe_hbm.at[pl.ds(e * cap + o, G)])

    return run(pos, xw)


FFN_TM = 1024
FFN_TF = 256


def _ffn_kernel(x_ref, wg_ref, wu_ref, wd_ref, o_ref, acc_ref, xb_ref):
    f = pl.program_id(2)

    @pl.when(f == 0)
    def _():
        w = x_ref[...]
        half = D_MODEL // 2
        xb_ref[:, :half] = pltpu.bitcast(lax.shift_left(w, 16), F32).astype(BF16)
        xb_ref[:, half:] = pltpu.bitcast(w & jnp.int32(-65536), F32).astype(BF16)

    x = xb_ref[...]
    g = _dot(x, wg_ref[...].astype(BF16))
    u = _dot(x, wu_ref[...].astype(BF16))
    hid = (_silu(g) * u).astype(BF16)
    part = _dot(hid, wd_ref[...].astype(BF16))

    @pl.when(f == 0)
    def _():
        acc_ref[...] = part

    @pl.when(f > 0)
    def _():
        acc_ref[...] += part

    @pl.when(f == pl.num_programs(2) - 1)
    def _():
        o_ref[...] = acc_ref[...].astype(BF16)


def _ffn(xe, w_gate, w_up, w_down):
    E, cap, _ = xe.shape
    tm, tf = min(FFN_TM, cap), FFN_TF
    return pl.pallas_call(
        _ffn_kernel,
        grid=(E, cap // tm, D_FF // tf),
        in_specs=[
            pl.BlockSpec((None, tm, D_MODEL // 2), lambda e, m, f: (e, m, 0)),
            pl.BlockSpec((None, D_MODEL, tf), lambda e, m, f: (e, 0, f)),
            pl.BlockSpec((None, D_MODEL, tf), lambda e, m, f: (e, 0, f)),
            pl.BlockSpec((None, tf, D_MODEL), lambda e, m, f: (e, f, 0)),
        ],
        out_specs=pl.BlockSpec((None, tm, D_MODEL), lambda e, m, f: (e, m, 0)),
        out_shape=jax.ShapeDtypeStruct((E, cap, D_MODEL), BF16),
        scratch_shapes=[pltpu.VMEM((tm, D_MODEL), F32), pltpu.VMEM((tm, D_MODEL), BF16)],
        compiler_params=_cparams(("parallel", "parallel", "arbitrary"), 56),
        name="ffn",
    )(xe, w_gate, w_up, w_down)


def _combine_kernel(pe_ref, pj_ref, ps_ref, pf_ref, pos_ref, gate_ref, ye_ref, h_ref, nf_ref, o_ref, acc_ref):
    p = pl.program_id(0)
    flag = pf_ref[p]
    ts, tt = MOE_TS, MOE_TT

    @pl.when((flag & 1) != 0)
    def _():
        acc_ref[...] = h_ref[...]

    @pl.when((flag & 4) != 0)
    def _():
        slot = lax.broadcasted_iota(I32, (ts, tt), 0) + ps_ref[p] * ts
        weights = jnp.where(pos_ref[...] == slot, gate_ref[...], 0.0).astype(BF16)
        acc_ref[...] += _dot_tn(weights, ye_ref[...])

    @pl.when((flag & 2) != 0)
    def _():
        y = acc_ref[...]
        ms = jnp.mean(y * y, axis=-1, keepdims=True)
        o_ref[...] = y * lax.rsqrt(ms + EPS) * nf_ref[...]


def _combine(lists, pos3, aff3, ye, h, nfw):
    pe, pj, ps, pf = lists
    T = h.shape[0]
    ts, tt = MOE_TS, MOE_TT
    grid_spec = pltpu.PrefetchScalarGridSpec(
        num_scalar_prefetch=4,
        grid=(pe.shape[0],),
        in_specs=[
            pl.BlockSpec((None, 1, tt), lambda p, pe, pj, ps, pf: (pe[p], 0, pj[p])),
            pl.BlockSpec((None, 1, tt), lambda p, pe, pj, ps, pf: (pe[p], 0, pj[p])),
            pl.BlockSpec((None, ts, D_MODEL), lambda p, pe, pj, ps, pf: (pe[p], ps[p], 0)),
            pl.BlockSpec((tt, D_MODEL), lambda p, pe, pj, ps, pf: (pj[p], 0)),
            pl.BlockSpec((1, D_MODEL), lambda p, pe, pj, ps, pf: (0, 0)),
        ],
        out_specs=pl.BlockSpec((tt, D_MODEL), lambda p, pe, pj, ps, pf: (pj[p], 0)),
        scratch_shapes=[pltpu.VMEM((tt, D_MODEL), F32)],
    )
    return pl.pallas_call(
        _combine_kernel,
        grid_spec=grid_spec,
        out_shape=jax.ShapeDtypeStruct((T, D_MODEL), F32),
        compiler_params=_cparams(("arbitrary",), 32),
        name="combine",
    )(pe, pj, ps, pf, pos3, aff3, ye, h, nfw)


def _rope_tables(seq_len):
    d = RET_DK
    inv = ROPE_BASE ** (-jnp.arange(0, d, 2, dtype=F32) / d)
    ang = jnp.arange(seq_len, dtype=F32)[:, None] * inv[None, :]
    return jnp.cos(ang), jnp.sin(ang)


def _chunk_tri(n, chunk, upper):
    r = np.arange(n)
    same = (r[:, None] // chunk) == (r[None, :] // chunk)
    tri = (r[:, None] <= r[None, :]) if upper else (r[:, None] >= r[None, :])
    return jnp.asarray(same & tri, BF16)


def _prep_params(norm1_w, w_in, ret_gn_w, gla_gate_up, gla_gate_bias, gla_gn_w, w_out, norm2_w, router_w,
                 normf_w):
    w = w_in[0]
    w_main = w[:, :IN_MAIN].astype(BF16)
    w_ga = jnp.pad(w[:, IN_MAIN:], ((0, 0), (0, LANE - 2 * GLA_RANK))).astype(BF16)
    cs = np.ones((1, IN_MAIN), np.float32)
    cs[:, _RQ:_RQ + RET_WIDTH] = RET_DK ** -0.5
    cs[:, _GQ:_GQ + GLA_KEY_WIDTH] = GLA_DK ** -0.5
    up = gla_gate_up[0].astype(F32)
    up_pad = jnp.zeros((LANE, 2 * GLA_KEY_WIDTH), F32)
    up_pad = up_pad.at[:GLA_RANK, :GLA_KEY_WIDTH].set(up[0])
    up_pad = up_pad.at[GLA_RANK:2 * GLA_RANK, GLA_KEY_WIDTH:].set(up[1])
    rt = router_w[0].T.astype(F32)
    r_hi = rt.astype(BF16)
    r_lo = (rt - r_hi.astype(F32)).astype(BF16)
    return dict(
        n1w=norm1_w[0].reshape(1, D_MODEL).astype(F32),
        w_main=w_main, w_ga=w_ga, colscale=jnp.asarray(cs),
        up_pad=up_pad.astype(BF16),
        bias=gla_gate_bias[0].reshape(1, 2 * GLA_KEY_WIDTH).astype(F32),
        lf=_chunk_tri(GATE_TM, GLA_CHUNK, upper=False),
        lb=_chunk_tri(GATE_TM, GLA_CHUNK, upper=True),
        ret_gn=ret_gn_w[0].reshape(1, RET_WIDTH).astype(F32),
        gla_gn=gla_gn_w[0].reshape(1, GLA_WIDTH).astype(F32),
        w_out=w_out[0].astype(BF16),
        n2w=norm2_w[0].reshape(1, D_MODEL).astype(F32),
        r_hi=r_hi, r_lo=r_lo,
        nfw=normf_w.reshape(1, D_MODEL).astype(F32),
    )


def _trunk(x, pp, decay_logit, w_gate, w_up, w_down):
    B, L, _ = x.shape
    T = B * L
    x2d = x.reshape(T, D_MODEL)
    cos, sin = _rope_tables(L)
    proj, ga = _in_proj(x2d, pp["n1w"], pp["w_main"], pp["w_ga"], pp["colscale"], cos, sin, L)
    b_f, b_b = _gla_gates(ga, pp["up_pad"], pp["bias"], pp["lf"], pp["lb"])

    ret_f = _ret_scan(proj, decay_logit, B, L, reverse=False)
    mix_r = _ret_scan(proj, decay_logit, B, L, reverse=True, o_fwd=ret_f, gn_w=pp["ret_gn"])
    gla_f = _gla_scan(proj, b_f, B, L, reverse=False)
    mix_g = _gla_scan(proj, b_b, B, L, reverse=True, o_fwd=gla_f, gn_w=pp["gla_gn"])

    h, xn2, aff = _out_proj(mix_r, mix_g, pp["w_out"], x2d, pp["n2w"], pp["r_hi"], pp["r_lo"])

    cap = CAPACITY_FACTOR * T // N_EXPERTS
    pos = _select(aff, cap)
    _, c_lists = _pair_lists(pos, cap)
    pos3 = pos.reshape(N_EXPERTS, 1, T)
    xe = _dispatch(pos, xn2, cap).reshape(N_EXPERTS, cap, D_MODEL // 2)
    ye = _ffn(xe, w_gate, w_up, w_down)
    y = _combine(c_lists, pos3, aff.reshape(N_EXPERTS, 1, T), ye, h, pp["nfw"])
    return y.reshape(B, L, D_MODEL)


def kernel(x_prompt, x_sample, norm1_w, w_in, ret_decay_logit, ret_gn_w, gla_gate_up, gla_gate_bias,
           gla_gn_w, w_out, norm2_w, router_w, w_gate, w_up, w_down, normf_w):
    pp = _prep_params(norm1_w, w_in, ret_gn_w, gla_gate_up, gla_gate_bias, gla_gn_w, w_out, norm2_w,
                      router_w, normf_w)
    decay_logit = ret_decay_logit[0].astype(F32)
    args = (pp, decay_logit, w_gate[0], w_up[0], w_down[0])
    return (_trunk(x_prompt, *args), _trunk(x_sample, *args))
```

```python
import functools

import numpy as np
import jax
import jax.numpy as jnp
from jax import lax
from jax.experimental import pallas as pl
from jax.experimental.pallas import tpu as pltpu
from jax.experimental.pallas import tpu_sc as plsc

F32, BF16, I32 = jnp.float32, jnp.bfloat16, jnp.int32

D_MODEL = 2048
RET_WIDTH = 1024
RET_HEADS = 4
RET_DK = 256
RET_DV = 256
GLA_WIDTH = 1024
GLA_HEADS = 4
GLA_DK = 128
GLA_DV = 256
GLA_KEY_WIDTH = 512
GLA_RANK = 16
GLA_TAU = 16.0
RET_CHUNK = 128
GLA_CHUNK = 64
GLA_SUB = 16
ROPE_BASE = 10000.0
N_EXPERTS = 16
CAPACITY_FACTOR = 2
D_FF = 2048
EPS = 1e-6
IN_MAIN = 4 * RET_WIDTH + 2 * GLA_KEY_WIDTH + 2 * GLA_WIDTH

_RQ, _RK, _RV, _RG = 0, 1024, 2048, 3072
_GQ, _GK, _GV, _GG = 4096, 4608, 5120, 6144

LANE = 128
MOE_TT = 256
MOE_TW = 256
V7X_VMEM_BYTES = 64 * 1024 * 1024


def _cparams(sem, vmem_mb):
    return pltpu.CompilerParams(dimension_semantics=sem, vmem_limit_bytes=vmem_mb * 1024 * 1024)


def _log_sigmoid(z):
    return jnp.minimum(z, 0.0) - jnp.log1p(jnp.exp(-jnp.abs(z)))


def _silu(g):
    return g * (1.0 / (1.0 + jnp.exp(-g)))


def _dot_nt(a, b):
    return lax.dot_general(a, b, (((1,), (1,)), ((), ())), preferred_element_type=F32)


def _dot_tn(a, b):
    return lax.dot_general(a, b, (((0,), (0,)), ((), ())), preferred_element_type=F32)


def _dot(a, b):
    return jnp.dot(a, b, preferred_element_type=F32)


def _pack_bf16_pairs(x):
    bits = pltpu.bitcast(x.astype(BF16).astype(F32), I32)
    w = x.shape[1] // 2
    return bits[:, w:] | lax.shift_right_logical(bits[:, :w], 16)


def _unpack_bf16_pairs(words):
    lo = pltpu.bitcast(lax.shift_left(words, 16), F32)
    hi = pltpu.bitcast(words & jnp.int32(-65536), F32)
    return lo, hi


IP_TM = 1024
IP_TN = 1024


def _in_proj_kernel(x_ref, n1_ref, w_ref, wga_ref, cs_ref, cos_ref, sin_ref, o_ref, ga_ref, xn_ref):
    j = pl.program_id(1)

    @pl.when(j == 0)
    def _():
        x = x_ref[...]
        ms = jnp.mean(x * x, axis=-1, keepdims=True)
        xn = (x * lax.rsqrt(ms + EPS) * n1_ref[...]).astype(BF16)
        xn_ref[...] = xn
        ga_ref[...] = _dot(xn, wga_ref[...])

    acc = _dot(xn_ref[...], w_ref[...]) * cs_ref[...]
    n_rope_blocks = 2 * RET_WIDTH // IP_TN

    @pl.when(j < n_rope_blocks)
    def _():
        cos = cos_ref[...]
        sin = sin_ref[...]
        half = RET_DK // 2
        for h in range(IP_TN // RET_DK):
            lo = h * RET_DK
            x1 = acc[:, lo:lo + half]
            x2 = acc[:, lo + half:lo + RET_DK]
            o_ref[:, lo:lo + half] = (x1 * cos - x2 * sin).astype(BF16)
            o_ref[:, lo + half:lo + RET_DK] = (x1 * sin + x2 * cos).astype(BF16)

    @pl.when(j >= n_rope_blocks)
    def _():
        o_ref[...] = acc.astype(BF16)


def _in_proj(x2d, n1w, w_main, w_ga, colscale, cos, sin, seq_len):
    T = x2d.shape[0]
    tm, tn = IP_TM, IP_TN
    nlb = seq_len // tm
    return pl.pallas_call(
        _in_proj_kernel,
        grid=(T // tm, IN_MAIN // tn),
        in_specs=[
            pl.BlockSpec((tm, D_MODEL), lambda i, j: (i, 0)),
            pl.BlockSpec((1, D_MODEL), lambda i, j: (0, 0)),
            pl.BlockSpec((D_MODEL, tn), lambda i, j: (0, j)),
            pl.BlockSpec((D_MODEL, LANE), lambda i, j: (0, 0)),
            pl.BlockSpec((1, tn), lambda i, j: (0, j)),
            pl.BlockSpec((tm, LANE), lambda i, j: (i % nlb, 0)),
            pl.BlockSpec((tm, LANE), lambda i, j: (i % nlb, 0)),
        ],
        out_specs=[
            pl.BlockSpec((tm, tn), lambda i, j: (i, j)),
            pl.BlockSpec((tm, LANE), lambda i, j: (i, 0)),
        ],
        out_shape=[
            jax.ShapeDtypeStruct((T, IN_MAIN), BF16),
            jax.ShapeDtypeStruct((T, LANE), F32),
        ],
        scratch_shapes=[pltpu.VMEM((tm, D_MODEL), BF16)],
        compiler_params=_cparams(("parallel", "arbitrary"), 48),
        name="in_proj",
    )(x2d, n1w, w_main, w_ga, colscale, cos, sin)


GATE_TM = 256


def _gates_kernel(ga_ref, up_ref, bias_ref, lf_ref, lb_ref, bf_ref, bb_ref):
    z = _dot(ga_ref[...].astype(BF16), up_ref[...]) + bias_ref[...]
    la = _log_sigmoid(z) * (1.0 / GLA_TAU)
    hi = la.astype(BF16)
    lo = (la - hi.astype(F32)).astype(BF16)
    kw = GLA_KEY_WIDTH
    bf_ref[...] = _dot(lf_ref[...], hi[:, :kw]) + _dot(lf_ref[...], lo[:, :kw])
    bb_ref[...] = _dot(lb_ref[...], hi[:, kw:]) + _dot(lb_ref[...], lo[:, kw:])


def _gla_gates(ga, up_pad, bias, lf, lb):
    T = ga.shape[0]
    tm = GATE_TM
    kw = GLA_KEY_WIDTH
    return pl.pallas_call(
        _gates_kernel,
        grid=(T // tm,),
        in_specs=[
            pl.BlockSpec((tm, LANE), lambda i: (i, 0)),
            pl.BlockSpec((LANE, 2 * kw), lambda i: (0, 0)),
            pl.BlockSpec((1, 2 * kw), lambda i: (0, 0)),
            pl.BlockSpec((tm, tm), lambda i: (0, 0)),
            pl.BlockSpec((tm, tm), lambda i: (0, 0)),
        ],
        out_specs=[pl.BlockSpec((tm, kw), lambda i: (i, 0)), pl.BlockSpec((tm, kw), lambda i: (i, 0))],
        out_shape=[jax.ShapeDtypeStruct((T, kw), F32), jax.ShapeDtypeStruct((T, kw), F32)],
        compiler_params=_cparams(("parallel",), 32),
        name="gla_gates",
    )(ga, up_pad, bias, lf, lb)


def _finish_heads(tot, gn, gate):
    ms = jnp.mean(tot * tot, axis=-1, keepdims=True)
    yn = tot * lax.rsqrt(ms + EPS) * gn
    return (yn * _silu(gate.astype(F32))).astype(BF16)


RET_TB = 512


def _ret_kernel(dl_ref, q_ref, k_ref, v_ref, *rest, reverse):
    if reverse:
        g_ref, of_ref, gn_ref, o_ref, s_ref, intra_ref, qd_ref, kd_ref, cd_ref = rest
    else:
        o_ref, s_ref, intra_ref, qd_ref, kd_ref, cd_ref = rest
    h = pl.program_id(1)
    n = pl.program_id(2)
    C = RET_CHUNK

    @pl.when(n == 0)
    def _():
        s_ref[...] = jnp.zeros_like(s_ref)
        logit = dl_ref[1 if reverse else 0, h]
        lg = _log_sigmoid(jnp.full((C, RET_DV), logit, F32))
        lg_c = _log_sigmoid(jnp.full((C, C), logit, F32))
        lg_r = _log_sigmoid(jnp.full((1, RET_DV), logit, F32))
        ri = lax.broadcasted_iota(I32, (C, RET_DV), 0).astype(F32)
        rc = lax.broadcasted_iota(I32, (C, C), 0).astype(F32)
        cc = lax.broadcasted_iota(I32, (C, C), 1).astype(F32)
        diff = (cc - rc) if reverse else (rc - cc)
        intra_ref[...] = jnp.where(diff >= 0, jnp.exp(lg_c * diff), 0.0)
        if reverse:
            qd_ref[...] = jnp.exp(lg * (C - ri))
            kd_ref[...] = jnp.exp(lg * ri)
        else:
            qd_ref[...] = jnp.exp(lg * (ri + 1.0))
            kd_ref[...] = jnp.exp(lg * (C - 1.0 - ri))
        cd_ref[...] = jnp.exp(lg_r * C)

    nchunks = q_ref.shape[0] // C
    order = range(nchunks - 1, -1, -1) if reverse else range(nchunks)
    for c in order:
        rows = slice(c * C, (c + 1) * C)
        q = q_ref[rows, :]
        k = k_ref[rows, :]
        v = v_ref[rows, :]
        s = _dot_nt(q, k) * intra_ref[...]
        state = s_ref[...]
        o = _dot(s.astype(BF16), v) + _dot(q, state.astype(BF16)) * qd_ref[...]
        kd = (k.astype(F32) * kd_ref[...]).astype(BF16)
        s_ref[...] = state * cd_ref[...] + _dot_tn(kd, v)
        if reverse:
            o_ref[rows, :] = _finish_heads(of_ref[rows, :] + o, gn_ref[...], g_ref[rows, :])
        else:
            o_ref[rows, :] = o


def _ret_scan(proj, decay_logit, batch, seq_len, reverse, o_fwd=None, gn_w=None):
    T = proj.shape[0]
    tb = RET_TB
    nb = seq_len // tb
    dk, dv, C = RET_DK, RET_DV, RET_CHUNK

    def rb(b, n):
        return b * nb + ((nb - 1 - n) if reverse else n)

    def col(base):
        return lambda b, h, n: (rb(b, n), base // dk + h)

    in_specs = [
        pl.BlockSpec(memory_space=pltpu.SMEM),
        pl.BlockSpec((tb, dk), col(_RQ)),
        pl.BlockSpec((tb, dk), col(_RK)),
        pl.BlockSpec((tb, dv), col(_RV)),
    ]
    args = [decay_logit, proj, proj, proj]
    if reverse:
        in_specs += [
            pl.BlockSpec((tb, dv), col(_RG)),
            pl.BlockSpec((tb, dv), lambda b, h, n: (rb(b, n), h)),
            pl.BlockSpec((1, dv), lambda b, h, n: (0, h)),
        ]
        args += [proj, o_fwd, gn_w]
    out_dtype = BF16 if reverse else F32
    return pl.pallas_call(
        functools.partial(_ret_kernel, reverse=reverse),
        grid=(batch, RET_HEADS, nb),
        in_specs=in_specs,
        out_specs=pl.BlockSpec((tb, dv), lambda b, h, n: (rb(b, n), h)),
        out_shape=jax.ShapeDtypeStruct((T, RET_WIDTH), out_dtype),
        scratch_shapes=[
            pltpu.VMEM((dk, dv), F32),
            pltpu.VMEM((C, C), F32),
            pltpu.VMEM((C, dv), F32),
            pltpu.VMEM((C, dk), F32),
            pltpu.VMEM((1, dv), F32),
        ],
        compiler_params=_cparams(("parallel", "parallel", "arbitrary"), 32),
        name="ret_bwd" if reverse else "ret_fwd",
    )(*args)


GLA_TB = 512


def _gla_kernel(q_ref, k_ref, v_ref, b_ref, *rest, reverse):
    if reverse:
        g_ref, of_ref, gn_ref, o_ref, st_ref = rest
    else:
        o_ref, st_ref = rest
    n = pl.program_id(2)
    C, SUB = GLA_CHUNK, GLA_SUB
    NS = C // SUB

    @pl.when(n == 0)
    def _():
        st_ref[...] = jnp.zeros_like(st_ref)

    nchunks = q_ref.shape[0] // C
    row_c = lax.broadcasted_iota(I32, (C, GLA_DK), 0)
    lane_s = lax.broadcasted_iota(I32, (SUB, C), 1)
    row_s = lax.broadcasted_iota(I32, (SUB, C), 0)

    def chunk(ci, carry):
        c = (nchunks - 1 - ci) if reverse else ci
        c0 = pl.multiple_of(c * C, C)
        q = q_ref[pl.ds(c0, C), :].astype(F32)
        k = k_ref[pl.ds(c0, C), :].astype(F32)
        v = v_ref[pl.ds(c0, C), :]
        b = b_ref[pl.ds(c0, C), :]
        b_end = b[0:1, :] if reverse else b[C - 1:C, :]

        st = st_ref[...]
        o = _dot_nt((q * jnp.exp(b)).astype(BF16), st.astype(BF16))
        ke = (k * jnp.exp(b_end - b)).astype(BF16)
        st_ref[...] = st * jnp.exp(b_end) + _dot_tn(v, ke)

        prows = []
        for si in range(NS):
            r0 = si * SUB
            b_i = b[r0:r0 + SUB, :]
            q_i = q[r0:r0 + SUB, :]
            sd = jnp.zeros((SUB, C), F32)
            for jj in range(SUB):
                b_j = b[r0 + jj:r0 + jj + 1, :]
                k_j = k[r0 + jj:r0 + jj + 1, :]
                w = jnp.exp(jnp.minimum(b_i - b_j, 0.0))
                col = jnp.sum(q_i * k_j * w, axis=-1, keepdims=True)
                sd = jnp.where(lane_s == r0 + jj, col, sd)
            if reverse:
                causal = (row_s + r0) <= lane_s
                ref_row = b[r0 + SUB - 1:r0 + SUB, :]
                has_off = si < NS - 1
                off_rows = row_c >= r0 + SUB
            else:
                causal = (row_s + r0) >= lane_s
                ref_row = b[r0:r0 + 1, :]
                has_off = si > 0
                off_rows = row_c < r0
            s_i = jnp.where(causal, sd, 0.0)
            if has_off:
                qs = (q_i * jnp.exp(b_i - ref_row)).astype(BF16)
                kk = jnp.where(off_rows, k * jnp.exp(jnp.minimum(ref_row - b, 0.0)), 0.0).astype(BF16)
                s_i = s_i + _dot_nt(qs, kk)
            prows.append(s_i)
        p = jnp.concatenate(prows, axis=0).astype(BF16)
        o = o + _dot(p, v)
        if reverse:
            tot = of_ref[pl.ds(c0, C), :] + o
            o_ref[pl.ds(c0, C), :] = _finish_heads(tot, gn_ref[...], g_ref[pl.ds(c0, C), :])
        else:
            o_ref[pl.ds(c0, C), :] = o
        return carry

    lax.fori_loop(0, nchunks, chunk, 0)


def _gla_scan(proj, bcum, batch, seq_len, reverse, o_fwd=None, gn_w=None):
    T = proj.shape[0]
    tb = GLA_TB
    nb = seq_len // tb
    dk, dv = GLA_DK, GLA_DV

    def rb(b, n):
        return b * nb + ((nb - 1 - n) if reverse else n)

    in_specs = [
        pl.BlockSpec((tb, dk), lambda b, h, n: (rb(b, n), _GQ // dk + h)),
        pl.BlockSpec((tb, dk), lambda b, h, n: (rb(b, n), _GK // dk + h)),
        pl.BlockSpec((tb, dv), lambda b, h, n: (rb(b, n), _GV // dv + h)),
        pl.BlockSpec((tb, dk), lambda b, h, n: (rb(b, n), h)),
    ]
    args = [proj, proj, proj, bcum]
    if reverse:
        in_specs += [
            pl.BlockSpec((tb, dv), lambda b, h, n: (rb(b, n), _GG // dv + h)),
            pl.BlockSpec((tb, dv), lambda b, h, n: (rb(b, n), h)),
            pl.BlockSpec((1, dv), lambda b, h, n: (0, h)),
        ]
        args += [proj, o_fwd, gn_w]
    out_dtype = BF16 if reverse else F32
    return pl.pallas_call(
        functools.partial(_gla_kernel, reverse=reverse),
        grid=(batch, GLA_HEADS, nb),
        in_specs=in_specs,
        out_specs=pl.BlockSpec((tb, dv), lambda b, h, n: (rb(b, n), h)),
        out_shape=jax.ShapeDtypeStruct((T, GLA_WIDTH), out_dtype),
        scratch_shapes=[pltpu.VMEM((dv, dk), F32)],
        compiler_params=_cparams(("parallel", "parallel", "arbitrary"), 32),
        name="gla_bwd" if reverse else "gla_fwd",
    )(*args)


OP_TM = 256


def _out_proj_kernel(mr_ref, mg_ref, w0_ref, w1_ref, x_ref, n2_ref, rh_ref, rl_ref, h_ref, xn_ref, aff_ref):
    h = x_ref[...] + _dot(mr_ref[...], w0_ref[...]) + _dot(mg_ref[...], w1_ref[...])
    h_ref[...] = h
    ms = jnp.mean(h * h, axis=-1, keepdims=True)
    xn = h * lax.rsqrt(ms + EPS) * n2_ref[...]
    xh = xn.astype(BF16)
    xn_ref[...] = _pack_bf16_pairs(xn)
    xl = (xn - xh.astype(F32)).astype(BF16)
    lt = _dot_nt(rh_ref[...], xh) + _dot_nt(rh_ref[...], xl) + _dot_nt(rl_ref[...], xh)
    m = jnp.max(lt, axis=0, keepdims=True)
    e = jnp.exp(lt - m)
    aff_ref[...] = e / jnp.sum(e, axis=0, keepdims=True)


def _out_proj(mix_r, mix_g, w_out, x2d, n2w, r_hi, r_lo):
    T = x2d.shape[0]
    tm = OP_TM
    half = RET_WIDTH
    return pl.pallas_call(
        _out_proj_kernel,
        grid=(T // tm,),
        in_specs=[
            pl.BlockSpec((tm, half), lambda i: (i, 0)),
            pl.BlockSpec((tm, half), lambda i: (i, 0)),
            pl.BlockSpec((half, D_MODEL), lambda i: (0, 0)),
            pl.BlockSpec((half, D_MODEL), lambda i: (1, 0)),
            pl.BlockSpec((tm, D_MODEL), lambda i: (i, 0)),
            pl.BlockSpec((1, D_MODEL), lambda i: (0, 0)),
            pl.BlockSpec((N_EXPERTS, D_MODEL), lambda i: (0, 0)),
            pl.BlockSpec((N_EXPERTS, D_MODEL), lambda i: (0, 0)),
        ],
        out_specs=[
            pl.BlockSpec((tm, D_MODEL), lambda i: (i, 0)),
            pl.BlockSpec((tm, D_MODEL // 2), lambda i: (i, 0)),
            pl.BlockSpec((N_EXPERTS, tm), lambda i: (0, i)),
        ],
        out_shape=[
            jax.ShapeDtypeStruct((T, D_MODEL), F32),
            jax.ShapeDtypeStruct((T, D_MODEL // 2), I32),
            jax.ShapeDtypeStruct((N_EXPERTS, T), F32),
        ],
        compiler_params=_cparams(("parallel",), 48),
        name="out_proj",
    )(mix_r, mix_g, w_out, w_out, x2d, n2w, r_hi, r_lo)


def _select_kernel(a_ref, pos_ref, posb_ref, rankb_ref, affb_ref, *, cap):
    E, T = a_ref.shape
    tt = MOE_TT

    def count(pred):
        return jnp.sum(pred.astype(F32), axis=1, keepdims=True)

    def bisect(i, tau):
        cand = tau | jnp.left_shift(jnp.int32(1), 30 - i)
        bits = pltpu.bitcast(a_ref[...], I32)
        return jnp.where(count(bits >= cand) >= cap, cand, tau)

    tau = lax.fori_loop(0, 31, bisect, jnp.zeros((E, 1), I32))
    bits_all = pltpu.bitcast(a_ref[...], I32)
    quota = cap - count(bits_all > tau)

    before = (lax.broadcasted_iota(I32, (tt, tt), 0) < lax.broadcasted_iota(I32, (tt, tt), 1)).astype(BF16)
    below = (lax.broadcasted_iota(I32, (E, E), 1) < lax.broadcasted_iota(I32, (E, E), 0)).astype(BF16)

    def block(j, carry):
        c_eq, c_sel = carry
        off = pl.multiple_of(j * tt, tt)
        aff = a_ref[:, pl.ds(off, tt)]
        bits = pltpu.bitcast(aff, I32)
        eq = bits == tau
        eqf = eq.astype(F32)
        rank_eq = _dot(eqf.astype(BF16), before) + c_eq
        sel = (bits > tau) | (eq & (rank_eq < quota))
        self_ = sel.astype(F32)
        selb = self_.astype(BF16)
        slot = _dot(selb, before) + c_sel
        pos = jnp.where(sel, slot, -1.0).astype(I32)
        pos_ref[:, pl.ds(off, tt)] = pos
        per_tok = jnp.broadcast_to(jnp.sum(self_, axis=0, keepdims=True), (E, tt))
        rank = _dot(per_tok.astype(BF16), before) + _dot(below, selb)
        posb_ref[j] = pos
        rankb_ref[j] = jnp.where(sel, rank, -1.0).astype(I32)
        affb_ref[j] = aff
        return (c_eq + jnp.sum(eqf, axis=1, keepdims=True), c_sel + jnp.sum(self_, axis=1, keepdims=True))

    zero = jnp.zeros((E, 1), F32)
    lax.fori_loop(0, T // tt, block, (zero, zero))


def _select(aff, cap):
    E, T = aff.shape
    nb = T // MOE_TT
    blk = jax.ShapeDtypeStruct((nb, E, MOE_TT), I32)
    return pl.pallas_call(
        functools.partial(_select_kernel, cap=cap),
        out_shape=[jax.ShapeDtypeStruct((E, T), I32), blk, blk, jax.ShapeDtypeStruct((nb, E, MOE_TT), F32)],
        compiler_params=pltpu.CompilerParams(vmem_limit_bytes=40 * 1024 * 1024),
        name="select",
    )(aff)


def _regroup_rows(T):
    nb = T // MOE_TT
    rows = CAPACITY_FACTOR * T + SC_GATHER_ROWS * nb
    return -(-rows // MOE_TW) * MOE_TW


def _combine_schedule(posb, T):
    nb = posb.shape[0]
    tw, g = MOE_TW, SC_GATHER_ROWS
    n = jnp.sum((posb >= 0).reshape(nb, -1), axis=1).astype(I32)
    seg = (n + g - 1) // g * g
    hi = jnp.cumsum(seg)
    lo = hi - seg
    off = jnp.concatenate([jnp.zeros((1,), I32), hi])
    nwin_max = (N_EXPERTS * MOE_TT) // tw + 1
    w0 = lo // tw
    w1 = jnp.where(seg > 0, (hi - 1) // tw, w0)
    cand = jnp.arange(nwin_max, dtype=I32)
    win = w0[:, None] + cand[None, :]
    valid = (win <= w1[:, None]).reshape(-1)
    nwin_total = _regroup_rows(T) // tw
    pmax = nb + nwin_total
    jv = jnp.broadcast_to(jnp.arange(nb, dtype=I32)[:, None], win.shape).reshape(-1)
    wv = jnp.minimum(win, nwin_total - 1).reshape(-1)
    total = jnp.sum(valid.astype(I32))
    dst = jnp.where(valid, jnp.cumsum(valid.astype(I32)) - 1, pmax)
    pj, pw = (jnp.zeros((pmax,), I32).at[dst].set(a, mode="drop") for a in (jv, wv))
    real = jnp.arange(pmax, dtype=I32) < total
    pj, pw = (jnp.where(real, a, a[total - 1]) for a in (pj, pw))
    first = jnp.concatenate([jnp.ones((1,), bool), pj[1:] != pj[:-1]])
    last = jnp.concatenate([pj[1:] != pj[:-1], jnp.ones((1,), bool)]) | (jnp.arange(pmax, dtype=I32) == total - 1)
    flag = jnp.where(real, first.astype(I32) + 2 * last.astype(I32) + 4, 0)
    return off, (pj, pw, flag, lo[pj], hi[pj])


SC_LANES = 16
SC_CORES = 2
SC_SUBCORES = 16
SC_GATHER_ROWS = 32


def _dispatch(pos, xw, cap):
    E, T = pos.shape
    W = xw.shape[1]
    G = SC_GATHER_ROWS
    part_rows = cap // SC_CORES
    mesh = plsc.VectorSubcoreMesh(core_axis_name="c", subcore_axis_name="s")

    @pl.kernel(
        out_type=jax.ShapeDtypeStruct((E * cap, W), I32),
        mesh=mesh,
        scratch_types=[pltpu.VMEM((T,), I32), pltpu.VMEM((cap,), I32), pltpu.VMEM((G, W), I32)],
        compiler_params=pltpu.CompilerParams(needs_layout_passes=False),
        name="sc_dispatch",
    )
    def run(pos_hbm, x_hbm, xe_hbm, pos_v, idx_v, buf):
        e = lax.axis_index("s")
        part = lax.axis_index("c")
        pltpu.sync_copy(pos_hbm.at[e], pos_v)
        lane = lax.iota(I32, SC_LANES)

        @pl.loop(0, T // SC_LANES)
        def _(i):
            off = pl.multiple_of(i * SC_LANES, SC_LANES)
            p = pos_v[pl.ds(off, SC_LANES)]
            plsc.store_scatter(idx_v, [p], lane + off, mask=p >= 0)

        @pl.loop(0, part_rows // G)
        def _(g):
            o = pl.multiple_of(part * part_rows + g * G, G)
            pltpu.sync_copy(x_hbm.at[idx_v.at[pl.ds(o, G)]], buf)
            pltpu.sync_copy(buf, xe_hbm.at[pl.ds(e * cap + o, G)])

    return run(pos, xw)


FFN_TM = 1024
FFN_TF = 256


def _ffn_kernel(x_ref, wg_ref, wu_ref, wd_ref, o_ref, acc_ref, xb_ref):
    f = pl.program_id(2)

    @pl.when(f == 0)
    def _():
        lo, hi = _unpack_bf16_pairs(x_ref[...])
        half = D_MODEL // 2
        xb_ref[:, :half] = lo.astype(BF16)
        xb_ref[:, half:] = hi.astype(BF16)

    x = xb_ref[...]
    g = _dot(x, wg_ref[...].astype(BF16))
    u = _dot(x, wu_ref[...].astype(BF16))
    hid = (_silu(g) * u).astype(BF16)
    part = _dot(hid, wd_ref[...].astype(BF16))

    @pl.when(f == 0)
    def _():
        acc_ref[...] = part

    @pl.when(f > 0)
    def _():
        acc_ref[...] += part

    @pl.when(f == pl.num_programs(2) - 1)
    def _():
        o_ref[...] = _pack_bf16_pairs(acc_ref[...])


def _ffn(xe, w_gate, w_up, w_down):
    E, cap, _ = xe.shape
    tm, tf = min(FFN_TM, cap), FFN_TF
    return pl.pallas_call(
        _ffn_kernel,
        grid=(E, cap // tm, D_FF // tf),
        in_specs=[
            pl.BlockSpec((None, tm, D_MODEL // 2), lambda e, m, f: (e, m, 0)),
            pl.BlockSpec((None, D_MODEL, tf), lambda e, m, f: (e, 0, f)),
            pl.BlockSpec((None, D_MODEL, tf), lambda e, m, f: (e, 0, f)),
            pl.BlockSpec((None, tf, D_MODEL), lambda e, m, f: (e, f, 0)),
        ],
        out_specs=pl.BlockSpec((None, tm, D_MODEL // 2), lambda e, m, f: (e, m, 0)),
        out_shape=jax.ShapeDtypeStruct((E, cap, D_MODEL // 2), I32),
        scratch_shapes=[pltpu.VMEM((tm, D_MODEL), F32), pltpu.VMEM((tm, D_MODEL), BF16)],
        compiler_params=_cparams(("parallel", "parallel", "arbitrary"), 56),
        name="ffn",
    )(xe, w_gate, w_up, w_down)


def _regroup(posb, rankb, affb, off, yw, cap, rows):
    NB, EB = posb.shape
    W = yw.shape[1]
    G, L, tt = SC_GATHER_ROWS, SC_LANES, MOE_TT
    n_workers = SC_CORES * SC_SUBCORES
    per = NB // n_workers
    mesh = plsc.VectorSubcoreMesh(core_axis_name="c", subcore_axis_name="s")
    off_pad = jnp.pad(off, (0, L))

    @pl.kernel(
        out_type=(jax.ShapeDtypeStruct((rows, W), I32), jax.ShapeDtypeStruct((rows,), I32),
                  jax.ShapeDtypeStruct((rows,), F32)),
        mesh=mesh,
        scratch_types=[pltpu.VMEM((EB,), I32), pltpu.VMEM((EB,), I32), pltpu.VMEM((EB,), F32),
                       pltpu.VMEM((EB,), I32), pltpu.VMEM((EB,), I32), pltpu.VMEM((EB,), F32),
                       pltpu.VMEM((G, W), I32), pltpu.VMEM((NB + 1 + L,), I32)],
        compiler_params=pltpu.CompilerParams(needs_layout_passes=False),
        name="sc_regroup",
    )
    def run(posb_hbm, rankb_hbm, affb_hbm, off_hbm, y_hbm, yg_hbm, tok_hbm, gate_hbm,
            pos_v, rank_v, aff_v, src_v, tok_v, gate_v, buf, off_v):
        wid = lax.axis_index("c") * SC_SUBCORES + lax.axis_index("s")
        pltpu.sync_copy(off_hbm, off_v)
        lane = lax.iota(I32, L)
        zi = jnp.zeros((L,), I32)
        zf = jnp.zeros((L,), F32)

        @pl.loop(0, per)
        def _(k):
            j = wid * per + k
            pltpu.sync_copy(posb_hbm.at[j], pos_v)
            pltpu.sync_copy(rankb_hbm.at[j], rank_v)
            pltpu.sync_copy(affb_hbm.at[j], aff_v)
            lo = jnp.max(plsc.load_gather(off_v, [zi + j]))
            hi = jnp.max(plsc.load_gather(off_v, [zi + j + 1]))

            @pl.loop(0, EB // L)
            def _(i):
                o = pl.multiple_of(i * L, L)
                src_v[pl.ds(o, L)] = zi
                tok_v[pl.ds(o, L)] = zi
                gate_v[pl.ds(o, L)] = zf

            @pl.loop(0, EB // L)
            def _(i):
                o = pl.multiple_of(i * L, L)
                p = pos_v[pl.ds(o, L)]
                r = rank_v[pl.ds(o, L)]
                m = p >= 0
                e = i // (tt // L)
                t0 = j * tt + (i % (tt // L)) * L
                plsc.store_scatter(src_v, [r], p + e * cap, mask=m)
                plsc.store_scatter(tok_v, [r], lane + t0, mask=m)
                plsc.store_scatter(gate_v, [r], aff_v[pl.ds(o, L)], mask=m)

            @pl.loop(0, (hi - lo) // G)
            def _(g):
                o = pl.multiple_of(g * G, G)
                dst = pl.multiple_of(lo + o, G)
                pltpu.sync_copy(y_hbm.at[src_v.at[pl.ds(o, G)]], buf)
                pltpu.sync_copy(buf, yg_hbm.at[pl.ds(dst, G)])
                pltpu.sync_copy(tok_v.at[pl.ds(o, G)], tok_hbm.at[pl.ds(dst, G)])
                pltpu.sync_copy(gate_v.at[pl.ds(o, G)], gate_hbm.at[pl.ds(dst, G)])

    return run(posb, rankb, affb, off_pad, yw)


def _combine_kernel(pj_ref, pw_ref, pf_ref, plo_ref, phi_ref, tok_ref, gate_ref, yg_ref, h_ref, nf_ref,
                    o_ref, acc_ref):
    p = pl.program_id(0)
    flag = pf_ref[p]
    tw, tt = MOE_TW, MOE_TT
    half = D_MODEL // 2

    @pl.when((flag & 1) != 0)
    def _():
        acc_ref[...] = h_ref[...]

    @pl.when((flag & 4) != 0)
    def _():
        lo, hi = plo_ref[p], phi_ref[p]
        row0 = pw_ref[p] * tw
        rid = lax.broadcasted_iota(I32, (tw, 1), 0) + row0
        keep = (rid >= lo) & (rid < hi)
        y_lo, y_hi = _unpack_bf16_pairs(yg_ref[...])
        y_lo = jnp.where(keep, y_lo, 0.0).astype(BF16)
        y_hi = jnp.where(keep, y_hi, 0.0).astype(BF16)
        tid = lax.broadcasted_iota(I32, (tt, tw), 0) + pj_ref[p] * tt
        cid = lax.broadcasted_iota(I32, (tt, tw), 1) + row0
        hit = (tok_ref[...] == tid) & (cid >= lo) & (cid < hi)
        weights = jnp.where(hit, gate_ref[...], 0.0).astype(BF16)
        acc_ref[:, :half] += _dot(weights, y_lo)
        acc_ref[:, half:] += _dot(weights, y_hi)

    @pl.when((flag & 2) != 0)
    def _():
        y = acc_ref[...]
        ms = jnp.mean(y * y, axis=-1, keepdims=True)
        o_ref[...] = y * lax.rsqrt(ms + EPS) * nf_ref[...]


def _combine(lists, tok, gate, yg, h, nfw):
    pj, pw, pf, plo, phi = lists
    T = h.shape[0]
    tw, tt = MOE_TW, MOE_TT
    nwin = yg.shape[0] // tw
    grid_spec = pltpu.PrefetchScalarGridSpec(
        num_scalar_prefetch=5,
        grid=(pj.shape[0],),
        in_specs=[
            pl.BlockSpec((None, 1, tw), lambda p, pj, pw, *_: (pw[p], 0, 0)),
            pl.BlockSpec((None, 1, tw), lambda p, pj, pw, *_: (pw[p], 0, 0)),
            pl.BlockSpec((tw, D_MODEL // 2), lambda p, pj, pw, *_: (pw[p], 0)),
            pl.BlockSpec((tt, D_MODEL), lambda p, pj, pw, *_: (pj[p], 0)),
            pl.BlockSpec((1, D_MODEL), lambda p, pj, pw, *_: (0, 0)),
        ],
        out_specs=pl.BlockSpec((tt, D_MODEL), lambda p, pj, pw, *_: (pj[p], 0)),
        scratch_shapes=[pltpu.VMEM((tt, D_MODEL), F32)],
    )
    return pl.pallas_call(
        _combine_kernel,
        grid_spec=grid_spec,
        out_shape=jax.ShapeDtypeStruct((T, D_MODEL), F32),
        compiler_params=_cparams(("arbitrary",), 32),
        name="combine",
    )(pj, pw, pf, plo, phi, tok.reshape(nwin, 1, tw), gate.reshape(nwin, 1, tw), yg, h, nfw)


def _rope_tables(seq_len):
    d = RET_DK
    inv = ROPE_BASE ** (-jnp.arange(0, d, 2, dtype=F32) / d)
    ang = jnp.arange(seq_len, dtype=F32)[:, None] * inv[None, :]
    return jnp.cos(ang), jnp.sin(ang)


def _chunk_tri(n, chunk, upper):
    r = np.arange(n)
    same = (r[:, None] // chunk) == (r[None, :] // chunk)
    tri = (r[:, None] <= r[None, :]) if upper else (r[:, None] >= r[None, :])
    return jnp.asarray(same & tri, BF16)


def _prep_params(norm1_w, w_in, ret_gn_w, gla_gate_up, gla_gate_bias, gla_gn_w, w_out, norm2_w, router_w,
                 normf_w):
    w = w_in[0]
    w_main = w[:, :IN_MAIN].astype(BF16)
    w_ga = jnp.pad(w[:, IN_MAIN:], ((0, 0), (0, LANE - 2 * GLA_RANK))).astype(BF16)
    cs = np.ones((1, IN_MAIN), np.float32)
    cs[:, _RQ:_RQ + RET_WIDTH] = RET_DK ** -0.5
    cs[:, _GQ:_GQ + GLA_KEY_WIDTH] = GLA_DK ** -0.5
    up = gla_gate_up[0].astype(F32)
    up_pad = jnp.zeros((LANE, 2 * GLA_KEY_WIDTH), F32)
    up_pad = up_pad.at[:GLA_RANK, :GLA_KEY_WIDTH].set(up[0])
    up_pad = up_pad.at[GLA_RANK:2 * GLA_RANK, GLA_KEY_WIDTH:].set(up[1])
    rt = router_w[0].T.astype(F32)
    r_hi = rt.astype(BF16)
    r_lo = (rt - r_hi.astype(F32)).astype(BF16)
    return dict(
        n1w=norm1_w[0].reshape(1, D_MODEL).astype(F32),
        w_main=w_main, w_ga=w_ga, colscale=jnp.asarray(cs),
        up_pad=up_pad.astype(BF16),
        bias=gla_gate_bias[0].reshape(1, 2 * GLA_KEY_WIDTH).astype(F32),
        lf=_chunk_tri(GATE_TM, GLA_CHUNK, upper=False),
        lb=_chunk_tri(GATE_TM, GLA_CHUNK, upper=True),
        ret_gn=ret_gn_w[0].reshape(1, RET_WIDTH).astype(F32),
        gla_gn=gla_gn_w[0].reshape(1, GLA_WIDTH).astype(F32),
        w_out=w_out[0].astype(BF16),
        n2w=norm2_w[0].reshape(1, D_MODEL).astype(F32),
        r_hi=r_hi, r_lo=r_lo,
        nfw=normf_w.reshape(1, D_MODEL).astype(F32),
    )


def _trunk(x, pp, decay_logit, w_gate, w_up, w_down):
    B, L, _ = x.shape
    T = B * L
    x2d = x.reshape(T, D_MODEL)
    cos, sin = _rope_tables(L)
    proj, ga = _in_proj(x2d, pp["n1w"], pp["w_main"], pp["w_ga"], pp["colscale"], cos, sin, L)
    b_f, b_b = _gla_gates(ga, pp["up_pad"], pp["bias"], pp["lf"], pp["lb"])

    ret_f = _ret_scan(proj, decay_logit, B, L, reverse=False)
    mix_r = _ret_scan(proj, decay_logit, B, L, reverse=True, o_fwd=ret_f, gn_w=pp["ret_gn"])
    gla_f = _gla_scan(proj, b_f, B, L, reverse=False)
    mix_g = _gla_scan(proj, b_b, B, L, reverse=True, o_fwd=gla_f, gn_w=pp["gla_gn"])

    h, xn2, aff = _out_proj(mix_r, mix_g, pp["w_out"], x2d, pp["n2w"], pp["r_hi"], pp["r_lo"])

    cap = CAPACITY_FACTOR * T // N_EXPERTS
    pos, posb, rankb, affb = _select(aff, cap)
    off, c_lists = _combine_schedule(posb, T)
    xe = _dispatch(pos, xn2, cap).reshape(N_EXPERTS, cap, D_MODEL // 2)
    ye = _ffn(xe, w_gate, w_up, w_down).reshape(N_EXPERTS * cap, D_MODEL // 2)
    nb = T // MOE_TT
    yg, tok, gate = _regroup(posb.reshape(nb, -1), rankb.reshape(nb, -1), affb.reshape(nb, -1), off, ye, cap,
                             _regroup_rows(T))
    y = _combine(c_lists, tok, gate, yg, h, pp["nfw"])
    return y.reshape(B, L, D_MODEL)


def kernel(x_prompt, x_sample, norm1_w, w_in, ret_decay_logit, ret_gn_w, gla_gate_up, gla_gate_bias,
           gla_gn_w, w_out, norm2_w, router_w, w_gate, w_up, w_down, normf_w):
    pp = _prep_params(norm1_w, w_in, ret_gn_w, gla_gate_up, gla_gate_bias, gla_gn_w, w_out, norm2_w,
                      router_w, normf_w)
    decay_logit = ret_decay_logit[0].astype(F32)
    args = (pp, decay_logit, w_gate[0], w_up[0], w_down[0])
    return (_trunk(x_prompt, *args), _trunk(x_sample, *args))
```

```python
import functools

import numpy as np
import jax
import jax.numpy as jnp
from jax import lax
from jax.experimental import pallas as pl
from jax.experimental.pallas import tpu as pltpu
from jax.experimental.pallas import tpu_sc as plsc

F32, BF16, I32 = jnp.float32, jnp.bfloat16, jnp.int32

D_MODEL = 2048
RET_WIDTH = 1024
RET_HEADS = 4
RET_DK = 256
RET_DV = 256
GLA_WIDTH = 1024
GLA_HEADS = 4
GLA_DK = 128
GLA_DV = 256
GLA_KEY_WIDTH = 512
GLA_RANK = 16
GLA_TAU = 16.0
RET_CHUNK = 128
GLA_CHUNK = 64
GLA_SUB = 16
ROPE_BASE = 10000.0
N_EXPERTS = 16
CAPACITY_FACTOR = 2
D_FF = 2048
EPS = 1e-6
LOG2_E = 1.4426950408889634
IN_MAIN = 4 * RET_WIDTH + 2 * GLA_KEY_WIDTH + 2 * GLA_WIDTH

_RQ, _RK, _RV, _RG = 0, 1024, 2048, 3072
_GQ, _GK, _GV, _GG = 4096, 4608, 5120, 6144

LANE = 128
MOE_TT = 256
MOE_TW = 256
V7X_VMEM_BYTES = 64 * 1024 * 1024


def _cparams(sem, vmem_mb):
    return pltpu.CompilerParams(dimension_semantics=sem, vmem_limit_bytes=vmem_mb * 1024 * 1024)


def _log_sigmoid(z):
    return jnp.minimum(z, 0.0) - jnp.log1p(jnp.exp(-jnp.abs(z)))


def _silu(g):
    return g * (1.0 / (1.0 + jnp.exp(-g)))


def _dot_nt(a, b):
    return lax.dot_general(a, b, (((1,), (1,)), ((), ())), preferred_element_type=F32)


def _dot_tn(a, b):
    return lax.dot_general(a, b, (((0,), (0,)), ((), ())), preferred_element_type=F32)


def _dot(a, b):
    return jnp.dot(a, b, preferred_element_type=F32)


def _pack_bf16_pairs(x):
    bits = pltpu.bitcast(x.astype(BF16).astype(F32), I32)
    w = x.shape[1] // 2
    return bits[:, w:] | lax.shift_right_logical(bits[:, :w], 16)


def _unpack_bf16_pairs(words):
    lo = pltpu.bitcast(lax.shift_left(words, 16), F32)
    hi = pltpu.bitcast(words & jnp.int32(-65536), F32)
    return lo, hi


IP_TM = 1024
IP_TN = 1024


def _in_proj_kernel(x_ref, n1_ref, w_ref, wga_ref, cs_ref, cos_ref, sin_ref, o_ref, ga_ref, xn_ref):
    j = pl.program_id(1)

    @pl.when(j == 0)
    def _():
        x = x_ref[...]
        ms = jnp.mean(x * x, axis=-1, keepdims=True)
        xn = (x * lax.rsqrt(ms + EPS) * n1_ref[...]).astype(BF16)
        xn_ref[...] = xn
        ga_ref[...] = _dot(xn, wga_ref[...])

    acc = _dot(xn_ref[...], w_ref[...]) * cs_ref[...]
    n_rope_blocks = 2 * RET_WIDTH // IP_TN

    @pl.when(j < n_rope_blocks)
    def _():
        cos = cos_ref[...]
        sin = sin_ref[...]
        half = RET_DK // 2
        for h in range(IP_TN // RET_DK):
            lo = h * RET_DK
            x1 = acc[:, lo:lo + half]
            x2 = acc[:, lo + half:lo + RET_DK]
            o_ref[:, lo:lo + half] = (x1 * cos - x2 * sin).astype(BF16)
            o_ref[:, lo + half:lo + RET_DK] = (x1 * sin + x2 * cos).astype(BF16)

    @pl.when(j >= n_rope_blocks)
    def _():
        o_ref[...] = acc.astype(BF16)


def _in_proj(x2d, n1w, w_main, w_ga, colscale, cos, sin, seq_len):
    T = x2d.shape[0]
    tm, tn = IP_TM, IP_TN
    nlb = seq_len // tm
    return pl.pallas_call(
        _in_proj_kernel,
        grid=(T // tm, IN_MAIN // tn),
        in_specs=[
            pl.BlockSpec((tm, D_MODEL), lambda i, j: (i, 0)),
            pl.BlockSpec((1, D_MODEL), lambda i, j: (0, 0)),
            pl.BlockSpec((D_MODEL, tn), lambda i, j: (0, j)),
            pl.BlockSpec((D_MODEL, LANE), lambda i, j: (0, 0)),
            pl.BlockSpec((1, tn), lambda i, j: (0, j)),
            pl.BlockSpec((tm, LANE), lambda i, j: (i % nlb, 0)),
            pl.BlockSpec((tm, LANE), lambda i, j: (i % nlb, 0)),
        ],
        out_specs=[
            pl.BlockSpec((tm, tn), lambda i, j: (i, j)),
            pl.BlockSpec((tm, LANE), lambda i, j: (i, 0)),
        ],
        out_shape=[
            jax.ShapeDtypeStruct((T, IN_MAIN), BF16),
            jax.ShapeDtypeStruct((T, LANE), F32),
        ],
        scratch_shapes=[pltpu.VMEM((tm, D_MODEL), BF16)],
        compiler_params=_cparams(("parallel", "arbitrary"), 48),
        name="in_proj",
    )(x2d, n1w, w_main, w_ga, colscale, cos, sin)


GATE_TM = 256


def _gates_kernel(ga_ref, up_ref, bias_ref, lf_ref, lb_ref, bf_ref, bb_ref):
    z = _dot(ga_ref[...].astype(BF16), up_ref[...]) + bias_ref[...]
    la = _log_sigmoid(z) * (LOG2_E / GLA_TAU)
    hi = la.astype(BF16)
    lo = (la - hi.astype(F32)).astype(BF16)
    kw = GLA_KEY_WIDTH
    bf_ref[...] = _dot(lf_ref[...], hi[:, :kw]) + _dot(lf_ref[...], lo[:, :kw])
    bb_ref[...] = _dot(lb_ref[...], hi[:, kw:]) + _dot(lb_ref[...], lo[:, kw:])


def _gla_gates(ga, up_pad, bias, lf, lb):
    T = ga.shape[0]
    tm = GATE_TM
    kw = GLA_KEY_WIDTH
    return pl.pallas_call(
        _gates_kernel,
        grid=(T // tm,),
        in_specs=[
            pl.BlockSpec((tm, LANE), lambda i: (i, 0)),
            pl.BlockSpec((LANE, 2 * kw), lambda i: (0, 0)),
            pl.BlockSpec((1, 2 * kw), lambda i: (0, 0)),
            pl.BlockSpec((tm, tm), lambda i: (0, 0)),
            pl.BlockSpec((tm, tm), lambda i: (0, 0)),
        ],
        out_specs=[pl.BlockSpec((tm, kw), lambda i: (i, 0)), pl.BlockSpec((tm, kw), lambda i: (i, 0))],
        out_shape=[jax.ShapeDtypeStruct((T, kw), F32), jax.ShapeDtypeStruct((T, kw), F32)],
        compiler_params=_cparams(("parallel",), 32),
        name="gla_gates",
    )(ga, up_pad, bias, lf, lb)


def _finish_heads(tot, gn, gate):
    ms = jnp.mean(tot * tot, axis=-1, keepdims=True)
    yn = tot * lax.rsqrt(ms + EPS) * gn
    return (yn * _silu(gate.astype(F32))).astype(BF16)


RET_TB = 512


def _ret_kernel(dl_ref, q_ref, k_ref, v_ref, *rest, reverse):
    if reverse:
        g_ref, of_ref, gn_ref, o_ref, s_ref, intra_ref, qd_ref, kd_ref, cd_ref = rest
    else:
        o_ref, s_ref, intra_ref, qd_ref, kd_ref, cd_ref = rest
    h = pl.program_id(1)
    n = pl.program_id(2)
    C = RET_CHUNK

    @pl.when(n == 0)
    def _():
        s_ref[...] = jnp.zeros_like(s_ref)
        logit = dl_ref[1 if reverse else 0, h]
        lg = _log_sigmoid(jnp.full((C, RET_DV), logit, F32))
        lg_c = _log_sigmoid(jnp.full((C, C), logit, F32))
        lg_r = _log_sigmoid(jnp.full((1, RET_DV), logit, F32))
        ri = lax.broadcasted_iota(I32, (C, RET_DV), 0).astype(F32)
        rc = lax.broadcasted_iota(I32, (C, C), 0).astype(F32)
        cc = lax.broadcasted_iota(I32, (C, C), 1).astype(F32)
        diff = (cc - rc) if reverse else (rc - cc)
        intra_ref[...] = jnp.where(diff >= 0, jnp.exp(lg_c * diff), 0.0)
        if reverse:
            qd_ref[...] = jnp.exp(lg * (C - ri))
            kd_ref[...] = jnp.exp(lg * ri)
        else:
            qd_ref[...] = jnp.exp(lg * (ri + 1.0))
            kd_ref[...] = jnp.exp(lg * (C - 1.0 - ri))
        cd_ref[...] = jnp.exp(lg_r * C)

    nchunks = q_ref.shape[0] // C
    order = range(nchunks - 1, -1, -1) if reverse else range(nchunks)
    for c in order:
        rows = slice(c * C, (c + 1) * C)
        q = q_ref[rows, :]
        k = k_ref[rows, :]
        v = v_ref[rows, :]
        s = _dot_nt(q, k) * intra_ref[...]
        state = s_ref[...]
        o = _dot(s.astype(BF16), v) + _dot(q, state.astype(BF16)) * qd_ref[...]
        kd = (k.astype(F32) * kd_ref[...]).astype(BF16)
        s_ref[...] = state * cd_ref[...] + _dot_tn(kd, v)
        if reverse:
            o_ref[rows, :] = _finish_heads(of_ref[rows, :] + o, gn_ref[...], g_ref[rows, :])
        else:
            o_ref[rows, :] = o


def _ret_scan(proj, decay_logit, batch, seq_len, reverse, o_fwd=None, gn_w=None):
    T = proj.shape[0]
    tb = RET_TB
    nb = seq_len // tb
    dk, dv, C = RET_DK, RET_DV, RET_CHUNK

    def rb(b, n):
        return b * nb + ((nb - 1 - n) if reverse else n)

    def col(base):
        return lambda b, h, n: (rb(b, n), base // dk + h)

    in_specs = [
        pl.BlockSpec(memory_space=pltpu.SMEM),
        pl.BlockSpec((tb, dk), col(_RQ)),
        pl.BlockSpec((tb, dk), col(_RK)),
        pl.BlockSpec((tb, dv), col(_RV)),
    ]
    args = [decay_logit, proj, proj, proj]
    if reverse:
        in_specs += [
            pl.BlockSpec((tb, dv), col(_RG)),
            pl.BlockSpec((tb, dv), lambda b, h, n: (rb(b, n), h)),
            pl.BlockSpec((1, dv), lambda b, h, n: (0, h)),
        ]
        args += [proj, o_fwd, gn_w]
    out_dtype = BF16 if reverse else F32
    return pl.pallas_call(
        functools.partial(_ret_kernel, reverse=reverse),
        grid=(batch, RET_HEADS, nb),
        in_specs=in_specs,
        out_specs=pl.BlockSpec((tb, dv), lambda b, h, n: (rb(b, n), h)),
        out_shape=jax.ShapeDtypeStruct((T, RET_WIDTH), out_dtype),
        scratch_shapes=[
            pltpu.VMEM((dk, dv), F32),
            pltpu.VMEM((C, C), F32),
            pltpu.VMEM((C, dv), F32),
            pltpu.VMEM((C, dk), F32),
            pltpu.VMEM((1, dv), F32),
        ],
        compiler_params=_cparams(("parallel", "parallel", "arbitrary"), 32),
        name="ret_bwd" if reverse else "ret_fwd",
    )(*args)


GLA_TB = 512
GLA_UNROLL = 8


def _gla_kernel(q_ref, k_ref, v_ref, b_ref, *rest, reverse):
    if reverse:
        g_ref, of_ref, gn_ref, o_ref, st_ref = rest
    else:
        o_ref, st_ref = rest
    n = pl.program_id(2)
    C, SUB = GLA_CHUNK, GLA_SUB
    NS = C // SUB

    @pl.when(n == 0)
    def _():
        st_ref[...] = jnp.zeros_like(st_ref)

    nchunks = q_ref.shape[0] // C
    row_c = lax.broadcasted_iota(I32, (C, GLA_DK), 0)
    lane_s = lax.broadcasted_iota(I32, (SUB, C), 1)
    lane_h = lax.broadcasted_iota(I32, (SUB // 2, C), 1)
    row_s = lax.broadcasted_iota(I32, (SUB, C), 0)

    def chunk(ci, carry):
        c = (nchunks - 1 - ci) if reverse else ci
        c0 = pl.multiple_of(c * C, C)
        q = q_ref[pl.ds(c0, C), :].astype(F32)
        k = k_ref[pl.ds(c0, C), :].astype(F32)
        v = v_ref[pl.ds(c0, C), :]
        b = b_ref[pl.ds(c0, C), :]
        b_end = b[0:1, :] if reverse else b[C - 1:C, :]

        st = st_ref[...]
        o = _dot_nt((q * jnp.exp2(b)).astype(BF16), st.astype(BF16))
        ke = (k * jnp.exp2(b_end - b)).astype(BF16)
        st_ref[...] = st * jnp.exp2(b_end) + _dot_tn(v, ke)

        prows = []
        for si in range(NS):
            r0 = si * SUB
            b_i = b[r0:r0 + SUB, :]
            q_i = q[r0:r0 + SUB, :]
            halves = [jnp.zeros((SUB // 2, C), F32), jnp.zeros((SUB // 2, C), F32)]
            for jj in range(SUB):
                b_j = b[r0 + jj:r0 + jj + 1, :]
                k_j = k[r0 + jj:r0 + jj + 1, :]
                for hf in range(2):
                    needed = (hf == 0 or jj >= SUB // 2) if reverse else (hf == 1 or jj < SUB // 2)
                    if not needed:
                        continue
                    rs = slice(hf * SUB // 2, (hf + 1) * SUB // 2)
                    w = jnp.exp2(b_i[rs] - b_j)
                    col = jnp.sum(q_i[rs] * k_j * w, axis=-1, keepdims=True)
                    halves[hf] = jnp.where(lane_h == r0 + jj, col, halves[hf])
            sd = jnp.concatenate(halves, axis=0)
            if reverse:
                causal = (row_s + r0) <= lane_s
                ref_row = b[r0 + SUB - 1:r0 + SUB, :]
                has_off = si < NS - 1
                off_rows = row_c >= r0 + SUB
            else:
                causal = (row_s + r0) >= lane_s
                ref_row = b[r0:r0 + 1, :]
                has_off = si > 0
                off_rows = row_c < r0
            s_i = jnp.where(causal, sd, 0.0)
            if has_off:
                qs = (q_i * jnp.exp2(b_i - ref_row)).astype(BF16)
                kk = jnp.where(off_rows, k * jnp.exp2(ref_row - b), 0.0).astype(BF16)
                s_i = s_i + _dot_nt(qs, kk)
            prows.append(s_i)
        p = jnp.concatenate(prows, axis=0).astype(BF16)
        o = o + _dot(p, v)
        if reverse:
            tot = of_ref[pl.ds(c0, C), :] + o
            o_ref[pl.ds(c0, C), :] = _finish_heads(tot, gn_ref[...], g_ref[pl.ds(c0, C), :])
        else:
            o_ref[pl.ds(c0, C), :] = o
        return carry

    lax.fori_loop(0, nchunks, chunk, 0, unroll=GLA_UNROLL)


def _gla_scan(proj, bcum, batch, seq_len, reverse, o_fwd=None, gn_w=None):
    T = proj.shape[0]
    tb = GLA_TB
    nb = seq_len // tb
    dk, dv = GLA_DK, GLA_DV

    def rb(b, n):
        return b * nb + ((nb - 1 - n) if reverse else n)

    in_specs = [
        pl.BlockSpec((tb, dk), lambda b, h, n: (rb(b, n), _GQ // dk + h)),
        pl.BlockSpec((tb, dk), lambda b, h, n: (rb(b, n), _GK // dk + h)),
        pl.BlockSpec((tb, dv), lambda b, h, n: (rb(b, n), _GV // dv + h)),
        pl.BlockSpec((tb, dk), lambda b, h, n: (rb(b, n), h)),
    ]
    args = [proj, proj, proj, bcum]
    if reverse:
        in_specs += [
            pl.BlockSpec((tb, dv), lambda b, h, n: (rb(b, n), _GG // dv + h)),
            pl.BlockSpec((tb, dv), lambda b, h, n: (rb(b, n), h)),
            pl.BlockSpec((1, dv), lambda b, h, n: (0, h)),
        ]
        args += [proj, o_fwd, gn_w]
    out_dtype = BF16 if reverse else F32
    return pl.pallas_call(
        functools.partial(_gla_kernel, reverse=reverse),
        grid=(batch, GLA_HEADS, nb),
        in_specs=in_specs,
        out_specs=pl.BlockSpec((tb, dv), lambda b, h, n: (rb(b, n), h)),
        out_shape=jax.ShapeDtypeStruct((T, GLA_WIDTH), out_dtype),
        scratch_shapes=[pltpu.VMEM((dv, dk), F32)],
        compiler_params=_cparams(("parallel", "parallel", "arbitrary"), 32),
        name="gla_bwd" if reverse else "gla_fwd",
    )(*args)


OP_TM = 512
OP_SUB = 256


def _out_proj_kernel(mr_ref, mg_ref, w0_ref, w1_ref, x_ref, n2_ref, rh_ref, rl_ref, h_ref, xn_ref, aff_ref):
    for r in range(OP_TM // OP_SUB):
        rows = slice(r * OP_SUB, (r + 1) * OP_SUB)
        h = x_ref[rows, :] + _dot(mr_ref[rows, :], w0_ref[...]) + _dot(mg_ref[rows, :], w1_ref[...])
        h_ref[rows, :] = h
        ms = jnp.mean(h * h, axis=-1, keepdims=True)
        xn = h * lax.rsqrt(ms + EPS) * n2_ref[...]
        xh = xn.astype(BF16)
        xn_ref[rows, :] = _pack_bf16_pairs(xn)
        xl = (xn - xh.astype(F32)).astype(BF16)
        lt = _dot_nt(rh_ref[...], xh) + _dot_nt(rh_ref[...], xl) + _dot_nt(rl_ref[...], xh)
        m = jnp.max(lt, axis=0, keepdims=True)
        e = jnp.exp(lt - m)
        aff_ref[:, rows] = e / jnp.sum(e, axis=0, keepdims=True)


def _out_proj(mix_r, mix_g, w_out, x2d, n2w, r_hi, r_lo):
    T = x2d.shape[0]
    tm = OP_TM
    half = RET_WIDTH
    return pl.pallas_call(
        _out_proj_kernel,
        grid=(T // tm,),
        in_specs=[
            pl.BlockSpec((tm, half), lambda i: (i, 0)),
            pl.BlockSpec((tm, half), lambda i: (i, 0)),
            pl.BlockSpec((half, D_MODEL), lambda i: (0, 0)),
            pl.BlockSpec((half, D_MODEL), lambda i: (1, 0)),
            pl.BlockSpec((tm, D_MODEL), lambda i: (i, 0)),
            pl.BlockSpec((1, D_MODEL), lambda i: (0, 0)),
            pl.BlockSpec((N_EXPERTS, D_MODEL), lambda i: (0, 0)),
            pl.BlockSpec((N_EXPERTS, D_MODEL), lambda i: (0, 0)),
        ],
        out_specs=[
            pl.BlockSpec((tm, D_MODEL), lambda i: (i, 0)),
            pl.BlockSpec((tm, D_MODEL // 2), lambda i: (i, 0)),
            pl.BlockSpec((N_EXPERTS, tm), lambda i: (0, i)),
        ],
        out_shape=[
            jax.ShapeDtypeStruct((T, D_MODEL), F32),
            jax.ShapeDtypeStruct((T, D_MODEL // 2), I32),
            jax.ShapeDtypeStruct((N_EXPERTS, T), F32),
        ],
        compiler_params=_cparams(("parallel",), 56),
        name="out_proj",
    )(mix_r, mix_g, w_out, w_out, x2d, n2w, r_hi, r_lo)


def _select_kernel(a_ref, pos_ref, posb_ref, rankb_ref, affb_ref, *, cap):
    E, T = a_ref.shape
    tt = MOE_TT

    def count(pred):
        return jnp.sum(pred.astype(F32), axis=1, keepdims=True)

    def bisect(i, tau):
        cand = tau | jnp.left_shift(jnp.int32(1), 30 - i)
        bits = pltpu.bitcast(a_ref[...], I32)
        return jnp.where(count(bits >= cand) >= cap, cand, tau)

    tau = lax.fori_loop(0, 31, bisect, jnp.zeros((E, 1), I32))
    bits_all = pltpu.bitcast(a_ref[...], I32)
    quota = cap - count(bits_all > tau)

    before = (lax.broadcasted_iota(I32, (tt, tt), 0) < lax.broadcasted_iota(I32, (tt, tt), 1)).astype(BF16)
    below = (lax.broadcasted_iota(I32, (E, E), 1) < lax.broadcasted_iota(I32, (E, E), 0)).astype(BF16)

    def block(j, carry):
        c_eq, c_sel = carry
        off = pl.multiple_of(j * tt, tt)
        aff = a_ref[:, pl.ds(off, tt)]
        bits = pltpu.bitcast(aff, I32)
        eq = bits == tau
        eqf = eq.astype(F32)
        rank_eq = _dot(eqf.astype(BF16), before) + c_eq
        sel = (bits > tau) | (eq & (rank_eq < quota))
        self_ = sel.astype(F32)
        selb = self_.astype(BF16)
        slot = _dot(selb, before) + c_sel
        pos = jnp.where(sel, slot, -1.0).astype(I32)
        pos_ref[:, pl.ds(off, tt)] = pos
        per_tok = jnp.broadcast_to(jnp.sum(self_, axis=0, keepdims=True), (E, tt))
        rank = _dot(per_tok.astype(BF16), before) + _dot(below, selb)
        posb_ref[j] = pos
        rankb_ref[j] = jnp.where(sel, rank, -1.0).astype(I32)
        affb_ref[j] = aff
        return (c_eq + jnp.sum(eqf, axis=1, keepdims=True), c_sel + jnp.sum(self_, axis=1, keepdims=True))

    zero = jnp.zeros((E, 1), F32)
    lax.fori_loop(0, T // tt, block, (zero, zero))


def _select(aff, cap):
    E, T = aff.shape
    nb = T // MOE_TT
    blk = jax.ShapeDtypeStruct((nb, E, MOE_TT), I32)
    return pl.pallas_call(
        functools.partial(_select_kernel, cap=cap),
        out_shape=[jax.ShapeDtypeStruct((E, T), I32), blk, blk, jax.ShapeDtypeStruct((nb, E, MOE_TT), F32)],
        compiler_params=pltpu.CompilerParams(vmem_limit_bytes=40 * 1024 * 1024),
        name="select",
    )(aff)


def _regroup_rows(T):
    nb = T // MOE_TT
    rows = CAPACITY_FACTOR * T + SC_GATHER_ROWS * nb
    return -(-rows // MOE_TW) * MOE_TW


def _combine_schedule(posb, T):
    nb = posb.shape[0]
    tw, g = MOE_TW, SC_GATHER_ROWS
    n = jnp.sum((posb >= 0).reshape(nb, -1), axis=1).astype(I32)
    seg = (n + g - 1) // g * g
    hi = jnp.cumsum(seg)
    lo = hi - seg
    off = jnp.concatenate([jnp.zeros((1,), I32), hi])
    nwin_max = (N_EXPERTS * MOE_TT) // tw + 1
    w0 = lo // tw
    w1 = jnp.where(seg > 0, (hi - 1) // tw, w0)
    cand = jnp.arange(nwin_max, dtype=I32)
    win = w0[:, None] + cand[None, :]
    valid = (win <= w1[:, None]).reshape(-1)
    nwin_total = _regroup_rows(T) // tw
    pmax = nb + nwin_total
    jv = jnp.broadcast_to(jnp.arange(nb, dtype=I32)[:, None], win.shape).reshape(-1)
    wv = jnp.minimum(win, nwin_total - 1).reshape(-1)
    total = jnp.sum(valid.astype(I32))
    dst = jnp.where(valid, jnp.cumsum(valid.astype(I32)) - 1, pmax)
    pj, pw = (jnp.zeros((pmax,), I32).at[dst].set(a, mode="drop") for a in (jv, wv))
    real = jnp.arange(pmax, dtype=I32) < total
    pj, pw = (jnp.where(real, a, a[total - 1]) for a in (pj, pw))
    first = jnp.concatenate([jnp.ones((1,), bool), pj[1:] != pj[:-1]])
    last = jnp.concatenate([pj[1:] != pj[:-1], jnp.ones((1,), bool)]) | (jnp.arange(pmax, dtype=I32) == total - 1)
    flag = jnp.where(real, first.astype(I32) + 2 * last.astype(I32) + 4, 0)
    return off, (pj, pw, flag, lo[pj], hi[pj])


SC_LANES = 16
SC_CORES = 2
SC_SUBCORES = 16
SC_GATHER_ROWS = 32


def _dispatch(pos, xw, cap):
    E, T = pos.shape
    W = xw.shape[1]
    G = SC_GATHER_ROWS
    part_rows = cap // SC_CORES
    mesh = plsc.VectorSubcoreMesh(core_axis_name="c", subcore_axis_name="s")

    @pl.kernel(
        out_type=jax.ShapeDtypeStruct((E * cap, W), I32),
        mesh=mesh,
        scratch_types=[pltpu.VMEM((T,), I32), pltpu.VMEM((cap,), I32), pltpu.VMEM((G, W), I32)],
        compiler_params=pltpu.CompilerParams(needs_layout_passes=False),
        name="sc_dispatch",
    )
    def run(pos_hbm, x_hbm, xe_hbm, pos_v, idx_v, buf):
        e = lax.axis_index("s")
        part = lax.axis_index("c")
        pltpu.sync_copy(pos_hbm.at[e], pos_v)
        lane = lax.iota(I32, SC_LANES)

        @pl.loop(0, T // SC_LANES)
        def _(i):
            off = pl.multiple_of(i * SC_LANES, SC_LANES)
            p = pos_v[pl.ds(off, SC_LANES)]
            plsc.store_scatter(idx_v, [p], lane + off, mask=p >= 0)

        @pl.loop(0, part_rows // G)
        def _(g):
            o = pl.multiple_of(part * part_rows + g * G, G)
            pltpu.sync_copy(x_hbm.at[idx_v.at[pl.ds(o, G)]], buf)
            pltpu.sync_copy(buf, xe_hbm.at[pl.ds(e * cap + o, G)])

    return run(pos, xw)


FFN_TM = 1024
FFN_TF = 256


FFN_TN = 128
FFN_UP_STEPS = D_FF // FFN_TF
FFN_DOWN_STEPS = (D_MODEL // 2) // FFN_TN


def _ffn_kernel(x_ref, wg_ref, wu_ref, wdl_ref, wdh_ref, o_ref, xb_ref, hid_ref):
    s = pl.program_id(2)

    @pl.when(s == 0)
    def _():
        lo, hi = _unpack_bf16_pairs(x_ref[...])
        half = D_MODEL // 2
        xb_ref[:, :half] = lo.astype(BF16)
        xb_ref[:, half:] = hi.astype(BF16)

    @pl.when(s < FFN_UP_STEPS)
    def _():
        x = xb_ref[...]
        g = _dot(x, wg_ref[...].astype(BF16))
        u = _dot(x, wu_ref[...].astype(BF16))
        col = pl.multiple_of(s * FFN_TF, FFN_TF)
        hid_ref[:, pl.ds(col, FFN_TF)] = (_silu(g) * u).astype(BF16)

    @pl.when(s >= FFN_UP_STEPS)
    def _():
        wd = jnp.concatenate([wdl_ref[...].astype(BF16), wdh_ref[...].astype(BF16)], axis=1)
        o_ref[...] = _pack_bf16_pairs(_dot(hid_ref[...], wd))


def _ffn(xe, w_gate, w_up, w_down):
    E, cap, _ = xe.shape
    tm, tf, tn = min(FFN_TM, cap), FFN_TF, FFN_TN
    n_up, n_down = FFN_UP_STEPS, FFN_DOWN_STEPS

    def up(e, m, s):
        return (e, 0, jnp.minimum(s, n_up - 1))

    def down(s):
        return jnp.maximum(s - n_up, 0)

    return pl.pallas_call(
        _ffn_kernel,
        grid=(E, cap // tm, n_up + n_down),
        in_specs=[
            pl.BlockSpec((None, tm, D_MODEL // 2), lambda e, m, s: (e, m, 0)),
            pl.BlockSpec((None, D_MODEL, tf), up),
            pl.BlockSpec((None, D_MODEL, tf), up),
            pl.BlockSpec((None, D_FF, tn), lambda e, m, s: (e, 0, down(s))),
            pl.BlockSpec((None, D_FF, tn), lambda e, m, s: (e, 0, n_down + down(s))),
        ],
        out_specs=pl.BlockSpec((None, tm, tn), lambda e, m, s: (e, m, down(s))),
        out_shape=jax.ShapeDtypeStruct((E, cap, D_MODEL // 2), I32),
        scratch_shapes=[pltpu.VMEM((tm, D_MODEL), BF16), pltpu.VMEM((tm, D_FF), BF16)],
        compiler_params=_cparams(("parallel", "parallel", "arbitrary"), 56),
        name="ffn",
    )(xe, w_gate, w_up, w_down, w_down)


def _regroup(posb, rankb, affb, off, yw, cap, rows):
    NB, EB = posb.shape
    W = yw.shape[1]
    G, L, tt = SC_GATHER_ROWS, SC_LANES, MOE_TT
    n_workers = SC_CORES * SC_SUBCORES
    per = NB // n_workers
    mesh = plsc.VectorSubcoreMesh(core_axis_name="c", subcore_axis_name="s")
    off_pad = jnp.pad(off, (0, L))

    @pl.kernel(
        out_type=(jax.ShapeDtypeStruct((rows, W), I32), jax.ShapeDtypeStruct((rows,), I32),
                  jax.ShapeDtypeStruct((rows,), F32)),
        mesh=mesh,
        scratch_types=[pltpu.VMEM((EB,), I32), pltpu.VMEM((EB,), I32), pltpu.VMEM((EB,), F32),
                       pltpu.VMEM((EB,), I32), pltpu.VMEM((EB,), I32), pltpu.VMEM((EB,), F32),
                       pltpu.VMEM((G, W), I32), pltpu.VMEM((NB + 1 + L,), I32)],
        compiler_params=pltpu.CompilerParams(needs_layout_passes=False),
        name="sc_regroup",
    )
    def run(posb_hbm, rankb_hbm, affb_hbm, off_hbm, y_hbm, yg_hbm, tok_hbm, gate_hbm,
            pos_v, rank_v, aff_v, src_v, tok_v, gate_v, buf, off_v):
        wid = lax.axis_index("c") * SC_SUBCORES + lax.axis_index("s")
        pltpu.sync_copy(off_hbm, off_v)
        lane = lax.iota(I32, L)
        zi = jnp.zeros((L,), I32)
        zf = jnp.zeros((L,), F32)

        @pl.loop(0, per)
        def _(k):
            j = wid * per + k
            pltpu.sync_copy(posb_hbm.at[j], pos_v)
            pltpu.sync_copy(rankb_hbm.at[j], rank_v)
            pltpu.sync_copy(affb_hbm.at[j], aff_v)
            lo = jnp.max(plsc.load_gather(off_v, [zi + j]))
            hi = jnp.max(plsc.load_gather(off_v, [zi + j + 1]))

            @pl.loop(0, EB // L)
            def _(i):
                o = pl.multiple_of(i * L, L)
                src_v[pl.ds(o, L)] = zi
                tok_v[pl.ds(o, L)] = zi
                gate_v[pl.ds(o, L)] = zf

            @pl.loop(0, EB // L)
            def _(i):
                o = pl.multiple_of(i * L, L)
                p = pos_v[pl.ds(o, L)]
                r = rank_v[pl.ds(o, L)]
                m = p >= 0
                e = i // (tt // L)
                t0 = j * tt + (i % (tt // L)) * L
                plsc.store_scatter(src_v, [r], p + e * cap, mask=m)
                plsc.store_scatter(tok_v, [r], lane + t0, mask=m)
                plsc.store_scatter(gate_v, [r], aff_v[pl.ds(o, L)], mask=m)

            @pl.loop(0, (hi - lo) // G)
            def _(g):
                o = pl.multiple_of(g * G, G)
                dst = pl.multiple_of(lo + o, G)
                pltpu.sync_copy(y_hbm.at[src_v.at[pl.ds(o, G)]], buf)
                pltpu.sync_copy(buf, yg_hbm.at[pl.ds(dst, G)])
                pltpu.sync_copy(tok_v.at[pl.ds(o, G)], tok_hbm.at[pl.ds(dst, G)])
                pltpu.sync_copy(gate_v.at[pl.ds(o, G)], gate_hbm.at[pl.ds(dst, G)])

    return run(posb, rankb, affb, off_pad, yw)


def _combine_kernel(pj_ref, pw_ref, pf_ref, plo_ref, phi_ref, tok_ref, gate_ref, yg_ref, h_ref, nf_ref,
                    o_ref, acc_ref):
    p = pl.program_id(0)
    flag = pf_ref[p]
    tw, tt = MOE_TW, MOE_TT
    half = D_MODEL // 2

    @pl.when((flag & 1) != 0)
    def _():
        acc_ref[...] = h_ref[...]

    @pl.when((flag & 4) != 0)
    def _():
        lo, hi = plo_ref[p], phi_ref[p]
        row0 = pw_ref[p] * tw
        rid = lax.broadcasted_iota(I32, (tw, 1), 0) + row0
        keep = (rid >= lo) & (rid < hi)
        y_lo, y_hi = _unpack_bf16_pairs(yg_ref[...])
        y_lo = jnp.where(keep, y_lo, 0.0).astype(BF16)
        y_hi = jnp.where(keep, y_hi, 0.0).astype(BF16)
        tid = lax.broadcasted_iota(I32, (tt, tw), 0) + pj_ref[p] * tt
        cid = lax.broadcasted_iota(I32, (tt, tw), 1) + row0
        hit = (tok_ref[...] == tid) & (cid >= lo) & (cid < hi)
        weights = jnp.where(hit, gate_ref[...], 0.0).astype(BF16)
        acc_ref[:, :half] += _dot(weights, y_lo)
        acc_ref[:, half:] += _dot(weights, y_hi)

    @pl.when((flag & 2) != 0)
    def _():
        y = acc_ref[...]
        ms = jnp.mean(y * y, axis=-1, keepdims=True)
        o_ref[...] = y * lax.rsqrt(ms + EPS) * nf_ref[...]


def _combine(lists, tok, gate, yg, h, nfw):
    pj, pw, pf, plo, phi = lists
    T = h.shape[0]
    tw, tt = MOE_TW, MOE_TT
    nwin = yg.shape[0] // tw
    grid_spec = pltpu.PrefetchScalarGridSpec(
        num_scalar_prefetch=5,
        grid=(pj.shape[0],),
        in_specs=[
            pl.BlockSpec((None, 1, tw), lambda p, pj, pw, *_: (pw[p], 0, 0)),
            pl.BlockSpec((None, 1, tw), lambda p, pj, pw, *_: (pw[p], 0, 0)),
            pl.BlockSpec((tw, D_MODEL // 2), lambda p, pj, pw, *_: (pw[p], 0)),
            pl.BlockSpec((tt, D_MODEL), lambda p, pj, pw, *_: (pj[p], 0)),
            pl.BlockSpec((1, D_MODEL), lambda p, pj, pw, *_: (0, 0)),
        ],
        out_specs=pl.BlockSpec((tt, D_MODEL), lambda p, pj, pw, *_: (pj[p], 0)),
        scratch_shapes=[pltpu.VMEM((tt, D_MODEL), F32)],
    )
    return pl.pallas_call(
        _combine_kernel,
        grid_spec=grid_spec,
        out_shape=jax.ShapeDtypeStruct((T, D_MODEL), F32),
        compiler_params=_cparams(("arbitrary",), 32),
        name="combine",
    )(pj, pw, pf, plo, phi, tok.reshape(nwin, 1, tw), gate.reshape(nwin, 1, tw), yg, h, nfw)


def _rope_tables(seq_len):
    d = RET_DK
    inv = ROPE_BASE ** (-jnp.arange(0, d, 2, dtype=F32) / d)
    ang = jnp.arange(seq_len, dtype=F32)[:, None] * inv[None, :]
    return jnp.cos(ang), jnp.sin(ang)


def _chunk_tri(n, chunk, upper):
    r = np.arange(n)
    same = (r[:, None] // chunk) == (r[None, :] // chunk)
    tri = (r[:, None] <= r[None, :]) if upper else (r[:, None] >= r[None, :])
    return jnp.asarray(same & tri, BF16)


def _prep_params(norm1_w, w_in, ret_gn_w, gla_gate_up, gla_gate_bias, gla_gn_w, w_out, norm2_w, router_w,
                 normf_w):
    w = w_in[0]
    w_main = w[:, :IN_MAIN].astype(BF16)
    w_ga = jnp.pad(w[:, IN_MAIN:], ((0, 0), (0, LANE - 2 * GLA_RANK))).astype(BF16)
    cs = np.ones((1, IN_MAIN), np.float32)
    cs[:, _RQ:_RQ + RET_WIDTH] = RET_DK ** -0.5
    cs[:, _GQ:_GQ + GLA_KEY_WIDTH] = GLA_DK ** -0.5
    up = gla_gate_up[0].astype(F32)
    up_pad = jnp.zeros((LANE, 2 * GLA_KEY_WIDTH), F32)
    up_pad = up_pad.at[:GLA_RANK, :GLA_KEY_WIDTH].set(up[0])
    up_pad = up_pad.at[GLA_RANK:2 * GLA_RANK, GLA_KEY_WIDTH:].set(up[1])
    rt = router_w[0].T.astype(F32)
    r_hi = rt.astype(BF16)
    r_lo = (rt - r_hi.astype(F32)).astype(BF16)
    return dict(
        n1w=norm1_w[0].reshape(1, D_MODEL).astype(F32),
        w_main=w_main, w_ga=w_ga, colscale=jnp.asarray(cs),
        up_pad=up_pad.astype(BF16),
        bias=gla_gate_bias[0].reshape(1, 2 * GLA_KEY_WIDTH).astype(F32),
        lf=_chunk_tri(GATE_TM, GLA_CHUNK, upper=False),
        lb=_chunk_tri(GATE_TM, GLA_CHUNK, upper=True),
        ret_gn=ret_gn_w[0].reshape(1, RET_WIDTH).astype(F32),
        gla_gn=gla_gn_w[0].reshape(1, GLA_WIDTH).astype(F32),
        w_out=w_out[0].astype(BF16),
        n2w=norm2_w[0].reshape(1, D_MODEL).astype(F32),
        r_hi=r_hi, r_lo=r_lo,
        nfw=normf_w.reshape(1, D_MODEL).astype(F32),
    )


def _trunk(x, pp, decay_logit, w_gate, w_up, w_down):
    B, L, _ = x.shape
    T = B * L
    x2d = x.reshape(T, D_MODEL)
    cos, sin = _rope_tables(L)
    proj, ga = _in_proj(x2d, pp["n1w"], pp["w_main"], pp["w_ga"], pp["colscale"], cos, sin, L)
    b_f, b_b = _gla_gates(ga, pp["up_pad"], pp["bias"], pp["lf"], pp["lb"])

    ret_f = _ret_scan(proj, decay_logit, B, L, reverse=False)
    mix_r = _ret_scan(proj, decay_logit, B, L, reverse=True, o_fwd=ret_f, gn_w=pp["ret_gn"])
    gla_f = _gla_scan(proj, b_f, B, L, reverse=False)
    mix_g = _gla_scan(proj, b_b, B, L, reverse=True, o_fwd=gla_f, gn_w=pp["gla_gn"])

    h, xn2, aff = _out_proj(mix_r, mix_g, pp["w_out"], x2d, pp["n2w"], pp["r_hi"], pp["r_lo"])

    cap = CAPACITY_FACTOR * T // N_EXPERTS
    pos, posb, rankb, affb = _select(aff, cap)
    off, c_lists = _combine_schedule(posb, T)
    xe = _dispatch(pos, xn2, cap).reshape(N_EXPERTS, cap, D_MODEL // 2)
    ye = _ffn(xe, w_gate, w_up, w_down).reshape(N_EXPERTS * cap, D_MODEL // 2)
    nb = T // MOE_TT
    yg, tok, gate = _regroup(posb.reshape(nb, -1), rankb.reshape(nb, -1), affb.reshape(nb, -1), off, ye, cap,
                             _regroup_rows(T))
    y = _combine(c_lists, tok, gate, yg, h, pp["nfw"])
    return y.reshape(B, L, D_MODEL)


def kernel(x_prompt, x_sample, norm1_w, w_in, ret_decay_logit, ret_gn_w, gla_gate_up, gla_gate_bias,
           gla_gn_w, w_out, norm2_w, router_w, w_gate, w_up, w_down, normf_w):
    pp = _prep_params(norm1_w, w_in, ret_gn_w, gla_gate_up, gla_gate_bias, gla_gn_w, w_out, norm2_w,
                      router_w, normf_w)
    decay_logit = ret_decay_logit[0].astype(F32)
    args = (pp, decay_logit, w_gate[0], w_up[0], w_down[0])
    return (_trunk(x_prompt, *args), _trunk(x_sample, *args))
```

```python
import functools

import numpy as np
import jax
import jax.numpy as jnp
from jax import lax
from jax.experimental import pallas as pl
from jax.experimental.pallas import tpu as pltpu
from jax.experimental.pallas import tpu_sc as plsc

F32, BF16, I32 = jnp.float32, jnp.bfloat16, jnp.int32

D_MODEL = 2048
RET_WIDTH = 1024
RET_HEADS = 4
RET_DK = 256
RET_DV = 256
GLA_WIDTH = 1024
GLA_HEADS = 4
GLA_DK = 128
GLA_DV = 256
GLA_KEY_WIDTH = 512
GLA_RANK = 16
GLA_TAU = 16.0
RET_CHUNK = 256
GLA_CHUNK = 64
GLA_SUB = 16
ROPE_BASE = 10000.0
N_EXPERTS = 16
CAPACITY_FACTOR = 2
D_FF = 2048
EPS = 1e-6
LOG2_E = 1.4426950408889634
IN_MAIN = 4 * RET_WIDTH + 2 * GLA_KEY_WIDTH + 2 * GLA_WIDTH

_RQ, _RK, _RV, _RG = 0, 1024, 2048, 3072
_GQ, _GK, _GV, _GG = 4096, 4608, 5120, 6144

LANE = 128
MOE_TT = 256
MOE_TW = 256
V7X_VMEM_BYTES = 64 * 1024 * 1024


def _cparams(sem, vmem_mb):
    return pltpu.CompilerParams(dimension_semantics=sem, vmem_limit_bytes=vmem_mb * 1024 * 1024)


def _log_sigmoid(z):
    return jnp.minimum(z, 0.0) - jnp.log1p(jnp.exp(-jnp.abs(z)))


def _silu(g):
    return g * (1.0 / (1.0 + jnp.exp(-g)))


def _dot_nt(a, b):
    return lax.dot_general(a, b, (((1,), (1,)), ((), ())), preferred_element_type=F32)


def _dot_tn(a, b):
    return lax.dot_general(a, b, (((0,), (0,)), ((), ())), preferred_element_type=F32)


def _dot(a, b):
    return jnp.dot(a, b, preferred_element_type=F32)


def _pack_bf16_pairs(x):
    bits = pltpu.bitcast(x.astype(BF16).astype(F32), I32)
    w = x.shape[1] // 2
    return bits[:, w:] | lax.shift_right_logical(bits[:, :w], 16)


def _unpack_bf16_pairs(words):
    lo = pltpu.bitcast(lax.shift_left(words, 16), F32)
    hi = pltpu.bitcast(words & jnp.int32(-65536), F32)
    return lo, hi


IP_TM = 1024
IP_TN = 1024


def _in_proj_kernel(x_ref, n1_ref, w_ref, wga_ref, cs_ref, cos_ref, sin_ref, o_ref, ga_ref, xn_ref):
    j = pl.program_id(1)

    @pl.when(j == 0)
    def _():
        x = x_ref[...]
        ms = jnp.mean(x * x, axis=-1, keepdims=True)
        xn = (x * lax.rsqrt(ms + EPS) * n1_ref[...]).astype(BF16)
        xn_ref[...] = xn
        ga_ref[...] = _dot(xn, wga_ref[...])

    acc = _dot(xn_ref[...], w_ref[...]) * cs_ref[...]
    n_rope_blocks = 2 * RET_WIDTH // IP_TN

    @pl.when(j < n_rope_blocks)
    def _():
        cos = cos_ref[...]
        sin = sin_ref[...]
        half = RET_DK // 2
        for h in range(IP_TN // RET_DK):
            lo = h * RET_DK
            x1 = acc[:, lo:lo + half]
            x2 = acc[:, lo + half:lo + RET_DK]
            o_ref[:, lo:lo + half] = (x1 * cos - x2 * sin).astype(BF16)
            o_ref[:, lo + half:lo + RET_DK] = (x1 * sin + x2 * cos).astype(BF16)

    @pl.when(j >= n_rope_blocks)
    def _():
        o_ref[...] = acc.astype(BF16)


def _in_proj(x2d, n1w, w_main, w_ga, colscale, cos, sin, seq_len):
    T = x2d.shape[0]
    tm, tn = IP_TM, IP_TN
    nlb = seq_len // tm
    return pl.pallas_call(
        _in_proj_kernel,
        grid=(T // tm, IN_MAIN // tn),
        in_specs=[
            pl.BlockSpec((tm, D_MODEL), lambda i, j: (i, 0)),
            pl.BlockSpec((1, D_MODEL), lambda i, j: (0, 0)),
            pl.BlockSpec((D_MODEL, tn), lambda i, j: (0, j)),
            pl.BlockSpec((D_MODEL, LANE), lambda i, j: (0, 0)),
            pl.BlockSpec((1, tn), lambda i, j: (0, j)),
            pl.BlockSpec((tm, LANE), lambda i, j: (i % nlb, 0)),
            pl.BlockSpec((tm, LANE), lambda i, j: (i % nlb, 0)),
        ],
        out_specs=[
            pl.BlockSpec((tm, tn), lambda i, j: (i, j)),
            pl.BlockSpec((tm, LANE), lambda i, j: (i, 0)),
        ],
        out_shape=[
            jax.ShapeDtypeStruct((T, IN_MAIN), BF16),
            jax.ShapeDtypeStruct((T, LANE), F32),
        ],
        scratch_shapes=[pltpu.VMEM((tm, D_MODEL), BF16)],
        compiler_params=_cparams(("parallel", "arbitrary"), 48),
        name="in_proj",
    )(x2d, n1w, w_main, w_ga, colscale, cos, sin)


GATE_TM = 256


def _gates_kernel(ga_ref, up_ref, bias_ref, lf_ref, lb_ref, bf_ref, bb_ref):
    z = _dot(ga_ref[...].astype(BF16), up_ref[...]) + bias_ref[...]
    la = _log_sigmoid(z) * (LOG2_E / GLA_TAU)
    hi = la.astype(BF16)
    lo = (la - hi.astype(F32)).astype(BF16)
    kw = GLA_KEY_WIDTH
    bf_ref[...] = _dot(lf_ref[...], hi[:, :kw]) + _dot(lf_ref[...], lo[:, :kw])
    bb_ref[...] = _dot(lb_ref[...], hi[:, kw:]) + _dot(lb_ref[...], lo[:, kw:])


def _gla_gates(ga, up_pad, bias, lf, lb):
    T = ga.shape[0]
    tm = GATE_TM
    kw = GLA_KEY_WIDTH
    return pl.pallas_call(
        _gates_kernel,
        grid=(T // tm,),
        in_specs=[
            pl.BlockSpec((tm, LANE), lambda i: (i, 0)),
            pl.BlockSpec((LANE, 2 * kw), lambda i: (0, 0)),
            pl.BlockSpec((1, 2 * kw), lambda i: (0, 0)),
            pl.BlockSpec((tm, tm), lambda i: (0, 0)),
            pl.BlockSpec((tm, tm), lambda i: (0, 0)),
        ],
        out_specs=[pl.BlockSpec((tm, kw), lambda i: (i, 0)), pl.BlockSpec((tm, kw), lambda i: (i, 0))],
        out_shape=[jax.ShapeDtypeStruct((T, kw), F32), jax.ShapeDtypeStruct((T, kw), F32)],
        compiler_params=_cparams(("parallel",), 32),
        name="gla_gates",
    )(ga, up_pad, bias, lf, lb)


def _finish_heads(tot, gn, gate):
    ms = jnp.mean(tot * tot, axis=-1, keepdims=True)
    yn = tot * lax.rsqrt(ms + EPS) * gn
    return (yn * _silu(gate.astype(F32))).astype(BF16)


RET_TB = 1024


def _ret_kernel(dl_ref, q_ref, k_ref, v_ref, *rest, reverse):
    if reverse:
        g_ref, of_ref, gn_ref, o_ref, s_ref, intra_ref, qd_ref, kd_ref, cd_ref = rest
    else:
        o_ref, s_ref, intra_ref, qd_ref, kd_ref, cd_ref = rest
    h = pl.program_id(1)
    n = pl.program_id(2)
    C = RET_CHUNK

    @pl.when(n == 0)
    def _():
        s_ref[...] = jnp.zeros_like(s_ref)
        logit = dl_ref[1 if reverse else 0, h]
        lg = _log_sigmoid(jnp.full((C, RET_DV), logit, F32))
        lg_c = _log_sigmoid(jnp.full((C, C), logit, F32))
        lg_r = _log_sigmoid(jnp.full((1, RET_DV), logit, F32))
        ri = lax.broadcasted_iota(I32, (C, RET_DV), 0).astype(F32)
        rc = lax.broadcasted_iota(I32, (C, C), 0).astype(F32)
        cc = lax.broadcasted_iota(I32, (C, C), 1).astype(F32)
        diff = (cc - rc) if reverse else (rc - cc)
        intra_ref[...] = jnp.where(diff >= 0, jnp.exp(lg_c * diff), 0.0)
        if reverse:
            qd_ref[...] = jnp.exp(lg * (C - ri))
            kd_ref[...] = jnp.exp(lg * ri)
        else:
            qd_ref[...] = jnp.exp(lg * (ri + 1.0))
            kd_ref[...] = jnp.exp(lg * (C - 1.0 - ri))
        cd_ref[...] = jnp.exp(lg_r * C)

    nchunks = q_ref.shape[0] // C
    order = range(nchunks - 1, -1, -1) if reverse else range(nchunks)
    for c in order:
        rows = slice(c * C, (c + 1) * C)
        q = q_ref[rows, :]
        k = k_ref[rows, :]
        v = v_ref[rows, :]
        s = _dot_nt(q, k) * intra_ref[...]
        state = s_ref[...]
        o = _dot(s.astype(BF16), v) + _dot(q, state.astype(BF16)) * qd_ref[...]
        kd = (k.astype(F32) * kd_ref[...]).astype(BF16)
        s_ref[...] = state * cd_ref[...] + _dot_tn(kd, v)
        if reverse:
            o_ref[rows, :] = _finish_heads(of_ref[rows, :] + o, gn_ref[...], g_ref[rows, :])
        else:
            o_ref[rows, :] = o


def _ret_scan(proj, decay_logit, batch, seq_len, reverse, o_fwd=None, gn_w=None):
    T = proj.shape[0]
    tb = RET_TB
    nb = seq_len // tb
    dk, dv, C = RET_DK, RET_DV, RET_CHUNK

    def rb(b, n):
        return b * nb + ((nb - 1 - n) if reverse else n)

    def col(base):
        return lambda b, h, n: (rb(b, n), base // dk + h)

    in_specs = [
        pl.BlockSpec(memory_space=pltpu.SMEM),
        pl.BlockSpec((tb, dk), col(_RQ)),
        pl.BlockSpec((tb, dk), col(_RK)),
        pl.BlockSpec((tb, dv), col(_RV)),
    ]
    args = [decay_logit, proj, proj, proj]
    if reverse:
        in_specs += [
            pl.BlockSpec((tb, dv), col(_RG)),
            pl.BlockSpec((tb, dv), lambda b, h, n: (rb(b, n), h)),
            pl.BlockSpec((1, dv), lambda b, h, n: (0, h)),
        ]
        args += [proj, o_fwd, gn_w]
    out_dtype = BF16 if reverse else F32
    return pl.pallas_call(
        functools.partial(_ret_kernel, reverse=reverse),
        grid=(batch, RET_HEADS, nb),
        in_specs=in_specs,
        out_specs=pl.BlockSpec((tb, dv), lambda b, h, n: (rb(b, n), h)),
        out_shape=jax.ShapeDtypeStruct((T, RET_WIDTH), out_dtype),
        scratch_shapes=[
            pltpu.VMEM((dk, dv), F32),
            pltpu.VMEM((C, C), F32),
            pltpu.VMEM((C, dv), F32),
            pltpu.VMEM((C, dk), F32),
            pltpu.VMEM((1, dv), F32),
        ],
        compiler_params=_cparams(("parallel", "parallel", "arbitrary"), 32),
        name="ret_bwd" if reverse else "ret_fwd",
    )(*args)


GLA_TB = 512
GLA_UNROLL = 8


def _gla_kernel(q_ref, k_ref, v_ref, b_ref, *rest, reverse):
    if reverse:
        g_ref, of_ref, gn_ref, o_ref, st_ref = rest
    else:
        o_ref, st_ref = rest
    n = pl.program_id(2)
    C, SUB = GLA_CHUNK, GLA_SUB
    NS = C // SUB

    @pl.when(n == 0)
    def _():
        st_ref[...] = jnp.zeros_like(st_ref)

    nchunks = q_ref.shape[0] // C
    row_c = lax.broadcasted_iota(I32, (C, GLA_DK), 0)
    lane_s = lax.broadcasted_iota(I32, (SUB, C), 1)
    lane_h = lax.broadcasted_iota(I32, (SUB // 2, C), 1)
    row_s = lax.broadcasted_iota(I32, (SUB, C), 0)

    def chunk(ci, carry):
        c = (nchunks - 1 - ci) if reverse else ci
        c0 = pl.multiple_of(c * C, C)
        q = q_ref[pl.ds(c0, C), :].astype(F32)
        k = k_ref[pl.ds(c0, C), :].astype(F32)
        v = v_ref[pl.ds(c0, C), :]
        b = b_ref[pl.ds(c0, C), :]
        b_end = b[0:1, :] if reverse else b[C - 1:C, :]

        st = st_ref[...]
        o = _dot_nt((q * jnp.exp2(b)).astype(BF16), st.astype(BF16))
        ke = (k * jnp.exp2(b_end - b)).astype(BF16)
        st_ref[...] = st * jnp.exp2(b_end) + _dot_tn(v, ke)

        prows = []
        for si in range(NS):
            r0 = si * SUB
            b_i = b[r0:r0 + SUB, :]
            q_i = q[r0:r0 + SUB, :]
            halves = [jnp.zeros((SUB // 2, C), F32), jnp.zeros((SUB // 2, C), F32)]
            for jj in range(SUB):
                b_j = b[r0 + jj:r0 + jj + 1, :]
                k_j = k[r0 + jj:r0 + jj + 1, :]
                for hf in range(2):
                    needed = (hf == 0 or jj >= SUB // 2) if reverse else (hf == 1 or jj < SUB // 2)
                    if not needed:
                        continue
                    rs = slice(hf * SUB // 2, (hf + 1) * SUB // 2)
                    w = jnp.exp2(b_i[rs] - b_j)
                    col = jnp.sum(q_i[rs] * k_j * w, axis=-1, keepdims=True)
                    halves[hf] = jnp.where(lane_h == r0 + jj, col, halves[hf])
            sd = jnp.concatenate(halves, axis=0)
            if reverse:
                causal = (row_s + r0) <= lane_s
                ref_row = b[r0 + SUB - 1:r0 + SUB, :]
                has_off = si < NS - 1
                off_rows = row_c >= r0 + SUB
            else:
                causal = (row_s + r0) >= lane_s
                ref_row = b[r0:r0 + 1, :]
                has_off = si > 0
                off_rows = row_c < r0
            s_i = jnp.where(causal, sd, 0.0)
            if has_off:
                qs = (q_i * jnp.exp2(b_i - ref_row)).astype(BF16)
                kk = jnp.where(off_rows, k * jnp.exp2(ref_row - b), 0.0).astype(BF16)
                s_i = s_i + _dot_nt(qs, kk)
            prows.append(s_i)
        p = jnp.concatenate(prows, axis=0).astype(BF16)
        o = o + _dot(p, v)
        if reverse:
            tot = of_ref[pl.ds(c0, C), :] + o
            o_ref[pl.ds(c0, C), :] = _finish_heads(tot, gn_ref[...], g_ref[pl.ds(c0, C), :])
        else:
            o_ref[pl.ds(c0, C), :] = o
        return carry

    lax.fori_loop(0, nchunks, chunk, 0, unroll=GLA_UNROLL)


def _gla_scan(proj, bcum, batch, seq_len, reverse, o_fwd=None, gn_w=None):
    T = proj.shape[0]
    tb = GLA_TB
    nb = seq_len // tb
    dk, dv = GLA_DK, GLA_DV

    def rb(b, n):
        return b * nb + ((nb - 1 - n) if reverse else n)

    in_specs = [
        pl.BlockSpec((tb, dk), lambda b, h, n: (rb(b, n), _GQ // dk + h)),
        pl.BlockSpec((tb, dk), lambda b, h, n: (rb(b, n), _GK // dk + h)),
        pl.BlockSpec((tb, dv), lambda b, h, n: (rb(b, n), _GV // dv + h)),
        pl.BlockSpec((tb, dk), lambda b, h, n: (rb(b, n), h)),
    ]
    args = [proj, proj, proj, bcum]
    if reverse:
        in_specs += [
            pl.BlockSpec((tb, dv), lambda b, h, n: (rb(b, n), _GG // dv + h)),
            pl.BlockSpec((tb, dv), lambda b, h, n: (rb(b, n), h)),
            pl.BlockSpec((1, dv), lambda b, h, n: (0, h)),
        ]
        args += [proj, o_fwd, gn_w]
    out_dtype = BF16 if reverse else F32
    return pl.pallas_call(
        functools.partial(_gla_kernel, reverse=reverse),
        grid=(batch, GLA_HEADS, nb),
        in_specs=in_specs,
        out_specs=pl.BlockSpec((tb, dv), lambda b, h, n: (rb(b, n), h)),
        out_shape=jax.ShapeDtypeStruct((T, GLA_WIDTH), out_dtype),
        scratch_shapes=[pltpu.VMEM((dv, dk), F32)],
        compiler_params=_cparams(("parallel", "parallel", "arbitrary"), 32),
        name="gla_bwd" if reverse else "gla_fwd",
    )(*args)


OP_TM = 512
OP_SUB = 256


def _out_proj_kernel(mr_ref, mg_ref, w0_ref, w1_ref, x_ref, n2_ref, rh_ref, rl_ref, h_ref, xn_ref, aff_ref):
    for r in range(OP_TM // OP_SUB):
        rows = slice(r * OP_SUB, (r + 1) * OP_SUB)
        h = x_ref[rows, :] + _dot(mr_ref[rows, :], w0_ref[...]) + _dot(mg_ref[rows, :], w1_ref[...])
        h_ref[rows, :] = h
        ms = jnp.mean(h * h, axis=-1, keepdims=True)
        xn = h * lax.rsqrt(ms + EPS) * n2_ref[...]
        xh = xn.astype(BF16)
        xn_ref[rows, :] = _pack_bf16_pairs(xn)
        xl = (xn - xh.astype(F32)).astype(BF16)
        lt = _dot_nt(rh_ref[...], xh) + _dot_nt(rh_ref[...], xl) + _dot_nt(rl_ref[...], xh)
        m = jnp.max(lt, axis=0, keepdims=True)
        e = jnp.exp(lt - m)
        aff_ref[:, rows] = e / jnp.sum(e, axis=0, keepdims=True)


def _out_proj(mix_r, mix_g, w_out, x2d, n2w, r_hi, r_lo):
    T = x2d.shape[0]
    tm = OP_TM
    half = RET_WIDTH
    return pl.pallas_call(
        _out_proj_kernel,
        grid=(T // tm,),
        in_specs=[
            pl.BlockSpec((tm, half), lambda i: (i, 0)),
            pl.BlockSpec((tm, half), lambda i: (i, 0)),
            pl.BlockSpec((half, D_MODEL), lambda i: (0, 0)),
            pl.BlockSpec((half, D_MODEL), lambda i: (1, 0)),
            pl.BlockSpec((tm, D_MODEL), lambda i: (i, 0)),
            pl.BlockSpec((1, D_MODEL), lambda i: (0, 0)),
            pl.BlockSpec((N_EXPERTS, D_MODEL), lambda i: (0, 0)),
            pl.BlockSpec((N_EXPERTS, D_MODEL), lambda i: (0, 0)),
        ],
        out_specs=[
            pl.BlockSpec((tm, D_MODEL), lambda i: (i, 0)),
            pl.BlockSpec((tm, D_MODEL // 2), lambda i: (i, 0)),
            pl.BlockSpec((N_EXPERTS, tm), lambda i: (0, i)),
        ],
        out_shape=[
            jax.ShapeDtypeStruct((T, D_MODEL), F32),
            jax.ShapeDtypeStruct((T, D_MODEL // 2), I32),
            jax.ShapeDtypeStruct((N_EXPERTS, T), F32),
        ],
        compiler_params=_cparams(("parallel",), 56),
        name="out_proj",
    )(mix_r, mix_g, w_out, w_out, x2d, n2w, r_hi, r_lo)


def _select_kernel(a_ref, pos_ref, posb_ref, rankb_ref, affb_ref, *, cap):
    E, T = a_ref.shape
    tt = MOE_TT

    def count(pred):
        return jnp.sum(pred.astype(F32), axis=1, keepdims=True)

    def bisect(i, tau):
        cand = tau | jnp.left_shift(jnp.int32(1), 30 - i)
        bits = pltpu.bitcast(a_ref[...], I32)
        return jnp.where(count(bits >= cand) >= cap, cand, tau)

    tau = lax.fori_loop(0, 31, bisect, jnp.zeros((E, 1), I32))
    bits_all = pltpu.bitcast(a_ref[...], I32)
    quota = cap - count(bits_all > tau)

    before = (lax.broadcasted_iota(I32, (tt, tt), 0) < lax.broadcasted_iota(I32, (tt, tt), 1)).astype(BF16)
    below = (lax.broadcasted_iota(I32, (E, E), 1) < lax.broadcasted_iota(I32, (E, E), 0)).astype(BF16)

    def block(j, carry):
        c_eq, c_sel = carry
        off = pl.multiple_of(j * tt, tt)
        aff = a_ref[:, pl.ds(off, tt)]
        bits = pltpu.bitcast(aff, I32)
        eq = bits == tau
        eqf = eq.astype(F32)
        rank_eq = _dot(eqf.astype(BF16), before) + c_eq
        sel = (bits > tau) | (eq & (rank_eq < quota))
        self_ = sel.astype(F32)
        selb = self_.astype(BF16)
        slot = _dot(selb, before) + c_sel
        pos = jnp.where(sel, slot, -1.0).astype(I32)
        pos_ref[:, pl.ds(off, tt)] = pos
        per_tok = jnp.broadcast_to(jnp.sum(self_, axis=0, keepdims=True), (E, tt))
        rank = _dot(per_tok.astype(BF16), before) + _dot(below, selb)
        posb_ref[j] = pos
        rankb_ref[j] = jnp.where(sel, rank, -1.0).astype(I32)
        affb_ref[j] = aff
        return (c_eq + jnp.sum(eqf, axis=1, keepdims=True), c_sel + jnp.sum(self_, axis=1, keepdims=True))

    zero = jnp.zeros((E, 1), F32)
    lax.fori_loop(0, T // tt, block, (zero, zero))


def _select(aff, cap):
    E, T = aff.shape
    nb = T // MOE_TT
    blk = jax.ShapeDtypeStruct((nb, E, MOE_TT), I32)
    return pl.pallas_call(
        functools.partial(_select_kernel, cap=cap),
        out_shape=[jax.ShapeDtypeStruct((E, T), I32), blk, blk, jax.ShapeDtypeStruct((nb, E, MOE_TT), F32)],
        compiler_params=pltpu.CompilerParams(vmem_limit_bytes=40 * 1024 * 1024),
        name="select",
    )(aff)


def _regroup_rows(T):
    nb = T // MOE_TT
    rows = CAPACITY_FACTOR * T + SC_GATHER_ROWS * nb
    return -(-rows // MOE_TW) * MOE_TW


def _combine_schedule(posb, T):
    nb = posb.shape[0]
    tw, g = MOE_TW, SC_GATHER_ROWS
    n = jnp.sum((posb >= 0).reshape(nb, -1), axis=1).astype(I32)
    seg = (n + g - 1) // g * g
    hi = jnp.cumsum(seg)
    lo = hi - seg
    off = jnp.concatenate([jnp.zeros((1,), I32), hi])
    nwin_max = (N_EXPERTS * MOE_TT) // tw + 1
    w0 = lo // tw
    w1 = jnp.where(seg > 0, (hi - 1) // tw, w0)
    cand = jnp.arange(nwin_max, dtype=I32)
    win = w0[:, None] + cand[None, :]
    valid = (win <= w1[:, None]).reshape(-1)
    nwin_total = _regroup_rows(T) // tw
    pmax = nb + nwin_total
    jv = jnp.broadcast_to(jnp.arange(nb, dtype=I32)[:, None], win.shape).reshape(-1)
    wv = jnp.minimum(win, nwin_total - 1).reshape(-1)
    total = jnp.sum(valid.astype(I32))
    dst = jnp.where(valid, jnp.cumsum(valid.astype(I32)) - 1, pmax)
    pj, pw = (jnp.zeros((pmax,), I32).at[dst].set(a, mode="drop") for a in (jv, wv))
    real = jnp.arange(pmax, dtype=I32) < total
    pj, pw = (jnp.where(real, a, a[total - 1]) for a in (pj, pw))
    first = jnp.concatenate([jnp.ones((1,), bool), pj[1:] != pj[:-1]])
    last = jnp.concatenate([pj[1:] != pj[:-1], jnp.ones((1,), bool)]) | (jnp.arange(pmax, dtype=I32) == total - 1)
    flag = jnp.where(real, first.astype(I32) + 2 * last.astype(I32) + 4, 0)
    return off, (pj, pw, flag, lo[pj], hi[pj])


SC_LANES = 16
SC_CORES = 2
SC_SUBCORES = 16
SC_GATHER_ROWS = 32


def _dispatch(pos, xw, cap):
    E, T = pos.shape
    W = xw.shape[1]
    G = SC_GATHER_ROWS
    part_rows = cap // SC_CORES
    mesh = plsc.VectorSubcoreMesh(core_axis_name="c", subcore_axis_name="s")

    @pl.kernel(
        out_type=jax.ShapeDtypeStruct((E * cap, W), I32),
        mesh=mesh,
        scratch_types=[pltpu.VMEM((T,), I32), pltpu.VMEM((cap,), I32), pltpu.VMEM((G, W), I32)],
        compiler_params=pltpu.CompilerParams(needs_layout_passes=False),
        name="sc_dispatch",
    )
    def run(pos_hbm, x_hbm, xe_hbm, pos_v, idx_v, buf):
        e = lax.axis_index("s")
        part = lax.axis_index("c")
        pltpu.sync_copy(pos_hbm.at[e], pos_v)
        lane = lax.iota(I32, SC_LANES)

        @pl.loop(0, T // SC_LANES)
        def _(i):
            off = pl.multiple_of(i * SC_LANES, SC_LANES)
            p = pos_v[pl.ds(off, SC_LANES)]
            plsc.store_scatter(idx_v, [p], lane + off, mask=p >= 0)

        @pl.loop(0, part_rows // G)
        def _(g):
            o = pl.multiple_of(part * part_rows + g * G, G)
            pltpu.sync_copy(x_hbm.at[idx_v.at[pl.ds(o, G)]], buf)
            pltpu.sync_copy(buf, xe_hbm.at[pl.ds(e * cap + o, G)])

    return run(pos, xw)


FFN_TM = 2048
FFN_TF = 256
FFN_UNPACK_ROWS = 256


FFN_TN = 256
FFN_UP_STEPS = D_FF // FFN_TF
FFN_DOWN_STEPS = D_MODEL // FFN_TN


def _ffn_kernel(x_ref, wg_ref, wu_ref, wd_ref, o_ref, xb_ref, hid_ref):
    s = pl.program_id(2)

    @pl.when(s == 0)
    def _():
        half = D_MODEL // 2

        def unpack_rows(i, carry):
            r = pl.multiple_of(i * FFN_UNPACK_ROWS, FFN_UNPACK_ROWS)
            lo, hi = _unpack_bf16_pairs(x_ref[pl.ds(r, FFN_UNPACK_ROWS), :])
            xb_ref[pl.ds(r, FFN_UNPACK_ROWS), :half] = lo.astype(BF16)
            xb_ref[pl.ds(r, FFN_UNPACK_ROWS), half:] = hi.astype(BF16)
            return carry

        lax.fori_loop(0, x_ref.shape[0] // FFN_UNPACK_ROWS, unpack_rows, 0)

    @pl.when(s < FFN_UP_STEPS)
    def _():
        x = xb_ref[...]
        g = _dot(x, wg_ref[...].astype(BF16))
        u = _dot(x, wu_ref[...].astype(BF16))
        col = pl.multiple_of(s * FFN_TF, FFN_TF)
        hid_ref[:, pl.ds(col, FFN_TF)] = (_silu(g) * u).astype(BF16)

    @pl.when(s >= FFN_UP_STEPS)
    def _():
        o_ref[...] = _pack_bf16_pairs(_dot(hid_ref[...], wd_ref[...].astype(BF16)))


def _ffn(xe, w_gate, w_up, w_down):
    E, cap, _ = xe.shape
    tm, tf, tn = min(FFN_TM, cap), FFN_TF, FFN_TN
    n_up, n_down = FFN_UP_STEPS, FFN_DOWN_STEPS

    def up(e, m, s):
        return (e, 0, jnp.minimum(s, n_up - 1))

    def down(s):
        return jnp.maximum(s - n_up, 0)

    return pl.pallas_call(
        _ffn_kernel,
        grid=(E, cap // tm, n_up + n_down),
        in_specs=[
            pl.BlockSpec((None, tm, D_MODEL // 2), lambda e, m, s: (e, m, 0)),
            pl.BlockSpec((None, D_MODEL, tf), up),
            pl.BlockSpec((None, D_MODEL, tf), up),
            pl.BlockSpec((None, D_FF, tn), lambda e, m, s: (e, 0, down(s))),
        ],
        out_specs=pl.BlockSpec((None, tm, tn // 2), lambda e, m, s: (e, m, down(s))),
        out_shape=jax.ShapeDtypeStruct((E, cap, D_MODEL // 2), I32),
        scratch_shapes=[pltpu.VMEM((tm, D_MODEL), BF16), pltpu.VMEM((tm, D_FF), BF16)],
        compiler_params=_cparams(("parallel", "parallel", "arbitrary"), 60),
        name="ffn",
    )(xe, w_gate, w_up, w_down)


def _regroup(posb, rankb, affb, off, yw, cap, rows):
    NB, EB = posb.shape
    W = yw.shape[1]
    G, L, tt = SC_GATHER_ROWS, SC_LANES, MOE_TT
    n_workers = SC_CORES * SC_SUBCORES
    per = NB // n_workers
    mesh = plsc.VectorSubcoreMesh(core_axis_name="c", subcore_axis_name="s")
    off_pad = jnp.pad(off, (0, L))

    @pl.kernel(
        out_type=(jax.ShapeDtypeStruct((rows, W), I32), jax.ShapeDtypeStruct((rows,), I32),
                  jax.ShapeDtypeStruct((rows,), F32)),
        mesh=mesh,
        scratch_types=[pltpu.VMEM((EB,), I32), pltpu.VMEM((EB,), I32), pltpu.VMEM((EB,), F32),
                       pltpu.VMEM((EB,), I32), pltpu.VMEM((EB,), I32), pltpu.VMEM((EB,), F32),
                       pltpu.VMEM((G, W), I32), pltpu.VMEM((NB + 1 + L,), I32)],
        compiler_params=pltpu.CompilerParams(needs_layout_passes=False),
        name="sc_regroup",
    )
    def run(posb_hbm, rankb_hbm, affb_hbm, off_hbm, y_hbm, yg_hbm, tok_hbm, gate_hbm,
            pos_v, rank_v, aff_v, src_v, tok_v, gate_v, buf, off_v):
        wid = lax.axis_index("c") * SC_SUBCORES + lax.axis_index("s")
        pltpu.sync_copy(off_hbm, off_v)
        lane = lax.iota(I32, L)
        zi = jnp.zeros((L,), I32)
        zf = jnp.zeros((L,), F32)

        @pl.loop(0, per)
        def _(k):
            j = wid * per + k
            pltpu.sync_copy(posb_hbm.at[j], pos_v)
            pltpu.sync_copy(rankb_hbm.at[j], rank_v)
            pltpu.sync_copy(affb_hbm.at[j], aff_v)
            lo = jnp.max(plsc.load_gather(off_v, [zi + j]))
            hi = jnp.max(plsc.load_gather(off_v, [zi + j + 1]))

            @pl.loop(0, EB // L)
            def _(i):
                o = pl.multiple_of(i * L, L)
                src_v[pl.ds(o, L)] = zi
                tok_v[pl.ds(o, L)] = zi
                gate_v[pl.ds(o, L)] = zf

            @pl.loop(0, EB // L)
            def _(i):
                o = pl.multiple_of(i * L, L)
                p = pos_v[pl.ds(o, L)]
                r = rank_v[pl.ds(o, L)]
                m = p >= 0
                e = i // (tt // L)
                t0 = j * tt + (i % (tt // L)) * L
                plsc.store_scatter(src_v, [r], p + e * cap, mask=m)
                plsc.store_scatter(tok_v, [r], lane + t0, mask=m)
                plsc.store_scatter(gate_v, [r], aff_v[pl.ds(o, L)], mask=m)

            @pl.loop(0, (hi - lo) // G)
            def _(g):
                o = pl.multiple_of(g * G, G)
                dst = pl.multiple_of(lo + o, G)
                pltpu.sync_copy(y_hbm.at[src_v.at[pl.ds(o, G)]], buf)
                pltpu.sync_copy(buf, yg_hbm.at[pl.ds(dst, G)])
                pltpu.sync_copy(tok_v.at[pl.ds(o, G)], tok_hbm.at[pl.ds(dst, G)])
                pltpu.sync_copy(gate_v.at[pl.ds(o, G)], gate_hbm.at[pl.ds(dst, G)])

    return run(posb, rankb, affb, off_pad, yw)


def _combine_kernel(pj_ref, pw_ref, pf_ref, plo_ref, phi_ref, tok_ref, gate_ref, yg_ref, h_ref, nf_ref,
                    o_ref, acc_ref):
    p = pl.program_id(0)
    flag = pf_ref[p]
    tw, tt = MOE_TW, MOE_TT
    half = D_MODEL // 2

    hw = FFN_TN // 2

    def col_blocks():
        for n in range(FFN_DOWN_STEPS):
            yield slice(n * hw, (n + 1) * hw), slice(n * FFN_TN, n * FFN_TN + hw)
            yield slice(half + n * hw, half + (n + 1) * hw), slice(n * FFN_TN + hw, (n + 1) * FFN_TN)

    @pl.when((flag & 1) != 0)
    def _():
        for packed, natural in col_blocks():
            acc_ref[:, packed] = h_ref[:, natural]

    @pl.when((flag & 4) != 0)
    def _():
        lo, hi = plo_ref[p], phi_ref[p]
        row0 = pw_ref[p] * tw
        rid = lax.broadcasted_iota(I32, (tw, 1), 0) + row0
        keep = (rid >= lo) & (rid < hi)
        y_lo, y_hi = _unpack_bf16_pairs(yg_ref[...])
        y_lo = jnp.where(keep, y_lo, 0.0).astype(BF16)
        y_hi = jnp.where(keep, y_hi, 0.0).astype(BF16)
        tid = lax.broadcasted_iota(I32, (tt, tw), 0) + pj_ref[p] * tt
        cid = lax.broadcasted_iota(I32, (tt, tw), 1) + row0
        hit = (tok_ref[...] == tid) & (cid >= lo) & (cid < hi)
        weights = jnp.where(hit, gate_ref[...], 0.0).astype(BF16)
        acc_ref[:, :half] += _dot(weights, y_lo)
        acc_ref[:, half:] += _dot(weights, y_hi)

    @pl.when((flag & 2) != 0)
    def _():
        y = acc_ref[...]
        scale = lax.rsqrt(jnp.mean(y * y, axis=-1, keepdims=True) + EPS)
        for packed, natural in col_blocks():
            o_ref[:, natural] = acc_ref[:, packed] * scale * nf_ref[:, natural]


def _combine(lists, tok, gate, yg, h, nfw):
    pj, pw, pf, plo, phi = lists
    T = h.shape[0]
    tw, tt = MOE_TW, MOE_TT
    nwin = yg.shape[0] // tw
    grid_spec = pltpu.PrefetchScalarGridSpec(
        num_scalar_prefetch=5,
        grid=(pj.shape[0],),
        in_specs=[
            pl.BlockSpec((None, 1, tw), lambda p, pj, pw, *_: (pw[p], 0, 0)),
            pl.BlockSpec((None, 1, tw), lambda p, pj, pw, *_: (pw[p], 0, 0)),
            pl.BlockSpec((tw, D_MODEL // 2), lambda p, pj, pw, *_: (pw[p], 0)),
            pl.BlockSpec((tt, D_MODEL), lambda p, pj, pw, *_: (pj[p], 0)),
            pl.BlockSpec((1, D_MODEL), lambda p, pj, pw, *_: (0, 0)),
        ],
        out_specs=pl.BlockSpec((tt, D_MODEL), lambda p, pj, pw, *_: (pj[p], 0)),
        scratch_shapes=[pltpu.VMEM((tt, D_MODEL), F32)],
    )
    return pl.pallas_call(
        _combine_kernel,
        grid_spec=grid_spec,
        out_shape=jax.ShapeDtypeStruct((T, D_MODEL), F32),
        compiler_params=_cparams(("arbitrary",), 32),
        name="combine",
    )(pj, pw, pf, plo, phi, tok.reshape(nwin, 1, tw), gate.reshape(nwin, 1, tw), yg, h, nfw)


def _rope_tables(seq_len):
    d = RET_DK
    inv = ROPE_BASE ** (-jnp.arange(0, d, 2, dtype=F32) / d)
    ang = jnp.arange(seq_len, dtype=F32)[:, None] * inv[None, :]
    return jnp.cos(ang), jnp.sin(ang)


def _chunk_tri(n, chunk, upper):
    r = np.arange(n)
    same = (r[:, None] // chunk) == (r[None, :] // chunk)
    tri = (r[:, None] <= r[None, :]) if upper else (r[:, None] >= r[None, :])
    return jnp.asarray(same & tri, BF16)


def _prep_params(norm1_w, w_in, ret_gn_w, gla_gate_up, gla_gate_bias, gla_gn_w, w_out, norm2_w, router_w,
                 normf_w):
    w = w_in[0]
    w_main = w[:, :IN_MAIN].astype(BF16)
    w_ga = jnp.pad(w[:, IN_MAIN:], ((0, 0), (0, LANE - 2 * GLA_RANK))).astype(BF16)
    cs = np.ones((1, IN_MAIN), np.float32)
    cs[:, _RQ:_RQ + RET_WIDTH] = RET_DK ** -0.5
    cs[:, _GQ:_GQ + GLA_KEY_WIDTH] = GLA_DK ** -0.5
    up = gla_gate_up[0].astype(F32)
    up_pad = jnp.zeros((LANE, 2 * GLA_KEY_WIDTH), F32)
    up_pad = up_pad.at[:GLA_RANK, :GLA_KEY_WIDTH].set(up[0])
    up_pad = up_pad.at[GLA_RANK:2 * GLA_RANK, GLA_KEY_WIDTH:].set(up[1])
    rt = router_w[0].T.astype(F32)
    r_hi = rt.astype(BF16)
    r_lo = (rt - r_hi.astype(F32)).astype(BF16)
    return dict(
        n1w=norm1_w[0].reshape(1, D_MODEL).astype(F32),
        w_main=w_main, w_ga=w_ga, colscale=jnp.asarray(cs),
        up_pad=up_pad.astype(BF16),
        bias=gla_gate_bias[0].reshape(1, 2 * GLA_KEY_WIDTH).astype(F32),
        lf=_chunk_tri(GATE_TM, GLA_CHUNK, upper=False),
        lb=_chunk_tri(GATE_TM, GLA_CHUNK, upper=True),
        ret_gn=ret_gn_w[0].reshape(1, RET_WIDTH).astype(F32),
        gla_gn=gla_gn_w[0].reshape(1, GLA_WIDTH).astype(F32),
        w_out=w_out[0].astype(BF16),
        n2w=norm2_w[0].reshape(1, D_MODEL).astype(F32),
        r_hi=r_hi, r_lo=r_lo,
        nfw=normf_w.reshape(1, D_MODEL).astype(F32),
    )


def _trunk(x, pp, decay_logit, w_gate, w_up, w_down):
    B, L, _ = x.shape
    T = B * L
    x2d = x.reshape(T, D_MODEL)
    cos, sin = _rope_tables(L)
    proj, ga = _in_proj(x2d, pp["n1w"], pp["w_main"], pp["w_ga"], pp["colscale"], cos, sin, L)
    b_f, b_b = _gla_gates(ga, pp["up_pad"], pp["bias"], pp["lf"], pp["lb"])

    ret_f = _ret_scan(proj, decay_logit, B, L, reverse=False)
    mix_r = _ret_scan(proj, decay_logit, B, L, reverse=True, o_fwd=ret_f, gn_w=pp["ret_gn"])
    gla_f = _gla_scan(proj, b_f, B, L, reverse=False)
    mix_g = _gla_scan(proj, b_b, B, L, reverse=True, o_fwd=gla_f, gn_w=pp["gla_gn"])

    h, xn2, aff = _out_proj(mix_r, mix_g, pp["w_out"], x2d, pp["n2w"], pp["r_hi"], pp["r_lo"])

    cap = CAPACITY_FACTOR * T // N_EXPERTS
    pos, posb, rankb, affb = _select(aff, cap)
    off, c_lists = _combine_schedule(posb, T)
    xe = _dispatch(pos, xn2, cap).reshape(N_EXPERTS, cap, D_MODEL // 2)
    ye = _ffn(xe, w_gate, w_up, w_down).reshape(N_EXPERTS * cap, D_MODEL // 2)
    nb = T // MOE_TT
    yg, tok, gate = _regroup(posb.reshape(nb, -1), rankb.reshape(nb, -1), affb.reshape(nb, -1), off, ye, cap,
                             _regroup_rows(T))
    y = _combine(c_lists, tok, gate, yg, h, pp["nfw"])
    return y.reshape(B, L, D_MODEL)


def kernel(x_prompt, x_sample, norm1_w, w_in, ret_decay_logit, ret_gn_w, gla_gate_up, gla_gate_bias,
           gla_gn_w, w_out, norm2_w, router_w, w_gate, w_up, w_down, normf_w):
    pp = _prep_params(norm1_w, w_in, ret_gn_w, gla_gate_up, gla_gate_bias, gla_gn_w, w_out, norm2_w,
                      router_w, normf_w)
    decay_logit = ret_decay_logit[0].astype(F32)
    args = (pp, decay_logit, w_gate[0], w_up[0], w_down[0])
    return (_trunk(x_prompt, *args), _trunk(x_sample, *args))
```

```python
import functools

import numpy as np
import jax
import jax.numpy as jnp
from jax import lax
from jax.experimental import pallas as pl
from jax.experimental.pallas import tpu as pltpu
from jax.experimental.pallas import tpu_sc as plsc

F32, BF16, I32 = jnp.float32, jnp.bfloat16, jnp.int32

D_MODEL = 2048
RET_WIDTH = 1024
RET_HEADS = 4
RET_DK = 256
RET_DV = 256
GLA_WIDTH = 1024
GLA_HEADS = 4
GLA_DK = 128
GLA_DV = 256
GLA_KEY_WIDTH = 512
GLA_RANK = 16
GLA_TAU = 16.0
RET_CHUNK = 256
GLA_CHUNK = 64
GLA_SUB = 16
ROPE_BASE = 10000.0
N_EXPERTS = 16
CAPACITY_FACTOR = 2
D_FF = 2048
EPS = 1e-6
LOG2_E = 1.4426950408889634
IN_MAIN = 4 * RET_WIDTH + 2 * GLA_KEY_WIDTH + 2 * GLA_WIDTH

_RQ, _RK, _RV, _RG = 0, 1024, 2048, 3072
_GQ, _GK, _GV, _GG = 4096, 4608, 5120, 6144

LANE = 128
MOE_TT = 256
MOE_TW = 256
V7X_VMEM_BYTES = 64 * 1024 * 1024


def _cparams(sem, vmem_mb):
    return pltpu.CompilerParams(dimension_semantics=sem, vmem_limit_bytes=vmem_mb * 1024 * 1024)


def _log_sigmoid(z):
    return jnp.minimum(z, 0.0) - jnp.log1p(jnp.exp(-jnp.abs(z)))


def _silu(g):
    return g * (1.0 / (1.0 + jnp.exp(-g)))


def _dot_nt(a, b):
    return lax.dot_general(a, b, (((1,), (1,)), ((), ())), preferred_element_type=F32)


def _dot_tn(a, b):
    return lax.dot_general(a, b, (((0,), (0,)), ((), ())), preferred_element_type=F32)


def _dot(a, b):
    return jnp.dot(a, b, preferred_element_type=F32)


def _pack_bf16_pairs(x):
    bits = pltpu.bitcast(x.astype(BF16).astype(F32), I32)
    w = x.shape[1] // 2
    return bits[:, w:] | lax.shift_right_logical(bits[:, :w], 16)


def _unpack_bf16_pairs(words):
    lo = pltpu.bitcast(lax.shift_left(words, 16), F32)
    hi = pltpu.bitcast(words & jnp.int32(-65536), F32)
    return lo, hi


IP_TM = 1024
IP_TN = 1024


def _in_proj_kernel(x_ref, n1_ref, w_ref, wga_ref, cs_ref, cos_ref, sin_ref, o_ref, ga_ref, xn_ref):
    j = pl.program_id(1)

    @pl.when(j == 0)
    def _():
        x = x_ref[...]
        ms = jnp.mean(x * x, axis=-1, keepdims=True)
        xn = (x * lax.rsqrt(ms + EPS) * n1_ref[...]).astype(BF16)
        xn_ref[...] = xn
        ga_ref[...] = _dot(xn, wga_ref[...])

    acc = _dot(xn_ref[...], w_ref[...]) * cs_ref[...]
    n_rope_blocks = 2 * RET_WIDTH // IP_TN

    @pl.when(j < n_rope_blocks)
    def _():
        cos = cos_ref[...]
        sin = sin_ref[...]
        half = RET_DK // 2
        for h in range(IP_TN // RET_DK):
            lo = h * RET_DK
            x1 = acc[:, lo:lo + half]
            x2 = acc[:, lo + half:lo + RET_DK]
            o_ref[:, lo:lo + half] = (x1 * cos - x2 * sin).astype(BF16)
            o_ref[:, lo + half:lo + RET_DK] = (x1 * sin + x2 * cos).astype(BF16)

    @pl.when(j >= n_rope_blocks)
    def _():
        o_ref[...] = acc.astype(BF16)


def _in_proj(x2d, n1w, w_main, w_ga, colscale, cos, sin, seq_len):
    T = x2d.shape[0]
    tm, tn = IP_TM, IP_TN
    nlb = seq_len // tm
    return pl.pallas_call(
        _in_proj_kernel,
        grid=(T // tm, IN_MAIN // tn),
        in_specs=[
            pl.BlockSpec((tm, D_MODEL), lambda i, j: (i, 0)),
            pl.BlockSpec((1, D_MODEL), lambda i, j: (0, 0)),
            pl.BlockSpec((None, D_MODEL, tn), lambda i, j: (j, 0, 0)),
            pl.BlockSpec((D_MODEL, LANE), lambda i, j: (0, 0)),
            pl.BlockSpec((1, tn), lambda i, j: (0, j)),
            pl.BlockSpec((tm, LANE), lambda i, j: (i % nlb, 0)),
            pl.BlockSpec((tm, LANE), lambda i, j: (i % nlb, 0)),
        ],
        out_specs=[
            pl.BlockSpec((tm, tn), lambda i, j: (i, j)),
            pl.BlockSpec((tm, LANE), lambda i, j: (i, 0)),
        ],
        out_shape=[
            jax.ShapeDtypeStruct((T, IN_MAIN), BF16),
            jax.ShapeDtypeStruct((T, LANE), F32),
        ],
        scratch_shapes=[pltpu.VMEM((tm, D_MODEL), BF16)],
        compiler_params=_cparams(("parallel", "arbitrary"), 48),
        name="in_proj",
    )(x2d, n1w, w_main, w_ga, colscale, cos, sin)


GATE_TM = 256


def _gates_kernel(ga_ref, up_ref, bias_ref, lf_ref, lb_ref, bf_ref, bb_ref):
    z = _dot(ga_ref[...].astype(BF16), up_ref[...]) + bias_ref[...]
    la = _log_sigmoid(z) * (LOG2_E / GLA_TAU)
    hi = la.astype(BF16)
    lo = (la - hi.astype(F32)).astype(BF16)
    kw = GLA_KEY_WIDTH
    bf_ref[...] = _dot(lf_ref[...], hi[:, :kw]) + _dot(lf_ref[...], lo[:, :kw])
    bb_ref[...] = _dot(lb_ref[...], hi[:, kw:]) + _dot(lb_ref[...], lo[:, kw:])


def _gla_gates(ga, up_pad, bias, lf, lb):
    T = ga.shape[0]
    tm = GATE_TM
    kw = GLA_KEY_WIDTH
    return pl.pallas_call(
        _gates_kernel,
        grid=(T // tm,),
        in_specs=[
            pl.BlockSpec((tm, LANE), lambda i: (i, 0)),
            pl.BlockSpec((LANE, 2 * kw), lambda i: (0, 0)),
            pl.BlockSpec((1, 2 * kw), lambda i: (0, 0)),
            pl.BlockSpec((tm, tm), lambda i: (0, 0)),
            pl.BlockSpec((tm, tm), lambda i: (0, 0)),
        ],
        out_specs=[pl.BlockSpec((tm, kw), lambda i: (i, 0)), pl.BlockSpec((tm, kw), lambda i: (i, 0))],
        out_shape=[jax.ShapeDtypeStruct((T, kw), F32), jax.ShapeDtypeStruct((T, kw), F32)],
        compiler_params=_cparams(("parallel",), 32),
        name="gla_gates",
    )(ga, up_pad, bias, lf, lb)


def _finish_heads(tot, gn, gate):
    ms = jnp.mean(tot * tot, axis=-1, keepdims=True)
    yn = tot * lax.rsqrt(ms + EPS) * gn
    return (yn * _silu(gate.astype(F32))).astype(BF16)


RET_TB = 1024


def _ret_kernel(dl_ref, q_ref, k_ref, v_ref, *rest, reverse):
    if reverse:
        g_ref, of_ref, gn_ref, o_ref, s_ref, intra_ref, qd_ref, kd_ref, cd_ref = rest
    else:
        o_ref, s_ref, intra_ref, qd_ref, kd_ref, cd_ref = rest
    h = pl.program_id(1)
    n = pl.program_id(2)
    C = RET_CHUNK

    @pl.when(n == 0)
    def _():
        s_ref[...] = jnp.zeros_like(s_ref)
        logit = dl_ref[1 if reverse else 0, h]
        lg = _log_sigmoid(jnp.full((C, RET_DV), logit, F32))
        lg_c = _log_sigmoid(jnp.full((C, C), logit, F32))
        lg_r = _log_sigmoid(jnp.full((1, RET_DV), logit, F32))
        ri = lax.broadcasted_iota(I32, (C, RET_DV), 0).astype(F32)
        rc = lax.broadcasted_iota(I32, (C, C), 0).astype(F32)
        cc = lax.broadcasted_iota(I32, (C, C), 1).astype(F32)
        diff = (cc - rc) if reverse else (rc - cc)
        intra_ref[...] = jnp.where(diff >= 0, jnp.exp(lg_c * diff), 0.0)
        if reverse:
            qd_ref[...] = jnp.exp(lg * (C - ri))
            kd_ref[...] = jnp.exp(lg * ri)
        else:
            qd_ref[...] = jnp.exp(lg * (ri + 1.0))
            kd_ref[...] = jnp.exp(lg * (C - 1.0 - ri))
        cd_ref[...] = jnp.exp(lg_r * C)

    nchunks = q_ref.shape[0] // C
    order = range(nchunks - 1, -1, -1) if reverse else range(nchunks)
    for c in order:
        rows = slice(c * C, (c + 1) * C)
        q = q_ref[rows, :]
        k = k_ref[rows, :]
        v = v_ref[rows, :]
        s = _dot_nt(q, k) * intra_ref[...]
        state = s_ref[...]
        o = _dot(s.astype(BF16), v) + _dot(q, state.astype(BF16)) * qd_ref[...]
        kd = (k.astype(F32) * kd_ref[...]).astype(BF16)
        s_ref[...] = state * cd_ref[...] + _dot_tn(kd, v)
        if reverse:
            o_ref[rows, :] = _finish_heads(of_ref[rows, :] + o, gn_ref[...], g_ref[rows, :])
        else:
            o_ref[rows, :] = o


def _ret_scan(proj, decay_logit, batch, seq_len, reverse, o_fwd=None, gn_w=None):
    T = proj.shape[0]
    tb = RET_TB
    nb = seq_len // tb
    dk, dv, C = RET_DK, RET_DV, RET_CHUNK

    def rb(b, n):
        return b * nb + ((nb - 1 - n) if reverse else n)

    def col(base):
        return lambda b, h, n: (rb(b, n), base // dk + h)

    in_specs = [
        pl.BlockSpec(memory_space=pltpu.SMEM),
        pl.BlockSpec((tb, dk), col(_RQ)),
        pl.BlockSpec((tb, dk), col(_RK)),
        pl.BlockSpec((tb, dv), col(_RV)),
    ]
    args = [decay_logit, proj, proj, proj]
    if reverse:
        in_specs += [
            pl.BlockSpec((tb, dv), col(_RG)),
            pl.BlockSpec((tb, dv), lambda b, h, n: (rb(b, n), h)),
            pl.BlockSpec((1, dv), lambda b, h, n: (0, h)),
        ]
        args += [proj, o_fwd, gn_w]
    out_dtype = BF16 if reverse else F32
    return pl.pallas_call(
        functools.partial(_ret_kernel, reverse=reverse),
        grid=(batch, RET_HEADS, nb),
        in_specs=in_specs,
        out_specs=pl.BlockSpec((tb, dv), lambda b, h, n: (rb(b, n), h)),
        out_shape=jax.ShapeDtypeStruct((T, RET_WIDTH), out_dtype),
        scratch_shapes=[
            pltpu.VMEM((dk, dv), F32),
            pltpu.VMEM((C, C), F32),
            pltpu.VMEM((C, dv), F32),
            pltpu.VMEM((C, dk), F32),
            pltpu.VMEM((1, dv), F32),
        ],
        compiler_params=_cparams(("parallel", "parallel", "arbitrary"), 32),
        name="ret_bwd" if reverse else "ret_fwd",
    )(*args)


GLA_TB = 512
GLA_UNROLL = 8


def _gla_kernel(q_ref, k_ref, v_ref, b_ref, *rest, reverse):
    if reverse:
        g_ref, of_ref, gn_ref, o_ref, st_ref = rest
    else:
        o_ref, st_ref = rest
    n = pl.program_id(2)
    C, SUB = GLA_CHUNK, GLA_SUB
    NS = C // SUB

    @pl.when(n == 0)
    def _():
        st_ref[...] = jnp.zeros_like(st_ref)

    nchunks = q_ref.shape[0] // C
    row_c = lax.broadcasted_iota(I32, (C, GLA_DK), 0)
    lane_s = lax.broadcasted_iota(I32, (SUB, C), 1)
    lane_h = lax.broadcasted_iota(I32, (SUB // 2, C), 1)
    row_s = lax.broadcasted_iota(I32, (SUB, C), 0)

    def chunk(ci, carry):
        c = (nchunks - 1 - ci) if reverse else ci
        c0 = pl.multiple_of(c * C, C)
        q = q_ref[pl.ds(c0, C), :].astype(F32)
        k = k_ref[pl.ds(c0, C), :].astype(F32)
        v = v_ref[pl.ds(c0, C), :]
        b = b_ref[pl.ds(c0, C), :]
        b_end = b[0:1, :] if reverse else b[C - 1:C, :]

        st = st_ref[...]
        o = _dot_nt((q * jnp.exp2(b)).astype(BF16), st.astype(BF16))
        ke = (k * jnp.exp2(b_end - b)).astype(BF16)
        st_ref[...] = st * jnp.exp2(b_end) + _dot_tn(v, ke)

        prows = []
        for si in range(NS):
            r0 = si * SUB
            b_i = b[r0:r0 + SUB, :]
            q_i = q[r0:r0 + SUB, :]
            halves = [jnp.zeros((SUB // 2, C), F32), jnp.zeros((SUB // 2, C), F32)]
            for jj in range(SUB):
                b_j = b[r0 + jj:r0 + jj + 1, :]
                k_j = k[r0 + jj:r0 + jj + 1, :]
                for hf in range(2):
                    needed = (hf == 0 or jj >= SUB // 2) if reverse else (hf == 1 or jj < SUB // 2)
                    if not needed:
                        continue
                    rs = slice(hf * SUB // 2, (hf + 1) * SUB // 2)
                    w = jnp.exp2(b_i[rs] - b_j)
                    col = jnp.sum(q_i[rs] * k_j * w, axis=-1, keepdims=True)
                    halves[hf] = jnp.where(lane_h == r0 + jj, col, halves[hf])
            sd = jnp.concatenate(halves, axis=0)
            if reverse:
                causal = (row_s + r0) <= lane_s
                ref_row = b[r0 + SUB - 1:r0 + SUB, :]
                has_off = si < NS - 1
                off_rows = row_c >= r0 + SUB
            else:
                causal = (row_s + r0) >= lane_s
                ref_row = b[r0:r0 + 1, :]
                has_off = si > 0
                off_rows = row_c < r0
            s_i = jnp.where(causal, sd, 0.0)
            if has_off:
                qs = (q_i * jnp.exp2(b_i - ref_row)).astype(BF16)
                kk = jnp.where(off_rows, k * jnp.exp2(ref_row - b), 0.0).astype(BF16)
                s_i = s_i + _dot_nt(qs, kk)
            prows.append(s_i)
        p = jnp.concatenate(prows, axis=0).astype(BF16)
        o = o + _dot(p, v)
        if reverse:
            tot = of_ref[pl.ds(c0, C), :] + o
            o_ref[pl.ds(c0, C), :] = _finish_heads(tot, gn_ref[...], g_ref[pl.ds(c0, C), :])
        else:
            o_ref[pl.ds(c0, C), :] = o
        return carry

    lax.fori_loop(0, nchunks, chunk, 0, unroll=GLA_UNROLL)


def _gla_scan(proj, bcum, batch, seq_len, reverse, o_fwd=None, gn_w=None):
    T = proj.shape[0]
    tb = GLA_TB
    nb = seq_len // tb
    dk, dv = GLA_DK, GLA_DV

    def rb(b, n):
        return b * nb + ((nb - 1 - n) if reverse else n)

    in_specs = [
        pl.BlockSpec((tb, dk), lambda b, h, n: (rb(b, n), _GQ // dk + h)),
        pl.BlockSpec((tb, dk), lambda b, h, n: (rb(b, n), _GK // dk + h)),
        pl.BlockSpec((tb, dv), lambda b, h, n: (rb(b, n), _GV // dv + h)),
        pl.BlockSpec((tb, dk), lambda b, h, n: (rb(b, n), h)),
    ]
    args = [proj, proj, proj, bcum]
    if reverse:
        in_specs += [
            pl.BlockSpec((tb, dv), lambda b, h, n: (rb(b, n), _GG // dv + h)),
            pl.BlockSpec((tb, dv), lambda b, h, n: (rb(b, n), h)),
            pl.BlockSpec((1, dv), lambda b, h, n: (0, h)),
        ]
        args += [proj, o_fwd, gn_w]
    out_dtype = BF16 if reverse else F32
    return pl.pallas_call(
        functools.partial(_gla_kernel, reverse=reverse),
        grid=(batch, GLA_HEADS, nb),
        in_specs=in_specs,
        out_specs=pl.BlockSpec((tb, dv), lambda b, h, n: (rb(b, n), h)),
        out_shape=jax.ShapeDtypeStruct((T, GLA_WIDTH), out_dtype),
        scratch_shapes=[pltpu.VMEM((dv, dk), F32)],
        compiler_params=_cparams(("parallel", "parallel", "arbitrary"), 32),
        name="gla_bwd" if reverse else "gla_fwd",
    )(*args)


OP_TM = 512
OP_SUB = 256


def _out_proj_kernel(mr_ref, mg_ref, w0_ref, w1_ref, x_ref, n2_ref, rh_ref, rl_ref, h_ref, xn_ref, aff_ref):
    for r in range(OP_TM // OP_SUB):
        rows = slice(r * OP_SUB, (r + 1) * OP_SUB)
        h = x_ref[rows, :] + _dot(mr_ref[rows, :], w0_ref[...]) + _dot(mg_ref[rows, :], w1_ref[...])
        h_ref[rows, :] = h
        ms = jnp.mean(h * h, axis=-1, keepdims=True)
        xn = h * lax.rsqrt(ms + EPS) * n2_ref[...]
        xh = xn.astype(BF16)
        xn_ref[rows, :] = _pack_bf16_pairs(xn)
        xl = (xn - xh.astype(F32)).astype(BF16)
        lt = _dot_nt(rh_ref[...], xh) + _dot_nt(rh_ref[...], xl) + _dot_nt(rl_ref[...], xh)
        m = jnp.max(lt, axis=0, keepdims=True)
        e = jnp.exp(lt - m)
        aff_ref[:, rows] = e / jnp.sum(e, axis=0, keepdims=True)


def _out_proj(mix_r, mix_g, w_out, x2d, n2w, r_hi, r_lo):
    T = x2d.shape[0]
    tm = OP_TM
    half = RET_WIDTH
    return pl.pallas_call(
        _out_proj_kernel,
        grid=(T // tm,),
        in_specs=[
            pl.BlockSpec((tm, half), lambda i: (i, 0)),
            pl.BlockSpec((tm, half), lambda i: (i, 0)),
            pl.BlockSpec((half, D_MODEL), lambda i: (0, 0)),
            pl.BlockSpec((half, D_MODEL), lambda i: (1, 0)),
            pl.BlockSpec((tm, D_MODEL), lambda i: (i, 0)),
            pl.BlockSpec((1, D_MODEL), lambda i: (0, 0)),
            pl.BlockSpec((N_EXPERTS, D_MODEL), lambda i: (0, 0)),
            pl.BlockSpec((N_EXPERTS, D_MODEL), lambda i: (0, 0)),
        ],
        out_specs=[
            pl.BlockSpec((tm, D_MODEL), lambda i: (i, 0)),
            pl.BlockSpec((tm, D_MODEL // 2), lambda i: (i, 0)),
            pl.BlockSpec((N_EXPERTS, tm), lambda i: (0, i)),
        ],
        out_shape=[
            jax.ShapeDtypeStruct((T, D_MODEL), F32),
            jax.ShapeDtypeStruct((T, D_MODEL // 2), I32),
            jax.ShapeDtypeStruct((N_EXPERTS, T), F32),
        ],
        compiler_params=_cparams(("parallel",), 56),
        name="out_proj",
    )(mix_r, mix_g, w_out, w_out, x2d, n2w, r_hi, r_lo)


def _select_kernel(a_ref, pos_ref, posb_ref, rankb_ref, affb_ref, *, cap):
    E, T = a_ref.shape
    tt = MOE_TT

    def count(pred):
        return jnp.sum(pred.astype(F32), axis=1, keepdims=True)

    def bisect(i, tau):
        cand = tau | jnp.left_shift(jnp.int32(1), 30 - i)
        bits = pltpu.bitcast(a_ref[...], I32)
        return jnp.where(count(bits >= cand) >= cap, cand, tau)

    tau = lax.fori_loop(0, 31, bisect, jnp.zeros((E, 1), I32))
    bits_all = pltpu.bitcast(a_ref[...], I32)
    quota = cap - count(bits_all > tau)

    before = (lax.broadcasted_iota(I32, (tt, tt), 0) < lax.broadcasted_iota(I32, (tt, tt), 1)).astype(BF16)
    below = (lax.broadcasted_iota(I32, (E, E), 1) < lax.broadcasted_iota(I32, (E, E), 0)).astype(BF16)

    def block(j, carry):
        c_eq, c_sel = carry
        off = pl.multiple_of(j * tt, tt)
        aff = a_ref[:, pl.ds(off, tt)]
        bits = pltpu.bitcast(aff, I32)
        eq = bits == tau
        eqf = eq.astype(F32)
        rank_eq = _dot(eqf.astype(BF16), before) + c_eq
        sel = (bits > tau) | (eq & (rank_eq < quota))
        self_ = sel.astype(F32)
        selb = self_.astype(BF16)
        slot = _dot(selb, before) + c_sel
        pos = jnp.where(sel, slot, -1.0).astype(I32)
        pos_ref[:, pl.ds(off, tt)] = pos
        per_tok = jnp.broadcast_to(jnp.sum(self_, axis=0, keepdims=True), (E, tt))
        rank = _dot(per_tok.astype(BF16), before) + _dot(below, selb)
        posb_ref[j] = pos
        rankb_ref[j] = jnp.where(sel, rank, -1.0).astype(I32)
        affb_ref[j] = aff
        return (c_eq + jnp.sum(eqf, axis=1, keepdims=True), c_sel + jnp.sum(self_, axis=1, keepdims=True))

    zero = jnp.zeros((E, 1), F32)
    lax.fori_loop(0, T // tt, block, (zero, zero))


def _select(aff, cap):
    E, T = aff.shape
    nb = T // MOE_TT
    blk = jax.ShapeDtypeStruct((nb, E, MOE_TT), I32)
    return pl.pallas_call(
        functools.partial(_select_kernel, cap=cap),
        out_shape=[jax.ShapeDtypeStruct((E, T), I32), blk, blk, jax.ShapeDtypeStruct((nb, E, MOE_TT), F32)],
        compiler_params=pltpu.CompilerParams(vmem_limit_bytes=40 * 1024 * 1024),
        name="select",
    )(aff)


def _regroup_rows(T):
    nb = T // MOE_TT
    rows = CAPACITY_FACTOR * T + SC_GATHER_ROWS * nb
    return -(-rows // MOE_TW) * MOE_TW


def _combine_schedule(posb, T):
    nb = posb.shape[0]
    tw, g = MOE_TW, SC_GATHER_ROWS
    n = jnp.sum((posb >= 0).reshape(nb, -1), axis=1).astype(I32)
    seg = (n + g - 1) // g * g
    hi = jnp.cumsum(seg)
    lo = hi - seg
    off = jnp.concatenate([jnp.zeros((1,), I32), hi])
    nwin_max = (N_EXPERTS * MOE_TT) // tw + 1
    w0 = lo // tw
    w1 = jnp.where(seg > 0, (hi - 1) // tw, w0)
    cand = jnp.arange(nwin_max, dtype=I32)
    win = w0[:, None] + cand[None, :]
    valid = (win <= w1[:, None]).reshape(-1)
    nwin_total = _regroup_rows(T) // tw
    pmax = nb + nwin_total
    jv = jnp.broadcast_to(jnp.arange(nb, dtype=I32)[:, None], win.shape).reshape(-1)
    wv = jnp.minimum(win, nwin_total - 1).reshape(-1)
    total = jnp.sum(valid.astype(I32))
    dst = jnp.where(valid, jnp.cumsum(valid.astype(I32)) - 1, pmax)
    pj, pw = (jnp.zeros((pmax,), I32).at[dst].set(a, mode="drop") for a in (jv, wv))
    real = jnp.arange(pmax, dtype=I32) < total
    pj, pw = (jnp.where(real, a, a[total - 1]) for a in (pj, pw))
    first = jnp.concatenate([jnp.ones((1,), bool), pj[1:] != pj[:-1]])
    last = jnp.concatenate([pj[1:] != pj[:-1], jnp.ones((1,), bool)]) | (jnp.arange(pmax, dtype=I32) == total - 1)
    flag = jnp.where(real, first.astype(I32) + 2 * last.astype(I32) + 4, 0)
    return off, (pj, pw, flag, lo[pj], hi[pj])


SC_LANES = 16
SC_CORES = 2
SC_SUBCORES = 16
SC_GATHER_ROWS = 32


def _dispatch(pos, xw, cap):
    E, T = pos.shape
    W = xw.shape[1]
    G = SC_GATHER_ROWS
    part_rows = cap // SC_CORES
    mesh = plsc.VectorSubcoreMesh(core_axis_name="c", subcore_axis_name="s")

    @pl.kernel(
        out_type=jax.ShapeDtypeStruct((E * cap, W), I32),
        mesh=mesh,
        scratch_types=[pltpu.VMEM((T,), I32), pltpu.VMEM((cap,), I32), pltpu.VMEM((G, W), I32)],
        compiler_params=pltpu.CompilerParams(needs_layout_passes=False),
        name="sc_dispatch",
    )
    def run(pos_hbm, x_hbm, xe_hbm, pos_v, idx_v, buf):
        e = lax.axis_index("s")
        part = lax.axis_index("c")
        pltpu.sync_copy(pos_hbm.at[e], pos_v)
        lane = lax.iota(I32, SC_LANES)

        @pl.loop(0, T // SC_LANES)
        def _(i):
            off = pl.multiple_of(i * SC_LANES, SC_LANES)
            p = pos_v[pl.ds(off, SC_LANES)]
            plsc.store_scatter(idx_v, [p], lane + off, mask=p >= 0)

        @pl.loop(0, part_rows // G)
        def _(g):
            o = pl.multiple_of(part * part_rows + g * G, G)
            pltpu.sync_copy(x_hbm.at[idx_v.at[pl.ds(o, G)]], buf)
            pltpu.sync_copy(buf, xe_hbm.at[pl.ds(e * cap + o, G)])

    return run(pos, xw)


FFN_TM = 2048
FFN_TF = 256
FFN_UNPACK_ROWS = 256


FFN_TN = 256
FFN_UP_STEPS = D_FF // FFN_TF
FFN_DOWN_STEPS = D_MODEL // FFN_TN


def _ffn_kernel(x_ref, wg_ref, wu_ref, wd_ref, o_ref, xb_ref, hid_ref):
    s = pl.program_id(2)

    @pl.when(s == 0)
    def _():
        half = D_MODEL // 2

        def unpack_rows(i, carry):
            r = pl.multiple_of(i * FFN_UNPACK_ROWS, FFN_UNPACK_ROWS)
            lo, hi = _unpack_bf16_pairs(x_ref[pl.ds(r, FFN_UNPACK_ROWS), :])
            xb_ref[pl.ds(r, FFN_UNPACK_ROWS), :half] = lo.astype(BF16)
            xb_ref[pl.ds(r, FFN_UNPACK_ROWS), half:] = hi.astype(BF16)
            return carry

        lax.fori_loop(0, x_ref.shape[0] // FFN_UNPACK_ROWS, unpack_rows, 0)

    @pl.when(s < FFN_UP_STEPS)
    def _():
        x = xb_ref[...]
        g = _dot(x, wg_ref[...])
        u = _dot(x, wu_ref[...])
        col = pl.multiple_of(s * FFN_TF, FFN_TF)
        hid_ref[:, pl.ds(col, FFN_TF)] = (_silu(g) * u).astype(BF16)

    @pl.when(s >= FFN_UP_STEPS)
    def _():
        o_ref[...] = _pack_bf16_pairs(_dot(hid_ref[...], wd_ref[...]))


def _ffn(xe, w_gate, w_up, w_down):
    E, cap, _ = xe.shape
    tm, tf, tn = min(FFN_TM, cap), FFN_TF, FFN_TN
    n_up, n_down = FFN_UP_STEPS, FFN_DOWN_STEPS

    def up(e, m, s):
        return (e, jnp.minimum(s, n_up - 1), 0, 0)

    def down(s):
        return jnp.maximum(s - n_up, 0)

    return pl.pallas_call(
        _ffn_kernel,
        grid=(E, cap // tm, n_up + n_down),
        in_specs=[
            pl.BlockSpec((None, tm, D_MODEL // 2), lambda e, m, s: (e, m, 0)),
            pl.BlockSpec((None, None, D_MODEL, tf), up),
            pl.BlockSpec((None, None, D_MODEL, tf), up),
            pl.BlockSpec((None, None, D_FF, tn), lambda e, m, s: (e, down(s), 0, 0)),
        ],
        out_specs=pl.BlockSpec((None, tm, tn // 2), lambda e, m, s: (e, m, down(s))),
        out_shape=jax.ShapeDtypeStruct((E, cap, D_MODEL // 2), I32),
        scratch_shapes=[pltpu.VMEM((tm, D_MODEL), BF16), pltpu.VMEM((tm, D_FF), BF16)],
        compiler_params=_cparams(("parallel", "parallel", "arbitrary"), 60),
        name="ffn",
    )(xe, w_gate, w_up, w_down)


def _regroup(posb, rankb, affb, off, yw, cap, rows):
    NB, EB = posb.shape
    W = yw.shape[1]
    G, L, tt = SC_GATHER_ROWS, SC_LANES, MOE_TT
    n_workers = SC_CORES * SC_SUBCORES
    per = NB // n_workers
    mesh = plsc.VectorSubcoreMesh(core_axis_name="c", subcore_axis_name="s")
    off_pad = jnp.pad(off, (0, L))

    @pl.kernel(
        out_type=(jax.ShapeDtypeStruct((rows, W), I32), jax.ShapeDtypeStruct((rows,), I32),
                  jax.ShapeDtypeStruct((rows,), F32)),
        mesh=mesh,
        scratch_types=[pltpu.VMEM((EB,), I32), pltpu.VMEM((EB,), I32), pltpu.VMEM((EB,), F32),
                       pltpu.VMEM((EB,), I32), pltpu.VMEM((EB,), I32), pltpu.VMEM((EB,), F32),
                       pltpu.VMEM((G, W), I32), pltpu.VMEM((NB + 1 + L,), I32)],
        compiler_params=pltpu.CompilerParams(needs_layout_passes=False),
        name="sc_regroup",
    )
    def run(posb_hbm, rankb_hbm, affb_hbm, off_hbm, y_hbm, yg_hbm, tok_hbm, gate_hbm,
            pos_v, rank_v, aff_v, src_v, tok_v, gate_v, buf, off_v):
        wid = lax.axis_index("c") * SC_SUBCORES + lax.axis_index("s")
        pltpu.sync_copy(off_hbm, off_v)
        lane = lax.iota(I32, L)
        zi = jnp.zeros((L,), I32)
        zf = jnp.zeros((L,), F32)

        @pl.loop(0, per)
        def _(k):
            j = wid * per + k
            pltpu.sync_copy(posb_hbm.at[j], pos_v)
            pltpu.sync_copy(rankb_hbm.at[j], rank_v)
            pltpu.sync_copy(affb_hbm.at[j], aff_v)
            lo = jnp.max(plsc.load_gather(off_v, [zi + j]))
            hi = jnp.max(plsc.load_gather(off_v, [zi + j + 1]))

            @pl.loop(0, EB // L)
            def _(i):
                o = pl.multiple_of(i * L, L)
                src_v[pl.ds(o, L)] = zi
                tok_v[pl.ds(o, L)] = zi
                gate_v[pl.ds(o, L)] = zf

            @pl.loop(0, EB // L)
            def _(i):
                o = pl.multiple_of(i * L, L)
                p = pos_v[pl.ds(o, L)]
                r = rank_v[pl.ds(o, L)]
                m = p >= 0
                e = i // (tt // L)
                t0 = j * tt + (i % (tt // L)) * L
                plsc.store_scatter(src_v, [r], p + e * cap, mask=m)
                plsc.store_scatter(tok_v, [r], lane + t0, mask=m)
                plsc.store_scatter(gate_v, [r], aff_v[pl.ds(o, L)], mask=m)

            @pl.loop(0, (hi - lo) // G)
            def _(g):
                o = pl.multiple_of(g * G, G)
                dst = pl.multiple_of(lo + o, G)
                pltpu.sync_copy(y_hbm.at[src_v.at[pl.ds(o, G)]], buf)
                pltpu.sync_copy(buf, yg_hbm.at[pl.ds(dst, G)])
                pltpu.sync_copy(tok_v.at[pl.ds(o, G)], tok_hbm.at[pl.ds(dst, G)])
                pltpu.sync_copy(gate_v.at[pl.ds(o, G)], gate_hbm.at[pl.ds(dst, G)])

    return run(posb, rankb, affb, off_pad, yw)


def _combine_kernel(pj_ref, pw_ref, pf_ref, plo_ref, phi_ref, tok_ref, gate_ref, yg_ref, h_ref, nf_ref,
                    o_ref, acc_ref):
    p = pl.program_id(0)
    flag = pf_ref[p]
    tw, tt = MOE_TW, MOE_TT
    half = D_MODEL // 2

    hw = FFN_TN // 2

    def col_blocks():
        for n in range(FFN_DOWN_STEPS):
            yield slice(n * hw, (n + 1) * hw), slice(n * FFN_TN, n * FFN_TN + hw)
            yield slice(half + n * hw, half + (n + 1) * hw), slice(n * FFN_TN + hw, (n + 1) * FFN_TN)

    @pl.when((flag & 1) != 0)
    def _():
        for packed, natural in col_blocks():
            acc_ref[:, packed] = h_ref[:, natural]

    @pl.when((flag & 4) != 0)
    def _():
        lo, hi = plo_ref[p], phi_ref[p]
        row0 = pw_ref[p] * tw
        rid = lax.broadcasted_iota(I32, (tw, 1), 0) + row0
        keep = (rid >= lo) & (rid < hi)
        y_lo, y_hi = _unpack_bf16_pairs(yg_ref[...])
        y_lo = jnp.where(keep, y_lo, 0.0).astype(BF16)
        y_hi = jnp.where(keep, y_hi, 0.0).astype(BF16)
        tid = lax.broadcasted_iota(I32, (tt, tw), 0) + pj_ref[p] * tt
        cid = lax.broadcasted_iota(I32, (tt, tw), 1) + row0
        hit = (tok_ref[...] == tid) & (cid >= lo) & (cid < hi)
        weights = jnp.where(hit, gate_ref[...], 0.0).astype(BF16)
        acc_ref[:, :half] += _dot(weights, y_lo)
        acc_ref[:, half:] += _dot(weights, y_hi)

    @pl.when((flag & 2) != 0)
    def _():
        y = acc_ref[...]
        scale = lax.rsqrt(jnp.mean(y * y, axis=-1, keepdims=True) + EPS)
        for packed, natural in col_blocks():
            o_ref[:, natural] = acc_ref[:, packed] * scale * nf_ref[:, natural]


def _combine(lists, tok, gate, yg, h, nfw):
    pj, pw, pf, plo, phi = lists
    T = h.shape[0]
    tw, tt = MOE_TW, MOE_TT
    nwin = yg.shape[0] // tw
    grid_spec = pltpu.PrefetchScalarGridSpec(
        num_scalar_prefetch=5,
        grid=(pj.shape[0],),
        in_specs=[
            pl.BlockSpec((None, 1, tw), lambda p, pj, pw, *_: (pw[p], 0, 0)),
            pl.BlockSpec((None, 1, tw), lambda p, pj, pw, *_: (pw[p], 0, 0)),
            pl.BlockSpec((tw, D_MODEL // 2), lambda p, pj, pw, *_: (pw[p], 0)),
            pl.BlockSpec((tt, D_MODEL), lambda p, pj, pw, *_: (pj[p], 0)),
            pl.BlockSpec((1, D_MODEL), lambda p, pj, pw, *_: (0, 0)),
        ],
        out_specs=pl.BlockSpec((tt, D_MODEL), lambda p, pj, pw, *_: (pj[p], 0)),
        scratch_shapes=[pltpu.VMEM((tt, D_MODEL), F32)],
    )
    return pl.pallas_call(
        _combine_kernel,
        grid_spec=grid_spec,
        out_shape=jax.ShapeDtypeStruct((T, D_MODEL), F32),
        compiler_params=_cparams(("arbitrary",), 32),
        name="combine",
    )(pj, pw, pf, plo, phi, tok.reshape(nwin, 1, tw), gate.reshape(nwin, 1, tw), yg, h, nfw)


def _rope_tables(seq_len):
    d = RET_DK
    inv = ROPE_BASE ** (-jnp.arange(0, d, 2, dtype=F32) / d)
    ang = jnp.arange(seq_len, dtype=F32)[:, None] * inv[None, :]
    return jnp.cos(ang), jnp.sin(ang)


def _chunk_tri(n, chunk, upper):
    r = np.arange(n)
    same = (r[:, None] // chunk) == (r[None, :] // chunk)
    tri = (r[:, None] <= r[None, :]) if upper else (r[:, None] >= r[None, :])
    return jnp.asarray(same & tri, BF16)


def _col_tiles(w, width):
    *lead, k, n = w.shape
    t = w.astype(BF16).reshape(*lead, k, n // width, width)
    return jnp.swapaxes(t, -3, -2)


def _prep_params(norm1_w, w_in, ret_gn_w, gla_gate_up, gla_gate_bias, gla_gn_w, w_out, norm2_w, router_w,
                 normf_w):
    w = w_in[0]
    w_main = _col_tiles(w[:, :IN_MAIN], IP_TN)
    w_ga = jnp.pad(w[:, IN_MAIN:], ((0, 0), (0, LANE - 2 * GLA_RANK))).astype(BF16)
    cs = np.ones((1, IN_MAIN), np.float32)
    cs[:, _RQ:_RQ + RET_WIDTH] = RET_DK ** -0.5
    cs[:, _GQ:_GQ + GLA_KEY_WIDTH] = GLA_DK ** -0.5
    up = gla_gate_up[0].astype(F32)
    up_pad = jnp.zeros((LANE, 2 * GLA_KEY_WIDTH), F32)
    up_pad = up_pad.at[:GLA_RANK, :GLA_KEY_WIDTH].set(up[0])
    up_pad = up_pad.at[GLA_RANK:2 * GLA_RANK, GLA_KEY_WIDTH:].set(up[1])
    rt = router_w[0].T.astype(F32)
    r_hi = rt.astype(BF16)
    r_lo = (rt - r_hi.astype(F32)).astype(BF16)
    return dict(
        n1w=norm1_w[0].reshape(1, D_MODEL).astype(F32),
        w_main=w_main, w_ga=w_ga, colscale=jnp.asarray(cs),
        up_pad=up_pad.astype(BF16),
        bias=gla_gate_bias[0].reshape(1, 2 * GLA_KEY_WIDTH).astype(F32),
        lf=_chunk_tri(GATE_TM, GLA_CHUNK, upper=False),
        lb=_chunk_tri(GATE_TM, GLA_CHUNK, upper=True),
        ret_gn=ret_gn_w[0].reshape(1, RET_WIDTH).astype(F32),
        gla_gn=gla_gn_w[0].reshape(1, GLA_WIDTH).astype(F32),
        w_out=w_out[0].astype(BF16),
        n2w=norm2_w[0].reshape(1, D_MODEL).astype(F32),
        r_hi=r_hi, r_lo=r_lo,
        nfw=normf_w.reshape(1, D_MODEL).astype(F32),
    )


def _trunk(x, pp, decay_logit, w_gate, w_up, w_down):
    B, L, _ = x.shape
    T = B * L
    x2d = x.reshape(T, D_MODEL)
    cos, sin = _rope_tables(L)
    proj, ga = _in_proj(x2d, pp["n1w"], pp["w_main"], pp["w_ga"], pp["colscale"], cos, sin, L)
    b_f, b_b = _gla_gates(ga, pp["up_pad"], pp["bias"], pp["lf"], pp["lb"])

    ret_f = _ret_scan(proj, decay_logit, B, L, reverse=False)
    mix_r = _ret_scan(proj, decay_logit, B, L, reverse=True, o_fwd=ret_f, gn_w=pp["ret_gn"])
    gla_f = _gla_scan(proj, b_f, B, L, reverse=False)
    mix_g = _gla_scan(proj, b_b, B, L, reverse=True, o_fwd=gla_f, gn_w=pp["gla_gn"])

    h, xn2, aff = _out_proj(mix_r, mix_g, pp["w_out"], x2d, pp["n2w"], pp["r_hi"], pp["r_lo"])

    cap = CAPACITY_FACTOR * T // N_EXPERTS
    pos, posb, rankb, affb = _select(aff, cap)
    off, c_lists = _combine_schedule(posb, T)
    xe = _dispatch(pos, xn2, cap).reshape(N_EXPERTS, cap, D_MODEL // 2)
    ye = _ffn(xe, w_gate, w_up, w_down).reshape(N_EXPERTS * cap, D_MODEL // 2)
    nb = T // MOE_TT
    yg, tok, gate = _regroup(posb.reshape(nb, -1), rankb.reshape(nb, -1), affb.reshape(nb, -1), off, ye, cap,
                             _regroup_rows(T))
    y = _combine(c_lists, tok, gate, yg, h, pp["nfw"])
    return y.reshape(B, L, D_MODEL)


def kernel(x_prompt, x_sample, norm1_w, w_in, ret_decay_logit, ret_gn_w, gla_gate_up, gla_gate_bias,
           gla_gn_w, w_out, norm2_w, router_w, w_gate, w_up, w_down, normf_w):
    pp = _prep_params(norm1_w, w_in, ret_gn_w, gla_gate_up, gla_gate_bias, gla_gn_w, w_out, norm2_w,
                      router_w, normf_w)
    decay_logit = ret_decay_logit[0].astype(F32)
    args = (pp, decay_logit, _col_tiles(w_gate[0], FFN_TF), _col_tiles(w_up[0], FFN_TF),
            _col_tiles(w_down[0], FFN_TN))
    return (_trunk(x_prompt, *args), _trunk(x_sample, *args))
```

```python
import functools

import numpy as np
import jax
import jax.numpy as jnp
from jax import lax
from jax.experimental import pallas as pl
from jax.experimental.pallas import tpu as pltpu
from jax.experimental.pallas import tpu_sc as plsc

F32, BF16, I32 = jnp.float32, jnp.bfloat16, jnp.int32

D_MODEL = 2048
RET_WIDTH = 1024
RET_HEADS = 4
RET_DK = 256
RET_DV = 256
GLA_WIDTH = 1024
GLA_HEADS = 4
GLA_DK = 128
GLA_DV = 256
GLA_KEY_WIDTH = 512
GLA_RANK = 16
GLA_TAU = 16.0
RET_CHUNK = 256
GLA_CHUNK = 64
GLA_SUB = 16
ROPE_BASE = 10000.0
N_EXPERTS = 16
CAPACITY_FACTOR = 2
D_FF = 2048
EPS = 1e-6
LOG2_E = 1.4426950408889634
IN_MAIN = 4 * RET_WIDTH + 2 * GLA_KEY_WIDTH + 2 * GLA_WIDTH

_RQ, _RK, _RV, _RG = 0, 1024, 2048, 3072
_GQ, _GK, _GV, _GG = 4096, 4608, 5120, 6144

LANE = 128
MOE_TT = 256
MOE_TW = 256
V7X_VMEM_BYTES = 64 * 1024 * 1024


def _cparams(sem, vmem_mb):
    return pltpu.CompilerParams(dimension_semantics=sem, vmem_limit_bytes=vmem_mb * 1024 * 1024)


def _log_sigmoid(z):
    return jnp.minimum(z, 0.0) - jnp.log1p(jnp.exp(-jnp.abs(z)))


def _silu(g):
    return g * (1.0 / (1.0 + jnp.exp(-g)))


def _dot_nt(a, b):
    return lax.dot_general(a, b, (((1,), (1,)), ((), ())), preferred_element_type=F32)


def _dot_tn(a, b):
    return lax.dot_general(a, b, (((0,), (0,)), ((), ())), preferred_element_type=F32)


def _dot(a, b):
    return jnp.dot(a, b, preferred_element_type=F32)


def _pack_bf16_pairs(x):
    bits = pltpu.bitcast(x.astype(BF16).astype(F32), I32)
    w = x.shape[1] // 2
    return bits[:, w:] | lax.shift_right_logical(bits[:, :w], 16)


def _unpack_bf16_pairs(words):
    lo = pltpu.bitcast(lax.shift_left(words, 16), F32)
    hi = pltpu.bitcast(words & jnp.int32(-65536), F32)
    return lo, hi


IP_TM = 1024
IP_TN = 1024


def _in_proj_kernel(x_ref, n1_ref, w_ref, wga_ref, cs_ref, cos_ref, sin_ref, o_ref, ga_ref, xn_ref):
    j = pl.program_id(1)

    @pl.when(j == 0)
    def _():
        x = x_ref[...]
        ms = jnp.mean(x * x, axis=-1, keepdims=True)
        xn = (x * lax.rsqrt(ms + EPS) * n1_ref[...]).astype(BF16)
        xn_ref[...] = xn
        ga_ref[...] = _dot(xn, wga_ref[...])

    acc = _dot(xn_ref[...], w_ref[...]) * cs_ref[...]
    n_rope_blocks = 2 * RET_WIDTH // IP_TN

    @pl.when(j < n_rope_blocks)
    def _():
        cos = cos_ref[...]
        sin = sin_ref[...]
        half = RET_DK // 2
        for h in range(IP_TN // RET_DK):
            lo = h * RET_DK
            x1 = acc[:, lo:lo + half]
            x2 = acc[:, lo + half:lo + RET_DK]
            o_ref[:, lo:lo + half] = (x1 * cos - x2 * sin).astype(BF16)
            o_ref[:, lo + half:lo + RET_DK] = (x1 * sin + x2 * cos).astype(BF16)

    @pl.when(j >= n_rope_blocks)
    def _():
        o_ref[...] = acc.astype(BF16)


def _in_proj(x2d, n1w, w_main, w_ga, colscale, cos, sin, seq_len):
    T = x2d.shape[0]
    tm, tn = IP_TM, IP_TN
    nlb = seq_len // tm
    return pl.pallas_call(
        _in_proj_kernel,
        grid=(T // tm, IN_MAIN // tn),
        in_specs=[
            pl.BlockSpec((tm, D_MODEL), lambda i, j: (i, 0)),
            pl.BlockSpec((1, D_MODEL), lambda i, j: (0, 0)),
            pl.BlockSpec((D_MODEL, tn), lambda i, j: (0, j)),
            pl.BlockSpec((D_MODEL, LANE), lambda i, j: (0, 0)),
            pl.BlockSpec((1, tn), lambda i, j: (0, j)),
            pl.BlockSpec((tm, LANE), lambda i, j: (i % nlb, 0)),
            pl.BlockSpec((tm, LANE), lambda i, j: (i % nlb, 0)),
        ],
        out_specs=[
            pl.BlockSpec((tm, tn), lambda i, j: (i, j)),
            pl.BlockSpec((tm, LANE), lambda i, j: (i, 0)),
        ],
        out_shape=[
            jax.ShapeDtypeStruct((T, IN_MAIN), BF16),
            jax.ShapeDtypeStruct((T, LANE), F32),
        ],
        scratch_shapes=[pltpu.VMEM((tm, D_MODEL), BF16)],
        compiler_params=_cparams(("parallel", "arbitrary"), 48),
        name="in_proj",
    )(x2d, n1w, w_main, w_ga, colscale, cos, sin)


GATE_TM = 256


def _gates_kernel(ga_ref, up_ref, bias_ref, lf_ref, lb_ref, bf_ref, bb_ref):
    z = _dot(ga_ref[...].astype(BF16), up_ref[...]) + bias_ref[...]
    la = _log_sigmoid(z) * (LOG2_E / GLA_TAU)
    hi = la.astype(BF16)
    lo = (la - hi.astype(F32)).astype(BF16)
    kw = GLA_KEY_WIDTH
    bf_ref[...] = _dot(lf_ref[...], hi[:, :kw]) + _dot(lf_ref[...], lo[:, :kw])
    bb_ref[...] = _dot(lb_ref[...], hi[:, kw:]) + _dot(lb_ref[...], lo[:, kw:])


def _gla_gates(ga, up_pad, bias, lf, lb):
    T = ga.shape[0]
    tm = GATE_TM
    kw = GLA_KEY_WIDTH
    return pl.pallas_call(
        _gates_kernel,
        grid=(T // tm,),
        in_specs=[
            pl.BlockSpec((tm, LANE), lambda i: (i, 0)),
            pl.BlockSpec((LANE, 2 * kw), lambda i: (0, 0)),
            pl.BlockSpec((1, 2 * kw), lambda i: (0, 0)),
            pl.BlockSpec((tm, tm), lambda i: (0, 0)),
            pl.BlockSpec((tm, tm), lambda i: (0, 0)),
        ],
        out_specs=[pl.BlockSpec((tm, kw), lambda i: (i, 0)), pl.BlockSpec((tm, kw), lambda i: (i, 0))],
        out_shape=[jax.ShapeDtypeStruct((T, kw), F32), jax.ShapeDtypeStruct((T, kw), F32)],
        compiler_params=_cparams(("parallel",), 32),
        name="gla_gates",
    )(ga, up_pad, bias, lf, lb)


def _finish_heads(tot, gn, gate):
    ms = jnp.mean(tot * tot, axis=-1, keepdims=True)
    yn = tot * lax.rsqrt(ms + EPS) * gn
    return (yn * _silu(gate.astype(F32))).astype(BF16)


RET_TB = 1024


def _ret_kernel(dl_ref, q_ref, k_ref, v_ref, *rest, reverse):
    if reverse:
        g_ref, of_ref, gn_ref, o_ref, s_ref, intra_ref, qd_ref, kd_ref, cd_ref = rest
    else:
        o_ref, s_ref, intra_ref, qd_ref, kd_ref, cd_ref = rest
    h = pl.program_id(1)
    n = pl.program_id(2)
    C = RET_CHUNK

    @pl.when(n == 0)
    def _():
        s_ref[...] = jnp.zeros_like(s_ref)
        logit = dl_ref[1 if reverse else 0, h]
        lg = _log_sigmoid(jnp.full((C, RET_DV), logit, F32))
        lg_c = _log_sigmoid(jnp.full((C, C), logit, F32))
        lg_r = _log_sigmoid(jnp.full((1, RET_DV), logit, F32))
        ri = lax.broadcasted_iota(I32, (C, RET_DV), 0).astype(F32)
        rc = lax.broadcasted_iota(I32, (C, C), 0).astype(F32)
        cc = lax.broadcasted_iota(I32, (C, C), 1).astype(F32)
        diff = (cc - rc) if reverse else (rc - cc)
        intra_ref[...] = jnp.where(diff >= 0, jnp.exp(lg_c * diff), 0.0)
        if reverse:
            qd_ref[...] = jnp.exp(lg * (C - ri))
            kd_ref[...] = jnp.exp(lg * ri)
        else:
            qd_ref[...] = jnp.exp(lg * (ri + 1.0))
            kd_ref[...] = jnp.exp(lg * (C - 1.0 - ri))
        cd_ref[...] = jnp.exp(lg_r * C)

    nchunks = q_ref.shape[0] // C
    order = range(nchunks - 1, -1, -1) if reverse else range(nchunks)
    for c in order:
        rows = slice(c * C, (c + 1) * C)
        q = q_ref[rows, :]
        k = k_ref[rows, :]
        v = v_ref[rows, :]
        s = _dot_nt(q, k) * intra_ref[...]
        state = s_ref[...]
        o = _dot(s.astype(BF16), v) + _dot(q, state.astype(BF16)) * qd_ref[...]
        kd = (k.astype(F32) * kd_ref[...]).astype(BF16)
        s_ref[...] = state * cd_ref[...] + _dot_tn(kd, v)
        if reverse:
            o_ref[rows, :] = _finish_heads(of_ref[rows, :] + o, gn_ref[...], g_ref[rows, :])
        else:
            o_ref[rows, :] = o


def _ret_scan(proj, decay_logit, batch, seq_len, reverse, o_fwd=None, gn_w=None):
    T = proj.shape[0]
    tb = RET_TB
    nb = seq_len // tb
    dk, dv, C = RET_DK, RET_DV, RET_CHUNK

    def rb(b, n):
        return b * nb + ((nb - 1 - n) if reverse else n)

    def col(base):
        return lambda b, h, n: (rb(b, n), base // dk + h)

    in_specs = [
        pl.BlockSpec(memory_space=pltpu.SMEM),
        pl.BlockSpec((tb, dk), col(_RQ)),
        pl.BlockSpec((tb, dk), col(_RK)),
        pl.BlockSpec((tb, dv), col(_RV)),
    ]
    args = [decay_logit, proj, proj, proj]
    if reverse:
        in_specs += [
            pl.BlockSpec((tb, dv), col(_RG)),
            pl.BlockSpec((tb, dv), lambda b, h, n: (rb(b, n), h)),
            pl.BlockSpec((1, dv), lambda b, h, n: (0, h)),
        ]
        args += [proj, o_fwd, gn_w]
    out_dtype = BF16 if reverse else F32
    return pl.pallas_call(
        functools.partial(_ret_kernel, reverse=reverse),
        grid=(batch, RET_HEADS, nb),
        in_specs=in_specs,
        out_specs=pl.BlockSpec((tb, dv), lambda b, h, n: (rb(b, n), h)),
        out_shape=jax.ShapeDtypeStruct((T, RET_WIDTH), out_dtype),
        scratch_shapes=[
            pltpu.VMEM((dk, dv), F32),
            pltpu.VMEM((C, C), F32),
            pltpu.VMEM((C, dv), F32),
            pltpu.VMEM((C, dk), F32),
            pltpu.VMEM((1, dv), F32),
        ],
        compiler_params=_cparams(("parallel", "parallel", "arbitrary"), 32),
        name="ret_bwd" if reverse else "ret_fwd",
    )(*args)


GLA_TB = 512
GLA_UNROLL = 8


def _gla_kernel(q_ref, k_ref, v_ref, b_ref, *rest, reverse):
    if reverse:
        g_ref, of_ref, gn_ref, o_ref, st_ref = rest
    else:
        o_ref, st_ref = rest
    n = pl.program_id(2)
    C, SUB = GLA_CHUNK, GLA_SUB
    NS = C // SUB

    @pl.when(n == 0)
    def _():
        st_ref[...] = jnp.zeros_like(st_ref)

    nchunks = q_ref.shape[0] // C
    row_c = lax.broadcasted_iota(I32, (C, GLA_DK), 0)
    lane_s = lax.broadcasted_iota(I32, (SUB, C), 1)
    lane_h = lax.broadcasted_iota(I32, (SUB // 2, C), 1)
    row_s = lax.broadcasted_iota(I32, (SUB, C), 0)

    def chunk(ci, carry):
        c = (nchunks - 1 - ci) if reverse else ci
        c0 = pl.multiple_of(c * C, C)
        q = q_ref[pl.ds(c0, C), :].astype(F32)
        k = k_ref[pl.ds(c0, C), :].astype(F32)
        v = v_ref[pl.ds(c0, C), :]
        b = b_ref[pl.ds(c0, C), :]
        b_end = b[0:1, :] if reverse else b[C - 1:C, :]

        st = st_ref[...]
        o = _dot_nt((q * jnp.exp2(b)).astype(BF16), st.astype(BF16))
        ke = (k * jnp.exp2(b_end - b)).astype(BF16)
        st_ref[...] = st * jnp.exp2(b_end) + _dot_tn(v, ke)

        prows = []
        for si in range(NS):
            r0 = si * SUB
            b_i = b[r0:r0 + SUB, :]
            q_i = q[r0:r0 + SUB, :]
            halves = [jnp.zeros((SUB // 2, C), F32), jnp.zeros((SUB // 2, C), F32)]
            for jj in range(SUB):
                b_j = b[r0 + jj:r0 + jj + 1, :]
                k_j = k[r0 + jj:r0 + jj + 1, :]
                for hf in range(2):
                    needed = (hf == 0 or jj >= SUB // 2) if reverse else (hf == 1 or jj < SUB // 2)
                    if not needed:
                        continue
                    rs = slice(hf * SUB // 2, (hf + 1) * SUB // 2)
                    w = jnp.exp2(b_i[rs] - b_j)
                    col = jnp.sum(q_i[rs] * k_j * w, axis=-1, keepdims=True)
                    halves[hf] = jnp.where(lane_h == r0 + jj, col, halves[hf])
            sd = jnp.concatenate(halves, axis=0)
            if reverse:
                causal = (row_s + r0) <= lane_s
                ref_row = b[r0 + SUB - 1:r0 + SUB, :]
                has_off = si < NS - 1
                off_rows = row_c >= r0 + SUB
            else:
                causal = (row_s + r0) >= lane_s
                ref_row = b[r0:r0 + 1, :]
                has_off = si > 0
                off_rows = row_c < r0
            s_i = jnp.where(causal, sd, 0.0)
            if has_off:
                qs = (q_i * jnp.exp2(b_i - ref_row)).astype(BF16)
                kk = jnp.where(off_rows, k * jnp.exp2(ref_row - b), 0.0).astype(BF16)
                s_i = s_i + _dot_nt(qs, kk)
            prows.append(s_i)
        p = jnp.concatenate(prows, axis=0).astype(BF16)
        o = o + _dot(p, v)
        if reverse:
            tot = of_ref[pl.ds(c0, C), :] + o
            o_ref[pl.ds(c0, C), :] = _finish_heads(tot, gn_ref[...], g_ref[pl.ds(c0, C), :])
        else:
            o_ref[pl.ds(c0, C), :] = o
        return carry

    lax.fori_loop(0, nchunks, chunk, 0, unroll=GLA_UNROLL)


def _gla_scan(proj, bcum, batch, seq_len, reverse, o_fwd=None, gn_w=None):
    T = proj.shape[0]
    tb = GLA_TB
    nb = seq_len // tb
    dk, dv = GLA_DK, GLA_DV

    def rb(b, n):
        return b * nb + ((nb - 1 - n) if reverse else n)

    in_specs = [
        pl.BlockSpec((tb, dk), lambda b, h, n: (rb(b, n), _GQ // dk + h)),
        pl.BlockSpec((tb, dk), lambda b, h, n: (rb(b, n), _GK // dk + h)),
        pl.BlockSpec((tb, dv), lambda b, h, n: (rb(b, n), _GV // dv + h)),
        pl.BlockSpec((tb, dk), lambda b, h, n: (rb(b, n), h)),
    ]
    args = [proj, proj, proj, bcum]
    if reverse:
        in_specs += [
            pl.BlockSpec((tb, dv), lambda b, h, n: (rb(b, n), _GG // dv + h)),
            pl.BlockSpec((tb, dv), lambda b, h, n: (rb(b, n), h)),
            pl.BlockSpec((1, dv), lambda b, h, n: (0, h)),
        ]
        args += [proj, o_fwd, gn_w]
    out_dtype = BF16 if reverse else F32
    return pl.pallas_call(
        functools.partial(_gla_kernel, reverse=reverse),
        grid=(batch, GLA_HEADS, nb),
        in_specs=in_specs,
        out_specs=pl.BlockSpec((tb, dv), lambda b, h, n: (rb(b, n), h)),
        out_shape=jax.ShapeDtypeStruct((T, GLA_WIDTH), out_dtype),
        scratch_shapes=[pltpu.VMEM((dv, dk), F32)],
        compiler_params=_cparams(("parallel", "parallel", "arbitrary"), 32),
        name="gla_bwd" if reverse else "gla_fwd",
    )(*args)


OP_TM = 512
OP_SUB = 256


def _out_proj_kernel(mr_ref, mg_ref, w0_ref, w1_ref, x_ref, n2_ref, rh_ref, rl_ref, h_ref, xn_ref, aff_ref):
    for r in range(OP_TM // OP_SUB):
        rows = slice(r * OP_SUB, (r + 1) * OP_SUB)
        h = x_ref[rows, :] + _dot(mr_ref[rows, :], w0_ref[...]) + _dot(mg_ref[rows, :], w1_ref[...])
        h_ref[rows, :] = h
        ms = jnp.mean(h * h, axis=-1, keepdims=True)
        xn = h * lax.rsqrt(ms + EPS) * n2_ref[...]
        xh = xn.astype(BF16)
        xn_ref[rows, :] = _pack_bf16_pairs(xn)
        xl = (xn - xh.astype(F32)).astype(BF16)
        lt = _dot_nt(rh_ref[...], xh) + _dot_nt(rh_ref[...], xl) + _dot_nt(rl_ref[...], xh)
        m = jnp.max(lt, axis=0, keepdims=True)
        e = jnp.exp(lt - m)
        aff_ref[:, rows] = e / jnp.sum(e, axis=0, keepdims=True)


def _out_proj(mix_r, mix_g, w_out, x2d, n2w, r_hi, r_lo):
    T = x2d.shape[0]
    tm = OP_TM
    half = RET_WIDTH
    return pl.pallas_call(
        _out_proj_kernel,
        grid=(T // tm,),
        in_specs=[
            pl.BlockSpec((tm, half), lambda i: (i, 0)),
            pl.BlockSpec((tm, half), lambda i: (i, 0)),
            pl.BlockSpec((half, D_MODEL), lambda i: (0, 0)),
            pl.BlockSpec((half, D_MODEL), lambda i: (1, 0)),
            pl.BlockSpec((tm, D_MODEL), lambda i: (i, 0)),
            pl.BlockSpec((1, D_MODEL), lambda i: (0, 0)),
            pl.BlockSpec((N_EXPERTS, D_MODEL), lambda i: (0, 0)),
            pl.BlockSpec((N_EXPERTS, D_MODEL), lambda i: (0, 0)),
        ],
        out_specs=[
            pl.BlockSpec((tm, D_MODEL), lambda i: (i, 0)),
            pl.BlockSpec((tm, D_MODEL // 2), lambda i: (i, 0)),
            pl.BlockSpec((N_EXPERTS, tm), lambda i: (0, i)),
        ],
        out_shape=[
            jax.ShapeDtypeStruct((T, D_MODEL), F32),
            jax.ShapeDtypeStruct((T, D_MODEL // 2), I32),
            jax.ShapeDtypeStruct((N_EXPERTS, T), F32),
        ],
        compiler_params=_cparams(("parallel",), 56),
        name="out_proj",
    )(mix_r, mix_g, w_out, w_out, x2d, n2w, r_hi, r_lo)


def _select_kernel(a_ref, pos_ref, posb_ref, rankb_ref, affb_ref, *, cap):
    E, T = a_ref.shape
    tt = MOE_TT

    def count(pred):
        return jnp.sum(pred.astype(F32), axis=1, keepdims=True)

    def bisect(i, tau):
        cand = tau | jnp.left_shift(jnp.int32(1), 30 - i)
        bits = pltpu.bitcast(a_ref[...], I32)
        return jnp.where(count(bits >= cand) >= cap, cand, tau)

    tau = lax.fori_loop(0, 31, bisect, jnp.zeros((E, 1), I32))
    bits_all = pltpu.bitcast(a_ref[...], I32)
    quota = cap - count(bits_all > tau)

    before = (lax.broadcasted_iota(I32, (tt, tt), 0) < lax.broadcasted_iota(I32, (tt, tt), 1)).astype(BF16)
    below = (lax.broadcasted_iota(I32, (E, E), 1) < lax.broadcasted_iota(I32, (E, E), 0)).astype(BF16)

    def block(j, carry):
        c_eq, c_sel = carry
        off = pl.multiple_of(j * tt, tt)
        aff = a_ref[:, pl.ds(off, tt)]
        bits = pltpu.bitcast(aff, I32)
        eq = bits == tau
        eqf = eq.astype(F32)
        rank_eq = _dot(eqf.astype(BF16), before) + c_eq
        sel = (bits > tau) | (eq & (rank_eq < quota))
        self_ = sel.astype(F32)
        selb = self_.astype(BF16)
        slot = _dot(selb, before) + c_sel
        pos = jnp.where(sel, slot, -1.0).astype(I32)
        pos_ref[:, pl.ds(off, tt)] = pos
        per_tok = jnp.broadcast_to(jnp.sum(self_, axis=0, keepdims=True), (E, tt))
        rank = _dot(per_tok.astype(BF16), before) + _dot(below, selb)
        posb_ref[j] = pos
        rankb_ref[j] = jnp.where(sel, rank, -1.0).astype(I32)
        affb_ref[j] = aff
        return (c_eq + jnp.sum(eqf, axis=1, keepdims=True), c_sel + jnp.sum(self_, axis=1, keepdims=True))

    zero = jnp.zeros((E, 1), F32)
    lax.fori_loop(0, T // tt, block, (zero, zero))


def _select(aff, cap):
    E, T = aff.shape
    nb = T // MOE_TT
    blk = jax.ShapeDtypeStruct((nb, E, MOE_TT), I32)
    return pl.pallas_call(
        functools.partial(_select_kernel, cap=cap),
        out_shape=[jax.ShapeDtypeStruct((E, T), I32), blk, blk, jax.ShapeDtypeStruct((nb, E, MOE_TT), F32)],
        compiler_params=pltpu.CompilerParams(vmem_limit_bytes=40 * 1024 * 1024),
        name="select",
    )(aff)


def _regroup_rows(T):
    nb = T // MOE_TT
    rows = CAPACITY_FACTOR * T + SC_GATHER_ROWS * nb
    return -(-rows // MOE_TW) * MOE_TW


def _combine_schedule(posb, T):
    nb = posb.shape[0]
    tw, g = MOE_TW, SC_GATHER_ROWS
    n = jnp.sum((posb >= 0).reshape(nb, -1), axis=1).astype(I32)
    seg = (n + g - 1) // g * g
    hi = jnp.cumsum(seg)
    lo = hi - seg
    off = jnp.concatenate([jnp.zeros((1,), I32), hi])
    nwin_max = (N_EXPERTS * MOE_TT) // tw + 1
    w0 = lo // tw
    w1 = jnp.where(seg > 0, (hi - 1) // tw, w0)
    cand = jnp.arange(nwin_max, dtype=I32)
    win = w0[:, None] + cand[None, :]
    valid = (win <= w1[:, None]).reshape(-1)
    nwin_total = _regroup_rows(T) // tw
    pmax = nb + nwin_total
    jv = jnp.broadcast_to(jnp.arange(nb, dtype=I32)[:, None], win.shape).reshape(-1)
    wv = jnp.minimum(win, nwin_total - 1).reshape(-1)
    total = jnp.sum(valid.astype(I32))
    dst = jnp.where(valid, jnp.cumsum(valid.astype(I32)) - 1, pmax)
    pj, pw = (jnp.zeros((pmax,), I32).at[dst].set(a, mode="drop") for a in (jv, wv))
    real = jnp.arange(pmax, dtype=I32) < total
    pj, pw = (jnp.where(real, a, a[total - 1]) for a in (pj, pw))
    first = jnp.concatenate([jnp.ones((1,), bool), pj[1:] != pj[:-1]])
    last = jnp.concatenate([pj[1:] != pj[:-1], jnp.ones((1,), bool)]) | (jnp.arange(pmax, dtype=I32) == total - 1)
    flag = jnp.where(real, first.astype(I32) + 2 * last.astype(I32) + 4, 0)
    return off, (pj, pw, flag, lo[pj], hi[pj])


SC_LANES = 16
SC_CORES = 2
SC_SUBCORES = 16
SC_GATHER_ROWS = 32


def _dispatch(pos, xw, cap):
    E, T = pos.shape
    W = xw.shape[1]
    G = SC_GATHER_ROWS
    part_rows = cap // SC_CORES
    mesh = plsc.VectorSubcoreMesh(core_axis_name="c", subcore_axis_name="s")

    @pl.kernel(
        out_type=jax.ShapeDtypeStruct((E * cap, W), I32),
        mesh=mesh,
        scratch_types=[pltpu.VMEM((T,), I32), pltpu.VMEM((cap,), I32), pltpu.VMEM((G, W), I32)],
        compiler_params=pltpu.CompilerParams(needs_layout_passes=False),
        name="sc_dispatch",
    )
    def run(pos_hbm, x_hbm, xe_hbm, pos_v, idx_v, buf):
        e = lax.axis_index("s")
        part = lax.axis_index("c")
        pltpu.sync_copy(pos_hbm.at[e], pos_v)
        lane = lax.iota(I32, SC_LANES)

        @pl.loop(0, T // SC_LANES)
        def _(i):
            off = pl.multiple_of(i * SC_LANES, SC_LANES)
            p = pos_v[pl.ds(off, SC_LANES)]
            plsc.store_scatter(idx_v, [p], lane + off, mask=p >= 0)

        @pl.loop(0, part_rows // G)
        def _(g):
            o = pl.multiple_of(part * part_rows + g * G, G)
            pltpu.sync_copy(x_hbm.at[idx_v.at[pl.ds(o, G)]], buf)
            pltpu.sync_copy(buf, xe_hbm.at[pl.ds(e * cap + o, G)])

    return run(pos, xw)


FFN_TM = 2048
FFN_UNPACK_ROWS = 256


def _ffn_tile_width(cap):
    return 512 if min(FFN_TM, cap) <= 1024 else 256


def _ffn_kernel(x_ref, wg_ref, wu_ref, wd_ref, o_ref, xb_ref, hid_ref, *, tw):
    s = pl.program_id(2)
    n_up = D_FF // tw

    @pl.when(s == 0)
    def _():
        half = D_MODEL // 2

        def unpack_rows(i, carry):
            r = pl.multiple_of(i * FFN_UNPACK_ROWS, FFN_UNPACK_ROWS)
            lo, hi = _unpack_bf16_pairs(x_ref[pl.ds(r, FFN_UNPACK_ROWS), :])
            xb_ref[pl.ds(r, FFN_UNPACK_ROWS), :half] = lo.astype(BF16)
            xb_ref[pl.ds(r, FFN_UNPACK_ROWS), half:] = hi.astype(BF16)
            return carry

        lax.fori_loop(0, x_ref.shape[0] // FFN_UNPACK_ROWS, unpack_rows, 0)

    @pl.when(s < n_up)
    def _():
        x = xb_ref[...]
        g = _dot(x, wg_ref[...].astype(BF16))
        u = _dot(x, wu_ref[...].astype(BF16))
        col = pl.multiple_of(s * tw, tw)
        hid_ref[:, pl.ds(col, tw)] = (_silu(g) * u).astype(BF16)

    @pl.when(s >= n_up)
    def _():
        o_ref[...] = _pack_bf16_pairs(_dot(hid_ref[...], wd_ref[...].astype(BF16)))


def _ffn(xe, w_gate, w_up, w_down):
    E, cap, _ = xe.shape
    tm = min(FFN_TM, cap)
    tw = _ffn_tile_width(cap)
    n_up, n_down = D_FF // tw, D_MODEL // tw

    def up(e, m, s):
        return (e, 0, jnp.minimum(s, n_up - 1))

    def down(s):
        return jnp.maximum(s - n_up, 0)

    return pl.pallas_call(
        functools.partial(_ffn_kernel, tw=tw),
        grid=(E, cap // tm, n_up + n_down),
        in_specs=[
            pl.BlockSpec((None, tm, D_MODEL // 2), lambda e, m, s: (e, m, 0)),
            pl.BlockSpec((None, D_MODEL, tw), up),
            pl.BlockSpec((None, D_MODEL, tw), up),
            pl.BlockSpec((None, D_FF, tw), lambda e, m, s: (e, 0, down(s))),
        ],
        out_specs=pl.BlockSpec((None, tm, tw // 2), lambda e, m, s: (e, m, down(s))),
        out_shape=jax.ShapeDtypeStruct((E, cap, D_MODEL // 2), I32),
        scratch_shapes=[pltpu.VMEM((tm, D_MODEL), BF16), pltpu.VMEM((tm, D_FF), BF16)],
        compiler_params=_cparams(("parallel", "parallel", "arbitrary"), 60),
        name="ffn",
    )(xe, w_gate, w_up, w_down)


def _regroup(posb, rankb, affb, off, yw, cap, rows):
    NB, EB = posb.shape
    W = yw.shape[1]
    G, L, tt = SC_GATHER_ROWS, SC_LANES, MOE_TT
    n_workers = SC_CORES * SC_SUBCORES
    per = NB // n_workers
    mesh = plsc.VectorSubcoreMesh(core_axis_name="c", subcore_axis_name="s")
    off_pad = jnp.pad(off, (0, L))

    @pl.kernel(
        out_type=(jax.ShapeDtypeStruct((rows, W), I32), jax.ShapeDtypeStruct((rows,), I32),
                  jax.ShapeDtypeStruct((rows,), F32)),
        mesh=mesh,
        scratch_types=[pltpu.VMEM((EB,), I32), pltpu.VMEM((EB,), I32), pltpu.VMEM((EB,), F32),
                       pltpu.VMEM((EB,), I32), pltpu.VMEM((EB,), I32), pltpu.VMEM((EB,), F32),
                       pltpu.VMEM((G, W), I32), pltpu.VMEM((NB + 1 + L,), I32)],
        compiler_params=pltpu.CompilerParams(needs_layout_passes=False),
        name="sc_regroup",
    )
    def run(posb_hbm, rankb_hbm, affb_hbm, off_hbm, y_hbm, yg_hbm, tok_hbm, gate_hbm,
            pos_v, rank_v, aff_v, src_v, tok_v, gate_v, buf, off_v):
        wid = lax.axis_index("c") * SC_SUBCORES + lax.axis_index("s")
        pltpu.sync_copy(off_hbm, off_v)
        lane = lax.iota(I32, L)
        zi = jnp.zeros((L,), I32)
        zf = jnp.zeros((L,), F32)

        @pl.loop(0, per)
        def _(k):
            j = wid * per + k
            pltpu.sync_copy(posb_hbm.at[j], pos_v)
            pltpu.sync_copy(rankb_hbm.at[j], rank_v)
            pltpu.sync_copy(affb_hbm.at[j], aff_v)
            lo = jnp.max(plsc.load_gather(off_v, [zi + j]))
            hi = jnp.max(plsc.load_gather(off_v, [zi + j + 1]))

            @pl.loop(0, EB // L)
            def _(i):
                o = pl.multiple_of(i * L, L)
                src_v[pl.ds(o, L)] = zi
                tok_v[pl.ds(o, L)] = zi
                gate_v[pl.ds(o, L)] = zf

            @pl.loop(0, EB // L)
            def _(i):
                o = pl.multiple_of(i * L, L)
                p = pos_v[pl.ds(o, L)]
                r = rank_v[pl.ds(o, L)]
                m = p >= 0
                e = i // (tt // L)
                t0 = j * tt + (i % (tt // L)) * L
                plsc.store_scatter(src_v, [r], p + e * cap, mask=m)
                plsc.store_scatter(tok_v, [r], lane + t0, mask=m)
                plsc.store_scatter(gate_v, [r], aff_v[pl.ds(o, L)], mask=m)

            @pl.loop(0, (hi - lo) // G)
            def _(g):
                o = pl.multiple_of(g * G, G)
                dst = pl.multiple_of(lo + o, G)
                pltpu.sync_copy(y_hbm.at[src_v.at[pl.ds(o, G)]], buf)
                pltpu.sync_copy(buf, yg_hbm.at[pl.ds(dst, G)])
                pltpu.sync_copy(tok_v.at[pl.ds(o, G)], tok_hbm.at[pl.ds(dst, G)])
                pltpu.sync_copy(gate_v.at[pl.ds(o, G)], gate_hbm.at[pl.ds(dst, G)])

    return run(posb, rankb, affb, off_pad, yw)


def _combine_kernel(pj_ref, pw_ref, pf_ref, plo_ref, phi_ref, tok_ref, gate_ref, yg_ref, h_ref, nf_ref,
                    o_ref, acc_ref, *, group):
    p = pl.program_id(0)
    flag = pf_ref[p]
    tw, tt = MOE_TW, MOE_TT
    half = D_MODEL // 2

    hw = group // 2

    def col_blocks():
        for n in range(D_MODEL // group):
            yield slice(n * hw, (n + 1) * hw), slice(n * group, n * group + hw)
            yield slice(half + n * hw, half + (n + 1) * hw), slice(n * group + hw, (n + 1) * group)

    @pl.when((flag & 1) != 0)
    def _():
        for packed, natural in col_blocks():
            acc_ref[:, packed] = h_ref[:, natural]

    @pl.when((flag & 4) != 0)
    def _():
        lo, hi = plo_ref[p], phi_ref[p]
        row0 = pw_ref[p] * tw
        rid = lax.broadcasted_iota(I32, (tw, 1), 0) + row0
        keep = (rid >= lo) & (rid < hi)
        y_lo, y_hi = _unpack_bf16_pairs(yg_ref[...])
        y_lo = jnp.where(keep, y_lo, 0.0).astype(BF16)
        y_hi = jnp.where(keep, y_hi, 0.0).astype(BF16)
        tid = lax.broadcasted_iota(I32, (tt, tw), 0) + pj_ref[p] * tt
        cid = lax.broadcasted_iota(I32, (tt, tw), 1) + row0
        hit = (tok_ref[...] == tid) & (cid >= lo) & (cid < hi)
        weights = jnp.where(hit, gate_ref[...], 0.0).astype(BF16)
        acc_ref[:, :half] += _dot(weights, y_lo)
        acc_ref[:, half:] += _dot(weights, y_hi)

    @pl.when((flag & 2) != 0)
    def _():
        y = acc_ref[...]
        scale = lax.rsqrt(jnp.mean(y * y, axis=-1, keepdims=True) + EPS)
        for packed, natural in col_blocks():
            o_ref[:, natural] = acc_ref[:, packed] * scale * nf_ref[:, natural]


def _combine(lists, tok, gate, yg, h, nfw, group):
    pj, pw, pf, plo, phi = lists
    T = h.shape[0]
    tw, tt = MOE_TW, MOE_TT
    nwin = yg.shape[0] // tw
    grid_spec = pltpu.PrefetchScalarGridSpec(
        num_scalar_prefetch=5,
        grid=(pj.shape[0],),
        in_specs=[
            pl.BlockSpec((None, 1, tw), lambda p, pj, pw, *_: (pw[p], 0, 0)),
            pl.BlockSpec((None, 1, tw), lambda p, pj, pw, *_: (pw[p], 0, 0)),
            pl.BlockSpec((tw, D_MODEL // 2), lambda p, pj, pw, *_: (pw[p], 0)),
            pl.BlockSpec((tt, D_MODEL), lambda p, pj, pw, *_: (pj[p], 0)),
            pl.BlockSpec((1, D_MODEL), lambda p, pj, pw, *_: (0, 0)),
        ],
        out_specs=pl.BlockSpec((tt, D_MODEL), lambda p, pj, pw, *_: (pj[p], 0)),
        scratch_shapes=[pltpu.VMEM((tt, D_MODEL), F32)],
    )
    return pl.pallas_call(
        functools.partial(_combine_kernel, group=group),
        grid_spec=grid_spec,
        out_shape=jax.ShapeDtypeStruct((T, D_MODEL), F32),
        compiler_params=_cparams(("arbitrary",), 32),
        name="combine",
    )(pj, pw, pf, plo, phi, tok.reshape(nwin, 1, tw), gate.reshape(nwin, 1, tw), yg, h, nfw)


def _rope_tables(seq_len):
    d = RET_DK
    inv = ROPE_BASE ** (-jnp.arange(0, d, 2, dtype=F32) / d)
    ang = jnp.arange(seq_len, dtype=F32)[:, None] * inv[None, :]
    return jnp.cos(ang), jnp.sin(ang)


def _chunk_tri(n, chunk, upper):
    r = np.arange(n)
    same = (r[:, None] // chunk) == (r[None, :] // chunk)
    tri = (r[:, None] <= r[None, :]) if upper else (r[:, None] >= r[None, :])
    return jnp.asarray(same & tri, BF16)


def _prep_params(norm1_w, w_in, ret_gn_w, gla_gate_up, gla_gate_bias, gla_gn_w, w_out, norm2_w, router_w,
                 normf_w):
    w = w_in[0]
    w_main = w[:, :IN_MAIN].astype(BF16)
    w_ga = jnp.pad(w[:, IN_MAIN:], ((0, 0), (0, LANE - 2 * GLA_RANK))).astype(BF16)
    cs = np.ones((1, IN_MAIN), np.float32)
    cs[:, _RQ:_RQ + RET_WIDTH] = RET_DK ** -0.5
    cs[:, _GQ:_GQ + GLA_KEY_WIDTH] = GLA_DK ** -0.5
    up = gla_gate_up[0].astype(F32)
    up_pad = jnp.zeros((LANE, 2 * GLA_KEY_WIDTH), F32)
    up_pad = up_pad.at[:GLA_RANK, :GLA_KEY_WIDTH].set(up[0])
    up_pad = up_pad.at[GLA_RANK:2 * GLA_RANK, GLA_KEY_WIDTH:].set(up[1])
    rt = router_w[0].T.astype(F32)
    r_hi = rt.astype(BF16)
    r_lo = (rt - r_hi.astype(F32)).astype(BF16)
    return dict(
        n1w=norm1_w[0].reshape(1, D_MODEL).astype(F32),
        w_main=w_main, w_ga=w_ga, colscale=jnp.asarray(cs),
        up_pad=up_pad.astype(BF16),
        bias=gla_gate_bias[0].reshape(1, 2 * GLA_KEY_WIDTH).astype(F32),
        lf=_chunk_tri(GATE_TM, GLA_CHUNK, upper=False),
        lb=_chunk_tri(GATE_TM, GLA_CHUNK, upper=True),
        ret_gn=ret_gn_w[0].reshape(1, RET_WIDTH).astype(F32),
        gla_gn=gla_gn_w[0].reshape(1, GLA_WIDTH).astype(F32),
        w_out=w_out[0].astype(BF16),
        n2w=norm2_w[0].reshape(1, D_MODEL).astype(F32),
        r_hi=r_hi, r_lo=r_lo,
        nfw=normf_w.reshape(1, D_MODEL).astype(F32),
    )


def _trunk(x, pp, decay_logit, w_gate, w_up, w_down):
    B, L, _ = x.shape
    T = B * L
    x2d = x.reshape(T, D_MODEL)
    cos, sin = _rope_tables(L)
    proj, ga = _in_proj(x2d, pp["n1w"], pp["w_main"], pp["w_ga"], pp["colscale"], cos, sin, L)
    b_f, b_b = _gla_gates(ga, pp["up_pad"], pp["bias"], pp["lf"], pp["lb"])

    ret_f = _ret_scan(proj, decay_logit, B, L, reverse=False)
    mix_r = _ret_scan(proj, decay_logit, B, L, reverse=True, o_fwd=ret_f, gn_w=pp["ret_gn"])
    gla_f = _gla_scan(proj, b_f, B, L, reverse=False)
    mix_g = _gla_scan(proj, b_b, B, L, reverse=True, o_fwd=gla_f, gn_w=pp["gla_gn"])

    h, xn2, aff = _out_proj(mix_r, mix_g, pp["w_out"], x2d, pp["n2w"], pp["r_hi"], pp["r_lo"])

    cap = CAPACITY_FACTOR * T // N_EXPERTS
    pos, posb, rankb, affb = _select(aff, cap)
    off, c_lists = _combine_schedule(posb, T)
    xe = _dispatch(pos, xn2, cap).reshape(N_EXPERTS, cap, D_MODEL // 2)
    ye = _ffn(xe, w_gate, w_up, w_down).reshape(N_EXPERTS * cap, D_MODEL // 2)
    nb = T // MOE_TT
    yg, tok, gate = _regroup(posb.reshape(nb, -1), rankb.reshape(nb, -1), affb.reshape(nb, -1), off, ye, cap,
                             _regroup_rows(T))
    y = _combine(c_lists, tok, gate, yg, h, pp["nfw"], _ffn_tile_width(cap))
    return y.reshape(B, L, D_MODEL)


def kernel(x_prompt, x_sample, norm1_w, w_in, ret_decay_logit, ret_gn_w, gla_gate_up, gla_gate_bias,
           gla_gn_w, w_out, norm2_w, router_w, w_gate, w_up, w_down, normf_w):
    pp = _prep_params(norm1_w, w_in, ret_gn_w, gla_gate_up, gla_gate_bias, gla_gn_w, w_out, norm2_w,
                      router_w, normf_w)
    decay_logit = ret_decay_logit[0].astype(F32)
    args = (pp, decay_logit, w_gate[0], w_up[0], w_down[0])
    return (_trunk(x_prompt, *args), _trunk(x_sample, *args))
```

```python
import functools

import numpy as np
import jax
import jax.numpy as jnp
from jax import lax
from jax.experimental import pallas as pl
from jax.experimental.pallas import tpu as pltpu
from jax.experimental.pallas import tpu_sc as plsc

F32, BF16, I32 = jnp.float32, jnp.bfloat16, jnp.int32

D_MODEL = 2048
RET_WIDTH = 1024
RET_HEADS = 4
RET_DK = 256
RET_DV = 256
GLA_WIDTH = 1024
GLA_HEADS = 4
GLA_DK = 128
GLA_DV = 256
GLA_KEY_WIDTH = 512
GLA_RANK = 16
GLA_TAU = 16.0
RET_CHUNK = 256
GLA_CHUNK = 64
GLA_SUB = 16
ROPE_BASE = 10000.0
N_EXPERTS = 16
CAPACITY_FACTOR = 2
D_FF = 2048
EPS = 1e-6
LOG2_E = 1.4426950408889634
IN_MAIN = 4 * RET_WIDTH + 2 * GLA_KEY_WIDTH + 2 * GLA_WIDTH

_RQ, _RK, _RV, _RG = 0, 1024, 2048, 3072
_GQ, _GK, _GV, _GG = 4096, 4608, 5120, 6144

LANE = 128
MOE_TT = 256
MOE_TW = 256
V7X_VMEM_BYTES = 64 * 1024 * 1024


def _cparams(sem, vmem_mb):
    return pltpu.CompilerParams(dimension_semantics=sem, vmem_limit_bytes=vmem_mb * 1024 * 1024)


def _log_sigmoid(z):
    return jnp.minimum(z, 0.0) - jnp.log1p(jnp.exp(-jnp.abs(z)))


def _silu(g):
    return g * (1.0 / (1.0 + jnp.exp(-g)))


def _dot_nt(a, b):
    return lax.dot_general(a, b, (((1,), (1,)), ((), ())), preferred_element_type=F32)


def _dot_tn(a, b):
    return lax.dot_general(a, b, (((0,), (0,)), ((), ())), preferred_element_type=F32)


def _dot(a, b):
    return jnp.dot(a, b, preferred_element_type=F32)


def _pack_bf16_pairs(x):
    bits = pltpu.bitcast(x.astype(BF16).astype(F32), I32)
    w = x.shape[1] // 2
    return bits[:, w:] | lax.shift_right_logical(bits[:, :w], 16)


def _unpack_bf16_pairs(words):
    lo = pltpu.bitcast(lax.shift_left(words, 16), F32)
    hi = pltpu.bitcast(words & jnp.int32(-65536), F32)
    return lo, hi


IP_TM = 1024
IP_TN = 1024


def _in_proj_kernel(x_ref, n1_ref, w_ref, wga_ref, cs_ref, cos_ref, sin_ref, o_ref, ga_ref, xn_ref):
    j = pl.program_id(1)

    @pl.when(j == 0)
    def _():
        x = x_ref[...]
        ms = jnp.mean(x * x, axis=-1, keepdims=True)
        xn = (x * lax.rsqrt(ms + EPS) * n1_ref[...]).astype(BF16)
        xn_ref[...] = xn
        ga_ref[...] = _dot(xn, wga_ref[...])

    acc = _dot(xn_ref[...], w_ref[...]) * cs_ref[...]
    n_rope_blocks = 2 * RET_WIDTH // IP_TN

    @pl.when(j < n_rope_blocks)
    def _():
        cos = cos_ref[...]
        sin = sin_ref[...]
        half = RET_DK // 2
        for h in range(IP_TN // RET_DK):
            lo = h * RET_DK
            x1 = acc[:, lo:lo + half]
            x2 = acc[:, lo + half:lo + RET_DK]
            o_ref[:, lo:lo + half] = (x1 * cos - x2 * sin).astype(BF16)
            o_ref[:, lo + half:lo + RET_DK] = (x1 * sin + x2 * cos).astype(BF16)

    @pl.when(j >= n_rope_blocks)
    def _():
        o_ref[...] = acc.astype(BF16)


def _in_proj(x2d, n1w, w_main, w_ga, colscale, cos, sin, seq_len):
    T = x2d.shape[0]
    tm, tn = IP_TM, IP_TN
    nlb = seq_len // tm
    return pl.pallas_call(
        _in_proj_kernel,
        grid=(T // tm, IN_MAIN // tn),
        in_specs=[
            pl.BlockSpec((tm, D_MODEL), lambda i, j: (i, 0)),
            pl.BlockSpec((1, D_MODEL), lambda i, j: (0, 0)),
            pl.BlockSpec((D_MODEL, tn), lambda i, j: (0, j)),
            pl.BlockSpec((D_MODEL, LANE), lambda i, j: (0, 0)),
            pl.BlockSpec((1, tn), lambda i, j: (0, j)),
            pl.BlockSpec((tm, LANE), lambda i, j: (i % nlb, 0)),
            pl.BlockSpec((tm, LANE), lambda i, j: (i % nlb, 0)),
        ],
        out_specs=[
            pl.BlockSpec((tm, tn), lambda i, j: (i, j)),
            pl.BlockSpec((tm, LANE), lambda i, j: (i, 0)),
        ],
        out_shape=[
            jax.ShapeDtypeStruct((T, IN_MAIN), BF16),
            jax.ShapeDtypeStruct((T, LANE), F32),
        ],
        scratch_shapes=[pltpu.VMEM((tm, D_MODEL), BF16)],
        compiler_params=_cparams(("parallel", "arbitrary"), 48),
        name="in_proj",
    )(x2d, n1w, w_main, w_ga, colscale, cos, sin)


GATE_TM = 256


def _gates_kernel(ga_ref, up_ref, bias_ref, lf_ref, lb_ref, bf_ref, bb_ref):
    z = _dot(ga_ref[...].astype(BF16), up_ref[...]) + bias_ref[...]
    la = _log_sigmoid(z) * (LOG2_E / GLA_TAU)
    hi = la.astype(BF16)
    lo = (la - hi.astype(F32)).astype(BF16)
    kw = GLA_KEY_WIDTH
    bf_ref[...] = _dot(lf_ref[...], hi[:, :kw]) + _dot(lf_ref[...], lo[:, :kw])
    bb_ref[...] = _dot(lb_ref[...], hi[:, kw:]) + _dot(lb_ref[...], lo[:, kw:])


def _gla_gates(ga, up_pad, bias, lf, lb):
    T = ga.shape[0]
    tm = GATE_TM
    kw = GLA_KEY_WIDTH
    return pl.pallas_call(
        _gates_kernel,
        grid=(T // tm,),
        in_specs=[
            pl.BlockSpec((tm, LANE), lambda i: (i, 0)),
            pl.BlockSpec((LANE, 2 * kw), lambda i: (0, 0)),
            pl.BlockSpec((1, 2 * kw), lambda i: (0, 0)),
            pl.BlockSpec((tm, tm), lambda i: (0, 0)),
            pl.BlockSpec((tm, tm), lambda i: (0, 0)),
        ],
        out_specs=[pl.BlockSpec((tm, kw), lambda i: (i, 0)), pl.BlockSpec((tm, kw), lambda i: (i, 0))],
        out_shape=[jax.ShapeDtypeStruct((T, kw), F32), jax.ShapeDtypeStruct((T, kw), F32)],
        compiler_params=_cparams(("parallel",), 32),
        name="gla_gates",
    )(ga, up_pad, bias, lf, lb)


def _finish_heads(tot, gn, gate):
    ms = jnp.mean(tot * tot, axis=-1, keepdims=True)
    yn = tot * lax.rsqrt(ms + EPS) * gn
    return (yn * _silu(gate.astype(F32))).astype(BF16)


RET_TB = 1024


def _ret_kernel(dl_ref, q_ref, k_ref, v_ref, *rest, reverse):
    if reverse:
        g_ref, of_ref, gn_ref, o_ref, s_ref, intra_ref, qd_ref, kd_ref, cd_ref = rest
    else:
        o_ref, s_ref, intra_ref, qd_ref, kd_ref, cd_ref = rest
    h = pl.program_id(1)
    n = pl.program_id(2)
    C = RET_CHUNK

    @pl.when(n == 0)
    def _():
        s_ref[...] = jnp.zeros_like(s_ref)
        logit = dl_ref[1 if reverse else 0, h]
        lg = _log_sigmoid(jnp.full((C, RET_DV), logit, F32))
        lg_c = _log_sigmoid(jnp.full((C, C), logit, F32))
        lg_r = _log_sigmoid(jnp.full((1, RET_DV), logit, F32))
        ri = lax.broadcasted_iota(I32, (C, RET_DV), 0).astype(F32)
        rc = lax.broadcasted_iota(I32, (C, C), 0).astype(F32)
        cc = lax.broadcasted_iota(I32, (C, C), 1).astype(F32)
        diff = (cc - rc) if reverse else (rc - cc)
        intra_ref[...] = jnp.where(diff >= 0, jnp.exp(lg_c * diff), 0.0)
        if reverse:
            qd_ref[...] = jnp.exp(lg * (C - ri))
            kd_ref[...] = jnp.exp(lg * ri)
        else:
            qd_ref[...] = jnp.exp(lg * (ri + 1.0))
            kd_ref[...] = jnp.exp(lg * (C - 1.0 - ri))
        cd_ref[...] = jnp.exp(lg_r * C)

    nchunks = q_ref.shape[0] // C
    order = range(nchunks - 1, -1, -1) if reverse else range(nchunks)
    for c in order:
        rows = slice(c * C, (c + 1) * C)
        q = q_ref[rows, :]
        k = k_ref[rows, :]
        v = v_ref[rows, :]
        s = _dot_nt(q, k) * intra_ref[...]
        state = s_ref[...]
        o = _dot(s.astype(BF16), v) + _dot(q, state.astype(BF16)) * qd_ref[...]
        kd = (k.astype(F32) * kd_ref[...]).astype(BF16)
        s_ref[...] = state * cd_ref[...] + _dot_tn(kd, v)
        if reverse:
            o_ref[rows, :] = _finish_heads(of_ref[rows, :] + o, gn_ref[...], g_ref[rows, :])
        else:
            o_ref[rows, :] = o


def _ret_scan(proj, decay_logit, batch, seq_len, reverse, o_fwd=None, gn_w=None):
    T = proj.shape[0]
    tb = RET_TB
    nb = seq_len // tb
    dk, dv, C = RET_DK, RET_DV, RET_CHUNK

    def rb(b, n):
        return b * nb + ((nb - 1 - n) if reverse else n)

    def col(base):
        return lambda b, h, n: (rb(b, n), base // dk + h)

    in_specs = [
        pl.BlockSpec(memory_space=pltpu.SMEM),
        pl.BlockSpec((tb, dk), col(_RQ)),
        pl.BlockSpec((tb, dk), col(_RK)),
        pl.BlockSpec((tb, dv), col(_RV)),
    ]
    args = [decay_logit, proj, proj, proj]
    if reverse:
        in_specs += [
            pl.BlockSpec((tb, dv), col(_RG)),
            pl.BlockSpec((tb, dv), lambda b, h, n: (rb(b, n), h)),
            pl.BlockSpec((1, dv), lambda b, h, n: (0, h)),
        ]
        args += [proj, o_fwd, gn_w]
    out_dtype = BF16 if reverse else F32
    return pl.pallas_call(
        functools.partial(_ret_kernel, reverse=reverse),
        grid=(batch, RET_HEADS, nb),
        in_specs=in_specs,
        out_specs=pl.BlockSpec((tb, dv), lambda b, h, n: (rb(b, n), h)),
        out_shape=jax.ShapeDtypeStruct((T, RET_WIDTH), out_dtype),
        scratch_shapes=[
            pltpu.VMEM((dk, dv), F32),
            pltpu.VMEM((C, C), F32),
            pltpu.VMEM((C, dv), F32),
            pltpu.VMEM((C, dk), F32),
            pltpu.VMEM((1, dv), F32),
        ],
        compiler_params=_cparams(("parallel", "parallel", "arbitrary"), 32),
        name="ret_bwd" if reverse else "ret_fwd",
    )(*args)


GLA_TB = 1024
GLA_UNROLL = 16


GLA_LEVELS = (32, 16, 8, 4, 2, 1)
SUBLANES = 8


def _gla_tables(reverse):
    C = GLA_CHUNK
    r = np.arange(C)
    masks = np.zeros((len(GLA_LEVELS) + 1, C, C), np.float32)
    for l, s in enumerate(GLA_LEVELS):
        upper = (r & s) != 0
        same = (r[:, None] // (2 * s)) == (r[None, :] // (2 * s))
        lhs_rows = ~upper if reverse else upper
        masks[l] = same & lhs_rows[:, None] & ~lhs_rows[None, :]
    masks[-1] = np.eye(C)
    return jnp.asarray(masks, F32)


def _gla_kernel(q_ref, k_ref, v_ref, b_ref, mask_ref, *rest, reverse):
    if reverse:
        g_ref, of_ref, gn_ref, o_ref, st_ref, sc_ref = rest
    else:
        o_ref, st_ref, sc_ref = rest
    n = pl.program_id(2)
    C = GLA_CHUNK

    @pl.when(n == 0)
    def _():
        st_ref[...] = jnp.zeros_like(st_ref)

    nchunks = q_ref.shape[0] // C
    sub_row = lax.broadcasted_iota(I32, (SUBLANES, GLA_DK), 0)
    zero_rows = jnp.zeros((SUBLANES, GLA_DK), F32)

    def chunk_scores(c, carry):
        c0 = pl.multiple_of(c * C, C)
        qb = q_ref[pl.ds(c0, C), :]
        kb = k_ref[pl.ds(c0, C), :]
        q = qb.astype(F32)
        k = kb.astype(F32)
        b = b_ref[pl.ds(c0, C), :]

        def mid_row(r):
            return jnp.broadcast_to(b[r:r + 1, :], (SUBLANES, GLA_DK))

        scores = mask_ref[len(GLA_LEVELS)] * _dot_nt(qb, kb)
        for l, s in enumerate(GLA_LEVELS):
            lhs, rhs = [], []
            for g in range(C // SUBLANES):
                r0 = g * SUBLANES
                rows = slice(r0, r0 + SUBLANES)
                if s >= SUBLANES:
                    m = mid_row((r0 // (2 * s)) * (2 * s) + s)
                    is_lhs = ((r0 & s) != 0) != reverse
                    if is_lhs:
                        lhs.append(q[rows] * jnp.exp2(b[rows] - m))
                        rhs.append(zero_rows)
                    else:
                        lhs.append(zero_rows)
                        rhs.append(k[rows] * jnp.exp2(m - b[rows]))
                else:
                    m = mid_row(r0 + SUBLANES - s)
                    for blk in range(SUBLANES // (2 * s) - 2, -1, -1):
                        m = jnp.where(sub_row < (blk + 1) * 2 * s, mid_row(r0 + blk * 2 * s + s), m)
                    upper = (sub_row & s) != 0
                    is_lhs = jnp.logical_not(upper) if reverse else upper
                    lhs.append(jnp.where(is_lhs, q[rows] * jnp.exp2(b[rows] - m), 0.0))
                    rhs.append(jnp.where(is_lhs, 0.0, k[rows] * jnp.exp2(m - b[rows])))
            lhs = jnp.concatenate(lhs, axis=0).astype(BF16)
            rhs = jnp.concatenate(rhs, axis=0).astype(BF16)
            scores = scores + mask_ref[l] * _dot_nt(lhs, rhs)
        sc_ref[c] = scores.astype(BF16)
        return carry

    lax.fori_loop(0, nchunks, chunk_scores, 0, unroll=GLA_UNROLL)

    def chunk(ci, carry):
        c = (nchunks - 1 - ci) if reverse else ci
        c0 = pl.multiple_of(c * C, C)
        q = q_ref[pl.ds(c0, C), :].astype(F32)
        k = k_ref[pl.ds(c0, C), :].astype(F32)
        v = v_ref[pl.ds(c0, C), :]
        b = b_ref[pl.ds(c0, C), :]
        b_end = b[0:1, :] if reverse else b[C - 1:C, :]

        st = st_ref[...]
        o = _dot_nt((q * jnp.exp2(b)).astype(BF16), st.astype(BF16))
        ke = (k * jnp.exp2(b_end - b)).astype(BF16)
        st_ref[...] = st * jnp.exp2(b_end) + _dot_tn(v, ke)
        o = o + _dot(sc_ref[c], v)
        if reverse:
            tot = of_ref[pl.ds(c0, C), :] + o
            o_ref[pl.ds(c0, C), :] = _finish_heads(tot, gn_ref[...], g_ref[pl.ds(c0, C), :])
        else:
            o_ref[pl.ds(c0, C), :] = o
        return carry

    lax.fori_loop(0, nchunks, chunk, 0, unroll=GLA_UNROLL)


def _gla_scan(proj, bcum, batch, seq_len, reverse, o_fwd=None, gn_w=None):
    T = proj.shape[0]
    tb = GLA_TB
    nb = seq_len // tb
    dk, dv = GLA_DK, GLA_DV
    masks = _gla_tables(reverse)

    def rb(b, n):
        return b * nb + ((nb - 1 - n) if reverse else n)

    in_specs = [
        pl.BlockSpec((tb, dk), lambda b, h, n: (rb(b, n), _GQ // dk + h)),
        pl.BlockSpec((tb, dk), lambda b, h, n: (rb(b, n), _GK // dk + h)),
        pl.BlockSpec((tb, dv), lambda b, h, n: (rb(b, n), _GV // dv + h)),
        pl.BlockSpec((tb, dk), lambda b, h, n: (rb(b, n), h)),
        pl.BlockSpec(masks.shape, lambda b, h, n: (0, 0, 0)),
    ]
    args = [proj, proj, proj, bcum, masks]
    if reverse:
        in_specs += [
            pl.BlockSpec((tb, dv), lambda b, h, n: (rb(b, n), _GG // dv + h)),
            pl.BlockSpec((tb, dv), lambda b, h, n: (rb(b, n), h)),
            pl.BlockSpec((1, dv), lambda b, h, n: (0, h)),
        ]
        args += [proj, o_fwd, gn_w]
    out_dtype = BF16 if reverse else F32
    return pl.pallas_call(
        functools.partial(_gla_kernel, reverse=reverse),
        grid=(batch, GLA_HEADS, nb),
        in_specs=in_specs,
        out_specs=pl.BlockSpec((tb, dv), lambda b, h, n: (rb(b, n), h)),
        out_shape=jax.ShapeDtypeStruct((T, GLA_WIDTH), out_dtype),
        scratch_shapes=[pltpu.VMEM((dv, dk), F32), pltpu.VMEM((tb // GLA_CHUNK, GLA_CHUNK, GLA_CHUNK), BF16)],
        compiler_params=_cparams(("parallel", "parallel", "arbitrary"), 32),
        name="gla_bwd" if reverse else "gla_fwd",
    )(*args)


OP_TM = 512
OP_SUB = 256


def _out_proj_kernel(mr_ref, mg_ref, w0_ref, w1_ref, x_ref, n2_ref, rh_ref, rl_ref, h_ref, xn_ref, aff_ref):
    for r in range(OP_TM // OP_SUB):
        rows = slice(r * OP_SUB, (r + 1) * OP_SUB)
        h = x_ref[rows, :] + _dot(mr_ref[rows, :], w0_ref[...]) + _dot(mg_ref[rows, :], w1_ref[...])
        h_ref[rows, :] = h
        ms = jnp.mean(h * h, axis=-1, keepdims=True)
        xn = h * lax.rsqrt(ms + EPS) * n2_ref[...]
        xh = xn.astype(BF16)
        xn_ref[rows, :] = _pack_bf16_pairs(xn)
        xl = (xn - xh.astype(F32)).astype(BF16)
        lt = _dot_nt(rh_ref[...], xh) + _dot_nt(rh_ref[...], xl) + _dot_nt(rl_ref[...], xh)
        m = jnp.max(lt, axis=0, keepdims=True)
        e = jnp.exp(lt - m)
        aff_ref[:, rows] = e / jnp.sum(e, axis=0, keepdims=True)


def _out_proj(mix_r, mix_g, w_out, x2d, n2w, r_hi, r_lo):
    T = x2d.shape[0]
    tm = OP_TM
    half = RET_WIDTH
    return pl.pallas_call(
        _out_proj_kernel,
        grid=(T // tm,),
        in_specs=[
            pl.BlockSpec((tm, half), lambda i: (i, 0)),
            pl.BlockSpec((tm, half), lambda i: (i, 0)),
            pl.BlockSpec((half, D_MODEL), lambda i: (0, 0)),
            pl.BlockSpec((half, D_MODEL), lambda i: (1, 0)),
            pl.BlockSpec((tm, D_MODEL), lambda i: (i, 0)),
            pl.BlockSpec((1, D_MODEL), lambda i: (0, 0)),
            pl.BlockSpec((N_EXPERTS, D_MODEL), lambda i: (0, 0)),
            pl.BlockSpec((N_EXPERTS, D_MODEL), lambda i: (0, 0)),
        ],
        out_specs=[
            pl.BlockSpec((tm, D_MODEL), lambda i: (i, 0)),
            pl.BlockSpec((tm, D_MODEL // 2), lambda i: (i, 0)),
            pl.BlockSpec((N_EXPERTS, tm), lambda i: (0, i)),
        ],
        out_shape=[
            jax.ShapeDtypeStruct((T, D_MODEL), F32),
            jax.ShapeDtypeStruct((T, D_MODEL // 2), I32),
            jax.ShapeDtypeStruct((N_EXPERTS, T), F32),
        ],
        compiler_params=_cparams(("parallel",), 56),
        name="out_proj",
    )(mix_r, mix_g, w_out, w_out, x2d, n2w, r_hi, r_lo)


def _select_kernel(a_ref, pos_ref, posb_ref, rankb_ref, affb_ref, *, cap):
    E, T = a_ref.shape
    tt = MOE_TT

    def count(pred):
        return jnp.sum(pred.astype(F32), axis=1, keepdims=True)

    def bisect(i, tau):
        cand = tau | jnp.left_shift(jnp.int32(1), 30 - i)
        bits = pltpu.bitcast(a_ref[...], I32)
        return jnp.where(count(bits >= cand) >= cap, cand, tau)

    tau = lax.fori_loop(0, 31, bisect, jnp.zeros((E, 1), I32))
    bits_all = pltpu.bitcast(a_ref[...], I32)
    quota = cap - count(bits_all > tau)

    before = (lax.broadcasted_iota(I32, (tt, tt), 0) < lax.broadcasted_iota(I32, (tt, tt), 1)).astype(BF16)
    below = (lax.broadcasted_iota(I32, (E, E), 1) < lax.broadcasted_iota(I32, (E, E), 0)).astype(BF16)

    def block(j, carry):
        c_eq, c_sel = carry
        off = pl.multiple_of(j * tt, tt)
        aff = a_ref[:, pl.ds(off, tt)]
        bits = pltpu.bitcast(aff, I32)
        eq = bits == tau
        eqf = eq.astype(F32)
        rank_eq = _dot(eqf.astype(BF16), before) + c_eq
        sel = (bits > tau) | (eq & (rank_eq < quota))
        self_ = sel.astype(F32)
        selb = self_.astype(BF16)
        slot = _dot(selb, before) + c_sel
        pos = jnp.where(sel, slot, -1.0).astype(I32)
        pos_ref[:, pl.ds(off, tt)] = pos
        per_tok = jnp.broadcast_to(jnp.sum(self_, axis=0, keepdims=True), (E, tt))
        rank = _dot(per_tok.astype(BF16), before) + _dot(below, selb)
        posb_ref[j] = pos
        rankb_ref[j] = jnp.where(sel, rank, -1.0).astype(I32)
        affb_ref[j] = aff
        return (c_eq + jnp.sum(eqf, axis=1, keepdims=True), c_sel + jnp.sum(self_, axis=1, keepdims=True))

    zero = jnp.zeros((E, 1), F32)
    lax.fori_loop(0, T // tt, block, (zero, zero))


def _select(aff, cap):
    E, T = aff.shape
    nb = T // MOE_TT
    blk = jax.ShapeDtypeStruct((nb, E, MOE_TT), I32)
    return pl.pallas_call(
        functools.partial(_select_kernel, cap=cap),
        out_shape=[jax.ShapeDtypeStruct((E, T), I32), blk, blk, jax.ShapeDtypeStruct((nb, E, MOE_TT), F32)],
        compiler_params=pltpu.CompilerParams(vmem_limit_bytes=40 * 1024 * 1024),
        name="select",
    )(aff)


def _regroup_rows(T):
    nb = T // MOE_TT
    rows = CAPACITY_FACTOR * T + SC_GATHER_ROWS * nb
    return -(-rows // MOE_TW) * MOE_TW


def _combine_schedule(posb, T):
    nb = posb.shape[0]
    tw, g = MOE_TW, SC_GATHER_ROWS
    n = jnp.sum((posb >= 0).reshape(nb, -1), axis=1).astype(I32)
    seg = (n + g - 1) // g * g
    hi = jnp.cumsum(seg)
    lo = hi - seg
    off = jnp.concatenate([jnp.zeros((1,), I32), hi])
    nwin_max = (N_EXPERTS * MOE_TT) // tw + 1
    w0 = lo // tw
    w1 = jnp.where(seg > 0, (hi - 1) // tw, w0)
    cand = jnp.arange(nwin_max, dtype=I32)
    win = w0[:, None] + cand[None, :]
    valid = (win <= w1[:, None]).reshape(-1)
    nwin_total = _regroup_rows(T) // tw
    pmax = nb + nwin_total
    jv = jnp.broadcast_to(jnp.arange(nb, dtype=I32)[:, None], win.shape).reshape(-1)
    wv = jnp.minimum(win, nwin_total - 1).reshape(-1)
    total = jnp.sum(valid.astype(I32))
    dst = jnp.where(valid, jnp.cumsum(valid.astype(I32)) - 1, pmax)
    pj, pw = (jnp.zeros((pmax,), I32).at[dst].set(a, mode="drop") for a in (jv, wv))
    real = jnp.arange(pmax, dtype=I32) < total
    pj, pw = (jnp.where(real, a, a[total - 1]) for a in (pj, pw))
    first = jnp.concatenate([jnp.ones((1,), bool), pj[1:] != pj[:-1]])
    last = jnp.concatenate([pj[1:] != pj[:-1], jnp.ones((1,), bool)]) | (jnp.arange(pmax, dtype=I32) == total - 1)
    flag = jnp.where(real, first.astype(I32) + 2 * last.astype(I32) + 4, 0)
    return off, (pj, pw, flag, lo[pj], hi[pj])


SC_LANES = 16
SC_CORES = 2
SC_SUBCORES = 16
SC_GATHER_ROWS = 32


def _dispatch(pos, xw, cap):
    E, T = pos.shape
    W = xw.shape[1]
    G = SC_GATHER_ROWS
    part_rows = cap // SC_CORES
    mesh = plsc.VectorSubcoreMesh(core_axis_name="c", subcore_axis_name="s")

    @pl.kernel(
        out_type=jax.ShapeDtypeStruct((E * cap, W), I32),
        mesh=mesh,
        scratch_types=[pltpu.VMEM((T,), I32), pltpu.VMEM((cap,), I32), pltpu.VMEM((G, W), I32)],
        compiler_params=pltpu.CompilerParams(needs_layout_passes=False),
        name="sc_dispatch",
    )
    def run(pos_hbm, x_hbm, xe_hbm, pos_v, idx_v, buf):
        e = lax.axis_index("s")
        part = lax.axis_index("c")
        pltpu.sync_copy(pos_hbm.at[e], pos_v)
        lane = lax.iota(I32, SC_LANES)

        @pl.loop(0, T // SC_LANES)
        def _(i):
            off = pl.multiple_of(i * SC_LANES, SC_LANES)
            p = pos_v[pl.ds(off, SC_LANES)]
            plsc.store_scatter(idx_v, [p], lane + off, mask=p >= 0)

        @pl.loop(0, part_rows // G)
        def _(g):
            o = pl.multiple_of(part * part_rows + g * G, G)
            pltpu.sync_copy(x_hbm.at[idx_v.at[pl.ds(o, G)]], buf)
            pltpu.sync_copy(buf, xe_hbm.at[pl.ds(e * cap + o, G)])

    return run(pos, xw)


FFN_TM = 2048
FFN_UNPACK_ROWS = 256


def _ffn_tile_width(cap):
    return 512 if min(FFN_TM, cap) <= 1024 else 256


def _ffn_kernel(x_ref, wg_ref, wu_ref, wd_ref, o_ref, xb_ref, hid_ref, *, tw):
    s = pl.program_id(2)
    n_up = D_FF // tw

    @pl.when(s == 0)
    def _():
        half = D_MODEL // 2

        def unpack_rows(i, carry):
            r = pl.multiple_of(i * FFN_UNPACK_ROWS, FFN_UNPACK_ROWS)
            lo, hi = _unpack_bf16_pairs(x_ref[pl.ds(r, FFN_UNPACK_ROWS), :])
            xb_ref[pl.ds(r, FFN_UNPACK_ROWS), :half] = lo.astype(BF16)
            xb_ref[pl.ds(r, FFN_UNPACK_ROWS), half:] = hi.astype(BF16)
            return carry

        lax.fori_loop(0, x_ref.shape[0] // FFN_UNPACK_ROWS, unpack_rows, 0)

    @pl.when(s < n_up)
    def _():
        x = xb_ref[...]
        g = _dot(x, wg_ref[...].astype(BF16))
        u = _dot(x, wu_ref[...].astype(BF16))
        col = pl.multiple_of(s * tw, tw)
        hid_ref[:, pl.ds(col, tw)] = (_silu(g) * u).astype(BF16)

    @pl.when(s >= n_up)
    def _():
        o_ref[...] = _pack_bf16_pairs(_dot(hid_ref[...], wd_ref[...].astype(BF16)))


def _ffn(xe, w_gate, w_up, w_down):
    E, cap, _ = xe.shape
    tm = min(FFN_TM, cap)
    tw = _ffn_tile_width(cap)
    n_up, n_down = D_FF // tw, D_MODEL // tw

    def up(e, m, s):
        return (e, 0, jnp.minimum(s, n_up - 1))

    def down(s):
        return jnp.maximum(s - n_up, 0)

    return pl.pallas_call(
        functools.partial(_ffn_kernel, tw=tw),
        grid=(E, cap // tm, n_up + n_down),
        in_specs=[
            pl.BlockSpec((None, tm, D_MODEL // 2), lambda e, m, s: (e, m, 0)),
            pl.BlockSpec((None, D_MODEL, tw), up),
            pl.BlockSpec((None, D_MODEL, tw), up),
            pl.BlockSpec((None, D_FF, tw), lambda e, m, s: (e, 0, down(s))),
        ],
        out_specs=pl.BlockSpec((None, tm, tw // 2), lambda e, m, s: (e, m, down(s))),
        out_shape=jax.ShapeDtypeStruct((E, cap, D_MODEL // 2), I32),
        scratch_shapes=[pltpu.VMEM((tm, D_MODEL), BF16), pltpu.VMEM((tm, D_FF), BF16)],
        compiler_params=_cparams(("parallel", "parallel", "arbitrary"), 60),
        name="ffn",
    )(xe, w_gate, w_up, w_down)


def _regroup(posb, rankb, affb, off, yw, cap, rows):
    NB, EB = posb.shape
    W = yw.shape[1]
    G, L, tt = SC_GATHER_ROWS, SC_LANES, MOE_TT
    n_workers = SC_CORES * SC_SUBCORES
    per = NB // n_workers
    mesh = plsc.VectorSubcoreMesh(core_axis_name="c", subcore_axis_name="s")
    off_pad = jnp.pad(off, (0, L))

    @pl.kernel(
        out_type=(jax.ShapeDtypeStruct((rows, W), I32), jax.ShapeDtypeStruct((rows,), I32),
                  jax.ShapeDtypeStruct((rows,), F32)),
        mesh=mesh,
        scratch_types=[pltpu.VMEM((EB,), I32), pltpu.VMEM((EB,), I32), pltpu.VMEM((EB,), F32),
                       pltpu.VMEM((EB,), I32), pltpu.VMEM((EB,), I32), pltpu.VMEM((EB,), F32),
                       pltpu.VMEM((G, W), I32), pltpu.VMEM((NB + 1 + L,), I32)],
        compiler_params=pltpu.CompilerParams(needs_layout_passes=False),
        name="sc_regroup",
    )
    def run(posb_hbm, rankb_hbm, affb_hbm, off_hbm, y_hbm, yg_hbm, tok_hbm, gate_hbm,
            pos_v, rank_v, aff_v, src_v, tok_v, gate_v, buf, off_v):
        wid = lax.axis_index("c") * SC_SUBCORES + lax.axis_index("s")
        pltpu.sync_copy(off_hbm, off_v)
        lane = lax.iota(I32, L)
        zi = jnp.zeros((L,), I32)
        zf = jnp.zeros((L,), F32)

        @pl.loop(0, per)
        def _(k):
            j = wid * per + k
            pltpu.sync_copy(posb_hbm.at[j], pos_v)
            pltpu.sync_copy(rankb_hbm.at[j], rank_v)
            pltpu.sync_copy(affb_hbm.at[j], aff_v)
            lo = jnp.max(plsc.load_gather(off_v, [zi + j]))
            hi = jnp.max(plsc.load_gather(off_v, [zi + j + 1]))

            @pl.loop(0, EB // L)
            def _(i):
                o = pl.multiple_of(i * L, L)
                src_v[pl.ds(o, L)] = zi
                tok_v[pl.ds(o, L)] = zi
                gate_v[pl.ds(o, L)] = zf

            @pl.loop(0, EB // L)
            def _(i):
                o = pl.multiple_of(i * L, L)
                p = pos_v[pl.ds(o, L)]
                r = rank_v[pl.ds(o, L)]
                m = p >= 0
                e = i // (tt // L)
                t0 = j * tt + (i % (tt // L)) * L
                plsc.store_scatter(src_v, [r], p + e * cap, mask=m)
                plsc.store_scatter(tok_v, [r], lane + t0, mask=m)
                plsc.store_scatter(gate_v, [r], aff_v[pl.ds(o, L)], mask=m)

            @pl.loop(0, (hi - lo) // G)
            def _(g):
                o = pl.multiple_of(g * G, G)
                dst = pl.multiple_of(lo + o, G)
                pltpu.sync_copy(y_hbm.at[src_v.at[pl.ds(o, G)]], buf)
                pltpu.sync_copy(buf, yg_hbm.at[pl.ds(dst, G)])
                pltpu.sync_copy(tok_v.at[pl.ds(o, G)], tok_hbm.at[pl.ds(dst, G)])
                pltpu.sync_copy(gate_v.at[pl.ds(o, G)], gate_hbm.at[pl.ds(dst, G)])

    return run(posb, rankb, affb, off_pad, yw)


def _combine_kernel(pj_ref, pw_ref, pf_ref, plo_ref, phi_ref, tok_ref, gate_ref, yg_ref, h_ref, nf_ref,
                    o_ref, acc_ref, *, group):
    p = pl.program_id(0)
    flag = pf_ref[p]
    tw, tt = MOE_TW, MOE_TT
    half = D_MODEL // 2

    hw = group // 2

    def col_blocks():
        for n in range(D_MODEL // group):
            yield slice(n * hw, (n + 1) * hw), slice(n * group, n * group + hw)
            yield slice(half + n * hw, half + (n + 1) * hw), slice(n * group + hw, (n + 1) * group)

    @pl.when((flag & 1) != 0)
    def _():
        for packed, natural in col_blocks():
            acc_ref[:, packed] = h_ref[:, natural]

    @pl.when((flag & 4) != 0)
    def _():
        lo, hi = plo_ref[p], phi_ref[p]
        row0 = pw_ref[p] * tw
        rid = lax.broadcasted_iota(I32, (tw, 1), 0) + row0
        keep = (rid >= lo) & (rid < hi)
        y_lo, y_hi = _unpack_bf16_pairs(yg_ref[...])
        y_lo = jnp.where(keep, y_lo, 0.0).astype(BF16)
        y_hi = jnp.where(keep, y_hi, 0.0).astype(BF16)
        tid = lax.broadcasted_iota(I32, (tt, tw), 0) + pj_ref[p] * tt
        cid = lax.broadcasted_iota(I32, (tt, tw), 1) + row0
        hit = (tok_ref[...] == tid) & (cid >= lo) & (cid < hi)
        weights = jnp.where(hit, gate_ref[...], 0.0).astype(BF16)
        acc_ref[:, :half] += _dot(weights, y_lo)
        acc_ref[:, half:] += _dot(weights, y_hi)

    @pl.when((flag & 2) != 0)
    def _():
        y = acc_ref[...]
        scale = lax.rsqrt(jnp.mean(y * y, axis=-1, keepdims=True) + EPS)
        for packed, natural in col_blocks():
            o_ref[:, natural] = acc_ref[:, packed] * scale * nf_ref[:, natural]


def _combine(lists, tok, gate, yg, h, nfw, group):
    pj, pw, pf, plo, phi = lists
    T = h.shape[0]
    tw, tt = MOE_TW, MOE_TT
    nwin = yg.shape[0] // tw
    grid_spec = pltpu.PrefetchScalarGridSpec(
        num_scalar_prefetch=5,
        grid=(pj.shape[0],),
        in_specs=[
            pl.BlockSpec((None, 1, tw), lambda p, pj, pw, *_: (pw[p], 0, 0)),
            pl.BlockSpec((None, 1, tw), lambda p, pj, pw, *_: (pw[p], 0, 0)),
            pl.BlockSpec((tw, D_MODEL // 2), lambda p, pj, pw, *_: (pw[p], 0)),
            pl.BlockSpec((tt, D_MODEL), lambda p, pj, pw, *_: (pj[p], 0)),
            pl.BlockSpec((1, D_MODEL), lambda p, pj, pw, *_: (0, 0)),
        ],
        out_specs=pl.BlockSpec((tt, D_MODEL), lambda p, pj, pw, *_: (pj[p], 0)),
        scratch_shapes=[pltpu.VMEM((tt, D_MODEL), F32)],
    )
    return pl.pallas_call(
        functools.partial(_combine_kernel, group=group),
        grid_spec=grid_spec,
        out_shape=jax.ShapeDtypeStruct((T, D_MODEL), F32),
        compiler_params=_cparams(("arbitrary",), 32),
        name="combine",
    )(pj, pw, pf, plo, phi, tok.reshape(nwin, 1, tw), gate.reshape(nwin, 1, tw), yg, h, nfw)


def _rope_tables(seq_len):
    d = RET_DK
    inv = ROPE_BASE ** (-jnp.arange(0, d, 2, dtype=F32) / d)
    ang = jnp.arange(seq_len, dtype=F32)[:, None] * inv[None, :]
    return jnp.cos(ang), jnp.sin(ang)


def _chunk_tri(n, chunk, upper):
    r = np.arange(n)
    same = (r[:, None] // chunk) == (r[None, :] // chunk)
    tri = (r[:, None] <= r[None, :]) if upper else (r[:, None] >= r[None, :])
    return jnp.asarray(same & tri, BF16)


def _prep_params(norm1_w, w_in, ret_gn_w, gla_gate_up, gla_gate_bias, gla_gn_w, w_out, norm2_w, router_w,
                 normf_w):
    w = w_in[0]
    w_main = w[:, :IN_MAIN].astype(BF16)
    w_ga = jnp.pad(w[:, IN_MAIN:], ((0, 0), (0, LANE - 2 * GLA_RANK))).astype(BF16)
    cs = np.ones((1, IN_MAIN), np.float32)
    cs[:, _RQ:_RQ + RET_WIDTH] = RET_DK ** -0.5
    cs[:, _GQ:_GQ + GLA_KEY_WIDTH] = GLA_DK ** -0.5
    up = gla_gate_up[0].astype(F32)
    up_pad = jnp.zeros((LANE, 2 * GLA_KEY_WIDTH), F32)
    up_pad = up_pad.at[:GLA_RANK, :GLA_KEY_WIDTH].set(up[0])
    up_pad = up_pad.at[GLA_RANK:2 * GLA_RANK, GLA_KEY_WIDTH:].set(up[1])
    rt = router_w[0].T.astype(F32)
    r_hi = rt.astype(BF16)
    r_lo = (rt - r_hi.astype(F32)).astype(BF16)
    return dict(
        n1w=norm1_w[0].reshape(1, D_MODEL).astype(F32),
        w_main=w_main, w_ga=w_ga, colscale=jnp.asarray(cs),
        up_pad=up_pad.astype(BF16),
        bias=gla_gate_bias[0].reshape(1, 2 * GLA_KEY_WIDTH).astype(F32),
        lf=_chunk_tri(GATE_TM, GLA_CHUNK, upper=False),
        lb=_chunk_tri(GATE_TM, GLA_CHUNK, upper=True),
        ret_gn=ret_gn_w[0].reshape(1, RET_WIDTH).astype(F32),
        gla_gn=gla_gn_w[0].reshape(1, GLA_WIDTH).astype(F32),
        w_out=w_out[0].astype(BF16),
        n2w=norm2_w[0].reshape(1, D_MODEL).astype(F32),
        r_hi=r_hi, r_lo=r_lo,
        nfw=normf_w.reshape(1, D_MODEL).astype(F32),
    )


def _trunk(x, pp, decay_logit, w_gate, w_up, w_down):
    B, L, _ = x.shape
    T = B * L
    x2d = x.reshape(T, D_MODEL)
    cos, sin = _rope_tables(L)
    proj, ga = _in_proj(x2d, pp["n1w"], pp["w_main"], pp["w_ga"], pp["colscale"], cos, sin, L)
    b_f, b_b = _gla_gates(ga, pp["up_pad"], pp["bias"], pp["lf"], pp["lb"])

    ret_f = _ret_scan(proj, decay_logit, B, L, reverse=False)
    mix_r = _ret_scan(proj, decay_logit, B, L, reverse=True, o_fwd=ret_f, gn_w=pp["ret_gn"])
    gla_f = _gla_scan(proj, b_f, B, L, reverse=False)
    mix_g = _gla_scan(proj, b_b, B, L, reverse=True, o_fwd=gla_f, gn_w=pp["gla_gn"])

    h, xn2, aff = _out_proj(mix_r, mix_g, pp["w_out"], x2d, pp["n2w"], pp["r_hi"], pp["r_lo"])

    cap = CAPACITY_FACTOR * T // N_EXPERTS
    pos, posb, rankb, affb = _select(aff, cap)
    off, c_lists = _combine_schedule(posb, T)
    xe = _dispatch(pos, xn2, cap).reshape(N_EXPERTS, cap, D_MODEL // 2)
    ye = _ffn(xe, w_gate, w_up, w_down).reshape(N_EXPERTS * cap, D_MODEL // 2)
    nb = T // MOE_TT
    yg, tok, gate = _regroup(posb.reshape(nb, -1), rankb.reshape(nb, -1), affb.reshape(nb, -1), off, ye, cap,
                             _regroup_rows(T))
    y = _combine(c_lists, tok, gate, yg, h, pp["nfw"], _ffn_tile_width(cap))
    return y.reshape(B, L, D_MODEL)


def kernel(x_prompt, x_sample, norm1_w, w_in, ret_decay_logit, ret_gn_w, gla_gate_up, gla_gate_bias,
           gla_gn_w, w_out, norm2_w, router_w, w_gate, w_up, w_down, normf_w):
    pp = _prep_params(norm1_w, w_in, ret_gn_w, gla_gate_up, gla_gate_bias, gla_gn_w, w_out, norm2_w,
                      router_w, normf_w)
    decay_logit = ret_decay_logit[0].astype(F32)
    args = (pp, decay_logit, w_gate[0], w_up[0], w_down[0])
    return (_trunk(x_prompt, *args), _trunk(x_sample, *args))
```

```python
import functools

import numpy as np
import jax
import jax.numpy as jnp
from jax import lax
from jax.experimental import pallas as pl
from jax.experimental.pallas import tpu as pltpu
from jax.experimental.pallas import tpu_sc as plsc

F32, BF16, I32 = jnp.float32, jnp.bfloat16, jnp.int32

D_MODEL = 2048
RET_WIDTH = 1024
RET_HEADS = 4
RET_DK = 256
RET_DV = 256
GLA_WIDTH = 1024
GLA_HEADS = 4
GLA_DK = 128
GLA_DV = 256
GLA_KEY_WIDTH = 512
GLA_RANK = 16
GLA_TAU = 16.0
RET_CHUNK = 256
GLA_CHUNK = 64
GLA_SUB = 16
ROPE_BASE = 10000.0
N_EXPERTS = 16
CAPACITY_FACTOR = 2
D_FF = 2048
EPS = 1e-6
LOG2_E = 1.4426950408889634
IN_MAIN = 4 * RET_WIDTH + 2 * GLA_KEY_WIDTH + 2 * GLA_WIDTH

_RQ, _RK, _RV, _RG = 0, 1024, 2048, 3072
_GQ, _GK, _GV, _GG = 4096, 4608, 5120, 6144

LANE = 128
MOE_TT = 256
MOE_TW = 256
V7X_VMEM_BYTES = 64 * 1024 * 1024


def _cparams(sem, vmem_mb):
    return pltpu.CompilerParams(dimension_semantics=sem, vmem_limit_bytes=vmem_mb * 1024 * 1024)


def _log_sigmoid(z):
    return jnp.minimum(z, 0.0) - jnp.log1p(jnp.exp(-jnp.abs(z)))


def _silu(g):
    return g * (1.0 / (1.0 + jnp.exp(-g)))


def _dot_nt(a, b):
    return lax.dot_general(a, b, (((1,), (1,)), ((), ())), preferred_element_type=F32)


def _dot_tn(a, b):
    return lax.dot_general(a, b, (((0,), (0,)), ((), ())), preferred_element_type=F32)


def _dot(a, b):
    return jnp.dot(a, b, preferred_element_type=F32)


def _pack_bf16_pairs(x):
    bits = pltpu.bitcast(x.astype(BF16).astype(F32), I32)
    w = x.shape[1] // 2
    return bits[:, w:] | lax.shift_right_logical(bits[:, :w], 16)


def _unpack_bf16_pairs(words):
    lo = pltpu.bitcast(lax.shift_left(words, 16), F32)
    hi = pltpu.bitcast(words & jnp.int32(-65536), F32)
    return lo, hi


IP_TM = 1024
IP_TN = 1024


def _in_proj_kernel(x_ref, n1_ref, w_ref, wga_ref, cs_ref, cos_ref, sin_ref, o_ref, ga_ref, xn_ref):
    j = pl.program_id(1)

    @pl.when(j == 0)
    def _():
        x = x_ref[...]
        ms = jnp.mean(x * x, axis=-1, keepdims=True)
        xn = (x * lax.rsqrt(ms + EPS) * n1_ref[...]).astype(BF16)
        xn_ref[...] = xn
        ga_ref[...] = _dot(xn, wga_ref[...])

    acc = _dot(xn_ref[...], w_ref[...]) * cs_ref[...]
    n_rope_blocks = 2 * RET_WIDTH // IP_TN

    @pl.when(j < n_rope_blocks)
    def _():
        cos = cos_ref[...]
        sin = sin_ref[...]
        half = RET_DK // 2
        for h in range(IP_TN // RET_DK):
            lo = h * RET_DK
            x1 = acc[:, lo:lo + half]
            x2 = acc[:, lo + half:lo + RET_DK]
            o_ref[:, lo:lo + half] = (x1 * cos - x2 * sin).astype(BF16)
            o_ref[:, lo + half:lo + RET_DK] = (x1 * sin + x2 * cos).astype(BF16)

    @pl.when(j >= n_rope_blocks)
    def _():
        o_ref[...] = acc.astype(BF16)


def _in_proj(x2d, n1w, w_main, w_ga, colscale, cos, sin, seq_len):
    T = x2d.shape[0]
    tm, tn = IP_TM, IP_TN
    nlb = seq_len // tm
    return pl.pallas_call(
        _in_proj_kernel,
        grid=(T // tm, IN_MAIN // tn),
        in_specs=[
            pl.BlockSpec((tm, D_MODEL), lambda i, j: (i, 0)),
            pl.BlockSpec((1, D_MODEL), lambda i, j: (0, 0)),
            pl.BlockSpec((D_MODEL, tn), lambda i, j: (0, j)),
            pl.BlockSpec((D_MODEL, LANE), lambda i, j: (0, 0)),
            pl.BlockSpec((1, tn), lambda i, j: (0, j)),
            pl.BlockSpec((tm, LANE), lambda i, j: (i % nlb, 0)),
            pl.BlockSpec((tm, LANE), lambda i, j: (i % nlb, 0)),
        ],
        out_specs=[
            pl.BlockSpec((tm, tn), lambda i, j: (i, j)),
            pl.BlockSpec((tm, LANE), lambda i, j: (i, 0)),
        ],
        out_shape=[
            jax.ShapeDtypeStruct((T, IN_MAIN), BF16),
            jax.ShapeDtypeStruct((T, LANE), F32),
        ],
        scratch_shapes=[pltpu.VMEM((tm, D_MODEL), BF16)],
        compiler_params=_cparams(("parallel", "arbitrary"), 48),
        name="in_proj",
    )(x2d, n1w, w_main, w_ga, colscale, cos, sin)


GATE_TM = 512


def _gates_kernel(ga_ref, up_ref, bias_ref, lf_ref, lb_ref, bf_ref, bb_ref):
    z = _dot(ga_ref[...].astype(BF16), up_ref[...]) + bias_ref[...]
    la = _log_sigmoid(z) * (LOG2_E / GLA_TAU)
    hi = la.astype(BF16)
    lo = (la - hi.astype(F32)).astype(BF16)
    kw = GLA_KEY_WIDTH
    bf_ref[...] = _dot(lf_ref[...], hi[:, :kw]) + _dot(lf_ref[...], lo[:, :kw])
    bb_ref[...] = _dot(lb_ref[...], hi[:, kw:]) + _dot(lb_ref[...], lo[:, kw:])


def _gla_gates(ga, up_pad, bias, lf, lb):
    T = ga.shape[0]
    tm = GATE_TM
    kw = GLA_KEY_WIDTH
    return pl.pallas_call(
        _gates_kernel,
        grid=(T // tm,),
        in_specs=[
            pl.BlockSpec((tm, LANE), lambda i: (i, 0)),
            pl.BlockSpec((LANE, 2 * kw), lambda i: (0, 0)),
            pl.BlockSpec((1, 2 * kw), lambda i: (0, 0)),
            pl.BlockSpec((tm, tm), lambda i: (0, 0)),
            pl.BlockSpec((tm, tm), lambda i: (0, 0)),
        ],
        out_specs=[pl.BlockSpec((tm, kw), lambda i: (i, 0)), pl.BlockSpec((tm, kw), lambda i: (i, 0))],
        out_shape=[jax.ShapeDtypeStruct((T, kw), F32), jax.ShapeDtypeStruct((T, kw), F32)],
        compiler_params=_cparams(("parallel",), 32),
        name="gla_gates",
    )(ga, up_pad, bias, lf, lb)


def _finish_heads(tot, gn, gate):
    ms = jnp.mean(tot * tot, axis=-1, keepdims=True)
    yn = tot * lax.rsqrt(ms + EPS) * gn
    return (yn * _silu(gate.astype(F32))).astype(BF16)


RET_TB = 1024


def _ret_kernel(dl_ref, q_ref, k_ref, v_ref, *rest, reverse):
    if reverse:
        g_ref, of_ref, gn_ref, o_ref, s_ref, intra_ref, qd_ref, kd_ref, cd_ref, p_ref, u_ref = rest
    else:
        o_ref, s_ref, intra_ref, qd_ref, kd_ref, cd_ref, p_ref, u_ref = rest
    h = pl.program_id(1)
    n = pl.program_id(2)
    C = RET_CHUNK

    @pl.when(n == 0)
    def _():
        s_ref[...] = jnp.zeros_like(s_ref)
        logit = dl_ref[1 if reverse else 0, h]
        lg = _log_sigmoid(jnp.full((C, RET_DV), logit, F32))
        lg_c = _log_sigmoid(jnp.full((C, C), logit, F32))
        lg_r = _log_sigmoid(jnp.full((1, RET_DV), logit, F32))
        ri = lax.broadcasted_iota(I32, (C, RET_DV), 0).astype(F32)
        rc = lax.broadcasted_iota(I32, (C, C), 0).astype(F32)
        cc = lax.broadcasted_iota(I32, (C, C), 1).astype(F32)
        diff = (cc - rc) if reverse else (rc - cc)
        intra_ref[...] = jnp.where(diff >= 0, jnp.exp(lg_c * diff), 0.0)
        if reverse:
            qd_ref[...] = jnp.exp(lg * (C - ri))
            kd_ref[...] = jnp.exp(lg * ri)
        else:
            qd_ref[...] = jnp.exp(lg * (ri + 1.0))
            kd_ref[...] = jnp.exp(lg * (C - 1.0 - ri))
        cd_ref[...] = jnp.exp(lg_r * C)

    nchunks = q_ref.shape[0] // C
    for c in range(nchunks):
        rows = slice(c * C, (c + 1) * C)
        k = k_ref[rows, :]
        p_ref[c] = (_dot_nt(q_ref[rows, :], k) * intra_ref[...]).astype(BF16)
        kd = (k.astype(F32) * kd_ref[...]).astype(BF16)
        u_ref[c] = _dot_tn(kd, v_ref[rows, :])
    order = range(nchunks - 1, -1, -1) if reverse else range(nchunks)
    for c in order:
        rows = slice(c * C, (c + 1) * C)
        state = s_ref[...]
        o = _dot(p_ref[c], v_ref[rows, :]) + _dot(q_ref[rows, :], state.astype(BF16)) * qd_ref[...]
        s_ref[...] = state * cd_ref[...] + u_ref[c]
        if reverse:
            o_ref[rows, :] = _finish_heads(of_ref[rows, :] + o, gn_ref[...], g_ref[rows, :])
        else:
            o_ref[rows, :] = o


def _ret_scan(proj, decay_logit, batch, seq_len, reverse, o_fwd=None, gn_w=None):
    T = proj.shape[0]
    tb = RET_TB
    nb = seq_len // tb
    dk, dv, C = RET_DK, RET_DV, RET_CHUNK

    def rb(b, n):
        return b * nb + ((nb - 1 - n) if reverse else n)

    def col(base):
        return lambda b, h, n: (rb(b, n), base // dk + h)

    in_specs = [
        pl.BlockSpec(memory_space=pltpu.SMEM),
        pl.BlockSpec((tb, dk), col(_RQ)),
        pl.BlockSpec((tb, dk), col(_RK)),
        pl.BlockSpec((tb, dv), col(_RV)),
    ]
    args = [decay_logit, proj, proj, proj]
    if reverse:
        in_specs += [
            pl.BlockSpec((tb, dv), col(_RG)),
            pl.BlockSpec((tb, dv), lambda b, h, n: (rb(b, n), h)),
            pl.BlockSpec((1, dv), lambda b, h, n: (0, h)),
        ]
        args += [proj, o_fwd, gn_w]
    out_dtype = BF16 if reverse else F32
    return pl.pallas_call(
        functools.partial(_ret_kernel, reverse=reverse),
        grid=(batch, RET_HEADS, nb),
        in_specs=in_specs,
        out_specs=pl.BlockSpec((tb, dv), lambda b, h, n: (rb(b, n), h)),
        out_shape=jax.ShapeDtypeStruct((T, RET_WIDTH), out_dtype),
        scratch_shapes=[
            pltpu.VMEM((dk, dv), F32),
            pltpu.VMEM((C, C), F32),
            pltpu.VMEM((C, dv), F32),
            pltpu.VMEM((C, dk), F32),
            pltpu.VMEM((1, dv), F32),
            pltpu.VMEM((tb // C, C, C), BF16),
            pltpu.VMEM((tb // C, dk, dv), F32),
        ],
        compiler_params=_cparams(("parallel", "parallel", "arbitrary"), 32),
        name="ret_bwd" if reverse else "ret_fwd",
    )(*args)


GLA_TB = 1024
GLA_UNROLL = 16


GLA_LEVELS = (32, 16, 8, 4, 2, 1)
SUBLANES = 8


def _gla_tables(reverse):
    C = GLA_CHUNK
    r = np.arange(C)
    masks = np.zeros((len(GLA_LEVELS) + 1, C, C), np.float32)
    for l, s in enumerate(GLA_LEVELS):
        upper = (r & s) != 0
        same = (r[:, None] // (2 * s)) == (r[None, :] // (2 * s))
        lhs_rows = ~upper if reverse else upper
        masks[l] = same & lhs_rows[:, None] & ~lhs_rows[None, :]
    masks[-1] = np.eye(C)
    return jnp.asarray(masks, F32)


def _gla_kernel(q_ref, k_ref, v_ref, b_ref, mask_ref, *rest, reverse):
    if reverse:
        g_ref, of_ref, gn_ref, o_ref, st_ref, sc_ref = rest
    else:
        o_ref, st_ref, sc_ref = rest
    n = pl.program_id(2)
    C = GLA_CHUNK

    @pl.when(n == 0)
    def _():
        st_ref[...] = jnp.zeros_like(st_ref)

    nchunks = q_ref.shape[0] // C
    sub_row = lax.broadcasted_iota(I32, (SUBLANES, GLA_DK), 0)
    zero_rows = jnp.zeros((SUBLANES, GLA_DK), F32)

    def chunk_scores(c, carry):
        c0 = pl.multiple_of(c * C, C)
        qb = q_ref[pl.ds(c0, C), :]
        kb = k_ref[pl.ds(c0, C), :]
        q = qb.astype(F32)
        k = kb.astype(F32)
        b = b_ref[pl.ds(c0, C), :]

        def mid_row(r):
            return jnp.broadcast_to(b[r:r + 1, :], (SUBLANES, GLA_DK))

        scores = mask_ref[len(GLA_LEVELS)] * _dot_nt(qb, kb)
        for l, s in enumerate(GLA_LEVELS):
            lhs, rhs = [], []
            for g in range(C // SUBLANES):
                r0 = g * SUBLANES
                rows = slice(r0, r0 + SUBLANES)
                if s >= SUBLANES:
                    m = mid_row((r0 // (2 * s)) * (2 * s) + s)
                    is_lhs = ((r0 & s) != 0) != reverse
                    if is_lhs:
                        lhs.append(q[rows] * jnp.exp2(b[rows] - m))
                        rhs.append(zero_rows)
                    else:
                        lhs.append(zero_rows)
                        rhs.append(k[rows] * jnp.exp2(m - b[rows]))
                else:
                    m = mid_row(r0 + SUBLANES - s)
                    for blk in range(SUBLANES // (2 * s) - 2, -1, -1):
                        m = jnp.where(sub_row < (blk + 1) * 2 * s, mid_row(r0 + blk * 2 * s + s), m)
                    upper = (sub_row & s) != 0
                    is_lhs = jnp.logical_not(upper) if reverse else upper
                    lhs.append(jnp.where(is_lhs, q[rows] * jnp.exp2(b[rows] - m), 0.0))
                    rhs.append(jnp.where(is_lhs, 0.0, k[rows] * jnp.exp2(m - b[rows])))
            lhs = jnp.concatenate(lhs, axis=0).astype(BF16)
            rhs = jnp.concatenate(rhs, axis=0).astype(BF16)
            scores = scores + mask_ref[l] * _dot_nt(lhs, rhs)
        sc_ref[c] = scores.astype(BF16)
        return carry

    lax.fori_loop(0, nchunks, chunk_scores, 0, unroll=GLA_UNROLL)

    def chunk(ci, carry):
        c = (nchunks - 1 - ci) if reverse else ci
        c0 = pl.multiple_of(c * C, C)
        q = q_ref[pl.ds(c0, C), :].astype(F32)
        k = k_ref[pl.ds(c0, C), :].astype(F32)
        v = v_ref[pl.ds(c0, C), :]
        b = b_ref[pl.ds(c0, C), :]
        b_end = b[0:1, :] if reverse else b[C - 1:C, :]

        st = st_ref[...]
        o = _dot_nt((q * jnp.exp2(b)).astype(BF16), st.astype(BF16))
        ke = (k * jnp.exp2(b_end - b)).astype(BF16)
        st_ref[...] = st * jnp.exp2(b_end) + _dot_tn(v, ke)
        o = o + _dot(sc_ref[c], v)
        if reverse:
            tot = of_ref[pl.ds(c0, C), :] + o
            o_ref[pl.ds(c0, C), :] = _finish_heads(tot, gn_ref[...], g_ref[pl.ds(c0, C), :])
        else:
            o_ref[pl.ds(c0, C), :] = o
        return carry

    lax.fori_loop(0, nchunks, chunk, 0, unroll=GLA_UNROLL)


def _gla_scan(proj, bcum, batch, seq_len, reverse, o_fwd=None, gn_w=None):
    T = proj.shape[0]
    tb = GLA_TB
    nb = seq_len // tb
    dk, dv = GLA_DK, GLA_DV
    masks = _gla_tables(reverse)

    def rb(b, n):
        return b * nb + ((nb - 1 - n) if reverse else n)

    in_specs = [
        pl.BlockSpec((tb, dk), lambda b, h, n: (rb(b, n), _GQ // dk + h)),
        pl.BlockSpec((tb, dk), lambda b, h, n: (rb(b, n), _GK // dk + h)),
        pl.BlockSpec((tb, dv), lambda b, h, n: (rb(b, n), _GV // dv + h)),
        pl.BlockSpec((tb, dk), lambda b, h, n: (rb(b, n), h)),
        pl.BlockSpec(masks.shape, lambda b, h, n: (0, 0, 0)),
    ]
    args = [proj, proj, proj, bcum, masks]
    if reverse:
        in_specs += [
            pl.BlockSpec((tb, dv), lambda b, h, n: (rb(b, n), _GG // dv + h)),
            pl.BlockSpec((tb, dv), lambda b, h, n: (rb(b, n), h)),
            pl.BlockSpec((1, dv), lambda b, h, n: (0, h)),
        ]
        args += [proj, o_fwd, gn_w]
    out_dtype = BF16 if reverse else F32
    return pl.pallas_call(
        functools.partial(_gla_kernel, reverse=reverse),
        grid=(batch, GLA_HEADS, nb),
        in_specs=in_specs,
        out_specs=pl.BlockSpec((tb, dv), lambda b, h, n: (rb(b, n), h)),
        out_shape=jax.ShapeDtypeStruct((T, GLA_WIDTH), out_dtype),
        scratch_shapes=[pltpu.VMEM((dv, dk), F32), pltpu.VMEM((tb // GLA_CHUNK, GLA_CHUNK, GLA_CHUNK), BF16)],
        compiler_params=_cparams(("parallel", "parallel", "arbitrary"), 32),
        name="gla_bwd" if reverse else "gla_fwd",
    )(*args)


OP_TM = 512
OP_SUB = 256


def _out_proj_kernel(mr_ref, mg_ref, w0_ref, w1_ref, x_ref, n2_ref, rh_ref, rl_ref, h_ref, xn_ref, aff_ref):
    for r in range(OP_TM // OP_SUB):
        rows = slice(r * OP_SUB, (r + 1) * OP_SUB)
        h = x_ref[rows, :] + _dot(mr_ref[rows, :], w0_ref[...]) + _dot(mg_ref[rows, :], w1_ref[...])
        h_ref[rows, :] = h
        ms = jnp.mean(h * h, axis=-1, keepdims=True)
        xn = h * lax.rsqrt(ms + EPS) * n2_ref[...]
        xh = xn.astype(BF16)
        xn_ref[rows, :] = _pack_bf16_pairs(xn)
        xl = (xn - xh.astype(F32)).astype(BF16)
        lt = _dot_nt(rh_ref[...], xh) + _dot_nt(rh_ref[...], xl) + _dot_nt(rl_ref[...], xh)
        m = jnp.max(lt, axis=0, keepdims=True)
        e = jnp.exp(lt - m)
        aff_ref[:, rows] = e / jnp.sum(e, axis=0, keepdims=True)


def _out_proj(mix_r, mix_g, w_out, x2d, n2w, r_hi, r_lo):
    T = x2d.shape[0]
    tm = OP_TM
    half = RET_WIDTH
    return pl.pallas_call(
        _out_proj_kernel,
        grid=(T // tm,),
        in_specs=[
            pl.BlockSpec((tm, half), lambda i: (i, 0)),
            pl.BlockSpec((tm, half), lambda i: (i, 0)),
            pl.BlockSpec((half, D_MODEL), lambda i: (0, 0)),
            pl.BlockSpec((half, D_MODEL), lambda i: (1, 0)),
            pl.BlockSpec((tm, D_MODEL), lambda i: (i, 0)),
            pl.BlockSpec((1, D_MODEL), lambda i: (0, 0)),
            pl.BlockSpec((N_EXPERTS, D_MODEL), lambda i: (0, 0)),
            pl.BlockSpec((N_EXPERTS, D_MODEL), lambda i: (0, 0)),
        ],
        out_specs=[
            pl.BlockSpec((tm, D_MODEL), lambda i: (i, 0)),
            pl.BlockSpec((tm, D_MODEL // 2), lambda i: (i, 0)),
            pl.BlockSpec((N_EXPERTS, tm), lambda i: (0, i)),
        ],
        out_shape=[
            jax.ShapeDtypeStruct((T, D_MODEL), F32),
            jax.ShapeDtypeStruct((T, D_MODEL // 2), I32),
            jax.ShapeDtypeStruct((N_EXPERTS, T), F32),
        ],
        compiler_params=_cparams(("parallel",), 56),
        name="out_proj",
    )(mix_r, mix_g, w_out, w_out, x2d, n2w, r_hi, r_lo)


def _select_kernel(a_ref, pos_ref, posb_ref, rankb_ref, affb_ref, *, cap):
    E, T = a_ref.shape
    tt = MOE_TT

    def count(pred):
        return jnp.sum(pred.astype(F32), axis=1, keepdims=True)

    def bisect(i, tau):
        cand = tau | jnp.left_shift(jnp.int32(1), 30 - i)
        bits = pltpu.bitcast(a_ref[...], I32)
        return jnp.where(count(bits >= cand) >= cap, cand, tau)

    tau = lax.fori_loop(0, 31, bisect, jnp.zeros((E, 1), I32))
    bits_all = pltpu.bitcast(a_ref[...], I32)
    quota = cap - count(bits_all > tau)

    before = (lax.broadcasted_iota(I32, (tt, tt), 0) < lax.broadcasted_iota(I32, (tt, tt), 1)).astype(BF16)
    below = (lax.broadcasted_iota(I32, (E, E), 1) < lax.broadcasted_iota(I32, (E, E), 0)).astype(BF16)

    def block(j, carry):
        c_eq, c_sel = carry
        off = pl.multiple_of(j * tt, tt)
        aff = a_ref[:, pl.ds(off, tt)]
        bits = pltpu.bitcast(aff, I32)
        eq = bits == tau
        eqf = eq.astype(F32)
        rank_eq = _dot(eqf.astype(BF16), before) + c_eq
        sel = (bits > tau) | (eq & (rank_eq < quota))
        self_ = sel.astype(F32)
        selb = self_.astype(BF16)
        slot = _dot(selb, before) + c_sel
        pos = jnp.where(sel, slot, -1.0).astype(I32)
        pos_ref[:, pl.ds(off, tt)] = pos
        per_tok = jnp.broadcast_to(jnp.sum(self_, axis=0, keepdims=True), (E, tt))
        rank = _dot(per_tok.astype(BF16), before) + _dot(below, selb)
        posb_ref[j] = pos
        rankb_ref[j] = jnp.where(sel, rank, -1.0).astype(I32)
        affb_ref[j] = aff
        return (c_eq + jnp.sum(eqf, axis=1, keepdims=True), c_sel + jnp.sum(self_, axis=1, keepdims=True))

    zero = jnp.zeros((E, 1), F32)
    lax.fori_loop(0, T // tt, block, (zero, zero))


def _select(aff, cap):
    E, T = aff.shape
    nb = T // MOE_TT
    blk = jax.ShapeDtypeStruct((nb, E, MOE_TT), I32)
    return pl.pallas_call(
        functools.partial(_select_kernel, cap=cap),
        out_shape=[jax.ShapeDtypeStruct((E, T), I32), blk, blk, jax.ShapeDtypeStruct((nb, E, MOE_TT), F32)],
        compiler_params=pltpu.CompilerParams(vmem_limit_bytes=40 * 1024 * 1024),
        name="select",
    )(aff)


def _regroup_rows(T):
    nb = T // MOE_TT
    rows = CAPACITY_FACTOR * T + SC_GATHER_ROWS * nb
    return -(-rows // MOE_TW) * MOE_TW


def _combine_schedule(posb, T):
    nb = posb.shape[0]
    tw, g = MOE_TW, SC_GATHER_ROWS
    n = jnp.sum((posb >= 0).reshape(nb, -1), axis=1).astype(I32)
    seg = (n + g - 1) // g * g
    hi = jnp.cumsum(seg)
    lo = hi - seg
    off = jnp.concatenate([jnp.zeros((1,), I32), hi])
    nwin_max = (N_EXPERTS * MOE_TT) // tw + 1
    w0 = lo // tw
    w1 = jnp.where(seg > 0, (hi - 1) // tw, w0)
    cand = jnp.arange(nwin_max, dtype=I32)
    win = w0[:, None] + cand[None, :]
    valid = (win <= w1[:, None]).reshape(-1)
    nwin_total = _regroup_rows(T) // tw
    pmax = nb + nwin_total
    jv = jnp.broadcast_to(jnp.arange(nb, dtype=I32)[:, None], win.shape).reshape(-1)
    wv = jnp.minimum(win, nwin_total - 1).reshape(-1)
    total = jnp.sum(valid.astype(I32))
    dst = jnp.where(valid, jnp.cumsum(valid.astype(I32)) - 1, pmax)
    pj, pw = (jnp.zeros((pmax,), I32).at[dst].set(a, mode="drop") for a in (jv, wv))
    real = jnp.arange(pmax, dtype=I32) < total
    pj, pw = (jnp.where(real, a, a[total - 1]) for a in (pj, pw))
    first = jnp.concatenate([jnp.ones((1,), bool), pj[1:] != pj[:-1]])
    last = jnp.concatenate([pj[1:] != pj[:-1], jnp.ones((1,), bool)]) | (jnp.arange(pmax, dtype=I32) == total - 1)
    flag = jnp.where(real, first.astype(I32) + 2 * last.astype(I32) + 4, 0)
    return off, (pj, pw, flag, lo[pj], hi[pj])


SC_LANES = 16
SC_CORES = 2
SC_SUBCORES = 16
SC_GATHER_ROWS = 32


def _dispatch(pos, xw, cap):
    E, T = pos.shape
    W = xw.shape[1]
    G = SC_GATHER_ROWS
    part_rows = cap // SC_CORES
    mesh = plsc.VectorSubcoreMesh(core_axis_name="c", subcore_axis_name="s")

    @pl.kernel(
        out_type=jax.ShapeDtypeStruct((E * cap, W), I32),
        mesh=mesh,
        scratch_types=[pltpu.VMEM((T,), I32), pltpu.VMEM((cap,), I32), pltpu.VMEM((G, W), I32)],
        compiler_params=pltpu.CompilerParams(needs_layout_passes=False),
        name="sc_dispatch",
    )
    def run(pos_hbm, x_hbm, xe_hbm, pos_v, idx_v, buf):
        e = lax.axis_index("s")
        part = lax.axis_index("c")
        pltpu.sync_copy(pos_hbm.at[e], pos_v)
        lane = lax.iota(I32, SC_LANES)

        @pl.loop(0, T // SC_LANES)
        def _(i):
            off = pl.multiple_of(i * SC_LANES, SC_LANES)
            p = pos_v[pl.ds(off, SC_LANES)]
            plsc.store_scatter(idx_v, [p], lane + off, mask=p >= 0)

        @pl.loop(0, part_rows // G)
        def _(g):
            o = pl.multiple_of(part * part_rows + g * G, G)
            pltpu.sync_copy(x_hbm.at[idx_v.at[pl.ds(o, G)]], buf)
            pltpu.sync_copy(buf, xe_hbm.at[pl.ds(e * cap + o, G)])

    return run(pos, xw)


FFN_TM = 2048
FFN_UNPACK_ROWS = 256


def _ffn_tile_width(cap):
    return 512 if min(FFN_TM, cap) <= 1024 else 256


def _ffn_kernel(x_ref, wg_ref, wu_ref, wd_ref, o_ref, xb_ref, hid_ref, *, tw):
    s = pl.program_id(2)
    n_up = D_FF // tw

    @pl.when(s == 0)
    def _():
        half = D_MODEL // 2

        def unpack_rows(i, carry):
            r = pl.multiple_of(i * FFN_UNPACK_ROWS, FFN_UNPACK_ROWS)
            lo, hi = _unpack_bf16_pairs(x_ref[pl.ds(r, FFN_UNPACK_ROWS), :])
            xb_ref[pl.ds(r, FFN_UNPACK_ROWS), :half] = lo.astype(BF16)
            xb_ref[pl.ds(r, FFN_UNPACK_ROWS), half:] = hi.astype(BF16)
            return carry

        lax.fori_loop(0, x_ref.shape[0] // FFN_UNPACK_ROWS, unpack_rows, 0)

    @pl.when(s < n_up)
    def _():
        x = xb_ref[...]
        g = _dot(x, wg_ref[...].astype(BF16))
        u = _dot(x, wu_ref[...].astype(BF16))
        col = pl.multiple_of(s * tw, tw)
        hid_ref[:, pl.ds(col, tw)] = (_silu(g) * u).astype(BF16)

    @pl.when(s >= n_up)
    def _():
        o_ref[...] = _pack_bf16_pairs(_dot(hid_ref[...], wd_ref[...].astype(BF16)))


def _ffn(xe, w_gate, w_up, w_down):
    E, cap, _ = xe.shape
    tm = min(FFN_TM, cap)
    tw = _ffn_tile_width(cap)
    n_up, n_down = D_FF // tw, D_MODEL // tw

    def up(e, m, s):
        return (e, 0, jnp.minimum(s, n_up - 1))

    def down(s):
        return jnp.maximum(s - n_up, 0)

    return pl.pallas_call(
        functools.partial(_ffn_kernel, tw=tw),
        grid=(E, cap // tm, n_up + n_down),
        in_specs=[
            pl.BlockSpec((None, tm, D_MODEL // 2), lambda e, m, s: (e, m, 0)),
            pl.BlockSpec((None, D_MODEL, tw), up),
            pl.BlockSpec((None, D_MODEL, tw), up),
            pl.BlockSpec((None, D_FF, tw), lambda e, m, s: (e, 0, down(s))),
        ],
        out_specs=pl.BlockSpec((None, tm, tw // 2), lambda e, m, s: (e, m, down(s))),
        out_shape=jax.ShapeDtypeStruct((E, cap, D_MODEL // 2), I32),
        scratch_shapes=[pltpu.VMEM((tm, D_MODEL), BF16), pltpu.VMEM((tm, D_FF), BF16)],
        compiler_params=_cparams(("parallel", "parallel", "arbitrary"), 60),
        name="ffn",
    )(xe, w_gate, w_up, w_down)


def _regroup(posb, rankb, affb, off, yw, cap, rows):
    NB, EB = posb.shape
    W = yw.shape[1]
    G, L, tt = SC_GATHER_ROWS, SC_LANES, MOE_TT
    n_workers = SC_CORES * SC_SUBCORES
    per = NB // n_workers
    mesh = plsc.VectorSubcoreMesh(core_axis_name="c", subcore_axis_name="s")
    off_pad = jnp.pad(off, (0, L))

    @pl.kernel(
        out_type=(jax.ShapeDtypeStruct((rows, W), I32), jax.ShapeDtypeStruct((rows,), I32),
                  jax.ShapeDtypeStruct((rows,), F32)),
        mesh=mesh,
        scratch_types=[pltpu.VMEM((EB,), I32), pltpu.VMEM((EB,), I32), pltpu.VMEM((EB,), F32),
                       pltpu.VMEM((EB,), I32), pltpu.VMEM((EB,), I32), pltpu.VMEM((EB,), F32),
                       pltpu.VMEM((G, W), I32), pltpu.VMEM((NB + 1 + L,), I32)],
        compiler_params=pltpu.CompilerParams(needs_layout_passes=False),
        name="sc_regroup",
    )
    def run(posb_hbm, rankb_hbm, affb_hbm, off_hbm, y_hbm, yg_hbm, tok_hbm, gate_hbm,
            pos_v, rank_v, aff_v, src_v, tok_v, gate_v, buf, off_v):
        wid = lax.axis_index("c") * SC_SUBCORES + lax.axis_index("s")
        pltpu.sync_copy(off_hbm, off_v)
        lane = lax.iota(I32, L)
        zi = jnp.zeros((L,), I32)
        zf = jnp.zeros((L,), F32)

        @pl.loop(0, per)
        def _(k):
            j = wid * per + k
            pltpu.sync_copy(posb_hbm.at[j], pos_v)
            pltpu.sync_copy(rankb_hbm.at[j], rank_v)
            pltpu.sync_copy(affb_hbm.at[j], aff_v)
            lo = jnp.max(plsc.load_gather(off_v, [zi + j]))
            hi = jnp.max(plsc.load_gather(off_v, [zi + j + 1]))

            @pl.loop(0, EB // L)
            def _(i):
                o = pl.multiple_of(i * L, L)
                src_v[pl.ds(o, L)] = zi
                tok_v[pl.ds(o, L)] = zi
                gate_v[pl.ds(o, L)] = zf

            @pl.loop(0, EB // L)
            def _(i):
                o = pl.multiple_of(i * L, L)
                p = pos_v[pl.ds(o, L)]
                r = rank_v[pl.ds(o, L)]
                m = p >= 0
                e = i // (tt // L)
                t0 = j * tt + (i % (tt // L)) * L
                plsc.store_scatter(src_v, [r], p + e * cap, mask=m)
                plsc.store_scatter(tok_v, [r], lane + t0, mask=m)
                plsc.store_scatter(gate_v, [r], aff_v[pl.ds(o, L)], mask=m)

            @pl.loop(0, (hi - lo) // G)
            def _(g):
                o = pl.multiple_of(g * G, G)
                dst = pl.multiple_of(lo + o, G)
                pltpu.sync_copy(y_hbm.at[src_v.at[pl.ds(o, G)]], buf)
                pltpu.sync_copy(buf, yg_hbm.at[pl.ds(dst, G)])
                pltpu.sync_copy(tok_v.at[pl.ds(o, G)], tok_hbm.at[pl.ds(dst, G)])
                pltpu.sync_copy(gate_v.at[pl.ds(o, G)], gate_hbm.at[pl.ds(dst, G)])

    return run(posb, rankb, affb, off_pad, yw)


def _combine_kernel(pj_ref, pw_ref, pf_ref, plo_ref, phi_ref, tok_ref, gate_ref, yg_ref, h_ref, nf_ref,
                    o_ref, acc_ref, *, group):
    p = pl.program_id(0)
    flag = pf_ref[p]
    tw, tt = MOE_TW, MOE_TT
    half = D_MODEL // 2

    hw = group // 2

    def col_blocks():
        for n in range(D_MODEL // group):
            yield slice(n * hw, (n + 1) * hw), slice(n * group, n * group + hw)
            yield slice(half + n * hw, half + (n + 1) * hw), slice(n * group + hw, (n + 1) * group)

    @pl.when((flag & 1) != 0)
    def _():
        for packed, natural in col_blocks():
            acc_ref[:, packed] = h_ref[:, natural]

    @pl.when((flag & 4) != 0)
    def _():
        lo, hi = plo_ref[p], phi_ref[p]
        row0 = pw_ref[p] * tw
        rid = lax.broadcasted_iota(I32, (tw, 1), 0) + row0
        keep = (rid >= lo) & (rid < hi)
        y_lo, y_hi = _unpack_bf16_pairs(yg_ref[...])
        y_lo = jnp.where(keep, y_lo, 0.0).astype(BF16)
        y_hi = jnp.where(keep, y_hi, 0.0).astype(BF16)
        tid = lax.broadcasted_iota(I32, (tt, tw), 0) + pj_ref[p] * tt
        cid = lax.broadcasted_iota(I32, (tt, tw), 1) + row0
        hit = (tok_ref[...] == tid) & (cid >= lo) & (cid < hi)
        weights = jnp.where(hit, gate_ref[...], 0.0).astype(BF16)
        acc_ref[:, :half] += _dot(weights, y_lo)
        acc_ref[:, half:] += _dot(weights, y_hi)

    @pl.when((flag & 2) != 0)
    def _():
        y = acc_ref[...]
        scale = lax.rsqrt(jnp.mean(y * y, axis=-1, keepdims=True) + EPS)
        for packed, natural in col_blocks():
            o_ref[:, natural] = acc_ref[:, packed] * scale * nf_ref[:, natural]


def _combine(lists, tok, gate, yg, h, nfw, group):
    pj, pw, pf, plo, phi = lists
    T = h.shape[0]
    tw, tt = MOE_TW, MOE_TT
    nwin = yg.shape[0] // tw
    grid_spec = pltpu.PrefetchScalarGridSpec(
        num_scalar_prefetch=5,
        grid=(pj.shape[0],),
        in_specs=[
            pl.BlockSpec((None, 1, tw), lambda p, pj, pw, *_: (pw[p], 0, 0)),
            pl.BlockSpec((None, 1, tw), lambda p, pj, pw, *_: (pw[p], 0, 0)),
            pl.BlockSpec((tw, D_MODEL // 2), lambda p, pj, pw, *_: (pw[p], 0)),
            pl.BlockSpec((tt, D_MODEL), lambda p, pj, pw, *_: (pj[p], 0)),
            pl.BlockSpec((1, D_MODEL), lambda p, pj, pw, *_: (0, 0)),
        ],
        out_specs=pl.BlockSpec((tt, D_MODEL), lambda p, pj, pw, *_: (pj[p], 0)),
        scratch_shapes=[pltpu.VMEM((tt, D_MODEL), F32)],
    )
    return pl.pallas_call(
        functools.partial(_combine_kernel, group=group),
        grid_spec=grid_spec,
        out_shape=jax.ShapeDtypeStruct((T, D_MODEL), F32),
        compiler_params=_cparams(("arbitrary",), 32),
        name="combine",
    )(pj, pw, pf, plo, phi, tok.reshape(nwin, 1, tw), gate.reshape(nwin, 1, tw), yg, h, nfw)


def _rope_tables(seq_len):
    d = RET_DK
    inv = ROPE_BASE ** (-jnp.arange(0, d, 2, dtype=F32) / d)
    ang = jnp.arange(seq_len, dtype=F32)[:, None] * inv[None, :]
    return jnp.cos(ang), jnp.sin(ang)


def _chunk_tri(n, chunk, upper):
    r = np.arange(n)
    same = (r[:, None] // chunk) == (r[None, :] // chunk)
    tri = (r[:, None] <= r[None, :]) if upper else (r[:, None] >= r[None, :])
    return jnp.asarray(same & tri, BF16)


def _prep_params(norm1_w, w_in, ret_gn_w, gla_gate_up, gla_gate_bias, gla_gn_w, w_out, norm2_w, router_w,
                 normf_w):
    w = w_in[0]
    w_main = w[:, :IN_MAIN].astype(BF16)
    w_ga = jnp.pad(w[:, IN_MAIN:], ((0, 0), (0, LANE - 2 * GLA_RANK))).astype(BF16)
    cs = np.ones((1, IN_MAIN), np.float32)
    cs[:, _RQ:_RQ + RET_WIDTH] = RET_DK ** -0.5
    cs[:, _GQ:_GQ + GLA_KEY_WIDTH] = GLA_DK ** -0.5
    up = gla_gate_up[0].astype(F32)
    up_pad = jnp.zeros((LANE, 2 * GLA_KEY_WIDTH), F32)
    up_pad = up_pad.at[:GLA_RANK, :GLA_KEY_WIDTH].set(up[0])
    up_pad = up_pad.at[GLA_RANK:2 * GLA_RANK, GLA_KEY_WIDTH:].set(up[1])
    rt = router_w[0].T.astype(F32)
    r_hi = rt.astype(BF16)
    r_lo = (rt - r_hi.astype(F32)).astype(BF16)
    return dict(
        n1w=norm1_w[0].reshape(1, D_MODEL).astype(F32),
        w_main=w_main, w_ga=w_ga, colscale=jnp.asarray(cs),
        up_pad=up_pad.astype(BF16),
        bias=gla_gate_bias[0].reshape(1, 2 * GLA_KEY_WIDTH).astype(F32),
        lf=_chunk_tri(GATE_TM, GLA_CHUNK, upper=False),
        lb=_chunk_tri(GATE_TM, GLA_CHUNK, upper=True),
        ret_gn=ret_gn_w[0].reshape(1, RET_WIDTH).astype(F32),
        gla_gn=gla_gn_w[0].reshape(1, GLA_WIDTH).astype(F32),
        w_out=w_out[0].astype(BF16),
        n2w=norm2_w[0].reshape(1, D_MODEL).astype(F32),
        r_hi=r_hi, r_lo=r_lo,
        nfw=normf_w.reshape(1, D_MODEL).astype(F32),
    )


def _trunk(x, pp, decay_logit, w_gate, w_up, w_down):
    B, L, _ = x.shape
    T = B * L
    x2d = x.reshape(T, D_MODEL)
    cos, sin = _rope_tables(L)
    proj, ga = _in_proj(x2d, pp["n1w"], pp["w_main"], pp["w_ga"], pp["colscale"], cos, sin, L)
    b_f, b_b = _gla_gates(ga, pp["up_pad"], pp["bias"], pp["lf"], pp["lb"])

    ret_f = _ret_scan(proj, decay_logit, B, L, reverse=False)
    mix_r = _ret_scan(proj, decay_logit, B, L, reverse=True, o_fwd=ret_f, gn_w=pp["ret_gn"])
    gla_f = _gla_scan(proj, b_f, B, L, reverse=False)
    mix_g = _gla_scan(proj, b_b, B, L, reverse=True, o_fwd=gla_f, gn_w=pp["gla_gn"])

    h, xn2, aff = _out_proj(mix_r, mix_g, pp["w_out"], x2d, pp["n2w"], pp["r_hi"], pp["r_lo"])

    cap = CAPACITY_FACTOR * T // N_EXPERTS
    pos, posb, rankb, affb = _select(aff, cap)
    off, c_lists = _combine_schedule(posb, T)
    xe = _dispatch(pos, xn2, cap).reshape(N_EXPERTS, cap, D_MODEL // 2)
    ye = _ffn(xe, w_gate, w_up, w_down).reshape(N_EXPERTS * cap, D_MODEL // 2)
    nb = T // MOE_TT
    yg, tok, gate = _regroup(posb.reshape(nb, -1), rankb.reshape(nb, -1), affb.reshape(nb, -1), off, ye, cap,
                             _regroup_rows(T))
    y = _combine(c_lists, tok, gate, yg, h, pp["nfw"], _ffn_tile_width(cap))
    return y.reshape(B, L, D_MODEL)


def kernel(x_prompt, x_sample, norm1_w, w_in, ret_decay_logit, ret_gn_w, gla_gate_up, gla_gate_bias,
           gla_gn_w, w_out, norm2_w, router_w, w_gate, w_up, w_down, normf_w):
    pp = _prep_params(norm1_w, w_in, ret_gn_w, gla_gate_up, gla_gate_bias, gla_gn_w, w_out, norm2_w,
                      router_w, normf_w)
    decay_logit = ret_decay_logit[0].astype(F32)
    args = (pp, decay_logit, w_gate[0], w_up[0], w_down[0])
    return (_trunk(x_prompt, *args), _trunk(x_sample, *args))
```

```python
import functools

import numpy as np
import jax
import jax.numpy as jnp
from jax import lax
from jax.experimental import pallas as pl
from jax.experimental.pallas import tpu as pltpu
from jax.experimental.pallas import tpu_sc as plsc

F32, BF16, I32 = jnp.float32, jnp.bfloat16, jnp.int32

D_MODEL = 2048
RET_WIDTH = 1024
RET_HEADS = 4
RET_DK = 256
RET_DV = 256
GLA_WIDTH = 1024
GLA_HEADS = 4
GLA_DK = 128
GLA_DV = 256
GLA_KEY_WIDTH = 512
GLA_RANK = 16
GLA_TAU = 16.0
RET_CHUNK = 256
GLA_CHUNK = 64
GLA_SUB = 16
ROPE_BASE = 10000.0
N_EXPERTS = 16
CAPACITY_FACTOR = 2
D_FF = 2048
EPS = 1e-6
LOG2_E = 1.4426950408889634
IN_MAIN = 4 * RET_WIDTH + 2 * GLA_KEY_WIDTH + 2 * GLA_WIDTH

_RQ, _RK, _RV, _RG = 0, 1024, 2048, 3072
_GQ, _GK, _GV, _GG = 4096, 4608, 5120, 6144

LANE = 128
MOE_TT = 256
MOE_TW = 256
V7X_VMEM_BYTES = 64 * 1024 * 1024


def _cparams(sem, vmem_mb):
    return pltpu.CompilerParams(dimension_semantics=sem, vmem_limit_bytes=vmem_mb * 1024 * 1024)


def _log_sigmoid(z):
    return jnp.minimum(z, 0.0) - jnp.log1p(jnp.exp(-jnp.abs(z)))


def _silu(g):
    return g * (1.0 / (1.0 + jnp.exp(-g)))


def _dot_nt(a, b):
    return lax.dot_general(a, b, (((1,), (1,)), ((), ())), preferred_element_type=F32)


def _dot_tn(a, b):
    return lax.dot_general(a, b, (((0,), (0,)), ((), ())), preferred_element_type=F32)


def _dot(a, b):
    return jnp.dot(a, b, preferred_element_type=F32)


def _pack_bf16_pairs(x):
    bits = pltpu.bitcast(x.astype(BF16).astype(F32), I32)
    w = x.shape[1] // 2
    return bits[:, w:] | lax.shift_right_logical(bits[:, :w], 16)


def _unpack_bf16_pairs(words):
    lo = pltpu.bitcast(lax.shift_left(words, 16), F32)
    hi = pltpu.bitcast(words & jnp.int32(-65536), F32)
    return lo, hi


IP_TM = 1024
IP_TN = 1024


def _in_proj_kernel(x_ref, n1_ref, w_ref, wga_ref, cs_ref, cos_ref, sin_ref, o_ref, ga_ref, xn_ref):
    j = pl.program_id(1)

    @pl.when(j == 0)
    def _():
        x = x_ref[...]
        ms = jnp.mean(x * x, axis=-1, keepdims=True)
        xn = (x * lax.rsqrt(ms + EPS) * n1_ref[...]).astype(BF16)
        xn_ref[...] = xn
        ga_ref[...] = _dot(xn, wga_ref[...])

    acc = _dot(xn_ref[...], w_ref[...]) * cs_ref[...]
    n_rope_blocks = 2 * RET_WIDTH // IP_TN

    @pl.when(j < n_rope_blocks)
    def _():
        cos = cos_ref[...]
        sin = sin_ref[...]
        for h in range(IP_TN // RET_DK):
            x1 = acc[:, 2 * h * LANE:(2 * h + 1) * LANE]
            x2 = acc[:, (2 * h + 1) * LANE:(2 * h + 2) * LANE]
            o_ref[2 * h] = (x1 * cos - x2 * sin).astype(BF16)
            o_ref[2 * h + 1] = (x1 * sin + x2 * cos).astype(BF16)

    @pl.when(j >= n_rope_blocks)
    def _():
        for c in range(IP_TN // LANE):
            o_ref[c] = acc[:, c * LANE:(c + 1) * LANE].astype(BF16)


def _in_proj(x2d, n1w, w_main, w_ga, colscale, cos, sin, seq_len):
    T = x2d.shape[0]
    tm, tn = IP_TM, IP_TN
    nlb = seq_len // tm
    return pl.pallas_call(
        _in_proj_kernel,
        grid=(T // tm, IN_MAIN // tn),
        in_specs=[
            pl.BlockSpec((tm, D_MODEL), lambda i, j: (i, 0)),
            pl.BlockSpec((1, D_MODEL), lambda i, j: (0, 0)),
            pl.BlockSpec((D_MODEL, tn), lambda i, j: (0, j)),
            pl.BlockSpec((D_MODEL, LANE), lambda i, j: (0, 0)),
            pl.BlockSpec((1, tn), lambda i, j: (0, j)),
            pl.BlockSpec((tm, LANE), lambda i, j: (i % nlb, 0)),
            pl.BlockSpec((tm, LANE), lambda i, j: (i % nlb, 0)),
        ],
        out_specs=[
            pl.BlockSpec((tn // LANE, tm, LANE), lambda i, j: (j, i, 0)),
            pl.BlockSpec((tm, LANE), lambda i, j: (i, 0)),
        ],
        out_shape=[
            jax.ShapeDtypeStruct((IN_MAIN // LANE, T, LANE), BF16),
            jax.ShapeDtypeStruct((T, LANE), F32),
        ],
        scratch_shapes=[pltpu.VMEM((tm, D_MODEL), BF16)],
        compiler_params=_cparams(("parallel", "arbitrary"), 48),
        name="in_proj",
    )(x2d, n1w, w_main, w_ga, colscale, cos, sin)


GATE_TM = 512


def _gates_kernel(ga_ref, up_ref, bias_ref, lf_ref, lb_ref, bf_ref, bb_ref):
    z = _dot(ga_ref[...].astype(BF16), up_ref[...]) + bias_ref[...]
    la = _log_sigmoid(z) * (LOG2_E / GLA_TAU)
    hi = la.astype(BF16)
    lo = (la - hi.astype(F32)).astype(BF16)
    kw = GLA_KEY_WIDTH
    bf_ref[...] = _dot(lf_ref[...], hi[:, :kw]) + _dot(lf_ref[...], lo[:, :kw])
    bb_ref[...] = _dot(lb_ref[...], hi[:, kw:]) + _dot(lb_ref[...], lo[:, kw:])


def _gla_gates(ga, up_pad, bias, lf, lb):
    T = ga.shape[0]
    tm = GATE_TM
    kw = GLA_KEY_WIDTH
    return pl.pallas_call(
        _gates_kernel,
        grid=(T // tm,),
        in_specs=[
            pl.BlockSpec((tm, LANE), lambda i: (i, 0)),
            pl.BlockSpec((LANE, 2 * kw), lambda i: (0, 0)),
            pl.BlockSpec((1, 2 * kw), lambda i: (0, 0)),
            pl.BlockSpec((tm, tm), lambda i: (0, 0)),
            pl.BlockSpec((tm, tm), lambda i: (0, 0)),
        ],
        out_specs=[pl.BlockSpec((tm, kw), lambda i: (i, 0)), pl.BlockSpec((tm, kw), lambda i: (i, 0))],
        out_shape=[jax.ShapeDtypeStruct((T, kw), F32), jax.ShapeDtypeStruct((T, kw), F32)],
        compiler_params=_cparams(("parallel",), 32),
        name="gla_gates",
    )(ga, up_pad, bias, lf, lb)


def _wide(ref, rows):
    return jnp.concatenate([ref[0, rows, :], ref[1, rows, :]], axis=1)


def _finish_heads(tot, gn, gate):
    ms = jnp.mean(tot * tot, axis=-1, keepdims=True)
    yn = tot * lax.rsqrt(ms + EPS) * gn
    return (yn * _silu(gate.astype(F32))).astype(BF16)


RET_TB = 1024


def _ret_kernel(dl_ref, q_ref, k_ref, v_ref, *rest, reverse):
    if reverse:
        g_ref, of_ref, gn_ref, o_ref, s_ref, intra_ref, qd_ref, kd_ref, cd_ref, p_ref, u_ref = rest
    else:
        o_ref, s_ref, intra_ref, qd_ref, kd_ref, cd_ref, p_ref, u_ref = rest
    h = pl.program_id(1)
    n = pl.program_id(2)
    C = RET_CHUNK

    @pl.when(n == 0)
    def _():
        s_ref[...] = jnp.zeros_like(s_ref)
        logit = dl_ref[1 if reverse else 0, h]
        lg = _log_sigmoid(jnp.full((C, RET_DV), logit, F32))
        lg_c = _log_sigmoid(jnp.full((C, C), logit, F32))
        lg_r = _log_sigmoid(jnp.full((1, RET_DV), logit, F32))
        ri = lax.broadcasted_iota(I32, (C, RET_DV), 0).astype(F32)
        rc = lax.broadcasted_iota(I32, (C, C), 0).astype(F32)
        cc = lax.broadcasted_iota(I32, (C, C), 1).astype(F32)
        diff = (cc - rc) if reverse else (rc - cc)
        intra_ref[...] = jnp.where(diff >= 0, jnp.exp(lg_c * diff), 0.0)
        if reverse:
            qd_ref[...] = jnp.exp(lg * (C - ri))
            kd_ref[...] = jnp.exp(lg * ri)
        else:
            qd_ref[...] = jnp.exp(lg * (ri + 1.0))
            kd_ref[...] = jnp.exp(lg * (C - 1.0 - ri))
        cd_ref[...] = jnp.exp(lg_r * C)

    nchunks = o_ref.shape[0] // C
    for c in range(nchunks):
        rows = slice(c * C, (c + 1) * C)
        k = _wide(k_ref, rows)
        p_ref[c] = (_dot_nt(_wide(q_ref, rows), k) * intra_ref[...]).astype(BF16)
        kd = (k.astype(F32) * kd_ref[...]).astype(BF16)
        u_ref[c] = _dot_tn(kd, _wide(v_ref, rows))
    order = range(nchunks - 1, -1, -1) if reverse else range(nchunks)
    for c in order:
        rows = slice(c * C, (c + 1) * C)
        state = s_ref[...]
        o = _dot(p_ref[c], _wide(v_ref, rows)) + _dot(_wide(q_ref, rows), state.astype(BF16)) * qd_ref[...]
        s_ref[...] = state * cd_ref[...] + u_ref[c]
        if reverse:
            tot = of_ref[rows, :].astype(F32) + o
            o_ref[rows, :] = _finish_heads(tot, gn_ref[...], _wide(g_ref, rows))
        else:
            o_ref[rows, :] = o.astype(BF16)


def _ret_scan(proj, decay_logit, batch, seq_len, reverse, o_fwd=None, gn_w=None):
    T = proj.shape[1]
    tb = RET_TB
    nb = seq_len // tb
    dk, dv, C = RET_DK, RET_DV, RET_CHUNK

    def rb(b, n):
        return b * nb + ((nb - 1 - n) if reverse else n)

    def head(base):
        return pl.BlockSpec((dk // LANE, tb, LANE), lambda b, h, n: (base // dk + h, rb(b, n), 0))

    in_specs = [pl.BlockSpec(memory_space=pltpu.SMEM), head(_RQ), head(_RK), head(_RV)]
    args = [decay_logit, proj, proj, proj]
    if reverse:
        in_specs += [
            head(_RG),
            pl.BlockSpec((tb, dv), lambda b, h, n: (rb(b, n), h)),
            pl.BlockSpec((1, dv), lambda b, h, n: (0, h)),
        ]
        args += [proj, o_fwd, gn_w]
    out_dtype = BF16
    return pl.pallas_call(
        functools.partial(_ret_kernel, reverse=reverse),
        grid=(batch, RET_HEADS, nb),
        in_specs=in_specs,
        out_specs=pl.BlockSpec((tb, dv), lambda b, h, n: (rb(b, n), h)),
        out_shape=jax.ShapeDtypeStruct((T, RET_WIDTH), out_dtype),
        scratch_shapes=[
            pltpu.VMEM((dk, dv), F32),
            pltpu.VMEM((C, C), F32),
            pltpu.VMEM((C, dv), F32),
            pltpu.VMEM((C, dk), F32),
            pltpu.VMEM((1, dv), F32),
            pltpu.VMEM((tb // C, C, C), BF16),
            pltpu.VMEM((tb // C, dk, dv), F32),
        ],
        compiler_params=_cparams(("parallel", "parallel", "arbitrary"), 32),
        name="ret_bwd" if reverse else "ret_fwd",
    )(*args)


GLA_TB = 1024
GLA_UNROLL = 16


GLA_LEVELS = (32, 16, 8, 4, 2, 1)
SUBLANES = 8


def _gla_tables(reverse):
    C = GLA_CHUNK
    r = np.arange(C)
    masks = np.zeros((len(GLA_LEVELS) + 1, C, C), np.float32)
    for l, s in enumerate(GLA_LEVELS):
        upper = (r & s) != 0
        same = (r[:, None] // (2 * s)) == (r[None, :] // (2 * s))
        lhs_rows = ~upper if reverse else upper
        masks[l] = same & lhs_rows[:, None] & ~lhs_rows[None, :]
    masks[-1] = np.eye(C)
    return jnp.asarray(masks, F32)


def _gla_kernel(q_ref, k_ref, v_ref, b_ref, mask_ref, *rest, reverse):
    if reverse:
        g_ref, of_ref, gn_ref, o_ref, st_ref, sc_ref = rest
    else:
        o_ref, st_ref, sc_ref = rest
    n = pl.program_id(2)
    C = GLA_CHUNK

    @pl.when(n == 0)
    def _():
        st_ref[...] = jnp.zeros_like(st_ref)

    nchunks = q_ref.shape[0] // C
    sub_row = lax.broadcasted_iota(I32, (SUBLANES, GLA_DK), 0)
    zero_rows = jnp.zeros((SUBLANES, GLA_DK), F32)

    def chunk_scores(c, carry):
        c0 = pl.multiple_of(c * C, C)
        qb = q_ref[pl.ds(c0, C), :]
        kb = k_ref[pl.ds(c0, C), :]
        q = qb.astype(F32)
        k = kb.astype(F32)
        b = b_ref[pl.ds(c0, C), :]

        def mid_row(r):
            return jnp.broadcast_to(b[r:r + 1, :], (SUBLANES, GLA_DK))

        scores = mask_ref[len(GLA_LEVELS)] * _dot_nt(qb, kb)
        for l, s in enumerate(GLA_LEVELS):
            lhs, rhs = [], []
            for g in range(C // SUBLANES):
                r0 = g * SUBLANES
                rows = slice(r0, r0 + SUBLANES)
                if s >= SUBLANES:
                    m = mid_row((r0 // (2 * s)) * (2 * s) + s)
                    is_lhs = ((r0 & s) != 0) != reverse
                    if is_lhs:
                        lhs.append(q[rows] * jnp.exp2(b[rows] - m))
                        rhs.append(zero_rows)
                    else:
                        lhs.append(zero_rows)
                        rhs.append(k[rows] * jnp.exp2(m - b[rows]))
                else:
                    m = mid_row(r0 + SUBLANES - s)
                    for blk in range(SUBLANES // (2 * s) - 2, -1, -1):
                        m = jnp.where(sub_row < (blk + 1) * 2 * s, mid_row(r0 + blk * 2 * s + s), m)
                    upper = (sub_row & s) != 0
                    is_lhs = jnp.logical_not(upper) if reverse else upper
                    lhs.append(jnp.where(is_lhs, q[rows] * jnp.exp2(b[rows] - m), 0.0))
                    rhs.append(jnp.where(is_lhs, 0.0, k[rows] * jnp.exp2(m - b[rows])))
            lhs = jnp.concatenate(lhs, axis=0).astype(BF16)
            rhs = jnp.concatenate(rhs, axis=0).astype(BF16)
            scores = scores + mask_ref[l] * _dot_nt(lhs, rhs)
        sc_ref[c] = scores.astype(BF16)
        return carry

    lax.fori_loop(0, nchunks, chunk_scores, 0, unroll=GLA_UNROLL)

    def chunk(ci, carry):
        c = (nchunks - 1 - ci) if reverse else ci
        c0 = pl.multiple_of(c * C, C)
        q = q_ref[pl.ds(c0, C), :].astype(F32)
        k = k_ref[pl.ds(c0, C), :].astype(F32)
        v = _wide(v_ref, pl.ds(c0, C))
        b = b_ref[pl.ds(c0, C), :]
        b_end = b[0:1, :] if reverse else b[C - 1:C, :]

        st = st_ref[...]
        o = _dot_nt((q * jnp.exp2(b)).astype(BF16), st.astype(BF16))
        ke = (k * jnp.exp2(b_end - b)).astype(BF16)
        st_ref[...] = st * jnp.exp2(b_end) + _dot_tn(v, ke)
        o = o + _dot(sc_ref[c], v)
        if reverse:
            tot = of_ref[pl.ds(c0, C), :].astype(F32) + o
            o_ref[pl.ds(c0, C), :] = _finish_heads(tot, gn_ref[...], _wide(g_ref, pl.ds(c0, C)))
        else:
            o_ref[pl.ds(c0, C), :] = o.astype(BF16)
        return carry

    lax.fori_loop(0, nchunks, chunk, 0, unroll=GLA_UNROLL)


def _gla_scan(proj, bcum, batch, seq_len, reverse, o_fwd=None, gn_w=None):
    T = proj.shape[1]
    tb = GLA_TB
    nb = seq_len // tb
    dk, dv = GLA_DK, GLA_DV
    masks = _gla_tables(reverse)

    def rb(b, n):
        return b * nb + ((nb - 1 - n) if reverse else n)

    def key_block(base):
        return pl.BlockSpec((None, tb, LANE), lambda b, h, n: (base // dk + h, rb(b, n), 0))

    def value_block(base):
        return pl.BlockSpec((dv // LANE, tb, LANE), lambda b, h, n: (base // dv + h, rb(b, n), 0))

    in_specs = [
        key_block(_GQ), key_block(_GK), value_block(_GV),
        pl.BlockSpec((tb, dk), lambda b, h, n: (rb(b, n), h)),
        pl.BlockSpec(masks.shape, lambda b, h, n: (0, 0, 0)),
    ]
    args = [proj, proj, proj, bcum, masks]
    if reverse:
        in_specs += [
            value_block(_GG),
            pl.BlockSpec((tb, dv), lambda b, h, n: (rb(b, n), h)),
            pl.BlockSpec((1, dv), lambda b, h, n: (0, h)),
        ]
        args += [proj, o_fwd, gn_w]
    out_dtype = BF16
    return pl.pallas_call(
        functools.partial(_gla_kernel, reverse=reverse),
        grid=(batch, GLA_HEADS, nb),
        in_specs=in_specs,
        out_specs=pl.BlockSpec((tb, dv), lambda b, h, n: (rb(b, n), h)),
        out_shape=jax.ShapeDtypeStruct((T, GLA_WIDTH), out_dtype),
        scratch_shapes=[pltpu.VMEM((dv, dk), F32), pltpu.VMEM((tb // GLA_CHUNK, GLA_CHUNK, GLA_CHUNK), BF16)],
        compiler_params=_cparams(("parallel", "parallel", "arbitrary"), 32),
        name="gla_bwd" if reverse else "gla_fwd",
    )(*args)


OP_TM = 512
OP_SUB = 256


def _out_proj_kernel(mr_ref, mg_ref, w0_ref, w1_ref, x_ref, n2_ref, rh_ref, rl_ref, h_ref, xn_ref, aff_ref):
    for r in range(OP_TM // OP_SUB):
        rows = slice(r * OP_SUB, (r + 1) * OP_SUB)
        h = x_ref[rows, :] + _dot(mr_ref[rows, :], w0_ref[...]) + _dot(mg_ref[rows, :], w1_ref[...])
        h_ref[rows, :] = h
        ms = jnp.mean(h * h, axis=-1, keepdims=True)
        xn = h * lax.rsqrt(ms + EPS) * n2_ref[...]
        xh = xn.astype(BF16)
        xn_ref[rows, :] = _pack_bf16_pairs(xn)
        xl = (xn - xh.astype(F32)).astype(BF16)
        lt = _dot_nt(rh_ref[...], xh) + _dot_nt(rh_ref[...], xl) + _dot_nt(rl_ref[...], xh)
        m = jnp.max(lt, axis=0, keepdims=True)
        e = jnp.exp(lt - m)
        aff_ref[:, rows] = e / jnp.sum(e, axis=0, keepdims=True)


def _out_proj(mix_r, mix_g, w_out, x2d, n2w, r_hi, r_lo):
    T = x2d.shape[0]
    tm = OP_TM
    half = RET_WIDTH
    return pl.pallas_call(
        _out_proj_kernel,
        grid=(T // tm,),
        in_specs=[
            pl.BlockSpec((tm, half), lambda i: (i, 0)),
            pl.BlockSpec((tm, half), lambda i: (i, 0)),
            pl.BlockSpec((half, D_MODEL), lambda i: (0, 0)),
            pl.BlockSpec((half, D_MODEL), lambda i: (1, 0)),
            pl.BlockSpec((tm, D_MODEL), lambda i: (i, 0)),
            pl.BlockSpec((1, D_MODEL), lambda i: (0, 0)),
            pl.BlockSpec((N_EXPERTS, D_MODEL), lambda i: (0, 0)),
            pl.BlockSpec((N_EXPERTS, D_MODEL), lambda i: (0, 0)),
        ],
        out_specs=[
            pl.BlockSpec((tm, D_MODEL), lambda i: (i, 0)),
            pl.BlockSpec((tm, D_MODEL // 2), lambda i: (i, 0)),
            pl.BlockSpec((N_EXPERTS, tm), lambda i: (0, i)),
        ],
        out_shape=[
            jax.ShapeDtypeStruct((T, D_MODEL), F32),
            jax.ShapeDtypeStruct((T, D_MODEL // 2), I32),
            jax.ShapeDtypeStruct((N_EXPERTS, T), F32),
        ],
        compiler_params=_cparams(("parallel",), 56),
        name="out_proj",
    )(mix_r, mix_g, w_out, w_out, x2d, n2w, r_hi, r_lo)


def _select_kernel(a_ref, pos_ref, posb_ref, rankb_ref, affb_ref, *, cap):
    E, T = a_ref.shape
    tt = MOE_TT

    def count(pred):
        return jnp.sum(pred.astype(F32), axis=1, keepdims=True)

    def bisect(i, tau):
        cand = tau | jnp.left_shift(jnp.int32(1), 30 - i)
        bits = pltpu.bitcast(a_ref[...], I32)
        return jnp.where(count(bits >= cand) >= cap, cand, tau)

    tau = lax.fori_loop(0, 31, bisect, jnp.zeros((E, 1), I32))
    bits_all = pltpu.bitcast(a_ref[...], I32)
    quota = cap - count(bits_all > tau)

    before = (lax.broadcasted_iota(I32, (tt, tt), 0) < lax.broadcasted_iota(I32, (tt, tt), 1)).astype(BF16)
    below = (lax.broadcasted_iota(I32, (E, E), 1) < lax.broadcasted_iota(I32, (E, E), 0)).astype(BF16)

    def block(j, carry):
        c_eq, c_sel = carry
        off = pl.multiple_of(j * tt, tt)
        aff = a_ref[:, pl.ds(off, tt)]
        bits = pltpu.bitcast(aff, I32)
        eq = bits == tau
        eqf = eq.astype(F32)
        rank_eq = _dot(eqf.astype(BF16), before) + c_eq
        sel = (bits > tau) | (eq & (rank_eq < quota))
        self_ = sel.astype(F32)
        selb = self_.astype(BF16)
        slot = _dot(selb, before) + c_sel
        pos = jnp.where(sel, slot, -1.0).astype(I32)
        pos_ref[:, pl.ds(off, tt)] = pos
        per_tok = jnp.broadcast_to(jnp.sum(self_, axis=0, keepdims=True), (E, tt))
        rank = _dot(per_tok.astype(BF16), before) + _dot(below, selb)
        posb_ref[j] = pos
        rankb_ref[j] = jnp.where(sel, rank, -1.0).astype(I32)
        affb_ref[j] = aff
        return (c_eq + jnp.sum(eqf, axis=1, keepdims=True), c_sel + jnp.sum(self_, axis=1, keepdims=True))

    zero = jnp.zeros((E, 1), F32)
    lax.fori_loop(0, T // tt, block, (zero, zero))


def _select(aff, cap):
    E, T = aff.shape
    nb = T // MOE_TT
    blk = jax.ShapeDtypeStruct((nb, E, MOE_TT), I32)
    return pl.pallas_call(
        functools.partial(_select_kernel, cap=cap),
        out_shape=[jax.ShapeDtypeStruct((E, T), I32), blk, blk, jax.ShapeDtypeStruct((nb, E, MOE_TT), F32)],
        compiler_params=pltpu.CompilerParams(vmem_limit_bytes=40 * 1024 * 1024),
        name="select",
    )(aff)


def _regroup_rows(T):
    nb = T // MOE_TT
    rows = CAPACITY_FACTOR * T + SC_GATHER_ROWS * nb
    return -(-rows // MOE_TW) * MOE_TW


def _combine_schedule(posb, T):
    nb = posb.shape[0]
    tw, g = MOE_TW, SC_GATHER_ROWS
    n = jnp.sum((posb >= 0).reshape(nb, -1), axis=1).astype(I32)
    seg = (n + g - 1) // g * g
    hi = jnp.cumsum(seg)
    lo = hi - seg
    off = jnp.concatenate([jnp.zeros((1,), I32), hi])
    nwin_max = (N_EXPERTS * MOE_TT) // tw + 1
    w0 = lo // tw
    w1 = jnp.where(seg > 0, (hi - 1) // tw, w0)
    cand = jnp.arange(nwin_max, dtype=I32)
    win = w0[:, None] + cand[None, :]
    valid = (win <= w1[:, None]).reshape(-1)
    nwin_total = _regroup_rows(T) // tw
    pmax = nb + nwin_total
    jv = jnp.broadcast_to(jnp.arange(nb, dtype=I32)[:, None], win.shape).reshape(-1)
    wv = jnp.minimum(win, nwin_total - 1).reshape(-1)
    total = jnp.sum(valid.astype(I32))
    dst = jnp.where(valid, jnp.cumsum(valid.astype(I32)) - 1, pmax)
    pj, pw = (jnp.zeros((pmax,), I32).at[dst].set(a, mode="drop") for a in (jv, wv))
    real = jnp.arange(pmax, dtype=I32) < total
    pj, pw = (jnp.where(real, a, a[total - 1]) for a in (pj, pw))
    first = jnp.concatenate([jnp.ones((1,), bool), pj[1:] != pj[:-1]])
    last = jnp.concatenate([pj[1:] != pj[:-1], jnp.ones((1,), bool)]) | (jnp.arange(pmax, dtype=I32) == total - 1)
    flag = jnp.where(real, first.astype(I32) + 2 * last.astype(I32) + 4, 0)
    return off, (pj, pw, flag, lo[pj], hi[pj])


SC_LANES = 16
SC_CORES = 2
SC_SUBCORES = 16
SC_GATHER_ROWS = 32


def _dispatch(pos, xw, cap):
    E, T = pos.shape
    W = xw.shape[1]
    G = SC_GATHER_ROWS
    part_rows = cap // SC_CORES
    mesh = plsc.VectorSubcoreMesh(core_axis_name="c", subcore_axis_name="s")

    @pl.kernel(
        out_type=jax.ShapeDtypeStruct((E * cap, W), I32),
        mesh=mesh,
        scratch_types=[pltpu.VMEM((T,), I32), pltpu.VMEM((cap,), I32), pltpu.VMEM((G, W), I32)],
        compiler_params=pltpu.CompilerParams(needs_layout_passes=False),
        name="sc_dispatch",
    )
    def run(pos_hbm, x_hbm, xe_hbm, pos_v, idx_v, buf):
        e = lax.axis_index("s")
        part = lax.axis_index("c")
        pltpu.sync_copy(pos_hbm.at[e], pos_v)
        lane = lax.iota(I32, SC_LANES)

        @pl.loop(0, T // SC_LANES)
        def _(i):
            off = pl.multiple_of(i * SC_LANES, SC_LANES)
            p = pos_v[pl.ds(off, SC_LANES)]
            plsc.store_scatter(idx_v, [p], lane + off, mask=p >= 0)

        @pl.loop(0, part_rows // G)
        def _(g):
            o = pl.multiple_of(part * part_rows + g * G, G)
            pltpu.sync_copy(x_hbm.at[idx_v.at[pl.ds(o, G)]], buf)
            pltpu.sync_copy(buf, xe_hbm.at[pl.ds(e * cap + o, G)])

    return run(pos, xw)


FFN_TM = 2048
FFN_UNPACK_ROWS = 256


def _ffn_tile_width(cap):
    return 512 if min(FFN_TM, cap) <= 1024 else 256


def _ffn_kernel(x_ref, wg_ref, wu_ref, wd_ref, o_ref, xb_ref, hid_ref, *, tw):
    s = pl.program_id(2)
    n_up = D_FF // tw

    @pl.when(s == 0)
    def _():
        half = D_MODEL // 2

        def unpack_rows(i, carry):
            r = pl.multiple_of(i * FFN_UNPACK_ROWS, FFN_UNPACK_ROWS)
            lo, hi = _unpack_bf16_pairs(x_ref[pl.ds(r, FFN_UNPACK_ROWS), :])
            xb_ref[pl.ds(r, FFN_UNPACK_ROWS), :half] = lo.astype(BF16)
            xb_ref[pl.ds(r, FFN_UNPACK_ROWS), half:] = hi.astype(BF16)
            return carry

        lax.fori_loop(0, x_ref.shape[0] // FFN_UNPACK_ROWS, unpack_rows, 0)

    @pl.when(s < n_up)
    def _():
        x = xb_ref[...]
        g = _dot(x, wg_ref[...].astype(BF16))
        u = _dot(x, wu_ref[...].astype(BF16))
        col = pl.multiple_of(s * tw, tw)
        hid_ref[:, pl.ds(col, tw)] = (_silu(g) * u).astype(BF16)

    @pl.when(s >= n_up)
    def _():
        o_ref[...] = _pack_bf16_pairs(_dot(hid_ref[...], wd_ref[...].astype(BF16)))


def _ffn(xe, w_gate, w_up, w_down):
    E, cap, _ = xe.shape
    tm = min(FFN_TM, cap)
    tw = _ffn_tile_width(cap)
    n_up, n_down = D_FF // tw, D_MODEL // tw

    def up(e, m, s):
        return (e, 0, jnp.minimum(s, n_up - 1))

    def down(s):
        return jnp.maximum(s - n_up, 0)

    return pl.pallas_call(
        functools.partial(_ffn_kernel, tw=tw),
        grid=(E, cap // tm, n_up + n_down),
        in_specs=[
            pl.BlockSpec((None, tm, D_MODEL // 2), lambda e, m, s: (e, m, 0)),
            pl.BlockSpec((None, D_MODEL, tw), up),
            pl.BlockSpec((None, D_MODEL, tw), up),
            pl.BlockSpec((None, D_FF, tw), lambda e, m, s: (e, 0, down(s))),
        ],
        out_specs=pl.BlockSpec((None, tm, tw // 2), lambda e, m, s: (e, m, down(s))),
        out_shape=jax.ShapeDtypeStruct((E, cap, D_MODEL // 2), I32),
        scratch_shapes=[pltpu.VMEM((tm, D_MODEL), BF16), pltpu.VMEM((tm, D_FF), BF16)],
        compiler_params=_cparams(("parallel", "parallel", "arbitrary"), 60),
        name="ffn",
    )(xe, w_gate, w_up, w_down)


def _regroup(posb, rankb, affb, off, yw, cap, rows):
    NB, EB = posb.shape
    W = yw.shape[1]
    G, L, tt = SC_GATHER_ROWS, SC_LANES, MOE_TT
    n_workers = SC_CORES * SC_SUBCORES
    per = NB // n_workers
    mesh = plsc.VectorSubcoreMesh(core_axis_name="c", subcore_axis_name="s")
    off_pad = jnp.pad(off, (0, L))

    @pl.kernel(
        out_type=(jax.ShapeDtypeStruct((rows, W), I32), jax.ShapeDtypeStruct((rows,), I32),
                  jax.ShapeDtypeStruct((rows,), F32)),
        mesh=mesh,
        scratch_types=[pltpu.VMEM((EB,), I32), pltpu.VMEM((EB,), I32), pltpu.VMEM((EB,), F32),
                       pltpu.VMEM((EB,), I32), pltpu.VMEM((EB,), I32), pltpu.VMEM((EB,), F32),
                       pltpu.VMEM((G, W), I32), pltpu.VMEM((NB + 1 + L,), I32)],
        compiler_params=pltpu.CompilerParams(needs_layout_passes=False),
        name="sc_regroup",
    )
    def run(posb_hbm, rankb_hbm, affb_hbm, off_hbm, y_hbm, yg_hbm, tok_hbm, gate_hbm,
            pos_v, rank_v, aff_v, src_v, tok_v, gate_v, buf, off_v):
        wid = lax.axis_index("c") * SC_SUBCORES + lax.axis_index("s")
        pltpu.sync_copy(off_hbm, off_v)
        lane = lax.iota(I32, L)
        zi = jnp.zeros((L,), I32)
        zf = jnp.zeros((L,), F32)

        @pl.loop(0, per)
        def _(k):
            j = wid * per + k
            pltpu.sync_copy(posb_hbm.at[j], pos_v)
            pltpu.sync_copy(rankb_hbm.at[j], rank_v)
            pltpu.sync_copy(affb_hbm.at[j], aff_v)
            lo = jnp.max(plsc.load_gather(off_v, [zi + j]))
            hi = jnp.max(plsc.load_gather(off_v, [zi + j + 1]))

            @pl.loop(0, EB // L)
            def _(i):
                o = pl.multiple_of(i * L, L)
                src_v[pl.ds(o, L)] = zi
                tok_v[pl.ds(o, L)] = zi
                gate_v[pl.ds(o, L)] = zf

            @pl.loop(0, EB // L)
            def _(i):
                o = pl.multiple_of(i * L, L)
                p = pos_v[pl.ds(o, L)]
                r = rank_v[pl.ds(o, L)]
                m = p >= 0
                e = i // (tt // L)
                t0 = j * tt + (i % (tt // L)) * L
                plsc.store_scatter(src_v, [r], p + e * cap, mask=m)
                plsc.store_scatter(tok_v, [r], lane + t0, mask=m)
                plsc.store_scatter(gate_v, [r], aff_v[pl.ds(o, L)], mask=m)

            @pl.loop(0, (hi - lo) // G)
            def _(g):
                o = pl.multiple_of(g * G, G)
                dst = pl.multiple_of(lo + o, G)
                pltpu.sync_copy(y_hbm.at[src_v.at[pl.ds(o, G)]], buf)
                pltpu.sync_copy(buf, yg_hbm.at[pl.ds(dst, G)])
                pltpu.sync_copy(tok_v.at[pl.ds(o, G)], tok_hbm.at[pl.ds(dst, G)])
                pltpu.sync_copy(gate_v.at[pl.ds(o, G)], gate_hbm.at[pl.ds(dst, G)])

    return run(posb, rankb, affb, off_pad, yw)


def _combine_kernel(pj_ref, pw_ref, pf_ref, plo_ref, phi_ref, tok_ref, gate_ref, yg_ref, h_ref, nf_ref,
                    o_ref, acc_ref, *, group):
    p = pl.program_id(0)
    flag = pf_ref[p]
    tw, tt = MOE_TW, MOE_TT
    half = D_MODEL // 2

    hw = group // 2

    def col_blocks():
        for n in range(D_MODEL // group):
            yield slice(n * hw, (n + 1) * hw), slice(n * group, n * group + hw)
            yield slice(half + n * hw, half + (n + 1) * hw), slice(n * group + hw, (n + 1) * group)

    @pl.when((flag & 1) != 0)
    def _():
        for packed, natural in col_blocks():
            acc_ref[:, packed] = h_ref[:, natural]

    @pl.when((flag & 4) != 0)
    def _():
        lo, hi = plo_ref[p], phi_ref[p]
        row0 = pw_ref[p] * tw
        rid = lax.broadcasted_iota(I32, (tw, 1), 0) + row0
        keep = (rid >= lo) & (rid < hi)
        y_lo, y_hi = _unpack_bf16_pairs(yg_ref[...])
        y_lo = jnp.where(keep, y_lo, 0.0).astype(BF16)
        y_hi = jnp.where(keep, y_hi, 0.0).astype(BF16)
        tid = lax.broadcasted_iota(I32, (tt, tw), 0) + pj_ref[p] * tt
        cid = lax.broadcasted_iota(I32, (tt, tw), 1) + row0
        hit = (tok_ref[...] == tid) & (cid >= lo) & (cid < hi)
        weights = jnp.where(hit, gate_ref[...], 0.0).astype(BF16)
        acc_ref[:, :half] += _dot(weights, y_lo)
        acc_ref[:, half:] += _dot(weights, y_hi)

    @pl.when((flag & 2) != 0)
    def _():
        y = acc_ref[...]
        scale = lax.rsqrt(jnp.mean(y * y, axis=-1, keepdims=True) + EPS)
        for packed, natural in col_blocks():
            o_ref[:, natural] = acc_ref[:, packed] * scale * nf_ref[:, natural]


def _combine(lists, tok, gate, yg, h, nfw, group):
    pj, pw, pf, plo, phi = lists
    T = h.shape[0]
    tw, tt = MOE_TW, MOE_TT
    nwin = yg.shape[0] // tw
    grid_spec = pltpu.PrefetchScalarGridSpec(
        num_scalar_prefetch=5,
        grid=(pj.shape[0],),
        in_specs=[
            pl.BlockSpec((None, 1, tw), lambda p, pj, pw, *_: (pw[p], 0, 0)),
            pl.BlockSpec((None, 1, tw), lambda p, pj, pw, *_: (pw[p], 0, 0)),
            pl.BlockSpec((tw, D_MODEL // 2), lambda p, pj, pw, *_: (pw[p], 0)),
            pl.BlockSpec((tt, D_MODEL), lambda p, pj, pw, *_: (pj[p], 0)),
            pl.BlockSpec((1, D_MODEL), lambda p, pj, pw, *_: (0, 0)),
        ],
        out_specs=pl.BlockSpec((tt, D_MODEL), lambda p, pj, pw, *_: (pj[p], 0)),
        scratch_shapes=[pltpu.VMEM((tt, D_MODEL), F32)],
    )
    return pl.pallas_call(
        functools.partial(_combine_kernel, group=group),
        grid_spec=grid_spec,
        out_shape=jax.ShapeDtypeStruct((T, D_MODEL), F32),
        compiler_params=_cparams(("arbitrary",), 32),
        name="combine",
    )(pj, pw, pf, plo, phi, tok.reshape(nwin, 1, tw), gate.reshape(nwin, 1, tw), yg, h, nfw)


def _rope_tables(seq_len):
    d = RET_DK
    inv = ROPE_BASE ** (-jnp.arange(0, d, 2, dtype=F32) / d)
    ang = jnp.arange(seq_len, dtype=F32)[:, None] * inv[None, :]
    return jnp.cos(ang), jnp.sin(ang)


def _chunk_tri(n, chunk, upper):
    r = np.arange(n)
    same = (r[:, None] // chunk) == (r[None, :] // chunk)
    tri = (r[:, None] <= r[None, :]) if upper else (r[:, None] >= r[None, :])
    return jnp.asarray(same & tri, BF16)


def _prep_params(norm1_w, w_in, ret_gn_w, gla_gate_up, gla_gate_bias, gla_gn_w, w_out, norm2_w, router_w,
                 normf_w):
    w = w_in[0]
    w_main = w[:, :IN_MAIN].astype(BF16)
    w_ga = jnp.pad(w[:, IN_MAIN:], ((0, 0), (0, LANE - 2 * GLA_RANK))).astype(BF16)
    cs = np.ones((1, IN_MAIN), np.float32)
    cs[:, _RQ:_RQ + RET_WIDTH] = RET_DK ** -0.5
    cs[:, _GQ:_GQ + GLA_KEY_WIDTH] = GLA_DK ** -0.5
    up = gla_gate_up[0].astype(F32)
    up_pad = jnp.zeros((LANE, 2 * GLA_KEY_WIDTH), F32)
    up_pad = up_pad.at[:GLA_RANK, :GLA_KEY_WIDTH].set(up[0])
    up_pad = up_pad.at[GLA_RANK:2 * GLA_RANK, GLA_KEY_WIDTH:].set(up[1])
    rt = router_w[0].T.astype(F32)
    r_hi = rt.astype(BF16)
    r_lo = (rt - r_hi.astype(F32)).astype(BF16)
    return dict(
        n1w=norm1_w[0].reshape(1, D_MODEL).astype(F32),
        w_main=w_main, w_ga=w_ga, colscale=jnp.asarray(cs),
        up_pad=up_pad.astype(BF16),
        bias=gla_gate_bias[0].reshape(1, 2 * GLA_KEY_WIDTH).astype(F32),
        lf=_chunk_tri(GATE_TM, GLA_CHUNK, upper=False),
        lb=_chunk_tri(GATE_TM, GLA_CHUNK, upper=True),
        ret_gn=ret_gn_w[0].reshape(1, RET_WIDTH).astype(F32),
        gla_gn=gla_gn_w[0].reshape(1, GLA_WIDTH).astype(F32),
        w_out=w_out[0].astype(BF16),
        n2w=norm2_w[0].reshape(1, D_MODEL).astype(F32),
        r_hi=r_hi, r_lo=r_lo,
        nfw=normf_w.reshape(1, D_MODEL).astype(F32),
    )


def _trunk(x, pp, decay_logit, w_gate, w_up, w_down):
    B, L, _ = x.shape
    T = B * L
    x2d = x.reshape(T, D_MODEL)
    cos, sin = _rope_tables(L)
    proj, ga = _in_proj(x2d, pp["n1w"], pp["w_main"], pp["w_ga"], pp["colscale"], cos, sin, L)
    b_f, b_b = _gla_gates(ga, pp["up_pad"], pp["bias"], pp["lf"], pp["lb"])

    ret_f = _ret_scan(proj, decay_logit, B, L, reverse=False)
    mix_r = _ret_scan(proj, decay_logit, B, L, reverse=True, o_fwd=ret_f, gn_w=pp["ret_gn"])
    gla_f = _gla_scan(proj, b_f, B, L, reverse=False)
    mix_g = _gla_scan(proj, b_b, B, L, reverse=True, o_fwd=gla_f, gn_w=pp["gla_gn"])

    h, xn2, aff = _out_proj(mix_r, mix_g, pp["w_out"], x2d, pp["n2w"], pp["r_hi"], pp["r_lo"])

    cap = CAPACITY_FACTOR * T // N_EXPERTS
    pos, posb, rankb, affb = _select(aff, cap)
    off, c_lists = _combine_schedule(posb, T)
    xe = _dispatch(pos, xn2, cap).reshape(N_EXPERTS, cap, D_MODEL // 2)
    ye = _ffn(xe, w_gate, w_up, w_down).reshape(N_EXPERTS * cap, D_MODEL // 2)
    nb = T // MOE_TT
    yg, tok, gate = _regroup(posb.reshape(nb, -1), rankb.reshape(nb, -1), affb.reshape(nb, -1), off, ye, cap,
                             _regroup_rows(T))
    y = _combine(c_lists, tok, gate, yg, h, pp["nfw"], _ffn_tile_width(cap))
    return y.reshape(B, L, D_MODEL)


def kernel(x_prompt, x_sample, norm1_w, w_in, ret_decay_logit, ret_gn_w, gla_gate_up, gla_gate_bias,
           gla_gn_w, w_out, norm2_w, router_w, w_gate, w_up, w_down, normf_w):
    pp = _prep_params(norm1_w, w_in, ret_gn_w, gla_gate_up, gla_gate_bias, gla_gn_w, w_out, norm2_w,
                      router_w, normf_w)
    decay_logit = ret_decay_logit[0].astype(F32)
    args = (pp, decay_logit, w_gate[0], w_up[0], w_down[0])
    return (_trunk(x_prompt, *args), _trunk(x_sample, *args))
```

```python
import functools

import numpy as np
import jax
import jax.numpy as jnp
from jax import lax
from jax.experimental import pallas as pl
from jax.experimental.pallas import tpu as pltpu
from jax.experimental.pallas import tpu_sc as plsc

F32, BF16, I32 = jnp.float32, jnp.bfloat16, jnp.int32

D_MODEL = 2048
RET_WIDTH = 1024
RET_HEADS = 4
RET_DK = 256
RET_DV = 256
GLA_WIDTH = 1024
GLA_HEADS = 4
GLA_DK = 128
GLA_DV = 256
GLA_KEY_WIDTH = 512
GLA_RANK = 16
GLA_TAU = 16.0
RET_CHUNK = 256
GLA_CHUNK = 64
GLA_SUB = 16
ROPE_BASE = 10000.0
N_EXPERTS = 16
CAPACITY_FACTOR = 2
D_FF = 2048
EPS = 1e-6
LOG2_E = 1.4426950408889634
IN_MAIN = 4 * RET_WIDTH + 2 * GLA_KEY_WIDTH + 2 * GLA_WIDTH

_RQ, _RK, _RV, _RG = 0, 1024, 2048, 3072
_GQ, _GK, _GV, _GG = 4096, 4608, 5120, 6144

LANE = 128
MOE_TT = 256
MOE_TW = 256
V7X_VMEM_BYTES = 64 * 1024 * 1024


def _cparams(sem, vmem_mb):
    return pltpu.CompilerParams(dimension_semantics=sem, vmem_limit_bytes=vmem_mb * 1024 * 1024)


def _log_sigmoid(z):
    return jnp.minimum(z, 0.0) - jnp.log1p(jnp.exp(-jnp.abs(z)))


def _silu(g):
    return g * (1.0 / (1.0 + jnp.exp(-g)))


def _dot_nt(a, b):
    return lax.dot_general(a, b, (((1,), (1,)), ((), ())), preferred_element_type=F32)


def _dot_tn(a, b):
    return lax.dot_general(a, b, (((0,), (0,)), ((), ())), preferred_element_type=F32)


def _dot(a, b):
    return jnp.dot(a, b, preferred_element_type=F32)


def _pack_bf16_pairs(x):
    bits = pltpu.bitcast(x.astype(BF16).astype(F32), I32)
    w = x.shape[1] // 2
    return bits[:, w:] | lax.shift_right_logical(bits[:, :w], 16)


def _unpack_bf16_pairs(words):
    lo = pltpu.bitcast(lax.shift_left(words, 16), F32)
    hi = pltpu.bitcast(words & jnp.int32(-65536), F32)
    return lo, hi


IP_TM = 1024
IP_TN = 1024


def _in_proj_kernel(x_ref, n1_ref, w_ref, wga_ref, cs_ref, cos_ref, sin_ref, o_ref, ga_ref, xn_ref):
    j = pl.program_id(1)

    @pl.when(j == 0)
    def _():
        x = x_ref[...]
        ms = jnp.mean(x * x, axis=-1, keepdims=True)
        xn = (x * lax.rsqrt(ms + EPS) * n1_ref[...]).astype(BF16)
        xn_ref[...] = xn
        ga_ref[...] = _dot(xn, wga_ref[...])

    acc = _dot(xn_ref[...], w_ref[...]) * cs_ref[...]
    n_rope_blocks = 2 * RET_WIDTH // IP_TN

    @pl.when(j < n_rope_blocks)
    def _():
        cos = cos_ref[...]
        sin = sin_ref[...]
        for h in range(IP_TN // RET_DK):
            x1 = acc[:, 2 * h * LANE:(2 * h + 1) * LANE]
            x2 = acc[:, (2 * h + 1) * LANE:(2 * h + 2) * LANE]
            o_ref[2 * h] = (x1 * cos - x2 * sin).astype(BF16)
            o_ref[2 * h + 1] = (x1 * sin + x2 * cos).astype(BF16)

    @pl.when(j >= n_rope_blocks)
    def _():
        for c in range(IP_TN // LANE):
            o_ref[c] = acc[:, c * LANE:(c + 1) * LANE].astype(BF16)


def _in_proj(x2d, n1w, w_main, w_ga, colscale, cos, sin, seq_len):
    T = x2d.shape[0]
    tm, tn = IP_TM, IP_TN
    nlb = seq_len // tm
    return pl.pallas_call(
        _in_proj_kernel,
        grid=(T // tm, IN_MAIN // tn),
        in_specs=[
            pl.BlockSpec((tm, D_MODEL), lambda i, j: (i, 0)),
            pl.BlockSpec((1, D_MODEL), lambda i, j: (0, 0)),
            pl.BlockSpec((D_MODEL, tn), lambda i, j: (0, j)),
            pl.BlockSpec((D_MODEL, LANE), lambda i, j: (0, 0)),
            pl.BlockSpec((1, tn), lambda i, j: (0, j)),
            pl.BlockSpec((tm, LANE), lambda i, j: (i % nlb, 0)),
            pl.BlockSpec((tm, LANE), lambda i, j: (i % nlb, 0)),
        ],
        out_specs=[
            pl.BlockSpec((tn // LANE, tm, LANE), lambda i, j: (j, i, 0)),
            pl.BlockSpec((tm, LANE), lambda i, j: (i, 0)),
        ],
        out_shape=[
            jax.ShapeDtypeStruct((IN_MAIN // LANE, T, LANE), BF16),
            jax.ShapeDtypeStruct((T, LANE), F32),
        ],
        scratch_shapes=[pltpu.VMEM((tm, D_MODEL), BF16)],
        compiler_params=_cparams(("parallel", "arbitrary"), 48),
        name="in_proj",
    )(x2d, n1w, w_main, w_ga, colscale, cos, sin)


GATE_TM = 512


def _gates_kernel(ga_ref, up_ref, bias_ref, lf_ref, lb_ref, bf_ref, bb_ref):
    z = _dot(ga_ref[...].astype(BF16), up_ref[...]) + bias_ref[...]
    la = _log_sigmoid(z) * (LOG2_E / GLA_TAU)
    hi = la.astype(BF16)
    lo = (la - hi.astype(F32)).astype(BF16)
    kw = GLA_KEY_WIDTH
    bf_ref[...] = _dot(lf_ref[...], hi[:, :kw]) + _dot(lf_ref[...], lo[:, :kw])
    bb_ref[...] = _dot(lb_ref[...], hi[:, kw:]) + _dot(lb_ref[...], lo[:, kw:])


def _gla_gates(ga, up_pad, bias, lf, lb):
    T = ga.shape[0]
    tm = GATE_TM
    kw = GLA_KEY_WIDTH
    return pl.pallas_call(
        _gates_kernel,
        grid=(T // tm,),
        in_specs=[
            pl.BlockSpec((tm, LANE), lambda i: (i, 0)),
            pl.BlockSpec((LANE, 2 * kw), lambda i: (0, 0)),
            pl.BlockSpec((1, 2 * kw), lambda i: (0, 0)),
            pl.BlockSpec((tm, tm), lambda i: (0, 0)),
            pl.BlockSpec((tm, tm), lambda i: (0, 0)),
        ],
        out_specs=[pl.BlockSpec((tm, kw), lambda i: (i, 0)), pl.BlockSpec((tm, kw), lambda i: (i, 0))],
        out_shape=[jax.ShapeDtypeStruct((T, kw), F32), jax.ShapeDtypeStruct((T, kw), F32)],
        compiler_params=_cparams(("parallel",), 32),
        name="gla_gates",
    )(ga, up_pad, bias, lf, lb)


def _wide(ref, rows):
    return jnp.concatenate([ref[0, rows, :], ref[1, rows, :]], axis=1)


def _finish_heads(tot, gn, gate):
    ms = jnp.mean(tot * tot, axis=-1, keepdims=True)
    yn = tot * lax.rsqrt(ms + EPS) * gn
    return (yn * _silu(gate.astype(F32))).astype(BF16)


RET_TB = 1024


def _ret_kernel(dl_ref, q_ref, k_ref, v_ref, *rest, reverse):
    if reverse:
        g_ref, of_ref, gn_ref, o_ref, s_ref, intra_ref, qd_ref, kd_ref, cd_ref, p_ref, u_ref = rest
    else:
        o_ref, s_ref, intra_ref, qd_ref, kd_ref, cd_ref, p_ref, u_ref = rest
    h = pl.program_id(1)
    n = pl.program_id(2)
    C = RET_CHUNK

    @pl.when(n == 0)
    def _():
        s_ref[...] = jnp.zeros_like(s_ref)
        logit = dl_ref[1 if reverse else 0, h]
        lg = _log_sigmoid(jnp.full((C, RET_DV), logit, F32))
        lg_c = _log_sigmoid(jnp.full((C, C), logit, F32))
        lg_r = _log_sigmoid(jnp.full((1, RET_DV), logit, F32))
        ri = lax.broadcasted_iota(I32, (C, RET_DV), 0).astype(F32)
        rc = lax.broadcasted_iota(I32, (C, C), 0).astype(F32)
        cc = lax.broadcasted_iota(I32, (C, C), 1).astype(F32)
        diff = (cc - rc) if reverse else (rc - cc)
        intra_ref[...] = jnp.where(diff >= 0, jnp.exp(lg_c * diff), 0.0)
        if reverse:
            qd_ref[...] = jnp.exp(lg * (C - ri))
            kd_ref[...] = jnp.exp(lg * ri)
        else:
            qd_ref[...] = jnp.exp(lg * (ri + 1.0))
            kd_ref[...] = jnp.exp(lg * (C - 1.0 - ri))
        cd_ref[...] = jnp.exp(lg_r * C)

    nchunks = o_ref.shape[0] // C
    for c in range(nchunks):
        rows = slice(c * C, (c + 1) * C)
        k = _wide(k_ref, rows)
        p_ref[c] = (_dot_nt(_wide(q_ref, rows), k) * intra_ref[...]).astype(BF16)
        kd = (k.astype(F32) * kd_ref[...]).astype(BF16)
        u_ref[c] = _dot_tn(kd, _wide(v_ref, rows))
    order = range(nchunks - 1, -1, -1) if reverse else range(nchunks)
    for c in order:
        rows = slice(c * C, (c + 1) * C)
        state = s_ref[...]
        o = _dot(p_ref[c], _wide(v_ref, rows)) + _dot(_wide(q_ref, rows), state.astype(BF16)) * qd_ref[...]
        s_ref[...] = state * cd_ref[...] + u_ref[c]
        if reverse:
            tot = of_ref[rows, :].astype(F32) + o
            o_ref[rows, :] = _finish_heads(tot, gn_ref[...], _wide(g_ref, rows))
        else:
            o_ref[rows, :] = o.astype(BF16)


def _ret_parts(proj, decay_logit, rb, reverse, o_fwd=None, gn_w=None):
    T = proj.shape[1]
    tb = RET_TB
    dk, dv, C = RET_DK, RET_DV, RET_CHUNK

    def head(base):
        return pl.BlockSpec((dk // LANE, tb, LANE), lambda b, h, n: (base // dk + h, rb(b, n), 0))

    in_specs = [pl.BlockSpec(memory_space=pltpu.SMEM), head(_RQ), head(_RK), head(_RV)]
    args = [decay_logit, proj, proj, proj]
    if reverse:
        in_specs += [
            head(_RG),
            pl.BlockSpec((tb, dv), lambda b, h, n: (rb(b, n), h)),
            pl.BlockSpec((1, dv), lambda b, h, n: (0, h)),
        ]
        args += [proj, o_fwd, gn_w]
    out_spec = pl.BlockSpec((tb, dv), lambda b, h, n: (rb(b, n), h))
    out_shape = jax.ShapeDtypeStruct((T, RET_WIDTH), BF16)
    scratch = [
        pltpu.VMEM((dk, dv), F32),
        pltpu.VMEM((C, C), F32),
        pltpu.VMEM((C, dv), F32),
        pltpu.VMEM((C, dk), F32),
        pltpu.VMEM((1, dv), F32),
        pltpu.VMEM((tb // C, C, C), BF16),
        pltpu.VMEM((tb // C, dk, dv), F32),
    ]
    return in_specs, args, out_spec, out_shape, scratch


GLA_TB = 1024
GLA_UNROLL = 16


GLA_LEVELS = (32, 16, 8, 4, 2, 1)
SUBLANES = 8


def _gla_tables(reverse):
    C = GLA_CHUNK
    r = np.arange(C)
    masks = np.zeros((len(GLA_LEVELS) + 1, C, C), np.float32)
    for l, s in enumerate(GLA_LEVELS):
        upper = (r & s) != 0
        same = (r[:, None] // (2 * s)) == (r[None, :] // (2 * s))
        lhs_rows = ~upper if reverse else upper
        masks[l] = same & lhs_rows[:, None] & ~lhs_rows[None, :]
    masks[-1] = np.eye(C)
    return jnp.asarray(masks, F32)


def _gla_kernel(q_ref, k_ref, v_ref, b_ref, mask_ref, *rest, reverse):
    if reverse:
        g_ref, of_ref, gn_ref, o_ref, st_ref, sc_ref = rest
    else:
        o_ref, st_ref, sc_ref = rest
    n = pl.program_id(2)
    C = GLA_CHUNK

    @pl.when(n == 0)
    def _():
        st_ref[...] = jnp.zeros_like(st_ref)

    nchunks = q_ref.shape[0] // C
    sub_row = lax.broadcasted_iota(I32, (SUBLANES, GLA_DK), 0)
    zero_rows = jnp.zeros((SUBLANES, GLA_DK), F32)

    def chunk_scores(c, carry):
        c0 = pl.multiple_of(c * C, C)
        qb = q_ref[pl.ds(c0, C), :]
        kb = k_ref[pl.ds(c0, C), :]
        q = qb.astype(F32)
        k = kb.astype(F32)
        b = b_ref[pl.ds(c0, C), :]

        def mid_row(r):
            return jnp.broadcast_to(b[r:r + 1, :], (SUBLANES, GLA_DK))

        scores = mask_ref[len(GLA_LEVELS)] * _dot_nt(qb, kb)
        for l, s in enumerate(GLA_LEVELS):
            lhs, rhs = [], []
            for g in range(C // SUBLANES):
                r0 = g * SUBLANES
                rows = slice(r0, r0 + SUBLANES)
                if s >= SUBLANES:
                    m = mid_row((r0 // (2 * s)) * (2 * s) + s)
                    is_lhs = ((r0 & s) != 0) != reverse
                    if is_lhs:
                        lhs.append(q[rows] * jnp.exp2(b[rows] - m))
                        rhs.append(zero_rows)
                    else:
                        lhs.append(zero_rows)
                        rhs.append(k[rows] * jnp.exp2(m - b[rows]))
                else:
                    m = mid_row(r0 + SUBLANES - s)
                    for blk in range(SUBLANES // (2 * s) - 2, -1, -1):
                        m = jnp.where(sub_row < (blk + 1) * 2 * s, mid_row(r0 + blk * 2 * s + s), m)
                    upper = (sub_row & s) != 0
                    is_lhs = jnp.logical_not(upper) if reverse else upper
                    lhs.append(jnp.where(is_lhs, q[rows] * jnp.exp2(b[rows] - m), 0.0))
                    rhs.append(jnp.where(is_lhs, 0.0, k[rows] * jnp.exp2(m - b[rows])))
            lhs = jnp.concatenate(lhs, axis=0).astype(BF16)
            rhs = jnp.concatenate(rhs, axis=0).astype(BF16)
            scores = scores + mask_ref[l] * _dot_nt(lhs, rhs)
        sc_ref[c] = scores.astype(BF16)
        return carry

    lax.fori_loop(0, nchunks, chunk_scores, 0, unroll=GLA_UNROLL)

    def chunk(ci, carry):
        c = (nchunks - 1 - ci) if reverse else ci
        c0 = pl.multiple_of(c * C, C)
        q = q_ref[pl.ds(c0, C), :].astype(F32)
        k = k_ref[pl.ds(c0, C), :].astype(F32)
        v = _wide(v_ref, pl.ds(c0, C))
        b = b_ref[pl.ds(c0, C), :]
        b_end = b[0:1, :] if reverse else b[C - 1:C, :]

        st = st_ref[...]
        o = _dot_nt((q * jnp.exp2(b)).astype(BF16), st.astype(BF16))
        ke = (k * jnp.exp2(b_end - b)).astype(BF16)
        st_ref[...] = st * jnp.exp2(b_end) + _dot_tn(v, ke)
        o = o + _dot(sc_ref[c], v)
        if reverse:
            tot = of_ref[pl.ds(c0, C), :].astype(F32) + o
            o_ref[pl.ds(c0, C), :] = _finish_heads(tot, gn_ref[...], _wide(g_ref, pl.ds(c0, C)))
        else:
            o_ref[pl.ds(c0, C), :] = o.astype(BF16)
        return carry

    lax.fori_loop(0, nchunks, chunk, 0, unroll=GLA_UNROLL)


def _gla_parts(proj, bcum, rb, reverse, o_fwd=None, gn_w=None):
    T = proj.shape[1]
    tb = GLA_TB
    dk, dv = GLA_DK, GLA_DV
    masks = _gla_tables(reverse)

    def key_block(base):
        return pl.BlockSpec((None, tb, LANE), lambda b, h, n: (base // dk + h, rb(b, n), 0))

    def value_block(base):
        return pl.BlockSpec((dv // LANE, tb, LANE), lambda b, h, n: (base // dv + h, rb(b, n), 0))

    in_specs = [
        key_block(_GQ), key_block(_GK), value_block(_GV),
        pl.BlockSpec((tb, dk), lambda b, h, n: (rb(b, n), h)),
        pl.BlockSpec(masks.shape, lambda b, h, n: (0, 0, 0)),
    ]
    args = [proj, proj, proj, bcum, masks]
    if reverse:
        in_specs += [
            value_block(_GG),
            pl.BlockSpec((tb, dv), lambda b, h, n: (rb(b, n), h)),
            pl.BlockSpec((1, dv), lambda b, h, n: (0, h)),
        ]
        args += [proj, o_fwd, gn_w]
    out_spec = pl.BlockSpec((tb, dv), lambda b, h, n: (rb(b, n), h))
    out_shape = jax.ShapeDtypeStruct((T, GLA_WIDTH), BF16)
    scratch = [pltpu.VMEM((dv, dk), F32), pltpu.VMEM((tb // GLA_CHUNK, GLA_CHUNK, GLA_CHUNK), BF16)]
    return in_specs, args, out_spec, out_shape, scratch


def _mixer_kernel(*refs, reverse, n_ret_in, n_gla_in, n_ret_scratch):
    ret_in = refs[:n_ret_in]
    gla_in = refs[n_ret_in:n_ret_in + n_gla_in]
    ret_out, gla_out = refs[n_ret_in + n_gla_in:n_ret_in + n_gla_in + 2]
    scratch = refs[n_ret_in + n_gla_in + 2:]
    _ret_kernel(*ret_in, ret_out, *scratch[:n_ret_scratch], reverse=reverse)
    _gla_kernel(*gla_in, gla_out, *scratch[n_ret_scratch:], reverse=reverse)


def _mixer_scan(proj, decay_logit, bcum, batch, seq_len, reverse, o_fwd=(None, None), gn_w=(None, None)):
    assert RET_TB == GLA_TB and RET_HEADS == GLA_HEADS
    nb = seq_len // RET_TB

    def rb(b, n):
        return b * nb + ((nb - 1 - n) if reverse else n)

    r_specs, r_args, r_out, r_shape, r_scratch = _ret_parts(proj, decay_logit, rb, reverse, o_fwd[0], gn_w[0])
    g_specs, g_args, g_out, g_shape, g_scratch = _gla_parts(proj, bcum, rb, reverse, o_fwd[1], gn_w[1])
    return pl.pallas_call(
        functools.partial(_mixer_kernel, reverse=reverse, n_ret_in=len(r_specs), n_gla_in=len(g_specs),
                          n_ret_scratch=len(r_scratch)),
        grid=(batch, RET_HEADS, nb),
        in_specs=r_specs + g_specs,
        out_specs=[r_out, g_out],
        out_shape=[r_shape, g_shape],
        scratch_shapes=r_scratch + g_scratch,
        compiler_params=_cparams(("parallel", "parallel", "arbitrary"), 48),
        name="mixer_bwd" if reverse else "mixer_fwd",
    )(*r_args, *g_args)


OP_TM = 512
OP_SUB = 256


def _out_proj_kernel(mr_ref, mg_ref, w0_ref, w1_ref, x_ref, n2_ref, rh_ref, rl_ref, h_ref, xn_ref, aff_ref):
    for r in range(OP_TM // OP_SUB):
        rows = slice(r * OP_SUB, (r + 1) * OP_SUB)
        h = x_ref[rows, :] + _dot(mr_ref[rows, :], w0_ref[...]) + _dot(mg_ref[rows, :], w1_ref[...])
        h_ref[rows, :] = h
        ms = jnp.mean(h * h, axis=-1, keepdims=True)
        xn = h * lax.rsqrt(ms + EPS) * n2_ref[...]
        xh = xn.astype(BF16)
        xn_ref[rows, :] = _pack_bf16_pairs(xn)
        xl = (xn - xh.astype(F32)).astype(BF16)
        lt = _dot_nt(rh_ref[...], xh) + _dot_nt(rh_ref[...], xl) + _dot_nt(rl_ref[...], xh)
        m = jnp.max(lt, axis=0, keepdims=True)
        e = jnp.exp(lt - m)
        aff_ref[:, rows] = e / jnp.sum(e, axis=0, keepdims=True)


def _out_proj(mix_r, mix_g, w_out, x2d, n2w, r_hi, r_lo):
    T = x2d.shape[0]
    tm = OP_TM
    half = RET_WIDTH
    return pl.pallas_call(
        _out_proj_kernel,
        grid=(T // tm,),
        in_specs=[
            pl.BlockSpec((tm, half), lambda i: (i, 0)),
            pl.BlockSpec((tm, half), lambda i: (i, 0)),
            pl.BlockSpec((half, D_MODEL), lambda i: (0, 0)),
            pl.BlockSpec((half, D_MODEL), lambda i: (1, 0)),
            pl.BlockSpec((tm, D_MODEL), lambda i: (i, 0)),
            pl.BlockSpec((1, D_MODEL), lambda i: (0, 0)),
            pl.BlockSpec((N_EXPERTS, D_MODEL), lambda i: (0, 0)),
            pl.BlockSpec((N_EXPERTS, D_MODEL), lambda i: (0, 0)),
        ],
        out_specs=[
            pl.BlockSpec((tm, D_MODEL), lambda i: (i, 0)),
            pl.BlockSpec((tm, D_MODEL // 2), lambda i: (i, 0)),
            pl.BlockSpec((N_EXPERTS, tm), lambda i: (0, i)),
        ],
        out_shape=[
            jax.ShapeDtypeStruct((T, D_MODEL), F32),
            jax.ShapeDtypeStruct((T, D_MODEL // 2), I32),
            jax.ShapeDtypeStruct((N_EXPERTS, T), F32),
        ],
        compiler_params=_cparams(("parallel",), 56),
        name="out_proj",
    )(mix_r, mix_g, w_out, w_out, x2d, n2w, r_hi, r_lo)


def _select_kernel(a_ref, pos_ref, posb_ref, rankb_ref, affb_ref, *, cap):
    E, T = a_ref.shape
    tt = MOE_TT

    def count(pred):
        return jnp.sum(pred.astype(F32), axis=1, keepdims=True)

    def bisect(i, tau):
        cand = tau | jnp.left_shift(jnp.int32(1), 30 - i)
        bits = pltpu.bitcast(a_ref[...], I32)
        return jnp.where(count(bits >= cand) >= cap, cand, tau)

    tau = lax.fori_loop(0, 31, bisect, jnp.zeros((E, 1), I32))
    bits_all = pltpu.bitcast(a_ref[...], I32)
    quota = cap - count(bits_all > tau)

    before = (lax.broadcasted_iota(I32, (tt, tt), 0) < lax.broadcasted_iota(I32, (tt, tt), 1)).astype(BF16)
    below = (lax.broadcasted_iota(I32, (E, E), 1) < lax.broadcasted_iota(I32, (E, E), 0)).astype(BF16)

    def block(j, carry):
        c_eq, c_sel = carry
        off = pl.multiple_of(j * tt, tt)
        aff = a_ref[:, pl.ds(off, tt)]
        bits = pltpu.bitcast(aff, I32)
        eq = bits == tau
        eqf = eq.astype(F32)
        rank_eq = _dot(eqf.astype(BF16), before) + c_eq
        sel = (bits > tau) | (eq & (rank_eq < quota))
        self_ = sel.astype(F32)
        selb = self_.astype(BF16)
        slot = _dot(selb, before) + c_sel
        pos = jnp.where(sel, slot, -1.0).astype(I32)
        pos_ref[:, pl.ds(off, tt)] = pos
        per_tok = jnp.broadcast_to(jnp.sum(self_, axis=0, keepdims=True), (E, tt))
        rank = _dot(per_tok.astype(BF16), before) + _dot(below, selb)
        posb_ref[j] = pos
        rankb_ref[j] = jnp.where(sel, rank, -1.0).astype(I32)
        affb_ref[j] = aff
        return (c_eq + jnp.sum(eqf, axis=1, keepdims=True), c_sel + jnp.sum(self_, axis=1, keepdims=True))

    zero = jnp.zeros((E, 1), F32)
    lax.fori_loop(0, T // tt, block, (zero, zero))


def _select(aff, cap):
    E, T = aff.shape
    nb = T // MOE_TT
    blk = jax.ShapeDtypeStruct((nb, E, MOE_TT), I32)
    return pl.pallas_call(
        functools.partial(_select_kernel, cap=cap),
        out_shape=[jax.ShapeDtypeStruct((E, T), I32), blk, blk, jax.ShapeDtypeStruct((nb, E, MOE_TT), F32)],
        compiler_params=pltpu.CompilerParams(vmem_limit_bytes=40 * 1024 * 1024),
        name="select",
    )(aff)


def _regroup_rows(T):
    nb = T // MOE_TT
    rows = CAPACITY_FACTOR * T + SC_GATHER_ROWS * nb
    return -(-rows // MOE_TW) * MOE_TW


def _combine_schedule(posb, T):
    nb = posb.shape[0]
    tw, g = MOE_TW, SC_GATHER_ROWS
    n = jnp.sum((posb >= 0).reshape(nb, -1), axis=1).astype(I32)
    seg = (n + g - 1) // g * g
    hi = jnp.cumsum(seg)
    lo = hi - seg
    off = jnp.concatenate([jnp.zeros((1,), I32), hi])
    nwin_max = (N_EXPERTS * MOE_TT) // tw + 1
    w0 = lo // tw
    w1 = jnp.where(seg > 0, (hi - 1) // tw, w0)
    cand = jnp.arange(nwin_max, dtype=I32)
    win = w0[:, None] + cand[None, :]
    valid = (win <= w1[:, None]).reshape(-1)
    nwin_total = _regroup_rows(T) // tw
    pmax = nb + nwin_total
    jv = jnp.broadcast_to(jnp.arange(nb, dtype=I32)[:, None], win.shape).reshape(-1)
    wv = jnp.minimum(win, nwin_total - 1).reshape(-1)
    total = jnp.sum(valid.astype(I32))
    dst = jnp.where(valid, jnp.cumsum(valid.astype(I32)) - 1, pmax)
    pj, pw = (jnp.zeros((pmax,), I32).at[dst].set(a, mode="drop") for a in (jv, wv))
    real = jnp.arange(pmax, dtype=I32) < total
    pj, pw = (jnp.where(real, a, a[total - 1]) for a in (pj, pw))
    first = jnp.concatenate([jnp.ones((1,), bool), pj[1:] != pj[:-1]])
    last = jnp.concatenate([pj[1:] != pj[:-1], jnp.ones((1,), bool)]) | (jnp.arange(pmax, dtype=I32) == total - 1)
    flag = jnp.where(real, first.astype(I32) + 2 * last.astype(I32) + 4, 0)
    return off, (pj, pw, flag, lo[pj], hi[pj])


SC_LANES = 16
SC_CORES = 2
SC_SUBCORES = 16
SC_GATHER_ROWS = 32


def _dispatch(pos, xw, cap):
    E, T = pos.shape
    W = xw.shape[1]
    G = SC_GATHER_ROWS
    part_rows = cap // SC_CORES
    mesh = plsc.VectorSubcoreMesh(core_axis_name="c", subcore_axis_name="s")

    @pl.kernel(
        out_type=jax.ShapeDtypeStruct((E * cap, W), I32),
        mesh=mesh,
        scratch_types=[pltpu.VMEM((T,), I32), pltpu.VMEM((cap,), I32), pltpu.VMEM((G, W), I32)],
        compiler_params=pltpu.CompilerParams(needs_layout_passes=False),
        name="sc_dispatch",
    )
    def run(pos_hbm, x_hbm, xe_hbm, pos_v, idx_v, buf):
        e = lax.axis_index("s")
        part = lax.axis_index("c")
        pltpu.sync_copy(pos_hbm.at[e], pos_v)
        lane = lax.iota(I32, SC_LANES)

        @pl.loop(0, T // SC_LANES)
        def _(i):
            off = pl.multiple_of(i * SC_LANES, SC_LANES)
            p = pos_v[pl.ds(off, SC_LANES)]
            plsc.store_scatter(idx_v, [p], lane + off, mask=p >= 0)

        @pl.loop(0, part_rows // G)
        def _(g):
            o = pl.multiple_of(part * part_rows + g * G, G)
            pltpu.sync_copy(x_hbm.at[idx_v.at[pl.ds(o, G)]], buf)
            pltpu.sync_copy(buf, xe_hbm.at[pl.ds(e * cap + o, G)])

    return run(pos, xw)


FFN_TM = 2048
FFN_UNPACK_ROWS = 256


def _ffn_tile_width(cap):
    return 512 if min(FFN_TM, cap) <= 1024 else 256


def _ffn_kernel(x_ref, wg_ref, wu_ref, wd_ref, o_ref, xb_ref, hid_ref, *, tw):
    s = pl.program_id(2)
    n_up = D_FF // tw

    @pl.when(s == 0)
    def _():
        half = D_MODEL // 2

        def unpack_rows(i, carry):
            r = pl.multiple_of(i * FFN_UNPACK_ROWS, FFN_UNPACK_ROWS)
            lo, hi = _unpack_bf16_pairs(x_ref[pl.ds(r, FFN_UNPACK_ROWS), :])
            xb_ref[pl.ds(r, FFN_UNPACK_ROWS), :half] = lo.astype(BF16)
            xb_ref[pl.ds(r, FFN_UNPACK_ROWS), half:] = hi.astype(BF16)
            return carry

        lax.fori_loop(0, x_ref.shape[0] // FFN_UNPACK_ROWS, unpack_rows, 0)

    @pl.when(s < n_up)
    def _():
        x = xb_ref[...]
        g = _dot(x, wg_ref[...].astype(BF16))
        u = _dot(x, wu_ref[...].astype(BF16))
        col = pl.multiple_of(s * tw, tw)
        hid_ref[:, pl.ds(col, tw)] = (_silu(g) * u).astype(BF16)

    @pl.when(s >= n_up)
    def _():
        o_ref[...] = _pack_bf16_pairs(_dot(hid_ref[...], wd_ref[...].astype(BF16)))


def _ffn(xe, w_gate, w_up, w_down):
    E, cap, _ = xe.shape
    tm = min(FFN_TM, cap)
    tw = _ffn_tile_width(cap)
    n_up, n_down = D_FF // tw, D_MODEL // tw

    def up(e, m, s):
        return (e, 0, jnp.minimum(s, n_up - 1))

    def down(s):
        return jnp.maximum(s - n_up, 0)

    return pl.pallas_call(
        functools.partial(_ffn_kernel, tw=tw),
        grid=(E, cap // tm, n_up + n_down),
        in_specs=[
            pl.BlockSpec((None, tm, D_MODEL // 2), lambda e, m, s: (e, m, 0)),
            pl.BlockSpec((None, D_MODEL, tw), up),
            pl.BlockSpec((None, D_MODEL, tw), up),
            pl.BlockSpec((None, D_FF, tw), lambda e, m, s: (e, 0, down(s))),
        ],
        out_specs=pl.BlockSpec((None, tm, tw // 2), lambda e, m, s: (e, m, down(s))),
        out_shape=jax.ShapeDtypeStruct((E, cap, D_MODEL // 2), I32),
        scratch_shapes=[pltpu.VMEM((tm, D_MODEL), BF16), pltpu.VMEM((tm, D_FF), BF16)],
        compiler_params=_cparams(("parallel", "parallel", "arbitrary"), 60),
        name="ffn",
    )(xe, w_gate, w_up, w_down)


def _regroup(posb, rankb, affb, off, yw, cap, rows):
    NB, EB = posb.shape
    W = yw.shape[1]
    G, L, tt = SC_GATHER_ROWS, SC_LANES, MOE_TT
    n_workers = SC_CORES * SC_SUBCORES
    per = NB // n_workers
    mesh = plsc.VectorSubcoreMesh(core_axis_name="c", subcore_axis_name="s")
    off_pad = jnp.pad(off, (0, L))

    @pl.kernel(
        out_type=(jax.ShapeDtypeStruct((rows, W), I32), jax.ShapeDtypeStruct((rows,), I32),
                  jax.ShapeDtypeStruct((rows,), F32)),
        mesh=mesh,
        scratch_types=[pltpu.VMEM((EB,), I32), pltpu.VMEM((EB,), I32), pltpu.VMEM((EB,), F32),
                       pltpu.VMEM((EB,), I32), pltpu.VMEM((EB,), I32), pltpu.VMEM((EB,), F32),
                       pltpu.VMEM((G, W), I32), pltpu.VMEM((NB + 1 + L,), I32)],
        compiler_params=pltpu.CompilerParams(needs_layout_passes=False),
        name="sc_regroup",
    )
    def run(posb_hbm, rankb_hbm, affb_hbm, off_hbm, y_hbm, yg_hbm, tok_hbm, gate_hbm,
            pos_v, rank_v, aff_v, src_v, tok_v, gate_v, buf, off_v):
        wid = lax.axis_index("c") * SC_SUBCORES + lax.axis_index("s")
        pltpu.sync_copy(off_hbm, off_v)
        lane = lax.iota(I32, L)
        zi = jnp.zeros((L,), I32)
        zf = jnp.zeros((L,), F32)

        @pl.loop(0, per)
        def _(k):
            j = wid * per + k
            pltpu.sync_copy(posb_hbm.at[j], pos_v)
            pltpu.sync_copy(rankb_hbm.at[j], rank_v)
            pltpu.sync_copy(affb_hbm.at[j], aff_v)
            lo = jnp.max(plsc.load_gather(off_v, [zi + j]))
            hi = jnp.max(plsc.load_gather(off_v, [zi + j + 1]))

            @pl.loop(0, EB // L)
            def _(i):
                o = pl.multiple_of(i * L, L)
                src_v[pl.ds(o, L)] = zi
                tok_v[pl.ds(o, L)] = zi
                gate_v[pl.ds(o, L)] = zf

            @pl.loop(0, EB // L)
            def _(i):
                o = pl.multiple_of(i * L, L)
                p = pos_v[pl.ds(o, L)]
                r = rank_v[pl.ds(o, L)]
                m = p >= 0
                e = i // (tt // L)
                t0 = j * tt + (i % (tt // L)) * L
                plsc.store_scatter(src_v, [r], p + e * cap, mask=m)
                plsc.store_scatter(tok_v, [r], lane + t0, mask=m)
                plsc.store_scatter(gate_v, [r], aff_v[pl.ds(o, L)], mask=m)

            @pl.loop(0, (hi - lo) // G)
            def _(g):
                o = pl.multiple_of(g * G, G)
                dst = pl.multiple_of(lo + o, G)
                pltpu.sync_copy(y_hbm.at[src_v.at[pl.ds(o, G)]], buf)
                pltpu.sync_copy(buf, yg_hbm.at[pl.ds(dst, G)])
                pltpu.sync_copy(tok_v.at[pl.ds(o, G)], tok_hbm.at[pl.ds(dst, G)])
                pltpu.sync_copy(gate_v.at[pl.ds(o, G)], gate_hbm.at[pl.ds(dst, G)])

    return run(posb, rankb, affb, off_pad, yw)


def _combine_kernel(pj_ref, pw_ref, pf_ref, plo_ref, phi_ref, tok_ref, gate_ref, yg_ref, h_ref, nf_ref,
                    o_ref, acc_ref, *, group):
    p = pl.program_id(0)
    flag = pf_ref[p]
    tw, tt = MOE_TW, MOE_TT
    half = D_MODEL // 2

    hw = group // 2

    def col_blocks():
        for n in range(D_MODEL // group):
            yield slice(n * hw, (n + 1) * hw), slice(n * group, n * group + hw)
            yield slice(half + n * hw, half + (n + 1) * hw), slice(n * group + hw, (n + 1) * group)

    @pl.when((flag & 1) != 0)
    def _():
        for packed, natural in col_blocks():
            acc_ref[:, packed] = h_ref[:, natural]

    @pl.when((flag & 4) != 0)
    def _():
        lo, hi = plo_ref[p], phi_ref[p]
        row0 = pw_ref[p] * tw
        rid = lax.broadcasted_iota(I32, (tw, 1), 0) + row0
        keep = (rid >= lo) & (rid < hi)
        y_lo, y_hi = _unpack_bf16_pairs(yg_ref[...])
        y_lo = jnp.where(keep, y_lo, 0.0).astype(BF16)
        y_hi = jnp.where(keep, y_hi, 0.0).astype(BF16)
        tid = lax.broadcasted_iota(I32, (tt, tw), 0) + pj_ref[p] * tt
        cid = lax.broadcasted_iota(I32, (tt, tw), 1) + row0
        hit = (tok_ref[...] == tid) & (cid >= lo) & (cid < hi)
        weights = jnp.where(hit, gate_ref[...], 0.0).astype(BF16)
        acc_ref[:, :half] += _dot(weights, y_lo)
        acc_ref[:, half:] += _dot(weights, y_hi)

    @pl.when((flag & 2) != 0)
    def _():
        y = acc_ref[...]
        scale = lax.rsqrt(jnp.mean(y * y, axis=-1, keepdims=True) + EPS)
        for packed, natural in col_blocks():
            o_ref[:, natural] = acc_ref[:, packed] * scale * nf_ref[:, natural]


def _combine(lists, tok, gate, yg, h, nfw, group):
    pj, pw, pf, plo, phi = lists
    T = h.shape[0]
    tw, tt = MOE_TW, MOE_TT
    nwin = yg.shape[0] // tw
    grid_spec = pltpu.PrefetchScalarGridSpec(
        num_scalar_prefetch=5,
        grid=(pj.shape[0],),
        in_specs=[
            pl.BlockSpec((None, 1, tw), lambda p, pj, pw, *_: (pw[p], 0, 0)),
            pl.BlockSpec((None, 1, tw), lambda p, pj, pw, *_: (pw[p], 0, 0)),
            pl.BlockSpec((tw, D_MODEL // 2), lambda p, pj, pw, *_: (pw[p], 0)),
            pl.BlockSpec((tt, D_MODEL), lambda p, pj, pw, *_: (pj[p], 0)),
            pl.BlockSpec((1, D_MODEL), lambda p, pj, pw, *_: (0, 0)),
        ],
        out_specs=pl.BlockSpec((tt, D_MODEL), lambda p, pj, pw, *_: (pj[p], 0)),
        scratch_shapes=[pltpu.VMEM((tt, D_MODEL), F32)],
    )
    return pl.pallas_call(
        functools.partial(_combine_kernel, group=group),
        grid_spec=grid_spec,
        out_shape=jax.ShapeDtypeStruct((T, D_MODEL), F32),
        compiler_params=_cparams(("arbitrary",), 32),
        name="combine",
    )(pj, pw, pf, plo, phi, tok.reshape(nwin, 1, tw), gate.reshape(nwin, 1, tw), yg, h, nfw)


def _rope_tables(seq_len):
    d = RET_DK
    inv = ROPE_BASE ** (-jnp.arange(0, d, 2, dtype=F32) / d)
    ang = jnp.arange(seq_len, dtype=F32)[:, None] * inv[None, :]
    return jnp.cos(ang), jnp.sin(ang)


def _chunk_tri(n, chunk, upper):
    r = np.arange(n)
    same = (r[:, None] // chunk) == (r[None, :] // chunk)
    tri = (r[:, None] <= r[None, :]) if upper else (r[:, None] >= r[None, :])
    return jnp.asarray(same & tri, BF16)


def _prep_params(norm1_w, w_in, ret_gn_w, gla_gate_up, gla_gate_bias, gla_gn_w, w_out, norm2_w, router_w,
                 normf_w):
    w = w_in[0]
    w_main = w[:, :IN_MAIN].astype(BF16)
    w_ga = jnp.pad(w[:, IN_MAIN:], ((0, 0), (0, LANE - 2 * GLA_RANK))).astype(BF16)
    cs = np.ones((1, IN_MAIN), np.float32)
    cs[:, _RQ:_RQ + RET_WIDTH] = RET_DK ** -0.5
    cs[:, _GQ:_GQ + GLA_KEY_WIDTH] = GLA_DK ** -0.5
    up = gla_gate_up[0].astype(F32)
    up_pad = jnp.zeros((LANE, 2 * GLA_KEY_WIDTH), F32)
    up_pad = up_pad.at[:GLA_RANK, :GLA_KEY_WIDTH].set(up[0])
    up_pad = up_pad.at[GLA_RANK:2 * GLA_RANK, GLA_KEY_WIDTH:].set(up[1])
    rt = router_w[0].T.astype(F32)
    r_hi = rt.astype(BF16)
    r_lo = (rt - r_hi.astype(F32)).astype(BF16)
    return dict(
        n1w=norm1_w[0].reshape(1, D_MODEL).astype(F32),
        w_main=w_main, w_ga=w_ga, colscale=jnp.asarray(cs),
        up_pad=up_pad.astype(BF16),
        bias=gla_gate_bias[0].reshape(1, 2 * GLA_KEY_WIDTH).astype(F32),
        lf=_chunk_tri(GATE_TM, GLA_CHUNK, upper=False),
        lb=_chunk_tri(GATE_TM, GLA_CHUNK, upper=True),
        ret_gn=ret_gn_w[0].reshape(1, RET_WIDTH).astype(F32),
        gla_gn=gla_gn_w[0].reshape(1, GLA_WIDTH).astype(F32),
        w_out=w_out[0].astype(BF16),
        n2w=norm2_w[0].reshape(1, D_MODEL).astype(F32),
        r_hi=r_hi, r_lo=r_lo,
        nfw=normf_w.reshape(1, D_MODEL).astype(F32),
    )


def _trunk(x, pp, decay_logit, w_gate, w_up, w_down):
    B, L, _ = x.shape
    T = B * L
    x2d = x.reshape(T, D_MODEL)
    cos, sin = _rope_tables(L)
    proj, ga = _in_proj(x2d, pp["n1w"], pp["w_main"], pp["w_ga"], pp["colscale"], cos, sin, L)
    b_f, b_b = _gla_gates(ga, pp["up_pad"], pp["bias"], pp["lf"], pp["lb"])

    fwd = _mixer_scan(proj, decay_logit, b_f, B, L, reverse=False)
    mix_r, mix_g = _mixer_scan(proj, decay_logit, b_b, B, L, reverse=True, o_fwd=fwd,
                               gn_w=(pp["ret_gn"], pp["gla_gn"]))

    h, xn2, aff = _out_proj(mix_r, mix_g, pp["w_out"], x2d, pp["n2w"], pp["r_hi"], pp["r_lo"])

    cap = CAPACITY_FACTOR * T // N_EXPERTS
    pos, posb, rankb, affb = _select(aff, cap)
    off, c_lists = _combine_schedule(posb, T)
    xe = _dispatch(pos, xn2, cap).reshape(N_EXPERTS, cap, D_MODEL // 2)
    ye = _ffn(xe, w_gate, w_up, w_down).reshape(N_EXPERTS * cap, D_MODEL // 2)
    nb = T // MOE_TT
    yg, tok, gate = _regroup(posb.reshape(nb, -1), rankb.reshape(nb, -1), affb.reshape(nb, -1), off, ye, cap,
                             _regroup_rows(T))
    y = _combine(c_lists, tok, gate, yg, h, pp["nfw"], _ffn_tile_width(cap))
    return y.reshape(B, L, D_MODEL)


def kernel(x_prompt, x_sample, norm1_w, w_in, ret_decay_logit, ret_gn_w, gla_gate_up, gla_gate_bias,
           gla_gn_w, w_out, norm2_w, router_w, w_gate, w_up, w_down, normf_w):
    pp = _prep_params(norm1_w, w_in, ret_gn_w, gla_gate_up, gla_gate_bias, gla_gn_w, w_out, norm2_w,
                      router_w, normf_w)
    decay_logit = ret_decay_logit[0].astype(F32)
    args = (pp, decay_logit, w_gate[0], w_up[0], w_down[0])
    return (_trunk(x_prompt, *args), _trunk(x_sample, *args))
```

```python
import functools

import numpy as np
import jax
import jax.numpy as jnp
from jax import lax
from jax.experimental import pallas as pl
from jax.experimental.pallas import tpu as pltpu
from jax.experimental.pallas import tpu_sc as plsc

F32, BF16, I32 = jnp.float32, jnp.bfloat16, jnp.int32

D_MODEL = 2048
RET_WIDTH = 1024
RET_HEADS = 4
RET_DK = 256
RET_DV = 256
GLA_WIDTH = 1024
GLA_HEADS = 4
GLA_DK = 128
GLA_DV = 256
GLA_KEY_WIDTH = 512
GLA_RANK = 16
GLA_TAU = 16.0
RET_CHUNK = 256
GLA_CHUNK = 64
GLA_SUB = 16
ROPE_BASE = 10000.0
N_EXPERTS = 16
CAPACITY_FACTOR = 2
D_FF = 2048
EPS = 1e-6
LOG2_E = 1.4426950408889634
IN_MAIN = 4 * RET_WIDTH + 2 * GLA_KEY_WIDTH + 2 * GLA_WIDTH

_RQ, _RK, _RV, _RG = 0, 1024, 2048, 3072
_GQ, _GK, _GV, _GG = 4096, 4608, 5120, 6144

LANE = 128
MOE_TT = 256
MOE_TW = 256
V7X_VMEM_BYTES = 64 * 1024 * 1024


def _cparams(sem, vmem_mb):
    return pltpu.CompilerParams(dimension_semantics=sem, vmem_limit_bytes=vmem_mb * 1024 * 1024)


def _log_sigmoid(z):
    return jnp.minimum(z, 0.0) - jnp.log1p(jnp.exp(-jnp.abs(z)))


def _silu(g):
    return g * (1.0 / (1.0 + jnp.exp(-g)))


def _dot_nt(a, b):
    return lax.dot_general(a, b, (((1,), (1,)), ((), ())), preferred_element_type=F32)


def _dot_tn(a, b):
    return lax.dot_general(a, b, (((0,), (0,)), ((), ())), preferred_element_type=F32)


def _dot(a, b):
    return jnp.dot(a, b, preferred_element_type=F32)


def _pack_bf16_pairs(x):
    bits = pltpu.bitcast(x.astype(BF16).astype(F32), I32)
    w = x.shape[1] // 2
    return bits[:, w:] | lax.shift_right_logical(bits[:, :w], 16)


def _unpack_bf16_pairs(words):
    lo = pltpu.bitcast(lax.shift_left(words, 16), F32)
    hi = pltpu.bitcast(words & jnp.int32(-65536), F32)
    return lo, hi


IP_TM = 1024
IP_TN = 1024


def _in_proj_kernel(x_ref, n1_ref, w_ref, wga_ref, cs_ref, cos_ref, sin_ref, o_ref, ga_ref, xn_ref):
    j = pl.program_id(1)

    @pl.when(j == 0)
    def _():
        x = x_ref[...]
        ms = jnp.mean(x * x, axis=-1, keepdims=True)
        xn = (x * lax.rsqrt(ms + EPS) * n1_ref[...]).astype(BF16)
        xn_ref[...] = xn
        ga_ref[...] = _dot(xn, wga_ref[...])

    acc = _dot(xn_ref[...], w_ref[...]) * cs_ref[...]
    n_rope_blocks = 2 * RET_WIDTH // IP_TN

    @pl.when(j < n_rope_blocks)
    def _():
        cos = cos_ref[...]
        sin = sin_ref[...]
        for h in range(IP_TN // RET_DK):
            x1 = acc[:, 2 * h * LANE:(2 * h + 1) * LANE]
            x2 = acc[:, (2 * h + 1) * LANE:(2 * h + 2) * LANE]
            o_ref[2 * h] = (x1 * cos - x2 * sin).astype(BF16)
            o_ref[2 * h + 1] = (x1 * sin + x2 * cos).astype(BF16)

    @pl.when(j >= n_rope_blocks)
    def _():
        for c in range(IP_TN // LANE):
            o_ref[c] = acc[:, c * LANE:(c + 1) * LANE].astype(BF16)


def _in_proj(x2d, n1w, w_main, w_ga, colscale, cos, sin, seq_len):
    T = x2d.shape[0]
    tm, tn = IP_TM, IP_TN
    nlb = seq_len // tm
    return pl.pallas_call(
        _in_proj_kernel,
        grid=(T // tm, IN_MAIN // tn),
        in_specs=[
            pl.BlockSpec((tm, D_MODEL), lambda i, j: (i, 0)),
            pl.BlockSpec((1, D_MODEL), lambda i, j: (0, 0)),
            pl.BlockSpec((D_MODEL, tn), lambda i, j: (0, j)),
            pl.BlockSpec((D_MODEL, LANE), lambda i, j: (0, 0)),
            pl.BlockSpec((1, tn), lambda i, j: (0, j)),
            pl.BlockSpec((tm, LANE), lambda i, j: (i % nlb, 0)),
            pl.BlockSpec((tm, LANE), lambda i, j: (i % nlb, 0)),
        ],
        out_specs=[
            pl.BlockSpec((tn // LANE, tm, LANE), lambda i, j: (j, i, 0)),
            pl.BlockSpec((tm, LANE), lambda i, j: (i, 0)),
        ],
        out_shape=[
            jax.ShapeDtypeStruct((IN_MAIN // LANE, T, LANE), BF16),
            jax.ShapeDtypeStruct((T, LANE), F32),
        ],
        scratch_shapes=[pltpu.VMEM((tm, D_MODEL), BF16)],
        compiler_params=_cparams(("parallel", "arbitrary"), 48),
        name="in_proj",
    )(x2d, n1w, w_main, w_ga, colscale, cos, sin)


GATE_TM = 512


def _gates_kernel(ga_ref, up_ref, bias_ref, lf_ref, lb_ref, bf_ref, bb_ref):
    z = _dot(ga_ref[...].astype(BF16), up_ref[...]) + bias_ref[...]
    la = _log_sigmoid(z) * (LOG2_E / GLA_TAU)
    hi = la.astype(BF16)
    lo = (la - hi.astype(F32)).astype(BF16)
    kw = GLA_KEY_WIDTH
    bf_ref[...] = _dot(lf_ref[...], hi[:, :kw]) + _dot(lf_ref[...], lo[:, :kw])
    bb_ref[...] = _dot(lb_ref[...], hi[:, kw:]) + _dot(lb_ref[...], lo[:, kw:])


def _gla_gates(ga, up_pad, bias, lf, lb):
    T = ga.shape[0]
    tm = GATE_TM
    kw = GLA_KEY_WIDTH
    return pl.pallas_call(
        _gates_kernel,
        grid=(T // tm,),
        in_specs=[
            pl.BlockSpec((tm, LANE), lambda i: (i, 0)),
            pl.BlockSpec((LANE, 2 * kw), lambda i: (0, 0)),
            pl.BlockSpec((1, 2 * kw), lambda i: (0, 0)),
            pl.BlockSpec((tm, tm), lambda i: (0, 0)),
            pl.BlockSpec((tm, tm), lambda i: (0, 0)),
        ],
        out_specs=[pl.BlockSpec((tm, kw), lambda i: (i, 0)), pl.BlockSpec((tm, kw), lambda i: (i, 0))],
        out_shape=[jax.ShapeDtypeStruct((T, kw), F32), jax.ShapeDtypeStruct((T, kw), F32)],
        compiler_params=_cparams(("parallel",), 32),
        name="gla_gates",
    )(ga, up_pad, bias, lf, lb)


def _wide(ref, rows):
    return jnp.concatenate([ref[0, rows, :], ref[1, rows, :]], axis=1)


def _finish_heads(tot, gn, gate):
    ms = jnp.mean(tot * tot, axis=-1, keepdims=True)
    yn = tot * lax.rsqrt(ms + EPS) * gn
    return (yn * _silu(gate.astype(F32))).astype(BF16)


RET_TB = 1024


def _ret_kernel(dl_ref, q_ref, k_ref, v_ref, *rest, reverse):
    if reverse:
        g_ref, of_ref, gn_ref, o_ref, s_ref, intra_ref, qd_ref, kd_ref, cd_ref, p_ref, u_ref = rest
    else:
        o_ref, s_ref, intra_ref, qd_ref, kd_ref, cd_ref, p_ref, u_ref = rest
    h = pl.program_id(1)
    n = pl.program_id(2)
    C = RET_CHUNK

    @pl.when(n == 0)
    def _():
        s_ref[...] = jnp.zeros_like(s_ref)
        logit = dl_ref[1 if reverse else 0, h]
        lg = _log_sigmoid(jnp.full((C, RET_DV), logit, F32))
        lg_c = _log_sigmoid(jnp.full((C, C), logit, F32))
        lg_r = _log_sigmoid(jnp.full((1, RET_DV), logit, F32))
        ri = lax.broadcasted_iota(I32, (C, RET_DV), 0).astype(F32)
        rc = lax.broadcasted_iota(I32, (C, C), 0).astype(F32)
        cc = lax.broadcasted_iota(I32, (C, C), 1).astype(F32)
        diff = (cc - rc) if reverse else (rc - cc)
        intra_ref[...] = jnp.where(diff >= 0, jnp.exp(lg_c * diff), 0.0)
        if reverse:
            qd_ref[...] = jnp.exp(lg * (C - ri))
            kd_ref[...] = jnp.exp(lg * ri)
        else:
            qd_ref[...] = jnp.exp(lg * (ri + 1.0))
            kd_ref[...] = jnp.exp(lg * (C - 1.0 - ri))
        cd_ref[...] = jnp.exp(lg_r * C)

    nchunks = o_ref.shape[0] // C
    for c in range(nchunks):
        rows = slice(c * C, (c + 1) * C)
        k = _wide(k_ref, rows)
        p_ref[c] = (_dot_nt(_wide(q_ref, rows), k) * intra_ref[...]).astype(BF16)
        kd = (k.astype(F32) * kd_ref[...]).astype(BF16)
        u_ref[c] = _dot_tn(kd, _wide(v_ref, rows))
    order = range(nchunks - 1, -1, -1) if reverse else range(nchunks)
    for c in order:
        rows = slice(c * C, (c + 1) * C)
        state = s_ref[...]
        o = _dot(p_ref[c], _wide(v_ref, rows)) + _dot(_wide(q_ref, rows), state.astype(BF16)) * qd_ref[...]
        s_ref[...] = state * cd_ref[...] + u_ref[c]
        if reverse:
            tot = of_ref[rows, :].astype(F32) + o
            o_ref[rows, :] = _finish_heads(tot, gn_ref[...], _wide(g_ref, rows))
        else:
            o_ref[rows, :] = o.astype(BF16)


def _ret_parts(proj, decay_logit, rb, reverse, o_fwd=None, gn_w=None):
    T = proj.shape[1]
    tb = RET_TB
    dk, dv, C = RET_DK, RET_DV, RET_CHUNK

    def head(base):
        return pl.BlockSpec((dk // LANE, tb, LANE), lambda b, h, n: (base // dk + h, rb(b, n), 0))

    in_specs = [pl.BlockSpec(memory_space=pltpu.SMEM), head(_RQ), head(_RK), head(_RV)]
    args = [decay_logit, proj, proj, proj]
    if reverse:
        in_specs += [
            head(_RG),
            pl.BlockSpec((tb, dv), lambda b, h, n: (rb(b, n), h)),
            pl.BlockSpec((1, dv), lambda b, h, n: (0, h)),
        ]
        args += [proj, o_fwd, gn_w]
    out_spec = pl.BlockSpec((tb, dv), lambda b, h, n: (rb(b, n), h))
    out_shape = jax.ShapeDtypeStruct((T, RET_WIDTH), BF16)
    scratch = [
        pltpu.VMEM((dk, dv), F32),
        pltpu.VMEM((C, C), F32),
        pltpu.VMEM((C, dv), F32),
        pltpu.VMEM((C, dk), F32),
        pltpu.VMEM((1, dv), F32),
        pltpu.VMEM((tb // C, C, C), BF16),
        pltpu.VMEM((tb // C, dk, dv), F32),
    ]
    return in_specs, args, out_spec, out_shape, scratch


GLA_TB = 1024
GLA_UNROLL = 16


GLA_LEVELS = (32, 16, 8, 4, 2, 1)
SUBLANES = 8


def _gla_tables(reverse):
    C = GLA_CHUNK
    r = np.arange(C)
    masks = np.zeros((len(GLA_LEVELS) + 1, C, C), np.float32)
    for l, s in enumerate(GLA_LEVELS):
        upper = (r & s) != 0
        same = (r[:, None] // (2 * s)) == (r[None, :] // (2 * s))
        lhs_rows = ~upper if reverse else upper
        masks[l] = same & lhs_rows[:, None] & ~lhs_rows[None, :]
    masks[-1] = np.eye(C)
    return jnp.asarray(masks, F32)


def _gla_kernel(q_ref, k_ref, v_ref, b_ref, mask_ref, *rest, reverse):
    if reverse:
        g_ref, of_ref, gn_ref, o_ref, st_ref, sc_ref = rest
    else:
        o_ref, st_ref, sc_ref = rest
    n = pl.program_id(2)
    C = GLA_CHUNK

    @pl.when(n == 0)
    def _():
        st_ref[...] = jnp.zeros_like(st_ref)

    nchunks = q_ref.shape[0] // C
    sub_row = lax.broadcasted_iota(I32, (SUBLANES, GLA_DK), 0)
    zero_rows = jnp.zeros((SUBLANES, GLA_DK), F32)

    def chunk_scores(c, carry):
        c0 = pl.multiple_of(c * C, C)
        qb = q_ref[pl.ds(c0, C), :]
        kb = k_ref[pl.ds(c0, C), :]
        q = qb.astype(F32)
        k = kb.astype(F32)
        b = b_ref[pl.ds(c0, C), :]

        def mid_row(r):
            return jnp.broadcast_to(b[r:r + 1, :], (SUBLANES, GLA_DK))

        scores = mask_ref[len(GLA_LEVELS)] * _dot_nt(qb, kb)
        for l, s in enumerate(GLA_LEVELS):
            lhs, rhs = [], []
            for g in range(C // SUBLANES):
                r0 = g * SUBLANES
                rows = slice(r0, r0 + SUBLANES)
                if s >= SUBLANES:
                    m = mid_row((r0 // (2 * s)) * (2 * s) + s)
                    is_lhs = ((r0 & s) != 0) != reverse
                    if is_lhs:
                        lhs.append(q[rows] * jnp.exp2(b[rows] - m))
                        rhs.append(zero_rows)
                    else:
                        lhs.append(zero_rows)
                        rhs.append(k[rows] * jnp.exp2(m - b[rows]))
                else:
                    m = mid_row(r0 + SUBLANES - s)
                    for blk in range(SUBLANES // (2 * s) - 2, -1, -1):
                        m = jnp.where(sub_row < (blk + 1) * 2 * s, mid_row(r0 + blk * 2 * s + s), m)
                    upper = (sub_row & s) != 0
                    is_lhs = jnp.logical_not(upper) if reverse else upper
                    lhs.append(jnp.where(is_lhs, q[rows] * jnp.exp2(b[rows] - m), 0.0))
                    rhs.append(jnp.where(is_lhs, 0.0, k[rows] * jnp.exp2(m - b[rows])))
            lhs = jnp.concatenate(lhs, axis=0).astype(BF16)
            rhs = jnp.concatenate(rhs, axis=0).astype(BF16)
            scores = scores + mask_ref[l] * _dot_nt(lhs, rhs)
        sc_ref[c] = scores.astype(BF16)
        return carry

    lax.fori_loop(0, nchunks, chunk_scores, 0, unroll=GLA_UNROLL)

    def chunk(ci, carry):
        c = (nchunks - 1 - ci) if reverse else ci
        c0 = pl.multiple_of(c * C, C)
        q = q_ref[pl.ds(c0, C), :].astype(F32)
        k = k_ref[pl.ds(c0, C), :].astype(F32)
        v = _wide(v_ref, pl.ds(c0, C))
        b = b_ref[pl.ds(c0, C), :]
        b_end = b[0:1, :] if reverse else b[C - 1:C, :]

        st = st_ref[...]
        o = _dot_nt((q * jnp.exp2(b)).astype(BF16), st.astype(BF16))
        ke = (k * jnp.exp2(b_end - b)).astype(BF16)
        st_ref[...] = st * jnp.exp2(b_end) + _dot_tn(v, ke)
        o = o + _dot(sc_ref[c], v)
        if reverse:
            tot = of_ref[pl.ds(c0, C), :].astype(F32) + o
            o_ref[pl.ds(c0, C), :] = _finish_heads(tot, gn_ref[...], _wide(g_ref, pl.ds(c0, C)))
        else:
            o_ref[pl.ds(c0, C), :] = o.astype(BF16)
        return carry

    lax.fori_loop(0, nchunks, chunk, 0, unroll=GLA_UNROLL)


def _gla_parts(proj, bcum, rb, reverse, o_fwd=None, gn_w=None):
    T = proj.shape[1]
    tb = GLA_TB
    dk, dv = GLA_DK, GLA_DV
    masks = _gla_tables(reverse)

    def key_block(base):
        return pl.BlockSpec((None, tb, LANE), lambda b, h, n: (base // dk + h, rb(b, n), 0))

    def value_block(base):
        return pl.BlockSpec((dv // LANE, tb, LANE), lambda b, h, n: (base // dv + h, rb(b, n), 0))

    in_specs = [
        key_block(_GQ), key_block(_GK), value_block(_GV),
        pl.BlockSpec((tb, dk), lambda b, h, n: (rb(b, n), h)),
        pl.BlockSpec(masks.shape, lambda b, h, n: (0, 0, 0)),
    ]
    args = [proj, proj, proj, bcum, masks]
    if reverse:
        in_specs += [
            value_block(_GG),
            pl.BlockSpec((tb, dv), lambda b, h, n: (rb(b, n), h)),
            pl.BlockSpec((1, dv), lambda b, h, n: (0, h)),
        ]
        args += [proj, o_fwd, gn_w]
    out_spec = pl.BlockSpec((tb, dv), lambda b, h, n: (rb(b, n), h))
    out_shape = jax.ShapeDtypeStruct((T, GLA_WIDTH), BF16)
    scratch = [pltpu.VMEM((dv, dk), F32), pltpu.VMEM((tb // GLA_CHUNK, GLA_CHUNK, GLA_CHUNK), BF16)]
    return in_specs, args, out_spec, out_shape, scratch


def _mixer_kernel(*refs, reverse, n_ret_in, n_gla_in, n_ret_scratch):
    ret_in = refs[:n_ret_in]
    gla_in = refs[n_ret_in:n_ret_in + n_gla_in]
    ret_out, gla_out = refs[n_ret_in + n_gla_in:n_ret_in + n_gla_in + 2]
    scratch = refs[n_ret_in + n_gla_in + 2:]
    _ret_kernel(*ret_in, ret_out, *scratch[:n_ret_scratch], reverse=reverse)
    _gla_kernel(*gla_in, gla_out, *scratch[n_ret_scratch:], reverse=reverse)


def _mixer_scan(proj, decay_logit, bcum, batch, seq_len, reverse, o_fwd=(None, None), gn_w=(None, None)):
    assert RET_TB == GLA_TB and RET_HEADS == GLA_HEADS
    nb = seq_len // RET_TB

    def rb(b, n):
        return b * nb + ((nb - 1 - n) if reverse else n)

    r_specs, r_args, r_out, r_shape, r_scratch = _ret_parts(proj, decay_logit, rb, reverse, o_fwd[0], gn_w[0])
    g_specs, g_args, g_out, g_shape, g_scratch = _gla_parts(proj, bcum, rb, reverse, o_fwd[1], gn_w[1])
    return pl.pallas_call(
        functools.partial(_mixer_kernel, reverse=reverse, n_ret_in=len(r_specs), n_gla_in=len(g_specs),
                          n_ret_scratch=len(r_scratch)),
        grid=(batch, RET_HEADS, nb),
        in_specs=r_specs + g_specs,
        out_specs=[r_out, g_out],
        out_shape=[r_shape, g_shape],
        scratch_shapes=r_scratch + g_scratch,
        compiler_params=_cparams(("parallel", "parallel", "arbitrary"), 48),
        name="mixer_bwd" if reverse else "mixer_fwd",
    )(*r_args, *g_args)


OP_TM = 512


def _out_proj_kernel(mr_ref, mg_ref, w0_ref, w1_ref, x_ref, n2_ref, rh_ref, rl_ref, h_ref, xn_ref, aff_ref,
                     hs_ref):
    s = pl.program_id(0)
    slot = s % 2

    @pl.when(s == 0)
    def _():
        hs_ref[1] = jnp.zeros(hs_ref.shape[1:], F32)

    hp = hs_ref[1 - slot]
    ms = jnp.mean(hp * hp, axis=-1, keepdims=True)
    xn = hp * lax.rsqrt(ms + EPS) * n2_ref[...]
    xh = xn.astype(BF16)
    xn_ref[...] = _pack_bf16_pairs(xn)
    xl = (xn - xh.astype(F32)).astype(BF16)
    lt = _dot_nt(rh_ref[...], xh) + _dot_nt(rh_ref[...], xl) + _dot_nt(rl_ref[...], xh)
    m = jnp.max(lt, axis=0, keepdims=True)
    e = jnp.exp(lt - m)
    aff_ref[...] = e / jnp.sum(e, axis=0, keepdims=True)

    h = x_ref[...] + _dot(mr_ref[...], w0_ref[...]) + _dot(mg_ref[...], w1_ref[...])
    h_ref[...] = h
    hs_ref[slot] = h


def _out_proj(mix_r, mix_g, w_out, x2d, n2w, r_hi, r_lo):
    T = x2d.shape[0]
    tm = OP_TM
    half = RET_WIDTH
    nblk = T // tm

    def head(s):
        return jnp.minimum(s, nblk - 1)

    def tail(s):
        return jnp.maximum(s - 1, 0)

    return pl.pallas_call(
        _out_proj_kernel,
        grid=(nblk + 1,),
        in_specs=[
            pl.BlockSpec((tm, half), lambda s: (head(s), 0)),
            pl.BlockSpec((tm, half), lambda s: (head(s), 0)),
            pl.BlockSpec((half, D_MODEL), lambda s: (0, 0)),
            pl.BlockSpec((half, D_MODEL), lambda s: (1, 0)),
            pl.BlockSpec((tm, D_MODEL), lambda s: (head(s), 0)),
            pl.BlockSpec((1, D_MODEL), lambda s: (0, 0)),
            pl.BlockSpec((N_EXPERTS, D_MODEL), lambda s: (0, 0)),
            pl.BlockSpec((N_EXPERTS, D_MODEL), lambda s: (0, 0)),
        ],
        out_specs=[
            pl.BlockSpec((tm, D_MODEL), lambda s: (head(s), 0)),
            pl.BlockSpec((tm, D_MODEL // 2), lambda s: (tail(s), 0)),
            pl.BlockSpec((N_EXPERTS, tm), lambda s: (0, tail(s))),
        ],
        out_shape=[
            jax.ShapeDtypeStruct((T, D_MODEL), F32),
            jax.ShapeDtypeStruct((T, D_MODEL // 2), I32),
            jax.ShapeDtypeStruct((N_EXPERTS, T), F32),
        ],
        scratch_shapes=[pltpu.VMEM((2, tm, D_MODEL), F32)],
        compiler_params=_cparams(("arbitrary",), 60),
        name="out_proj",
    )(mix_r, mix_g, w_out, w_out, x2d, n2w, r_hi, r_lo)


def _select_kernel(a_ref, pos_ref, posb_ref, rankb_ref, affb_ref, *, cap):
    E, T = a_ref.shape
    tt = MOE_TT

    def count(pred):
        return jnp.sum(pred.astype(F32), axis=1, keepdims=True)

    def bisect(i, tau):
        cand = tau | jnp.left_shift(jnp.int32(1), 30 - i)
        bits = pltpu.bitcast(a_ref[...], I32)
        return jnp.where(count(bits >= cand) >= cap, cand, tau)

    tau = lax.fori_loop(0, 31, bisect, jnp.zeros((E, 1), I32))
    bits_all = pltpu.bitcast(a_ref[...], I32)
    quota = cap - count(bits_all > tau)

    before = (lax.broadcasted_iota(I32, (tt, tt), 0) < lax.broadcasted_iota(I32, (tt, tt), 1)).astype(BF16)
    below = (lax.broadcasted_iota(I32, (E, E), 1) < lax.broadcasted_iota(I32, (E, E), 0)).astype(BF16)

    def block(j, carry):
        c_eq, c_sel = carry
        off = pl.multiple_of(j * tt, tt)
        aff = a_ref[:, pl.ds(off, tt)]
        bits = pltpu.bitcast(aff, I32)
        eq = bits == tau
        eqf = eq.astype(F32)
        rank_eq = _dot(eqf.astype(BF16), before) + c_eq
        sel = (bits > tau) | (eq & (rank_eq < quota))
        self_ = sel.astype(F32)
        selb = self_.astype(BF16)
        slot = _dot(selb, before) + c_sel
        pos = jnp.where(sel, slot, -1.0).astype(I32)
        pos_ref[:, pl.ds(off, tt)] = pos
        per_tok = jnp.broadcast_to(jnp.sum(self_, axis=0, keepdims=True), (E, tt))
        rank = _dot(per_tok.astype(BF16), before) + _dot(below, selb)
        posb_ref[j] = pos
        rankb_ref[j] = jnp.where(sel, rank, -1.0).astype(I32)
        affb_ref[j] = aff
        return (c_eq + jnp.sum(eqf, axis=1, keepdims=True), c_sel + jnp.sum(self_, axis=1, keepdims=True))

    zero = jnp.zeros((E, 1), F32)
    lax.fori_loop(0, T // tt, block, (zero, zero))


def _select(aff, cap):
    E, T = aff.shape
    nb = T // MOE_TT
    blk = jax.ShapeDtypeStruct((nb, E, MOE_TT), I32)
    return pl.pallas_call(
        functools.partial(_select_kernel, cap=cap),
        out_shape=[jax.ShapeDtypeStruct((E, T), I32), blk, blk, jax.ShapeDtypeStruct((nb, E, MOE_TT), F32)],
        compiler_params=pltpu.CompilerParams(vmem_limit_bytes=40 * 1024 * 1024),
        name="select",
    )(aff)


def _regroup_rows(T):
    nb = T // MOE_TT
    rows = CAPACITY_FACTOR * T + SC_GATHER_ROWS * nb
    return -(-rows // MOE_TW) * MOE_TW


def _combine_schedule(posb, T):
    nb = posb.shape[0]
    tw, g = MOE_TW, SC_GATHER_ROWS
    n = jnp.sum((posb >= 0).reshape(nb, -1), axis=1).astype(I32)
    seg = (n + g - 1) // g * g
    hi = jnp.cumsum(seg)
    lo = hi - seg
    off = jnp.concatenate([jnp.zeros((1,), I32), hi])
    nwin_max = (N_EXPERTS * MOE_TT) // tw + 1
    w0 = lo // tw
    w1 = jnp.where(seg > 0, (hi - 1) // tw, w0)
    cand = jnp.arange(nwin_max, dtype=I32)
    win = w0[:, None] + cand[None, :]
    valid = (win <= w1[:, None]).reshape(-1)
    nwin_total = _regroup_rows(T) // tw
    pmax = nb + nwin_total
    jv = jnp.broadcast_to(jnp.arange(nb, dtype=I32)[:, None], win.shape).reshape(-1)
    wv = jnp.minimum(win, nwin_total - 1).reshape(-1)
    total = jnp.sum(valid.astype(I32))
    dst = jnp.where(valid, jnp.cumsum(valid.astype(I32)) - 1, pmax)
    pj, pw = (jnp.zeros((pmax,), I32).at[dst].set(a, mode="drop") for a in (jv, wv))
    real = jnp.arange(pmax, dtype=I32) < total
    pj, pw = (jnp.where(real, a, a[total - 1]) for a in (pj, pw))
    first = jnp.concatenate([jnp.ones((1,), bool), pj[1:] != pj[:-1]])
    last = jnp.concatenate([pj[1:] != pj[:-1], jnp.ones((1,), bool)]) | (jnp.arange(pmax, dtype=I32) == total - 1)
    flag = jnp.where(real, first.astype(I32) + 2 * last.astype(I32) + 4, 0)
    return off, (pj, pw, flag, lo[pj], hi[pj])


SC_LANES = 16
SC_CORES = 2
SC_SUBCORES = 16
SC_GATHER_ROWS = 32


def _dispatch(pos, xw, cap):
    E, T = pos.shape
    W = xw.shape[1]
    G = SC_GATHER_ROWS
    part_rows = cap // SC_CORES
    mesh = plsc.VectorSubcoreMesh(core_axis_name="c", subcore_axis_name="s")

    @pl.kernel(
        out_type=jax.ShapeDtypeStruct((E * cap, W), I32),
        mesh=mesh,
        scratch_types=[pltpu.VMEM((T,), I32), pltpu.VMEM((cap,), I32), pltpu.VMEM((G, W), I32)],
        compiler_params=pltpu.CompilerParams(needs_layout_passes=False),
        name="sc_dispatch",
    )
    def run(pos_hbm, x_hbm, xe_hbm, pos_v, idx_v, buf):
        e = lax.axis_index("s")
        part = lax.axis_index("c")
        pltpu.sync_copy(pos_hbm.at[e], pos_v)
        lane = lax.iota(I32, SC_LANES)

        @pl.loop(0, T // SC_LANES)
        def _(i):
            off = pl.multiple_of(i * SC_LANES, SC_LANES)
            p = pos_v[pl.ds(off, SC_LANES)]
            plsc.store_scatter(idx_v, [p], lane + off, mask=p >= 0)

        @pl.loop(0, part_rows // G)
        def _(g):
            o = pl.multiple_of(part * part_rows + g * G, G)
            pltpu.sync_copy(x_hbm.at[idx_v.at[pl.ds(o, G)]], buf)
            pltpu.sync_copy(buf, xe_hbm.at[pl.ds(e * cap + o, G)])

    return run(pos, xw)


FFN_TM = 2048
FFN_UNPACK_ROWS = 256


def _ffn_tile_width(cap):
    return 512 if min(FFN_TM, cap) <= 1024 else 256


def _ffn_kernel(x_ref, wg_ref, wu_ref, wd_ref, o_ref, xb_ref, hid_ref, *, tw):
    s = pl.program_id(2)
    n_up = D_FF // tw

    @pl.when(s == 0)
    def _():
        half = D_MODEL // 2

        def unpack_rows(i, carry):
            r = pl.multiple_of(i * FFN_UNPACK_ROWS, FFN_UNPACK_ROWS)
            lo, hi = _unpack_bf16_pairs(x_ref[pl.ds(r, FFN_UNPACK_ROWS), :])
            xb_ref[pl.ds(r, FFN_UNPACK_ROWS), :half] = lo.astype(BF16)
            xb_ref[pl.ds(r, FFN_UNPACK_ROWS), half:] = hi.astype(BF16)
            return carry

        lax.fori_loop(0, x_ref.shape[0] // FFN_UNPACK_ROWS, unpack_rows, 0)

    @pl.when(s < n_up)
    def _():
        x = xb_ref[...]
        g = _dot(x, wg_ref[...].astype(BF16))
        u = _dot(x, wu_ref[...].astype(BF16))
        col = pl.multiple_of(s * tw, tw)
        hid_ref[:, pl.ds(col, tw)] = (_silu(g) * u).astype(BF16)

    @pl.when(s >= n_up)
    def _():
        o_ref[...] = _pack_bf16_pairs(_dot(hid_ref[...], wd_ref[...].astype(BF16)))


def _ffn(xe, w_gate, w_up, w_down):
    E, cap, _ = xe.shape
    tm = min(FFN_TM, cap)
    tw = _ffn_tile_width(cap)
    n_up, n_down = D_FF // tw, D_MODEL // tw

    def up(e, m, s):
        return (e, 0, jnp.minimum(s, n_up - 1))

    def down(s):
        return jnp.maximum(s - n_up, 0)

    return pl.pallas_call(
        functools.partial(_ffn_kernel, tw=tw),
        grid=(E, cap // tm, n_up + n_down),
        in_specs=[
            pl.BlockSpec((None, tm, D_MODEL // 2), lambda e, m, s: (e, m, 0)),
            pl.BlockSpec((None, D_MODEL, tw), up),
            pl.BlockSpec((None, D_MODEL, tw), up),
            pl.BlockSpec((None, D_FF, tw), lambda e, m, s: (e, 0, down(s))),
        ],
        out_specs=pl.BlockSpec((None, tm, tw // 2), lambda e, m, s: (e, m, down(s))),
        out_shape=jax.ShapeDtypeStruct((E, cap, D_MODEL // 2), I32),
        scratch_shapes=[pltpu.VMEM((tm, D_MODEL), BF16), pltpu.VMEM((tm, D_FF), BF16)],
        compiler_params=_cparams(("parallel", "parallel", "arbitrary"), 60),
        name="ffn",
    )(xe, w_gate, w_up, w_down)


def _regroup(posb, rankb, affb, off, yw, cap, rows):
    NB, EB = posb.shape
    W = yw.shape[1]
    G, L, tt = SC_GATHER_ROWS, SC_LANES, MOE_TT
    n_workers = SC_CORES * SC_SUBCORES
    per = NB // n_workers
    mesh = plsc.VectorSubcoreMesh(core_axis_name="c", subcore_axis_name="s")
    off_pad = jnp.pad(off, (0, L))

    @pl.kernel(
        out_type=(jax.ShapeDtypeStruct((rows, W), I32), jax.ShapeDtypeStruct((rows,), I32),
                  jax.ShapeDtypeStruct((rows,), F32)),
        mesh=mesh,
        scratch_types=[pltpu.VMEM((EB,), I32), pltpu.VMEM((EB,), I32), pltpu.VMEM((EB,), F32),
                       pltpu.VMEM((EB,), I32), pltpu.VMEM((EB,), I32), pltpu.VMEM((EB,), F32),
                       pltpu.VMEM((G, W), I32), pltpu.VMEM((NB + 1 + L,), I32)],
        compiler_params=pltpu.CompilerParams(needs_layout_passes=False),
        name="sc_regroup",
    )
    def run(posb_hbm, rankb_hbm, affb_hbm, off_hbm, y_hbm, yg_hbm, tok_hbm, gate_hbm,
            pos_v, rank_v, aff_v, src_v, tok_v, gate_v, buf, off_v):
        wid = lax.axis_index("c") * SC_SUBCORES + lax.axis_index("s")
        pltpu.sync_copy(off_hbm, off_v)
        lane = lax.iota(I32, L)
        zi = jnp.zeros((L,), I32)
        zf = jnp.zeros((L,), F32)

        @pl.loop(0, per)
        def _(k):
            j = wid * per + k
            pltpu.sync_copy(posb_hbm.at[j], pos_v)
            pltpu.sync_copy(rankb_hbm.at[j], rank_v)
            pltpu.sync_copy(affb_hbm.at[j], aff_v)
            lo = jnp.max(plsc.load_gather(off_v, [zi + j]))
            hi = jnp.max(plsc.load_gather(off_v, [zi + j + 1]))

            @pl.loop(0, EB // L)
            def _(i):
                o = pl.multiple_of(i * L, L)
                src_v[pl.ds(o, L)] = zi
                tok_v[pl.ds(o, L)] = zi
                gate_v[pl.ds(o, L)] = zf

            @pl.loop(0, EB // L)
            def _(i):
                o = pl.multiple_of(i * L, L)
                p = pos_v[pl.ds(o, L)]
                r = rank_v[pl.ds(o, L)]
                m = p >= 0
                e = i // (tt // L)
                t0 = j * tt + (i % (tt // L)) * L
                plsc.store_scatter(src_v, [r], p + e * cap, mask=m)
                plsc.store_scatter(tok_v, [r], lane + t0, mask=m)
                plsc.store_scatter(gate_v, [r], aff_v[pl.ds(o, L)], mask=m)

            @pl.loop(0, (hi - lo) // G)
            def _(g):
                o = pl.multiple_of(g * G, G)
                dst = pl.multiple_of(lo + o, G)
                pltpu.sync_copy(y_hbm.at[src_v.at[pl.ds(o, G)]], buf)
                pltpu.sync_copy(buf, yg_hbm.at[pl.ds(dst, G)])
                pltpu.sync_copy(tok_v.at[pl.ds(o, G)], tok_hbm.at[pl.ds(dst, G)])
                pltpu.sync_copy(gate_v.at[pl.ds(o, G)], gate_hbm.at[pl.ds(dst, G)])

    return run(posb, rankb, affb, off_pad, yw)


def _combine_kernel(pj_ref, pw_ref, pf_ref, plo_ref, phi_ref, tok_ref, gate_ref, yg_ref, h_ref, nf_ref,
                    o_ref, acc_ref, *, group):
    p = pl.program_id(0)
    flag = pf_ref[p]
    tw, tt = MOE_TW, MOE_TT
    half = D_MODEL // 2

    hw = group // 2

    def col_blocks():
        for n in range(D_MODEL // group):
            yield slice(n * hw, (n + 1) * hw), slice(n * group, n * group + hw)
            yield slice(half + n * hw, half + (n + 1) * hw), slice(n * group + hw, (n + 1) * group)

    @pl.when((flag & 1) != 0)
    def _():
        for packed, natural in col_blocks():
            acc_ref[:, packed] = h_ref[:, natural]

    @pl.when((flag & 4) != 0)
    def _():
        lo, hi = plo_ref[p], phi_ref[p]
        row0 = pw_ref[p] * tw
        rid = lax.broadcasted_iota(I32, (tw, 1), 0) + row0
        keep = (rid >= lo) & (rid < hi)
        y_lo, y_hi = _unpack_bf16_pairs(yg_ref[...])
        y_lo = jnp.where(keep, y_lo, 0.0).astype(BF16)
        y_hi = jnp.where(keep, y_hi, 0.0).astype(BF16)
        tid = lax.broadcasted_iota(I32, (tt, tw), 0) + pj_ref[p] * tt
        cid = lax.broadcasted_iota(I32, (tt, tw), 1) + row0
        hit = (tok_ref[...] == tid) & (cid >= lo) & (cid < hi)
        weights = jnp.where(hit, gate_ref[...], 0.0).astype(BF16)
        acc_ref[:, :half] += _dot(weights, y_lo)
        acc_ref[:, half:] += _dot(weights, y_hi)

    @pl.when((flag & 2) != 0)
    def _():
        y = acc_ref[...]
        scale = lax.rsqrt(jnp.mean(y * y, axis=-1, keepdims=True) + EPS)
        for packed, natural in col_blocks():
            o_ref[:, natural] = acc_ref[:, packed] * scale * nf_ref[:, natural]


def _combine(lists, tok, gate, yg, h, nfw, group):
    pj, pw, pf, plo, phi = lists
    T = h.shape[0]
    tw, tt = MOE_TW, MOE_TT
    nwin = yg.shape[0] // tw
    grid_spec = pltpu.PrefetchScalarGridSpec(
        num_scalar_prefetch=5,
        grid=(pj.shape[0],),
        in_specs=[
            pl.BlockSpec((None, 1, tw), lambda p, pj, pw, *_: (pw[p], 0, 0)),
            pl.BlockSpec((None, 1, tw), lambda p, pj, pw, *_: (pw[p], 0, 0)),
            pl.BlockSpec((tw, D_MODEL // 2), lambda p, pj, pw, *_: (pw[p], 0)),
            pl.BlockSpec((tt, D_MODEL), lambda p, pj, pw, *_: (pj[p], 0)),
            pl.BlockSpec((1, D_MODEL), lambda p, pj, pw, *_: (0, 0)),
        ],
        out_specs=pl.BlockSpec((tt, D_MODEL), lambda p, pj, pw, *_: (pj[p], 0)),
        scratch_shapes=[pltpu.VMEM((tt, D_MODEL), F32)],
    )
    return pl.pallas_call(
        functools.partial(_combine_kernel, group=group),
        grid_spec=grid_spec,
        out_shape=jax.ShapeDtypeStruct((T, D_MODEL), F32),
        compiler_params=_cparams(("arbitrary",), 32),
        name="combine",
    )(pj, pw, pf, plo, phi, tok.reshape(nwin, 1, tw), gate.reshape(nwin, 1, tw), yg, h, nfw)


def _rope_tables(seq_len):
    d = RET_DK
    inv = ROPE_BASE ** (-jnp.arange(0, d, 2, dtype=F32) / d)
    ang = jnp.arange(seq_len, dtype=F32)[:, None] * inv[None, :]
    return jnp.cos(ang), jnp.sin(ang)


def _chunk_tri(n, chunk, upper):
    r = np.arange(n)
    same = (r[:, None] // chunk) == (r[None, :] // chunk)
    tri = (r[:, None] <= r[None, :]) if upper else (r[:, None] >= r[None, :])
    return jnp.asarray(same & tri, BF16)


def _prep_params(norm1_w, w_in, ret_gn_w, gla_gate_up, gla_gate_bias, gla_gn_w, w_out, norm2_w, router_w,
                 normf_w):
    w = w_in[0]
    w_main = w[:, :IN_MAIN].astype(BF16)
    w_ga = jnp.pad(w[:, IN_MAIN:], ((0, 0), (0, LANE - 2 * GLA_RANK))).astype(BF16)
    cs = np.ones((1, IN_MAIN), np.float32)
    cs[:, _RQ:_RQ + RET_WIDTH] = RET_DK ** -0.5
    cs[:, _GQ:_GQ + GLA_KEY_WIDTH] = GLA_DK ** -0.5
    up = gla_gate_up[0].astype(F32)
    up_pad = jnp.zeros((LANE, 2 * GLA_KEY_WIDTH), F32)
    up_pad = up_pad.at[:GLA_RANK, :GLA_KEY_WIDTH].set(up[0])
    up_pad = up_pad.at[GLA_RANK:2 * GLA_RANK, GLA_KEY_WIDTH:].set(up[1])
    rt = router_w[0].T.astype(F32)
    r_hi = rt.astype(BF16)
    r_lo = (rt - r_hi.astype(F32)).astype(BF16)
    return dict(
        n1w=norm1_w[0].reshape(1, D_MODEL).astype(F32),
        w_main=w_main, w_ga=w_ga, colscale=jnp.asarray(cs),
        up_pad=up_pad.astype(BF16),
        bias=gla_gate_bias[0].reshape(1, 2 * GLA_KEY_WIDTH).astype(F32),
        lf=_chunk_tri(GATE_TM, GLA_CHUNK, upper=False),
        lb=_chunk_tri(GATE_TM, GLA_CHUNK, upper=True),
        ret_gn=ret_gn_w[0].reshape(1, RET_WIDTH).astype(F32),
        gla_gn=gla_gn_w[0].reshape(1, GLA_WIDTH).astype(F32),
        w_out=w_out[0].astype(BF16),
        n2w=norm2_w[0].reshape(1, D_MODEL).astype(F32),
        r_hi=r_hi, r_lo=r_lo,
        nfw=normf_w.reshape(1, D_MODEL).astype(F32),
    )


def _trunk(x, pp, decay_logit, w_gate, w_up, w_down):
    B, L, _ = x.shape
    T = B * L
    x2d = x.reshape(T, D_MODEL)
    cos, sin = _rope_tables(L)
    proj, ga = _in_proj(x2d, pp["n1w"], pp["w_main"], pp["w_ga"], pp["colscale"], cos, sin, L)
    b_f, b_b = _gla_gates(ga, pp["up_pad"], pp["bias"], pp["lf"], pp["lb"])

    fwd = _mixer_scan(proj, decay_logit, b_f, B, L, reverse=False)
    mix_r, mix_g = _mixer_scan(proj, decay_logit, b_b, B, L, reverse=True, o_fwd=fwd,
                               gn_w=(pp["ret_gn"], pp["gla_gn"]))

    h, xn2, aff = _out_proj(mix_r, mix_g, pp["w_out"], x2d, pp["n2w"], pp["r_hi"], pp["r_lo"])

    cap = CAPACITY_FACTOR * T // N_EXPERTS
    pos, posb, rankb, affb = _select(aff, cap)
    off, c_lists = _combine_schedule(posb, T)
    xe = _dispatch(pos, xn2, cap).reshape(N_EXPERTS, cap, D_MODEL // 2)
    ye = _ffn(xe, w_gate, w_up, w_down).reshape(N_EXPERTS * cap, D_MODEL // 2)
    nb = T // MOE_TT
    yg, tok, gate = _regroup(posb.reshape(nb, -1), rankb.reshape(nb, -1), affb.reshape(nb, -1), off, ye, cap,
                             _regroup_rows(T))
    y = _combine(c_lists, tok, gate, yg, h, pp["nfw"], _ffn_tile_width(cap))
    return y.reshape(B, L, D_MODEL)


def kernel(x_prompt, x_sample, norm1_w, w_in, ret_decay_logit, ret_gn_w, gla_gate_up, gla_gate_bias,
           gla_gn_w, w_out, norm2_w, router_w, w_gate, w_up, w_down, normf_w):
    pp = _prep_params(norm1_w, w_in, ret_gn_w, gla_gate_up, gla_gate_bias, gla_gn_w, w_out, norm2_w,
                      router_w, normf_w)
    decay_logit = ret_decay_logit[0].astype(F32)
    args = (pp, decay_logit, w_gate[0], w_up[0], w_down[0])
    return (_trunk(x_prompt, *args), _trunk(x_sample, *args))
```

```python
import functools

import numpy as np
import jax
import jax.numpy as jnp
from jax import lax
from jax.experimental import pallas as pl
from jax.experimental.pallas import tpu as pltpu
from jax.experimental.pallas import tpu_sc as plsc

F32, BF16, I32 = jnp.float32, jnp.bfloat16, jnp.int32

D_MODEL = 2048
RET_WIDTH = 1024
RET_HEADS = 4
RET_DK = 256
RET_DV = 256
GLA_WIDTH = 1024
GLA_HEADS = 4
GLA_DK = 128
GLA_DV = 256
GLA_KEY_WIDTH = 512
GLA_RANK = 16
GLA_TAU = 16.0
RET_CHUNK = 256
GLA_CHUNK = 64
GLA_SUB = 16
ROPE_BASE = 10000.0
N_EXPERTS = 16
CAPACITY_FACTOR = 2
D_FF = 2048
EPS = 1e-6
LOG2_E = 1.4426950408889634
IN_MAIN = 4 * RET_WIDTH + 2 * GLA_KEY_WIDTH + 2 * GLA_WIDTH

_RQ, _RK, _RV, _RG = 0, 1024, 2048, 3072
_GQ, _GK, _GV, _GG = 4096, 4608, 5120, 6144

LANE = 128
MOE_TT = 256
MOE_TW = 256
V7X_VMEM_BYTES = 64 * 1024 * 1024


def _cparams(sem, vmem_mb):
    return pltpu.CompilerParams(dimension_semantics=sem, vmem_limit_bytes=vmem_mb * 1024 * 1024)


def _log_sigmoid(z):
    return jnp.minimum(z, 0.0) - jnp.log1p(jnp.exp(-jnp.abs(z)))


def _silu(g):
    return g * (1.0 / (1.0 + jnp.exp(-g)))


def _dot_nt(a, b):
    return lax.dot_general(a, b, (((1,), (1,)), ((), ())), preferred_element_type=F32)


def _dot_tn(a, b):
    return lax.dot_general(a, b, (((0,), (0,)), ((), ())), preferred_element_type=F32)


def _dot(a, b):
    return jnp.dot(a, b, preferred_element_type=F32)


def _pack_bf16_pairs(x):
    bits = pltpu.bitcast(x.astype(BF16).astype(F32), I32)
    w = x.shape[1] // 2
    return bits[:, w:] | lax.shift_right_logical(bits[:, :w], 16)


def _unpack_bf16_pairs(words):
    lo = pltpu.bitcast(lax.shift_left(words, 16), F32)
    hi = pltpu.bitcast(words & jnp.int32(-65536), F32)
    return lo, hi


IP_TM = 1024
IP_TN = 1024


def _in_proj_kernel(x_ref, n1_ref, w_ref, wga_ref, cs_ref, cos_ref, sin_ref, o_ref, ga_ref, xn_ref):
    j = pl.program_id(1)

    @pl.when(j == 0)
    def _():
        x = x_ref[...]
        ms = jnp.mean(x * x, axis=-1, keepdims=True)
        xn = (x * lax.rsqrt(ms + EPS) * n1_ref[...]).astype(BF16)
        xn_ref[...] = xn
        ga_ref[...] = _dot(xn, wga_ref[...])

    acc = _dot(xn_ref[...], w_ref[...]) * cs_ref[...]
    n_rope_blocks = 2 * RET_WIDTH // IP_TN

    @pl.when(j < n_rope_blocks)
    def _():
        cos = cos_ref[...]
        sin = sin_ref[...]
        for h in range(IP_TN // RET_DK):
            x1 = acc[:, 2 * h * LANE:(2 * h + 1) * LANE]
            x2 = acc[:, (2 * h + 1) * LANE:(2 * h + 2) * LANE]
            o_ref[2 * h] = (x1 * cos - x2 * sin).astype(BF16)
            o_ref[2 * h + 1] = (x1 * sin + x2 * cos).astype(BF16)

    @pl.when(j >= n_rope_blocks)
    def _():
        for c in range(IP_TN // LANE):
            o_ref[c] = acc[:, c * LANE:(c + 1) * LANE].astype(BF16)


def _in_proj(x2d, n1w, w_main, w_ga, colscale, cos, sin, seq_len):
    T = x2d.shape[0]
    tm, tn = IP_TM, IP_TN
    nlb = seq_len // tm
    return pl.pallas_call(
        _in_proj_kernel,
        grid=(T // tm, IN_MAIN // tn),
        in_specs=[
            pl.BlockSpec((tm, D_MODEL), lambda i, j: (i, 0)),
            pl.BlockSpec((1, D_MODEL), lambda i, j: (0, 0)),
            pl.BlockSpec((D_MODEL, tn), lambda i, j: (0, j)),
            pl.BlockSpec((D_MODEL, LANE), lambda i, j: (0, 0)),
            pl.BlockSpec((1, tn), lambda i, j: (0, j)),
            pl.BlockSpec((tm, LANE), lambda i, j: (i % nlb, 0)),
            pl.BlockSpec((tm, LANE), lambda i, j: (i % nlb, 0)),
        ],
        out_specs=[
            pl.BlockSpec((tn // LANE, tm, LANE), lambda i, j: (j, i, 0)),
            pl.BlockSpec((tm, LANE), lambda i, j: (i, 0)),
        ],
        out_shape=[
            jax.ShapeDtypeStruct((IN_MAIN // LANE, T, LANE), BF16),
            jax.ShapeDtypeStruct((T, LANE), F32),
        ],
        scratch_shapes=[pltpu.VMEM((tm, D_MODEL), BF16)],
        compiler_params=_cparams(("parallel", "arbitrary"), 48),
        name="in_proj",
    )(x2d, n1w, w_main, w_ga, colscale, cos, sin)


GATE_TM = 512


def _gates_kernel(ga_ref, up_ref, bias_ref, lf_ref, lb_ref, bf_ref, bb_ref):
    z = _dot(ga_ref[...].astype(BF16), up_ref[...]) + bias_ref[...]
    la = _log_sigmoid(z) * (LOG2_E / GLA_TAU)
    hi = la.astype(BF16)
    lo = (la - hi.astype(F32)).astype(BF16)
    kw = GLA_KEY_WIDTH
    bf_ref[...] = _dot(lf_ref[...], hi[:, :kw]) + _dot(lf_ref[...], lo[:, :kw])
    bb_ref[...] = _dot(lb_ref[...], hi[:, kw:]) + _dot(lb_ref[...], lo[:, kw:])


def _gla_gates(ga, up_pad, bias, lf, lb):
    T = ga.shape[0]
    tm = GATE_TM
    kw = GLA_KEY_WIDTH
    return pl.pallas_call(
        _gates_kernel,
        grid=(T // tm,),
        in_specs=[
            pl.BlockSpec((tm, LANE), lambda i: (i, 0)),
            pl.BlockSpec((LANE, 2 * kw), lambda i: (0, 0)),
            pl.BlockSpec((1, 2 * kw), lambda i: (0, 0)),
            pl.BlockSpec((tm, tm), lambda i: (0, 0)),
            pl.BlockSpec((tm, tm), lambda i: (0, 0)),
        ],
        out_specs=[pl.BlockSpec((tm, kw), lambda i: (i, 0)), pl.BlockSpec((tm, kw), lambda i: (i, 0))],
        out_shape=[jax.ShapeDtypeStruct((T, kw), F32), jax.ShapeDtypeStruct((T, kw), F32)],
        compiler_params=_cparams(("parallel",), 32),
        name="gla_gates",
    )(ga, up_pad, bias, lf, lb)


def _wide(ref, rows):
    return jnp.concatenate([ref[0, rows, :], ref[1, rows, :]], axis=1)


def _finish_heads(tot, gn, gate):
    ms = jnp.mean(tot * tot, axis=-1, keepdims=True)
    yn = tot * lax.rsqrt(ms + EPS) * gn
    return (yn * _silu(gate.astype(F32))).astype(BF16)


RET_TB = 1024


def _ret_kernel(dl_ref, q_ref, k_ref, v_ref, *rest, reverse):
    if reverse:
        g_ref, of_ref, gn_ref, o_ref, s_ref, intra_ref, qd_ref, kd_ref, cd_ref, p_ref, u_ref = rest
    else:
        o_ref, s_ref, intra_ref, qd_ref, kd_ref, cd_ref, p_ref, u_ref = rest
    h = pl.program_id(1)
    n = pl.program_id(2)
    C = RET_CHUNK

    @pl.when(n == 0)
    def _():
        s_ref[...] = jnp.zeros_like(s_ref)
        logit = dl_ref[1 if reverse else 0, h]
        lg = _log_sigmoid(jnp.full((C, RET_DV), logit, F32))
        lg_c = _log_sigmoid(jnp.full((C, C), logit, F32))
        lg_r = _log_sigmoid(jnp.full((1, RET_DV), logit, F32))
        ri = lax.broadcasted_iota(I32, (C, RET_DV), 0).astype(F32)
        rc = lax.broadcasted_iota(I32, (C, C), 0).astype(F32)
        cc = lax.broadcasted_iota(I32, (C, C), 1).astype(F32)
        diff = (cc - rc) if reverse else (rc - cc)
        intra_ref[...] = jnp.where(diff >= 0, jnp.exp(lg_c * diff), 0.0)
        if reverse:
            qd_ref[...] = jnp.exp(lg * (C - ri))
            kd_ref[...] = jnp.exp(lg * ri)
        else:
            qd_ref[...] = jnp.exp(lg * (ri + 1.0))
            kd_ref[...] = jnp.exp(lg * (C - 1.0 - ri))
        cd_ref[...] = jnp.exp(lg_r * C)

    nchunks = o_ref.shape[0] // C
    for c in range(nchunks):
        rows = slice(c * C, (c + 1) * C)
        k = _wide(k_ref, rows)
        p_ref[c] = (_dot_nt(_wide(q_ref, rows), k) * intra_ref[...]).astype(BF16)
        kd = (k.astype(F32) * kd_ref[...]).astype(BF16)
        u_ref[c] = _dot_tn(kd, _wide(v_ref, rows))
    order = range(nchunks - 1, -1, -1) if reverse else range(nchunks)
    for c in order:
        rows = slice(c * C, (c + 1) * C)
        state = s_ref[...]
        o = _dot(p_ref[c], _wide(v_ref, rows)) + _dot(_wide(q_ref, rows), state.astype(BF16)) * qd_ref[...]
        s_ref[...] = state * cd_ref[...] + u_ref[c]
        if reverse:
            tot = of_ref[rows, :].astype(F32) + o
            o_ref[rows, :] = _finish_heads(tot, gn_ref[...], _wide(g_ref, rows))
        else:
            o_ref[rows, :] = o.astype(BF16)


def _ret_parts(proj, decay_logit, rb, reverse, o_fwd=None, gn_w=None):
    T = proj.shape[1]
    tb = RET_TB
    dk, dv, C = RET_DK, RET_DV, RET_CHUNK

    def head(base):
        return pl.BlockSpec((dk // LANE, tb, LANE), lambda b, h, n: (base // dk + h, rb(b, n), 0))

    in_specs = [pl.BlockSpec(memory_space=pltpu.SMEM), head(_RQ), head(_RK), head(_RV)]
    args = [decay_logit, proj, proj, proj]
    if reverse:
        in_specs += [
            head(_RG),
            pl.BlockSpec((tb, dv), lambda b, h, n: (rb(b, n), h)),
            pl.BlockSpec((1, dv), lambda b, h, n: (0, h)),
        ]
        args += [proj, o_fwd, gn_w]
    out_spec = pl.BlockSpec((tb, dv), lambda b, h, n: (rb(b, n), h))
    out_shape = jax.ShapeDtypeStruct((T, RET_WIDTH), BF16)
    scratch = [
        pltpu.VMEM((dk, dv), F32),
        pltpu.VMEM((C, C), F32),
        pltpu.VMEM((C, dv), F32),
        pltpu.VMEM((C, dk), F32),
        pltpu.VMEM((1, dv), F32),
        pltpu.VMEM((tb // C, C, C), BF16),
        pltpu.VMEM((tb // C, dk, dv), F32),
    ]
    return in_specs, args, out_spec, out_shape, scratch


GLA_TB = 1024
GLA_UNROLL = 16


GLA_LEVELS = (32, 16, 8, 4, 2, 1)
SUBLANES = 8


def _gla_tables(reverse):
    C = GLA_CHUNK
    r = np.arange(C)
    masks = np.zeros((len(GLA_LEVELS) + 1, C, C), np.float32)
    for l, s in enumerate(GLA_LEVELS):
        upper = (r & s) != 0
        same = (r[:, None] // (2 * s)) == (r[None, :] // (2 * s))
        lhs_rows = ~upper if reverse else upper
        masks[l] = same & lhs_rows[:, None] & ~lhs_rows[None, :]
    masks[-1] = np.eye(C)
    return jnp.asarray(masks, F32)


def _gla_kernel(q_ref, k_ref, v_ref, b_ref, mask_ref, *rest, reverse):
    if reverse:
        g_ref, of_ref, gn_ref, o_ref, st_ref, sc_ref = rest
    else:
        o_ref, st_ref, sc_ref = rest
    n = pl.program_id(2)
    C = GLA_CHUNK

    @pl.when(n == 0)
    def _():
        st_ref[...] = jnp.zeros_like(st_ref)

    nchunks = q_ref.shape[0] // C
    sub_row = lax.broadcasted_iota(I32, (SUBLANES, GLA_DK), 0)
    zero_rows = jnp.zeros((SUBLANES, GLA_DK), F32)

    def chunk_scores(c, carry):
        c0 = pl.multiple_of(c * C, C)
        qb = q_ref[pl.ds(c0, C), :]
        kb = k_ref[pl.ds(c0, C), :]
        q = qb.astype(F32)
        k = kb.astype(F32)
        b = b_ref[pl.ds(c0, C), :]

        def mid_row(r):
            return jnp.broadcast_to(b[r:r + 1, :], (SUBLANES, GLA_DK))

        scores = mask_ref[len(GLA_LEVELS)] * _dot_nt(qb, kb)
        for l, s in enumerate(GLA_LEVELS):
            lhs, rhs = [], []
            for g in range(C // SUBLANES):
                r0 = g * SUBLANES
                rows = slice(r0, r0 + SUBLANES)
                if s >= SUBLANES:
                    m = mid_row((r0 // (2 * s)) * (2 * s) + s)
                    is_lhs = ((r0 & s) != 0) != reverse
                    if is_lhs:
                        lhs.append(q[rows] * jnp.exp2(b[rows] - m))
                        rhs.append(zero_rows)
                    else:
                        lhs.append(zero_rows)
                        rhs.append(k[rows] * jnp.exp2(m - b[rows]))
                else:
                    m = mid_row(r0 + SUBLANES - s)
                    for blk in range(SUBLANES // (2 * s) - 2, -1, -1):
                        m = jnp.where(sub_row < (blk + 1) * 2 * s, mid_row(r0 + blk * 2 * s + s), m)
                    upper = (sub_row & s) != 0
                    is_lhs = jnp.logical_not(upper) if reverse else upper
                    lhs.append(jnp.where(is_lhs, q[rows] * jnp.exp2(b[rows] - m), 0.0))
                    rhs.append(jnp.where(is_lhs, 0.0, k[rows] * jnp.exp2(m - b[rows])))
            lhs = jnp.concatenate(lhs, axis=0).astype(BF16)
            rhs = jnp.concatenate(rhs, axis=0).astype(BF16)
            scores = scores + mask_ref[l] * _dot_nt(lhs, rhs)
        sc_ref[c] = scores.astype(BF16)
        return carry

    lax.fori_loop(0, nchunks, chunk_scores, 0, unroll=GLA_UNROLL)

    def chunk(ci, carry):
        c = (nchunks - 1 - ci) if reverse else ci
        c0 = pl.multiple_of(c * C, C)
        q = q_ref[pl.ds(c0, C), :].astype(F32)
        k = k_ref[pl.ds(c0, C), :].astype(F32)
        v = _wide(v_ref, pl.ds(c0, C))
        b = b_ref[pl.ds(c0, C), :]
        b_end = b[0:1, :] if reverse else b[C - 1:C, :]

        st = st_ref[...]
        o = _dot_nt((q * jnp.exp2(b)).astype(BF16), st.astype(BF16))
        ke = (k * jnp.exp2(b_end - b)).astype(BF16)
        st_ref[...] = st * jnp.exp2(b_end) + _dot_tn(v, ke)
        o = o + _dot(sc_ref[c], v)
        if reverse:
            tot = of_ref[pl.ds(c0, C), :].astype(F32) + o
            o_ref[pl.ds(c0, C), :] = _finish_heads(tot, gn_ref[...], _wide(g_ref, pl.ds(c0, C)))
        else:
            o_ref[pl.ds(c0, C), :] = o.astype(BF16)
        return carry

    lax.fori_loop(0, nchunks, chunk, 0, unroll=GLA_UNROLL)


def _gla_parts(proj, bcum, rb, reverse, o_fwd=None, gn_w=None):
    T = proj.shape[1]
    tb = GLA_TB
    dk, dv = GLA_DK, GLA_DV
    masks = _gla_tables(reverse)

    def key_block(base):
        return pl.BlockSpec((None, tb, LANE), lambda b, h, n: (base // dk + h, rb(b, n), 0))

    def value_block(base):
        return pl.BlockSpec((dv // LANE, tb, LANE), lambda b, h, n: (base // dv + h, rb(b, n), 0))

    in_specs = [
        key_block(_GQ), key_block(_GK), value_block(_GV),
        pl.BlockSpec((tb, dk), lambda b, h, n: (rb(b, n), h)),
        pl.BlockSpec(masks.shape, lambda b, h, n: (0, 0, 0)),
    ]
    args = [proj, proj, proj, bcum, masks]
    if reverse:
        in_specs += [
            value_block(_GG),
            pl.BlockSpec((tb, dv), lambda b, h, n: (rb(b, n), h)),
            pl.BlockSpec((1, dv), lambda b, h, n: (0, h)),
        ]
        args += [proj, o_fwd, gn_w]
    out_spec = pl.BlockSpec((tb, dv), lambda b, h, n: (rb(b, n), h))
    out_shape = jax.ShapeDtypeStruct((T, GLA_WIDTH), BF16)
    scratch = [pltpu.VMEM((dv, dk), F32), pltpu.VMEM((tb // GLA_CHUNK, GLA_CHUNK, GLA_CHUNK), BF16)]
    return in_specs, args, out_spec, out_shape, scratch


def _mixer_kernel(*refs, reverse, n_ret_in, n_gla_in, n_ret_scratch):
    ret_in = refs[:n_ret_in]
    gla_in = refs[n_ret_in:n_ret_in + n_gla_in]
    ret_out, gla_out = refs[n_ret_in + n_gla_in:n_ret_in + n_gla_in + 2]
    scratch = refs[n_ret_in + n_gla_in + 2:]
    _ret_kernel(*ret_in, ret_out, *scratch[:n_ret_scratch], reverse=reverse)
    _gla_kernel(*gla_in, gla_out, *scratch[n_ret_scratch:], reverse=reverse)


def _mixer_scan(proj, decay_logit, bcum, batch, seq_len, reverse, o_fwd=(None, None), gn_w=(None, None)):
    assert RET_TB == GLA_TB and RET_HEADS == GLA_HEADS
    nb = seq_len // RET_TB

    def rb(b, n):
        return b * nb + ((nb - 1 - n) if reverse else n)

    r_specs, r_args, r_out, r_shape, r_scratch = _ret_parts(proj, decay_logit, rb, reverse, o_fwd[0], gn_w[0])
    g_specs, g_args, g_out, g_shape, g_scratch = _gla_parts(proj, bcum, rb, reverse, o_fwd[1], gn_w[1])
    return pl.pallas_call(
        functools.partial(_mixer_kernel, reverse=reverse, n_ret_in=len(r_specs), n_gla_in=len(g_specs),
                          n_ret_scratch=len(r_scratch)),
        grid=(batch, RET_HEADS, nb),
        in_specs=r_specs + g_specs,
        out_specs=[r_out, g_out],
        out_shape=[r_shape, g_shape],
        scratch_shapes=r_scratch + g_scratch,
        compiler_params=_cparams(("parallel", "parallel", "arbitrary"), 48),
        name="mixer_bwd" if reverse else "mixer_fwd",
    )(*r_args, *g_args)


OP_TM = 512


def _out_proj_kernel(mr_ref, mg_ref, w0_ref, w1_ref, x_ref, n2_ref, rh_ref, rl_ref, h_ref, xn_ref, aff_ref,
                     hs_ref):
    s = pl.program_id(0)
    slot = s % 2

    @pl.when(s == 0)
    def _():
        hs_ref[1] = jnp.zeros(hs_ref.shape[1:], F32)

    hp = hs_ref[1 - slot]
    ms = jnp.mean(hp * hp, axis=-1, keepdims=True)
    xn = hp * lax.rsqrt(ms + EPS) * n2_ref[...]
    xh = xn.astype(BF16)
    xn_ref[...] = _pack_bf16_pairs(xn)
    xl = (xn - xh.astype(F32)).astype(BF16)
    lt = _dot_nt(rh_ref[...], xh) + _dot_nt(rh_ref[...], xl) + _dot_nt(rl_ref[...], xh)
    m = jnp.max(lt, axis=0, keepdims=True)
    e = jnp.exp(lt - m)
    aff_ref[...] = e / jnp.sum(e, axis=0, keepdims=True)

    h = x_ref[...] + _dot(mr_ref[...], w0_ref[...]) + _dot(mg_ref[...], w1_ref[...])
    h_ref[...] = h
    hs_ref[slot] = h


def _out_proj(mix_r, mix_g, w_out, x2d, n2w, r_hi, r_lo):
    T = x2d.shape[0]
    tm = OP_TM
    half = RET_WIDTH
    nblk = T // tm

    def head(s):
        return jnp.minimum(s, nblk - 1)

    def tail(s):
        return jnp.maximum(s - 1, 0)

    return pl.pallas_call(
        _out_proj_kernel,
        grid=(nblk + 1,),
        in_specs=[
            pl.BlockSpec((tm, half), lambda s: (head(s), 0)),
            pl.BlockSpec((tm, half), lambda s: (head(s), 0)),
            pl.BlockSpec((half, D_MODEL), lambda s: (0, 0)),
            pl.BlockSpec((half, D_MODEL), lambda s: (1, 0)),
            pl.BlockSpec((tm, D_MODEL), lambda s: (head(s), 0)),
            pl.BlockSpec((1, D_MODEL), lambda s: (0, 0)),
            pl.BlockSpec((N_EXPERTS, D_MODEL), lambda s: (0, 0)),
            pl.BlockSpec((N_EXPERTS, D_MODEL), lambda s: (0, 0)),
        ],
        out_specs=[
            pl.BlockSpec((tm, D_MODEL), lambda s: (head(s), 0)),
            pl.BlockSpec((tm, D_MODEL // 2), lambda s: (tail(s), 0)),
            pl.BlockSpec((N_EXPERTS, tm), lambda s: (0, tail(s))),
        ],
        out_shape=[
            jax.ShapeDtypeStruct((T, D_MODEL), F32),
            jax.ShapeDtypeStruct((T, D_MODEL // 2), I32),
            jax.ShapeDtypeStruct((N_EXPERTS, T), F32),
        ],
        scratch_shapes=[pltpu.VMEM((2, tm, D_MODEL), F32)],
        compiler_params=_cparams(("arbitrary",), 60),
        name="out_proj",
    )(mix_r, mix_g, w_out, w_out, x2d, n2w, r_hi, r_lo)


def _select_kernel(a_ref, pos_ref, posb_ref, rankb_ref, affb_ref, *, cap):
    E, T = a_ref.shape
    tt = MOE_TT

    def count(pred):
        return jnp.sum(pred.astype(F32), axis=1, keepdims=True)

    def bisect(i, tau):
        cand = tau | jnp.left_shift(jnp.int32(1), 30 - i)
        bits = pltpu.bitcast(a_ref[...], I32)
        return jnp.where(count(bits >= cand) >= cap, cand, tau)

    tau = lax.fori_loop(0, 31, bisect, jnp.zeros((E, 1), I32))
    bits_all = pltpu.bitcast(a_ref[...], I32)
    quota = cap - count(bits_all > tau)

    before = (lax.broadcasted_iota(I32, (tt, tt), 0) < lax.broadcasted_iota(I32, (tt, tt), 1)).astype(BF16)
    below = (lax.broadcasted_iota(I32, (E, E), 1) < lax.broadcasted_iota(I32, (E, E), 0)).astype(BF16)

    def block(j, carry):
        c_eq, c_sel = carry
        off = pl.multiple_of(j * tt, tt)
        aff = a_ref[:, pl.ds(off, tt)]
        bits = pltpu.bitcast(aff, I32)
        eq = bits == tau
        eqf = eq.astype(F32)
        rank_eq = _dot(eqf.astype(BF16), before) + c_eq
        sel = (bits > tau) | (eq & (rank_eq < quota))
        self_ = sel.astype(F32)
        selb = self_.astype(BF16)
        slot = _dot(selb, before) + c_sel
        pos = jnp.where(sel, slot, -1.0).astype(I32)
        pos_ref[:, pl.ds(off, tt)] = pos
        per_tok = jnp.broadcast_to(jnp.sum(self_, axis=0, keepdims=True), (E, tt))
        rank = _dot(per_tok.astype(BF16), before) + _dot(below, selb)
        posb_ref[j] = pos
        rankb_ref[j] = jnp.where(sel, rank, -1.0).astype(I32)
        affb_ref[j] = aff
        return (c_eq + jnp.sum(eqf, axis=1, keepdims=True), c_sel + jnp.sum(self_, axis=1, keepdims=True))

    zero = jnp.zeros((E, 1), F32)
    lax.fori_loop(0, T // tt, block, (zero, zero))


def _select(aff, cap):
    E, T = aff.shape
    nb = T // MOE_TT
    blk = jax.ShapeDtypeStruct((nb, E, MOE_TT), I32)
    return pl.pallas_call(
        functools.partial(_select_kernel, cap=cap),
        out_shape=[jax.ShapeDtypeStruct((E, T), I32), blk, blk, jax.ShapeDtypeStruct((nb, E, MOE_TT), F32)],
        compiler_params=pltpu.CompilerParams(vmem_limit_bytes=40 * 1024 * 1024),
        name="select",
    )(aff)


def _regroup_rows(T):
    nb = T // MOE_TT
    rows = CAPACITY_FACTOR * T + SC_GATHER_ROWS * nb
    return -(-rows // MOE_TW) * MOE_TW


def _combine_schedule(posb, T):
    nb = posb.shape[0]
    tw, g = MOE_TW, SC_GATHER_ROWS
    n = jnp.sum((posb >= 0).reshape(nb, -1), axis=1).astype(I32)
    seg = (n + g - 1) // g * g
    hi = jnp.cumsum(seg)
    lo = hi - seg
    off = jnp.concatenate([jnp.zeros((1,), I32), hi])
    nwin_max = (N_EXPERTS * MOE_TT) // tw + 1
    w0 = lo // tw
    w1 = jnp.where(seg > 0, (hi - 1) // tw, w0)
    cand = jnp.arange(nwin_max, dtype=I32)
    win = w0[:, None] + cand[None, :]
    valid = (win <= w1[:, None]).reshape(-1)
    nwin_total = _regroup_rows(T) // tw
    pmax = nb + nwin_total
    jv = jnp.broadcast_to(jnp.arange(nb, dtype=I32)[:, None], win.shape).reshape(-1)
    wv = jnp.minimum(win, nwin_total - 1).reshape(-1)
    total = jnp.sum(valid.astype(I32))
    dst = jnp.where(valid, jnp.cumsum(valid.astype(I32)) - 1, pmax)
    pj, pw = (jnp.zeros((pmax,), I32).at[dst].set(a, mode="drop") for a in (jv, wv))
    real = jnp.arange(pmax, dtype=I32) < total
    pj, pw = (jnp.where(real, a, a[total - 1]) for a in (pj, pw))
    first = jnp.concatenate([jnp.ones((1,), bool), pj[1:] != pj[:-1]])
    last = jnp.concatenate([pj[1:] != pj[:-1], jnp.ones((1,), bool)]) | (jnp.arange(pmax, dtype=I32) == total - 1)
    flag = jnp.where(real, first.astype(I32) + 2 * last.astype(I32) + 4, 0)
    return off, (pj, pw, flag, lo[pj], hi[pj])


SC_LANES = 16
SC_CORES = 2
SC_SUBCORES = 16
SC_GATHER_ROWS = 32


def _dispatch(pos, xw, cap):
    E, T = pos.shape
    W = xw.shape[1]
    G = SC_GATHER_ROWS
    part_rows = cap // SC_CORES
    mesh = plsc.VectorSubcoreMesh(core_axis_name="c", subcore_axis_name="s")

    @pl.kernel(
        out_type=jax.ShapeDtypeStruct((E * cap, W), I32),
        mesh=mesh,
        scratch_types=[pltpu.VMEM((T,), I32), pltpu.VMEM((cap,), I32), pltpu.VMEM((2, G, W), I32),
                       pltpu.SemaphoreType.DMA((2,))],
        compiler_params=pltpu.CompilerParams(needs_layout_passes=False),
        name="sc_dispatch",
    )
    def run(pos_hbm, x_hbm, xe_hbm, pos_v, idx_v, buf, sems):
        e = lax.axis_index("s")
        part = lax.axis_index("c")
        pltpu.sync_copy(pos_hbm.at[e], pos_v)
        lane = lax.iota(I32, SC_LANES)

        @pl.loop(0, T // SC_LANES)
        def _(i):
            off = pl.multiple_of(i * SC_LANES, SC_LANES)
            p = pos_v[pl.ds(off, SC_LANES)]
            plsc.store_scatter(idx_v, [p], lane + off, mask=p >= 0)

        n_win = part_rows // G
        assert n_win % 2 == 0

        def window(g):
            return pl.multiple_of(part * part_rows + g * G, G)

        def gather(g, slot):
            return pltpu.make_async_copy(x_hbm.at[idx_v.at[pl.ds(window(g), G)]], buf.at[slot], sems.at[slot])

        gather(0, 0).start()

        @pl.loop(0, n_win // 2)
        def _(i):
            g = 2 * i
            gather(g, 0).wait()
            gather(g + 1, 1).start()
            pltpu.sync_copy(buf.at[0], xe_hbm.at[pl.ds(e * cap + window(g), G)])
            gather(g + 1, 1).wait()

            @pl.when(g + 2 < n_win)
            def _():
                gather(g + 2, 0).start()

            pltpu.sync_copy(buf.at[1], xe_hbm.at[pl.ds(e * cap + window(g + 1), G)])

    return run(pos, xw)


FFN_TM = 2048
FFN_UNPACK_ROWS = 256


def _ffn_tile_width(cap):
    return 512 if min(FFN_TM, cap) <= 1024 else 256


def _ffn_kernel(x_ref, wg_ref, wu_ref, wd_ref, o_ref, xb_ref, hid_ref, *, tw):
    s = pl.program_id(2)
    n_up = D_FF // tw

    @pl.when(s == 0)
    def _():
        half = D_MODEL // 2

        def unpack_rows(i, carry):
            r = pl.multiple_of(i * FFN_UNPACK_ROWS, FFN_UNPACK_ROWS)
            lo, hi = _unpack_bf16_pairs(x_ref[pl.ds(r, FFN_UNPACK_ROWS), :])
            xb_ref[pl.ds(r, FFN_UNPACK_ROWS), :half] = lo.astype(BF16)
            xb_ref[pl.ds(r, FFN_UNPACK_ROWS), half:] = hi.astype(BF16)
            return carry

        lax.fori_loop(0, x_ref.shape[0] // FFN_UNPACK_ROWS, unpack_rows, 0)

    @pl.when(s < n_up)
    def _():
        x = xb_ref[...]
        g = _dot(x, wg_ref[...].astype(BF16))
        u = _dot(x, wu_ref[...].astype(BF16))
        col = pl.multiple_of(s * tw, tw)
        hid_ref[:, pl.ds(col, tw)] = (_silu(g) * u).astype(BF16)

    @pl.when(s >= n_up)
    def _():
        o_ref[...] = _pack_bf16_pairs(_dot(hid_ref[...], wd_ref[...].astype(BF16)))


def _ffn(xe, w_gate, w_up, w_down):
    E, cap, _ = xe.shape
    tm = min(FFN_TM, cap)
    tw = _ffn_tile_width(cap)
    n_up, n_down = D_FF // tw, D_MODEL // tw

    def up(e, m, s):
        return (e, 0, jnp.minimum(s, n_up - 1))

    def down(s):
        return jnp.maximum(s - n_up, 0)

    return pl.pallas_call(
        functools.partial(_ffn_kernel, tw=tw),
        grid=(E, cap // tm, n_up + n_down),
        in_specs=[
            pl.BlockSpec((None, tm, D_MODEL // 2), lambda e, m, s: (e, m, 0)),
            pl.BlockSpec((None, D_MODEL, tw), up),
            pl.BlockSpec((None, D_MODEL, tw), up),
            pl.BlockSpec((None, D_FF, tw), lambda e, m, s: (e, 0, down(s))),
        ],
        out_specs=pl.BlockSpec((None, tm, tw // 2), lambda e, m, s: (e, m, down(s))),
        out_shape=jax.ShapeDtypeStruct((E, cap, D_MODEL // 2), I32),
        scratch_shapes=[pltpu.VMEM((tm, D_MODEL), BF16), pltpu.VMEM((tm, D_FF), BF16)],
        compiler_params=_cparams(("parallel", "parallel", "arbitrary"), 60),
        name="ffn",
    )(xe, w_gate, w_up, w_down)


def _regroup(posb, rankb, affb, off, yw, cap, rows):
    NB, EB = posb.shape
    W = yw.shape[1]
    G, L, tt = SC_GATHER_ROWS, SC_LANES, MOE_TT
    n_workers = SC_CORES * SC_SUBCORES
    per = NB // n_workers
    mesh = plsc.VectorSubcoreMesh(core_axis_name="c", subcore_axis_name="s")
    off_pad = jnp.pad(off, (0, L))

    @pl.kernel(
        out_type=(jax.ShapeDtypeStruct((rows, W), I32), jax.ShapeDtypeStruct((rows,), I32),
                  jax.ShapeDtypeStruct((rows,), F32)),
        mesh=mesh,
        scratch_types=[pltpu.VMEM((EB,), I32), pltpu.VMEM((EB,), I32), pltpu.VMEM((EB,), F32),
                       pltpu.VMEM((EB,), I32), pltpu.VMEM((EB,), I32), pltpu.VMEM((EB,), F32),
                       pltpu.VMEM((2, G, W), I32), pltpu.VMEM((NB + 1 + L,), I32),
                       pltpu.SemaphoreType.DMA((4,))],
        compiler_params=pltpu.CompilerParams(needs_layout_passes=False),
        name="sc_regroup",
    )
    def run(posb_hbm, rankb_hbm, affb_hbm, off_hbm, y_hbm, yg_hbm, tok_hbm, gate_hbm,
            pos_v, rank_v, aff_v, src_v, tok_v, gate_v, buf, off_v, sems):
        wid = lax.axis_index("c") * SC_SUBCORES + lax.axis_index("s")
        pltpu.sync_copy(off_hbm, off_v)
        lane = lax.iota(I32, L)
        zi = jnp.zeros((L,), I32)
        zf = jnp.zeros((L,), F32)

        @pl.loop(0, per)
        def _(k):
            j = wid * per + k
            pltpu.sync_copy(posb_hbm.at[j], pos_v)
            pltpu.sync_copy(rankb_hbm.at[j], rank_v)
            pltpu.sync_copy(affb_hbm.at[j], aff_v)
            lo = jnp.max(plsc.load_gather(off_v, [zi + j]))
            hi = jnp.max(plsc.load_gather(off_v, [zi + j + 1]))

            @pl.loop(0, EB // L)
            def _(i):
                o = pl.multiple_of(i * L, L)
                src_v[pl.ds(o, L)] = zi
                tok_v[pl.ds(o, L)] = zi
                gate_v[pl.ds(o, L)] = zf

            @pl.loop(0, EB // L)
            def _(i):
                o = pl.multiple_of(i * L, L)
                p = pos_v[pl.ds(o, L)]
                r = rank_v[pl.ds(o, L)]
                m = p >= 0
                e = i // (tt // L)
                t0 = j * tt + (i % (tt // L)) * L
                plsc.store_scatter(src_v, [r], p + e * cap, mask=m)
                plsc.store_scatter(tok_v, [r], lane + t0, mask=m)
                plsc.store_scatter(gate_v, [r], aff_v[pl.ds(o, L)], mask=m)

            n_win = (hi - lo) // G

            def gather(g, slot):
                o = pl.multiple_of(g * G, G)
                return pltpu.make_async_copy(y_hbm.at[src_v.at[pl.ds(o, G)]], buf.at[slot], sems.at[slot])

            def list_writes(g):
                o = pl.multiple_of(g * G, G)
                dst = pl.multiple_of(lo + o, G)
                return (pltpu.make_async_copy(tok_v.at[pl.ds(o, G)], tok_hbm.at[pl.ds(dst, G)], sems.at[2]),
                        pltpu.make_async_copy(gate_v.at[pl.ds(o, G)], gate_hbm.at[pl.ds(dst, G)], sems.at[3]))

            def emit(g, slot):
                for cp in list_writes(g):
                    cp.start()
                dst = pl.multiple_of(lo + g * G, G)
                pltpu.sync_copy(buf.at[slot], yg_hbm.at[pl.ds(dst, G)])

            @pl.when(n_win > 0)
            def _():
                gather(0, 0).start()

            @pl.loop(0, (n_win + 1) // 2)
            def _(i):
                g = 2 * i
                gather(g, 0).wait()

                @pl.when(g + 1 < n_win)
                def _():
                    gather(g + 1, 1).start()

                emit(g, 0)

                @pl.when(g + 1 < n_win)
                def _():
                    gather(g + 1, 1).wait()

                    @pl.when(g + 2 < n_win)
                    def _():
                        gather(g + 2, 0).start()

                    emit(g + 1, 1)

            @pl.loop(0, n_win)
            def _(g):
                for cp in list_writes(g):
                    cp.wait()

    return run(posb, rankb, affb, off_pad, yw)


def _combine_kernel(pj_ref, pw_ref, pf_ref, plo_ref, phi_ref, tok_ref, gate_ref, yg_ref, h_ref, nf_ref,
                    o_ref, acc_ref, *, group):
    p = pl.program_id(0)
    flag = pf_ref[p]
    tw, tt = MOE_TW, MOE_TT
    half = D_MODEL // 2

    hw = group // 2

    def col_blocks():
        for n in range(D_MODEL // group):
            yield slice(n * hw, (n + 1) * hw), slice(n * group, n * group + hw)
            yield slice(half + n * hw, half + (n + 1) * hw), slice(n * group + hw, (n + 1) * group)

    @pl.when((flag & 1) != 0)
    def _():
        for packed, natural in col_blocks():
            acc_ref[:, packed] = h_ref[:, natural]

    @pl.when((flag & 4) != 0)
    def _():
        lo, hi = plo_ref[p], phi_ref[p]
        row0 = pw_ref[p] * tw
        rid = lax.broadcasted_iota(I32, (tw, 1), 0) + row0
        keep = (rid >= lo) & (rid < hi)
        y_lo, y_hi = _unpack_bf16_pairs(yg_ref[...])
        y_lo = jnp.where(keep, y_lo, 0.0).astype(BF16)
        y_hi = jnp.where(keep, y_hi, 0.0).astype(BF16)
        tid = lax.broadcasted_iota(I32, (tt, tw), 0) + pj_ref[p] * tt
        cid = lax.broadcasted_iota(I32, (tt, tw), 1) + row0
        hit = (tok_ref[...] == tid) & (cid >= lo) & (cid < hi)
        weights = jnp.where(hit, gate_ref[...], 0.0).astype(BF16)
        acc_ref[:, :half] += _dot(weights, y_lo)
        acc_ref[:, half:] += _dot(weights, y_hi)

    @pl.when((flag & 2) != 0)
    def _():
        y = acc_ref[...]
        scale = lax.rsqrt(jnp.mean(y * y, axis=-1, keepdims=True) + EPS)
        for packed, natural in col_blocks():
            o_ref[:, natural] = acc_ref[:, packed] * scale * nf_ref[:, natural]


def _combine(lists, tok, gate, yg, h, nfw, group):
    pj, pw, pf, plo, phi = lists
    T = h.shape[0]
    tw, tt = MOE_TW, MOE_TT
    nwin = yg.shape[0] // tw
    grid_spec = pltpu.PrefetchScalarGridSpec(
        num_scalar_prefetch=5,
        grid=(pj.shape[0],),
        in_specs=[
            pl.BlockSpec((None, 1, tw), lambda p, pj, pw, *_: (pw[p], 0, 0)),
            pl.BlockSpec((None, 1, tw), lambda p, pj, pw, *_: (pw[p], 0, 0)),
            pl.BlockSpec((tw, D_MODEL // 2), lambda p, pj, pw, *_: (pw[p], 0)),
            pl.BlockSpec((tt, D_MODEL), lambda p, pj, pw, *_: (pj[p], 0)),
            pl.BlockSpec((1, D_MODEL), lambda p, pj, pw, *_: (0, 0)),
        ],
        out_specs=pl.BlockSpec((tt, D_MODEL), lambda p, pj, pw, *_: (pj[p], 0)),
        scratch_shapes=[pltpu.VMEM((tt, D_MODEL), F32)],
    )
    return pl.pallas_call(
        functools.partial(_combine_kernel, group=group),
        grid_spec=grid_spec,
        out_shape=jax.ShapeDtypeStruct((T, D_MODEL), F32),
        compiler_params=_cparams(("arbitrary",), 32),
        name="combine",
    )(pj, pw, pf, plo, phi, tok.reshape(nwin, 1, tw), gate.reshape(nwin, 1, tw), yg, h, nfw)


def _rope_tables(seq_len):
    d = RET_DK
    inv = ROPE_BASE ** (-jnp.arange(0, d, 2, dtype=F32) / d)
    ang = jnp.arange(seq_len, dtype=F32)[:, None] * inv[None, :]
    return jnp.cos(ang), jnp.sin(ang)


def _chunk_tri(n, chunk, upper):
    r = np.arange(n)
    same = (r[:, None] // chunk) == (r[None, :] // chunk)
    tri = (r[:, None] <= r[None, :]) if upper else (r[:, None] >= r[None, :])
    return jnp.asarray(same & tri, BF16)


def _prep_params(norm1_w, w_in, ret_gn_w, gla_gate_up, gla_gate_bias, gla_gn_w, w_out, norm2_w, router_w,
                 normf_w):
    w = w_in[0]
    w_main = w[:, :IN_MAIN].astype(BF16)
    w_ga = jnp.pad(w[:, IN_MAIN:], ((0, 0), (0, LANE - 2 * GLA_RANK))).astype(BF16)
    cs = np.ones((1, IN_MAIN), np.float32)
    cs[:, _RQ:_RQ + RET_WIDTH] = RET_DK ** -0.5
    cs[:, _GQ:_GQ + GLA_KEY_WIDTH] = GLA_DK ** -0.5
    up = gla_gate_up[0].astype(F32)
    up_pad = jnp.zeros((LANE, 2 * GLA_KEY_WIDTH), F32)
    up_pad = up_pad.at[:GLA_RANK, :GLA_KEY_WIDTH].set(up[0])
    up_pad = up_pad.at[GLA_RANK:2 * GLA_RANK, GLA_KEY_WIDTH:].set(up[1])
    rt = router_w[0].T.astype(F32)
    r_hi = rt.astype(BF16)
    r_lo = (rt - r_hi.astype(F32)).astype(BF16)
    return dict(
        n1w=norm1_w[0].reshape(1, D_MODEL).astype(F32),
        w_main=w_main, w_ga=w_ga, colscale=jnp.asarray(cs),
        up_pad=up_pad.astype(BF16),
        bias=gla_gate_bias[0].reshape(1, 2 * GLA_KEY_WIDTH).astype(F32),
        lf=_chunk_tri(GATE_TM, GLA_CHUNK, upper=False),
        lb=_chunk_tri(GATE_TM, GLA_CHUNK, upper=True),
        ret_gn=ret_gn_w[0].reshape(1, RET_WIDTH).astype(F32),
        gla_gn=gla_gn_w[0].reshape(1, GLA_WIDTH).astype(F32),
        w_out=w_out[0].astype(BF16),
        n2w=norm2_w[0].reshape(1, D_MODEL).astype(F32),
        r_hi=r_hi, r_lo=r_lo,
        nfw=normf_w.reshape(1, D_MODEL).astype(F32),
    )


def _trunk(x, pp, decay_logit, w_gate, w_up, w_down):
    B, L, _ = x.shape
    T = B * L
    x2d = x.reshape(T, D_MODEL)
    cos, sin = _rope_tables(L)
    proj, ga = _in_proj(x2d, pp["n1w"], pp["w_main"], pp["w_ga"], pp["colscale"], cos, sin, L)
    b_f, b_b = _gla_gates(ga, pp["up_pad"], pp["bias"], pp["lf"], pp["lb"])

    fwd = _mixer_scan(proj, decay_logit, b_f, B, L, reverse=False)
    mix_r, mix_g = _mixer_scan(proj, decay_logit, b_b, B, L, reverse=True, o_fwd=fwd,
                               gn_w=(pp["ret_gn"], pp["gla_gn"]))

    h, xn2, aff = _out_proj(mix_r, mix_g, pp["w_out"], x2d, pp["n2w"], pp["r_hi"], pp["r_lo"])

    cap = CAPACITY_FACTOR * T // N_EXPERTS
    pos, posb, rankb, affb = _select(aff, cap)
    off, c_lists = _combine_schedule(posb, T)
    xe = _dispatch(pos, xn2, cap).reshape(N_EXPERTS, cap, D_MODEL // 2)
    ye = _ffn(xe, w_gate, w_up, w_down).reshape(N_EXPERTS * cap, D_MODEL // 2)
    nb = T // MOE_TT
    yg, tok, gate = _regroup(posb.reshape(nb, -1), rankb.reshape(nb, -1), affb.reshape(nb, -1), off, ye, cap,
                             _regroup_rows(T))
    y = _combine(c_lists, tok, gate, yg, h, pp["nfw"], _ffn_tile_width(cap))
    return y.reshape(B, L, D_MODEL)


def kernel(x_prompt, x_sample, norm1_w, w_in, ret_decay_logit, ret_gn_w, gla_gate_up, gla_gate_bias,
           gla_gn_w, w_out, norm2_w, router_w, w_gate, w_up, w_down, normf_w):
    pp = _prep_params(norm1_w, w_in, ret_gn_w, gla_gate_up, gla_gate_bias, gla_gn_w, w_out, norm2_w,
                      router_w, normf_w)
    decay_logit = ret_decay_logit[0].astype(F32)
    args = (pp, decay_logit, w_gate[0], w_up[0], w_down[0])
    return (_trunk(x_prompt, *args), _trunk(x_sample, *args))
```

```python
import functools

import numpy as np
import jax
import jax.numpy as jnp
from jax import lax
from jax.experimental import pallas as pl
from jax.experimental.pallas import tpu as pltpu
from jax.experimental.pallas import tpu_sc as plsc

F32, BF16, I32 = jnp.float32, jnp.bfloat16, jnp.int32

D_MODEL = 2048
RET_WIDTH = 1024
RET_HEADS = 4
RET_DK = 256
RET_DV = 256
GLA_WIDTH = 1024
GLA_HEADS = 4
GLA_DK = 128
GLA_DV = 256
GLA_KEY_WIDTH = 512
GLA_RANK = 16
GLA_TAU = 16.0
RET_CHUNK = 256
GLA_CHUNK = 64
GLA_SUB = 16
ROPE_BASE = 10000.0
N_EXPERTS = 16
CAPACITY_FACTOR = 2
D_FF = 2048
EPS = 1e-6
LOG2_E = 1.4426950408889634
IN_MAIN = 4 * RET_WIDTH + 2 * GLA_KEY_WIDTH + 2 * GLA_WIDTH

_RQ, _RK, _RV, _RG = 0, 1024, 2048, 3072
_GQ, _GK, _GV, _GG = 4096, 4608, 5120, 6144

LANE = 128
MOE_TT = 256
MOE_TW = 512
V7X_VMEM_BYTES = 64 * 1024 * 1024


def _cparams(sem, vmem_mb):
    return pltpu.CompilerParams(dimension_semantics=sem, vmem_limit_bytes=vmem_mb * 1024 * 1024)


def _log_sigmoid(z):
    return jnp.minimum(z, 0.0) - jnp.log1p(jnp.exp(-jnp.abs(z)))


def _silu(g):
    return g * (1.0 / (1.0 + jnp.exp(-g)))


def _dot_nt(a, b):
    return lax.dot_general(a, b, (((1,), (1,)), ((), ())), preferred_element_type=F32)


def _dot_tn(a, b):
    return lax.dot_general(a, b, (((0,), (0,)), ((), ())), preferred_element_type=F32)


def _dot(a, b):
    return jnp.dot(a, b, preferred_element_type=F32)


def _pack_bf16_pairs(x):
    bits = pltpu.bitcast(x.astype(BF16).astype(F32), I32)
    w = x.shape[1] // 2
    return bits[:, w:] | lax.shift_right_logical(bits[:, :w], 16)


def _unpack_bf16_pairs(words):
    lo = pltpu.bitcast(lax.shift_left(words, 16), F32)
    hi = pltpu.bitcast(words & jnp.int32(-65536), F32)
    return lo, hi


IP_TM = 1024
IP_TN = 1024


def _in_proj_kernel(x_ref, n1_ref, w_ref, wga_ref, cs_ref, cos_ref, sin_ref, o_ref, ga_ref, xn_ref):
    j = pl.program_id(1)

    @pl.when(j == 0)
    def _():
        x = x_ref[...]
        ms = jnp.mean(x * x, axis=-1, keepdims=True)
        xn = (x * lax.rsqrt(ms + EPS) * n1_ref[...]).astype(BF16)
        xn_ref[...] = xn
        ga_ref[...] = _dot(xn, wga_ref[...])

    acc = _dot(xn_ref[...], w_ref[...]) * cs_ref[...]
    n_rope_blocks = 2 * RET_WIDTH // IP_TN

    @pl.when(j < n_rope_blocks)
    def _():
        cos = cos_ref[...]
        sin = sin_ref[...]
        for h in range(IP_TN // RET_DK):
            x1 = acc[:, 2 * h * LANE:(2 * h + 1) * LANE]
            x2 = acc[:, (2 * h + 1) * LANE:(2 * h + 2) * LANE]
            o_ref[2 * h] = (x1 * cos - x2 * sin).astype(BF16)
            o_ref[2 * h + 1] = (x1 * sin + x2 * cos).astype(BF16)

    @pl.when(j >= n_rope_blocks)
    def _():
        for c in range(IP_TN // LANE):
            o_ref[c] = acc[:, c * LANE:(c + 1) * LANE].astype(BF16)


def _in_proj(x2d, n1w, w_main, w_ga, colscale, cos, sin, seq_len):
    T = x2d.shape[0]
    tm, tn = IP_TM, IP_TN
    nlb = seq_len // tm
    return pl.pallas_call(
        _in_proj_kernel,
        grid=(T // tm, IN_MAIN // tn),
        in_specs=[
            pl.BlockSpec((tm, D_MODEL), lambda i, j: (i, 0)),
            pl.BlockSpec((1, D_MODEL), lambda i, j: (0, 0)),
            pl.BlockSpec((D_MODEL, tn), lambda i, j: (0, j)),
            pl.BlockSpec((D_MODEL, LANE), lambda i, j: (0, 0)),
            pl.BlockSpec((1, tn), lambda i, j: (0, j)),
            pl.BlockSpec((tm, LANE), lambda i, j: (i % nlb, 0)),
            pl.BlockSpec((tm, LANE), lambda i, j: (i % nlb, 0)),
        ],
        out_specs=[
            pl.BlockSpec((tn // LANE, tm, LANE), lambda i, j: (j, i, 0)),
            pl.BlockSpec((tm, LANE), lambda i, j: (i, 0)),
        ],
        out_shape=[
            jax.ShapeDtypeStruct((IN_MAIN // LANE, T, LANE), BF16),
            jax.ShapeDtypeStruct((T, LANE), F32),
        ],
        scratch_shapes=[pltpu.VMEM((tm, D_MODEL), BF16)],
        compiler_params=_cparams(("parallel", "arbitrary"), 48),
        name="in_proj",
    )(x2d, n1w, w_main, w_ga, colscale, cos, sin)


GATE_TM = 512


def _gates_kernel(ga_ref, up_ref, bias_ref, lf_ref, lb_ref, bf_ref, bb_ref):
    z = _dot(ga_ref[...].astype(BF16), up_ref[...]) + bias_ref[...]
    la = _log_sigmoid(z) * (LOG2_E / GLA_TAU)
    hi = la.astype(BF16)
    lo = (la - hi.astype(F32)).astype(BF16)
    kw = GLA_KEY_WIDTH
    bf_ref[...] = _dot(lf_ref[...], hi[:, :kw]) + _dot(lf_ref[...], lo[:, :kw])
    bb_ref[...] = _dot(lb_ref[...], hi[:, kw:]) + _dot(lb_ref[...], lo[:, kw:])


def _gla_gates(ga, up_pad, bias, lf, lb):
    T = ga.shape[0]
    tm = GATE_TM
    kw = GLA_KEY_WIDTH
    return pl.pallas_call(
        _gates_kernel,
        grid=(T // tm,),
        in_specs=[
            pl.BlockSpec((tm, LANE), lambda i: (i, 0)),
            pl.BlockSpec((LANE, 2 * kw), lambda i: (0, 0)),
            pl.BlockSpec((1, 2 * kw), lambda i: (0, 0)),
            pl.BlockSpec((tm, tm), lambda i: (0, 0)),
            pl.BlockSpec((tm, tm), lambda i: (0, 0)),
        ],
        out_specs=[pl.BlockSpec((tm, kw), lambda i: (i, 0)), pl.BlockSpec((tm, kw), lambda i: (i, 0))],
        out_shape=[jax.ShapeDtypeStruct((T, kw), F32), jax.ShapeDtypeStruct((T, kw), F32)],
        compiler_params=_cparams(("parallel",), 32),
        name="gla_gates",
    )(ga, up_pad, bias, lf, lb)


def _wide(ref, rows):
    return jnp.concatenate([ref[0, rows, :], ref[1, rows, :]], axis=1)


def _finish_heads(tot, gn, gate):
    ms = jnp.mean(tot * tot, axis=-1, keepdims=True)
    yn = tot * lax.rsqrt(ms + EPS) * gn
    return (yn * _silu(gate.astype(F32))).astype(BF16)


RET_TB = 1024


def _ret_kernel(dl_ref, q_ref, k_ref, v_ref, *rest, reverse):
    if reverse:
        g_ref, of_ref, gn_ref, o_ref, s_ref, intra_ref, qd_ref, kd_ref, cd_ref, p_ref, u_ref = rest
    else:
        o_ref, s_ref, intra_ref, qd_ref, kd_ref, cd_ref, p_ref, u_ref = rest
    h = pl.program_id(1)
    n = pl.program_id(2)
    C = RET_CHUNK

    @pl.when(n == 0)
    def _():
        s_ref[...] = jnp.zeros_like(s_ref)
        logit = dl_ref[1 if reverse else 0, h]
        lg = _log_sigmoid(jnp.full((C, RET_DV), logit, F32))
        lg_c = _log_sigmoid(jnp.full((C, C), logit, F32))
        lg_r = _log_sigmoid(jnp.full((1, RET_DV), logit, F32))
        ri = lax.broadcasted_iota(I32, (C, RET_DV), 0).astype(F32)
        rc = lax.broadcasted_iota(I32, (C, C), 0).astype(F32)
        cc = lax.broadcasted_iota(I32, (C, C), 1).astype(F32)
        diff = (cc - rc) if reverse else (rc - cc)
        intra_ref[...] = jnp.where(diff >= 0, jnp.exp(lg_c * diff), 0.0)
        if reverse:
            qd_ref[...] = jnp.exp(lg * (C - ri))
            kd_ref[...] = jnp.exp(lg * ri)
        else:
            qd_ref[...] = jnp.exp(lg * (ri + 1.0))
            kd_ref[...] = jnp.exp(lg * (C - 1.0 - ri))
        cd_ref[...] = jnp.exp(lg_r * C)

    nchunks = o_ref.shape[0] // C
    for c in range(nchunks):
        rows = slice(c * C, (c + 1) * C)
        k = _wide(k_ref, rows)
        p_ref[c] = (_dot_nt(_wide(q_ref, rows), k) * intra_ref[...]).astype(BF16)
        kd = (k.astype(F32) * kd_ref[...]).astype(BF16)
        u_ref[c] = _dot_tn(kd, _wide(v_ref, rows))
    order = range(nchunks - 1, -1, -1) if reverse else range(nchunks)
    for c in order:
        rows = slice(c * C, (c + 1) * C)
        state = s_ref[...]
        o = _dot(p_ref[c], _wide(v_ref, rows)) + _dot(_wide(q_ref, rows), state.astype(BF16)) * qd_ref[...]
        s_ref[...] = state * cd_ref[...] + u_ref[c]
        if reverse:
            tot = of_ref[rows, :].astype(F32) + o
            o_ref[rows, :] = _finish_heads(tot, gn_ref[...], _wide(g_ref, rows))
        else:
            o_ref[rows, :] = o.astype(BF16)


def _ret_parts(proj, decay_logit, rb, reverse, o_fwd=None, gn_w=None):
    T = proj.shape[1]
    tb = RET_TB
    dk, dv, C = RET_DK, RET_DV, RET_CHUNK

    def head(base):
        return pl.BlockSpec((dk // LANE, tb, LANE), lambda b, h, n: (base // dk + h, rb(b, n), 0))

    in_specs = [pl.BlockSpec(memory_space=pltpu.SMEM), head(_RQ), head(_RK), head(_RV)]
    args = [decay_logit, proj, proj, proj]
    if reverse:
        in_specs += [
            head(_RG),
            pl.BlockSpec((tb, dv), lambda b, h, n: (rb(b, n), h)),
            pl.BlockSpec((1, dv), lambda b, h, n: (0, h)),
        ]
        args += [proj, o_fwd, gn_w]
    out_spec = pl.BlockSpec((tb, dv), lambda b, h, n: (rb(b, n), h))
    out_shape = jax.ShapeDtypeStruct((T, RET_WIDTH), BF16)
    scratch = [
        pltpu.VMEM((dk, dv), F32),
        pltpu.VMEM((C, C), F32),
        pltpu.VMEM((C, dv), F32),
        pltpu.VMEM((C, dk), F32),
        pltpu.VMEM((1, dv), F32),
        pltpu.VMEM((tb // C, C, C), BF16),
        pltpu.VMEM((tb // C, dk, dv), F32),
    ]
    return in_specs, args, out_spec, out_shape, scratch


GLA_TB = 1024
GLA_UNROLL = 16


GLA_LEVELS = (32, 16, 8, 4, 2, 1)
SUBLANES = 8


def _gla_tables(reverse):
    C = GLA_CHUNK
    r = np.arange(C)
    masks = np.zeros((len(GLA_LEVELS) + 1, C, C), np.float32)
    for l, s in enumerate(GLA_LEVELS):
        upper = (r & s) != 0
        same = (r[:, None] // (2 * s)) == (r[None, :] // (2 * s))
        lhs_rows = ~upper if reverse else upper
        masks[l] = same & lhs_rows[:, None] & ~lhs_rows[None, :]
    masks[-1] = np.eye(C)
    return jnp.asarray(masks, F32)


def _gla_kernel(q_ref, k_ref, v_ref, b_ref, mask_ref, *rest, reverse):
    if reverse:
        g_ref, of_ref, gn_ref, o_ref, st_ref, sc_ref = rest
    else:
        o_ref, st_ref, sc_ref = rest
    n = pl.program_id(2)
    C = GLA_CHUNK

    @pl.when(n == 0)
    def _():
        st_ref[...] = jnp.zeros_like(st_ref)

    nchunks = q_ref.shape[0] // C
    sub_row = lax.broadcasted_iota(I32, (SUBLANES, GLA_DK), 0)
    zero_rows = jnp.zeros((SUBLANES, GLA_DK), F32)

    def chunk_scores(c, carry):
        c0 = pl.multiple_of(c * C, C)
        qb = q_ref[pl.ds(c0, C), :]
        kb = k_ref[pl.ds(c0, C), :]
        q = qb.astype(F32)
        k = kb.astype(F32)
        b = b_ref[pl.ds(c0, C), :]

        def mid_row(r):
            return jnp.broadcast_to(b[r:r + 1, :], (SUBLANES, GLA_DK))

        scores = mask_ref[len(GLA_LEVELS)] * _dot_nt(qb, kb)
        for l, s in enumerate(GLA_LEVELS):
            lhs, rhs = [], []
            for g in range(C // SUBLANES):
                r0 = g * SUBLANES
                rows = slice(r0, r0 + SUBLANES)
                if s >= SUBLANES:
                    m = mid_row((r0 // (2 * s)) * (2 * s) + s)
                    is_lhs = ((r0 & s) != 0) != reverse
                    if is_lhs:
                        lhs.append(q[rows] * jnp.exp2(b[rows] - m))
                        rhs.append(zero_rows)
                    else:
                        lhs.append(zero_rows)
                        rhs.append(k[rows] * jnp.exp2(m - b[rows]))
                else:
                    m = mid_row(r0 + SUBLANES - s)
                    for blk in range(SUBLANES // (2 * s) - 2, -1, -1):
                        m = jnp.where(sub_row < (blk + 1) * 2 * s, mid_row(r0 + blk * 2 * s + s), m)
                    upper = (sub_row & s) != 0
                    is_lhs = jnp.logical_not(upper) if reverse else upper
                    lhs.append(jnp.where(is_lhs, q[rows] * jnp.exp2(b[rows] - m), 0.0))
                    rhs.append(jnp.where(is_lhs, 0.0, k[rows] * jnp.exp2(m - b[rows])))
            lhs = jnp.concatenate(lhs, axis=0).astype(BF16)
            rhs = jnp.concatenate(rhs, axis=0).astype(BF16)
            scores = scores + mask_ref[l] * _dot_nt(lhs, rhs)
        sc_ref[c] = scores.astype(BF16)
        return carry

    lax.fori_loop(0, nchunks, chunk_scores, 0, unroll=GLA_UNROLL)

    def chunk(ci, carry):
        c = (nchunks - 1 - ci) if reverse else ci
        c0 = pl.multiple_of(c * C, C)
        q = q_ref[pl.ds(c0, C), :].astype(F32)
        k = k_ref[pl.ds(c0, C), :].astype(F32)
        v = _wide(v_ref, pl.ds(c0, C))
        b = b_ref[pl.ds(c0, C), :]
        b_end = b[0:1, :] if reverse else b[C - 1:C, :]

        st = st_ref[...]
        o = _dot_nt((q * jnp.exp2(b)).astype(BF16), st.astype(BF16))
        ke = (k * jnp.exp2(b_end - b)).astype(BF16)
        st_ref[...] = st * jnp.exp2(b_end) + _dot_tn(v, ke)
        o = o + _dot(sc_ref[c], v)
        if reverse:
            tot = of_ref[pl.ds(c0, C), :].astype(F32) + o
            o_ref[pl.ds(c0, C), :] = _finish_heads(tot, gn_ref[...], _wide(g_ref, pl.ds(c0, C)))
        else:
            o_ref[pl.ds(c0, C), :] = o.astype(BF16)
        return carry

    lax.fori_loop(0, nchunks, chunk, 0, unroll=GLA_UNROLL)


def _gla_parts(proj, bcum, rb, reverse, o_fwd=None, gn_w=None):
    T = proj.shape[1]
    tb = GLA_TB
    dk, dv = GLA_DK, GLA_DV
    masks = _gla_tables(reverse)

    def key_block(base):
        return pl.BlockSpec((None, tb, LANE), lambda b, h, n: (base // dk + h, rb(b, n), 0))

    def value_block(base):
        return pl.BlockSpec((dv // LANE, tb, LANE), lambda b, h, n: (base // dv + h, rb(b, n), 0))

    in_specs = [
        key_block(_GQ), key_block(_GK), value_block(_GV),
        pl.BlockSpec((tb, dk), lambda b, h, n: (rb(b, n), h)),
        pl.BlockSpec(masks.shape, lambda b, h, n: (0, 0, 0)),
    ]
    args = [proj, proj, proj, bcum, masks]
    if reverse:
        in_specs += [
            value_block(_GG),
            pl.BlockSpec((tb, dv), lambda b, h, n: (rb(b, n), h)),
            pl.BlockSpec((1, dv), lambda b, h, n: (0, h)),
        ]
        args += [proj, o_fwd, gn_w]
    out_spec = pl.BlockSpec((tb, dv), lambda b, h, n: (rb(b, n), h))
    out_shape = jax.ShapeDtypeStruct((T, GLA_WIDTH), BF16)
    scratch = [pltpu.VMEM((dv, dk), F32), pltpu.VMEM((tb // GLA_CHUNK, GLA_CHUNK, GLA_CHUNK), BF16)]
    return in_specs, args, out_spec, out_shape, scratch


def _mixer_kernel(*refs, reverse, n_ret_in, n_gla_in, n_ret_scratch):
    ret_in = refs[:n_ret_in]
    gla_in = refs[n_ret_in:n_ret_in + n_gla_in]
    ret_out, gla_out = refs[n_ret_in + n_gla_in:n_ret_in + n_gla_in + 2]
    scratch = refs[n_ret_in + n_gla_in + 2:]
    _ret_kernel(*ret_in, ret_out, *scratch[:n_ret_scratch], reverse=reverse)
    _gla_kernel(*gla_in, gla_out, *scratch[n_ret_scratch:], reverse=reverse)


def _mixer_scan(proj, decay_logit, bcum, batch, seq_len, reverse, o_fwd=(None, None), gn_w=(None, None)):
    assert RET_TB == GLA_TB and RET_HEADS == GLA_HEADS
    nb = seq_len // RET_TB

    def rb(b, n):
        return b * nb + ((nb - 1 - n) if reverse else n)

    r_specs, r_args, r_out, r_shape, r_scratch = _ret_parts(proj, decay_logit, rb, reverse, o_fwd[0], gn_w[0])
    g_specs, g_args, g_out, g_shape, g_scratch = _gla_parts(proj, bcum, rb, reverse, o_fwd[1], gn_w[1])
    return pl.pallas_call(
        functools.partial(_mixer_kernel, reverse=reverse, n_ret_in=len(r_specs), n_gla_in=len(g_specs),
                          n_ret_scratch=len(r_scratch)),
        grid=(batch, RET_HEADS, nb),
        in_specs=r_specs + g_specs,
        out_specs=[r_out, g_out],
        out_shape=[r_shape, g_shape],
        scratch_shapes=r_scratch + g_scratch,
        compiler_params=_cparams(("parallel", "parallel", "arbitrary"), 48),
        name="mixer_bwd" if reverse else "mixer_fwd",
    )(*r_args, *g_args)


OP_TM = 512


def _out_proj_kernel(mr_ref, mg_ref, w0_ref, w1_ref, x_ref, n2_ref, rh_ref, rl_ref, h_ref, xn_ref, aff_ref,
                     hs_ref):
    s = pl.program_id(0)
    slot = s % 2

    @pl.when(s == 0)
    def _():
        hs_ref[1] = jnp.zeros(hs_ref.shape[1:], F32)

    hp = hs_ref[1 - slot]
    ms = jnp.mean(hp * hp, axis=-1, keepdims=True)
    xn = hp * lax.rsqrt(ms + EPS) * n2_ref[...]
    xh = xn.astype(BF16)
    xn_ref[...] = _pack_bf16_pairs(xn)
    xl = (xn - xh.astype(F32)).astype(BF16)
    lt = _dot_nt(rh_ref[...], xh) + _dot_nt(rh_ref[...], xl) + _dot_nt(rl_ref[...], xh)
    m = jnp.max(lt, axis=0, keepdims=True)
    e = jnp.exp(lt - m)
    aff_ref[...] = e / jnp.sum(e, axis=0, keepdims=True)

    h = x_ref[...] + _dot(mr_ref[...], w0_ref[...]) + _dot(mg_ref[...], w1_ref[...])
    h_ref[...] = h
    hs_ref[slot] = h


def _out_proj(mix_r, mix_g, w_out, x2d, n2w, r_hi, r_lo):
    T = x2d.shape[0]
    tm = OP_TM
    half = RET_WIDTH
    nblk = T // tm

    def head(s):
        return jnp.minimum(s, nblk - 1)

    def tail(s):
        return jnp.maximum(s - 1, 0)

    return pl.pallas_call(
        _out_proj_kernel,
        grid=(nblk + 1,),
        in_specs=[
            pl.BlockSpec((tm, half), lambda s: (head(s), 0)),
            pl.BlockSpec((tm, half), lambda s: (head(s), 0)),
            pl.BlockSpec((half, D_MODEL), lambda s: (0, 0)),
            pl.BlockSpec((half, D_MODEL), lambda s: (1, 0)),
            pl.BlockSpec((tm, D_MODEL), lambda s: (head(s), 0)),
            pl.BlockSpec((1, D_MODEL), lambda s: (0, 0)),
            pl.BlockSpec((N_EXPERTS, D_MODEL), lambda s: (0, 0)),
            pl.BlockSpec((N_EXPERTS, D_MODEL), lambda s: (0, 0)),
        ],
        out_specs=[
            pl.BlockSpec((tm, D_MODEL), lambda s: (head(s), 0)),
            pl.BlockSpec((tm, D_MODEL // 2), lambda s: (tail(s), 0)),
            pl.BlockSpec((N_EXPERTS, tm), lambda s: (0, tail(s))),
        ],
        out_shape=[
            jax.ShapeDtypeStruct((T, D_MODEL), F32),
            jax.ShapeDtypeStruct((T, D_MODEL // 2), I32),
            jax.ShapeDtypeStruct((N_EXPERTS, T), F32),
        ],
        scratch_shapes=[pltpu.VMEM((2, tm, D_MODEL), F32)],
        compiler_params=_cparams(("arbitrary",), 60),
        name="out_proj",
    )(mix_r, mix_g, w_out, w_out, x2d, n2w, r_hi, r_lo)


def _select_kernel(a_ref, pos_ref, posb_ref, rankb_ref, affb_ref, *, cap):
    E, T = a_ref.shape
    tt = MOE_TT

    def count(pred):
        return jnp.sum(pred.astype(F32), axis=1, keepdims=True)

    def bisect(i, tau):
        cand = tau | jnp.left_shift(jnp.int32(1), 30 - i)
        bits = pltpu.bitcast(a_ref[...], I32)
        return jnp.where(count(bits >= cand) >= cap, cand, tau)

    tau = lax.fori_loop(0, 31, bisect, jnp.zeros((E, 1), I32))
    bits_all = pltpu.bitcast(a_ref[...], I32)
    quota = cap - count(bits_all > tau)

    before = (lax.broadcasted_iota(I32, (tt, tt), 0) < lax.broadcasted_iota(I32, (tt, tt), 1)).astype(BF16)
    below = (lax.broadcasted_iota(I32, (E, E), 1) < lax.broadcasted_iota(I32, (E, E), 0)).astype(BF16)

    def block(j, carry):
        c_eq, c_sel = carry
        off = pl.multiple_of(j * tt, tt)
        aff = a_ref[:, pl.ds(off, tt)]
        bits = pltpu.bitcast(aff, I32)
        eq = bits == tau
        eqf = eq.astype(F32)
        rank_eq = _dot(eqf.astype(BF16), before) + c_eq
        sel = (bits > tau) | (eq & (rank_eq < quota))
        self_ = sel.astype(F32)
        selb = self_.astype(BF16)
        slot = _dot(selb, before) + c_sel
        pos = jnp.where(sel, slot, -1.0).astype(I32)
        pos_ref[:, pl.ds(off, tt)] = pos
        per_tok = jnp.broadcast_to(jnp.sum(self_, axis=0, keepdims=True), (E, tt))
        rank = _dot(per_tok.astype(BF16), before) + _dot(below, selb)
        posb_ref[j] = pos
        rankb_ref[j] = jnp.where(sel, rank, -1.0).astype(I32)
        affb_ref[j] = aff
        return (c_eq + jnp.sum(eqf, axis=1, keepdims=True), c_sel + jnp.sum(self_, axis=1, keepdims=True))

    zero = jnp.zeros((E, 1), F32)
    lax.fori_loop(0, T // tt, block, (zero, zero))


def _select(aff, cap):
    E, T = aff.shape
    nb = T // MOE_TT
    blk = jax.ShapeDtypeStruct((nb, E, MOE_TT), I32)
    return pl.pallas_call(
        functools.partial(_select_kernel, cap=cap),
        out_shape=[jax.ShapeDtypeStruct((E, T), I32), blk, blk, jax.ShapeDtypeStruct((nb, E, MOE_TT), F32)],
        compiler_params=pltpu.CompilerParams(vmem_limit_bytes=40 * 1024 * 1024),
        name="select",
    )(aff)


def _regroup_rows(T):
    nb = T // MOE_TT
    rows = CAPACITY_FACTOR * T + SC_GATHER_ROWS * nb
    return -(-rows // MOE_TW) * MOE_TW


def _combine_schedule(posb, T):
    nb = posb.shape[0]
    tw, g = MOE_TW, SC_GATHER_ROWS
    n = jnp.sum((posb >= 0).reshape(nb, -1), axis=1).astype(I32)
    seg = (n + g - 1) // g * g
    hi = jnp.cumsum(seg)
    lo = hi - seg
    off = jnp.concatenate([jnp.zeros((1,), I32), hi])
    nwin_max = (N_EXPERTS * MOE_TT) // tw + 1
    w0 = lo // tw
    w1 = jnp.where(seg > 0, (hi - 1) // tw, w0)
    cand = jnp.arange(nwin_max, dtype=I32)
    win = w0[:, None] + cand[None, :]
    valid = (win <= w1[:, None]).reshape(-1)
    nwin_total = _regroup_rows(T) // tw
    pmax = nb + nwin_total
    jv = jnp.broadcast_to(jnp.arange(nb, dtype=I32)[:, None], win.shape).reshape(-1)
    wv = jnp.minimum(win, nwin_total - 1).reshape(-1)
    total = jnp.sum(valid.astype(I32))
    dst = jnp.where(valid, jnp.cumsum(valid.astype(I32)) - 1, pmax)
    pj, pw = (jnp.zeros((pmax,), I32).at[dst].set(a, mode="drop") for a in (jv, wv))
    real = jnp.arange(pmax, dtype=I32) < total
    pj, pw = (jnp.where(real, a, a[total - 1]) for a in (pj, pw))
    first = jnp.concatenate([jnp.ones((1,), bool), pj[1:] != pj[:-1]])
    last = jnp.concatenate([pj[1:] != pj[:-1], jnp.ones((1,), bool)]) | (jnp.arange(pmax, dtype=I32) == total - 1)
    flag = jnp.where(real, first.astype(I32) + 2 * last.astype(I32) + 4, 0)
    return off, (pj, pw, flag, lo[pj], hi[pj])


SC_LANES = 16
SC_CORES = 2
SC_SUBCORES = 16
SC_GATHER_ROWS = 32


def _dispatch(pos, xw, cap):
    E, T = pos.shape
    W = xw.shape[1]
    G = SC_GATHER_ROWS
    part_rows = cap // SC_CORES
    mesh = plsc.VectorSubcoreMesh(core_axis_name="c", subcore_axis_name="s")

    @pl.kernel(
        out_type=jax.ShapeDtypeStruct((E * cap, W), I32),
        mesh=mesh,
        scratch_types=[pltpu.VMEM((T,), I32), pltpu.VMEM((cap,), I32), pltpu.VMEM((G, W), I32)],
        compiler_params=pltpu.CompilerParams(needs_layout_passes=False),
        name="sc_dispatch",
    )
    def run(pos_hbm, x_hbm, xe_hbm, pos_v, idx_v, buf):
        e = lax.axis_index("s")
        part = lax.axis_index("c")
        pltpu.sync_copy(pos_hbm.at[e], pos_v)
        lane = lax.iota(I32, SC_LANES)

        @pl.loop(0, T // SC_LANES)
        def _(i):
            off = pl.multiple_of(i * SC_LANES, SC_LANES)
            p = pos_v[pl.ds(off, SC_LANES)]
            plsc.store_scatter(idx_v, [p], lane + off, mask=p >= 0)

        @pl.loop(0, part_rows // G)
        def _(g):
            o = pl.multiple_of(part * part_rows + g * G, G)
            pltpu.sync_copy(x_hbm.at[idx_v.at[pl.ds(o, G)]], buf)
            pltpu.sync_copy(buf, xe_hbm.at[pl.ds(e * cap + o, G)])

    return run(pos, xw)


FFN_TM = 2048
FFN_UNPACK_ROWS = 256


def _ffn_tile_width(cap):
    return 512 if min(FFN_TM, cap) <= 1024 else 256


def _ffn_kernel(x_ref, wg_ref, wu_ref, wd_ref, o_ref, xb_ref, hid_ref, *, tw):
    s = pl.program_id(2)
    n_up = D_FF // tw

    @pl.when(s == 0)
    def _():
        half = D_MODEL // 2

        def unpack_rows(i, carry):
            r = pl.multiple_of(i * FFN_UNPACK_ROWS, FFN_UNPACK_ROWS)
            lo, hi = _unpack_bf16_pairs(x_ref[pl.ds(r, FFN_UNPACK_ROWS), :])
            xb_ref[pl.ds(r, FFN_UNPACK_ROWS), :half] = lo.astype(BF16)
            xb_ref[pl.ds(r, FFN_UNPACK_ROWS), half:] = hi.astype(BF16)
            return carry

        lax.fori_loop(0, x_ref.shape[0] // FFN_UNPACK_ROWS, unpack_rows, 0)

    @pl.when(s < n_up)
    def _():
        x = xb_ref[...]
        g = _dot(x, wg_ref[...].astype(BF16))
        u = _dot(x, wu_ref[...].astype(BF16))
        col = pl.multiple_of(s * tw, tw)
        hid_ref[:, pl.ds(col, tw)] = (_silu(g) * u).astype(BF16)

    @pl.when(s >= n_up)
    def _():
        o_ref[...] = _pack_bf16_pairs(_dot(hid_ref[...], wd_ref[...].astype(BF16)))


def _ffn(xe, w_gate, w_up, w_down):
    E, cap, _ = xe.shape
    tm = min(FFN_TM, cap)
    tw = _ffn_tile_width(cap)
    n_up, n_down = D_FF // tw, D_MODEL // tw

    def up(e, m, s):
        return (e, 0, jnp.minimum(s, n_up - 1))

    def down(s):
        return jnp.maximum(s - n_up, 0)

    return pl.pallas_call(
        functools.partial(_ffn_kernel, tw=tw),
        grid=(E, cap // tm, n_up + n_down),
        in_specs=[
            pl.BlockSpec((None, tm, D_MODEL // 2), lambda e, m, s: (e, m, 0)),
            pl.BlockSpec((None, D_MODEL, tw), up),
            pl.BlockSpec((None, D_MODEL, tw), up),
            pl.BlockSpec((None, D_FF, tw), lambda e, m, s: (e, 0, down(s))),
        ],
        out_specs=pl.BlockSpec((None, tm, tw // 2), lambda e, m, s: (e, m, down(s))),
        out_shape=jax.ShapeDtypeStruct((E, cap, D_MODEL // 2), I32),
        scratch_shapes=[pltpu.VMEM((tm, D_MODEL), BF16), pltpu.VMEM((tm, D_FF), BF16)],
        compiler_params=_cparams(("parallel", "parallel", "arbitrary"), 60),
        name="ffn",
    )(xe, w_gate, w_up, w_down)


def _regroup(posb, rankb, affb, off, yw, cap, rows):
    NB, EB = posb.shape
    W = yw.shape[1]
    G, L, tt = SC_GATHER_ROWS, SC_LANES, MOE_TT
    n_workers = SC_CORES * SC_SUBCORES
    per = NB // n_workers
    mesh = plsc.VectorSubcoreMesh(core_axis_name="c", subcore_axis_name="s")
    off_pad = jnp.pad(off, (0, L))

    @pl.kernel(
        out_type=(jax.ShapeDtypeStruct((rows, W), I32), jax.ShapeDtypeStruct((rows,), I32),
                  jax.ShapeDtypeStruct((rows,), F32)),
        mesh=mesh,
        scratch_types=[pltpu.VMEM((EB,), I32), pltpu.VMEM((EB,), I32), pltpu.VMEM((EB,), F32),
                       pltpu.VMEM((EB,), I32), pltpu.VMEM((EB,), I32), pltpu.VMEM((EB,), F32),
                       pltpu.VMEM((G, W), I32), pltpu.VMEM((NB + 1 + L,), I32)],
        compiler_params=pltpu.CompilerParams(needs_layout_passes=False),
        name="sc_regroup",
    )
    def run(posb_hbm, rankb_hbm, affb_hbm, off_hbm, y_hbm, yg_hbm, tok_hbm, gate_hbm,
            pos_v, rank_v, aff_v, src_v, tok_v, gate_v, buf, off_v):
        wid = lax.axis_index("c") * SC_SUBCORES + lax.axis_index("s")
        pltpu.sync_copy(off_hbm, off_v)
        lane = lax.iota(I32, L)
        zi = jnp.zeros((L,), I32)
        zf = jnp.zeros((L,), F32)

        @pl.loop(0, per)
        def _(k):
            j = wid * per + k
            pltpu.sync_copy(posb_hbm.at[j], pos_v)
            pltpu.sync_copy(rankb_hbm.at[j], rank_v)
            pltpu.sync_copy(affb_hbm.at[j], aff_v)
            lo = jnp.max(plsc.load_gather(off_v, [zi + j]))
            hi = jnp.max(plsc.load_gather(off_v, [zi + j + 1]))

            @pl.loop(0, EB // L)
            def _(i):
                o = pl.multiple_of(i * L, L)
                src_v[pl.ds(o, L)] = zi
                tok_v[pl.ds(o, L)] = zi
                gate_v[pl.ds(o, L)] = zf

            @pl.loop(0, EB // L)
            def _(i):
                o = pl.multiple_of(i * L, L)
                p = pos_v[pl.ds(o, L)]
                r = rank_v[pl.ds(o, L)]
                m = p >= 0
                e = i // (tt // L)
                t0 = j * tt + (i % (tt // L)) * L
                plsc.store_scatter(src_v, [r], p + e * cap, mask=m)
                plsc.store_scatter(tok_v, [r], lane + t0, mask=m)
                plsc.store_scatter(gate_v, [r], aff_v[pl.ds(o, L)], mask=m)

            @pl.loop(0, (hi - lo) // G)
            def _(g):
                o = pl.multiple_of(g * G, G)
                dst = pl.multiple_of(lo + o, G)
                pltpu.sync_copy(y_hbm.at[src_v.at[pl.ds(o, G)]], buf)
                pltpu.sync_copy(buf, yg_hbm.at[pl.ds(dst, G)])
                pltpu.sync_copy(tok_v.at[pl.ds(o, G)], tok_hbm.at[pl.ds(dst, G)])
                pltpu.sync_copy(gate_v.at[pl.ds(o, G)], gate_hbm.at[pl.ds(dst, G)])

    return run(posb, rankb, affb, off_pad, yw)


def _combine_kernel(pj_ref, pw_ref, pf_ref, plo_ref, phi_ref, tok_ref, gate_ref, yg_ref, h_ref, nf_ref,
                    o_ref, acc_ref, *, group):
    p = pl.program_id(0)
    flag = pf_ref[p]
    tw, tt = MOE_TW, MOE_TT
    half = D_MODEL // 2

    hw = group // 2

    def col_blocks():
        for n in range(D_MODEL // group):
            yield slice(n * hw, (n + 1) * hw), slice(n * group, n * group + hw)
            yield slice(half + n * hw, half + (n + 1) * hw), slice(n * group + hw, (n + 1) * group)

    @pl.when((flag & 1) != 0)
    def _():
        for packed, natural in col_blocks():
            acc_ref[:, packed] = h_ref[:, natural]

    @pl.when((flag & 4) != 0)
    def _():
        lo, hi = plo_ref[p], phi_ref[p]
        row0 = pw_ref[p] * tw
        rid = lax.broadcasted_iota(I32, (tw, 1), 0) + row0
        keep = (rid >= lo) & (rid < hi)
        y_lo, y_hi = _unpack_bf16_pairs(yg_ref[...])
        y_lo = jnp.where(keep, y_lo, 0.0).astype(BF16)
        y_hi = jnp.where(keep, y_hi, 0.0).astype(BF16)
        tid = lax.broadcasted_iota(I32, (tt, tw), 0) + pj_ref[p] * tt
        cid = lax.broadcasted_iota(I32, (tt, tw), 1) + row0
        hit = (tok_ref[...] == tid) & (cid >= lo) & (cid < hi)
        weights = jnp.where(hit, gate_ref[...], 0.0).astype(BF16)
        acc_ref[:, :half] += _dot(weights, y_lo)
        acc_ref[:, half:] += _dot(weights, y_hi)

    @pl.when((flag & 2) != 0)
    def _():
        y = acc_ref[...]
        scale = lax.rsqrt(jnp.mean(y * y, axis=-1, keepdims=True) + EPS)
        for packed, natural in col_blocks():
            o_ref[:, natural] = acc_ref[:, packed] * scale * nf_ref[:, natural]


def _combine(lists, tok, gate, yg, h, nfw, group):
    pj, pw, pf, plo, phi = lists
    T = h.shape[0]
    tw, tt = MOE_TW, MOE_TT
    nwin = yg.shape[0] // tw
    grid_spec = pltpu.PrefetchScalarGridSpec(
        num_scalar_prefetch=5,
        grid=(pj.shape[0],),
        in_specs=[
            pl.BlockSpec((None, 1, tw), lambda p, pj, pw, *_: (pw[p], 0, 0)),
            pl.BlockSpec((None, 1, tw), lambda p, pj, pw, *_: (pw[p], 0, 0)),
            pl.BlockSpec((tw, D_MODEL // 2), lambda p, pj, pw, *_: (pw[p], 0)),
            pl.BlockSpec((tt, D_MODEL), lambda p, pj, pw, *_: (pj[p], 0)),
            pl.BlockSpec((1, D_MODEL), lambda p, pj, pw, *_: (0, 0)),
        ],
        out_specs=pl.BlockSpec((tt, D_MODEL), lambda p, pj, pw, *_: (pj[p], 0)),
        scratch_shapes=[pltpu.VMEM((tt, D_MODEL), F32)],
    )
    return pl.pallas_call(
        functools.partial(_combine_kernel, group=group),
        grid_spec=grid_spec,
        out_shape=jax.ShapeDtypeStruct((T, D_MODEL), F32),
        compiler_params=_cparams(("arbitrary",), 32),
        name="combine",
    )(pj, pw, pf, plo, phi, tok.reshape(nwin, 1, tw), gate.reshape(nwin, 1, tw), yg, h, nfw)


def _rope_tables(seq_len):
    d = RET_DK
    inv = ROPE_BASE ** (-jnp.arange(0, d, 2, dtype=F32) / d)
    ang = jnp.arange(seq_len, dtype=F32)[:, None] * inv[None, :]
    return jnp.cos(ang), jnp.sin(ang)


def _chunk_tri(n, chunk, upper):
    r = np.arange(n)
    same = (r[:, None] // chunk) == (r[None, :] // chunk)
    tri = (r[:, None] <= r[None, :]) if upper else (r[:, None] >= r[None, :])
    return jnp.asarray(same & tri, BF16)


def _prep_params(norm1_w, w_in, ret_gn_w, gla_gate_up, gla_gate_bias, gla_gn_w, w_out, norm2_w, router_w,
                 normf_w):
    w = w_in[0]
    w_main = w[:, :IN_MAIN].astype(BF16)
    w_ga = jnp.pad(w[:, IN_MAIN:], ((0, 0), (0, LANE - 2 * GLA_RANK))).astype(BF16)
    cs = np.ones((1, IN_MAIN), np.float32)
    cs[:, _RQ:_RQ + RET_WIDTH] = RET_DK ** -0.5
    cs[:, _GQ:_GQ + GLA_KEY_WIDTH] = GLA_DK ** -0.5
    up = gla_gate_up[0].astype(F32)
    up_pad = jnp.zeros((LANE, 2 * GLA_KEY_WIDTH), F32)
    up_pad = up_pad.at[:GLA_RANK, :GLA_KEY_WIDTH].set(up[0])
    up_pad = up_pad.at[GLA_RANK:2 * GLA_RANK, GLA_KEY_WIDTH:].set(up[1])
    rt = router_w[0].T.astype(F32)
    r_hi = rt.astype(BF16)
    r_lo = (rt - r_hi.astype(F32)).astype(BF16)
    return dict(
        n1w=norm1_w[0].reshape(1, D_MODEL).astype(F32),
        w_main=w_main, w_ga=w_ga, colscale=jnp.asarray(cs),
        up_pad=up_pad.astype(BF16),
        bias=gla_gate_bias[0].reshape(1, 2 * GLA_KEY_WIDTH).astype(F32),
        lf=_chunk_tri(GATE_TM, GLA_CHUNK, upper=False),
        lb=_chunk_tri(GATE_TM, GLA_CHUNK, upper=True),
        ret_gn=ret_gn_w[0].reshape(1, RET_WIDTH).astype(F32),
        gla_gn=gla_gn_w[0].reshape(1, GLA_WIDTH).astype(F32),
        w_out=w_out[0].astype(BF16),
        n2w=norm2_w[0].reshape(1, D_MODEL).astype(F32),
        r_hi=r_hi, r_lo=r_lo,
        nfw=normf_w.reshape(1, D_MODEL).astype(F32),
    )


def _trunk(x, pp, decay_logit, w_gate, w_up, w_down):
    B, L, _ = x.shape
    T = B * L
    x2d = x.reshape(T, D_MODEL)
    cos, sin = _rope_tables(L)
    proj, ga = _in_proj(x2d, pp["n1w"], pp["w_main"], pp["w_ga"], pp["colscale"], cos, sin, L)
    b_f, b_b = _gla_gates(ga, pp["up_pad"], pp["bias"], pp["lf"], pp["lb"])

    fwd = _mixer_scan(proj, decay_logit, b_f, B, L, reverse=False)
    mix_r, mix_g = _mixer_scan(proj, decay_logit, b_b, B, L, reverse=True, o_fwd=fwd,
                               gn_w=(pp["ret_gn"], pp["gla_gn"]))

    h, xn2, aff = _out_proj(mix_r, mix_g, pp["w_out"], x2d, pp["n2w"], pp["r_hi"], pp["r_lo"])

    cap = CAPACITY_FACTOR * T // N_EXPERTS
    pos, posb, rankb, affb = _select(aff, cap)
    off, c_lists = _combine_schedule(posb, T)
    xe = _dispatch(pos, xn2, cap).reshape(N_EXPERTS, cap, D_MODEL // 2)
    ye = _ffn(xe, w_gate, w_up, w_down).reshape(N_EXPERTS * cap, D_MODEL // 2)
    nb = T // MOE_TT
    yg, tok, gate = _regroup(posb.reshape(nb, -1), rankb.reshape(nb, -1), affb.reshape(nb, -1), off, ye, cap,
                             _regroup_rows(T))
    y = _combine(c_lists, tok, gate, yg, h, pp["nfw"], _ffn_tile_width(cap))
    return y.reshape(B, L, D_MODEL)


def kernel(x_prompt, x_sample, norm1_w, w_in, ret_decay_logit, ret_gn_w, gla_gate_up, gla_gate_bias,
           gla_gn_w, w_out, norm2_w, router_w, w_gate, w_up, w_down, normf_w):
    pp = _prep_params(norm1_w, w_in, ret_gn_w, gla_gate_up, gla_gate_bias, gla_gn_w, w_out, norm2_w,
                      router_w, normf_w)
    decay_logit = ret_decay_logit[0].astype(F32)
    args = (pp, decay_logit, w_gate[0], w_up[0], w_down[0])
    return (_trunk(x_prompt, *args), _trunk(x_sample, *args))
```

```python
import functools

import numpy as np
import jax
import jax.numpy as jnp
from jax import lax
from jax.experimental import pallas as pl
from jax.experimental.pallas import tpu as pltpu
from jax.experimental.pallas import tpu_sc as plsc

F32, BF16, I32 = jnp.float32, jnp.bfloat16, jnp.int32

D_MODEL = 2048
RET_WIDTH = 1024
RET_HEADS = 4
RET_DK = 256
RET_DV = 256
GLA_WIDTH = 1024
GLA_HEADS = 4
GLA_DK = 128
GLA_DV = 256
GLA_KEY_WIDTH = 512
GLA_RANK = 16
GLA_TAU = 16.0
RET_CHUNK = 256
GLA_CHUNK = 64
GLA_SUB = 16
ROPE_BASE = 10000.0
N_EXPERTS = 16
CAPACITY_FACTOR = 2
D_FF = 2048
EPS = 1e-6
LOG2_E = 1.4426950408889634
IN_MAIN = 4 * RET_WIDTH + 2 * GLA_KEY_WIDTH + 2 * GLA_WIDTH

_RQ, _RK, _RV, _RG = 0, 1024, 2048, 3072
_GQ, _GK, _GV, _GG = 4096, 4608, 5120, 6144

LANE = 128
MOE_TT = 512
MOE_TW = 512
V7X_VMEM_BYTES = 64 * 1024 * 1024


def _cparams(sem, vmem_mb):
    return pltpu.CompilerParams(dimension_semantics=sem, vmem_limit_bytes=vmem_mb * 1024 * 1024)


def _log_sigmoid(z):
    return jnp.minimum(z, 0.0) - jnp.log1p(jnp.exp(-jnp.abs(z)))


def _silu(g):
    return g * (1.0 / (1.0 + jnp.exp(-g)))


def _dot_nt(a, b):
    return lax.dot_general(a, b, (((1,), (1,)), ((), ())), preferred_element_type=F32)


def _dot_tn(a, b):
    return lax.dot_general(a, b, (((0,), (0,)), ((), ())), preferred_element_type=F32)


def _dot(a, b):
    return jnp.dot(a, b, preferred_element_type=F32)


def _pack_bf16_pairs(x):
    bits = pltpu.bitcast(x.astype(BF16).astype(F32), I32)
    w = x.shape[1] // 2
    return bits[:, w:] | lax.shift_right_logical(bits[:, :w], 16)


def _unpack_bf16_pairs(words):
    lo = pltpu.bitcast(lax.shift_left(words, 16), F32)
    hi = pltpu.bitcast(words & jnp.int32(-65536), F32)
    return lo, hi


IP_TM = 1024
IP_TN = 1024


def _in_proj_kernel(x_ref, n1_ref, w_ref, wga_ref, cs_ref, cos_ref, sin_ref, o_ref, ga_ref, xn_ref):
    j = pl.program_id(1)

    @pl.when(j == 0)
    def _():
        x = x_ref[...]
        ms = jnp.mean(x * x, axis=-1, keepdims=True)
        xn = (x * lax.rsqrt(ms + EPS) * n1_ref[...]).astype(BF16)
        xn_ref[...] = xn
        ga_ref[...] = _dot(xn, wga_ref[...])

    acc = _dot(xn_ref[...], w_ref[...]) * cs_ref[...]
    n_rope_blocks = 2 * RET_WIDTH // IP_TN

    @pl.when(j < n_rope_blocks)
    def _():
        cos = cos_ref[...]
        sin = sin_ref[...]
        for h in range(IP_TN // RET_DK):
            x1 = acc[:, 2 * h * LANE:(2 * h + 1) * LANE]
            x2 = acc[:, (2 * h + 1) * LANE:(2 * h + 2) * LANE]
            o_ref[2 * h] = (x1 * cos - x2 * sin).astype(BF16)
            o_ref[2 * h + 1] = (x1 * sin + x2 * cos).astype(BF16)

    @pl.when(j >= n_rope_blocks)
    def _():
        for c in range(IP_TN // LANE):
            o_ref[c] = acc[:, c * LANE:(c + 1) * LANE].astype(BF16)


def _in_proj(x2d, n1w, w_main, w_ga, colscale, cos, sin, seq_len):
    T = x2d.shape[0]
    tm, tn = IP_TM, IP_TN
    nlb = seq_len // tm
    return pl.pallas_call(
        _in_proj_kernel,
        grid=(T // tm, IN_MAIN // tn),
        in_specs=[
            pl.BlockSpec((tm, D_MODEL), lambda i, j: (i, 0)),
            pl.BlockSpec((1, D_MODEL), lambda i, j: (0, 0)),
            pl.BlockSpec((D_MODEL, tn), lambda i, j: (0, j)),
            pl.BlockSpec((D_MODEL, LANE), lambda i, j: (0, 0)),
            pl.BlockSpec((1, tn), lambda i, j: (0, j)),
            pl.BlockSpec((tm, LANE), lambda i, j: (i % nlb, 0)),
            pl.BlockSpec((tm, LANE), lambda i, j: (i % nlb, 0)),
        ],
        out_specs=[
            pl.BlockSpec((tn // LANE, tm, LANE), lambda i, j: (j, i, 0)),
            pl.BlockSpec((tm, LANE), lambda i, j: (i, 0)),
        ],
        out_shape=[
            jax.ShapeDtypeStruct((IN_MAIN // LANE, T, LANE), BF16),
            jax.ShapeDtypeStruct((T, LANE), F32),
        ],
        scratch_shapes=[pltpu.VMEM((tm, D_MODEL), BF16)],
        compiler_params=_cparams(("parallel", "arbitrary"), 48),
        name="in_proj",
    )(x2d, n1w, w_main, w_ga, colscale, cos, sin)


GATE_TM = 512


def _gates_kernel(ga_ref, up_ref, bias_ref, lf_ref, lb_ref, bf_ref, bb_ref):
    z = _dot(ga_ref[...].astype(BF16), up_ref[...]) + bias_ref[...]
    la = _log_sigmoid(z) * (LOG2_E / GLA_TAU)
    hi = la.astype(BF16)
    lo = (la - hi.astype(F32)).astype(BF16)
    kw = GLA_KEY_WIDTH
    bf_ref[...] = _dot(lf_ref[...], hi[:, :kw]) + _dot(lf_ref[...], lo[:, :kw])
    bb_ref[...] = _dot(lb_ref[...], hi[:, kw:]) + _dot(lb_ref[...], lo[:, kw:])


def _gla_gates(ga, up_pad, bias, lf, lb):
    T = ga.shape[0]
    tm = GATE_TM
    kw = GLA_KEY_WIDTH
    return pl.pallas_call(
        _gates_kernel,
        grid=(T // tm,),
        in_specs=[
            pl.BlockSpec((tm, LANE), lambda i: (i, 0)),
            pl.BlockSpec((LANE, 2 * kw), lambda i: (0, 0)),
            pl.BlockSpec((1, 2 * kw), lambda i: (0, 0)),
            pl.BlockSpec((tm, tm), lambda i: (0, 0)),
            pl.BlockSpec((tm, tm), lambda i: (0, 0)),
        ],
        out_specs=[pl.BlockSpec((tm, kw), lambda i: (i, 0)), pl.BlockSpec((tm, kw), lambda i: (i, 0))],
        out_shape=[jax.ShapeDtypeStruct((T, kw), F32), jax.ShapeDtypeStruct((T, kw), F32)],
        compiler_params=_cparams(("parallel",), 32),
        name="gla_gates",
    )(ga, up_pad, bias, lf, lb)


def _wide(ref, rows):
    return jnp.concatenate([ref[0, rows, :], ref[1, rows, :]], axis=1)


def _finish_heads(tot, gn, gate):
    ms = jnp.mean(tot * tot, axis=-1, keepdims=True)
    yn = tot * lax.rsqrt(ms + EPS) * gn
    return (yn * _silu(gate.astype(F32))).astype(BF16)


RET_TB = 1024


def _ret_kernel(dl_ref, q_ref, k_ref, v_ref, *rest, reverse):
    if reverse:
        g_ref, of_ref, gn_ref, o_ref, s_ref, intra_ref, qd_ref, kd_ref, cd_ref, p_ref, u_ref = rest
    else:
        o_ref, s_ref, intra_ref, qd_ref, kd_ref, cd_ref, p_ref, u_ref = rest
    h = pl.program_id(1)
    n = pl.program_id(2)
    C = RET_CHUNK

    @pl.when(n == 0)
    def _():
        s_ref[...] = jnp.zeros_like(s_ref)
        logit = dl_ref[1 if reverse else 0, h]
        lg = _log_sigmoid(jnp.full((C, RET_DV), logit, F32))
        lg_c = _log_sigmoid(jnp.full((C, C), logit, F32))
        lg_r = _log_sigmoid(jnp.full((1, RET_DV), logit, F32))
        ri = lax.broadcasted_iota(I32, (C, RET_DV), 0).astype(F32)
        rc = lax.broadcasted_iota(I32, (C, C), 0).astype(F32)
        cc = lax.broadcasted_iota(I32, (C, C), 1).astype(F32)
        diff = (cc - rc) if reverse else (rc - cc)
        intra_ref[...] = jnp.where(diff >= 0, jnp.exp(lg_c * diff), 0.0)
        if reverse:
            qd_ref[...] = jnp.exp(lg * (C - ri))
            kd_ref[...] = jnp.exp(lg * ri)
        else:
            qd_ref[...] = jnp.exp(lg * (ri + 1.0))
            kd_ref[...] = jnp.exp(lg * (C - 1.0 - ri))
        cd_ref[...] = jnp.exp(lg_r * C)

    nchunks = o_ref.shape[0] // C
    for c in range(nchunks):
        rows = slice(c * C, (c + 1) * C)
        k = _wide(k_ref, rows)
        p_ref[c] = (_dot_nt(_wide(q_ref, rows), k) * intra_ref[...]).astype(BF16)
        kd = (k.astype(F32) * kd_ref[...]).astype(BF16)
        u_ref[c] = _dot_tn(kd, _wide(v_ref, rows))
    order = range(nchunks - 1, -1, -1) if reverse else range(nchunks)
    for c in order:
        rows = slice(c * C, (c + 1) * C)
        state = s_ref[...]
        o = _dot(p_ref[c], _wide(v_ref, rows)) + _dot(_wide(q_ref, rows), state.astype(BF16)) * qd_ref[...]
        s_ref[...] = state * cd_ref[...] + u_ref[c]
        if reverse:
            tot = of_ref[rows, :].astype(F32) + o
            o_ref[rows, :] = _finish_heads(tot, gn_ref[...], _wide(g_ref, rows))
        else:
            o_ref[rows, :] = o.astype(BF16)


def _ret_parts(proj, decay_logit, rb, reverse, o_fwd=None, gn_w=None):
    T = proj.shape[1]
    tb = RET_TB
    dk, dv, C = RET_DK, RET_DV, RET_CHUNK

    def head(base):
        return pl.BlockSpec((dk // LANE, tb, LANE), lambda b, h, n: (base // dk + h, rb(b, n), 0))

    in_specs = [pl.BlockSpec(memory_space=pltpu.SMEM), head(_RQ), head(_RK), head(_RV)]
    args = [decay_logit, proj, proj, proj]
    if reverse:
        in_specs += [
            head(_RG),
            pl.BlockSpec((tb, dv), lambda b, h, n: (rb(b, n), h)),
            pl.BlockSpec((1, dv), lambda b, h, n: (0, h)),
        ]
        args += [proj, o_fwd, gn_w]
    out_spec = pl.BlockSpec((tb, dv), lambda b, h, n: (rb(b, n), h))
    out_shape = jax.ShapeDtypeStruct((T, RET_WIDTH), BF16)
    scratch = [
        pltpu.VMEM((dk, dv), F32),
        pltpu.VMEM((C, C), F32),
        pltpu.VMEM((C, dv), F32),
        pltpu.VMEM((C, dk), F32),
        pltpu.VMEM((1, dv), F32),
        pltpu.VMEM((tb // C, C, C), BF16),
        pltpu.VMEM((tb // C, dk, dv), F32),
    ]
    return in_specs, args, out_spec, out_shape, scratch


GLA_TB = 1024
GLA_UNROLL = 16


GLA_LEVELS = (32, 16, 8, 4, 2, 1)
SUBLANES = 8


def _gla_tables(reverse):
    C = GLA_CHUNK
    r = np.arange(C)
    masks = np.zeros((len(GLA_LEVELS) + 1, C, C), np.float32)
    for l, s in enumerate(GLA_LEVELS):
        upper = (r & s) != 0
        same = (r[:, None] // (2 * s)) == (r[None, :] // (2 * s))
        lhs_rows = ~upper if reverse else upper
        masks[l] = same & lhs_rows[:, None] & ~lhs_rows[None, :]
    masks[-1] = np.eye(C)
    return jnp.asarray(masks, F32)


def _gla_kernel(q_ref, k_ref, v_ref, b_ref, mask_ref, *rest, reverse):
    if reverse:
        g_ref, of_ref, gn_ref, o_ref, st_ref, sc_ref = rest
    else:
        o_ref, st_ref, sc_ref = rest
    n = pl.program_id(2)
    C = GLA_CHUNK

    @pl.when(n == 0)
    def _():
        st_ref[...] = jnp.zeros_like(st_ref)

    nchunks = q_ref.shape[0] // C
    sub_row = lax.broadcasted_iota(I32, (SUBLANES, GLA_DK), 0)
    zero_rows = jnp.zeros((SUBLANES, GLA_DK), F32)

    def chunk_scores(c, carry):
        c0 = pl.multiple_of(c * C, C)
        qb = q_ref[pl.ds(c0, C), :]
        kb = k_ref[pl.ds(c0, C), :]
        q = qb.astype(F32)
        k = kb.astype(F32)
        b = b_ref[pl.ds(c0, C), :]

        def mid_row(r):
            return jnp.broadcast_to(b[r:r + 1, :], (SUBLANES, GLA_DK))

        scores = mask_ref[len(GLA_LEVELS)] * _dot_nt(qb, kb)
        for l, s in enumerate(GLA_LEVELS):
            lhs, rhs = [], []
            for g in range(C // SUBLANES):
                r0 = g * SUBLANES
                rows = slice(r0, r0 + SUBLANES)
                if s >= SUBLANES:
                    m = mid_row((r0 // (2 * s)) * (2 * s) + s)
                    is_lhs = ((r0 & s) != 0) != reverse
                    if is_lhs:
                        lhs.append(q[rows] * jnp.exp2(b[rows] - m))
                        rhs.append(zero_rows)
                    else:
                        lhs.append(zero_rows)
                        rhs.append(k[rows] * jnp.exp2(m - b[rows]))
                else:
                    m = mid_row(r0 + SUBLANES - s)
                    for blk in range(SUBLANES // (2 * s) - 2, -1, -1):
                        m = jnp.where(sub_row < (blk + 1) * 2 * s, mid_row(r0 + blk * 2 * s + s), m)
                    upper = (sub_row & s) != 0
                    is_lhs = jnp.logical_not(upper) if reverse else upper
                    lhs.append(jnp.where(is_lhs, q[rows] * jnp.exp2(b[rows] - m), 0.0))
                    rhs.append(jnp.where(is_lhs, 0.0, k[rows] * jnp.exp2(m - b[rows])))
            lhs = jnp.concatenate(lhs, axis=0).astype(BF16)
            rhs = jnp.concatenate(rhs, axis=0).astype(BF16)
            scores = scores + mask_ref[l] * _dot_nt(lhs, rhs)
        sc_ref[c] = scores.astype(BF16)
        return carry

    lax.fori_loop(0, nchunks, chunk_scores, 0, unroll=GLA_UNROLL)

    def chunk(ci, carry):
        c = (nchunks - 1 - ci) if reverse else ci
        c0 = pl.multiple_of(c * C, C)
        q = q_ref[pl.ds(c0, C), :].astype(F32)
        k = k_ref[pl.ds(c0, C), :].astype(F32)
        v = _wide(v_ref, pl.ds(c0, C))
        b = b_ref[pl.ds(c0, C), :]
        b_end = b[0:1, :] if reverse else b[C - 1:C, :]

        st = st_ref[...]
        o = _dot_nt((q * jnp.exp2(b)).astype(BF16), st.astype(BF16))
        ke = (k * jnp.exp2(b_end - b)).astype(BF16)
        st_ref[...] = st * jnp.exp2(b_end) + _dot_tn(v, ke)
        o = o + _dot(sc_ref[c], v)
        if reverse:
            tot = of_ref[pl.ds(c0, C), :].astype(F32) + o
            o_ref[pl.ds(c0, C), :] = _finish_heads(tot, gn_ref[...], _wide(g_ref, pl.ds(c0, C)))
        else:
            o_ref[pl.ds(c0, C), :] = o.astype(BF16)
        return carry

    lax.fori_loop(0, nchunks, chunk, 0, unroll=GLA_UNROLL)


def _gla_parts(proj, bcum, rb, reverse, o_fwd=None, gn_w=None):
    T = proj.shape[1]
    tb = GLA_TB
    dk, dv = GLA_DK, GLA_DV
    masks = _gla_tables(reverse)

    def key_block(base):
        return pl.BlockSpec((None, tb, LANE), lambda b, h, n: (base // dk + h, rb(b, n), 0))

    def value_block(base):
        return pl.BlockSpec((dv // LANE, tb, LANE), lambda b, h, n: (base // dv + h, rb(b, n), 0))

    in_specs = [
        key_block(_GQ), key_block(_GK), value_block(_GV),
        pl.BlockSpec((tb, dk), lambda b, h, n: (rb(b, n), h)),
        pl.BlockSpec(masks.shape, lambda b, h, n: (0, 0, 0)),
    ]
    args = [proj, proj, proj, bcum, masks]
    if reverse:
        in_specs += [
            value_block(_GG),
            pl.BlockSpec((tb, dv), lambda b, h, n: (rb(b, n), h)),
            pl.BlockSpec((1, dv), lambda b, h, n: (0, h)),
        ]
        args += [proj, o_fwd, gn_w]
    out_spec = pl.BlockSpec((tb, dv), lambda b, h, n: (rb(b, n), h))
    out_shape = jax.ShapeDtypeStruct((T, GLA_WIDTH), BF16)
    scratch = [pltpu.VMEM((dv, dk), F32), pltpu.VMEM((tb // GLA_CHUNK, GLA_CHUNK, GLA_CHUNK), BF16)]
    return in_specs, args, out_spec, out_shape, scratch


def _mixer_kernel(*refs, reverse, n_ret_in, n_gla_in, n_ret_scratch):
    ret_in = refs[:n_ret_in]
    gla_in = refs[n_ret_in:n_ret_in + n_gla_in]
    ret_out, gla_out = refs[n_ret_in + n_gla_in:n_ret_in + n_gla_in + 2]
    scratch = refs[n_ret_in + n_gla_in + 2:]
    _ret_kernel(*ret_in, ret_out, *scratch[:n_ret_scratch], reverse=reverse)
    _gla_kernel(*gla_in, gla_out, *scratch[n_ret_scratch:], reverse=reverse)


def _mixer_scan(proj, decay_logit, bcum, batch, seq_len, reverse, o_fwd=(None, None), gn_w=(None, None)):
    assert RET_TB == GLA_TB and RET_HEADS == GLA_HEADS
    nb = seq_len // RET_TB

    def rb(b, n):
        return b * nb + ((nb - 1 - n) if reverse else n)

    r_specs, r_args, r_out, r_shape, r_scratch = _ret_parts(proj, decay_logit, rb, reverse, o_fwd[0], gn_w[0])
    g_specs, g_args, g_out, g_shape, g_scratch = _gla_parts(proj, bcum, rb, reverse, o_fwd[1], gn_w[1])
    return pl.pallas_call(
        functools.partial(_mixer_kernel, reverse=reverse, n_ret_in=len(r_specs), n_gla_in=len(g_specs),
                          n_ret_scratch=len(r_scratch)),
        grid=(batch, RET_HEADS, nb),
        in_specs=r_specs + g_specs,
        out_specs=[r_out, g_out],
        out_shape=[r_shape, g_shape],
        scratch_shapes=r_scratch + g_scratch,
        compiler_params=_cparams(("parallel", "parallel", "arbitrary"), 48),
        name="mixer_bwd" if reverse else "mixer_fwd",
    )(*r_args, *g_args)


OP_TM = 512


def _out_proj_kernel(mr_ref, mg_ref, w0_ref, w1_ref, x_ref, n2_ref, rh_ref, rl_ref, h_ref, xn_ref, aff_ref,
                     hs_ref):
    s = pl.program_id(0)
    slot = s % 2

    @pl.when(s == 0)
    def _():
        hs_ref[1] = jnp.zeros(hs_ref.shape[1:], F32)

    hp = hs_ref[1 - slot]
    ms = jnp.mean(hp * hp, axis=-1, keepdims=True)
    xn = hp * lax.rsqrt(ms + EPS) * n2_ref[...]
    xh = xn.astype(BF16)
    xn_ref[...] = _pack_bf16_pairs(xn)
    xl = (xn - xh.astype(F32)).astype(BF16)
    lt = _dot_nt(rh_ref[...], xh) + _dot_nt(rh_ref[...], xl) + _dot_nt(rl_ref[...], xh)
    m = jnp.max(lt, axis=0, keepdims=True)
    e = jnp.exp(lt - m)
    aff_ref[...] = e / jnp.sum(e, axis=0, keepdims=True)

    h = x_ref[...] + _dot(mr_ref[...], w0_ref[...]) + _dot(mg_ref[...], w1_ref[...])
    h_ref[...] = h
    hs_ref[slot] = h


def _out_proj(mix_r, mix_g, w_out, x2d, n2w, r_hi, r_lo):
    T = x2d.shape[0]
    tm = OP_TM
    half = RET_WIDTH
    nblk = T // tm

    def head(s):
        return jnp.minimum(s, nblk - 1)

    def tail(s):
        return jnp.maximum(s - 1, 0)

    return pl.pallas_call(
        _out_proj_kernel,
        grid=(nblk + 1,),
        in_specs=[
            pl.BlockSpec((tm, half), lambda s: (head(s), 0)),
            pl.BlockSpec((tm, half), lambda s: (head(s), 0)),
            pl.BlockSpec((half, D_MODEL), lambda s: (0, 0)),
            pl.BlockSpec((half, D_MODEL), lambda s: (1, 0)),
            pl.BlockSpec((tm, D_MODEL), lambda s: (head(s), 0)),
            pl.BlockSpec((1, D_MODEL), lambda s: (0, 0)),
            pl.BlockSpec((N_EXPERTS, D_MODEL), lambda s: (0, 0)),
            pl.BlockSpec((N_EXPERTS, D_MODEL), lambda s: (0, 0)),
        ],
        out_specs=[
            pl.BlockSpec((tm, D_MODEL), lambda s: (head(s), 0)),
            pl.BlockSpec((tm, D_MODEL // 2), lambda s: (tail(s), 0)),
            pl.BlockSpec((N_EXPERTS, tm), lambda s: (0, tail(s))),
        ],
        out_shape=[
            jax.ShapeDtypeStruct((T, D_MODEL), F32),
            jax.ShapeDtypeStruct((T, D_MODEL // 2), I32),
            jax.ShapeDtypeStruct((N_EXPERTS, T), F32),
        ],
        scratch_shapes=[pltpu.VMEM((2, tm, D_MODEL), F32)],
        compiler_params=_cparams(("arbitrary",), 60),
        name="out_proj",
    )(mix_r, mix_g, w_out, w_out, x2d, n2w, r_hi, r_lo)


def _select_kernel(a_ref, pos_ref, posb_ref, rankb_ref, affb_ref, *, cap):
    E, T = a_ref.shape
    tt = MOE_TT

    def count(pred):
        return jnp.sum(pred.astype(F32), axis=1, keepdims=True)

    def bisect(i, tau):
        cand = tau | jnp.left_shift(jnp.int32(1), 30 - i)
        bits = pltpu.bitcast(a_ref[...], I32)
        return jnp.where(count(bits >= cand) >= cap, cand, tau)

    tau = lax.fori_loop(0, 31, bisect, jnp.zeros((E, 1), I32))
    bits_all = pltpu.bitcast(a_ref[...], I32)
    quota = cap - count(bits_all > tau)

    before = (lax.broadcasted_iota(I32, (tt, tt), 0) < lax.broadcasted_iota(I32, (tt, tt), 1)).astype(BF16)
    below = (lax.broadcasted_iota(I32, (E, E), 1) < lax.broadcasted_iota(I32, (E, E), 0)).astype(BF16)

    def block(j, carry):
        c_eq, c_sel = carry
        off = pl.multiple_of(j * tt, tt)
        aff = a_ref[:, pl.ds(off, tt)]
        bits = pltpu.bitcast(aff, I32)
        eq = bits == tau
        eqf = eq.astype(F32)
        rank_eq = _dot(eqf.astype(BF16), before) + c_eq
        sel = (bits > tau) | (eq & (rank_eq < quota))
        self_ = sel.astype(F32)
        selb = self_.astype(BF16)
        slot = _dot(selb, before) + c_sel
        pos = jnp.where(sel, slot, -1.0).astype(I32)
        pos_ref[:, pl.ds(off, tt)] = pos
        per_tok = jnp.broadcast_to(jnp.sum(self_, axis=0, keepdims=True), (E, tt))
        rank = _dot(per_tok.astype(BF16), before) + _dot(below, selb)
        posb_ref[j] = pos
        rankb_ref[j] = jnp.where(sel, rank, -1.0).astype(I32)
        affb_ref[j] = aff
        return (c_eq + jnp.sum(eqf, axis=1, keepdims=True), c_sel + jnp.sum(self_, axis=1, keepdims=True))

    zero = jnp.zeros((E, 1), F32)
    lax.fori_loop(0, T // tt, block, (zero, zero))


def _select(aff, cap):
    E, T = aff.shape
    nb = T // MOE_TT
    blk = jax.ShapeDtypeStruct((nb, E, MOE_TT), I32)
    return pl.pallas_call(
        functools.partial(_select_kernel, cap=cap),
        out_shape=[jax.ShapeDtypeStruct((E, T), I32), blk, blk, jax.ShapeDtypeStruct((nb, E, MOE_TT), F32)],
        compiler_params=pltpu.CompilerParams(vmem_limit_bytes=40 * 1024 * 1024),
        name="select",
    )(aff)


def _regroup_rows(T):
    nb = T // MOE_TT
    rows = CAPACITY_FACTOR * T + SC_GATHER_ROWS * nb
    return -(-rows // MOE_TW) * MOE_TW


def _combine_schedule(posb, T):
    nb = posb.shape[0]
    tw, g = MOE_TW, SC_GATHER_ROWS
    n = jnp.sum((posb >= 0).reshape(nb, -1), axis=1).astype(I32)
    seg = (n + g - 1) // g * g
    hi = jnp.cumsum(seg)
    lo = hi - seg
    off = jnp.concatenate([jnp.zeros((1,), I32), hi])
    nwin_max = (N_EXPERTS * MOE_TT) // tw + 1
    w0 = lo // tw
    w1 = jnp.where(seg > 0, (hi - 1) // tw, w0)
    cand = jnp.arange(nwin_max, dtype=I32)
    win = w0[:, None] + cand[None, :]
    valid = (win <= w1[:, None]).reshape(-1)
    nwin_total = _regroup_rows(T) // tw
    pmax = nb + nwin_total
    jv = jnp.broadcast_to(jnp.arange(nb, dtype=I32)[:, None], win.shape).reshape(-1)
    wv = jnp.minimum(win, nwin_total - 1).reshape(-1)
    total = jnp.sum(valid.astype(I32))
    dst = jnp.where(valid, jnp.cumsum(valid.astype(I32)) - 1, pmax)
    pj, pw = (jnp.zeros((pmax,), I32).at[dst].set(a, mode="drop") for a in (jv, wv))
    real = jnp.arange(pmax, dtype=I32) < total
    pj, pw = (jnp.where(real, a, a[total - 1]) for a in (pj, pw))
    first = jnp.concatenate([jnp.ones((1,), bool), pj[1:] != pj[:-1]])
    last = jnp.concatenate([pj[1:] != pj[:-1], jnp.ones((1,), bool)]) | (jnp.arange(pmax, dtype=I32) == total - 1)
    flag = jnp.where(real, first.astype(I32) + 2 * last.astype(I32) + 4, 0)
    return off, (pj, pw, flag, lo[pj], hi[pj])


SC_LANES = 16
SC_CORES = 2
SC_SUBCORES = 16
SC_GATHER_ROWS = 32


def _dispatch(pos, xw, cap):
    E, T = pos.shape
    W = xw.shape[1]
    G = SC_GATHER_ROWS
    part_rows = cap // SC_CORES
    mesh = plsc.VectorSubcoreMesh(core_axis_name="c", subcore_axis_name="s")

    @pl.kernel(
        out_type=jax.ShapeDtypeStruct((E * cap, W), I32),
        mesh=mesh,
        scratch_types=[pltpu.VMEM((T,), I32), pltpu.VMEM((cap,), I32), pltpu.VMEM((G, W), I32)],
        compiler_params=pltpu.CompilerParams(needs_layout_passes=False),
        name="sc_dispatch",
    )
    def run(pos_hbm, x_hbm, xe_hbm, pos_v, idx_v, buf):
        e = lax.axis_index("s")
        part = lax.axis_index("c")
        pltpu.sync_copy(pos_hbm.at[e], pos_v)
        lane = lax.iota(I32, SC_LANES)

        @pl.loop(0, T // SC_LANES)
        def _(i):
            off = pl.multiple_of(i * SC_LANES, SC_LANES)
            p = pos_v[pl.ds(off, SC_LANES)]
            plsc.store_scatter(idx_v, [p], lane + off, mask=p >= 0)

        @pl.loop(0, part_rows // G)
        def _(g):
            o = pl.multiple_of(part * part_rows + g * G, G)
            pltpu.sync_copy(x_hbm.at[idx_v.at[pl.ds(o, G)]], buf)
            pltpu.sync_copy(buf, xe_hbm.at[pl.ds(e * cap + o, G)])

    return run(pos, xw)


FFN_TM = 2048
FFN_UNPACK_ROWS = 256


def _ffn_tile_width(cap):
    return 512 if min(FFN_TM, cap) <= 1024 else 256


def _ffn_kernel(x_ref, wg_ref, wu_ref, wd_ref, o_ref, xb_ref, hid_ref, *, tw):
    s = pl.program_id(2)
    n_up = D_FF // tw

    @pl.when(s == 0)
    def _():
        half = D_MODEL // 2

        def unpack_rows(i, carry):
            r = pl.multiple_of(i * FFN_UNPACK_ROWS, FFN_UNPACK_ROWS)
            lo, hi = _unpack_bf16_pairs(x_ref[pl.ds(r, FFN_UNPACK_ROWS), :])
            xb_ref[pl.ds(r, FFN_UNPACK_ROWS), :half] = lo.astype(BF16)
            xb_ref[pl.ds(r, FFN_UNPACK_ROWS), half:] = hi.astype(BF16)
            return carry

        lax.fori_loop(0, x_ref.shape[0] // FFN_UNPACK_ROWS, unpack_rows, 0)

    @pl.when(s < n_up)
    def _():
        x = xb_ref[...]
        g = _dot(x, wg_ref[...].astype(BF16))
        u = _dot(x, wu_ref[...].astype(BF16))
        col = pl.multiple_of(s * tw, tw)
        hid_ref[:, pl.ds(col, tw)] = (_silu(g) * u).astype(BF16)

    @pl.when(s >= n_up)
    def _():
        o_ref[...] = _pack_bf16_pairs(_dot(hid_ref[...], wd_ref[...].astype(BF16)))


def _ffn(xe, w_gate, w_up, w_down):
    E, cap, _ = xe.shape
    tm = min(FFN_TM, cap)
    tw = _ffn_tile_width(cap)
    n_up, n_down = D_FF // tw, D_MODEL // tw

    def up(e, m, s):
        return (e, 0, jnp.minimum(s, n_up - 1))

    def down(s):
        return jnp.maximum(s - n_up, 0)

    return pl.pallas_call(
        functools.partial(_ffn_kernel, tw=tw),
        grid=(E, cap // tm, n_up + n_down),
        in_specs=[
            pl.BlockSpec((None, tm, D_MODEL // 2), lambda e, m, s: (e, m, 0)),
            pl.BlockSpec((None, D_MODEL, tw), up),
            pl.BlockSpec((None, D_MODEL, tw), up),
            pl.BlockSpec((None, D_FF, tw), lambda e, m, s: (e, 0, down(s))),
        ],
        out_specs=pl.BlockSpec((None, tm, tw // 2), lambda e, m, s: (e, m, down(s))),
        out_shape=jax.ShapeDtypeStruct((E, cap, D_MODEL // 2), I32),
        scratch_shapes=[pltpu.VMEM((tm, D_MODEL), BF16), pltpu.VMEM((tm, D_FF), BF16)],
        compiler_params=_cparams(("parallel", "parallel", "arbitrary"), 60),
        name="ffn",
    )(xe, w_gate, w_up, w_down)


def _regroup(posb, rankb, affb, off, yw, cap, rows):
    NB, EB = posb.shape
    W = yw.shape[1]
    G, L, tt = SC_GATHER_ROWS, SC_LANES, MOE_TT
    n_workers = SC_CORES * SC_SUBCORES
    per = -(-NB // n_workers)
    mesh = plsc.VectorSubcoreMesh(core_axis_name="c", subcore_axis_name="s")
    off_pad = jnp.pad(off, (0, L))

    @pl.kernel(
        out_type=(jax.ShapeDtypeStruct((rows, W), I32), jax.ShapeDtypeStruct((rows,), I32),
                  jax.ShapeDtypeStruct((rows,), F32)),
        mesh=mesh,
        scratch_types=[pltpu.VMEM((EB,), I32), pltpu.VMEM((EB,), I32), pltpu.VMEM((EB,), F32),
                       pltpu.VMEM((EB,), I32), pltpu.VMEM((EB,), I32), pltpu.VMEM((EB,), F32),
                       pltpu.VMEM((G, W), I32), pltpu.VMEM((NB + 1 + L,), I32)],
        compiler_params=pltpu.CompilerParams(needs_layout_passes=False),
        name="sc_regroup",
    )
    def run(posb_hbm, rankb_hbm, affb_hbm, off_hbm, y_hbm, yg_hbm, tok_hbm, gate_hbm,
            pos_v, rank_v, aff_v, src_v, tok_v, gate_v, buf, off_v):
        wid = lax.axis_index("c") * SC_SUBCORES + lax.axis_index("s")
        pltpu.sync_copy(off_hbm, off_v)
        lane = lax.iota(I32, L)
        zi = jnp.zeros((L,), I32)
        zf = jnp.zeros((L,), F32)

        def regroup_block(j):
            pltpu.sync_copy(posb_hbm.at[j], pos_v)
            pltpu.sync_copy(rankb_hbm.at[j], rank_v)
            pltpu.sync_copy(affb_hbm.at[j], aff_v)
            lo = jnp.max(plsc.load_gather(off_v, [zi + j]))
            hi = jnp.max(plsc.load_gather(off_v, [zi + j + 1]))

            @pl.loop(0, EB // L)
            def _(i):
                o = pl.multiple_of(i * L, L)
                src_v[pl.ds(o, L)] = zi
                tok_v[pl.ds(o, L)] = zi
                gate_v[pl.ds(o, L)] = zf

            @pl.loop(0, EB // L)
            def _(i):
                o = pl.multiple_of(i * L, L)
                p = pos_v[pl.ds(o, L)]
                r = rank_v[pl.ds(o, L)]
                m = p >= 0
                e = i // (tt // L)
                t0 = j * tt + (i % (tt // L)) * L
                plsc.store_scatter(src_v, [r], p + e * cap, mask=m)
                plsc.store_scatter(tok_v, [r], lane + t0, mask=m)
                plsc.store_scatter(gate_v, [r], aff_v[pl.ds(o, L)], mask=m)

            @pl.loop(0, (hi - lo) // G)
            def _(g):
                o = pl.multiple_of(g * G, G)
                dst = pl.multiple_of(lo + o, G)
                pltpu.sync_copy(y_hbm.at[src_v.at[pl.ds(o, G)]], buf)
                pltpu.sync_copy(buf, yg_hbm.at[pl.ds(dst, G)])
                pltpu.sync_copy(tok_v.at[pl.ds(o, G)], tok_hbm.at[pl.ds(dst, G)])
                pltpu.sync_copy(gate_v.at[pl.ds(o, G)], gate_hbm.at[pl.ds(dst, G)])

        @pl.loop(0, per)
        def _(k):
            j = wid * per + k

            @pl.when(j < NB)
            def _():
                regroup_block(j)

    return run(posb, rankb, affb, off_pad, yw)


def _combine_kernel(pj_ref, pw_ref, pf_ref, plo_ref, phi_ref, tok_ref, gate_ref, yg_ref, h_ref, nf_ref,
                    o_ref, acc_ref, *, group):
    p = pl.program_id(0)
    flag = pf_ref[p]
    tw, tt = MOE_TW, MOE_TT
    half = D_MODEL // 2

    hw = group // 2

    def col_blocks():
        for n in range(D_MODEL // group):
            yield slice(n * hw, (n + 1) * hw), slice(n * group, n * group + hw)
            yield slice(half + n * hw, half + (n + 1) * hw), slice(n * group + hw, (n + 1) * group)

    @pl.when((flag & 1) != 0)
    def _():
        for packed, natural in col_blocks():
            acc_ref[:, packed] = h_ref[:, natural]

    @pl.when((flag & 4) != 0)
    def _():
        lo, hi = plo_ref[p], phi_ref[p]
        row0 = pw_ref[p] * tw
        rid = lax.broadcasted_iota(I32, (tw, 1), 0) + row0
        keep = (rid >= lo) & (rid < hi)
        y_lo, y_hi = _unpack_bf16_pairs(yg_ref[...])
        y_lo = jnp.where(keep, y_lo, 0.0).astype(BF16)
        y_hi = jnp.where(keep, y_hi, 0.0).astype(BF16)
        tid = lax.broadcasted_iota(I32, (tt, tw), 0) + pj_ref[p] * tt
        cid = lax.broadcasted_iota(I32, (tt, tw), 1) + row0
        hit = (tok_ref[...] == tid) & (cid >= lo) & (cid < hi)
        weights = jnp.where(hit, gate_ref[...], 0.0).astype(BF16)
        acc_ref[:, :half] += _dot(weights, y_lo)
        acc_ref[:, half:] += _dot(weights, y_hi)

    @pl.when((flag & 2) != 0)
    def _():
        y = acc_ref[...]
        scale = lax.rsqrt(jnp.mean(y * y, axis=-1, keepdims=True) + EPS)
        for packed, natural in col_blocks():
            o_ref[:, natural] = acc_ref[:, packed] * scale * nf_ref[:, natural]


def _combine(lists, tok, gate, yg, h, nfw, group):
    pj, pw, pf, plo, phi = lists
    T = h.shape[0]
    tw, tt = MOE_TW, MOE_TT
    nwin = yg.shape[0] // tw
    grid_spec = pltpu.PrefetchScalarGridSpec(
        num_scalar_prefetch=5,
        grid=(pj.shape[0],),
        in_specs=[
            pl.BlockSpec((None, 1, tw), lambda p, pj, pw, *_: (pw[p], 0, 0)),
            pl.BlockSpec((None, 1, tw), lambda p, pj, pw, *_: (pw[p], 0, 0)),
            pl.BlockSpec((tw, D_MODEL // 2), lambda p, pj, pw, *_: (pw[p], 0)),
            pl.BlockSpec((tt, D_MODEL), lambda p, pj, pw, *_: (pj[p], 0)),
            pl.BlockSpec((1, D_MODEL), lambda p, pj, pw, *_: (0, 0)),
        ],
        out_specs=pl.BlockSpec((tt, D_MODEL), lambda p, pj, pw, *_: (pj[p], 0)),
        scratch_shapes=[pltpu.VMEM((tt, D_MODEL), F32)],
    )
    return pl.pallas_call(
        functools.partial(_combine_kernel, group=group),
        grid_spec=grid_spec,
        out_shape=jax.ShapeDtypeStruct((T, D_MODEL), F32),
        compiler_params=_cparams(("arbitrary",), 48),
        name="combine",
    )(pj, pw, pf, plo, phi, tok.reshape(nwin, 1, tw), gate.reshape(nwin, 1, tw), yg, h, nfw)


def _rope_tables(seq_len):
    d = RET_DK
    inv = ROPE_BASE ** (-jnp.arange(0, d, 2, dtype=F32) / d)
    ang = jnp.arange(seq_len, dtype=F32)[:, None] * inv[None, :]
    return jnp.cos(ang), jnp.sin(ang)


def _chunk_tri(n, chunk, upper):
    r = np.arange(n)
    same = (r[:, None] // chunk) == (r[None, :] // chunk)
    tri = (r[:, None] <= r[None, :]) if upper else (r[:, None] >= r[None, :])
    return jnp.asarray(same & tri, BF16)


def _prep_params(norm1_w, w_in, ret_gn_w, gla_gate_up, gla_gate_bias, gla_gn_w, w_out, norm2_w, router_w,
                 normf_w):
    w = w_in[0]
    w_main = w[:, :IN_MAIN].astype(BF16)
    w_ga = jnp.pad(w[:, IN_MAIN:], ((0, 0), (0, LANE - 2 * GLA_RANK))).astype(BF16)
    cs = np.ones((1, IN_MAIN), np.float32)
    cs[:, _RQ:_RQ + RET_WIDTH] = RET_DK ** -0.5
    cs[:, _GQ:_GQ + GLA_KEY_WIDTH] = GLA_DK ** -0.5
    up = gla_gate_up[0].astype(F32)
    up_pad = jnp.zeros((LANE, 2 * GLA_KEY_WIDTH), F32)
    up_pad = up_pad.at[:GLA_RANK, :GLA_KEY_WIDTH].set(up[0])
    up_pad = up_pad.at[GLA_RANK:2 * GLA_RANK, GLA_KEY_WIDTH:].set(up[1])
    rt = router_w[0].T.astype(F32)
    r_hi = rt.astype(BF16)
    r_lo = (rt - r_hi.astype(F32)).astype(BF16)
    return dict(
        n1w=norm1_w[0].reshape(1, D_MODEL).astype(F32),
        w_main=w_main, w_ga=w_ga, colscale=jnp.asarray(cs),
        up_pad=up_pad.astype(BF16),
        bias=gla_gate_bias[0].reshape(1, 2 * GLA_KEY_WIDTH).astype(F32),
        lf=_chunk_tri(GATE_TM, GLA_CHUNK, upper=False),
        lb=_chunk_tri(GATE_TM, GLA_CHUNK, upper=True),
        ret_gn=ret_gn_w[0].reshape(1, RET_WIDTH).astype(F32),
        gla_gn=gla_gn_w[0].reshape(1, GLA_WIDTH).astype(F32),
        w_out=w_out[0].astype(BF16),
        n2w=norm2_w[0].reshape(1, D_MODEL).astype(F32),
        r_hi=r_hi, r_lo=r_lo,
        nfw=normf_w.reshape(1, D_MODEL).astype(F32),
    )


def _trunk(x, pp, decay_logit, w_gate, w_up, w_down):
    B, L, _ = x.shape
    T = B * L
    x2d = x.reshape(T, D_MODEL)
    cos, sin = _rope_tables(L)
    proj, ga = _in_proj(x2d, pp["n1w"], pp["w_main"], pp["w_ga"], pp["colscale"], cos, sin, L)
    b_f, b_b = _gla_gates(ga, pp["up_pad"], pp["bias"], pp["lf"], pp["lb"])

    fwd = _mixer_scan(proj, decay_logit, b_f, B, L, reverse=False)
    mix_r, mix_g = _mixer_scan(proj, decay_logit, b_b, B, L, reverse=True, o_fwd=fwd,
                               gn_w=(pp["ret_gn"], pp["gla_gn"]))

    h, xn2, aff = _out_proj(mix_r, mix_g, pp["w_out"], x2d, pp["n2w"], pp["r_hi"], pp["r_lo"])

    cap = CAPACITY_FACTOR * T // N_EXPERTS
    pos, posb, rankb, affb = _select(aff, cap)
    off, c_lists = _combine_schedule(posb, T)
    xe = _dispatch(pos, xn2, cap).reshape(N_EXPERTS, cap, D_MODEL // 2)
    ye = _ffn(xe, w_gate, w_up, w_down).reshape(N_EXPERTS * cap, D_MODEL // 2)
    nb = T // MOE_TT
    yg, tok, gate = _regroup(posb.reshape(nb, -1), rankb.reshape(nb, -1), affb.reshape(nb, -1), off, ye, cap,
                             _regroup_rows(T))
    y = _combine(c_lists, tok, gate, yg, h, pp["nfw"], _ffn_tile_width(cap))
    return y.reshape(B, L, D_MODEL)


def kernel(x_prompt, x_sample, norm1_w, w_in, ret_decay_logit, ret_gn_w, gla_gate_up, gla_gate_bias,
           gla_gn_w, w_out, norm2_w, router_w, w_gate, w_up, w_down, normf_w):
    pp = _prep_params(norm1_w, w_in, ret_gn_w, gla_gate_up, gla_gate_bias, gla_gn_w, w_out, norm2_w,
                      router_w, normf_w)
    decay_logit = ret_decay_logit[0].astype(F32)
    args = (pp, decay_logit, w_gate[0], w_up[0], w_down[0])
    return (_trunk(x_prompt, *args), _trunk(x_sample, *args))
```

```python
import functools

import numpy as np
import jax
import jax.numpy as jnp
from jax import lax
from jax.experimental import pallas as pl
from jax.experimental.pallas import tpu as pltpu
from jax.experimental.pallas import tpu_sc as plsc

F32, BF16, I32 = jnp.float32, jnp.bfloat16, jnp.int32

D_MODEL = 2048
RET_WIDTH = 1024
RET_HEADS = 4
RET_DK = 256
RET_DV = 256
GLA_WIDTH = 1024
GLA_HEADS = 4
GLA_DK = 128
GLA_DV = 256
GLA_KEY_WIDTH = 512
GLA_RANK = 16
GLA_TAU = 16.0
RET_CHUNK = 256
GLA_CHUNK = 64
GLA_SUB = 16
ROPE_BASE = 10000.0
N_EXPERTS = 16
CAPACITY_FACTOR = 2
D_FF = 2048
EPS = 1e-6
LOG2_E = 1.4426950408889634
IN_MAIN = 4 * RET_WIDTH + 2 * GLA_KEY_WIDTH + 2 * GLA_WIDTH

_RQ, _RK, _RV, _RG = 0, 1024, 2048, 3072
_GQ, _GK, _GV, _GG = 4096, 4608, 5120, 6144

LANE = 128
MOE_TT = 512
MOE_TW = 512
V7X_VMEM_BYTES = 64 * 1024 * 1024


def _cparams(sem, vmem_mb):
    return pltpu.CompilerParams(dimension_semantics=sem, vmem_limit_bytes=vmem_mb * 1024 * 1024)


def _log_sigmoid(z):
    return jnp.minimum(z, 0.0) - jnp.log1p(jnp.exp(-jnp.abs(z)))


def _silu(g):
    return g * (1.0 / (1.0 + jnp.exp(-g)))


def _dot_nt(a, b):
    return lax.dot_general(a, b, (((1,), (1,)), ((), ())), preferred_element_type=F32)


def _dot_tn(a, b):
    return lax.dot_general(a, b, (((0,), (0,)), ((), ())), preferred_element_type=F32)


def _dot(a, b):
    return jnp.dot(a, b, preferred_element_type=F32)


def _pack_bf16_pairs(x):
    bits = pltpu.bitcast(x.astype(BF16).astype(F32), I32)
    w = x.shape[1] // 2
    return bits[:, w:] | lax.shift_right_logical(bits[:, :w], 16)


def _unpack_bf16_pairs(words):
    lo = pltpu.bitcast(lax.shift_left(words, 16), F32)
    hi = pltpu.bitcast(words & jnp.int32(-65536), F32)
    return lo, hi


IP_TM = 1024
IP_TN = 1024


def _in_proj_kernel(x_ref, n1_ref, w_ref, wga_ref, cs_ref, cos_ref, sin_ref, o_ref, ga_ref, xn_ref):
    j = pl.program_id(1)

    @pl.when(j == 0)
    def _():
        x = x_ref[...]
        ms = jnp.mean(x * x, axis=-1, keepdims=True)
        xn = (x * lax.rsqrt(ms + EPS) * n1_ref[...]).astype(BF16)
        xn_ref[...] = xn
        ga_ref[...] = _dot(xn, wga_ref[...])

    acc = _dot(xn_ref[...], w_ref[...]) * cs_ref[...]
    n_rope_blocks = 2 * RET_WIDTH // IP_TN

    @pl.when(j < n_rope_blocks)
    def _():
        cos = cos_ref[...]
        sin = sin_ref[...]
        for h in range(IP_TN // RET_DK):
            x1 = acc[:, 2 * h * LANE:(2 * h + 1) * LANE]
            x2 = acc[:, (2 * h + 1) * LANE:(2 * h + 2) * LANE]
            o_ref[2 * h] = (x1 * cos - x2 * sin).astype(BF16)
            o_ref[2 * h + 1] = (x1 * sin + x2 * cos).astype(BF16)

    @pl.when(j >= n_rope_blocks)
    def _():
        for c in range(IP_TN // LANE):
            o_ref[c] = acc[:, c * LANE:(c + 1) * LANE].astype(BF16)


def _in_proj(x2d, n1w, w_main, w_ga, colscale, cos, sin, seq_len):
    T = x2d.shape[0]
    tm, tn = IP_TM, IP_TN
    nlb = seq_len // tm
    return pl.pallas_call(
        _in_proj_kernel,
        grid=(T // tm, IN_MAIN // tn),
        in_specs=[
            pl.BlockSpec((tm, D_MODEL), lambda i, j: (i, 0)),
            pl.BlockSpec((1, D_MODEL), lambda i, j: (0, 0)),
            pl.BlockSpec((D_MODEL, tn), lambda i, j: (0, j)),
            pl.BlockSpec((D_MODEL, LANE), lambda i, j: (0, 0)),
            pl.BlockSpec((1, tn), lambda i, j: (0, j)),
            pl.BlockSpec((tm, LANE), lambda i, j: (i % nlb, 0)),
            pl.BlockSpec((tm, LANE), lambda i, j: (i % nlb, 0)),
        ],
        out_specs=[
            pl.BlockSpec((tn // LANE, tm, LANE), lambda i, j: (j, i, 0)),
            pl.BlockSpec((tm, LANE), lambda i, j: (i, 0)),
        ],
        out_shape=[
            jax.ShapeDtypeStruct((IN_MAIN // LANE, T, LANE), BF16),
            jax.ShapeDtypeStruct((T, LANE), F32),
        ],
        scratch_shapes=[pltpu.VMEM((tm, D_MODEL), BF16)],
        compiler_params=_cparams(("parallel", "arbitrary"), 48),
        name="in_proj",
    )(x2d, n1w, w_main, w_ga, colscale, cos, sin)


GATE_TM = 512


def _gates_kernel(ga_ref, up_ref, bias_ref, lf_ref, lb_ref, bf_ref, bb_ref):
    z = _dot(ga_ref[...].astype(BF16), up_ref[...]) + bias_ref[...]
    la = _log_sigmoid(z) * (LOG2_E / GLA_TAU)
    hi = la.astype(BF16)
    lo = (la - hi.astype(F32)).astype(BF16)
    kw = GLA_KEY_WIDTH
    bf_ref[...] = _dot(lf_ref[...], hi[:, :kw]) + _dot(lf_ref[...], lo[:, :kw])
    bb_ref[...] = _dot(lb_ref[...], hi[:, kw:]) + _dot(lb_ref[...], lo[:, kw:])


def _gla_gates(ga, up_pad, bias, lf, lb):
    T = ga.shape[0]
    tm = GATE_TM
    kw = GLA_KEY_WIDTH
    return pl.pallas_call(
        _gates_kernel,
        grid=(T // tm,),
        in_specs=[
            pl.BlockSpec((tm, LANE), lambda i: (i, 0)),
            pl.BlockSpec((LANE, 2 * kw), lambda i: (0, 0)),
            pl.BlockSpec((1, 2 * kw), lambda i: (0, 0)),
            pl.BlockSpec((tm, tm), lambda i: (0, 0)),
            pl.BlockSpec((tm, tm), lambda i: (0, 0)),
        ],
        out_specs=[pl.BlockSpec((tm, kw), lambda i: (i, 0)), pl.BlockSpec((tm, kw), lambda i: (i, 0))],
        out_shape=[jax.ShapeDtypeStruct((T, kw), F32), jax.ShapeDtypeStruct((T, kw), F32)],
        compiler_params=_cparams(("parallel",), 32),
        name="gla_gates",
    )(ga, up_pad, bias, lf, lb)


def _wide(ref, rows):
    return jnp.concatenate([ref[0, rows, :], ref[1, rows, :]], axis=1)


def _finish_heads(tot, gn, gate):
    ms = jnp.mean(tot * tot, axis=-1, keepdims=True)
    yn = tot * lax.rsqrt(ms + EPS) * gn
    return (yn * _silu(gate.astype(F32))).astype(BF16)


RET_TB = 1024


def _ret_kernel(dl_ref, q_ref, k_ref, v_ref, *rest, reverse):
    if reverse:
        g_ref, of_ref, gn_ref, o_ref, s_ref, intra_ref, qd_ref, kd_ref, cd_ref, p_ref, u_ref = rest
    else:
        o_ref, s_ref, intra_ref, qd_ref, kd_ref, cd_ref, p_ref, u_ref = rest
    h = pl.program_id(1)
    n = pl.program_id(2)
    C = RET_CHUNK

    @pl.when(n == 0)
    def _():
        s_ref[...] = jnp.zeros_like(s_ref)
        logit = dl_ref[1 if reverse else 0, h]
        lg = _log_sigmoid(jnp.full((C, RET_DV), logit, F32))
        lg_c = _log_sigmoid(jnp.full((C, C), logit, F32))
        lg_r = _log_sigmoid(jnp.full((1, RET_DV), logit, F32))
        ri = lax.broadcasted_iota(I32, (C, RET_DV), 0).astype(F32)
        rc = lax.broadcasted_iota(I32, (C, C), 0).astype(F32)
        cc = lax.broadcasted_iota(I32, (C, C), 1).astype(F32)
        diff = (cc - rc) if reverse else (rc - cc)
        intra_ref[...] = jnp.where(diff >= 0, jnp.exp(lg_c * diff), 0.0)
        if reverse:
            qd_ref[...] = jnp.exp(lg * (C - ri))
            kd_ref[...] = jnp.exp(lg * ri)
        else:
            qd_ref[...] = jnp.exp(lg * (ri + 1.0))
            kd_ref[...] = jnp.exp(lg * (C - 1.0 - ri))
        cd_ref[...] = jnp.exp(lg_r * C)

    nchunks = o_ref.shape[0] // C
    for c in range(nchunks):
        rows = slice(c * C, (c + 1) * C)
        k = _wide(k_ref, rows)
        p_ref[c] = (_dot_nt(_wide(q_ref, rows), k) * intra_ref[...]).astype(BF16)
        kd = (k.astype(F32) * kd_ref[...]).astype(BF16)
        u_ref[c] = _dot_tn(kd, _wide(v_ref, rows))
    order = range(nchunks - 1, -1, -1) if reverse else range(nchunks)
    for c in order:
        rows = slice(c * C, (c + 1) * C)
        state = s_ref[...]
        o = _dot(p_ref[c], _wide(v_ref, rows)) + _dot(_wide(q_ref, rows), state.astype(BF16)) * qd_ref[...]
        s_ref[...] = state * cd_ref[...] + u_ref[c]
        if reverse:
            tot = of_ref[rows, :].astype(F32) + o
            o_ref[rows, :] = _finish_heads(tot, gn_ref[...], _wide(g_ref, rows))
        else:
            o_ref[rows, :] = o.astype(BF16)


def _ret_parts(proj, decay_logit, rb, reverse, o_fwd=None, gn_w=None):
    T = proj.shape[1]
    tb = RET_TB
    dk, dv, C = RET_DK, RET_DV, RET_CHUNK

    def head(base):
        return pl.BlockSpec((dk // LANE, tb, LANE), lambda b, h, n: (base // dk + h, rb(b, n), 0))

    in_specs = [pl.BlockSpec(memory_space=pltpu.SMEM), head(_RQ), head(_RK), head(_RV)]
    args = [decay_logit, proj, proj, proj]
    if reverse:
        in_specs += [
            head(_RG),
            pl.BlockSpec((tb, dv), lambda b, h, n: (rb(b, n), h)),
            pl.BlockSpec((1, dv), lambda b, h, n: (0, h)),
        ]
        args += [proj, o_fwd, gn_w]
    out_spec = pl.BlockSpec((tb, dv), lambda b, h, n: (rb(b, n), h))
    out_shape = jax.ShapeDtypeStruct((T, RET_WIDTH), BF16)
    scratch = [
        pltpu.VMEM((dk, dv), F32),
        pltpu.VMEM((C, C), F32),
        pltpu.VMEM((C, dv), F32),
        pltpu.VMEM((C, dk), F32),
        pltpu.VMEM((1, dv), F32),
        pltpu.VMEM((tb // C, C, C), BF16),
        pltpu.VMEM((tb // C, dk, dv), F32),
    ]
    return in_specs, args, out_spec, out_shape, scratch


GLA_TB = 1024
GLA_UNROLL = 16


GLA_LEVELS = (32, 16, 8, 4, 2, 1)
SUBLANES = 8


def _gla_tables(reverse):
    C = GLA_CHUNK
    r = np.arange(C)
    masks = np.zeros((len(GLA_LEVELS) + 1, C, C), np.float32)
    for l, s in enumerate(GLA_LEVELS):
        upper = (r & s) != 0
        same = (r[:, None] // (2 * s)) == (r[None, :] // (2 * s))
        lhs_rows = ~upper if reverse else upper
        masks[l] = same & lhs_rows[:, None] & ~lhs_rows[None, :]
    masks[-1] = np.eye(C)
    return jnp.asarray(masks, F32)


def _gla_kernel(q_ref, k_ref, v_ref, b_ref, mask_ref, *rest, reverse):
    if reverse:
        g_ref, of_ref, gn_ref, o_ref, st_ref, sc_ref = rest
    else:
        o_ref, st_ref, sc_ref = rest
    n = pl.program_id(2)
    C = GLA_CHUNK

    @pl.when(n == 0)
    def _():
        st_ref[...] = jnp.zeros_like(st_ref)

    nchunks = q_ref.shape[0] // C
    sub_row = lax.broadcasted_iota(I32, (SUBLANES, GLA_DK), 0)
    zero_rows = jnp.zeros((SUBLANES, GLA_DK), F32)

    def chunk_scores(c, carry):
        c0 = pl.multiple_of(c * C, C)
        qb = q_ref[pl.ds(c0, C), :]
        kb = k_ref[pl.ds(c0, C), :]
        q = qb.astype(F32)
        k = kb.astype(F32)
        b = b_ref[pl.ds(c0, C), :]

        def mid_row(r):
            return jnp.broadcast_to(b[r:r + 1, :], (SUBLANES, GLA_DK))

        scores = mask_ref[len(GLA_LEVELS)] * _dot_nt(qb, kb)
        for l, s in enumerate(GLA_LEVELS):
            lhs, rhs = [], []
            for g in range(C // SUBLANES):
                r0 = g * SUBLANES
                rows = slice(r0, r0 + SUBLANES)
                if s >= SUBLANES:
                    m = mid_row((r0 // (2 * s)) * (2 * s) + s)
                    is_lhs = ((r0 & s) != 0) != reverse
                    if is_lhs:
                        lhs.append(q[rows] * jnp.exp2(b[rows] - m))
                        rhs.append(zero_rows)
                    else:
                        lhs.append(zero_rows)
                        rhs.append(k[rows] * jnp.exp2(m - b[rows]))
                else:
                    m = mid_row(r0 + SUBLANES - s)
                    for blk in range(SUBLANES // (2 * s) - 2, -1, -1):
                        m = jnp.where(sub_row < (blk + 1) * 2 * s, mid_row(r0 + blk * 2 * s + s), m)
                    upper = (sub_row & s) != 0
                    is_lhs = jnp.logical_not(upper) if reverse else upper
                    lhs.append(jnp.where(is_lhs, q[rows] * jnp.exp2(b[rows] - m), 0.0))
                    rhs.append(jnp.where(is_lhs, 0.0, k[rows] * jnp.exp2(m - b[rows])))
            lhs = jnp.concatenate(lhs, axis=0).astype(BF16)
            rhs = jnp.concatenate(rhs, axis=0).astype(BF16)
            scores = scores + mask_ref[l] * _dot_nt(lhs, rhs)
        sc_ref[c] = scores.astype(BF16)
        return carry

    lax.fori_loop(0, nchunks, chunk_scores, 0, unroll=GLA_UNROLL)

    def chunk(ci, carry):
        c = (nchunks - 1 - ci) if reverse else ci
        c0 = pl.multiple_of(c * C, C)
        q = q_ref[pl.ds(c0, C), :].astype(F32)
        k = k_ref[pl.ds(c0, C), :].astype(F32)
        v = _wide(v_ref, pl.ds(c0, C))
        b = b_ref[pl.ds(c0, C), :]
        b_end = b[0:1, :] if reverse else b[C - 1:C, :]

        st = st_ref[...]
        o = _dot_nt((q * jnp.exp2(b)).astype(BF16), st.astype(BF16))
        ke = (k * jnp.exp2(b_end - b)).astype(BF16)
        st_ref[...] = st * jnp.exp2(b_end) + _dot_tn(v, ke)
        o = o + _dot(sc_ref[c], v)
        if reverse:
            tot = of_ref[pl.ds(c0, C), :].astype(F32) + o
            o_ref[pl.ds(c0, C), :] = _finish_heads(tot, gn_ref[...], _wide(g_ref, pl.ds(c0, C)))
        else:
            o_ref[pl.ds(c0, C), :] = o.astype(BF16)
        return carry

    lax.fori_loop(0, nchunks, chunk, 0, unroll=GLA_UNROLL)


def _gla_parts(proj, bcum, rb, reverse, o_fwd=None, gn_w=None):
    T = proj.shape[1]
    tb = GLA_TB
    dk, dv = GLA_DK, GLA_DV
    masks = _gla_tables(reverse)

    def key_block(base):
        return pl.BlockSpec((None, tb, LANE), lambda b, h, n: (base // dk + h, rb(b, n), 0))

    def value_block(base):
        return pl.BlockSpec((dv // LANE, tb, LANE), lambda b, h, n: (base // dv + h, rb(b, n), 0))

    in_specs = [
        key_block(_GQ), key_block(_GK), value_block(_GV),
        pl.BlockSpec((tb, dk), lambda b, h, n: (rb(b, n), h)),
        pl.BlockSpec(masks.shape, lambda b, h, n: (0, 0, 0)),
    ]
    args = [proj, proj, proj, bcum, masks]
    if reverse:
        in_specs += [
            value_block(_GG),
            pl.BlockSpec((tb, dv), lambda b, h, n: (rb(b, n), h)),
            pl.BlockSpec((1, dv), lambda b, h, n: (0, h)),
        ]
        args += [proj, o_fwd, gn_w]
    out_spec = pl.BlockSpec((tb, dv), lambda b, h, n: (rb(b, n), h))
    out_shape = jax.ShapeDtypeStruct((T, GLA_WIDTH), BF16)
    scratch = [pltpu.VMEM((dv, dk), F32), pltpu.VMEM((tb // GLA_CHUNK, GLA_CHUNK, GLA_CHUNK), BF16)]
    return in_specs, args, out_spec, out_shape, scratch


def _mixer_kernel(*refs, reverse, n_ret_in, n_gla_in, n_ret_scratch):
    ret_in = refs[:n_ret_in]
    gla_in = refs[n_ret_in:n_ret_in + n_gla_in]
    ret_out, gla_out = refs[n_ret_in + n_gla_in:n_ret_in + n_gla_in + 2]
    scratch = refs[n_ret_in + n_gla_in + 2:]
    _ret_kernel(*ret_in, ret_out, *scratch[:n_ret_scratch], reverse=reverse)
    _gla_kernel(*gla_in, gla_out, *scratch[n_ret_scratch:], reverse=reverse)


def _mixer_scan(proj, decay_logit, bcum, batch, seq_len, reverse, o_fwd=(None, None), gn_w=(None, None)):
    assert RET_TB == GLA_TB and RET_HEADS == GLA_HEADS
    nb = seq_len // RET_TB

    def rb(b, n):
        return b * nb + ((nb - 1 - n) if reverse else n)

    r_specs, r_args, r_out, r_shape, r_scratch = _ret_parts(proj, decay_logit, rb, reverse, o_fwd[0], gn_w[0])
    g_specs, g_args, g_out, g_shape, g_scratch = _gla_parts(proj, bcum, rb, reverse, o_fwd[1], gn_w[1])
    return pl.pallas_call(
        functools.partial(_mixer_kernel, reverse=reverse, n_ret_in=len(r_specs), n_gla_in=len(g_specs),
                          n_ret_scratch=len(r_scratch)),
        grid=(batch, RET_HEADS, nb),
        in_specs=r_specs + g_specs,
        out_specs=[r_out, g_out],
        out_shape=[r_shape, g_shape],
        scratch_shapes=r_scratch + g_scratch,
        compiler_params=_cparams(("parallel", "parallel", "arbitrary"), 48),
        name="mixer_bwd" if reverse else "mixer_fwd",
    )(*r_args, *g_args)


OP_TM = 512


def _out_proj_kernel(mr_ref, mg_ref, w0_ref, w1_ref, x_ref, n2_ref, rh_ref, rl_ref, h_ref, xn_ref, aff_ref,
                     hs_ref):
    s = pl.program_id(0)
    slot = s % 2

    @pl.when(s == 0)
    def _():
        hs_ref[1] = jnp.zeros(hs_ref.shape[1:], F32)

    hp = hs_ref[1 - slot]
    ms = jnp.mean(hp * hp, axis=-1, keepdims=True)
    xn = hp * lax.rsqrt(ms + EPS) * n2_ref[...]
    xh = xn.astype(BF16)
    xn_ref[...] = _pack_bf16_pairs(xn)
    xl = (xn - xh.astype(F32)).astype(BF16)
    lt = _dot_nt(rh_ref[...], xh) + _dot_nt(rh_ref[...], xl) + _dot_nt(rl_ref[...], xh)
    m = jnp.max(lt, axis=0, keepdims=True)
    e = jnp.exp(lt - m)
    aff_ref[...] = e / jnp.sum(e, axis=0, keepdims=True)

    h = x_ref[...] + _dot(mr_ref[...], w0_ref[...]) + _dot(mg_ref[...], w1_ref[...])
    h_ref[...] = h
    hs_ref[slot] = h


def _out_proj(mix_r, mix_g, w_out, x2d, n2w, r_hi, r_lo):
    T = x2d.shape[0]
    tm = OP_TM
    half = RET_WIDTH
    nblk = T // tm

    def head(s):
        return jnp.minimum(s, nblk - 1)

    def tail(s):
        return jnp.maximum(s - 1, 0)

    return pl.pallas_call(
        _out_proj_kernel,
        grid=(nblk + 1,),
        in_specs=[
            pl.BlockSpec((tm, half), lambda s: (head(s), 0)),
            pl.BlockSpec((tm, half), lambda s: (head(s), 0)),
            pl.BlockSpec((half, D_MODEL), lambda s: (0, 0)),
            pl.BlockSpec((half, D_MODEL), lambda s: (1, 0)),
            pl.BlockSpec((tm, D_MODEL), lambda s: (head(s), 0)),
            pl.BlockSpec((1, D_MODEL), lambda s: (0, 0)),
            pl.BlockSpec((N_EXPERTS, D_MODEL), lambda s: (0, 0)),
            pl.BlockSpec((N_EXPERTS, D_MODEL), lambda s: (0, 0)),
        ],
        out_specs=[
            pl.BlockSpec((tm, D_MODEL), lambda s: (head(s), 0)),
            pl.BlockSpec((tm, D_MODEL // 2), lambda s: (tail(s), 0)),
            pl.BlockSpec((N_EXPERTS, tm), lambda s: (0, tail(s))),
        ],
        out_shape=[
            jax.ShapeDtypeStruct((T, D_MODEL), F32),
            jax.ShapeDtypeStruct((T, D_MODEL // 2), I32),
            jax.ShapeDtypeStruct((N_EXPERTS, T), F32),
        ],
        scratch_shapes=[pltpu.VMEM((2, tm, D_MODEL), F32)],
        compiler_params=_cparams(("arbitrary",), 60),
        name="out_proj",
    )(mix_r, mix_g, w_out, w_out, x2d, n2w, r_hi, r_lo)


def _select_kernel(a_ref, pos_ref, posb_ref, rankb_ref, affb_ref, *, cap):
    E, T = a_ref.shape
    tt = MOE_TT

    def count(pred):
        return jnp.sum(pred.astype(F32), axis=1, keepdims=True)

    def bisect(i, tau):
        cand = tau | jnp.left_shift(jnp.int32(1), 30 - i)
        bits = pltpu.bitcast(a_ref[...], I32)
        return jnp.where(count(bits >= cand) >= cap, cand, tau)

    tau = lax.fori_loop(0, 31, bisect, jnp.zeros((E, 1), I32))
    bits_all = pltpu.bitcast(a_ref[...], I32)
    quota = cap - count(bits_all > tau)

    before = (lax.broadcasted_iota(I32, (tt, tt), 0) < lax.broadcasted_iota(I32, (tt, tt), 1)).astype(BF16)
    below = (lax.broadcasted_iota(I32, (E, E), 1) < lax.broadcasted_iota(I32, (E, E), 0)).astype(BF16)

    def block(j, carry):
        c_eq, c_sel = carry
        off = pl.multiple_of(j * tt, tt)
        aff = a_ref[:, pl.ds(off, tt)]
        bits = pltpu.bitcast(aff, I32)
        eq = bits == tau
        eqf = eq.astype(F32)
        rank_eq = _dot(eqf.astype(BF16), before) + c_eq
        sel = (bits > tau) | (eq & (rank_eq < quota))
        self_ = sel.astype(F32)
        selb = self_.astype(BF16)
        slot = _dot(selb, before) + c_sel
        pos = jnp.where(sel, slot, -1.0).astype(I32)
        pos_ref[:, pl.ds(off, tt)] = pos
        per_tok = jnp.broadcast_to(jnp.sum(self_, axis=0, keepdims=True), (E, tt))
        rank = _dot(per_tok.astype(BF16), before) + _dot(below, selb)
        posb_ref[j] = pos
        rankb_ref[j] = jnp.where(sel, rank, -1.0).astype(I32)
        affb_ref[j] = aff
        return (c_eq + jnp.sum(eqf, axis=1, keepdims=True), c_sel + jnp.sum(self_, axis=1, keepdims=True))

    zero = jnp.zeros((E, 1), F32)
    lax.fori_loop(0, T // tt, block, (zero, zero))


def _select(aff, cap):
    E, T = aff.shape
    nb = T // MOE_TT
    blk = jax.ShapeDtypeStruct((nb, E, MOE_TT), I32)
    return pl.pallas_call(
        functools.partial(_select_kernel, cap=cap),
        out_shape=[jax.ShapeDtypeStruct((E, T), I32), blk, blk, jax.ShapeDtypeStruct((nb, E, MOE_TT), F32)],
        compiler_params=pltpu.CompilerParams(vmem_limit_bytes=40 * 1024 * 1024),
        name="select",
    )(aff)


def _regroup_rows(T):
    nb = T // MOE_TT
    rows = CAPACITY_FACTOR * T + SC_GATHER_ROWS * nb
    return -(-rows // MOE_TW) * MOE_TW


def _combine_schedule(posb, T):
    nb = posb.shape[0]
    tw, g = MOE_TW, SC_GATHER_ROWS
    n = jnp.sum((posb >= 0).reshape(nb, -1), axis=1).astype(I32)
    seg = (n + g - 1) // g * g
    hi = jnp.cumsum(seg)
    lo = hi - seg
    off = jnp.concatenate([jnp.zeros((1,), I32), hi])
    nwin_max = (N_EXPERTS * MOE_TT) // tw + 1
    w0 = lo // tw
    w1 = jnp.where(seg > 0, (hi - 1) // tw, w0)
    cand = jnp.arange(nwin_max, dtype=I32)
    win = w0[:, None] + cand[None, :]
    valid = (win <= w1[:, None]).reshape(-1)
    nwin_total = _regroup_rows(T) // tw
    pmax = nb + nwin_total
    jv = jnp.broadcast_to(jnp.arange(nb, dtype=I32)[:, None], win.shape).reshape(-1)
    wv = jnp.minimum(win, nwin_total - 1).reshape(-1)
    total = jnp.sum(valid.astype(I32))
    dst = jnp.where(valid, jnp.cumsum(valid.astype(I32)) - 1, pmax)
    pj, pw = (jnp.zeros((pmax,), I32).at[dst].set(a, mode="drop") for a in (jv, wv))
    real = jnp.arange(pmax, dtype=I32) < total
    pj, pw = (jnp.where(real, a, a[total - 1]) for a in (pj, pw))
    first = jnp.concatenate([jnp.ones((1,), bool), pj[1:] != pj[:-1]])
    last = jnp.concatenate([pj[1:] != pj[:-1], jnp.ones((1,), bool)]) | (jnp.arange(pmax, dtype=I32) == total - 1)
    flag = jnp.where(real, first.astype(I32) + 2 * last.astype(I32) + 4, 0)
    return off, (pj, pw, flag, lo[pj], hi[pj])


SC_LANES = 16
SC_CORES = 2
SC_SUBCORES = 16
SC_GATHER_ROWS = 32


def _dispatch(pos, xw, cap):
    E, T = pos.shape
    W = xw.shape[1]
    G = SC_GATHER_ROWS
    part_rows = cap // SC_CORES
    mesh = plsc.VectorSubcoreMesh(core_axis_name="c", subcore_axis_name="s")

    @pl.kernel(
        out_type=jax.ShapeDtypeStruct((E * cap, W), I32),
        mesh=mesh,
        scratch_types=[pltpu.VMEM((T,), I32), pltpu.VMEM((cap,), I32), pltpu.VMEM((G, W), I32)],
        compiler_params=pltpu.CompilerParams(needs_layout_passes=False),
        name="sc_dispatch",
    )
    def run(pos_hbm, x_hbm, xe_hbm, pos_v, idx_v, buf):
        e = lax.axis_index("s")
        part = lax.axis_index("c")
        pltpu.sync_copy(pos_hbm.at[e], pos_v)
        lane = lax.iota(I32, SC_LANES)

        @pl.loop(0, T // SC_LANES)
        def _(i):
            off = pl.multiple_of(i * SC_LANES, SC_LANES)
            p = pos_v[pl.ds(off, SC_LANES)]
            plsc.store_scatter(idx_v, [p], lane + off, mask=p >= 0)

        @pl.loop(0, part_rows // G)
        def _(g):
            o = pl.multiple_of(part * part_rows + g * G, G)
            pltpu.sync_copy(x_hbm.at[idx_v.at[pl.ds(o, G)]], buf)
            pltpu.sync_copy(buf, xe_hbm.at[pl.ds(e * cap + o, G)])

    return run(pos, xw)


FFN_TM = 2048
FFN_UNPACK_ROWS = 256


def _ffn_tile_width(cap):
    return 512 if min(FFN_TM, cap) <= 1024 else 256


def _ffn_kernel(x_ref, wg_ref, wu_ref, wd_ref, o_ref, xb_ref, hid_ref, *, tw):
    s = pl.program_id(2)
    n_up = D_FF // tw

    @pl.when(s == 0)
    def _():
        half = D_MODEL // 2

        def unpack_rows(i, carry):
            r = pl.multiple_of(i * FFN_UNPACK_ROWS, FFN_UNPACK_ROWS)
            lo, hi = _unpack_bf16_pairs(x_ref[pl.ds(r, FFN_UNPACK_ROWS), :])
            xb_ref[pl.ds(r, FFN_UNPACK_ROWS), :half] = lo.astype(BF16)
            xb_ref[pl.ds(r, FFN_UNPACK_ROWS), half:] = hi.astype(BF16)
            return carry

        lax.fori_loop(0, x_ref.shape[0] // FFN_UNPACK_ROWS, unpack_rows, 0)

    @pl.when(s < n_up)
    def _():
        x = xb_ref[...]
        g = _dot(x, wg_ref[...].astype(BF16))
        u = _dot(x, wu_ref[...].astype(BF16))
        col = pl.multiple_of(s * tw, tw)
        hid_ref[:, pl.ds(col, tw)] = (_silu(g) * u).astype(BF16)

    @pl.when(s >= n_up)
    def _():
        o_ref[...] = _pack_bf16_pairs(_dot(hid_ref[...], wd_ref[...].astype(BF16)))


def _ffn(xe, w_gate, w_up, w_down):
    E, cap, _ = xe.shape
    tm = min(FFN_TM, cap)
    tw = _ffn_tile_width(cap)
    n_up, n_down = D_FF // tw, D_MODEL // tw

    def up(e, m, s):
        return (e, 0, jnp.minimum(s, n_up - 1))

    def down(s):
        return jnp.maximum(s - n_up, 0)

    return pl.pallas_call(
        functools.partial(_ffn_kernel, tw=tw),
        grid=(E, cap // tm, n_up + n_down),
        in_specs=[
            pl.BlockSpec((None, tm, D_MODEL // 2), lambda e, m, s: (e, m, 0)),
            pl.BlockSpec((None, D_MODEL, tw), up),
            pl.BlockSpec((None, D_MODEL, tw), up),
            pl.BlockSpec((None, D_FF, tw), lambda e, m, s: (e, 0, down(s))),
        ],
        out_specs=pl.BlockSpec((None, tm, tw // 2), lambda e, m, s: (e, m, down(s))),
        out_shape=jax.ShapeDtypeStruct((E, cap, D_MODEL // 2), I32),
        scratch_shapes=[pltpu.VMEM((tm, D_MODEL), BF16), pltpu.VMEM((tm, D_FF), BF16)],
        compiler_params=_cparams(("parallel", "parallel", "arbitrary"), 60),
        name="ffn",
    )(xe, w_gate, w_up, w_down)


def _regroup(posb, rankb, affb, off, yw, cap, rows):
    NB, EB = posb.shape
    W = yw.shape[1]
    G, L, tt = SC_GATHER_ROWS, SC_LANES, MOE_TT
    n_workers = SC_CORES * SC_SUBCORES
    split = max(1, n_workers // NB)
    per = -(-NB * split // n_workers)
    mesh = plsc.VectorSubcoreMesh(core_axis_name="c", subcore_axis_name="s")
    off_pad = jnp.pad(off, (0, L))

    @pl.kernel(
        out_type=(jax.ShapeDtypeStruct((rows, W), I32), jax.ShapeDtypeStruct((rows,), I32),
                  jax.ShapeDtypeStruct((rows,), F32)),
        mesh=mesh,
        scratch_types=[pltpu.VMEM((EB,), I32), pltpu.VMEM((EB,), I32), pltpu.VMEM((EB,), F32),
                       pltpu.VMEM((EB,), I32), pltpu.VMEM((EB,), I32), pltpu.VMEM((EB,), F32),
                       pltpu.VMEM((G, W), I32), pltpu.VMEM((NB + 1 + L,), I32)],
        compiler_params=pltpu.CompilerParams(needs_layout_passes=False),
        name="sc_regroup",
    )
    def run(posb_hbm, rankb_hbm, affb_hbm, off_hbm, y_hbm, yg_hbm, tok_hbm, gate_hbm,
            pos_v, rank_v, aff_v, src_v, tok_v, gate_v, buf, off_v):
        wid = lax.axis_index("c") * SC_SUBCORES + lax.axis_index("s")
        pltpu.sync_copy(off_hbm, off_v)
        lane = lax.iota(I32, L)
        zi = jnp.zeros((L,), I32)
        zf = jnp.zeros((L,), F32)

        def regroup_block(j, part):
            pltpu.sync_copy(posb_hbm.at[j], pos_v)
            pltpu.sync_copy(rankb_hbm.at[j], rank_v)
            pltpu.sync_copy(affb_hbm.at[j], aff_v)
            lo = jnp.max(plsc.load_gather(off_v, [zi + j]))
            hi = jnp.max(plsc.load_gather(off_v, [zi + j + 1]))

            @pl.loop(0, EB // L)
            def _(i):
                o = pl.multiple_of(i * L, L)
                src_v[pl.ds(o, L)] = zi
                tok_v[pl.ds(o, L)] = zi
                gate_v[pl.ds(o, L)] = zf

            @pl.loop(0, EB // L)
            def _(i):
                o = pl.multiple_of(i * L, L)
                p = pos_v[pl.ds(o, L)]
                r = rank_v[pl.ds(o, L)]
                m = p >= 0
                e = i // (tt // L)
                t0 = j * tt + (i % (tt // L)) * L
                plsc.store_scatter(src_v, [r], p + e * cap, mask=m)
                plsc.store_scatter(tok_v, [r], lane + t0, mask=m)
                plsc.store_scatter(gate_v, [r], aff_v[pl.ds(o, L)], mask=m)

            @pl.loop(part, (hi - lo) // G, step=split)
            def _(g):
                o = pl.multiple_of(g * G, G)
                dst = pl.multiple_of(lo + o, G)
                pltpu.sync_copy(y_hbm.at[src_v.at[pl.ds(o, G)]], buf)
                pltpu.sync_copy(buf, yg_hbm.at[pl.ds(dst, G)])
                pltpu.sync_copy(tok_v.at[pl.ds(o, G)], tok_hbm.at[pl.ds(dst, G)])
                pltpu.sync_copy(gate_v.at[pl.ds(o, G)], gate_hbm.at[pl.ds(dst, G)])

        @pl.loop(0, per)
        def _(k):
            unit = wid * per + k

            @pl.when(unit < NB * split)
            def _():
                regroup_block(unit // split, unit % split)

    return run(posb, rankb, affb, off_pad, yw)


def _combine_kernel(pj_ref, pw_ref, pf_ref, plo_ref, phi_ref, tok_ref, gate_ref, yg_ref, h_ref, nf_ref,
                    o_ref, acc_ref, *, group):
    p = pl.program_id(0)
    flag = pf_ref[p]
    tw, tt = MOE_TW, MOE_TT
    half = D_MODEL // 2

    hw = group // 2

    def col_blocks():
        for n in range(D_MODEL // group):
            yield slice(n * hw, (n + 1) * hw), slice(n * group, n * group + hw)
            yield slice(half + n * hw, half + (n + 1) * hw), slice(n * group + hw, (n + 1) * group)

    @pl.when((flag & 1) != 0)
    def _():
        for packed, natural in col_blocks():
            acc_ref[:, packed] = h_ref[:, natural]

    @pl.when((flag & 4) != 0)
    def _():
        lo, hi = plo_ref[p], phi_ref[p]
        row0 = pw_ref[p] * tw
        rid = lax.broadcasted_iota(I32, (tw, 1), 0) + row0
        keep = (rid >= lo) & (rid < hi)
        y_lo, y_hi = _unpack_bf16_pairs(yg_ref[...])
        y_lo = jnp.where(keep, y_lo, 0.0).astype(BF16)
        y_hi = jnp.where(keep, y_hi, 0.0).astype(BF16)
        tid = lax.broadcasted_iota(I32, (tt, tw), 0) + pj_ref[p] * tt
        cid = lax.broadcasted_iota(I32, (tt, tw), 1) + row0
        hit = (tok_ref[...] == tid) & (cid >= lo) & (cid < hi)
        weights = jnp.where(hit, gate_ref[...], 0.0).astype(BF16)
        acc_ref[:, :half] += _dot(weights, y_lo)
        acc_ref[:, half:] += _dot(weights, y_hi)

    @pl.when((flag & 2) != 0)
    def _():
        y = acc_ref[...]
        scale = lax.rsqrt(jnp.mean(y * y, axis=-1, keepdims=True) + EPS)
        for packed, natural in col_blocks():
            o_ref[:, natural] = acc_ref[:, packed] * scale * nf_ref[:, natural]


def _combine(lists, tok, gate, yg, h, nfw, group):
    pj, pw, pf, plo, phi = lists
    T = h.shape[0]
    tw, tt = MOE_TW, MOE_TT
    nwin = yg.shape[0] // tw
    grid_spec = pltpu.PrefetchScalarGridSpec(
        num_scalar_prefetch=5,
        grid=(pj.shape[0],),
        in_specs=[
            pl.BlockSpec((None, 1, tw), lambda p, pj, pw, *_: (pw[p], 0, 0)),
            pl.BlockSpec((None, 1, tw), lambda p, pj, pw, *_: (pw[p], 0, 0)),
            pl.BlockSpec((tw, D_MODEL // 2), lambda p, pj, pw, *_: (pw[p], 0)),
            pl.BlockSpec((tt, D_MODEL), lambda p, pj, pw, *_: (pj[p], 0)),
            pl.BlockSpec((1, D_MODEL), lambda p, pj, pw, *_: (0, 0)),
        ],
        out_specs=pl.BlockSpec((tt, D_MODEL), lambda p, pj, pw, *_: (pj[p], 0)),
        scratch_shapes=[pltpu.VMEM((tt, D_MODEL), F32)],
    )
    return pl.pallas_call(
        functools.partial(_combine_kernel, group=group),
        grid_spec=grid_spec,
        out_shape=jax.ShapeDtypeStruct((T, D_MODEL), F32),
        compiler_params=_cparams(("arbitrary",), 48),
        name="combine",
    )(pj, pw, pf, plo, phi, tok.reshape(nwin, 1, tw), gate.reshape(nwin, 1, tw), yg, h, nfw)


def _rope_tables(seq_len):
    d = RET_DK
    inv = ROPE_BASE ** (-jnp.arange(0, d, 2, dtype=F32) / d)
    ang = jnp.arange(seq_len, dtype=F32)[:, None] * inv[None, :]
    return jnp.cos(ang), jnp.sin(ang)


def _chunk_tri(n, chunk, upper):
    r = np.arange(n)
    same = (r[:, None] // chunk) == (r[None, :] // chunk)
    tri = (r[:, None] <= r[None, :]) if upper else (r[:, None] >= r[None, :])
    return jnp.asarray(same & tri, BF16)


def _prep_params(norm1_w, w_in, ret_gn_w, gla_gate_up, gla_gate_bias, gla_gn_w, w_out, norm2_w, router_w,
                 normf_w):
    w = w_in[0]
    w_main = w[:, :IN_MAIN].astype(BF16)
    w_ga = jnp.pad(w[:, IN_MAIN:], ((0, 0), (0, LANE - 2 * GLA_RANK))).astype(BF16)
    cs = np.ones((1, IN_MAIN), np.float32)
    cs[:, _RQ:_RQ + RET_WIDTH] = RET_DK ** -0.5
    cs[:, _GQ:_GQ + GLA_KEY_WIDTH] = GLA_DK ** -0.5
    up = gla_gate_up[0].astype(F32)
    up_pad = jnp.zeros((LANE, 2 * GLA_KEY_WIDTH), F32)
    up_pad = up_pad.at[:GLA_RANK, :GLA_KEY_WIDTH].set(up[0])
    up_pad = up_pad.at[GLA_RANK:2 * GLA_RANK, GLA_KEY_WIDTH:].set(up[1])
    rt = router_w[0].T.astype(F32)
    r_hi = rt.astype(BF16)
    r_lo = (rt - r_hi.astype(F32)).astype(BF16)
    return dict(
        n1w=norm1_w[0].reshape(1, D_MODEL).astype(F32),
        w_main=w_main, w_ga=w_ga, colscale=jnp.asarray(cs),
        up_pad=up_pad.astype(BF16),
        bias=gla_gate_bias[0].reshape(1, 2 * GLA_KEY_WIDTH).astype(F32),
        lf=_chunk_tri(GATE_TM, GLA_CHUNK, upper=False),
        lb=_chunk_tri(GATE_TM, GLA_CHUNK, upper=True),
        ret_gn=ret_gn_w[0].reshape(1, RET_WIDTH).astype(F32),
        gla_gn=gla_gn_w[0].reshape(1, GLA_WIDTH).astype(F32),
        w_out=w_out[0].astype(BF16),
        n2w=norm2_w[0].reshape(1, D_MODEL).astype(F32),
        r_hi=r_hi, r_lo=r_lo,
        nfw=normf_w.reshape(1, D_MODEL).astype(F32),
    )


def _trunk(x, pp, decay_logit, w_gate, w_up, w_down):
    B, L, _ = x.shape
    T = B * L
    x2d = x.reshape(T, D_MODEL)
    cos, sin = _rope_tables(L)
    proj, ga = _in_proj(x2d, pp["n1w"], pp["w_main"], pp["w_ga"], pp["colscale"], cos, sin, L)
    b_f, b_b = _gla_gates(ga, pp["up_pad"], pp["bias"], pp["lf"], pp["lb"])

    fwd = _mixer_scan(proj, decay_logit, b_f, B, L, reverse=False)
    mix_r, mix_g = _mixer_scan(proj, decay_logit, b_b, B, L, reverse=True, o_fwd=fwd,
                               gn_w=(pp["ret_gn"], pp["gla_gn"]))

    h, xn2, aff = _out_proj(mix_r, mix_g, pp["w_out"], x2d, pp["n2w"], pp["r_hi"], pp["r_lo"])

    cap = CAPACITY_FACTOR * T // N_EXPERTS
    pos, posb, rankb, affb = _select(aff, cap)
    off, c_lists = _combine_schedule(posb, T)
    xe = _dispatch(pos, xn2, cap).reshape(N_EXPERTS, cap, D_MODEL // 2)
    ye = _ffn(xe, w_gate, w_up, w_down).reshape(N_EXPERTS * cap, D_MODEL // 2)
    nb = T // MOE_TT
    yg, tok, gate = _regroup(posb.reshape(nb, -1), rankb.reshape(nb, -1), affb.reshape(nb, -1), off, ye, cap,
                             _regroup_rows(T))
    y = _combine(c_lists, tok, gate, yg, h, pp["nfw"], _ffn_tile_width(cap))
    return y.reshape(B, L, D_MODEL)


def kernel(x_prompt, x_sample, norm1_w, w_in, ret_decay_logit, ret_gn_w, gla_gate_up, gla_gate_bias,
           gla_gn_w, w_out, norm2_w, router_w, w_gate, w_up, w_down, normf_w):
    pp = _prep_params(norm1_w, w_in, ret_gn_w, gla_gate_up, gla_gate_bias, gla_gn_w, w_out, norm2_w,
                      router_w, normf_w)
    decay_logit = ret_decay_logit[0].astype(F32)
    args = (pp, decay_logit, w_gate[0], w_up[0], w_down[0])
    return (_trunk(x_prompt, *args), _trunk(x_sample, *args))
```

```python
import functools

import numpy as np
import jax
import jax.numpy as jnp
from jax import lax
from jax.experimental import pallas as pl
from jax.experimental.pallas import tpu as pltpu
from jax.experimental.pallas import tpu_sc as plsc

F32, BF16, I32 = jnp.float32, jnp.bfloat16, jnp.int32

D_MODEL = 2048
RET_WIDTH = 1024
RET_HEADS = 4
RET_DK = 256
RET_DV = 256
GLA_WIDTH = 1024
GLA_HEADS = 4
GLA_DK = 128
GLA_DV = 256
GLA_KEY_WIDTH = 512
GLA_RANK = 16
GLA_TAU = 16.0
RET_CHUNK = 256
GLA_CHUNK = 64
GLA_SUB = 16
ROPE_BASE = 10000.0
N_EXPERTS = 16
CAPACITY_FACTOR = 2
D_FF = 2048
EPS = 1e-6
LOG2_E = 1.4426950408889634
IN_MAIN = 4 * RET_WIDTH + 2 * GLA_KEY_WIDTH + 2 * GLA_WIDTH

_RQ, _RK, _RV, _RG = 0, 1024, 2048, 3072
_GQ, _GK, _GV, _GG = 4096, 4608, 5120, 6144

LANE = 128
MOE_TT = 512
MOE_TW = 512
V7X_VMEM_BYTES = 64 * 1024 * 1024


def _cparams(sem, vmem_mb):
    return pltpu.CompilerParams(dimension_semantics=sem, vmem_limit_bytes=vmem_mb * 1024 * 1024)


def _log_sigmoid(z):
    return jnp.minimum(z, 0.0) - jnp.log1p(jnp.exp(-jnp.abs(z)))


def _silu(g):
    return g * (1.0 / (1.0 + jnp.exp(-g)))


def _dot_nt(a, b):
    return lax.dot_general(a, b, (((1,), (1,)), ((), ())), preferred_element_type=F32)


def _dot_tn(a, b):
    return lax.dot_general(a, b, (((0,), (0,)), ((), ())), preferred_element_type=F32)


def _dot(a, b):
    return jnp.dot(a, b, preferred_element_type=F32)


def _pack_bf16_pairs(x):
    bits = pltpu.bitcast(x.astype(BF16).astype(F32), I32)
    w = x.shape[1] // 2
    return bits[:, w:] | lax.shift_right_logical(bits[:, :w], 16)


def _unpack_bf16_pairs(words):
    lo = pltpu.bitcast(lax.shift_left(words, 16), F32)
    hi = pltpu.bitcast(words & jnp.int32(-65536), F32)
    return lo, hi


IP_TM = 1024
IP_TN = 1024


def _in_proj_kernel(x_ref, n1_ref, w_ref, wga_ref, cs_ref, cos_ref, sin_ref, o_ref, ga_ref, xn_ref):
    j = pl.program_id(1)

    @pl.when(j == 0)
    def _():
        x = x_ref[...]
        ms = jnp.mean(x * x, axis=-1, keepdims=True)
        xn = (x * lax.rsqrt(ms + EPS) * n1_ref[...]).astype(BF16)
        xn_ref[...] = xn
        ga_ref[...] = _dot(xn, wga_ref[...])

    acc = _dot(xn_ref[...], w_ref[...]) * cs_ref[...]
    n_rope_blocks = 2 * RET_WIDTH // IP_TN

    @pl.when(j < n_rope_blocks)
    def _():
        cos = cos_ref[...]
        sin = sin_ref[...]
        for h in range(IP_TN // RET_DK):
            x1 = acc[:, 2 * h * LANE:(2 * h + 1) * LANE]
            x2 = acc[:, (2 * h + 1) * LANE:(2 * h + 2) * LANE]
            o_ref[2 * h] = (x1 * cos - x2 * sin).astype(BF16)
            o_ref[2 * h + 1] = (x1 * sin + x2 * cos).astype(BF16)

    @pl.when(j >= n_rope_blocks)
    def _():
        for c in range(IP_TN // LANE):
            o_ref[c] = acc[:, c * LANE:(c + 1) * LANE].astype(BF16)


def _in_proj(x2d, n1w, w_main, w_ga, colscale, cos, sin, seq_len):
    T = x2d.shape[0]
    tm, tn = IP_TM, IP_TN
    nlb = seq_len // tm
    return pl.pallas_call(
        _in_proj_kernel,
        grid=(T // tm, IN_MAIN // tn),
        in_specs=[
            pl.BlockSpec((tm, D_MODEL), lambda i, j: (i, 0)),
            pl.BlockSpec((1, D_MODEL), lambda i, j: (0, 0)),
            pl.BlockSpec((D_MODEL, tn), lambda i, j: (0, j)),
            pl.BlockSpec((D_MODEL, LANE), lambda i, j: (0, 0)),
            pl.BlockSpec((1, tn), lambda i, j: (0, j)),
            pl.BlockSpec((tm, LANE), lambda i, j: (i % nlb, 0)),
            pl.BlockSpec((tm, LANE), lambda i, j: (i % nlb, 0)),
        ],
        out_specs=[
            pl.BlockSpec((tn // LANE, tm, LANE), lambda i, j: (j, i, 0)),
            pl.BlockSpec((tm, LANE), lambda i, j: (i, 0)),
        ],
        out_shape=[
            jax.ShapeDtypeStruct((IN_MAIN // LANE, T, LANE), BF16),
            jax.ShapeDtypeStruct((T, LANE), F32),
        ],
        scratch_shapes=[pltpu.VMEM((tm, D_MODEL), BF16)],
        compiler_params=_cparams(("parallel", "arbitrary"), 48),
        name="in_proj",
    )(x2d, n1w, w_main, w_ga, colscale, cos, sin)


GATE_TM = 512


def _gates_kernel(ga_ref, up_ref, bias_ref, lf_ref, lb_ref, bf_ref, bb_ref):
    z = _dot(ga_ref[...].astype(BF16), up_ref[...]) + bias_ref[...]
    la = _log_sigmoid(z) * (LOG2_E / GLA_TAU)
    hi = la.astype(BF16)
    lo = (la - hi.astype(F32)).astype(BF16)
    kw = GLA_KEY_WIDTH
    bf_ref[...] = _dot(lf_ref[...], hi[:, :kw]) + _dot(lf_ref[...], lo[:, :kw])
    bb_ref[...] = _dot(lb_ref[...], hi[:, kw:]) + _dot(lb_ref[...], lo[:, kw:])


def _gla_gates(ga, up_pad, bias, lf, lb):
    T = ga.shape[0]
    tm = GATE_TM
    kw = GLA_KEY_WIDTH
    return pl.pallas_call(
        _gates_kernel,
        grid=(T // tm,),
        in_specs=[
            pl.BlockSpec((tm, LANE), lambda i: (i, 0)),
            pl.BlockSpec((LANE, 2 * kw), lambda i: (0, 0)),
            pl.BlockSpec((1, 2 * kw), lambda i: (0, 0)),
            pl.BlockSpec((tm, tm), lambda i: (0, 0)),
            pl.BlockSpec((tm, tm), lambda i: (0, 0)),
        ],
        out_specs=[pl.BlockSpec((tm, kw), lambda i: (i, 0)), pl.BlockSpec((tm, kw), lambda i: (i, 0))],
        out_shape=[jax.ShapeDtypeStruct((T, kw), F32), jax.ShapeDtypeStruct((T, kw), F32)],
        compiler_params=_cparams(("parallel",), 32),
        name="gla_gates",
    )(ga, up_pad, bias, lf, lb)


def _wide(ref, rows):
    return jnp.concatenate([ref[0, rows, :], ref[1, rows, :]], axis=1)


def _finish_heads(tot, gn, gate):
    ms = jnp.mean(tot * tot, axis=-1, keepdims=True)
    yn = tot * lax.rsqrt(ms + EPS) * gn
    return (yn * _silu(gate.astype(F32))).astype(BF16)


RET_TB = 1024


def _ret_kernel(dl_ref, q_ref, k_ref, v_ref, *rest, reverse):
    if reverse:
        g_ref, of_ref, gn_ref, o_ref, s_ref, intra_ref, qd_ref, kd_ref, cd_ref, p_ref, u_ref = rest
    else:
        o_ref, s_ref, intra_ref, qd_ref, kd_ref, cd_ref, p_ref, u_ref = rest
    h = pl.program_id(1)
    n = pl.program_id(2)
    C = RET_CHUNK

    @pl.when(n == 0)
    def _():
        s_ref[...] = jnp.zeros_like(s_ref)
        logit = dl_ref[1 if reverse else 0, h]
        lg = _log_sigmoid(jnp.full((C, RET_DV), logit, F32))
        lg_c = _log_sigmoid(jnp.full((C, C), logit, F32))
        lg_r = _log_sigmoid(jnp.full((1, RET_DV), logit, F32))
        ri = lax.broadcasted_iota(I32, (C, RET_DV), 0).astype(F32)
        rc = lax.broadcasted_iota(I32, (C, C), 0).astype(F32)
        cc = lax.broadcasted_iota(I32, (C, C), 1).astype(F32)
        diff = (cc - rc) if reverse else (rc - cc)
        intra_ref[...] = jnp.where(diff >= 0, jnp.exp(lg_c * diff), 0.0)
        if reverse:
            qd_ref[...] = jnp.exp(lg * (C - ri))
            kd_ref[...] = jnp.exp(lg * ri)
        else:
            qd_ref[...] = jnp.exp(lg * (ri + 1.0))
            kd_ref[...] = jnp.exp(lg * (C - 1.0 - ri))
        cd_ref[...] = jnp.exp(lg_r * C)

    nchunks = o_ref.shape[0] // C
    for c in range(nchunks):
        rows = slice(c * C, (c + 1) * C)
        k = _wide(k_ref, rows)
        p_ref[c] = (_dot_nt(_wide(q_ref, rows), k) * intra_ref[...]).astype(BF16)
        kd = (k.astype(F32) * kd_ref[...]).astype(BF16)
        u_ref[c] = _dot_tn(kd, _wide(v_ref, rows))
    order = range(nchunks - 1, -1, -1) if reverse else range(nchunks)
    for c in order:
        rows = slice(c * C, (c + 1) * C)
        state = s_ref[...]
        o = _dot(p_ref[c], _wide(v_ref, rows)) + _dot(_wide(q_ref, rows), state.astype(BF16)) * qd_ref[...]
        s_ref[...] = state * cd_ref[...] + u_ref[c]
        if reverse:
            tot = of_ref[rows, :].astype(F32) + o
            o_ref[rows, :] = _finish_heads(tot, gn_ref[...], _wide(g_ref, rows))
        else:
            o_ref[rows, :] = o.astype(BF16)


def _ret_parts(proj, decay_logit, rb, reverse, o_fwd=None, gn_w=None):
    T = proj.shape[1]
    tb = RET_TB
    dk, dv, C = RET_DK, RET_DV, RET_CHUNK

    def head(base):
        return pl.BlockSpec((dk // LANE, tb, LANE), lambda b, h, n: (base // dk + h, rb(b, n), 0))

    in_specs = [pl.BlockSpec(memory_space=pltpu.SMEM), head(_RQ), head(_RK), head(_RV)]
    args = [decay_logit, proj, proj, proj]
    if reverse:
        in_specs += [
            head(_RG),
            pl.BlockSpec((tb, dv), lambda b, h, n: (rb(b, n), h)),
            pl.BlockSpec((1, dv), lambda b, h, n: (0, h)),
        ]
        args += [proj, o_fwd, gn_w]
    out_spec = pl.BlockSpec((tb, dv), lambda b, h, n: (rb(b, n), h))
    out_shape = jax.ShapeDtypeStruct((T, RET_WIDTH), BF16)
    scratch = [
        pltpu.VMEM((dk, dv), F32),
        pltpu.VMEM((C, C), F32),
        pltpu.VMEM((C, dv), F32),
        pltpu.VMEM((C, dk), F32),
        pltpu.VMEM((1, dv), F32),
        pltpu.VMEM((tb // C, C, C), BF16),
        pltpu.VMEM((tb // C, dk, dv), F32),
    ]
    return in_specs, args, out_spec, out_shape, scratch


GLA_TB = 1024
GLA_UNROLL = 16


GLA_LEVELS = (32, 16, 8, 4, 2, 1)
SUBLANES = 8


def _gla_tables(reverse):
    C = GLA_CHUNK
    r = np.arange(C)
    masks = np.zeros((len(GLA_LEVELS) + 1, C, C), np.float32)
    for l, s in enumerate(GLA_LEVELS):
        upper = (r & s) != 0
        same = (r[:, None] // (2 * s)) == (r[None, :] // (2 * s))
        lhs_rows = ~upper if reverse else upper
        masks[l] = same & lhs_rows[:, None] & ~lhs_rows[None, :]
    masks[-1] = np.eye(C)
    return jnp.asarray(masks, F32)


def _gla_kernel(q_ref, k_ref, v_ref, b_ref, mask_ref, *rest, reverse):
    if reverse:
        g_ref, of_ref, gn_ref, o_ref, st_ref, sc_ref = rest
    else:
        o_ref, st_ref, sc_ref = rest
    n = pl.program_id(2)
    C = GLA_CHUNK

    @pl.when(n == 0)
    def _():
        st_ref[...] = jnp.zeros_like(st_ref)

    nchunks = q_ref.shape[0] // C
    sub_row = lax.broadcasted_iota(I32, (SUBLANES, GLA_DK), 0)
    zero_rows = jnp.zeros((SUBLANES, GLA_DK), F32)

    def chunk_scores(c, carry):
        c0 = pl.multiple_of(c * C, C)
        qb = q_ref[pl.ds(c0, C), :]
        kb = k_ref[pl.ds(c0, C), :]
        q = qb.astype(F32)
        k = kb.astype(F32)
        b = b_ref[pl.ds(c0, C), :]

        def mid_row(r):
            return jnp.broadcast_to(b[r:r + 1, :], (SUBLANES, GLA_DK))

        scores = mask_ref[len(GLA_LEVELS)] * _dot_nt(qb, kb)
        for l, s in enumerate(GLA_LEVELS):
            lhs, rhs = [], []
            for g in range(C // SUBLANES):
                r0 = g * SUBLANES
                rows = slice(r0, r0 + SUBLANES)
                if s >= SUBLANES:
                    m = mid_row((r0 // (2 * s)) * (2 * s) + s)
                    is_lhs = ((r0 & s) != 0) != reverse
                    if is_lhs:
                        lhs.append(q[rows] * jnp.exp2(b[rows] - m))
                        rhs.append(zero_rows)
                    else:
                        lhs.append(zero_rows)
                        rhs.append(k[rows] * jnp.exp2(m - b[rows]))
                else:
                    m = mid_row(r0 + SUBLANES - s)
                    for blk in range(SUBLANES // (2 * s) - 2, -1, -1):
                        m = jnp.where(sub_row < (blk + 1) * 2 * s, mid_row(r0 + blk * 2 * s + s), m)
                    upper = (sub_row & s) != 0
                    is_lhs = jnp.logical_not(upper) if reverse else upper
                    lhs.append(jnp.where(is_lhs, q[rows] * jnp.exp2(b[rows] - m), 0.0))
                    rhs.append(jnp.where(is_lhs, 0.0, k[rows] * jnp.exp2(m - b[rows])))
            lhs = jnp.concatenate(lhs, axis=0).astype(BF16)
            rhs = jnp.concatenate(rhs, axis=0).astype(BF16)
            scores = scores + mask_ref[l] * _dot_nt(lhs, rhs)
        sc_ref[c] = scores.astype(BF16)
        return carry

    lax.fori_loop(0, nchunks, chunk_scores, 0, unroll=GLA_UNROLL)

    def chunk(ci, carry):
        c = (nchunks - 1 - ci) if reverse else ci
        c0 = pl.multiple_of(c * C, C)
        q = q_ref[pl.ds(c0, C), :].astype(F32)
        k = k_ref[pl.ds(c0, C), :].astype(F32)
        v = _wide(v_ref, pl.ds(c0, C))
        b = b_ref[pl.ds(c0, C), :]
        b_end = b[0:1, :] if reverse else b[C - 1:C, :]

        st = st_ref[...]
        o = _dot_nt((q * jnp.exp2(b)).astype(BF16), st.astype(BF16))
        ke = (k * jnp.exp2(b_end - b)).astype(BF16)
        st_ref[...] = st * jnp.exp2(b_end) + _dot_tn(v, ke)
        o = o + _dot(sc_ref[c], v)
        if reverse:
            tot = of_ref[pl.ds(c0, C), :].astype(F32) + o
            o_ref[pl.ds(c0, C), :] = _finish_heads(tot, gn_ref[...], _wide(g_ref, pl.ds(c0, C)))
        else:
            o_ref[pl.ds(c0, C), :] = o.astype(BF16)
        return carry

    lax.fori_loop(0, nchunks, chunk, 0, unroll=GLA_UNROLL)


def _gla_parts(proj, bcum, rb, reverse, o_fwd=None, gn_w=None):
    T = proj.shape[1]
    tb = GLA_TB
    dk, dv = GLA_DK, GLA_DV
    masks = _gla_tables(reverse)

    def key_block(base):
        return pl.BlockSpec((None, tb, LANE), lambda b, h, n: (base // dk + h, rb(b, n), 0))

    def value_block(base):
        return pl.BlockSpec((dv // LANE, tb, LANE), lambda b, h, n: (base // dv + h, rb(b, n), 0))

    in_specs = [
        key_block(_GQ), key_block(_GK), value_block(_GV),
        pl.BlockSpec((tb, dk), lambda b, h, n: (rb(b, n), h)),
        pl.BlockSpec(masks.shape, lambda b, h, n: (0, 0, 0)),
    ]
    args = [proj, proj, proj, bcum, masks]
    if reverse:
        in_specs += [
            value_block(_GG),
            pl.BlockSpec((tb, dv), lambda b, h, n: (rb(b, n), h)),
            pl.BlockSpec((1, dv), lambda b, h, n: (0, h)),
        ]
        args += [proj, o_fwd, gn_w]
    out_spec = pl.BlockSpec((tb, dv), lambda b, h, n: (rb(b, n), h))
    out_shape = jax.ShapeDtypeStruct((T, GLA_WIDTH), BF16)
    scratch = [pltpu.VMEM((dv, dk), F32), pltpu.VMEM((tb // GLA_CHUNK, GLA_CHUNK, GLA_CHUNK), BF16)]
    return in_specs, args, out_spec, out_shape, scratch


def _mixer_kernel(*refs, reverse, n_ret_in, n_gla_in, n_ret_scratch):
    ret_in = refs[:n_ret_in]
    gla_in = refs[n_ret_in:n_ret_in + n_gla_in]
    ret_out, gla_out = refs[n_ret_in + n_gla_in:n_ret_in + n_gla_in + 2]
    scratch = refs[n_ret_in + n_gla_in + 2:]
    _ret_kernel(*ret_in, ret_out, *scratch[:n_ret_scratch], reverse=reverse)
    _gla_kernel(*gla_in, gla_out, *scratch[n_ret_scratch:], reverse=reverse)


def _mixer_scan(proj, decay_logit, bcum, batch, seq_len, reverse, o_fwd=(None, None), gn_w=(None, None)):
    assert RET_TB == GLA_TB and RET_HEADS == GLA_HEADS
    nb = seq_len // RET_TB

    def rb(b, n):
        return b * nb + ((nb - 1 - n) if reverse else n)

    r_specs, r_args, r_out, r_shape, r_scratch = _ret_parts(proj, decay_logit, rb, reverse, o_fwd[0], gn_w[0])
    g_specs, g_args, g_out, g_shape, g_scratch = _gla_parts(proj, bcum, rb, reverse, o_fwd[1], gn_w[1])
    return pl.pallas_call(
        functools.partial(_mixer_kernel, reverse=reverse, n_ret_in=len(r_specs), n_gla_in=len(g_specs),
                          n_ret_scratch=len(r_scratch)),
        grid=(batch, RET_HEADS, nb),
        in_specs=r_specs + g_specs,
        out_specs=[r_out, g_out],
        out_shape=[r_shape, g_shape],
        scratch_shapes=r_scratch + g_scratch,
        compiler_params=_cparams(("parallel", "parallel", "arbitrary"), 48),
        name="mixer_bwd" if reverse else "mixer_fwd",
    )(*r_args, *g_args)


OP_TM = 512


def _out_proj_kernel(mr_ref, mg_ref, w0_ref, w1_ref, x_ref, n2_ref, rh_ref, rl_ref, h_ref, xn_ref, aff_ref,
                     hs_ref):
    s = pl.program_id(0)
    slot = s % 2

    @pl.when(s == 0)
    def _():
        hs_ref[1] = jnp.zeros(hs_ref.shape[1:], F32)

    hp = hs_ref[1 - slot]
    ms = jnp.mean(hp * hp, axis=-1, keepdims=True)
    xn = hp * lax.rsqrt(ms + EPS) * n2_ref[...]
    xh = xn.astype(BF16)
    xn_ref[...] = _pack_bf16_pairs(xn)
    xl = (xn - xh.astype(F32)).astype(BF16)
    lt = _dot_nt(rh_ref[...], xh) + _dot_nt(rh_ref[...], xl) + _dot_nt(rl_ref[...], xh)
    m = jnp.max(lt, axis=0, keepdims=True)
    e = jnp.exp(lt - m)
    aff_ref[...] = e / jnp.sum(e, axis=0, keepdims=True)

    h = x_ref[...] + _dot(mr_ref[...], w0_ref[...]) + _dot(mg_ref[...], w1_ref[...])
    h_ref[...] = h
    hs_ref[slot] = h


def _out_proj(mix_r, mix_g, w_out, x2d, n2w, r_hi, r_lo):
    T = x2d.shape[0]
    tm = OP_TM
    half = RET_WIDTH
    nblk = T // tm

    def head(s):
        return jnp.minimum(s, nblk - 1)

    def tail(s):
        return jnp.maximum(s - 1, 0)

    return pl.pallas_call(
        _out_proj_kernel,
        grid=(nblk + 1,),
        in_specs=[
            pl.BlockSpec((tm, half), lambda s: (head(s), 0)),
            pl.BlockSpec((tm, half), lambda s: (head(s), 0)),
            pl.BlockSpec((half, D_MODEL), lambda s: (0, 0)),
            pl.BlockSpec((half, D_MODEL), lambda s: (1, 0)),
            pl.BlockSpec((tm, D_MODEL), lambda s: (head(s), 0)),
            pl.BlockSpec((1, D_MODEL), lambda s: (0, 0)),
            pl.BlockSpec((N_EXPERTS, D_MODEL), lambda s: (0, 0)),
            pl.BlockSpec((N_EXPERTS, D_MODEL), lambda s: (0, 0)),
        ],
        out_specs=[
            pl.BlockSpec((tm, D_MODEL), lambda s: (head(s), 0)),
            pl.BlockSpec((tm, D_MODEL // 2), lambda s: (tail(s), 0)),
            pl.BlockSpec((N_EXPERTS, tm), lambda s: (0, tail(s))),
        ],
        out_shape=[
            jax.ShapeDtypeStruct((T, D_MODEL), F32),
            jax.ShapeDtypeStruct((T, D_MODEL // 2), I32),
            jax.ShapeDtypeStruct((N_EXPERTS, T), F32),
        ],
        scratch_shapes=[pltpu.VMEM((2, tm, D_MODEL), F32)],
        compiler_params=_cparams(("arbitrary",), 60),
        name="out_proj",
    )(mix_r, mix_g, w_out, w_out, x2d, n2w, r_hi, r_lo)


def _select_kernel(a_ref, pos_ref, posb_ref, rankb_ref, affb_ref, *, cap):
    E, T = a_ref.shape
    tt = MOE_TT

    def count(pred):
        return jnp.sum(pred.astype(F32), axis=1, keepdims=True)

    def bisect(i, tau):
        cand = tau | jnp.left_shift(jnp.int32(1), 30 - i)
        bits = pltpu.bitcast(a_ref[...], I32)
        return jnp.where(count(bits >= cand) >= cap, cand, tau)

    tau = lax.fori_loop(0, 31, bisect, jnp.zeros((E, 1), I32))
    bits_all = pltpu.bitcast(a_ref[...], I32)
    quota = cap - count(bits_all > tau)

    before = (lax.broadcasted_iota(I32, (tt, tt), 0) < lax.broadcasted_iota(I32, (tt, tt), 1)).astype(BF16)
    below = (lax.broadcasted_iota(I32, (E, E), 1) < lax.broadcasted_iota(I32, (E, E), 0)).astype(BF16)

    def block(j, carry):
        c_eq, c_sel = carry
        off = pl.multiple_of(j * tt, tt)
        aff = a_ref[:, pl.ds(off, tt)]
        bits = pltpu.bitcast(aff, I32)
        eq = bits == tau
        eqf = eq.astype(F32)
        rank_eq = _dot(eqf.astype(BF16), before) + c_eq
        sel = (bits > tau) | (eq & (rank_eq < quota))
        self_ = sel.astype(F32)
        selb = self_.astype(BF16)
        slot = _dot(selb, before) + c_sel
        pos = jnp.where(sel, slot, -1.0).astype(I32)
        pos_ref[:, pl.ds(off, tt)] = pos
        per_tok = jnp.broadcast_to(jnp.sum(self_, axis=0, keepdims=True), (E, tt))
        rank = _dot(per_tok.astype(BF16), before) + _dot(below, selb)
        posb_ref[j] = pos
        rankb_ref[j] = jnp.where(sel, rank, -1.0).astype(I32)
        affb_ref[j] = aff
        return (c_eq + jnp.sum(eqf, axis=1, keepdims=True), c_sel + jnp.sum(self_, axis=1, keepdims=True))

    zero = jnp.zeros((E, 1), F32)
    lax.fori_loop(0, T // tt, block, (zero, zero))


def _select(aff, cap):
    E, T = aff.shape
    nb = T // MOE_TT
    blk = jax.ShapeDtypeStruct((nb, E, MOE_TT), I32)
    return pl.pallas_call(
        functools.partial(_select_kernel, cap=cap),
        out_shape=[jax.ShapeDtypeStruct((E, T), I32), blk, blk, jax.ShapeDtypeStruct((nb, E, MOE_TT), F32)],
        compiler_params=pltpu.CompilerParams(vmem_limit_bytes=40 * 1024 * 1024),
        name="select",
    )(aff)


def _regroup_rows(T):
    nb = T // MOE_TT
    rows = CAPACITY_FACTOR * T + SC_GATHER_ROWS * nb
    return -(-rows // MOE_TW) * MOE_TW


def _combine_schedule(posb, T):
    nb = posb.shape[0]
    tw, g = MOE_TW, SC_GATHER_ROWS
    n = jnp.sum((posb >= 0).reshape(nb, -1), axis=1).astype(I32)
    seg = (n + g - 1) // g * g
    hi = jnp.cumsum(seg)
    lo = hi - seg
    off = jnp.concatenate([jnp.zeros((1,), I32), hi])
    nwin_max = (N_EXPERTS * MOE_TT) // tw + 1
    w0 = lo // tw
    w1 = jnp.where(seg > 0, (hi - 1) // tw, w0)
    cand = jnp.arange(nwin_max, dtype=I32)
    win = w0[:, None] + cand[None, :]
    valid = (win <= w1[:, None]).reshape(-1)
    nwin_total = _regroup_rows(T) // tw
    pmax = nb + nwin_total
    jv = jnp.broadcast_to(jnp.arange(nb, dtype=I32)[:, None], win.shape).reshape(-1)
    wv = jnp.minimum(win, nwin_total - 1).reshape(-1)
    total = jnp.sum(valid.astype(I32))
    dst = jnp.where(valid, jnp.cumsum(valid.astype(I32)) - 1, pmax)
    pj, pw = (jnp.zeros((pmax,), I32).at[dst].set(a, mode="drop") for a in (jv, wv))
    real = jnp.arange(pmax, dtype=I32) < total
    pj, pw = (jnp.where(real, a, a[total - 1]) for a in (pj, pw))
    first = jnp.concatenate([jnp.ones((1,), bool), pj[1:] != pj[:-1]])
    last = jnp.concatenate([pj[1:] != pj[:-1], jnp.ones((1,), bool)]) | (jnp.arange(pmax, dtype=I32) == total - 1)
    flag = jnp.where(real, first.astype(I32) + 2 * last.astype(I32) + 4, 0)
    return off, (pj, pw, flag, lo[pj], hi[pj])


SC_LANES = 16
SC_CORES = 2
SC_SUBCORES = 16
SC_GATHER_ROWS = 32


def _dispatch(pos, xw, cap):
    E, T = pos.shape
    W = xw.shape[1]
    G = SC_GATHER_ROWS
    part_rows = cap // SC_CORES
    mesh = plsc.VectorSubcoreMesh(core_axis_name="c", subcore_axis_name="s")

    @pl.kernel(
        out_type=jax.ShapeDtypeStruct((E * cap, W), I32),
        mesh=mesh,
        scratch_types=[pltpu.VMEM((T,), I32), pltpu.VMEM((cap,), I32), pltpu.VMEM((G, W), I32)],
        compiler_params=pltpu.CompilerParams(needs_layout_passes=False),
        name="sc_dispatch",
    )
    def run(pos_hbm, x_hbm, xe_hbm, pos_v, idx_v, buf):
        e = lax.axis_index("s")
        part = lax.axis_index("c")
        pltpu.sync_copy(pos_hbm.at[e], pos_v)
        lane = lax.iota(I32, SC_LANES)

        @pl.loop(0, T // SC_LANES)
        def _(i):
            off = pl.multiple_of(i * SC_LANES, SC_LANES)
            p = pos_v[pl.ds(off, SC_LANES)]
            plsc.store_scatter(idx_v, [p], lane + off, mask=p >= 0)

        @pl.loop(0, part_rows // G)
        def _(g):
            o = pl.multiple_of(part * part_rows + g * G, G)
            pltpu.sync_copy(x_hbm.at[idx_v.at[pl.ds(o, G)]], buf)
            pltpu.sync_copy(buf, xe_hbm.at[pl.ds(e * cap + o, G)])

    return run(pos, xw)


FFN_TM = 2048
FFN_UNPACK_ROWS = 256


def _ffn_tile_width(cap):
    return 512 if min(FFN_TM, cap) <= 1024 else 256


def _ffn_kernel(x_ref, wg_ref, wu_ref, wd_ref, o_ref, xb_ref, hid_ref, *, tw):
    s = pl.program_id(2)
    n_up = D_FF // tw

    @pl.when(s == 0)
    def _():
        half = D_MODEL // 2

        def unpack_rows(i, carry):
            r = pl.multiple_of(i * FFN_UNPACK_ROWS, FFN_UNPACK_ROWS)
            lo, hi = _unpack_bf16_pairs(x_ref[pl.ds(r, FFN_UNPACK_ROWS), :])
            xb_ref[pl.ds(r, FFN_UNPACK_ROWS), :half] = lo.astype(BF16)
            xb_ref[pl.ds(r, FFN_UNPACK_ROWS), half:] = hi.astype(BF16)
            return carry

        lax.fori_loop(0, x_ref.shape[0] // FFN_UNPACK_ROWS, unpack_rows, 0)

    @pl.when(s < n_up)
    def _():
        x = xb_ref[...]
        g = _dot(x, wg_ref[...].astype(BF16))
        u = _dot(x, wu_ref[...].astype(BF16))
        col = pl.multiple_of(s * tw, tw)
        hid_ref[:, pl.ds(col, tw)] = (_silu(g) * u).astype(BF16)

    @pl.when(s >= n_up)
    def _():
        o_ref[...] = _pack_bf16_pairs(_dot(hid_ref[...], wd_ref[...].astype(BF16)))


def _ffn(xe, w_gate, w_up, w_down):
    E, cap, _ = xe.shape
    tm = min(FFN_TM, cap)
    tw = _ffn_tile_width(cap)
    n_up, n_down = D_FF // tw, D_MODEL // tw

    def up(e, m, s):
        return (e, 0, jnp.minimum(s, n_up - 1))

    def down(s):
        return jnp.maximum(s - n_up, 0)

    return pl.pallas_call(
        functools.partial(_ffn_kernel, tw=tw),
        grid=(E, cap // tm, n_up + n_down),
        in_specs=[
            pl.BlockSpec((None, tm, D_MODEL // 2), lambda e, m, s: (e, m, 0)),
            pl.BlockSpec((None, D_MODEL, tw), up),
            pl.BlockSpec((None, D_MODEL, tw), up),
            pl.BlockSpec((None, D_FF, tw), lambda e, m, s: (e, 0, down(s))),
        ],
        out_specs=pl.BlockSpec((None, tm, tw // 2), lambda e, m, s: (e, m, down(s))),
        out_shape=jax.ShapeDtypeStruct((E, cap, D_MODEL // 2), I32),
        scratch_shapes=[pltpu.VMEM((tm, D_MODEL), BF16), pltpu.VMEM((tm, D_FF), BF16)],
        compiler_params=_cparams(("parallel", "parallel", "arbitrary"), 60),
        name="ffn",
    )(xe, w_gate, w_up, w_down)


def _regroup(posb, rankb, affb, off, yw, cap, rows):
    NB, EB = posb.shape
    W = yw.shape[1]
    G, L, tt = SC_GATHER_ROWS, SC_LANES, MOE_TT
    n_workers = SC_CORES * SC_SUBCORES
    split = max(1, n_workers // NB)
    per = -(-NB * split // n_workers)
    mesh = plsc.VectorSubcoreMesh(core_axis_name="c", subcore_axis_name="s")
    off_pad = jnp.pad(off, (0, L))

    @pl.kernel(
        out_type=(jax.ShapeDtypeStruct((rows, W), I32), jax.ShapeDtypeStruct((rows,), I32),
                  jax.ShapeDtypeStruct((rows,), F32)),
        mesh=mesh,
        scratch_types=[pltpu.VMEM((EB,), I32), pltpu.VMEM((EB,), I32), pltpu.VMEM((EB,), F32),
                       pltpu.VMEM((EB,), I32), pltpu.VMEM((EB,), I32), pltpu.VMEM((EB,), F32),
                       pltpu.VMEM((G, W), I32), pltpu.VMEM((NB + 1 + L,), I32)],
        compiler_params=pltpu.CompilerParams(needs_layout_passes=False),
        name="sc_regroup",
    )
    def run(posb_hbm, rankb_hbm, affb_hbm, off_hbm, y_hbm, yg_hbm, tok_hbm, gate_hbm,
            pos_v, rank_v, aff_v, src_v, tok_v, gate_v, buf, off_v):
        wid = lax.axis_index("c") * SC_SUBCORES + lax.axis_index("s")
        pltpu.sync_copy(off_hbm, off_v)
        lane = lax.iota(I32, L)
        zi = jnp.zeros((L,), I32)
        zf = jnp.zeros((L,), F32)

        def regroup_block(j, part):
            pltpu.sync_copy(posb_hbm.at[j], pos_v)
            pltpu.sync_copy(rankb_hbm.at[j], rank_v)
            pltpu.sync_copy(affb_hbm.at[j], aff_v)
            lo = jnp.max(plsc.load_gather(off_v, [zi + j]))
            hi = jnp.max(plsc.load_gather(off_v, [zi + j + 1]))

            @pl.loop(0, EB // L)
            def _(i):
                o = pl.multiple_of(i * L, L)
                src_v[pl.ds(o, L)] = zi
                tok_v[pl.ds(o, L)] = zi
                gate_v[pl.ds(o, L)] = zf

            @pl.loop(0, EB // L)
            def _(i):
                o = pl.multiple_of(i * L, L)
                p = pos_v[pl.ds(o, L)]
                r = rank_v[pl.ds(o, L)]
                m = p >= 0
                e = i // (tt // L)
                t0 = j * tt + (i % (tt // L)) * L
                plsc.store_scatter(src_v, [r], p + e * cap, mask=m)
                plsc.store_scatter(tok_v, [r], lane + t0, mask=m)
                plsc.store_scatter(gate_v, [r], aff_v[pl.ds(o, L)], mask=m)

            @pl.loop(part, (hi - lo) // G, step=split)
            def _(g):
                o = pl.multiple_of(g * G, G)
                dst = pl.multiple_of(lo + o, G)
                pltpu.sync_copy(y_hbm.at[src_v.at[pl.ds(o, G)]], buf)
                pltpu.sync_copy(buf, yg_hbm.at[pl.ds(dst, G)])
                pltpu.sync_copy(tok_v.at[pl.ds(o, G)], tok_hbm.at[pl.ds(dst, G)])
                pltpu.sync_copy(gate_v.at[pl.ds(o, G)], gate_hbm.at[pl.ds(dst, G)])

        @pl.loop(0, per)
        def _(k):
            unit = wid * per + k

            @pl.when(unit < NB * split)
            def _():
                regroup_block(unit // split, unit % split)

    return run(posb, rankb, affb, off_pad, yw)


def _combine_kernel(pj_ref, pw_ref, pf_ref, plo_ref, phi_ref, tok_ref, gate_ref, yg_ref, h_ref, nf_ref,
                    o_ref, acc_ref, *, group):
    p = pl.program_id(0)
    flag = pf_ref[p]
    tw, tt = MOE_TW, MOE_TT
    half = D_MODEL // 2

    hw = group // 2

    def col_blocks():
        for n in range(D_MODEL // group):
            yield slice(n * hw, (n + 1) * hw), slice(n * group, n * group + hw)
            yield slice(half + n * hw, half + (n + 1) * hw), slice(n * group + hw, (n + 1) * group)

    @pl.when((flag & 1) != 0)
    def _():
        for packed, natural in col_blocks():
            acc_ref[:, packed] = h_ref[:, natural]

    @pl.when((flag & 4) != 0)
    def _():
        lo, hi = plo_ref[p], phi_ref[p]
        row0 = pw_ref[p] * tw
        rid = lax.broadcasted_iota(I32, (tw, 1), 0) + row0
        keep = (rid >= lo) & (rid < hi)
        y_lo, y_hi = _unpack_bf16_pairs(yg_ref[...])
        y_lo = jnp.where(keep, y_lo, 0.0).astype(BF16)
        y_hi = jnp.where(keep, y_hi, 0.0).astype(BF16)
        tid = lax.broadcasted_iota(I32, (tt, tw), 0) + pj_ref[p] * tt
        cid = lax.broadcasted_iota(I32, (tt, tw), 1) + row0
        hit = (tok_ref[...] == tid) & (cid >= lo) & (cid < hi)
        weights = jnp.where(hit, gate_ref[...], 0.0).astype(BF16)
        acc_ref[:, :half] += _dot(weights, y_lo)
        acc_ref[:, half:] += _dot(weights, y_hi)

    @pl.when((flag & 2) != 0)
    def _():
        y = acc_ref[...]
        scale = lax.rsqrt(jnp.mean(y * y, axis=-1, keepdims=True) + EPS)
        for packed, natural in col_blocks():
            o_ref[:, natural] = acc_ref[:, packed] * scale * nf_ref[:, natural]


def _combine(lists, tok, gate, yg, h, nfw, group):
    pj, pw, pf, plo, phi = lists
    T = h.shape[0]
    tw, tt = MOE_TW, MOE_TT
    nwin = yg.shape[0] // tw
    grid_spec = pltpu.PrefetchScalarGridSpec(
        num_scalar_prefetch=5,
        grid=(pj.shape[0],),
        in_specs=[
            pl.BlockSpec((None, 1, tw), lambda p, pj, pw, *_: (pw[p], 0, 0)),
            pl.BlockSpec((None, 1, tw), lambda p, pj, pw, *_: (pw[p], 0, 0)),
            pl.BlockSpec((tw, D_MODEL // 2), lambda p, pj, pw, *_: (pw[p], 0)),
            pl.BlockSpec((tt, D_MODEL), lambda p, pj, pw, *_: (pj[p], 0)),
            pl.BlockSpec((1, D_MODEL), lambda p, pj, pw, *_: (0, 0)),
        ],
        out_specs=pl.BlockSpec((tt, D_MODEL), lambda p, pj, pw, *_: (pj[p], 0)),
        scratch_shapes=[pltpu.VMEM((tt, D_MODEL), F32)],
    )
    return pl.pallas_call(
        functools.partial(_combine_kernel, group=group),
        grid_spec=grid_spec,
        out_shape=jax.ShapeDtypeStruct((T, D_MODEL), F32),
        compiler_params=_cparams(("arbitrary",), 48),
        name="combine",
    )(pj, pw, pf, plo, phi, tok.reshape(nwin, 1, tw), gate.reshape(nwin, 1, tw), yg, h, nfw)


def _rope_tables(seq_len):
    d = RET_DK
    inv = ROPE_BASE ** (-jnp.arange(0, d, 2, dtype=F32) / d)
    ang = jnp.arange(seq_len, dtype=F32)[:, None] * inv[None, :]
    return jnp.cos(ang), jnp.sin(ang)


def _chunk_tri(n, chunk, upper):
    r = np.arange(n)
    same = (r[:, None] // chunk) == (r[None, :] // chunk)
    tri = (r[:, None] <= r[None, :]) if upper else (r[:, None] >= r[None, :])
    return jnp.asarray(same & tri, BF16)


def _prep_params(norm1_w, w_in, ret_gn_w, gla_gate_up, gla_gate_bias, gla_gn_w, w_out, norm2_w, router_w,
                 normf_w):
    w = w_in[0]
    w_main = w[:, :IN_MAIN].astype(BF16)
    w_ga = jnp.pad(w[:, IN_MAIN:], ((0, 0), (0, LANE - 2 * GLA_RANK))).astype(BF16)
    cs = np.ones((1, IN_MAIN), np.float32)
    cs[:, _RQ:_RQ + RET_WIDTH] = RET_DK ** -0.5
    cs[:, _GQ:_GQ + GLA_KEY_WIDTH] = GLA_DK ** -0.5
    up = gla_gate_up[0].astype(F32)
    up_pad = jnp.zeros((LANE, 2 * GLA_KEY_WIDTH), F32)
    up_pad = up_pad.at[:GLA_RANK, :GLA_KEY_WIDTH].set(up[0])
    up_pad = up_pad.at[GLA_RANK:2 * GLA_RANK, GLA_KEY_WIDTH:].set(up[1])
    rt = router_w[0].T.astype(F32)
    r_hi = rt.astype(BF16)
    r_lo = (rt - r_hi.astype(F32)).astype(BF16)
    return dict(
        n1w=norm1_w[0].reshape(1, D_MODEL).astype(F32),
        w_main=w_main, w_ga=w_ga, colscale=jnp.asarray(cs),
        up_pad=up_pad.astype(BF16),
        bias=gla_gate_bias[0].reshape(1, 2 * GLA_KEY_WIDTH).astype(F32),
        lf=_chunk_tri(GATE_TM, GLA_CHUNK, upper=False),
        lb=_chunk_tri(GATE_TM, GLA_CHUNK, upper=True),
        ret_gn=ret_gn_w[0].reshape(1, RET_WIDTH).astype(F32),
        gla_gn=gla_gn_w[0].reshape(1, GLA_WIDTH).astype(F32),
        w_out=w_out[0].astype(BF16),
        n2w=norm2_w[0].reshape(1, D_MODEL).astype(F32),
        r_hi=r_hi, r_lo=r_lo,
        nfw=normf_w.reshape(1, D_MODEL).astype(F32),
    )


def _trunk_front(x, pp, decay_logit, w_gate, w_up, w_down):
    B, L, _ = x.shape
    T = B * L
    x2d = x.reshape(T, D_MODEL)
    cos, sin = _rope_tables(L)
    proj, ga = _in_proj(x2d, pp["n1w"], pp["w_main"], pp["w_ga"], pp["colscale"], cos, sin, L)
    b_f, b_b = _gla_gates(ga, pp["up_pad"], pp["bias"], pp["lf"], pp["lb"])

    fwd = _mixer_scan(proj, decay_logit, b_f, B, L, reverse=False)
    mix_r, mix_g = _mixer_scan(proj, decay_logit, b_b, B, L, reverse=True, o_fwd=fwd,
                               gn_w=(pp["ret_gn"], pp["gla_gn"]))

    h, xn2, aff = _out_proj(mix_r, mix_g, pp["w_out"], x2d, pp["n2w"], pp["r_hi"], pp["r_lo"])

    cap = CAPACITY_FACTOR * T // N_EXPERTS
    pos, posb, rankb, affb = _select(aff, cap)
    off, c_lists = _combine_schedule(posb, T)
    xe = _dispatch(pos, xn2, cap).reshape(N_EXPERTS, cap, D_MODEL // 2)
    ye = _ffn(xe, w_gate, w_up, w_down).reshape(N_EXPERTS * cap, D_MODEL // 2)
    nb = T // MOE_TT
    yg, tok, gate = _regroup(posb.reshape(nb, -1), rankb.reshape(nb, -1), affb.reshape(nb, -1), off, ye, cap,
                             _regroup_rows(T))
    return c_lists, tok, gate, yg, h


def _trunk_back(front, nfw, shape):
    c_lists, tok, gate, yg, h = front
    cap = CAPACITY_FACTOR * h.shape[0] // N_EXPERTS
    return _combine(c_lists, tok, gate, yg, h, nfw, _ffn_tile_width(cap)).reshape(shape)


def kernel(x_prompt, x_sample, norm1_w, w_in, ret_decay_logit, ret_gn_w, gla_gate_up, gla_gate_bias,
           gla_gn_w, w_out, norm2_w, router_w, w_gate, w_up, w_down, normf_w):
    pp = _prep_params(norm1_w, w_in, ret_gn_w, gla_gate_up, gla_gate_bias, gla_gn_w, w_out, norm2_w,
                      router_w, normf_w)
    decay_logit = ret_decay_logit[0].astype(F32)
    args = (pp, decay_logit, w_gate[0], w_up[0], w_down[0])
    front_p = _trunk_front(x_prompt, *args)
    front_s = _trunk_front(x_sample, *args)
    y_prompt = _trunk_back(front_p, pp["nfw"], x_prompt.shape)
    front_s, y_prompt = lax.optimization_barrier((front_s, y_prompt))
    y_sample = _trunk_back(front_s, pp["nfw"], x_sample.shape)
    return (y_prompt, y_sample)
```

```python
import functools

import numpy as np
import jax
import jax.numpy as jnp
from jax import lax
from jax.experimental import pallas as pl
from jax.experimental.pallas import tpu as pltpu
from jax.experimental.pallas import tpu_sc as plsc

F32, BF16, I32 = jnp.float32, jnp.bfloat16, jnp.int32

D_MODEL = 2048
RET_WIDTH = 1024
RET_HEADS = 4
RET_DK = 256
RET_DV = 256
GLA_WIDTH = 1024
GLA_HEADS = 4
GLA_DK = 128
GLA_DV = 256
GLA_KEY_WIDTH = 512
GLA_RANK = 16
GLA_TAU = 16.0
RET_CHUNK = 256
GLA_CHUNK = 64
ROPE_BASE = 10000.0
N_EXPERTS = 16
CAPACITY_FACTOR = 2
D_FF = 2048
EPS = 1e-6
LOG2_E = 1.4426950408889634
IN_MAIN = 4 * RET_WIDTH + 2 * GLA_KEY_WIDTH + 2 * GLA_WIDTH

_RQ, _RK, _RV, _RG = 0, 1024, 2048, 3072
_GQ, _GK, _GV, _GG = 4096, 4608, 5120, 6144

LANE = 128
MOE_TT = 512
MOE_TW = 512


def _cparams(sem, vmem_mb):
    return pltpu.CompilerParams(dimension_semantics=sem, vmem_limit_bytes=vmem_mb * 1024 * 1024)


def _log_sigmoid(z):
    return jnp.minimum(z, 0.0) - jnp.log(1.0 + jnp.exp(-jnp.abs(z)))


def _silu(g):
    return g * (1.0 / (1.0 + jnp.exp(-g)))


def _dot_nt(a, b):
    return lax.dot_general(a, b, (((1,), (1,)), ((), ())), preferred_element_type=F32)


def _dot_tn(a, b):
    return lax.dot_general(a, b, (((0,), (0,)), ((), ())), preferred_element_type=F32)


def _dot(a, b):
    return jnp.dot(a, b, preferred_element_type=F32)


def _pack_bf16_pairs(x):
    bits = pltpu.bitcast(x.astype(BF16).astype(F32), I32)
    w = x.shape[1] // 2
    return bits[:, w:] | lax.shift_right_logical(bits[:, :w], 16)


def _unpack_bf16_pairs(words):
    lo = pltpu.bitcast(lax.shift_left(words, 16), F32)
    hi = pltpu.bitcast(words & jnp.int32(-65536), F32)
    return lo, hi


IP_TM = 1024
IP_TN = 1024


def _in_proj_kernel(x_ref, n1_ref, w_ref, wga_ref, cs_ref, cos_ref, sin_ref, o_ref, ga_ref, xn_ref):
    j = pl.program_id(1)

    @pl.when(j == 0)
    def _():
        x = x_ref[...]
        ms = jnp.mean(x * x, axis=-1, keepdims=True)
        xn = (x * lax.rsqrt(ms + EPS) * n1_ref[...]).astype(BF16)
        xn_ref[...] = xn
        ga_ref[...] = _dot(xn, wga_ref[...])

    acc = _dot(xn_ref[...], w_ref[...]) * cs_ref[...]
    n_rope_blocks = 2 * RET_WIDTH // IP_TN

    @pl.when(j < n_rope_blocks)
    def _():
        cos = cos_ref[...]
        sin = sin_ref[...]
        for h in range(IP_TN // RET_DK):
            x1 = acc[:, 2 * h * LANE:(2 * h + 1) * LANE]
            x2 = acc[:, (2 * h + 1) * LANE:(2 * h + 2) * LANE]
            o_ref[2 * h] = (x1 * cos - x2 * sin).astype(BF16)
            o_ref[2 * h + 1] = (x1 * sin + x2 * cos).astype(BF16)

    @pl.when(j >= n_rope_blocks)
    def _():
        for c in range(IP_TN // LANE):
            o_ref[c] = acc[:, c * LANE:(c + 1) * LANE].astype(BF16)


def _in_proj(x2d, n1w, w_main, w_ga, colscale, cos, sin, seq_len):
    T = x2d.shape[0]
    tm, tn = IP_TM, IP_TN
    nlb = seq_len // tm
    return pl.pallas_call(
        _in_proj_kernel,
        grid=(T // tm, IN_MAIN // tn),
        in_specs=[
            pl.BlockSpec((tm, D_MODEL), lambda i, j: (i, 0)),
            pl.BlockSpec((1, D_MODEL), lambda i, j: (0, 0)),
            pl.BlockSpec((D_MODEL, tn), lambda i, j: (0, j)),
            pl.BlockSpec((D_MODEL, LANE), lambda i, j: (0, 0)),
            pl.BlockSpec((1, tn), lambda i, j: (0, j)),
            pl.BlockSpec((tm, LANE), lambda i, j: (i % nlb, 0)),
            pl.BlockSpec((tm, LANE), lambda i, j: (i % nlb, 0)),
        ],
        out_specs=[
            pl.BlockSpec((tn // LANE, tm, LANE), lambda i, j: (j, i, 0)),
            pl.BlockSpec((tm, LANE), lambda i, j: (i, 0)),
        ],
        out_shape=[
            jax.ShapeDtypeStruct((IN_MAIN // LANE, T, LANE), BF16),
            jax.ShapeDtypeStruct((T, LANE), F32),
        ],
        scratch_shapes=[pltpu.VMEM((tm, D_MODEL), BF16)],
        compiler_params=_cparams(("parallel", "arbitrary"), 48),
        name="in_proj",
    )(x2d, n1w, w_main, w_ga, colscale, cos, sin)


GATE_TM = 512


def _gates_kernel(ga_ref, up_ref, bias_ref, lf_ref, lb_ref, bf_ref, bb_ref):
    z = _dot(ga_ref[...].astype(BF16), up_ref[...]) + bias_ref[...]
    la = _log_sigmoid(z) * (LOG2_E / GLA_TAU)
    hi = la.astype(BF16)
    lo = (la - hi.astype(F32)).astype(BF16)
    kw = GLA_KEY_WIDTH
    bf_ref[...] = _dot(lf_ref[...], hi[:, :kw]) + _dot(lf_ref[...], lo[:, :kw])
    bb_ref[...] = _dot(lb_ref[...], hi[:, kw:]) + _dot(lb_ref[...], lo[:, kw:])


def _gla_gates(ga, up_pad, bias, lf, lb):
    T = ga.shape[0]
    tm = GATE_TM
    kw = GLA_KEY_WIDTH
    return pl.pallas_call(
        _gates_kernel,
        grid=(T // tm,),
        in_specs=[
            pl.BlockSpec((tm, LANE), lambda i: (i, 0)),
            pl.BlockSpec((LANE, 2 * kw), lambda i: (0, 0)),
            pl.BlockSpec((1, 2 * kw), lambda i: (0, 0)),
            pl.BlockSpec((tm, tm), lambda i: (0, 0)),
            pl.BlockSpec((tm, tm), lambda i: (0, 0)),
        ],
        out_specs=[pl.BlockSpec((tm, kw), lambda i: (i, 0)), pl.BlockSpec((tm, kw), lambda i: (i, 0))],
        out_shape=[jax.ShapeDtypeStruct((T, kw), F32), jax.ShapeDtypeStruct((T, kw), F32)],
        compiler_params=_cparams(("parallel",), 32),
        name="gla_gates",
    )(ga, up_pad, bias, lf, lb)


def _wide(ref, rows):
    return jnp.concatenate([ref[0, rows, :], ref[1, rows, :]], axis=1)


def _finish_heads(tot, gn, gate):
    ms = jnp.mean(tot * tot, axis=-1, keepdims=True)
    yn = tot * lax.rsqrt(ms + EPS) * gn
    return (yn * _silu(gate.astype(F32))).astype(BF16)


RET_TB = 1024


def _ret_kernel(dl_ref, q_ref, k_ref, v_ref, *rest, reverse):
    if reverse:
        g_ref, of_ref, gn_ref, o_ref, s_ref, intra_ref, qd_ref, kd_ref, cd_ref, p_ref, u_ref = rest
    else:
        o_ref, s_ref, intra_ref, qd_ref, kd_ref, cd_ref, p_ref, u_ref = rest
    h = pl.program_id(1)
    n = pl.program_id(2)
    C = RET_CHUNK

    @pl.when(n == 0)
    def _():
        s_ref[...] = jnp.zeros_like(s_ref)
        logit = dl_ref[1 if reverse else 0, h]
        lg = _log_sigmoid(jnp.full((C, RET_DV), logit, F32))
        lg_c = _log_sigmoid(jnp.full((C, C), logit, F32))
        lg_r = _log_sigmoid(jnp.full((1, RET_DV), logit, F32))
        ri = lax.broadcasted_iota(I32, (C, RET_DV), 0).astype(F32)
        rc = lax.broadcasted_iota(I32, (C, C), 0).astype(F32)
        cc = lax.broadcasted_iota(I32, (C, C), 1).astype(F32)
        diff = (cc - rc) if reverse else (rc - cc)
        intra_ref[...] = jnp.where(diff >= 0, jnp.exp(lg_c * diff), 0.0)
        if reverse:
            qd_ref[...] = jnp.exp(lg * (C - ri))
            kd_ref[...] = jnp.exp(lg * ri)
        else:
            qd_ref[...] = jnp.exp(lg * (ri + 1.0))
            kd_ref[...] = jnp.exp(lg * (C - 1.0 - ri))
        cd_ref[...] = jnp.exp(lg_r * C)

    nchunks = o_ref.shape[0] // C
    for c in range(nchunks):
        rows = slice(c * C, (c + 1) * C)
        k = _wide(k_ref, rows)
        p_ref[c] = (_dot_nt(_wide(q_ref, rows), k) * intra_ref[...]).astype(BF16)
        kd = (k.astype(F32) * kd_ref[...]).astype(BF16)
        u_ref[c] = _dot_tn(kd, _wide(v_ref, rows))
    order = range(nchunks - 1, -1, -1) if reverse else range(nchunks)
    for c in order:
        rows = slice(c * C, (c + 1) * C)
        state = s_ref[...]
        o = _dot(p_ref[c], _wide(v_ref, rows)) + _dot(_wide(q_ref, rows), state.astype(BF16)) * qd_ref[...]
        s_ref[...] = state * cd_ref[...] + u_ref[c]
        if reverse:
            tot = of_ref[rows, :].astype(F32) + o
            o_ref[rows, :] = _finish_heads(tot, gn_ref[...], _wide(g_ref, rows))
        else:
            o_ref[rows, :] = o.astype(BF16)


def _ret_parts(proj, decay_logit, rb, reverse, o_fwd=None, gn_w=None):
    T = proj.shape[1]
    tb = RET_TB
    dk, dv, C = RET_DK, RET_DV, RET_CHUNK

    def head(base):
        return pl.BlockSpec((dk // LANE, tb, LANE), lambda b, h, n: (base // dk + h, rb(b, n), 0))

    in_specs = [pl.BlockSpec(memory_space=pltpu.SMEM), head(_RQ), head(_RK), head(_RV)]
    args = [decay_logit, proj, proj, proj]
    if reverse:
        in_specs += [
            head(_RG),
            pl.BlockSpec((tb, dv), lambda b, h, n: (rb(b, n), h)),
            pl.BlockSpec((1, dv), lambda b, h, n: (0, h)),
        ]
        args += [proj, o_fwd, gn_w]
    out_spec = pl.BlockSpec((tb, dv), lambda b, h, n: (rb(b, n), h))
    out_shape = jax.ShapeDtypeStruct((T, RET_WIDTH), BF16)
    scratch = [
        pltpu.VMEM((dk, dv), F32),
        pltpu.VMEM((C, C), F32),
        pltpu.VMEM((C, dv), F32),
        pltpu.VMEM((C, dk), F32),
        pltpu.VMEM((1, dv), F32),
        pltpu.VMEM((tb // C, C, C), BF16),
        pltpu.VMEM((tb // C, dk, dv), F32),
    ]
    return in_specs, args, out_spec, out_shape, scratch


GLA_TB = 1024
GLA_UNROLL = 16


GLA_LEVELS = (32, 16, 8, 4, 2, 1)
SUBLANES = 8


def _gla_tables(reverse):
    C = GLA_CHUNK
    r = np.arange(C)
    masks = np.zeros((len(GLA_LEVELS) + 1, C, C), np.float32)
    for l, s in enumerate(GLA_LEVELS):
        upper = (r & s) != 0
        same = (r[:, None] // (2 * s)) == (r[None, :] // (2 * s))
        lhs_rows = ~upper if reverse else upper
        masks[l] = same & lhs_rows[:, None] & ~lhs_rows[None, :]
    masks[-1] = np.eye(C)
    return jnp.asarray(masks, F32)


def _gla_kernel(q_ref, k_ref, v_ref, b_ref, mask_ref, *rest, reverse):
    if reverse:
        g_ref, of_ref, gn_ref, o_ref, st_ref, sc_ref = rest
    else:
        o_ref, st_ref, sc_ref = rest
    n = pl.program_id(2)
    C = GLA_CHUNK

    @pl.when(n == 0)
    def _():
        st_ref[...] = jnp.zeros_like(st_ref)

    nchunks = q_ref.shape[0] // C
    sub_row = lax.broadcasted_iota(I32, (SUBLANES, GLA_DK), 0)
    zero_rows = jnp.zeros((SUBLANES, GLA_DK), F32)

    def chunk_scores(c, carry):
        c0 = pl.multiple_of(c * C, C)
        qb = q_ref[pl.ds(c0, C), :]
        kb = k_ref[pl.ds(c0, C), :]
        q = qb.astype(F32)
        k = kb.astype(F32)
        b = b_ref[pl.ds(c0, C), :]

        def mid_row(r):
            return jnp.broadcast_to(b[r:r + 1, :], (SUBLANES, GLA_DK))

        scores = mask_ref[len(GLA_LEVELS)] * _dot_nt(qb, kb)
        for l, s in enumerate(GLA_LEVELS):
            lhs, rhs = [], []
            for g in range(C // SUBLANES):
                r0 = g * SUBLANES
                rows = slice(r0, r0 + SUBLANES)
                if s >= SUBLANES:
                    m = mid_row((r0 // (2 * s)) * (2 * s) + s)
                    is_lhs = ((r0 & s) != 0) != reverse
                    if is_lhs:
                        lhs.append(q[rows] * jnp.exp2(b[rows] - m))
                        rhs.append(zero_rows)
                    else:
                        lhs.append(zero_rows)
                        rhs.append(k[rows] * jnp.exp2(m - b[rows]))
                else:
                    m = mid_row(r0 + SUBLANES - s)
                    for blk in range(SUBLANES // (2 * s) - 2, -1, -1):
                        m = jnp.where(sub_row < (blk + 1) * 2 * s, mid_row(r0 + blk * 2 * s + s), m)
                    upper = (sub_row & s) != 0
                    is_lhs = jnp.logical_not(upper) if reverse else upper
                    lhs.append(jnp.where(is_lhs, q[rows] * jnp.exp2(b[rows] - m), 0.0))
                    rhs.append(jnp.where(is_lhs, 0.0, k[rows] * jnp.exp2(m - b[rows])))
            lhs = jnp.concatenate(lhs, axis=0).astype(BF16)
            rhs = jnp.concatenate(rhs, axis=0).astype(BF16)
            scores = scores + mask_ref[l] * _dot_nt(lhs, rhs)
        sc_ref[c] = scores.astype(BF16)
        return carry

    lax.fori_loop(0, nchunks, chunk_scores, 0, unroll=GLA_UNROLL)

    def chunk(ci, carry):
        c = (nchunks - 1 - ci) if reverse else ci
        c0 = pl.multiple_of(c * C, C)
        q = q_ref[pl.ds(c0, C), :].astype(F32)
        k = k_ref[pl.ds(c0, C), :].astype(F32)
        v = _wide(v_ref, pl.ds(c0, C))
        b = b_ref[pl.ds(c0, C), :]
        b_end = b[0:1, :] if reverse else b[C - 1:C, :]

        st = st_ref[...]
        o = _dot_nt((q * jnp.exp2(b)).astype(BF16), st.astype(BF16))
        ke = (k * jnp.exp2(b_end - b)).astype(BF16)
        st_ref[...] = st * jnp.exp2(b_end) + _dot_tn(v, ke)
        o = o + _dot(sc_ref[c], v)
        if reverse:
            tot = of_ref[pl.ds(c0, C), :].astype(F32) + o
            o_ref[pl.ds(c0, C), :] = _finish_heads(tot, gn_ref[...], _wide(g_ref, pl.ds(c0, C)))
        else:
            o_ref[pl.ds(c0, C), :] = o.astype(BF16)
        return carry

    lax.fori_loop(0, nchunks, chunk, 0, unroll=GLA_UNROLL)


def _gla_parts(proj, bcum, rb, reverse, o_fwd=None, gn_w=None):
    T = proj.shape[1]
    tb = GLA_TB
    dk, dv = GLA_DK, GLA_DV
    masks = _gla_tables(reverse)

    def key_block(base):
        return pl.BlockSpec((None, tb, LANE), lambda b, h, n: (base // dk + h, rb(b, n), 0))

    def value_block(base):
        return pl.BlockSpec((dv // LANE, tb, LANE), lambda b, h, n: (base // dv + h, rb(b, n), 0))

    in_specs = [
        key_block(_GQ), key_block(_GK), value_block(_GV),
        pl.BlockSpec((tb, dk), lambda b, h, n: (rb(b, n), h)),
        pl.BlockSpec(masks.shape, lambda b, h, n: (0, 0, 0)),
    ]
    args = [proj, proj, proj, bcum, masks]
    if reverse:
        in_specs += [
            value_block(_GG),
            pl.BlockSpec((tb, dv), lambda b, h, n: (rb(b, n), h)),
            pl.BlockSpec((1, dv), lambda b, h, n: (0, h)),
        ]
        args += [proj, o_fwd, gn_w]
    out_spec = pl.BlockSpec((tb, dv), lambda b, h, n: (rb(b, n), h))
    out_shape = jax.ShapeDtypeStruct((T, GLA_WIDTH), BF16)
    scratch = [pltpu.VMEM((dv, dk), F32), pltpu.VMEM((tb // GLA_CHUNK, GLA_CHUNK, GLA_CHUNK), BF16)]
    return in_specs, args, out_spec, out_shape, scratch


def _mixer_kernel(*refs, reverse, n_ret_in, n_gla_in, n_ret_scratch):
    ret_in = refs[:n_ret_in]
    gla_in = refs[n_ret_in:n_ret_in + n_gla_in]
    ret_out, gla_out = refs[n_ret_in + n_gla_in:n_ret_in + n_gla_in + 2]
    scratch = refs[n_ret_in + n_gla_in + 2:]
    _ret_kernel(*ret_in, ret_out, *scratch[:n_ret_scratch], reverse=reverse)
    _gla_kernel(*gla_in, gla_out, *scratch[n_ret_scratch:], reverse=reverse)


def _mixer_scan(proj, decay_logit, bcum, batch, seq_len, reverse, o_fwd=(None, None), gn_w=(None, None)):
    assert RET_TB == GLA_TB and RET_HEADS == GLA_HEADS
    nb = seq_len // RET_TB

    def rb(b, n):
        return b * nb + ((nb - 1 - n) if reverse else n)

    r_specs, r_args, r_out, r_shape, r_scratch = _ret_parts(proj, decay_logit, rb, reverse, o_fwd[0], gn_w[0])
    g_specs, g_args, g_out, g_shape, g_scratch = _gla_parts(proj, bcum, rb, reverse, o_fwd[1], gn_w[1])
    return pl.pallas_call(
        functools.partial(_mixer_kernel, reverse=reverse, n_ret_in=len(r_specs), n_gla_in=len(g_specs),
                          n_ret_scratch=len(r_scratch)),
        grid=(batch, RET_HEADS, nb),
        in_specs=r_specs + g_specs,
        out_specs=[r_out, g_out],
        out_shape=[r_shape, g_shape],
        scratch_shapes=r_scratch + g_scratch,
        compiler_params=_cparams(("parallel", "parallel", "arbitrary"), 48),
        name="mixer_bwd" if reverse else "mixer_fwd",
    )(*r_args, *g_args)


OP_TM = 512


def _out_proj_kernel(mr_ref, mg_ref, w0_ref, w1_ref, x_ref, n2_ref, rh_ref, rl_ref, h_ref, xn_ref, aff_ref,
                     hs_ref):
    s = pl.program_id(0)
    slot = s % 2

    @pl.when(s == 0)
    def _():
        hs_ref[1] = jnp.zeros(hs_ref.shape[1:], F32)

    hp = hs_ref[1 - slot]
    ms = jnp.mean(hp * hp, axis=-1, keepdims=True)
    xn = hp * lax.rsqrt(ms + EPS) * n2_ref[...]
    xh = xn.astype(BF16)
    xn_ref[...] = _pack_bf16_pairs(xn)
    xl = (xn - xh.astype(F32)).astype(BF16)
    lt = _dot_nt(rh_ref[...], xh) + _dot_nt(rh_ref[...], xl) + _dot_nt(rl_ref[...], xh)
    m = jnp.max(lt, axis=0, keepdims=True)
    e = jnp.exp(lt - m)
    aff_ref[...] = e / jnp.sum(e, axis=0, keepdims=True)

    h = x_ref[...] + _dot(mr_ref[...], w0_ref[...]) + _dot(mg_ref[...], w1_ref[...])
    h_ref[...] = h
    hs_ref[slot] = h


def _out_proj(mix_r, mix_g, w_out, x2d, n2w, r_hi, r_lo):
    T = x2d.shape[0]
    tm = OP_TM
    half = RET_WIDTH
    nblk = T // tm

    def head(s):
        return jnp.minimum(s, nblk - 1)

    def tail(s):
        return jnp.maximum(s - 1, 0)

    return pl.pallas_call(
        _out_proj_kernel,
        grid=(nblk + 1,),
        in_specs=[
            pl.BlockSpec((tm, half), lambda s: (head(s), 0)),
            pl.BlockSpec((tm, half), lambda s: (head(s), 0)),
            pl.BlockSpec((half, D_MODEL), lambda s: (0, 0)),
            pl.BlockSpec((half, D_MODEL), lambda s: (1, 0)),
            pl.BlockSpec((tm, D_MODEL), lambda s: (head(s), 0)),
            pl.BlockSpec((1, D_MODEL), lambda s: (0, 0)),
            pl.BlockSpec((N_EXPERTS, D_MODEL), lambda s: (0, 0)),
            pl.BlockSpec((N_EXPERTS, D_MODEL), lambda s: (0, 0)),
        ],
        out_specs=[
            pl.BlockSpec((tm, D_MODEL), lambda s: (head(s), 0)),
            pl.BlockSpec((tm, D_MODEL // 2), lambda s: (tail(s), 0)),
            pl.BlockSpec((N_EXPERTS, tm), lambda s: (0, tail(s))),
        ],
        out_shape=[
            jax.ShapeDtypeStruct((T, D_MODEL), F32),
            jax.ShapeDtypeStruct((T, D_MODEL // 2), I32),
            jax.ShapeDtypeStruct((N_EXPERTS, T), F32),
        ],
        scratch_shapes=[pltpu.VMEM((2, tm, D_MODEL), F32)],
        compiler_params=_cparams(("arbitrary",), 60),
        name="out_proj",
    )(mix_r, mix_g, w_out, w_out, x2d, n2w, r_hi, r_lo)


def _select_kernel(a_ref, pos_ref, posb_ref, rankb_ref, affb_ref, *, cap):
    E, T = a_ref.shape
    tt = MOE_TT

    def count(pred):
        return jnp.sum(pred.astype(F32), axis=1, keepdims=True)

    def bisect(i, tau):
        cand = tau | jnp.left_shift(jnp.int32(1), 30 - i)
        bits = pltpu.bitcast(a_ref[...], I32)
        return jnp.where(count(bits >= cand) >= cap, cand, tau)

    tau = lax.fori_loop(0, 31, bisect, jnp.zeros((E, 1), I32))
    bits_all = pltpu.bitcast(a_ref[...], I32)
    quota = cap - count(bits_all > tau)

    before = (lax.broadcasted_iota(I32, (tt, tt), 0) < lax.broadcasted_iota(I32, (tt, tt), 1)).astype(BF16)
    below = (lax.broadcasted_iota(I32, (E, E), 1) < lax.broadcasted_iota(I32, (E, E), 0)).astype(BF16)

    def block(j, carry):
        c_eq, c_sel = carry
        off = pl.multiple_of(j * tt, tt)
        aff = a_ref[:, pl.ds(off, tt)]
        bits = pltpu.bitcast(aff, I32)
        eq = bits == tau
        eqf = eq.astype(F32)
        rank_eq = _dot(eqf.astype(BF16), before) + c_eq
        sel = (bits > tau) | (eq & (rank_eq < quota))
        self_ = sel.astype(F32)
        selb = self_.astype(BF16)
        slot = _dot(selb, before) + c_sel
        pos = jnp.where(sel, slot, -1.0).astype(I32)
        pos_ref[:, pl.ds(off, tt)] = pos
        per_tok = jnp.broadcast_to(jnp.sum(self_, axis=0, keepdims=True), (E, tt))
        rank = _dot(per_tok.astype(BF16), before) + _dot(below, selb)
        posb_ref[j] = pos
        rankb_ref[j] = jnp.where(sel, rank, -1.0).astype(I32)
        affb_ref[j] = aff
        return (c_eq + jnp.sum(eqf, axis=1, keepdims=True), c_sel + jnp.sum(self_, axis=1, keepdims=True))

    zero = jnp.zeros((E, 1), F32)
    lax.fori_loop(0, T // tt, block, (zero, zero))


def _select(aff, cap):
    E, T = aff.shape
    nb = T // MOE_TT
    blk = jax.ShapeDtypeStruct((nb, E, MOE_TT), I32)
    return pl.pallas_call(
        functools.partial(_select_kernel, cap=cap),
        out_shape=[jax.ShapeDtypeStruct((E, T), I32), blk, blk, jax.ShapeDtypeStruct((nb, E, MOE_TT), F32)],
        compiler_params=pltpu.CompilerParams(vmem_limit_bytes=40 * 1024 * 1024),
        name="select",
    )(aff)


def _regroup_rows(T):
    nb = T // MOE_TT
    rows = CAPACITY_FACTOR * T + SC_GATHER_ROWS * nb
    return -(-rows // MOE_TW) * MOE_TW


def _combine_schedule(posb, T):
    nb = posb.shape[0]
    tw, g = MOE_TW, SC_GATHER_ROWS
    n = jnp.sum((posb >= 0).reshape(nb, -1), axis=1).astype(I32)
    seg = (n + g - 1) // g * g
    hi = jnp.cumsum(seg)
    lo = hi - seg
    off = jnp.concatenate([jnp.zeros((1,), I32), hi])
    nwin_max = (N_EXPERTS * MOE_TT) // tw + 1
    w0 = lo // tw
    w1 = jnp.where(seg > 0, (hi - 1) // tw, w0)
    cand = jnp.arange(nwin_max, dtype=I32)
    win = w0[:, None] + cand[None, :]
    valid = (win <= w1[:, None]).reshape(-1)
    nwin_total = _regroup_rows(T) // tw
    pmax = nb + nwin_total
    jv = jnp.broadcast_to(jnp.arange(nb, dtype=I32)[:, None], win.shape).reshape(-1)
    wv = jnp.minimum(win, nwin_total - 1).reshape(-1)
    total = jnp.sum(valid.astype(I32))
    dst = jnp.where(valid, jnp.cumsum(valid.astype(I32)) - 1, pmax)
    pj, pw = (jnp.zeros((pmax,), I32).at[dst].set(a, mode="drop") for a in (jv, wv))
    real = jnp.arange(pmax, dtype=I32) < total
    pj, pw = (jnp.where(real, a, a[total - 1]) for a in (pj, pw))
    first = jnp.concatenate([jnp.ones((1,), bool), pj[1:] != pj[:-1]])
    last = jnp.concatenate([pj[1:] != pj[:-1], jnp.ones((1,), bool)]) | (jnp.arange(pmax, dtype=I32) == total - 1)
    flag = jnp.where(real, first.astype(I32) + 2 * last.astype(I32) + 4, 0)
    return off, (pj, pw, flag, lo[pj], hi[pj])


SC_LANES = 16
SC_CORES = 2
SC_SUBCORES = 16
SC_GATHER_ROWS = 32


def _dispatch(pos, xw, cap):
    E, T = pos.shape
    W = xw.shape[1]
    G = SC_GATHER_ROWS
    part_rows = cap // SC_CORES
    mesh = plsc.VectorSubcoreMesh(core_axis_name="c", subcore_axis_name="s")

    @pl.kernel(
        out_type=jax.ShapeDtypeStruct((E * cap, W), I32),
        mesh=mesh,
        scratch_types=[pltpu.VMEM((T,), I32), pltpu.VMEM((cap,), I32), pltpu.VMEM((G, W), I32)],
        compiler_params=pltpu.CompilerParams(needs_layout_passes=False),
        name="sc_dispatch",
    )
    def run(pos_hbm, x_hbm, xe_hbm, pos_v, idx_v, buf):
        e = lax.axis_index("s")
        part = lax.axis_index("c")
        pltpu.sync_copy(pos_hbm.at[e], pos_v)
        lane = lax.iota(I32, SC_LANES)

        @pl.loop(0, T // SC_LANES)
        def _(i):
            off = pl.multiple_of(i * SC_LANES, SC_LANES)
            p = pos_v[pl.ds(off, SC_LANES)]
            plsc.store_scatter(idx_v, [p], lane + off, mask=p >= 0)

        @pl.loop(0, part_rows // G)
        def _(g):
            o = pl.multiple_of(part * part_rows + g * G, G)
            pltpu.sync_copy(x_hbm.at[idx_v.at[pl.ds(o, G)]], buf)
            pltpu.sync_copy(buf, xe_hbm.at[pl.ds(e * cap + o, G)])

    return run(pos, xw)


FFN_TM = 2048
FFN_UNPACK_ROWS = 256


def _ffn_tile_width(cap):
    return 512 if min(FFN_TM, cap) <= 1024 else 256


def _ffn_kernel(x_ref, wg_ref, wu_ref, wd_ref, o_ref, xb_ref, hid_ref, *, tw):
    s = pl.program_id(2)
    n_up = D_FF // tw

    @pl.when(s == 0)
    def _():
        half = D_MODEL // 2

        def unpack_rows(i, carry):
            r = pl.multiple_of(i * FFN_UNPACK_ROWS, FFN_UNPACK_ROWS)
            lo, hi = _unpack_bf16_pairs(x_ref[pl.ds(r, FFN_UNPACK_ROWS), :])
            xb_ref[pl.ds(r, FFN_UNPACK_ROWS), :half] = lo.astype(BF16)
            xb_ref[pl.ds(r, FFN_UNPACK_ROWS), half:] = hi.astype(BF16)
            return carry

        lax.fori_loop(0, x_ref.shape[0] // FFN_UNPACK_ROWS, unpack_rows, 0)

    @pl.when(s < n_up)
    def _():
        x = xb_ref[...]
        g = _dot(x, wg_ref[...].astype(BF16))
        u = _dot(x, wu_ref[...].astype(BF16))
        col = pl.multiple_of(s * tw, tw)
        hid_ref[:, pl.ds(col, tw)] = (_silu(g) * u).astype(BF16)

    @pl.when(s >= n_up)
    def _():
        o_ref[...] = _pack_bf16_pairs(_dot(hid_ref[...], wd_ref[...].astype(BF16)))


def _ffn(xe, w_gate, w_up, w_down):
    E, cap, _ = xe.shape
    tm = min(FFN_TM, cap)
    tw = _ffn_tile_width(cap)
    n_up, n_down = D_FF // tw, D_MODEL // tw

    def up(e, m, s):
        return (e, 0, jnp.minimum(s, n_up - 1))

    def down(s):
        return jnp.maximum(s - n_up, 0)

    return pl.pallas_call(
        functools.partial(_ffn_kernel, tw=tw),
        grid=(E, cap // tm, n_up + n_down),
        in_specs=[
            pl.BlockSpec((None, tm, D_MODEL // 2), lambda e, m, s: (e, m, 0)),
            pl.BlockSpec((None, D_MODEL, tw), up),
            pl.BlockSpec((None, D_MODEL, tw), up),
            pl.BlockSpec((None, D_FF, tw), lambda e, m, s: (e, 0, down(s))),
        ],
        out_specs=pl.BlockSpec((None, tm, tw // 2), lambda e, m, s: (e, m, down(s))),
        out_shape=jax.ShapeDtypeStruct((E, cap, D_MODEL // 2), I32),
        scratch_shapes=[pltpu.VMEM((tm, D_MODEL), BF16), pltpu.VMEM((tm, D_FF), BF16)],
        compiler_params=_cparams(("parallel", "parallel", "arbitrary"), 60),
        name="ffn",
    )(xe, w_gate, w_up, w_down)


def _regroup(posb, rankb, affb, off, yw, cap, rows):
    NB, EB = posb.shape
    W = yw.shape[1]
    G, L, tt = SC_GATHER_ROWS, SC_LANES, MOE_TT
    n_workers = SC_CORES * SC_SUBCORES
    split = max(1, n_workers // NB)
    per = -(-NB * split // n_workers)
    mesh = plsc.VectorSubcoreMesh(core_axis_name="c", subcore_axis_name="s")
    off_pad = jnp.pad(off, (0, L))

    @pl.kernel(
        out_type=(jax.ShapeDtypeStruct((rows, W), I32), jax.ShapeDtypeStruct((rows,), I32),
                  jax.ShapeDtypeStruct((rows,), F32)),
        mesh=mesh,
        scratch_types=[pltpu.VMEM((EB,), I32), pltpu.VMEM((EB,), I32), pltpu.VMEM((EB,), F32),
                       pltpu.VMEM((EB,), I32), pltpu.VMEM((EB,), I32), pltpu.VMEM((EB,), F32),
                       pltpu.VMEM((G, W), I32), pltpu.VMEM((NB + 1 + L,), I32)],
        compiler_params=pltpu.CompilerParams(needs_layout_passes=False),
        name="sc_regroup",
    )
    def run(posb_hbm, rankb_hbm, affb_hbm, off_hbm, y_hbm, yg_hbm, tok_hbm, gate_hbm,
            pos_v, rank_v, aff_v, src_v, tok_v, gate_v, buf, off_v):
        wid = lax.axis_index("c") * SC_SUBCORES + lax.axis_index("s")
        pltpu.sync_copy(off_hbm, off_v)
        lane = lax.iota(I32, L)
        zi = jnp.zeros((L,), I32)
        zf = jnp.zeros((L,), F32)

        def regroup_block(j, part):
            pltpu.sync_copy(posb_hbm.at[j], pos_v)
            pltpu.sync_copy(rankb_hbm.at[j], rank_v)
            pltpu.sync_copy(affb_hbm.at[j], aff_v)
            lo = jnp.max(plsc.load_gather(off_v, [zi + j]))
            hi = jnp.max(plsc.load_gather(off_v, [zi + j + 1]))

            @pl.loop(0, EB // L)
            def _(i):
                o = pl.multiple_of(i * L, L)
                src_v[pl.ds(o, L)] = zi
                tok_v[pl.ds(o, L)] = zi
                gate_v[pl.ds(o, L)] = zf

            @pl.loop(0, EB // L)
            def _(i):
                o = pl.multiple_of(i * L, L)
                p = pos_v[pl.ds(o, L)]
                r = rank_v[pl.ds(o, L)]
                m = p >= 0
                e = i // (tt // L)
                t0 = j * tt + (i % (tt // L)) * L
                plsc.store_scatter(src_v, [r], p + e * cap, mask=m)
                plsc.store_scatter(tok_v, [r], lane + t0, mask=m)
                plsc.store_scatter(gate_v, [r], aff_v[pl.ds(o, L)], mask=m)

            @pl.loop(part, (hi - lo) // G, step=split)
            def _(g):
                o = pl.multiple_of(g * G, G)
                dst = pl.multiple_of(lo + o, G)
                pltpu.sync_copy(y_hbm.at[src_v.at[pl.ds(o, G)]], buf)
                pltpu.sync_copy(buf, yg_hbm.at[pl.ds(dst, G)])
                pltpu.sync_copy(tok_v.at[pl.ds(o, G)], tok_hbm.at[pl.ds(dst, G)])
                pltpu.sync_copy(gate_v.at[pl.ds(o, G)], gate_hbm.at[pl.ds(dst, G)])

        @pl.loop(0, per)
        def _(k):
            unit = wid * per + k

            @pl.when(unit < NB * split)
            def _():
                regroup_block(unit // split, unit % split)

    return run(posb, rankb, affb, off_pad, yw)


def _combine_kernel(pj_ref, pw_ref, pf_ref, plo_ref, phi_ref, tok_ref, gate_ref, yg_ref, h_ref, nf_ref,
                    o_ref, acc_ref, *, group):
    p = pl.program_id(0)
    flag = pf_ref[p]
    tw, tt = MOE_TW, MOE_TT
    half = D_MODEL // 2

    hw = group // 2

    def col_blocks():
        for n in range(D_MODEL // group):
            yield slice(n * hw, (n + 1) * hw), slice(n * group, n * group + hw)
            yield slice(half + n * hw, half + (n + 1) * hw), slice(n * group + hw, (n + 1) * group)

    @pl.when((flag & 1) != 0)
    def _():
        for packed, natural in col_blocks():
            acc_ref[:, packed] = h_ref[:, natural]

    @pl.when((flag & 4) != 0)
    def _():
        lo, hi = plo_ref[p], phi_ref[p]
        row0 = pw_ref[p] * tw
        rid = lax.broadcasted_iota(I32, (tw, 1), 0) + row0
        keep = (rid >= lo) & (rid < hi)
        y_lo, y_hi = _unpack_bf16_pairs(yg_ref[...])
        y_lo = jnp.where(keep, y_lo, 0.0).astype(BF16)
        y_hi = jnp.where(keep, y_hi, 0.0).astype(BF16)
        tid = lax.broadcasted_iota(I32, (tt, tw), 0) + pj_ref[p] * tt
        cid = lax.broadcasted_iota(I32, (tt, tw), 1) + row0
        hit = (tok_ref[...] == tid) & (cid >= lo) & (cid < hi)
        weights = jnp.where(hit, gate_ref[...], 0.0).astype(BF16)
        acc_ref[:, :half] += _dot(weights, y_lo)
        acc_ref[:, half:] += _dot(weights, y_hi)

    @pl.when((flag & 2) != 0)
    def _():
        y = acc_ref[...]
        scale = lax.rsqrt(jnp.mean(y * y, axis=-1, keepdims=True) + EPS)
        for packed, natural in col_blocks():
            o_ref[:, natural] = acc_ref[:, packed] * scale * nf_ref[:, natural]


def _combine(lists, tok, gate, yg, h, nfw, group):
    pj, pw, pf, plo, phi = lists
    T = h.shape[0]
    tw, tt = MOE_TW, MOE_TT
    nwin = yg.shape[0] // tw
    grid_spec = pltpu.PrefetchScalarGridSpec(
        num_scalar_prefetch=5,
        grid=(pj.shape[0],),
        in_specs=[
            pl.BlockSpec((None, 1, tw), lambda p, pj, pw, *_: (pw[p], 0, 0)),
            pl.BlockSpec((None, 1, tw), lambda p, pj, pw, *_: (pw[p], 0, 0)),
            pl.BlockSpec((tw, D_MODEL // 2), lambda p, pj, pw, *_: (pw[p], 0)),
            pl.BlockSpec((tt, D_MODEL), lambda p, pj, pw, *_: (pj[p], 0)),
            pl.BlockSpec((1, D_MODEL), lambda p, pj, pw, *_: (0, 0)),
        ],
        out_specs=pl.BlockSpec((tt, D_MODEL), lambda p, pj, pw, *_: (pj[p], 0)),
        scratch_shapes=[pltpu.VMEM((tt, D_MODEL), F32)],
    )
    return pl.pallas_call(
        functools.partial(_combine_kernel, group=group),
        grid_spec=grid_spec,
        out_shape=jax.ShapeDtypeStruct((T, D_MODEL), F32),
        compiler_params=_cparams(("arbitrary",), 48),
        name="combine",
    )(pj, pw, pf, plo, phi, tok.reshape(nwin, 1, tw), gate.reshape(nwin, 1, tw), yg, h, nfw)


def _rope_tables(seq_len):
    d = RET_DK
    inv = ROPE_BASE ** (-jnp.arange(0, d, 2, dtype=F32) / d)
    ang = jnp.arange(seq_len, dtype=F32)[:, None] * inv[None, :]
    return jnp.cos(ang), jnp.sin(ang)


def _chunk_tri(n, chunk, upper):
    r = np.arange(n)
    same = (r[:, None] // chunk) == (r[None, :] // chunk)
    tri = (r[:, None] <= r[None, :]) if upper else (r[:, None] >= r[None, :])
    return jnp.asarray(same & tri, BF16)


def _prep_params(norm1_w, w_in, ret_gn_w, gla_gate_up, gla_gate_bias, gla_gn_w, w_out, norm2_w, router_w,
                 normf_w):
    w = w_in[0]
    w_main = w[:, :IN_MAIN].astype(BF16)
    w_ga = jnp.pad(w[:, IN_MAIN:], ((0, 0), (0, LANE - 2 * GLA_RANK))).astype(BF16)
    cs = np.ones((1, IN_MAIN), np.float32)
    cs[:, _RQ:_RQ + RET_WIDTH] = RET_DK ** -0.5
    cs[:, _GQ:_GQ + GLA_KEY_WIDTH] = GLA_DK ** -0.5
    up = gla_gate_up[0].astype(F32)
    up_pad = jnp.zeros((LANE, 2 * GLA_KEY_WIDTH), F32)
    up_pad = up_pad.at[:GLA_RANK, :GLA_KEY_WIDTH].set(up[0])
    up_pad = up_pad.at[GLA_RANK:2 * GLA_RANK, GLA_KEY_WIDTH:].set(up[1])
    rt = router_w[0].T.astype(F32)
    r_hi = rt.astype(BF16)
    r_lo = (rt - r_hi.astype(F32)).astype(BF16)
    return dict(
        n1w=norm1_w[0].reshape(1, D_MODEL).astype(F32),
        w_main=w_main, w_ga=w_ga, colscale=jnp.asarray(cs),
        up_pad=up_pad.astype(BF16),
        bias=gla_gate_bias[0].reshape(1, 2 * GLA_KEY_WIDTH).astype(F32),
        lf=_chunk_tri(GATE_TM, GLA_CHUNK, upper=False),
        lb=_chunk_tri(GATE_TM, GLA_CHUNK, upper=True),
        ret_gn=ret_gn_w[0].reshape(1, RET_WIDTH).astype(F32),
        gla_gn=gla_gn_w[0].reshape(1, GLA_WIDTH).astype(F32),
        w_out=w_out[0].astype(BF16),
        n2w=norm2_w[0].reshape(1, D_MODEL).astype(F32),
        r_hi=r_hi, r_lo=r_lo,
        nfw=normf_w.reshape(1, D_MODEL).astype(F32),
    )


def _trunk_front(x, pp, decay_logit, w_gate, w_up, w_down):
    B, L, _ = x.shape
    T = B * L
    x2d = x.reshape(T, D_MODEL)
    cos, sin = _rope_tables(L)
    proj, ga = _in_proj(x2d, pp["n1w"], pp["w_main"], pp["w_ga"], pp["colscale"], cos, sin, L)
    b_f, b_b = _gla_gates(ga, pp["up_pad"], pp["bias"], pp["lf"], pp["lb"])

    fwd = _mixer_scan(proj, decay_logit, b_f, B, L, reverse=False)
    mix_r, mix_g = _mixer_scan(proj, decay_logit, b_b, B, L, reverse=True, o_fwd=fwd,
                               gn_w=(pp["ret_gn"], pp["gla_gn"]))

    h, xn2, aff = _out_proj(mix_r, mix_g, pp["w_out"], x2d, pp["n2w"], pp["r_hi"], pp["r_lo"])

    cap = CAPACITY_FACTOR * T // N_EXPERTS
    pos, posb, rankb, affb = _select(aff, cap)
    off, c_lists = _combine_schedule(posb, T)
    xe = _dispatch(pos, xn2, cap).reshape(N_EXPERTS, cap, D_MODEL // 2)
    ye = _ffn(xe, w_gate, w_up, w_down).reshape(N_EXPERTS * cap, D_MODEL // 2)
    nb = T // MOE_TT
    yg, tok, gate = _regroup(posb.reshape(nb, -1), rankb.reshape(nb, -1), affb.reshape(nb, -1), off, ye, cap,
                             _regroup_rows(T))
    return c_lists, tok, gate, yg, h


def _trunk_back(front, nfw, shape):
    c_lists, tok, gate, yg, h = front
    cap = CAPACITY_FACTOR * h.shape[0] // N_EXPERTS
    return _combine(c_lists, tok, gate, yg, h, nfw, _ffn_tile_width(cap)).reshape(shape)


def kernel(x_prompt, x_sample, norm1_w, w_in, ret_decay_logit, ret_gn_w, gla_gate_up, gla_gate_bias,
           gla_gn_w, w_out, norm2_w, router_w, w_gate, w_up, w_down, normf_w):
    pp = _prep_params(norm1_w, w_in, ret_gn_w, gla_gate_up, gla_gate_bias, gla_gn_w, w_out, norm2_w,
                      router_w, normf_w)
    decay_logit = ret_decay_logit[0].astype(F32)
    args = (pp, decay_logit, w_gate[0], w_up[0], w_down[0])
    front_p = _trunk_front(x_prompt, *args)
    front_s = _trunk_front(x_sample, *args)
    y_prompt = _trunk_back(front_p, pp["nfw"], x_prompt.shape)
    front_s, y_prompt = lax.optimization_barrier((front_s, y_prompt))
    y_sample = _trunk_back(front_s, pp["nfw"], x_sample.shape)
    return (y_prompt, y_sample)
```

```python
import functools

import numpy as np
import jax
import jax.numpy as jnp
from jax import lax
from jax.experimental import pallas as pl
from jax.experimental.pallas import tpu as pltpu
from jax.experimental.pallas import tpu_sc as plsc

F32, BF16, I32 = jnp.float32, jnp.bfloat16, jnp.int32

D_MODEL = 2048
RET_WIDTH = 1024
RET_HEADS = 4
RET_DK = 256
RET_DV = 256
GLA_WIDTH = 1024
GLA_HEADS = 4
GLA_DK = 128
GLA_DV = 256
GLA_KEY_WIDTH = 512
GLA_RANK = 16
GLA_TAU = 16.0
RET_CHUNK = 256
GLA_CHUNK = 64
ROPE_BASE = 10000.0
N_EXPERTS = 16
CAPACITY_FACTOR = 2
D_FF = 2048
EPS = 1e-6
LOG2_E = 1.4426950408889634
IN_MAIN = 4 * RET_WIDTH + 2 * GLA_KEY_WIDTH + 2 * GLA_WIDTH

_RQ, _RK, _RV, _RG = 0, 1024, 2048, 3072
_GQ, _GK, _GV, _GG = 4096, 4608, 5120, 6144

LANE = 128
MOE_TT = 512
MOE_TW = 512


def _cparams(sem, vmem_mb):
    return pltpu.CompilerParams(dimension_semantics=sem, vmem_limit_bytes=vmem_mb * 1024 * 1024)


def _log_sigmoid(z):
    return jnp.minimum(z, 0.0) - jnp.log(1.0 + jnp.exp(-jnp.abs(z)))


def _silu(g):
    return g * (1.0 / (1.0 + jnp.exp(-g)))


def _dot_nt(a, b):
    return lax.dot_general(a, b, (((1,), (1,)), ((), ())), preferred_element_type=F32)


def _dot_tn(a, b):
    return lax.dot_general(a, b, (((0,), (0,)), ((), ())), preferred_element_type=F32)


def _dot(a, b):
    return jnp.dot(a, b, preferred_element_type=F32)


def _pack_bf16_pairs(x):
    bits = pltpu.bitcast(x.astype(BF16).astype(F32), I32)
    w = x.shape[1] // 2
    return bits[:, w:] | lax.shift_right_logical(bits[:, :w], 16)


def _unpack_bf16_pairs(words):
    lo = pltpu.bitcast(lax.shift_left(words, 16), F32)
    hi = pltpu.bitcast(words & jnp.int32(-65536), F32)
    return lo, hi


IP_TM = 1024
IP_TN = 1024


def _in_proj_kernel(x_ref, n1_ref, w_ref, wga_ref, cs_ref, cos_ref, sin_ref, o_ref, ga_ref, xn_ref):
    j = pl.program_id(1)

    @pl.when(j == 0)
    def _():
        x = x_ref[...]
        ms = jnp.mean(x * x, axis=-1, keepdims=True)
        xn = (x * lax.rsqrt(ms + EPS) * n1_ref[...]).astype(BF16)
        xn_ref[...] = xn
        ga_ref[...] = _dot(xn, wga_ref[...])

    acc = _dot(xn_ref[...], w_ref[...]) * cs_ref[...]
    n_rope_blocks = 2 * RET_WIDTH // IP_TN

    @pl.when(j < n_rope_blocks)
    def _():
        cos = cos_ref[...]
        sin = sin_ref[...]
        for h in range(IP_TN // RET_DK):
            x1 = acc[:, 2 * h * LANE:(2 * h + 1) * LANE]
            x2 = acc[:, (2 * h + 1) * LANE:(2 * h + 2) * LANE]
            o_ref[2 * h] = (x1 * cos - x2 * sin).astype(BF16)
            o_ref[2 * h + 1] = (x1 * sin + x2 * cos).astype(BF16)

    @pl.when(j >= n_rope_blocks)
    def _():
        for c in range(IP_TN // LANE):
            o_ref[c] = acc[:, c * LANE:(c + 1) * LANE].astype(BF16)


def _in_proj(x2d, n1w, w_main, w_ga, colscale, cos, sin, seq_len):
    T = x2d.shape[0]
    tm, tn = IP_TM, IP_TN
    nlb = seq_len // tm
    return pl.pallas_call(
        _in_proj_kernel,
        grid=(T // tm, IN_MAIN // tn),
        in_specs=[
            pl.BlockSpec((tm, D_MODEL), lambda i, j: (i, 0)),
            pl.BlockSpec((1, D_MODEL), lambda i, j: (0, 0)),
            pl.BlockSpec((D_MODEL, tn), lambda i, j: (0, j)),
            pl.BlockSpec((D_MODEL, LANE), lambda i, j: (0, 0)),
            pl.BlockSpec((1, tn), lambda i, j: (0, j)),
            pl.BlockSpec((tm, LANE), lambda i, j: (i % nlb, 0)),
            pl.BlockSpec((tm, LANE), lambda i, j: (i % nlb, 0)),
        ],
        out_specs=[
            pl.BlockSpec((tn // LANE, tm, LANE), lambda i, j: (j, i, 0)),
            pl.BlockSpec((tm, LANE), lambda i, j: (i, 0)),
        ],
        out_shape=[
            jax.ShapeDtypeStruct((IN_MAIN // LANE, T, LANE), BF16),
            jax.ShapeDtypeStruct((T, LANE), F32),
        ],
        scratch_shapes=[pltpu.VMEM((tm, D_MODEL), BF16)],
        compiler_params=_cparams(("parallel", "arbitrary"), 48),
        name="in_proj",
    )(x2d, n1w, w_main, w_ga, colscale, cos, sin)


GATE_TM = 512


def _gates_kernel(ga_ref, up_ref, bias_ref, lf_ref, lb_ref, bf_ref, bb_ref):
    z = _dot(ga_ref[...].astype(BF16), up_ref[...]) + bias_ref[...]
    la = _log_sigmoid(z) * (LOG2_E / GLA_TAU)
    hi = la.astype(BF16)
    lo = (la - hi.astype(F32)).astype(BF16)
    kw = GLA_KEY_WIDTH
    bf_ref[...] = _dot(lf_ref[...], hi[:, :kw]) + _dot(lf_ref[...], lo[:, :kw])
    bb_ref[...] = _dot(lb_ref[...], hi[:, kw:]) + _dot(lb_ref[...], lo[:, kw:])


def _gla_gates(ga, up_pad, bias, lf, lb):
    T = ga.shape[0]
    tm = GATE_TM
    kw = GLA_KEY_WIDTH
    return pl.pallas_call(
        _gates_kernel,
        grid=(T // tm,),
        in_specs=[
            pl.BlockSpec((tm, LANE), lambda i: (i, 0)),
            pl.BlockSpec((LANE, 2 * kw), lambda i: (0, 0)),
            pl.BlockSpec((1, 2 * kw), lambda i: (0, 0)),
            pl.BlockSpec((tm, tm), lambda i: (0, 0)),
            pl.BlockSpec((tm, tm), lambda i: (0, 0)),
        ],
        out_specs=[pl.BlockSpec((tm, kw), lambda i: (i, 0)), pl.BlockSpec((tm, kw), lambda i: (i, 0))],
        out_shape=[jax.ShapeDtypeStruct((T, kw), F32), jax.ShapeDtypeStruct((T, kw), F32)],
        compiler_params=_cparams(("parallel",), 32),
        name="gla_gates",
    )(ga, up_pad, bias, lf, lb)


def _wide(ref, rows):
    return jnp.concatenate([ref[0, rows, :], ref[1, rows, :]], axis=1)


def _finish_heads(tot, gn, gate):
    ms = jnp.mean(tot * tot, axis=-1, keepdims=True)
    yn = tot * lax.rsqrt(ms + EPS) * gn
    return (yn * _silu(gate.astype(F32))).astype(BF16)


RET_TB = 2048


def _ret_kernel(dl_ref, q_ref, k_ref, v_ref, *rest, reverse):
    if reverse:
        g_ref, of_ref, gn_ref, o_ref, s_ref, intra_ref, qd_ref, kd_ref, cd_ref, p_ref, u_ref = rest
    else:
        o_ref, s_ref, intra_ref, qd_ref, kd_ref, cd_ref, p_ref, u_ref = rest
    h = pl.program_id(1)
    n = pl.program_id(2)
    C = RET_CHUNK

    @pl.when(n == 0)
    def _():
        s_ref[...] = jnp.zeros_like(s_ref)
        logit = dl_ref[1 if reverse else 0, h]
        lg = _log_sigmoid(jnp.full((C, RET_DV), logit, F32))
        lg_c = _log_sigmoid(jnp.full((C, C), logit, F32))
        lg_r = _log_sigmoid(jnp.full((1, RET_DV), logit, F32))
        ri = lax.broadcasted_iota(I32, (C, RET_DV), 0).astype(F32)
        rc = lax.broadcasted_iota(I32, (C, C), 0).astype(F32)
        cc = lax.broadcasted_iota(I32, (C, C), 1).astype(F32)
        diff = (cc - rc) if reverse else (rc - cc)
        intra_ref[...] = jnp.where(diff >= 0, jnp.exp(lg_c * diff), 0.0)
        if reverse:
            qd_ref[...] = jnp.exp(lg * (C - ri))
            kd_ref[...] = jnp.exp(lg * ri)
        else:
            qd_ref[...] = jnp.exp(lg * (ri + 1.0))
            kd_ref[...] = jnp.exp(lg * (C - 1.0 - ri))
        cd_ref[...] = jnp.exp(lg_r * C)

    nchunks = o_ref.shape[0] // C
    for c in range(nchunks):
        rows = slice(c * C, (c + 1) * C)
        k = _wide(k_ref, rows)
        p_ref[c] = (_dot_nt(_wide(q_ref, rows), k) * intra_ref[...]).astype(BF16)
        kd = (k.astype(F32) * kd_ref[...]).astype(BF16)
        u_ref[c] = _dot_tn(kd, _wide(v_ref, rows))
    order = range(nchunks - 1, -1, -1) if reverse else range(nchunks)
    for c in order:
        rows = slice(c * C, (c + 1) * C)
        state = s_ref[...]
        o = _dot(p_ref[c], _wide(v_ref, rows)) + _dot(_wide(q_ref, rows), state.astype(BF16)) * qd_ref[...]
        s_ref[...] = state * cd_ref[...] + u_ref[c]
        if reverse:
            tot = of_ref[rows, :].astype(F32) + o
            o_ref[rows, :] = _finish_heads(tot, gn_ref[...], _wide(g_ref, rows))
        else:
            o_ref[rows, :] = o.astype(BF16)


def _ret_parts(proj, decay_logit, rb, reverse, o_fwd=None, gn_w=None):
    T = proj.shape[1]
    tb = RET_TB
    dk, dv, C = RET_DK, RET_DV, RET_CHUNK

    def head(base):
        return pl.BlockSpec((dk // LANE, tb, LANE), lambda b, h, n: (base // dk + h, rb(b, n), 0))

    in_specs = [pl.BlockSpec(memory_space=pltpu.SMEM), head(_RQ), head(_RK), head(_RV)]
    args = [decay_logit, proj, proj, proj]
    if reverse:
        in_specs += [
            head(_RG),
            pl.BlockSpec((tb, dv), lambda b, h, n: (rb(b, n), h)),
            pl.BlockSpec((1, dv), lambda b, h, n: (0, h)),
        ]
        args += [proj, o_fwd, gn_w]
    out_spec = pl.BlockSpec((tb, dv), lambda b, h, n: (rb(b, n), h))
    out_shape = jax.ShapeDtypeStruct((T, RET_WIDTH), BF16)
    scratch = [
        pltpu.VMEM((dk, dv), F32),
        pltpu.VMEM((C, C), F32),
        pltpu.VMEM((C, dv), F32),
        pltpu.VMEM((C, dk), F32),
        pltpu.VMEM((1, dv), F32),
        pltpu.VMEM((tb // C, C, C), BF16),
        pltpu.VMEM((tb // C, dk, dv), F32),
    ]
    return in_specs, args, out_spec, out_shape, scratch


GLA_TB = 2048
GLA_UNROLL = 16


GLA_LEVELS = (32, 16, 8, 4, 2, 1)
SUBLANES = 8


def _gla_tables(reverse):
    C = GLA_CHUNK
    r = np.arange(C)
    masks = np.zeros((len(GLA_LEVELS) + 1, C, C), np.float32)
    for l, s in enumerate(GLA_LEVELS):
        upper = (r & s) != 0
        same = (r[:, None] // (2 * s)) == (r[None, :] // (2 * s))
        lhs_rows = ~upper if reverse else upper
        masks[l] = same & lhs_rows[:, None] & ~lhs_rows[None, :]
    masks[-1] = np.eye(C)
    return jnp.asarray(masks, F32)


def _gla_kernel(q_ref, k_ref, v_ref, b_ref, mask_ref, *rest, reverse):
    if reverse:
        g_ref, of_ref, gn_ref, o_ref, st_ref, sc_ref = rest
    else:
        o_ref, st_ref, sc_ref = rest
    n = pl.program_id(2)
    C = GLA_CHUNK

    @pl.when(n == 0)
    def _():
        st_ref[...] = jnp.zeros_like(st_ref)

    nchunks = q_ref.shape[0] // C
    sub_row = lax.broadcasted_iota(I32, (SUBLANES, GLA_DK), 0)
    zero_rows = jnp.zeros((SUBLANES, GLA_DK), F32)

    def chunk_scores(c, carry):
        c0 = pl.multiple_of(c * C, C)
        qb = q_ref[pl.ds(c0, C), :]
        kb = k_ref[pl.ds(c0, C), :]
        q = qb.astype(F32)
        k = kb.astype(F32)
        b = b_ref[pl.ds(c0, C), :]

        def mid_row(r):
            return jnp.broadcast_to(b[r:r + 1, :], (SUBLANES, GLA_DK))

        scores = mask_ref[len(GLA_LEVELS)] * _dot_nt(qb, kb)
        for l, s in enumerate(GLA_LEVELS):
            lhs, rhs = [], []
            for g in range(C // SUBLANES):
                r0 = g * SUBLANES
                rows = slice(r0, r0 + SUBLANES)
                if s >= SUBLANES:
                    m = mid_row((r0 // (2 * s)) * (2 * s) + s)
                    is_lhs = ((r0 & s) != 0) != reverse
                    if is_lhs:
                        lhs.append(q[rows] * jnp.exp2(b[rows] - m))
                        rhs.append(zero_rows)
                    else:
                        lhs.append(zero_rows)
                        rhs.append(k[rows] * jnp.exp2(m - b[rows]))
                else:
                    m = mid_row(r0 + SUBLANES - s)
                    for blk in range(SUBLANES // (2 * s) - 2, -1, -1):
                        m = jnp.where(sub_row < (blk + 1) * 2 * s, mid_row(r0 + blk * 2 * s + s), m)
                    upper = (sub_row & s) != 0
                    is_lhs = jnp.logical_not(upper) if reverse else upper
                    lhs.append(jnp.where(is_lhs, q[rows] * jnp.exp2(b[rows] - m), 0.0))
                    rhs.append(jnp.where(is_lhs, 0.0, k[rows] * jnp.exp2(m - b[rows])))
            lhs = jnp.concatenate(lhs, axis=0).astype(BF16)
            rhs = jnp.concatenate(rhs, axis=0).astype(BF16)
            scores = scores + mask_ref[l] * _dot_nt(lhs, rhs)
        sc_ref[c] = scores.astype(BF16)
        return carry

    lax.fori_loop(0, nchunks, chunk_scores, 0, unroll=GLA_UNROLL)

    def chunk(ci, carry):
        c = (nchunks - 1 - ci) if reverse else ci
        c0 = pl.multiple_of(c * C, C)
        q = q_ref[pl.ds(c0, C), :].astype(F32)
        k = k_ref[pl.ds(c0, C), :].astype(F32)
        v = _wide(v_ref, pl.ds(c0, C))
        b = b_ref[pl.ds(c0, C), :]
        b_end = b[0:1, :] if reverse else b[C - 1:C, :]

        st = st_ref[...]
        o = _dot_nt((q * jnp.exp2(b)).astype(BF16), st.astype(BF16))
        ke = (k * jnp.exp2(b_end - b)).astype(BF16)
        st_ref[...] = st * jnp.exp2(b_end) + _dot_tn(v, ke)
        o = o + _dot(sc_ref[c], v)
        if reverse:
            tot = of_ref[pl.ds(c0, C), :].astype(F32) + o
            o_ref[pl.ds(c0, C), :] = _finish_heads(tot, gn_ref[...], _wide(g_ref, pl.ds(c0, C)))
        else:
            o_ref[pl.ds(c0, C), :] = o.astype(BF16)
        return carry

    lax.fori_loop(0, nchunks, chunk, 0, unroll=GLA_UNROLL)


def _gla_parts(proj, bcum, rb, reverse, o_fwd=None, gn_w=None):
    T = proj.shape[1]
    tb = GLA_TB
    dk, dv = GLA_DK, GLA_DV
    masks = _gla_tables(reverse)

    def key_block(base):
        return pl.BlockSpec((None, tb, LANE), lambda b, h, n: (base // dk + h, rb(b, n), 0))

    def value_block(base):
        return pl.BlockSpec((dv // LANE, tb, LANE), lambda b, h, n: (base // dv + h, rb(b, n), 0))

    in_specs = [
        key_block(_GQ), key_block(_GK), value_block(_GV),
        pl.BlockSpec((tb, dk), lambda b, h, n: (rb(b, n), h)),
        pl.BlockSpec(masks.shape, lambda b, h, n: (0, 0, 0)),
    ]
    args = [proj, proj, proj, bcum, masks]
    if reverse:
        in_specs += [
            value_block(_GG),
            pl.BlockSpec((tb, dv), lambda b, h, n: (rb(b, n), h)),
            pl.BlockSpec((1, dv), lambda b, h, n: (0, h)),
        ]
        args += [proj, o_fwd, gn_w]
    out_spec = pl.BlockSpec((tb, dv), lambda b, h, n: (rb(b, n), h))
    out_shape = jax.ShapeDtypeStruct((T, GLA_WIDTH), BF16)
    scratch = [pltpu.VMEM((dv, dk), F32), pltpu.VMEM((tb // GLA_CHUNK, GLA_CHUNK, GLA_CHUNK), BF16)]
    return in_specs, args, out_spec, out_shape, scratch


def _mixer_kernel(*refs, reverse, n_ret_in, n_gla_in, n_ret_scratch):
    ret_in = refs[:n_ret_in]
    gla_in = refs[n_ret_in:n_ret_in + n_gla_in]
    ret_out, gla_out = refs[n_ret_in + n_gla_in:n_ret_in + n_gla_in + 2]
    scratch = refs[n_ret_in + n_gla_in + 2:]
    _ret_kernel(*ret_in, ret_out, *scratch[:n_ret_scratch], reverse=reverse)
    _gla_kernel(*gla_in, gla_out, *scratch[n_ret_scratch:], reverse=reverse)


def _mixer_scan(proj, decay_logit, bcum, batch, seq_len, reverse, o_fwd=(None, None), gn_w=(None, None)):
    assert RET_TB == GLA_TB and RET_HEADS == GLA_HEADS
    nb = seq_len // RET_TB

    def rb(b, n):
        return b * nb + ((nb - 1 - n) if reverse else n)

    r_specs, r_args, r_out, r_shape, r_scratch = _ret_parts(proj, decay_logit, rb, reverse, o_fwd[0], gn_w[0])
    g_specs, g_args, g_out, g_shape, g_scratch = _gla_parts(proj, bcum, rb, reverse, o_fwd[1], gn_w[1])
    return pl.pallas_call(
        functools.partial(_mixer_kernel, reverse=reverse, n_ret_in=len(r_specs), n_gla_in=len(g_specs),
                          n_ret_scratch=len(r_scratch)),
        grid=(batch, RET_HEADS, nb),
        in_specs=r_specs + g_specs,
        out_specs=[r_out, g_out],
        out_shape=[r_shape, g_shape],
        scratch_shapes=r_scratch + g_scratch,
        compiler_params=_cparams(("parallel", "parallel", "arbitrary"), 48),
        name="mixer_bwd" if reverse else "mixer_fwd",
    )(*r_args, *g_args)


OP_TM = 512


def _out_proj_kernel(mr_ref, mg_ref, w0_ref, w1_ref, x_ref, n2_ref, rh_ref, rl_ref, h_ref, xn_ref, aff_ref,
                     hs_ref):
    s = pl.program_id(0)
    slot = s % 2

    @pl.when(s == 0)
    def _():
        hs_ref[1] = jnp.zeros(hs_ref.shape[1:], F32)

    hp = hs_ref[1 - slot]
    ms = jnp.mean(hp * hp, axis=-1, keepdims=True)
    xn = hp * lax.rsqrt(ms + EPS) * n2_ref[...]
    xh = xn.astype(BF16)
    xn_ref[...] = _pack_bf16_pairs(xn)
    xl = (xn - xh.astype(F32)).astype(BF16)
    lt = _dot_nt(rh_ref[...], xh) + _dot_nt(rh_ref[...], xl) + _dot_nt(rl_ref[...], xh)
    m = jnp.max(lt, axis=0, keepdims=True)
    e = jnp.exp(lt - m)
    aff_ref[...] = e / jnp.sum(e, axis=0, keepdims=True)

    h = x_ref[...] + _dot(mr_ref[...], w0_ref[...]) + _dot(mg_ref[...], w1_ref[...])
    h_ref[...] = h
    hs_ref[slot] = h


def _out_proj(mix_r, mix_g, w_out, x2d, n2w, r_hi, r_lo):
    T = x2d.shape[0]
    tm = OP_TM
    half = RET_WIDTH
    nblk = T // tm

    def head(s):
        return jnp.minimum(s, nblk - 1)

    def tail(s):
        return jnp.maximum(s - 1, 0)

    return pl.pallas_call(
        _out_proj_kernel,
        grid=(nblk + 1,),
        in_specs=[
            pl.BlockSpec((tm, half), lambda s: (head(s), 0)),
            pl.BlockSpec((tm, half), lambda s: (head(s), 0)),
            pl.BlockSpec((half, D_MODEL), lambda s: (0, 0)),
            pl.BlockSpec((half, D_MODEL), lambda s: (1, 0)),
            pl.BlockSpec((tm, D_MODEL), lambda s: (head(s), 0)),
            pl.BlockSpec((1, D_MODEL), lambda s: (0, 0)),
            pl.BlockSpec((N_EXPERTS, D_MODEL), lambda s: (0, 0)),
            pl.BlockSpec((N_EXPERTS, D_MODEL), lambda s: (0, 0)),
        ],
        out_specs=[
            pl.BlockSpec((tm, D_MODEL), lambda s: (head(s), 0)),
            pl.BlockSpec((tm, D_MODEL // 2), lambda s: (tail(s), 0)),
            pl.BlockSpec((N_EXPERTS, tm), lambda s: (0, tail(s))),
        ],
        out_shape=[
            jax.ShapeDtypeStruct((T, D_MODEL), F32),
            jax.ShapeDtypeStruct((T, D_MODEL // 2), I32),
            jax.ShapeDtypeStruct((N_EXPERTS, T), F32),
        ],
        scratch_shapes=[pltpu.VMEM((2, tm, D_MODEL), F32)],
        compiler_params=_cparams(("arbitrary",), 60),
        name="out_proj",
    )(mix_r, mix_g, w_out, w_out, x2d, n2w, r_hi, r_lo)


def _select_kernel(a_ref, pos_ref, posb_ref, rankb_ref, affb_ref, *, cap):
    E, T = a_ref.shape
    tt = MOE_TT

    def count(pred):
        return jnp.sum(pred.astype(F32), axis=1, keepdims=True)

    def bisect(i, tau):
        cand = tau | jnp.left_shift(jnp.int32(1), 30 - i)
        bits = pltpu.bitcast(a_ref[...], I32)
        return jnp.where(count(bits >= cand) >= cap, cand, tau)

    tau = lax.fori_loop(0, 31, bisect, jnp.zeros((E, 1), I32))
    bits_all = pltpu.bitcast(a_ref[...], I32)
    quota = cap - count(bits_all > tau)

    before = (lax.broadcasted_iota(I32, (tt, tt), 0) < lax.broadcasted_iota(I32, (tt, tt), 1)).astype(BF16)
    below = (lax.broadcasted_iota(I32, (E, E), 1) < lax.broadcasted_iota(I32, (E, E), 0)).astype(BF16)

    def block(j, carry):
        c_eq, c_sel = carry
        off = pl.multiple_of(j * tt, tt)
        aff = a_ref[:, pl.ds(off, tt)]
        bits = pltpu.bitcast(aff, I32)
        eq = bits == tau
        eqf = eq.astype(F32)
        rank_eq = _dot(eqf.astype(BF16), before) + c_eq
        sel = (bits > tau) | (eq & (rank_eq < quota))
        self_ = sel.astype(F32)
        selb = self_.astype(BF16)
        slot = _dot(selb, before) + c_sel
        pos = jnp.where(sel, slot, -1.0).astype(I32)
        pos_ref[:, pl.ds(off, tt)] = pos
        per_tok = jnp.broadcast_to(jnp.sum(self_, axis=0, keepdims=True), (E, tt))
        rank = _dot(per_tok.astype(BF16), before) + _dot(below, selb)
        posb_ref[j] = pos
        rankb_ref[j] = jnp.where(sel, rank, -1.0).astype(I32)
        affb_ref[j] = aff
        return (c_eq + jnp.sum(eqf, axis=1, keepdims=True), c_sel + jnp.sum(self_, axis=1, keepdims=True))

    zero = jnp.zeros((E, 1), F32)
    lax.fori_loop(0, T // tt, block, (zero, zero))


def _select(aff, cap):
    E, T = aff.shape
    nb = T // MOE_TT
    blk = jax.ShapeDtypeStruct((nb, E, MOE_TT), I32)
    return pl.pallas_call(
        functools.partial(_select_kernel, cap=cap),
        out_shape=[jax.ShapeDtypeStruct((E, T), I32), blk, blk, jax.ShapeDtypeStruct((nb, E, MOE_TT), F32)],
        compiler_params=pltpu.CompilerParams(vmem_limit_bytes=40 * 1024 * 1024),
        name="select",
    )(aff)


def _regroup_rows(T):
    nb = T // MOE_TT
    rows = CAPACITY_FACTOR * T + SC_GATHER_ROWS * nb
    return -(-rows // MOE_TW) * MOE_TW


def _combine_schedule(posb, T):
    nb = posb.shape[0]
    tw, g = MOE_TW, SC_GATHER_ROWS
    n = jnp.sum((posb >= 0).reshape(nb, -1), axis=1).astype(I32)
    seg = (n + g - 1) // g * g
    hi = jnp.cumsum(seg)
    lo = hi - seg
    off = jnp.concatenate([jnp.zeros((1,), I32), hi])
    nwin_max = (N_EXPERTS * MOE_TT) // tw + 1
    w0 = lo // tw
    w1 = jnp.where(seg > 0, (hi - 1) // tw, w0)
    cand = jnp.arange(nwin_max, dtype=I32)
    win = w0[:, None] + cand[None, :]
    valid = (win <= w1[:, None]).reshape(-1)
    nwin_total = _regroup_rows(T) // tw
    pmax = nb + nwin_total
    jv = jnp.broadcast_to(jnp.arange(nb, dtype=I32)[:, None], win.shape).reshape(-1)
    wv = jnp.minimum(win, nwin_total - 1).reshape(-1)
    total = jnp.sum(valid.astype(I32))
    dst = jnp.where(valid, jnp.cumsum(valid.astype(I32)) - 1, pmax)
    pj, pw = (jnp.zeros((pmax,), I32).at[dst].set(a, mode="drop") for a in (jv, wv))
    real = jnp.arange(pmax, dtype=I32) < total
    pj, pw = (jnp.where(real, a, a[total - 1]) for a in (pj, pw))
    first = jnp.concatenate([jnp.ones((1,), bool), pj[1:] != pj[:-1]])
    last = jnp.concatenate([pj[1:] != pj[:-1], jnp.ones((1,), bool)]) | (jnp.arange(pmax, dtype=I32) == total - 1)
    flag = jnp.where(real, first.astype(I32) + 2 * last.astype(I32) + 4, 0)
    return off, (pj, pw, flag, lo[pj], hi[pj])


SC_LANES = 16
SC_CORES = 2
SC_SUBCORES = 16
SC_GATHER_ROWS = 32


def _dispatch(pos, xw, cap):
    E, T = pos.shape
    W = xw.shape[1]
    G = SC_GATHER_ROWS
    part_rows = cap // SC_CORES
    mesh = plsc.VectorSubcoreMesh(core_axis_name="c", subcore_axis_name="s")

    @pl.kernel(
        out_type=jax.ShapeDtypeStruct((E * cap, W), I32),
        mesh=mesh,
        scratch_types=[pltpu.VMEM((T,), I32), pltpu.VMEM((cap,), I32), pltpu.VMEM((G, W), I32)],
        compiler_params=pltpu.CompilerParams(needs_layout_passes=False),
        name="sc_dispatch",
    )
    def run(pos_hbm, x_hbm, xe_hbm, pos_v, idx_v, buf):
        e = lax.axis_index("s")
        part = lax.axis_index("c")
        pltpu.sync_copy(pos_hbm.at[e], pos_v)
        lane = lax.iota(I32, SC_LANES)

        @pl.loop(0, T // SC_LANES)
        def _(i):
            off = pl.multiple_of(i * SC_LANES, SC_LANES)
            p = pos_v[pl.ds(off, SC_LANES)]
            plsc.store_scatter(idx_v, [p], lane + off, mask=p >= 0)

        @pl.loop(0, part_rows // G)
        def _(g):
            o = pl.multiple_of(part * part_rows + g * G, G)
            pltpu.sync_copy(x_hbm.at[idx_v.at[pl.ds(o, G)]], buf)
            pltpu.sync_copy(buf, xe_hbm.at[pl.ds(e * cap + o, G)])

    return run(pos, xw)


FFN_TM = 2048
FFN_UNPACK_ROWS = 256


def _ffn_tile_width(cap):
    return 512 if min(FFN_TM, cap) <= 1024 else 256


def _ffn_kernel(x_ref, wg_ref, wu_ref, wd_ref, o_ref, xb_ref, hid_ref, *, tw):
    s = pl.program_id(2)
    n_up = D_FF // tw

    @pl.when(s == 0)
    def _():
        half = D_MODEL // 2

        def unpack_rows(i, carry):
            r = pl.multiple_of(i * FFN_UNPACK_ROWS, FFN_UNPACK_ROWS)
            lo, hi = _unpack_bf16_pairs(x_ref[pl.ds(r, FFN_UNPACK_ROWS), :])
            xb_ref[pl.ds(r, FFN_UNPACK_ROWS), :half] = lo.astype(BF16)
            xb_ref[pl.ds(r, FFN_UNPACK_ROWS), half:] = hi.astype(BF16)
            return carry

        lax.fori_loop(0, x_ref.shape[0] // FFN_UNPACK_ROWS, unpack_rows, 0)

    @pl.when(s < n_up)
    def _():
        x = xb_ref[...]
        g = _dot(x, wg_ref[...].astype(BF16))
        u = _dot(x, wu_ref[...].astype(BF16))
        col = pl.multiple_of(s * tw, tw)
        hid_ref[:, pl.ds(col, tw)] = (_silu(g) * u).astype(BF16)

    @pl.when(s >= n_up)
    def _():
        o_ref[...] = _pack_bf16_pairs(_dot(hid_ref[...], wd_ref[...].astype(BF16)))


def _ffn(xe, w_gate, w_up, w_down):
    E, cap, _ = xe.shape
    tm = min(FFN_TM, cap)
    tw = _ffn_tile_width(cap)
    n_up, n_down = D_FF // tw, D_MODEL // tw

    def up(e, m, s):
        return (e, 0, jnp.minimum(s, n_up - 1))

    def down(s):
        return jnp.maximum(s - n_up, 0)

    return pl.pallas_call(
        functools.partial(_ffn_kernel, tw=tw),
        grid=(E, cap // tm, n_up + n_down),
        in_specs=[
            pl.BlockSpec((None, tm, D_MODEL // 2), lambda e, m, s: (e, m, 0)),
            pl.BlockSpec((None, D_MODEL, tw), up),
            pl.BlockSpec((None, D_MODEL, tw), up),
            pl.BlockSpec((None, D_FF, tw), lambda e, m, s: (e, 0, down(s))),
        ],
        out_specs=pl.BlockSpec((None, tm, tw // 2), lambda e, m, s: (e, m, down(s))),
        out_shape=jax.ShapeDtypeStruct((E, cap, D_MODEL // 2), I32),
        scratch_shapes=[pltpu.VMEM((tm, D_MODEL), BF16), pltpu.VMEM((tm, D_FF), BF16)],
        compiler_params=_cparams(("parallel", "parallel", "arbitrary"), 60),
        name="ffn",
    )(xe, w_gate, w_up, w_down)


def _regroup(posb, rankb, affb, off, yw, cap, rows):
    NB, EB = posb.shape
    W = yw.shape[1]
    G, L, tt = SC_GATHER_ROWS, SC_LANES, MOE_TT
    n_workers = SC_CORES * SC_SUBCORES
    split = max(1, n_workers // NB)
    per = -(-NB * split // n_workers)
    mesh = plsc.VectorSubcoreMesh(core_axis_name="c", subcore_axis_name="s")
    off_pad = jnp.pad(off, (0, L))

    @pl.kernel(
        out_type=(jax.ShapeDtypeStruct((rows, W), I32), jax.ShapeDtypeStruct((rows,), I32),
                  jax.ShapeDtypeStruct((rows,), F32)),
        mesh=mesh,
        scratch_types=[pltpu.VMEM((EB,), I32), pltpu.VMEM((EB,), I32), pltpu.VMEM((EB,), F32),
                       pltpu.VMEM((EB,), I32), pltpu.VMEM((EB,), I32), pltpu.VMEM((EB,), F32),
                       pltpu.VMEM((G, W), I32), pltpu.VMEM((NB + 1 + L,), I32)],
        compiler_params=pltpu.CompilerParams(needs_layout_passes=False),
        name="sc_regroup",
    )
    def run(posb_hbm, rankb_hbm, affb_hbm, off_hbm, y_hbm, yg_hbm, tok_hbm, gate_hbm,
            pos_v, rank_v, aff_v, src_v, tok_v, gate_v, buf, off_v):
        wid = lax.axis_index("c") * SC_SUBCORES + lax.axis_index("s")
        pltpu.sync_copy(off_hbm, off_v)
        lane = lax.iota(I32, L)
        zi = jnp.zeros((L,), I32)
        zf = jnp.zeros((L,), F32)

        def regroup_block(j, part):
            pltpu.sync_copy(posb_hbm.at[j], pos_v)
            pltpu.sync_copy(rankb_hbm.at[j], rank_v)
            pltpu.sync_copy(affb_hbm.at[j], aff_v)
            lo = jnp.max(plsc.load_gather(off_v, [zi + j]))
            hi = jnp.max(plsc.load_gather(off_v, [zi + j + 1]))

            @pl.loop(0, EB // L)
            def _(i):
                o = pl.multiple_of(i * L, L)
                src_v[pl.ds(o, L)] = zi
                tok_v[pl.ds(o, L)] = zi
                gate_v[pl.ds(o, L)] = zf

            @pl.loop(0, EB // L)
            def _(i):
                o = pl.multiple_of(i * L, L)
                p = pos_v[pl.ds(o, L)]
                r = rank_v[pl.ds(o, L)]
                m = p >= 0
                e = i // (tt // L)
                t0 = j * tt + (i % (tt // L)) * L
                plsc.store_scatter(src_v, [r], p + e * cap, mask=m)
                plsc.store_scatter(tok_v, [r], lane + t0, mask=m)
                plsc.store_scatter(gate_v, [r], aff_v[pl.ds(o, L)], mask=m)

            @pl.loop(part, (hi - lo) // G, step=split)
            def _(g):
                o = pl.multiple_of(g * G, G)
                dst = pl.multiple_of(lo + o, G)
                pltpu.sync_copy(y_hbm.at[src_v.at[pl.ds(o, G)]], buf)
                pltpu.sync_copy(buf, yg_hbm.at[pl.ds(dst, G)])
                pltpu.sync_copy(tok_v.at[pl.ds(o, G)], tok_hbm.at[pl.ds(dst, G)])
                pltpu.sync_copy(gate_v.at[pl.ds(o, G)], gate_hbm.at[pl.ds(dst, G)])

        @pl.loop(0, per)
        def _(k):
            unit = wid * per + k

            @pl.when(unit < NB * split)
            def _():
                regroup_block(unit // split, unit % split)

    return run(posb, rankb, affb, off_pad, yw)


def _combine_kernel(pj_ref, pw_ref, pf_ref, plo_ref, phi_ref, tok_ref, gate_ref, yg_ref, h_ref, nf_ref,
                    o_ref, acc_ref, *, group):
    p = pl.program_id(0)
    flag = pf_ref[p]
    tw, tt = MOE_TW, MOE_TT
    half = D_MODEL // 2

    hw = group // 2

    def col_blocks():
        for n in range(D_MODEL // group):
            yield slice(n * hw, (n + 1) * hw), slice(n * group, n * group + hw)
            yield slice(half + n * hw, half + (n + 1) * hw), slice(n * group + hw, (n + 1) * group)

    @pl.when((flag & 1) != 0)
    def _():
        for packed, natural in col_blocks():
            acc_ref[:, packed] = h_ref[:, natural]

    @pl.when((flag & 4) != 0)
    def _():
        lo, hi = plo_ref[p], phi_ref[p]
        row0 = pw_ref[p] * tw
        rid = lax.broadcasted_iota(I32, (tw, 1), 0) + row0
        keep = (rid >= lo) & (rid < hi)
        y_lo, y_hi = _unpack_bf16_pairs(yg_ref[...])
        y_lo = jnp.where(keep, y_lo, 0.0).astype(BF16)
        y_hi = jnp.where(keep, y_hi, 0.0).astype(BF16)
        tid = lax.broadcasted_iota(I32, (tt, tw), 0) + pj_ref[p] * tt
        cid = lax.broadcasted_iota(I32, (tt, tw), 1) + row0
        hit = (tok_ref[...] == tid) & (cid >= lo) & (cid < hi)
        weights = jnp.where(hit, gate_ref[...], 0.0).astype(BF16)
        acc_ref[:, :half] += _dot(weights, y_lo)
        acc_ref[:, half:] += _dot(weights, y_hi)

    @pl.when((flag & 2) != 0)
    def _():
        y = acc_ref[...]
        scale = lax.rsqrt(jnp.mean(y * y, axis=-1, keepdims=True) + EPS)
        for packed, natural in col_blocks():
            o_ref[:, natural] = acc_ref[:, packed] * scale * nf_ref[:, natural]


def _combine(lists, tok, gate, yg, h, nfw, group):
    pj, pw, pf, plo, phi = lists
    T = h.shape[0]
    tw, tt = MOE_TW, MOE_TT
    nwin = yg.shape[0] // tw
    grid_spec = pltpu.PrefetchScalarGridSpec(
        num_scalar_prefetch=5,
        grid=(pj.shape[0],),
        in_specs=[
            pl.BlockSpec((None, 1, tw), lambda p, pj, pw, *_: (pw[p], 0, 0)),
            pl.BlockSpec((None, 1, tw), lambda p, pj, pw, *_: (pw[p], 0, 0)),
            pl.BlockSpec((tw, D_MODEL // 2), lambda p, pj, pw, *_: (pw[p], 0)),
            pl.BlockSpec((tt, D_MODEL), lambda p, pj, pw, *_: (pj[p], 0)),
            pl.BlockSpec((1, D_MODEL), lambda p, pj, pw, *_: (0, 0)),
        ],
        out_specs=pl.BlockSpec((tt, D_MODEL), lambda p, pj, pw, *_: (pj[p], 0)),
        scratch_shapes=[pltpu.VMEM((tt, D_MODEL), F32)],
    )
    return pl.pallas_call(
        functools.partial(_combine_kernel, group=group),
        grid_spec=grid_spec,
        out_shape=jax.ShapeDtypeStruct((T, D_MODEL), F32),
        compiler_params=_cparams(("arbitrary",), 48),
        name="combine",
    )(pj, pw, pf, plo, phi, tok.reshape(nwin, 1, tw), gate.reshape(nwin, 1, tw), yg, h, nfw)


def _rope_tables(seq_len):
    d = RET_DK
    inv = ROPE_BASE ** (-jnp.arange(0, d, 2, dtype=F32) / d)
    ang = jnp.arange(seq_len, dtype=F32)[:, None] * inv[None, :]
    return jnp.cos(ang), jnp.sin(ang)


def _chunk_tri(n, chunk, upper):
    r = np.arange(n)
    same = (r[:, None] // chunk) == (r[None, :] // chunk)
    tri = (r[:, None] <= r[None, :]) if upper else (r[:, None] >= r[None, :])
    return jnp.asarray(same & tri, BF16)


def _prep_params(norm1_w, w_in, ret_gn_w, gla_gate_up, gla_gate_bias, gla_gn_w, w_out, norm2_w, router_w,
                 normf_w):
    w = w_in[0]
    w_main = w[:, :IN_MAIN].astype(BF16)
    w_ga = jnp.pad(w[:, IN_MAIN:], ((0, 0), (0, LANE - 2 * GLA_RANK))).astype(BF16)
    cs = np.ones((1, IN_MAIN), np.float32)
    cs[:, _RQ:_RQ + RET_WIDTH] = RET_DK ** -0.5
    cs[:, _GQ:_GQ + GLA_KEY_WIDTH] = GLA_DK ** -0.5
    up = gla_gate_up[0].astype(F32)
    up_pad = jnp.zeros((LANE, 2 * GLA_KEY_WIDTH), F32)
    up_pad = up_pad.at[:GLA_RANK, :GLA_KEY_WIDTH].set(up[0])
    up_pad = up_pad.at[GLA_RANK:2 * GLA_RANK, GLA_KEY_WIDTH:].set(up[1])
    rt = router_w[0].T.astype(F32)
    r_hi = rt.astype(BF16)
    r_lo = (rt - r_hi.astype(F32)).astype(BF16)
    return dict(
        n1w=norm1_w[0].reshape(1, D_MODEL).astype(F32),
        w_main=w_main, w_ga=w_ga, colscale=jnp.asarray(cs),
        up_pad=up_pad.astype(BF16),
        bias=gla_gate_bias[0].reshape(1, 2 * GLA_KEY_WIDTH).astype(F32),
        lf=_chunk_tri(GATE_TM, GLA_CHUNK, upper=False),
        lb=_chunk_tri(GATE_TM, GLA_CHUNK, upper=True),
        ret_gn=ret_gn_w[0].reshape(1, RET_WIDTH).astype(F32),
        gla_gn=gla_gn_w[0].reshape(1, GLA_WIDTH).astype(F32),
        w_out=w_out[0].astype(BF16),
        n2w=norm2_w[0].reshape(1, D_MODEL).astype(F32),
        r_hi=r_hi, r_lo=r_lo,
        nfw=normf_w.reshape(1, D_MODEL).astype(F32),
    )


def _trunk_front(x, pp, decay_logit, w_gate, w_up, w_down):
    B, L, _ = x.shape
    T = B * L
    x2d = x.reshape(T, D_MODEL)
    cos, sin = _rope_tables(L)
    proj, ga = _in_proj(x2d, pp["n1w"], pp["w_main"], pp["w_ga"], pp["colscale"], cos, sin, L)
    b_f, b_b = _gla_gates(ga, pp["up_pad"], pp["bias"], pp["lf"], pp["lb"])

    fwd = _mixer_scan(proj, decay_logit, b_f, B, L, reverse=False)
    mix_r, mix_g = _mixer_scan(proj, decay_logit, b_b, B, L, reverse=True, o_fwd=fwd,
                               gn_w=(pp["ret_gn"], pp["gla_gn"]))

    h, xn2, aff = _out_proj(mix_r, mix_g, pp["w_out"], x2d, pp["n2w"], pp["r_hi"], pp["r_lo"])

    cap = CAPACITY_FACTOR * T // N_EXPERTS
    pos, posb, rankb, affb = _select(aff, cap)
    off, c_lists = _combine_schedule(posb, T)
    xe = _dispatch(pos, xn2, cap).reshape(N_EXPERTS, cap, D_MODEL // 2)
    ye = _ffn(xe, w_gate, w_up, w_down).reshape(N_EXPERTS * cap, D_MODEL // 2)
    nb = T // MOE_TT
    yg, tok, gate = _regroup(posb.reshape(nb, -1), rankb.reshape(nb, -1), affb.reshape(nb, -1), off, ye, cap,
                             _regroup_rows(T))
    return c_lists, tok, gate, yg, h


def _trunk_back(front, nfw, shape):
    c_lists, tok, gate, yg, h = front
    cap = CAPACITY_FACTOR * h.shape[0] // N_EXPERTS
    return _combine(c_lists, tok, gate, yg, h, nfw, _ffn_tile_width(cap)).reshape(shape)


def kernel(x_prompt, x_sample, norm1_w, w_in, ret_decay_logit, ret_gn_w, gla_gate_up, gla_gate_bias,
           gla_gn_w, w_out, norm2_w, router_w, w_gate, w_up, w_down, normf_w):
    pp = _prep_params(norm1_w, w_in, ret_gn_w, gla_gate_up, gla_gate_bias, gla_gn_w, w_out, norm2_w,
                      router_w, normf_w)
    decay_logit = ret_decay_logit[0].astype(F32)
    args = (pp, decay_logit, w_gate[0], w_up[0], w_down[0])
    front_p = _trunk_front(x_prompt, *args)
    front_s = _trunk_front(x_sample, *args)
    y_prompt = _trunk_back(front_p, pp["nfw"], x_prompt.shape)
    front_s, y_prompt = lax.optimization_barrier((front_s, y_prompt))
    y_sample = _trunk_back(front_s, pp["nfw"], x_sample.shape)
    return (y_prompt, y_sample)
```

```python
import functools

import numpy as np
import jax
import jax.numpy as jnp
from jax import lax
from jax.experimental import pallas as pl
from jax.experimental.pallas import tpu as pltpu
from jax.experimental.pallas import tpu_sc as plsc

F32, BF16, I32 = jnp.float32, jnp.bfloat16, jnp.int32

D_MODEL = 2048
RET_WIDTH = 1024
RET_HEADS = 4
RET_DK = 256
RET_DV = 256
GLA_WIDTH = 1024
GLA_HEADS = 4
GLA_DK = 128
GLA_DV = 256
GLA_KEY_WIDTH = 512
GLA_RANK = 16
GLA_TAU = 16.0
RET_CHUNK = 256
GLA_CHUNK = 64
ROPE_BASE = 10000.0
N_EXPERTS = 16
CAPACITY_FACTOR = 2
D_FF = 2048
EPS = 1e-6
LOG2_E = 1.4426950408889634
IN_MAIN = 4 * RET_WIDTH + 2 * GLA_KEY_WIDTH + 2 * GLA_WIDTH

_RQ, _RK, _RV, _RG = 0, 1024, 2048, 3072
_GQ, _GK, _GV, _GG = 4096, 4608, 5120, 6144

LANE = 128
MOE_TT = 512
MOE_TW = 512


def _cparams(sem, vmem_mb):
    return pltpu.CompilerParams(dimension_semantics=sem, vmem_limit_bytes=vmem_mb * 1024 * 1024)


def _log_sigmoid(z):
    return jnp.minimum(z, 0.0) - jnp.log(1.0 + jnp.exp(-jnp.abs(z)))


def _silu(g):
    return g * (1.0 / (1.0 + jnp.exp(-g)))


def _dot_nt(a, b):
    return lax.dot_general(a, b, (((1,), (1,)), ((), ())), preferred_element_type=F32)


def _dot_tn(a, b):
    return lax.dot_general(a, b, (((0,), (0,)), ((), ())), preferred_element_type=F32)


def _dot(a, b):
    return jnp.dot(a, b, preferred_element_type=F32)


def _pack_bf16_pairs(x):
    bits = pltpu.bitcast(x.astype(BF16).astype(F32), I32)
    w = x.shape[1] // 2
    return bits[:, w:] | lax.shift_right_logical(bits[:, :w], 16)


def _unpack_bf16_pairs(words):
    lo = pltpu.bitcast(lax.shift_left(words, 16), F32)
    hi = pltpu.bitcast(words & jnp.int32(-65536), F32)
    return lo, hi


IP_TM = 1024
IP_TN = 1024


def _in_proj_kernel(x_ref, n1_ref, w_ref, wga_ref, cs_ref, cos_ref, sin_ref, o_ref, ga_ref, xn_ref):
    j = pl.program_id(1)

    @pl.when(j == 0)
    def _():
        x = x_ref[...]
        ms = jnp.mean(x * x, axis=-1, keepdims=True)
        xn = (x * lax.rsqrt(ms + EPS) * n1_ref[...]).astype(BF16)
        xn_ref[...] = xn
        ga_ref[...] = _dot(xn, wga_ref[...])

    acc = _dot(xn_ref[...], w_ref[...]) * cs_ref[...]
    n_rope_blocks = 2 * RET_WIDTH // IP_TN

    @pl.when(j < n_rope_blocks)
    def _():
        cos = cos_ref[...]
        sin = sin_ref[...]
        for h in range(IP_TN // RET_DK):
            x1 = acc[:, 2 * h * LANE:(2 * h + 1) * LANE]
            x2 = acc[:, (2 * h + 1) * LANE:(2 * h + 2) * LANE]
            o_ref[2 * h] = (x1 * cos - x2 * sin).astype(BF16)
            o_ref[2 * h + 1] = (x1 * sin + x2 * cos).astype(BF16)

    @pl.when(j >= n_rope_blocks)
    def _():
        for c in range(IP_TN // LANE):
            o_ref[c] = acc[:, c * LANE:(c + 1) * LANE].astype(BF16)


def _in_proj(x2d, n1w, w_main, w_ga, colscale, cos, sin, seq_len):
    T = x2d.shape[0]
    tm, tn = IP_TM, IP_TN
    nlb = seq_len // tm
    return pl.pallas_call(
        _in_proj_kernel,
        grid=(T // tm, IN_MAIN // tn),
        in_specs=[
            pl.BlockSpec((tm, D_MODEL), lambda i, j: (i, 0)),
            pl.BlockSpec((1, D_MODEL), lambda i, j: (0, 0)),
            pl.BlockSpec((D_MODEL, tn), lambda i, j: (0, j)),
            pl.BlockSpec((D_MODEL, LANE), lambda i, j: (0, 0)),
            pl.BlockSpec((1, tn), lambda i, j: (0, j)),
            pl.BlockSpec((tm, LANE), lambda i, j: (i % nlb, 0)),
            pl.BlockSpec((tm, LANE), lambda i, j: (i % nlb, 0)),
        ],
        out_specs=[
            pl.BlockSpec((tn // LANE, tm, LANE), lambda i, j: (j, i, 0)),
            pl.BlockSpec((tm, LANE), lambda i, j: (i, 0)),
        ],
        out_shape=[
            jax.ShapeDtypeStruct((IN_MAIN // LANE, T, LANE), BF16),
            jax.ShapeDtypeStruct((T, LANE), F32),
        ],
        scratch_shapes=[pltpu.VMEM((tm, D_MODEL), BF16)],
        compiler_params=_cparams(("parallel", "arbitrary"), 48),
        name="in_proj",
    )(x2d, n1w, w_main, w_ga, colscale, cos, sin)


GATE_TM = 512


def _gates_kernel(ga_ref, up_ref, bias_ref, lf_ref, lb_ref, bf_ref, bb_ref):
    z = _dot(ga_ref[...].astype(BF16), up_ref[...]) + bias_ref[...]
    la = _log_sigmoid(z) * (LOG2_E / GLA_TAU)
    hi = la.astype(BF16)
    lo = (la - hi.astype(F32)).astype(BF16)
    kw = GLA_KEY_WIDTH
    bf_ref[...] = _dot(lf_ref[...], hi[:, :kw]) + _dot(lf_ref[...], lo[:, :kw])
    bb_ref[...] = _dot(lb_ref[...], hi[:, kw:]) + _dot(lb_ref[...], lo[:, kw:])


def _gla_gates(ga, up_pad, bias, lf, lb):
    T = ga.shape[0]
    tm = GATE_TM
    kw = GLA_KEY_WIDTH
    return pl.pallas_call(
        _gates_kernel,
        grid=(T // tm,),
        in_specs=[
            pl.BlockSpec((tm, LANE), lambda i: (i, 0)),
            pl.BlockSpec((LANE, 2 * kw), lambda i: (0, 0)),
            pl.BlockSpec((1, 2 * kw), lambda i: (0, 0)),
            pl.BlockSpec((tm, tm), lambda i: (0, 0)),
            pl.BlockSpec((tm, tm), lambda i: (0, 0)),
        ],
        out_specs=[pl.BlockSpec((tm, kw), lambda i: (i, 0)), pl.BlockSpec((tm, kw), lambda i: (i, 0))],
        out_shape=[jax.ShapeDtypeStruct((T, kw), F32), jax.ShapeDtypeStruct((T, kw), F32)],
        compiler_params=_cparams(("parallel",), 32),
        name="gla_gates",
    )(ga, up_pad, bias, lf, lb)


def _wide(ref, rows):
    return jnp.concatenate([ref[0, rows, :], ref[1, rows, :]], axis=1)


def _finish_heads(tot, gn, gate):
    ms = jnp.mean(tot * tot, axis=-1, keepdims=True)
    yn = tot * lax.rsqrt(ms + EPS) * gn
    return (yn * _silu(gate.astype(F32))).astype(BF16)


RET_TB = 1024


def _ret_kernel(dl_ref, q_ref, k_ref, v_ref, *rest, reverse):
    if reverse:
        g_ref, of_ref, gn_ref, o_ref, s_ref, intra_ref, qd_ref, kd_ref, cd_ref, p_ref, u_ref = rest
    else:
        o_ref, s_ref, intra_ref, qd_ref, kd_ref, cd_ref, p_ref, u_ref = rest
    h = pl.program_id(1)
    n = pl.program_id(2)
    C = RET_CHUNK

    @pl.when(n == 0)
    def _():
        s_ref[...] = jnp.zeros_like(s_ref)
        logit = dl_ref[1 if reverse else 0, h]
        lg = _log_sigmoid(jnp.full((C, RET_DV), logit, F32))
        lg_c = _log_sigmoid(jnp.full((C, C), logit, F32))
        lg_r = _log_sigmoid(jnp.full((1, RET_DV), logit, F32))
        ri = lax.broadcasted_iota(I32, (C, RET_DV), 0).astype(F32)
        rc = lax.broadcasted_iota(I32, (C, C), 0).astype(F32)
        cc = lax.broadcasted_iota(I32, (C, C), 1).astype(F32)
        diff = (cc - rc) if reverse else (rc - cc)
        intra_ref[...] = jnp.where(diff >= 0, jnp.exp(lg_c * diff), 0.0)
        if reverse:
            qd_ref[...] = jnp.exp(lg * (C - ri))
            kd_ref[...] = jnp.exp(lg * ri)
        else:
            qd_ref[...] = jnp.exp(lg * (ri + 1.0))
            kd_ref[...] = jnp.exp(lg * (C - 1.0 - ri))
        cd_ref[...] = jnp.exp(lg_r * C)

    nchunks = o_ref.shape[0] // C
    for c in range(nchunks):
        rows = slice(c * C, (c + 1) * C)
        k = _wide(k_ref, rows)
        p_ref[c] = (_dot_nt(_wide(q_ref, rows), k) * intra_ref[...]).astype(BF16)
        kd = (k.astype(F32) * kd_ref[...]).astype(BF16)
        u_ref[c] = _dot_tn(kd, _wide(v_ref, rows))
    order = range(nchunks - 1, -1, -1) if reverse else range(nchunks)
    for c in order:
        rows = slice(c * C, (c + 1) * C)
        state = s_ref[...]
        o = _dot(p_ref[c], _wide(v_ref, rows)) + _dot(_wide(q_ref, rows), state.astype(BF16)) * qd_ref[...]
        s_ref[...] = state * cd_ref[...] + u_ref[c]
        if reverse:
            tot = of_ref[rows, :].astype(F32) + o
            o_ref[rows, :] = _finish_heads(tot, gn_ref[...], _wide(g_ref, rows))
        else:
            o_ref[rows, :] = o.astype(BF16)


def _ret_parts(proj, decay_logit, rb, reverse, o_fwd=None, gn_w=None):
    T = proj.shape[1]
    tb = RET_TB
    dk, dv, C = RET_DK, RET_DV, RET_CHUNK

    def head(base):
        return pl.BlockSpec((dk // LANE, tb, LANE), lambda b, h, n: (base // dk + h, rb(b, n), 0))

    in_specs = [pl.BlockSpec(memory_space=pltpu.SMEM), head(_RQ), head(_RK), head(_RV)]
    args = [decay_logit, proj, proj, proj]
    if reverse:
        in_specs += [
            head(_RG),
            pl.BlockSpec((tb, dv), lambda b, h, n: (rb(b, n), h)),
            pl.BlockSpec((1, dv), lambda b, h, n: (0, h)),
        ]
        args += [proj, o_fwd, gn_w]
    out_spec = pl.BlockSpec((tb, dv), lambda b, h, n: (rb(b, n), h))
    out_shape = jax.ShapeDtypeStruct((T, RET_WIDTH), BF16)
    scratch = [
        pltpu.VMEM((dk, dv), F32),
        pltpu.VMEM((C, C), F32),
        pltpu.VMEM((C, dv), F32),
        pltpu.VMEM((C, dk), F32),
        pltpu.VMEM((1, dv), F32),
        pltpu.VMEM((tb // C, C, C), BF16),
        pltpu.VMEM((tb // C, dk, dv), F32),
    ]
    return in_specs, args, out_spec, out_shape, scratch


GLA_TB = 1024
GLA_UNROLL = 16


GLA_LEVELS = (32, 16, 8, 4, 2, 1)
SUBLANES = 8


def _gla_tables(reverse):
    C = GLA_CHUNK
    r = np.arange(C)
    masks = np.zeros((len(GLA_LEVELS) + 1, C, C), np.float32)
    for l, s in enumerate(GLA_LEVELS):
        upper = (r & s) != 0
        same = (r[:, None] // (2 * s)) == (r[None, :] // (2 * s))
        lhs_rows = ~upper if reverse else upper
        masks[l] = same & lhs_rows[:, None] & ~lhs_rows[None, :]
    masks[-1] = np.eye(C)
    return jnp.asarray(masks, F32)


def _gla_kernel(q_ref, k_ref, v_ref, b_ref, mask_ref, *rest, reverse):
    if reverse:
        g_ref, of_ref, gn_ref, o_ref, st_ref, sc_ref = rest
    else:
        o_ref, st_ref, sc_ref = rest
    n = pl.program_id(2)
    C = GLA_CHUNK

    @pl.when(n == 0)
    def _():
        st_ref[...] = jnp.zeros_like(st_ref)

    nchunks = q_ref.shape[0] // C
    sub_row = lax.broadcasted_iota(I32, (SUBLANES, GLA_DK), 0)
    zero_rows = jnp.zeros((SUBLANES, GLA_DK), F32)

    def chunk_scores(c, carry):
        c0 = pl.multiple_of(c * C, C)
        qb = q_ref[pl.ds(c0, C), :]
        kb = k_ref[pl.ds(c0, C), :]
        q = qb.astype(F32)
        k = kb.astype(F32)
        b = b_ref[pl.ds(c0, C), :]

        def mid_row(r):
            return jnp.broadcast_to(b[r:r + 1, :], (SUBLANES, GLA_DK))

        scores = mask_ref[len(GLA_LEVELS)] * _dot_nt(qb, kb)
        for l, s in enumerate(GLA_LEVELS):
            lhs, rhs = [], []
            for g in range(C // SUBLANES):
                r0 = g * SUBLANES
                rows = slice(r0, r0 + SUBLANES)
                if s >= SUBLANES:
                    m = mid_row((r0 // (2 * s)) * (2 * s) + s)
                    is_lhs = ((r0 & s) != 0) != reverse
                    if is_lhs:
                        lhs.append(q[rows] * jnp.exp2(b[rows] - m))
                        rhs.append(zero_rows)
                    else:
                        lhs.append(zero_rows)
                        rhs.append(k[rows] * jnp.exp2(m - b[rows]))
                else:
                    m = mid_row(r0 + SUBLANES - s)
                    for blk in range(SUBLANES // (2 * s) - 2, -1, -1):
                        m = jnp.where(sub_row < (blk + 1) * 2 * s, mid_row(r0 + blk * 2 * s + s), m)
                    upper = (sub_row & s) != 0
                    is_lhs = jnp.logical_not(upper) if reverse else upper
                    lhs.append(jnp.where(is_lhs, q[rows] * jnp.exp2(b[rows] - m), 0.0))
                    rhs.append(jnp.where(is_lhs, 0.0, k[rows] * jnp.exp2(m - b[rows])))
            lhs = jnp.concatenate(lhs, axis=0).astype(BF16)
            rhs = jnp.concatenate(rhs, axis=0).astype(BF16)
            scores = scores + mask_ref[l] * _dot_nt(lhs, rhs)
        sc_ref[c] = scores.astype(BF16)
        return carry

    lax.fori_loop(0, nchunks, chunk_scores, 0, unroll=GLA_UNROLL)

    def chunk(ci, carry):
        c = (nchunks - 1 - ci) if reverse else ci
        c0 = pl.multiple_of(c * C, C)
        q = q_ref[pl.ds(c0, C), :].astype(F32)
        k = k_ref[pl.ds(c0, C), :].astype(F32)
        v = _wide(v_ref, pl.ds(c0, C))
        b = b_ref[pl.ds(c0, C), :]
        b_end = b[0:1, :] if reverse else b[C - 1:C, :]

        st = st_ref[...]
        o = _dot_nt((q * jnp.exp2(b)).astype(BF16), st.astype(BF16))
        ke = (k * jnp.exp2(b_end - b)).astype(BF16)
        st_ref[...] = st * jnp.exp2(b_end) + _dot_tn(v, ke)
        o = o + _dot(sc_ref[c], v)
        if reverse:
            tot = of_ref[pl.ds(c0, C), :].astype(F32) + o
            o_ref[pl.ds(c0, C), :] = _finish_heads(tot, gn_ref[...], _wide(g_ref, pl.ds(c0, C)))
        else:
            o_ref[pl.ds(c0, C), :] = o.astype(BF16)
        return carry

    lax.fori_loop(0, nchunks, chunk, 0, unroll=GLA_UNROLL)


def _gla_parts(proj, bcum, rb, reverse, o_fwd=None, gn_w=None):
    T = proj.shape[1]
    tb = GLA_TB
    dk, dv = GLA_DK, GLA_DV
    masks = _gla_tables(reverse)

    def key_block(base):
        return pl.BlockSpec((None, tb, LANE), lambda b, h, n: (base // dk + h, rb(b, n), 0))

    def value_block(base):
        return pl.BlockSpec((dv // LANE, tb, LANE), lambda b, h, n: (base // dv + h, rb(b, n), 0))

    in_specs = [
        key_block(_GQ), key_block(_GK), value_block(_GV),
        pl.BlockSpec((tb, dk), lambda b, h, n: (rb(b, n), h)),
        pl.BlockSpec(masks.shape, lambda b, h, n: (0, 0, 0)),
    ]
    args = [proj, proj, proj, bcum, masks]
    if reverse:
        in_specs += [
            value_block(_GG),
            pl.BlockSpec((tb, dv), lambda b, h, n: (rb(b, n), h)),
            pl.BlockSpec((1, dv), lambda b, h, n: (0, h)),
        ]
        args += [proj, o_fwd, gn_w]
    out_spec = pl.BlockSpec((tb, dv), lambda b, h, n: (rb(b, n), h))
    out_shape = jax.ShapeDtypeStruct((T, GLA_WIDTH), BF16)
    scratch = [pltpu.VMEM((dv, dk), F32), pltpu.VMEM((tb // GLA_CHUNK, GLA_CHUNK, GLA_CHUNK), BF16)]
    return in_specs, args, out_spec, out_shape, scratch


def _mixer_kernel(*refs, reverse, n_ret_in, n_gla_in, n_ret_scratch):
    ret_in = refs[:n_ret_in]
    gla_in = refs[n_ret_in:n_ret_in + n_gla_in]
    ret_out, gla_out = refs[n_ret_in + n_gla_in:n_ret_in + n_gla_in + 2]
    scratch = refs[n_ret_in + n_gla_in + 2:]
    _ret_kernel(*ret_in, ret_out, *scratch[:n_ret_scratch], reverse=reverse)
    _gla_kernel(*gla_in, gla_out, *scratch[n_ret_scratch:], reverse=reverse)


def _mixer_scan(proj, decay_logit, bcum, batch, seq_len, reverse, o_fwd=(None, None), gn_w=(None, None)):
    assert RET_TB == GLA_TB and RET_HEADS == GLA_HEADS
    nb = seq_len // RET_TB

    def rb(b, n):
        return b * nb + ((nb - 1 - n) if reverse else n)

    r_specs, r_args, r_out, r_shape, r_scratch = _ret_parts(proj, decay_logit, rb, reverse, o_fwd[0], gn_w[0])
    g_specs, g_args, g_out, g_shape, g_scratch = _gla_parts(proj, bcum, rb, reverse, o_fwd[1], gn_w[1])
    return pl.pallas_call(
        functools.partial(_mixer_kernel, reverse=reverse, n_ret_in=len(r_specs), n_gla_in=len(g_specs),
                          n_ret_scratch=len(r_scratch)),
        grid=(batch, RET_HEADS, nb),
        in_specs=r_specs + g_specs,
        out_specs=[r_out, g_out],
        out_shape=[r_shape, g_shape],
        scratch_shapes=r_scratch + g_scratch,
        compiler_params=_cparams(("parallel", "parallel", "arbitrary"), 48),
        name="mixer_bwd" if reverse else "mixer_fwd",
    )(*r_args, *g_args)


OP_TM = 512


def _out_proj_kernel(mr_ref, mg_ref, w0_ref, w1_ref, x_ref, n2_ref, rh_ref, rl_ref, h_ref, xn_ref, aff_ref,
                     hs_ref):
    s = pl.program_id(0)
    slot = s % 2

    @pl.when(s == 0)
    def _():
        hs_ref[1] = jnp.zeros(hs_ref.shape[1:], F32)

    hp = hs_ref[1 - slot]
    ms = jnp.mean(hp * hp, axis=-1, keepdims=True)
    xn = hp * lax.rsqrt(ms + EPS) * n2_ref[...]
    xh = xn.astype(BF16)
    xn_ref[...] = _pack_bf16_pairs(xn)
    xl = (xn - xh.astype(F32)).astype(BF16)
    lt = _dot_nt(rh_ref[...], xh) + _dot_nt(rh_ref[...], xl) + _dot_nt(rl_ref[...], xh)
    m = jnp.max(lt, axis=0, keepdims=True)
    e = jnp.exp(lt - m)
    aff_ref[...] = e / jnp.sum(e, axis=0, keepdims=True)

    h = x_ref[...] + _dot(mr_ref[...], w0_ref[...]) + _dot(mg_ref[...], w1_ref[...])
    h_ref[...] = h
    hs_ref[slot] = h


def _out_proj(mix_r, mix_g, w_out, x2d, n2w, r_hi, r_lo):
    T = x2d.shape[0]
    tm = OP_TM
    half = RET_WIDTH
    nblk = T // tm

    def head(s):
        return jnp.minimum(s, nblk - 1)

    def tail(s):
        return jnp.maximum(s - 1, 0)

    return pl.pallas_call(
        _out_proj_kernel,
        grid=(nblk + 1,),
        in_specs=[
            pl.BlockSpec((tm, half), lambda s: (head(s), 0)),
            pl.BlockSpec((tm, half), lambda s: (head(s), 0)),
            pl.BlockSpec((half, D_MODEL), lambda s: (0, 0)),
            pl.BlockSpec((half, D_MODEL), lambda s: (1, 0)),
            pl.BlockSpec((tm, D_MODEL), lambda s: (head(s), 0)),
            pl.BlockSpec((1, D_MODEL), lambda s: (0, 0)),
            pl.BlockSpec((N_EXPERTS, D_MODEL), lambda s: (0, 0)),
            pl.BlockSpec((N_EXPERTS, D_MODEL), lambda s: (0, 0)),
        ],
        out_specs=[
            pl.BlockSpec((tm, D_MODEL), lambda s: (head(s), 0)),
            pl.BlockSpec((tm, D_MODEL // 2), lambda s: (tail(s), 0)),
            pl.BlockSpec((N_EXPERTS, tm), lambda s: (0, tail(s))),
        ],
        out_shape=[
            jax.ShapeDtypeStruct((T, D_MODEL), F32),
            jax.ShapeDtypeStruct((T, D_MODEL // 2), I32),
            jax.ShapeDtypeStruct((N_EXPERTS, T), F32),
        ],
        scratch_shapes=[pltpu.VMEM((2, tm, D_MODEL), F32)],
        compiler_params=_cparams(("arbitrary",), 60),
        name="out_proj",
    )(mix_r, mix_g, w_out, w_out, x2d, n2w, r_hi, r_lo)


def _select_kernel(a_ref, pos_ref, posb_ref, rankb_ref, affb_ref, *, cap):
    E, T = a_ref.shape
    tt = MOE_TT

    def count(pred):
        return jnp.sum(pred.astype(F32), axis=1, keepdims=True)

    def bisect(i, tau):
        cand = tau | jnp.left_shift(jnp.int32(1), 30 - i)
        bits = pltpu.bitcast(a_ref[...], I32)
        return jnp.where(count(bits >= cand) >= cap, cand, tau)

    tau = lax.fori_loop(0, 31, bisect, jnp.zeros((E, 1), I32))
    bits_all = pltpu.bitcast(a_ref[...], I32)
    quota = cap - count(bits_all > tau)

    before = (lax.broadcasted_iota(I32, (tt, tt), 0) < lax.broadcasted_iota(I32, (tt, tt), 1)).astype(BF16)
    below = (lax.broadcasted_iota(I32, (E, E), 1) < lax.broadcasted_iota(I32, (E, E), 0)).astype(BF16)

    def block(j, carry):
        c_eq, c_sel = carry
        off = pl.multiple_of(j * tt, tt)
        aff = a_ref[:, pl.ds(off, tt)]
        bits = pltpu.bitcast(aff, I32)
        eq = bits == tau
        eqf = eq.astype(F32)
        rank_eq = _dot(eqf.astype(BF16), before) + c_eq
        sel = (bits > tau) | (eq & (rank_eq < quota))
        self_ = sel.astype(F32)
        selb = self_.astype(BF16)
        slot = _dot(selb, before) + c_sel
        pos = jnp.where(sel, slot, -1.0).astype(I32)
        pos_ref[:, pl.ds(off, tt)] = pos
        per_tok = jnp.broadcast_to(jnp.sum(self_, axis=0, keepdims=True), (E, tt))
        rank = _dot(per_tok.astype(BF16), before) + _dot(below, selb)
        posb_ref[j] = pos
        rankb_ref[j] = jnp.where(sel, rank, -1.0).astype(I32)
        affb_ref[j] = aff
        return (c_eq + jnp.sum(eqf, axis=1, keepdims=True), c_sel + jnp.sum(self_, axis=1, keepdims=True))

    zero = jnp.zeros((E, 1), F32)
    lax.fori_loop(0, T // tt, block, (zero, zero))


def _select(aff, cap):
    E, T = aff.shape
    nb = T // MOE_TT
    blk = jax.ShapeDtypeStruct((nb, E, MOE_TT), I32)
    return pl.pallas_call(
        functools.partial(_select_kernel, cap=cap),
        out_shape=[jax.ShapeDtypeStruct((E, T), I32), blk, blk, jax.ShapeDtypeStruct((nb, E, MOE_TT), F32)],
        compiler_params=pltpu.CompilerParams(vmem_limit_bytes=40 * 1024 * 1024),
        name="select",
    )(aff)


def _regroup_rows(T):
    nb = T // MOE_TT
    rows = CAPACITY_FACTOR * T + SC_GATHER_ROWS * nb
    return -(-rows // MOE_TW) * MOE_TW


def _combine_schedule(posb, T):
    nb = posb.shape[0]
    tw, g = MOE_TW, SC_GATHER_ROWS
    n = jnp.sum((posb >= 0).reshape(nb, -1), axis=1).astype(I32)
    seg = (n + g - 1) // g * g
    hi = jnp.cumsum(seg)
    lo = hi - seg
    off = jnp.concatenate([jnp.zeros((1,), I32), hi])
    nwin_max = (N_EXPERTS * MOE_TT) // tw + 1
    w0 = lo // tw
    w1 = jnp.where(seg > 0, (hi - 1) // tw, w0)
    cand = jnp.arange(nwin_max, dtype=I32)
    win = w0[:, None] + cand[None, :]
    valid = (win <= w1[:, None]).reshape(-1)
    nwin_total = _regroup_rows(T) // tw
    pmax = nb + nwin_total
    jv = jnp.broadcast_to(jnp.arange(nb, dtype=I32)[:, None], win.shape).reshape(-1)
    wv = jnp.minimum(win, nwin_total - 1).reshape(-1)
    total = jnp.sum(valid.astype(I32))
    dst = jnp.where(valid, jnp.cumsum(valid.astype(I32)) - 1, pmax)
    pj, pw = (jnp.zeros((pmax,), I32).at[dst].set(a, mode="drop") for a in (jv, wv))
    real = jnp.arange(pmax, dtype=I32) < total
    pj, pw = (jnp.where(real, a, a[total - 1]) for a in (pj, pw))
    first = jnp.concatenate([jnp.ones((1,), bool), pj[1:] != pj[:-1]])
    last = jnp.concatenate([pj[1:] != pj[:-1], jnp.ones((1,), bool)]) | (jnp.arange(pmax, dtype=I32) == total - 1)
    flag = jnp.where(real, first.astype(I32) + 2 * last.astype(I32) + 4, 0)
    return off, (pj, pw, flag, lo[pj], hi[pj])


SC_LANES = 16
SC_CORES = 2
SC_SUBCORES = 16
SC_GATHER_ROWS = 32


def _dispatch(pos, xw, cap):
    E, T = pos.shape
    W = xw.shape[1]
    G = SC_GATHER_ROWS
    part_rows = cap // SC_CORES
    mesh = plsc.VectorSubcoreMesh(core_axis_name="c", subcore_axis_name="s")

    @pl.kernel(
        out_type=jax.ShapeDtypeStruct((E * cap, W), I32),
        mesh=mesh,
        scratch_types=[pltpu.VMEM((T,), I32), pltpu.VMEM((cap,), I32), pltpu.VMEM((G, W), I32)],
        compiler_params=pltpu.CompilerParams(needs_layout_passes=False),
        name="sc_dispatch",
    )
    def run(pos_hbm, x_hbm, xe_hbm, pos_v, idx_v, buf):
        e = lax.axis_index("s")
        part = lax.axis_index("c")
        pltpu.sync_copy(pos_hbm.at[e], pos_v)
        lane = lax.iota(I32, SC_LANES)

        @pl.loop(0, T // SC_LANES)
        def _(i):
            off = pl.multiple_of(i * SC_LANES, SC_LANES)
            p = pos_v[pl.ds(off, SC_LANES)]
            plsc.store_scatter(idx_v, [p], lane + off, mask=p >= 0)

        @pl.loop(0, part_rows // G)
        def _(g):
            o = pl.multiple_of(part * part_rows + g * G, G)
            pltpu.sync_copy(x_hbm.at[idx_v.at[pl.ds(o, G)]], buf)
            pltpu.sync_copy(buf, xe_hbm.at[pl.ds(e * cap + o, G)])

    return run(pos, xw)


FFN_TM = 2048
FFN_UNPACK_ROWS = 256


def _ffn_tile_width(cap):
    return 512 if min(FFN_TM, cap) <= 1024 else 256


def _ffn_kernel(x_ref, wg_ref, wu_ref, wd_ref, *rest, tw):
    o_ref, xb_ref, hid_ref = rest[-3:]
    s = pl.program_id(2)
    n_up = D_FF // tw

    @pl.when(s == 0)
    def _():
        half = D_MODEL // 2

        def unpack_rows(i, carry):
            r = pl.multiple_of(i * FFN_UNPACK_ROWS, FFN_UNPACK_ROWS)
            lo, hi = _unpack_bf16_pairs(x_ref[pl.ds(r, FFN_UNPACK_ROWS), :])
            xb_ref[pl.ds(r, FFN_UNPACK_ROWS), :half] = lo.astype(BF16)
            xb_ref[pl.ds(r, FFN_UNPACK_ROWS), half:] = hi.astype(BF16)
            return carry

        lax.fori_loop(0, x_ref.shape[0] // FFN_UNPACK_ROWS, unpack_rows, 0)

    @pl.when(s < n_up)
    def _():
        x = xb_ref[...]
        g = _dot(x, wg_ref[...].astype(BF16))
        u = _dot(x, wu_ref[...].astype(BF16))
        col = pl.multiple_of(s * tw, tw)
        hid_ref[:, pl.ds(col, tw)] = (_silu(g) * u).astype(BF16)

    @pl.when(s >= n_up)
    def _():
        o_ref[...] = _pack_bf16_pairs(_dot(hid_ref[...], wd_ref[...].astype(BF16)))


def _ffn(xe, w_gate, w_up, w_down, experts=None, earlier=None):
    E, cap, _ = xe.shape
    e0, e1 = experts or (0, E)
    tm = min(FFN_TM, cap)
    tw = _ffn_tile_width(cap)
    n_up, n_down = D_FF // tw, D_MODEL // tw

    def up(e, m, s):
        return (e + e0, 0, jnp.minimum(s, n_up - 1))

    def down(s):
        return jnp.maximum(s - n_up, 0)

    in_specs = [
        pl.BlockSpec((None, tm, D_MODEL // 2), lambda e, m, s: (e + e0, m, 0)),
        pl.BlockSpec((None, D_MODEL, tw), up),
        pl.BlockSpec((None, D_MODEL, tw), up),
        pl.BlockSpec((None, D_FF, tw), lambda e, m, s: (e + e0, 0, down(s))),
    ]
    args = [xe, w_gate, w_up, w_down]
    aliases = {}
    if earlier is not None:
        in_specs.append(pl.BlockSpec(memory_space=pl.ANY))
        args.append(earlier)
        aliases = {len(args) - 1: 0}
    return pl.pallas_call(
        functools.partial(_ffn_kernel, tw=tw),
        grid=(e1 - e0, cap // tm, n_up + n_down),
        in_specs=in_specs,
        out_specs=pl.BlockSpec((None, tm, tw // 2), lambda e, m, s: (e + e0, m, down(s))),
        out_shape=jax.ShapeDtypeStruct((E, cap, D_MODEL // 2), I32),
        scratch_shapes=[pltpu.VMEM((tm, D_MODEL), BF16), pltpu.VMEM((tm, D_FF), BF16)],
        input_output_aliases=aliases,
        compiler_params=_cparams(("parallel", "parallel", "arbitrary"), 60),
        name="ffn",
    )(*args)


def _regroup(posb, rankb, affb, off, yw, cap, rows):
    NB, EB = posb.shape
    W = yw.shape[1]
    G, L, tt = SC_GATHER_ROWS, SC_LANES, MOE_TT
    n_workers = SC_CORES * SC_SUBCORES
    split = max(1, n_workers // NB)
    per = -(-NB * split // n_workers)
    mesh = plsc.VectorSubcoreMesh(core_axis_name="c", subcore_axis_name="s")
    off_pad = jnp.pad(off, (0, L))

    @pl.kernel(
        out_type=(jax.ShapeDtypeStruct((rows, W), I32), jax.ShapeDtypeStruct((rows,), I32),
                  jax.ShapeDtypeStruct((rows,), F32)),
        mesh=mesh,
        scratch_types=[pltpu.VMEM((EB,), I32), pltpu.VMEM((EB,), I32), pltpu.VMEM((EB,), F32),
                       pltpu.VMEM((EB,), I32), pltpu.VMEM((EB,), I32), pltpu.VMEM((EB,), F32),
                       pltpu.VMEM((G, W), I32), pltpu.VMEM((NB + 1 + L,), I32)],
        compiler_params=pltpu.CompilerParams(needs_layout_passes=False),
        name="sc_regroup",
    )
    def run(posb_hbm, rankb_hbm, affb_hbm, off_hbm, y_hbm, yg_hbm, tok_hbm, gate_hbm,
            pos_v, rank_v, aff_v, src_v, tok_v, gate_v, buf, off_v):
        wid = lax.axis_index("c") * SC_SUBCORES + lax.axis_index("s")
        pltpu.sync_copy(off_hbm, off_v)
        lane = lax.iota(I32, L)
        zi = jnp.zeros((L,), I32)
        zf = jnp.zeros((L,), F32)

        def regroup_block(j, part):
            pltpu.sync_copy(posb_hbm.at[j], pos_v)
            pltpu.sync_copy(rankb_hbm.at[j], rank_v)
            pltpu.sync_copy(affb_hbm.at[j], aff_v)
            lo = jnp.max(plsc.load_gather(off_v, [zi + j]))
            hi = jnp.max(plsc.load_gather(off_v, [zi + j + 1]))

            @pl.loop(0, EB // L)
            def _(i):
                o = pl.multiple_of(i * L, L)
                src_v[pl.ds(o, L)] = zi
                tok_v[pl.ds(o, L)] = zi
                gate_v[pl.ds(o, L)] = zf

            @pl.loop(0, EB // L)
            def _(i):
                o = pl.multiple_of(i * L, L)
                p = pos_v[pl.ds(o, L)]
                r = rank_v[pl.ds(o, L)]
                m = p >= 0
                e = i // (tt // L)
                t0 = j * tt + (i % (tt // L)) * L
                plsc.store_scatter(src_v, [r], p + e * cap, mask=m)
                plsc.store_scatter(tok_v, [r], lane + t0, mask=m)
                plsc.store_scatter(gate_v, [r], aff_v[pl.ds(o, L)], mask=m)

            @pl.loop(part, (hi - lo) // G, step=split)
            def _(g):
                o = pl.multiple_of(g * G, G)
                dst = pl.multiple_of(lo + o, G)
                pltpu.sync_copy(y_hbm.at[src_v.at[pl.ds(o, G)]], buf)
                pltpu.sync_copy(buf, yg_hbm.at[pl.ds(dst, G)])
                pltpu.sync_copy(tok_v.at[pl.ds(o, G)], tok_hbm.at[pl.ds(dst, G)])
                pltpu.sync_copy(gate_v.at[pl.ds(o, G)], gate_hbm.at[pl.ds(dst, G)])

        @pl.loop(0, per)
        def _(k):
            unit = wid * per + k

            @pl.when(unit < NB * split)
            def _():
                regroup_block(unit // split, unit % split)

    return run(posb, rankb, affb, off_pad, yw)


def _combine_kernel(pj_ref, pw_ref, pf_ref, plo_ref, phi_ref, tok_ref, gate_ref, yg_ref, h_ref, nf_ref,
                    o_ref, acc_ref, *, group):
    p = pl.program_id(0)
    flag = pf_ref[p]
    tw, tt = MOE_TW, MOE_TT
    half = D_MODEL // 2

    hw = group // 2

    def col_blocks():
        for n in range(D_MODEL // group):
            yield slice(n * hw, (n + 1) * hw), slice(n * group, n * group + hw)
            yield slice(half + n * hw, half + (n + 1) * hw), slice(n * group + hw, (n + 1) * group)

    @pl.when((flag & 1) != 0)
    def _():
        for packed, natural in col_blocks():
            acc_ref[:, packed] = h_ref[:, natural]

    @pl.when((flag & 4) != 0)
    def _():
        lo, hi = plo_ref[p], phi_ref[p]
        row0 = pw_ref[p] * tw
        rid = lax.broadcasted_iota(I32, (tw, 1), 0) + row0
        keep = (rid >= lo) & (rid < hi)
        y_lo, y_hi = _unpack_bf16_pairs(yg_ref[...])
        y_lo = jnp.where(keep, y_lo, 0.0).astype(BF16)
        y_hi = jnp.where(keep, y_hi, 0.0).astype(BF16)
        tid = lax.broadcasted_iota(I32, (tt, tw), 0) + pj_ref[p] * tt
        cid = lax.broadcasted_iota(I32, (tt, tw), 1) + row0
        hit = (tok_ref[...] == tid) & (cid >= lo) & (cid < hi)
        weights = jnp.where(hit, gate_ref[...], 0.0).astype(BF16)
        acc_ref[:, :half] += _dot(weights, y_lo)
        acc_ref[:, half:] += _dot(weights, y_hi)

    @pl.when((flag & 2) != 0)
    def _():
        y = acc_ref[...]
        scale = lax.rsqrt(jnp.mean(y * y, axis=-1, keepdims=True) + EPS)
        for packed, natural in col_blocks():
            o_ref[:, natural] = acc_ref[:, packed] * scale * nf_ref[:, natural]


def _combine(lists, tok, gate, yg, h, nfw, group):
    pj, pw, pf, plo, phi = lists
    T = h.shape[0]
    tw, tt = MOE_TW, MOE_TT
    nwin = yg.shape[0] // tw
    grid_spec = pltpu.PrefetchScalarGridSpec(
        num_scalar_prefetch=5,
        grid=(pj.shape[0],),
        in_specs=[
            pl.BlockSpec((None, 1, tw), lambda p, pj, pw, *_: (pw[p], 0, 0)),
            pl.BlockSpec((None, 1, tw), lambda p, pj, pw, *_: (pw[p], 0, 0)),
            pl.BlockSpec((tw, D_MODEL // 2), lambda p, pj, pw, *_: (pw[p], 0)),
            pl.BlockSpec((tt, D_MODEL), lambda p, pj, pw, *_: (pj[p], 0)),
            pl.BlockSpec((1, D_MODEL), lambda p, pj, pw, *_: (0, 0)),
        ],
        out_specs=pl.BlockSpec((tt, D_MODEL), lambda p, pj, pw, *_: (pj[p], 0)),
        scratch_shapes=[pltpu.VMEM((tt, D_MODEL), F32)],
    )
    return pl.pallas_call(
        functools.partial(_combine_kernel, group=group),
        grid_spec=grid_spec,
        out_shape=jax.ShapeDtypeStruct((T, D_MODEL), F32),
        compiler_params=_cparams(("arbitrary",), 48),
        name="combine",
    )(pj, pw, pf, plo, phi, tok.reshape(nwin, 1, tw), gate.reshape(nwin, 1, tw), yg, h, nfw)


def _rope_tables(seq_len):
    d = RET_DK
    inv = ROPE_BASE ** (-jnp.arange(0, d, 2, dtype=F32) / d)
    ang = jnp.arange(seq_len, dtype=F32)[:, None] * inv[None, :]
    return jnp.cos(ang), jnp.sin(ang)


def _chunk_tri(n, chunk, upper):
    r = np.arange(n)
    same = (r[:, None] // chunk) == (r[None, :] // chunk)
    tri = (r[:, None] <= r[None, :]) if upper else (r[:, None] >= r[None, :])
    return jnp.asarray(same & tri, BF16)


def _prep_params(norm1_w, w_in, ret_gn_w, gla_gate_up, gla_gate_bias, gla_gn_w, w_out, norm2_w, router_w,
                 normf_w):
    w = w_in[0]
    w_main = w[:, :IN_MAIN].astype(BF16)
    w_ga = jnp.pad(w[:, IN_MAIN:], ((0, 0), (0, LANE - 2 * GLA_RANK))).astype(BF16)
    cs = np.ones((1, IN_MAIN), np.float32)
    cs[:, _RQ:_RQ + RET_WIDTH] = RET_DK ** -0.5
    cs[:, _GQ:_GQ + GLA_KEY_WIDTH] = GLA_DK ** -0.5
    up = gla_gate_up[0].astype(F32)
    up_pad = jnp.zeros((LANE, 2 * GLA_KEY_WIDTH), F32)
    up_pad = up_pad.at[:GLA_RANK, :GLA_KEY_WIDTH].set(up[0])
    up_pad = up_pad.at[GLA_RANK:2 * GLA_RANK, GLA_KEY_WIDTH:].set(up[1])
    rt = router_w[0].T.astype(F32)
    r_hi = rt.astype(BF16)
    r_lo = (rt - r_hi.astype(F32)).astype(BF16)
    return dict(
        n1w=norm1_w[0].reshape(1, D_MODEL).astype(F32),
        w_main=w_main, w_ga=w_ga, colscale=jnp.asarray(cs),
        up_pad=up_pad.astype(BF16),
        bias=gla_gate_bias[0].reshape(1, 2 * GLA_KEY_WIDTH).astype(F32),
        lf=_chunk_tri(GATE_TM, GLA_CHUNK, upper=False),
        lb=_chunk_tri(GATE_TM, GLA_CHUNK, upper=True),
        ret_gn=ret_gn_w[0].reshape(1, RET_WIDTH).astype(F32),
        gla_gn=gla_gn_w[0].reshape(1, GLA_WIDTH).astype(F32),
        w_out=w_out[0].astype(BF16),
        n2w=norm2_w[0].reshape(1, D_MODEL).astype(F32),
        r_hi=r_hi, r_lo=r_lo,
        nfw=normf_w.reshape(1, D_MODEL).astype(F32),
    )


def _trunk_route(x, pp, decay_logit):
    B, L, _ = x.shape
    T = B * L
    x2d = x.reshape(T, D_MODEL)
    cos, sin = _rope_tables(L)
    proj, ga = _in_proj(x2d, pp["n1w"], pp["w_main"], pp["w_ga"], pp["colscale"], cos, sin, L)
    b_f, b_b = _gla_gates(ga, pp["up_pad"], pp["bias"], pp["lf"], pp["lb"])

    fwd = _mixer_scan(proj, decay_logit, b_f, B, L, reverse=False)
    mix_r, mix_g = _mixer_scan(proj, decay_logit, b_b, B, L, reverse=True, o_fwd=fwd,
                               gn_w=(pp["ret_gn"], pp["gla_gn"]))

    h, xn2, aff = _out_proj(mix_r, mix_g, pp["w_out"], x2d, pp["n2w"], pp["r_hi"], pp["r_lo"])

    cap = CAPACITY_FACTOR * T // N_EXPERTS
    pos, posb, rankb, affb = _select(aff, cap)
    off, c_lists = _combine_schedule(posb, T)
    xe = _dispatch(pos, xn2, cap).reshape(N_EXPERTS, cap, D_MODEL // 2)
    return dict(xe=xe, h=h, posb=posb, rankb=rankb, affb=affb, off=off, c_lists=c_lists)


def _trunk_regroup(route, ye):
    T = route["h"].shape[0]
    cap = CAPACITY_FACTOR * T // N_EXPERTS
    nb = T // MOE_TT
    flat = lambda a: a.reshape(nb, -1)
    yg, tok, gate = _regroup(flat(route["posb"]), flat(route["rankb"]), flat(route["affb"]), route["off"],
                             ye.reshape(N_EXPERTS * cap, D_MODEL // 2), cap, _regroup_rows(T))
    return route["c_lists"], tok, gate, yg, route["h"]


def _trunk_back(front, nfw, shape):
    c_lists, tok, gate, yg, h = front
    cap = CAPACITY_FACTOR * h.shape[0] // N_EXPERTS
    return _combine(c_lists, tok, gate, yg, h, nfw, _ffn_tile_width(cap)).reshape(shape)


def kernel(x_prompt, x_sample, norm1_w, w_in, ret_decay_logit, ret_gn_w, gla_gate_up, gla_gate_bias,
           gla_gn_w, w_out, norm2_w, router_w, w_gate, w_up, w_down, normf_w):
    pp = _prep_params(norm1_w, w_in, ret_gn_w, gla_gate_up, gla_gate_bias, gla_gn_w, w_out, norm2_w,
                      router_w, normf_w)
    decay_logit = ret_decay_logit[0].astype(F32)
    weights = (w_gate[0], w_up[0], w_down[0])
    route_p = _trunk_route(x_prompt, pp, decay_logit)
    route_s = _trunk_route(x_sample, pp, decay_logit)
    half = N_EXPERTS // 2
    ye_p = _ffn(route_p["xe"], *weights, experts=(0, half))
    ye_s = _ffn(route_s["xe"], *weights)
    xe_p, ye_p, ye_s = lax.optimization_barrier((route_p["xe"], ye_p, ye_s))
    ye_p = _ffn(xe_p, *weights, experts=(half, N_EXPERTS), earlier=ye_p)
    front_s = _trunk_regroup(route_s, ye_s)
    front_p = _trunk_regroup(route_p, ye_p)
    y_sample = _trunk_back(front_s, pp["nfw"], x_sample.shape)
    front_p, y_sample = lax.optimization_barrier((front_p, y_sample))
    y_prompt = _trunk_back(front_p, pp["nfw"], x_prompt.shape)
    return (y_prompt, y_sample)
```

```python
import functools

import numpy as np
import jax
import jax.numpy as jnp
from jax import lax
from jax.experimental import pallas as pl
from jax.experimental.pallas import tpu as pltpu
from jax.experimental.pallas import tpu_sc as plsc

F32, BF16, I32 = jnp.float32, jnp.bfloat16, jnp.int32

D_MODEL = 2048
RET_WIDTH = 1024
RET_HEADS = 4
RET_DK = 256
RET_DV = 256
GLA_WIDTH = 1024
GLA_HEADS = 4
GLA_DK = 128
GLA_DV = 256
GLA_KEY_WIDTH = 512
GLA_RANK = 16
GLA_TAU = 16.0
RET_CHUNK = 256
GLA_CHUNK = 64
ROPE_BASE = 10000.0
N_EXPERTS = 16
CAPACITY_FACTOR = 2
D_FF = 2048
EPS = 1e-6
LOG2_E = 1.4426950408889634
IN_MAIN = 4 * RET_WIDTH + 2 * GLA_KEY_WIDTH + 2 * GLA_WIDTH

_RQ, _RK, _RV, _RG = 0, 1024, 2048, 3072
_GQ, _GK, _GV, _GG = 4096, 4608, 5120, 6144

LANE = 128
MOE_TT = 512
MOE_TW = 512


def _cparams(sem, vmem_mb):
    return pltpu.CompilerParams(dimension_semantics=sem, vmem_limit_bytes=vmem_mb * 1024 * 1024)


def _log_sigmoid(z):
    return jnp.minimum(z, 0.0) - jnp.log(1.0 + jnp.exp(-jnp.abs(z)))


def _silu(g):
    return g * (1.0 / (1.0 + jnp.exp(-g)))


def _dot_nt(a, b):
    return lax.dot_general(a, b, (((1,), (1,)), ((), ())), preferred_element_type=F32)


def _dot_tn(a, b):
    return lax.dot_general(a, b, (((0,), (0,)), ((), ())), preferred_element_type=F32)


def _dot(a, b):
    return jnp.dot(a, b, preferred_element_type=F32)


def _pack_bf16_pairs(x):
    bits = pltpu.bitcast(x.astype(BF16).astype(F32), I32)
    w = x.shape[1] // 2
    return bits[:, w:] | lax.shift_right_logical(bits[:, :w], 16)


def _unpack_bf16_pairs(words):
    lo = pltpu.bitcast(lax.shift_left(words, 16), F32)
    hi = pltpu.bitcast(words & jnp.int32(-65536), F32)
    return lo, hi


IP_TM = 1024
IP_TN = 1024


def _in_proj_kernel(x_ref, n1_ref, w_ref, wga_ref, cs_ref, cos_ref, sin_ref, o_ref, ga_ref, xn_ref):
    j = pl.program_id(1)

    @pl.when(j == 0)
    def _():
        x = x_ref[...]
        ms = jnp.mean(x * x, axis=-1, keepdims=True)
        xn = (x * lax.rsqrt(ms + EPS) * n1_ref[...]).astype(BF16)
        xn_ref[...] = xn
        ga_ref[...] = _dot(xn, wga_ref[...])

    acc = _dot(xn_ref[...], w_ref[...]) * cs_ref[...]
    n_rope_blocks = 2 * RET_WIDTH // IP_TN

    @pl.when(j < n_rope_blocks)
    def _():
        cos = cos_ref[...]
        sin = sin_ref[...]
        for h in range(IP_TN // RET_DK):
            x1 = acc[:, 2 * h * LANE:(2 * h + 1) * LANE]
            x2 = acc[:, (2 * h + 1) * LANE:(2 * h + 2) * LANE]
            o_ref[2 * h] = (x1 * cos - x2 * sin).astype(BF16)
            o_ref[2 * h + 1] = (x1 * sin + x2 * cos).astype(BF16)

    @pl.when(j >= n_rope_blocks)
    def _():
        for c in range(IP_TN // LANE):
            o_ref[c] = acc[:, c * LANE:(c + 1) * LANE].astype(BF16)


def _in_proj(x2d, n1w, w_main, w_ga, colscale, cos, sin, seq_len):
    T = x2d.shape[0]
    tm, tn = IP_TM, IP_TN
    nlb = seq_len // tm
    return pl.pallas_call(
        _in_proj_kernel,
        grid=(T // tm, IN_MAIN // tn),
        in_specs=[
            pl.BlockSpec((tm, D_MODEL), lambda i, j: (i, 0)),
            pl.BlockSpec((1, D_MODEL), lambda i, j: (0, 0)),
            pl.BlockSpec((D_MODEL, tn), lambda i, j: (0, j)),
            pl.BlockSpec((D_MODEL, LANE), lambda i, j: (0, 0)),
            pl.BlockSpec((1, tn), lambda i, j: (0, j)),
            pl.BlockSpec((tm, LANE), lambda i, j: (i % nlb, 0)),
            pl.BlockSpec((tm, LANE), lambda i, j: (i % nlb, 0)),
        ],
        out_specs=[
            pl.BlockSpec((tn // LANE, tm, LANE), lambda i, j: (j, i, 0)),
            pl.BlockSpec((tm, LANE), lambda i, j: (i, 0)),
        ],
        out_shape=[
            jax.ShapeDtypeStruct((IN_MAIN // LANE, T, LANE), BF16),
            jax.ShapeDtypeStruct((T, LANE), F32),
        ],
        scratch_shapes=[pltpu.VMEM((tm, D_MODEL), BF16)],
        compiler_params=_cparams(("parallel", "arbitrary"), 48),
        name="in_proj",
    )(x2d, n1w, w_main, w_ga, colscale, cos, sin)


GATE_TM = 512


def _gates_kernel(ga_ref, up_ref, bias_ref, lf_ref, lb_ref, bf_ref, bb_ref):
    z = _dot(ga_ref[...].astype(BF16), up_ref[...]) + bias_ref[...]
    la = _log_sigmoid(z) * (LOG2_E / GLA_TAU)
    hi = la.astype(BF16)
    lo = (la - hi.astype(F32)).astype(BF16)
    kw = GLA_KEY_WIDTH
    bf_ref[...] = _dot(lf_ref[...], hi[:, :kw]) + _dot(lf_ref[...], lo[:, :kw])
    bb_ref[...] = _dot(lb_ref[...], hi[:, kw:]) + _dot(lb_ref[...], lo[:, kw:])


def _gla_gates(ga, up_pad, bias, lf, lb):
    T = ga.shape[0]
    tm = GATE_TM
    kw = GLA_KEY_WIDTH
    return pl.pallas_call(
        _gates_kernel,
        grid=(T // tm,),
        in_specs=[
            pl.BlockSpec((tm, LANE), lambda i: (i, 0)),
            pl.BlockSpec((LANE, 2 * kw), lambda i: (0, 0)),
            pl.BlockSpec((1, 2 * kw), lambda i: (0, 0)),
            pl.BlockSpec((tm, tm), lambda i: (0, 0)),
            pl.BlockSpec((tm, tm), lambda i: (0, 0)),
        ],
        out_specs=[pl.BlockSpec((tm, kw), lambda i: (i, 0)), pl.BlockSpec((tm, kw), lambda i: (i, 0))],
        out_shape=[jax.ShapeDtypeStruct((T, kw), F32), jax.ShapeDtypeStruct((T, kw), F32)],
        compiler_params=_cparams(("parallel",), 32),
        name="gla_gates",
    )(ga, up_pad, bias, lf, lb)


def _wide(ref, rows):
    return jnp.concatenate([ref[0, rows, :], ref[1, rows, :]], axis=1)


def _finish_heads(tot, gn, gate):
    ms = jnp.mean(tot * tot, axis=-1, keepdims=True)
    yn = tot * lax.rsqrt(ms + EPS) * gn
    return (yn * _silu(gate.astype(F32))).astype(BF16)


RET_TB = 1024


def _ret_kernel(dl_ref, q_ref, k_ref, v_ref, *rest, reverse):
    if reverse:
        g_ref, of_ref, gn_ref, o_ref, s_ref, intra_ref, qd_ref, kd_ref, cd_ref, p_ref, u_ref = rest
    else:
        o_ref, s_ref, intra_ref, qd_ref, kd_ref, cd_ref, p_ref, u_ref = rest
    h = pl.program_id(1)
    n = pl.program_id(2)
    C = RET_CHUNK

    @pl.when(n == 0)
    def _():
        s_ref[...] = jnp.zeros_like(s_ref)
        logit = dl_ref[1 if reverse else 0, h]
        lg = _log_sigmoid(jnp.full((C, RET_DV), logit, F32))
        lg_c = _log_sigmoid(jnp.full((C, C), logit, F32))
        lg_r = _log_sigmoid(jnp.full((1, RET_DV), logit, F32))
        ri = lax.broadcasted_iota(I32, (C, RET_DV), 0).astype(F32)
        rc = lax.broadcasted_iota(I32, (C, C), 0).astype(F32)
        cc = lax.broadcasted_iota(I32, (C, C), 1).astype(F32)
        diff = (cc - rc) if reverse else (rc - cc)
        intra_ref[...] = jnp.where(diff >= 0, jnp.exp(lg_c * diff), 0.0)
        if reverse:
            qd_ref[...] = jnp.exp(lg * (C - ri))
            kd_ref[...] = jnp.exp(lg * ri)
        else:
            qd_ref[...] = jnp.exp(lg * (ri + 1.0))
            kd_ref[...] = jnp.exp(lg * (C - 1.0 - ri))
        cd_ref[...] = jnp.exp(lg_r * C)

    nchunks = o_ref.shape[0] // C
    for c in range(nchunks):
        rows = slice(c * C, (c + 1) * C)
        k = _wide(k_ref, rows)
        p_ref[c] = (_dot_nt(_wide(q_ref, rows), k) * intra_ref[...]).astype(BF16)
        kd = (k.astype(F32) * kd_ref[...]).astype(BF16)
        u_ref[c] = _dot_tn(kd, _wide(v_ref, rows))
    order = range(nchunks - 1, -1, -1) if reverse else range(nchunks)
    for c in order:
        rows = slice(c * C, (c + 1) * C)
        state = s_ref[...]
        o = _dot(p_ref[c], _wide(v_ref, rows)) + _dot(_wide(q_ref, rows), state.astype(BF16)) * qd_ref[...]
        s_ref[...] = state * cd_ref[...] + u_ref[c]
        if reverse:
            tot = of_ref[rows, :].astype(F32) + o
            o_ref[rows, :] = _finish_heads(tot, gn_ref[...], _wide(g_ref, rows))
        else:
            o_ref[rows, :] = o.astype(BF16)


def _ret_parts(proj, decay_logit, rb, reverse, o_fwd=None, gn_w=None):
    T = proj.shape[1]
    tb = RET_TB
    dk, dv, C = RET_DK, RET_DV, RET_CHUNK

    def head(base):
        return pl.BlockSpec((dk // LANE, tb, LANE), lambda b, h, n: (base // dk + h, rb(b, n), 0))

    in_specs = [pl.BlockSpec(memory_space=pltpu.SMEM), head(_RQ), head(_RK), head(_RV)]
    args = [decay_logit, proj, proj, proj]
    if reverse:
        in_specs += [
            head(_RG),
            pl.BlockSpec((tb, dv), lambda b, h, n: (rb(b, n), h)),
            pl.BlockSpec((1, dv), lambda b, h, n: (0, h)),
        ]
        args += [proj, o_fwd, gn_w]
    out_spec = pl.BlockSpec((tb, dv), lambda b, h, n: (rb(b, n), h))
    out_shape = jax.ShapeDtypeStruct((T, RET_WIDTH), BF16)
    scratch = [
        pltpu.VMEM((dk, dv), F32),
        pltpu.VMEM((C, C), F32),
        pltpu.VMEM((C, dv), F32),
        pltpu.VMEM((C, dk), F32),
        pltpu.VMEM((1, dv), F32),
        pltpu.VMEM((tb // C, C, C), BF16),
        pltpu.VMEM((tb // C, dk, dv), F32),
    ]
    return in_specs, args, out_spec, out_shape, scratch


GLA_TB = 1024
GLA_UNROLL = 16


GLA_LEVELS = (32, 16, 8, 4, 2, 1)
SUBLANES = 8


def _gla_tables(reverse):
    C = GLA_CHUNK
    r = np.arange(C)
    masks = np.zeros((len(GLA_LEVELS) + 1, C, C), np.float32)
    for l, s in enumerate(GLA_LEVELS):
        upper = (r & s) != 0
        same = (r[:, None] // (2 * s)) == (r[None, :] // (2 * s))
        lhs_rows = ~upper if reverse else upper
        masks[l] = same & lhs_rows[:, None] & ~lhs_rows[None, :]
    masks[-1] = np.eye(C)
    return jnp.asarray(masks, F32)


def _gla_kernel(q_ref, k_ref, v_ref, b_ref, mask_ref, *rest, reverse):
    if reverse:
        g_ref, of_ref, gn_ref, o_ref, st_ref, sc_ref = rest
    else:
        o_ref, st_ref, sc_ref = rest
    n = pl.program_id(2)
    C = GLA_CHUNK

    @pl.when(n == 0)
    def _():
        st_ref[...] = jnp.zeros_like(st_ref)

    nchunks = q_ref.shape[0] // C
    sub_row = lax.broadcasted_iota(I32, (SUBLANES, GLA_DK), 0)
    zero_rows = jnp.zeros((SUBLANES, GLA_DK), F32)

    def chunk_scores(c, carry):
        c0 = pl.multiple_of(c * C, C)
        qb = q_ref[pl.ds(c0, C), :]
        kb = k_ref[pl.ds(c0, C), :]
        q = qb.astype(F32)
        k = kb.astype(F32)
        b = b_ref[pl.ds(c0, C), :]

        def mid_row(r):
            return jnp.broadcast_to(b[r:r + 1, :], (SUBLANES, GLA_DK))

        scores = mask_ref[len(GLA_LEVELS)] * _dot_nt(qb, kb)
        for l, s in enumerate(GLA_LEVELS):
            lhs, rhs = [], []
            for g in range(C // SUBLANES):
                r0 = g * SUBLANES
                rows = slice(r0, r0 + SUBLANES)
                if s >= SUBLANES:
                    m = mid_row((r0 // (2 * s)) * (2 * s) + s)
                    is_lhs = ((r0 & s) != 0) != reverse
                    if is_lhs:
                        lhs.append(q[rows] * jnp.exp2(b[rows] - m))
                        rhs.append(zero_rows)
                    else:
                        lhs.append(zero_rows)
                        rhs.append(k[rows] * jnp.exp2(m - b[rows]))
                else:
                    m = mid_row(r0 + SUBLANES - s)
                    for blk in range(SUBLANES // (2 * s) - 2, -1, -1):
                        m = jnp.where(sub_row < (blk + 1) * 2 * s, mid_row(r0 + blk * 2 * s + s), m)
                    upper = (sub_row & s) != 0
                    is_lhs = jnp.logical_not(upper) if reverse else upper
                    lhs.append(jnp.where(is_lhs, q[rows] * jnp.exp2(b[rows] - m), 0.0))
                    rhs.append(jnp.where(is_lhs, 0.0, k[rows] * jnp.exp2(m - b[rows])))
            lhs = jnp.concatenate(lhs, axis=0).astype(BF16)
            rhs = jnp.concatenate(rhs, axis=0).astype(BF16)
            scores = scores + mask_ref[l] * _dot_nt(lhs, rhs)
        sc_ref[c] = scores.astype(BF16)
        return carry

    lax.fori_loop(0, nchunks, chunk_scores, 0, unroll=GLA_UNROLL)

    def chunk(ci, carry):
        c = (nchunks - 1 - ci) if reverse else ci
        c0 = pl.multiple_of(c * C, C)
        q = q_ref[pl.ds(c0, C), :].astype(F32)
        k = k_ref[pl.ds(c0, C), :].astype(F32)
        v = _wide(v_ref, pl.ds(c0, C))
        b = b_ref[pl.ds(c0, C), :]
        b_end = b[0:1, :] if reverse else b[C - 1:C, :]

        st = st_ref[...]
        o = _dot_nt((q * jnp.exp2(b)).astype(BF16), st.astype(BF16))
        ke = (k * jnp.exp2(b_end - b)).astype(BF16)
        st_ref[...] = st * jnp.exp2(b_end) + _dot_tn(v, ke)
        o = o + _dot(sc_ref[c], v)
        if reverse:
            tot = of_ref[pl.ds(c0, C), :].astype(F32) + o
            o_ref[pl.ds(c0, C), :] = _finish_heads(tot, gn_ref[...], _wide(g_ref, pl.ds(c0, C)))
        else:
            o_ref[pl.ds(c0, C), :] = o.astype(BF16)
        return carry

    lax.fori_loop(0, nchunks, chunk, 0, unroll=GLA_UNROLL)


def _gla_parts(proj, bcum, rb, reverse, o_fwd=None, gn_w=None):
    T = proj.shape[1]
    tb = GLA_TB
    dk, dv = GLA_DK, GLA_DV
    masks = _gla_tables(reverse)

    def key_block(base):
        return pl.BlockSpec((None, tb, LANE), lambda b, h, n: (base // dk + h, rb(b, n), 0))

    def value_block(base):
        return pl.BlockSpec((dv // LANE, tb, LANE), lambda b, h, n: (base // dv + h, rb(b, n), 0))

    in_specs = [
        key_block(_GQ), key_block(_GK), value_block(_GV),
        pl.BlockSpec((tb, dk), lambda b, h, n: (rb(b, n), h)),
        pl.BlockSpec(masks.shape, lambda b, h, n: (0, 0, 0)),
    ]
    args = [proj, proj, proj, bcum, masks]
    if reverse:
        in_specs += [
            value_block(_GG),
            pl.BlockSpec((tb, dv), lambda b, h, n: (rb(b, n), h)),
            pl.BlockSpec((1, dv), lambda b, h, n: (0, h)),
        ]
        args += [proj, o_fwd, gn_w]
    out_spec = pl.BlockSpec((tb, dv), lambda b, h, n: (rb(b, n), h))
    out_shape = jax.ShapeDtypeStruct((T, GLA_WIDTH), BF16)
    scratch = [pltpu.VMEM((dv, dk), F32), pltpu.VMEM((tb // GLA_CHUNK, GLA_CHUNK, GLA_CHUNK), BF16)]
    return in_specs, args, out_spec, out_shape, scratch


def _mixer_kernel(*refs, reverse, n_ret_in, n_gla_in, n_ret_scratch):
    ret_in = refs[:n_ret_in]
    gla_in = refs[n_ret_in:n_ret_in + n_gla_in]
    ret_out, gla_out = refs[n_ret_in + n_gla_in:n_ret_in + n_gla_in + 2]
    scratch = refs[n_ret_in + n_gla_in + 2:]
    _ret_kernel(*ret_in, ret_out, *scratch[:n_ret_scratch], reverse=reverse)
    _gla_kernel(*gla_in, gla_out, *scratch[n_ret_scratch:], reverse=reverse)


def _mixer_scan(proj, decay_logit, bcum, batch, seq_len, reverse, o_fwd=(None, None), gn_w=(None, None)):
    assert RET_TB == GLA_TB and RET_HEADS == GLA_HEADS
    nb = seq_len // RET_TB

    def rb(b, n):
        return b * nb + ((nb - 1 - n) if reverse else n)

    r_specs, r_args, r_out, r_shape, r_scratch = _ret_parts(proj, decay_logit, rb, reverse, o_fwd[0], gn_w[0])
    g_specs, g_args, g_out, g_shape, g_scratch = _gla_parts(proj, bcum, rb, reverse, o_fwd[1], gn_w[1])
    return pl.pallas_call(
        functools.partial(_mixer_kernel, reverse=reverse, n_ret_in=len(r_specs), n_gla_in=len(g_specs),
                          n_ret_scratch=len(r_scratch)),
        grid=(batch, RET_HEADS, nb),
        in_specs=r_specs + g_specs,
        out_specs=[r_out, g_out],
        out_shape=[r_shape, g_shape],
        scratch_shapes=r_scratch + g_scratch,
        compiler_params=_cparams(("parallel", "parallel", "arbitrary"), 48),
        name="mixer_bwd" if reverse else "mixer_fwd",
    )(*r_args, *g_args)


OP_TM = 512


def _out_proj_kernel(mr_ref, mg_ref, w0_ref, w1_ref, x_ref, n2_ref, rh_ref, rl_ref, h_ref, xn_ref, aff_ref,
                     hs_ref):
    s = pl.program_id(0)
    slot = s % 2

    @pl.when(s == 0)
    def _():
        hs_ref[1] = jnp.zeros(hs_ref.shape[1:], F32)

    hp = hs_ref[1 - slot]
    ms = jnp.mean(hp * hp, axis=-1, keepdims=True)
    xn = hp * lax.rsqrt(ms + EPS) * n2_ref[...]
    xh = xn.astype(BF16)
    xn_ref[...] = _pack_bf16_pairs(xn)
    xl = (xn - xh.astype(F32)).astype(BF16)
    lt = _dot_nt(rh_ref[...], xh) + _dot_nt(rh_ref[...], xl) + _dot_nt(rl_ref[...], xh)
    m = jnp.max(lt, axis=0, keepdims=True)
    e = jnp.exp(lt - m)
    aff_ref[...] = e / jnp.sum(e, axis=0, keepdims=True)

    h = x_ref[...] + _dot(mr_ref[...], w0_ref[...]) + _dot(mg_ref[...], w1_ref[...])
    h_ref[...] = h
    hs_ref[slot] = h


def _out_proj(mix_r, mix_g, w_out, x2d, n2w, r_hi, r_lo):
    T = x2d.shape[0]
    tm = OP_TM
    half = RET_WIDTH
    nblk = T // tm

    def head(s):
        return jnp.minimum(s, nblk - 1)

    def tail(s):
        return jnp.maximum(s - 1, 0)

    return pl.pallas_call(
        _out_proj_kernel,
        grid=(nblk + 1,),
        in_specs=[
            pl.BlockSpec((tm, half), lambda s: (head(s), 0)),
            pl.BlockSpec((tm, half), lambda s: (head(s), 0)),
            pl.BlockSpec((half, D_MODEL), lambda s: (0, 0)),
            pl.BlockSpec((half, D_MODEL), lambda s: (1, 0)),
            pl.BlockSpec((tm, D_MODEL), lambda s: (head(s), 0)),
            pl.BlockSpec((1, D_MODEL), lambda s: (0, 0)),
            pl.BlockSpec((N_EXPERTS, D_MODEL), lambda s: (0, 0)),
            pl.BlockSpec((N_EXPERTS, D_MODEL), lambda s: (0, 0)),
        ],
        out_specs=[
            pl.BlockSpec((tm, D_MODEL), lambda s: (head(s), 0)),
            pl.BlockSpec((tm, D_MODEL // 2), lambda s: (tail(s), 0)),
            pl.BlockSpec((N_EXPERTS, tm), lambda s: (0, tail(s))),
        ],
        out_shape=[
            jax.ShapeDtypeStruct((T, D_MODEL), F32),
            jax.ShapeDtypeStruct((T, D_MODEL // 2), I32),
            jax.ShapeDtypeStruct((N_EXPERTS, T), F32),
        ],
        scratch_shapes=[pltpu.VMEM((2, tm, D_MODEL), F32)],
        compiler_params=_cparams(("arbitrary",), 60),
        name="out_proj",
    )(mix_r, mix_g, w_out, w_out, x2d, n2w, r_hi, r_lo)


def _select_kernel(a_ref, pos_ref, posb_ref, rankb_ref, affb_ref, *, cap):
    E, T = a_ref.shape
    tt = MOE_TT

    def count(pred):
        return jnp.sum(pred.astype(F32), axis=1, keepdims=True)

    def bisect(i, tau):
        cand = tau | jnp.left_shift(jnp.int32(1), 30 - i)
        bits = pltpu.bitcast(a_ref[...], I32)
        return jnp.where(count(bits >= cand) >= cap, cand, tau)

    tau = lax.fori_loop(0, 31, bisect, jnp.zeros((E, 1), I32))
    bits_all = pltpu.bitcast(a_ref[...], I32)
    quota = cap - count(bits_all > tau)

    before = (lax.broadcasted_iota(I32, (tt, tt), 0) < lax.broadcasted_iota(I32, (tt, tt), 1)).astype(BF16)
    below = (lax.broadcasted_iota(I32, (E, E), 1) < lax.broadcasted_iota(I32, (E, E), 0)).astype(BF16)

    def block(j, carry):
        c_eq, c_sel = carry
        off = pl.multiple_of(j * tt, tt)
        aff = a_ref[:, pl.ds(off, tt)]
        bits = pltpu.bitcast(aff, I32)
        eq = bits == tau
        eqf = eq.astype(F32)
        rank_eq = _dot(eqf.astype(BF16), before) + c_eq
        sel = (bits > tau) | (eq & (rank_eq < quota))
        self_ = sel.astype(F32)
        selb = self_.astype(BF16)
        slot = _dot(selb, before) + c_sel
        pos = jnp.where(sel, slot, -1.0).astype(I32)
        pos_ref[:, pl.ds(off, tt)] = pos
        per_tok = jnp.broadcast_to(jnp.sum(self_, axis=0, keepdims=True), (E, tt))
        rank = _dot(per_tok.astype(BF16), before) + _dot(below, selb)
        posb_ref[j] = pos
        rankb_ref[j] = jnp.where(sel, rank, -1.0).astype(I32)
        affb_ref[j] = aff
        return (c_eq + jnp.sum(eqf, axis=1, keepdims=True), c_sel + jnp.sum(self_, axis=1, keepdims=True))

    zero = jnp.zeros((E, 1), F32)
    lax.fori_loop(0, T // tt, block, (zero, zero))


def _select(aff, cap):
    E, T = aff.shape
    nb = T // MOE_TT
    blk = jax.ShapeDtypeStruct((nb, E, MOE_TT), I32)
    return pl.pallas_call(
        functools.partial(_select_kernel, cap=cap),
        out_shape=[jax.ShapeDtypeStruct((E, T), I32), blk, blk, jax.ShapeDtypeStruct((nb, E, MOE_TT), F32)],
        compiler_params=pltpu.CompilerParams(vmem_limit_bytes=40 * 1024 * 1024),
        name="select",
    )(aff)


def _regroup_rows(T):
    nb = T // MOE_TT
    rows = CAPACITY_FACTOR * T + SC_GATHER_ROWS * nb
    return -(-rows // MOE_TW) * MOE_TW


def _combine_schedule(posb, T):
    nb = posb.shape[0]
    tw, g = MOE_TW, SC_GATHER_ROWS
    n = jnp.sum((posb >= 0).reshape(nb, -1), axis=1).astype(I32)
    seg = (n + g - 1) // g * g
    hi = jnp.cumsum(seg)
    lo = hi - seg
    off = jnp.concatenate([jnp.zeros((1,), I32), hi])
    nwin_max = (N_EXPERTS * MOE_TT) // tw + 1
    w0 = lo // tw
    w1 = jnp.where(seg > 0, (hi - 1) // tw, w0)
    cand = jnp.arange(nwin_max, dtype=I32)
    win = w0[:, None] + cand[None, :]
    valid = (win <= w1[:, None]).reshape(-1)
    nwin_total = _regroup_rows(T) // tw
    pmax = nb + nwin_total
    jv = jnp.broadcast_to(jnp.arange(nb, dtype=I32)[:, None], win.shape).reshape(-1)
    wv = jnp.minimum(win, nwin_total - 1).reshape(-1)
    total = jnp.sum(valid.astype(I32))
    dst = jnp.where(valid, jnp.cumsum(valid.astype(I32)) - 1, pmax)
    pj, pw = (jnp.zeros((pmax,), I32).at[dst].set(a, mode="drop") for a in (jv, wv))
    real = jnp.arange(pmax, dtype=I32) < total
    pj, pw = (jnp.where(real, a, a[total - 1]) for a in (pj, pw))
    first = jnp.concatenate([jnp.ones((1,), bool), pj[1:] != pj[:-1]])
    last = jnp.concatenate([pj[1:] != pj[:-1], jnp.ones((1,), bool)]) | (jnp.arange(pmax, dtype=I32) == total - 1)
    flag = jnp.where(real, first.astype(I32) + 2 * last.astype(I32) + 4, 0)
    return off, (pj, pw, flag, lo[pj], hi[pj])


SC_LANES = 16
SC_CORES = 2
SC_SUBCORES = 16
SC_GATHER_ROWS = 64


def _dispatch(pos, xw, cap):
    E, T = pos.shape
    W = xw.shape[1]
    G = SC_GATHER_ROWS
    part_rows = cap // SC_CORES
    mesh = plsc.VectorSubcoreMesh(core_axis_name="c", subcore_axis_name="s")

    @pl.kernel(
        out_type=jax.ShapeDtypeStruct((E * cap, W), I32),
        mesh=mesh,
        scratch_types=[pltpu.VMEM((T,), I32), pltpu.VMEM((cap,), I32), pltpu.VMEM((G, W), I32)],
        compiler_params=pltpu.CompilerParams(needs_layout_passes=False),
        name="sc_dispatch",
    )
    def run(pos_hbm, x_hbm, xe_hbm, pos_v, idx_v, buf):
        e = lax.axis_index("s")
        part = lax.axis_index("c")
        pltpu.sync_copy(pos_hbm.at[e], pos_v)
        lane = lax.iota(I32, SC_LANES)

        @pl.loop(0, T // SC_LANES)
        def _(i):
            off = pl.multiple_of(i * SC_LANES, SC_LANES)
            p = pos_v[pl.ds(off, SC_LANES)]
            plsc.store_scatter(idx_v, [p], lane + off, mask=p >= 0)

        @pl.loop(0, part_rows // G)
        def _(g):
            o = pl.multiple_of(part * part_rows + g * G, G)
            pltpu.sync_copy(x_hbm.at[idx_v.at[pl.ds(o, G)]], buf)
            pltpu.sync_copy(buf, xe_hbm.at[pl.ds(e * cap + o, G)])

    return run(pos, xw)


FFN_TM = 2048
FFN_UNPACK_ROWS = 256


def _ffn_tile_width(cap):
    return 512 if min(FFN_TM, cap) <= 1024 else 256


def _ffn_kernel(x_ref, wg_ref, wu_ref, wd_ref, *rest, tw):
    o_ref, xb_ref, hid_ref = rest[-3:]
    s = pl.program_id(2)
    n_up = D_FF // tw

    @pl.when(s == 0)
    def _():
        half = D_MODEL // 2

        def unpack_rows(i, carry):
            r = pl.multiple_of(i * FFN_UNPACK_ROWS, FFN_UNPACK_ROWS)
            lo, hi = _unpack_bf16_pairs(x_ref[pl.ds(r, FFN_UNPACK_ROWS), :])
            xb_ref[pl.ds(r, FFN_UNPACK_ROWS), :half] = lo.astype(BF16)
            xb_ref[pl.ds(r, FFN_UNPACK_ROWS), half:] = hi.astype(BF16)
            return carry

        lax.fori_loop(0, x_ref.shape[0] // FFN_UNPACK_ROWS, unpack_rows, 0)

    @pl.when(s < n_up)
    def _():
        x = xb_ref[...]
        g = _dot(x, wg_ref[...].astype(BF16))
        u = _dot(x, wu_ref[...].astype(BF16))
        col = pl.multiple_of(s * tw, tw)
        hid_ref[:, pl.ds(col, tw)] = (_silu(g) * u).astype(BF16)

    @pl.when(s >= n_up)
    def _():
        o_ref[...] = _pack_bf16_pairs(_dot(hid_ref[...], wd_ref[...].astype(BF16)))


def _ffn(xe, w_gate, w_up, w_down, experts=None, earlier=None):
    E, cap, _ = xe.shape
    e0, e1 = experts or (0, E)
    tm = min(FFN_TM, cap)
    tw = _ffn_tile_width(cap)
    n_up, n_down = D_FF // tw, D_MODEL // tw

    def up(e, m, s):
        return (e + e0, 0, jnp.minimum(s, n_up - 1))

    def down(s):
        return jnp.maximum(s - n_up, 0)

    in_specs = [
        pl.BlockSpec((None, tm, D_MODEL // 2), lambda e, m, s: (e + e0, m, 0)),
        pl.BlockSpec((None, D_MODEL, tw), up),
        pl.BlockSpec((None, D_MODEL, tw), up),
        pl.BlockSpec((None, D_FF, tw), lambda e, m, s: (e + e0, 0, down(s))),
    ]
    args = [xe, w_gate, w_up, w_down]
    aliases = {}
    if earlier is not None:
        in_specs.append(pl.BlockSpec(memory_space=pl.ANY))
        args.append(earlier)
        aliases = {len(args) - 1: 0}
    return pl.pallas_call(
        functools.partial(_ffn_kernel, tw=tw),
        grid=(e1 - e0, cap // tm, n_up + n_down),
        in_specs=in_specs,
        out_specs=pl.BlockSpec((None, tm, tw // 2), lambda e, m, s: (e + e0, m, down(s))),
        out_shape=jax.ShapeDtypeStruct((E, cap, D_MODEL // 2), I32),
        scratch_shapes=[pltpu.VMEM((tm, D_MODEL), BF16), pltpu.VMEM((tm, D_FF), BF16)],
        input_output_aliases=aliases,
        compiler_params=_cparams(("parallel", "parallel", "arbitrary"), 60),
        name="ffn",
    )(*args)


def _regroup(posb, rankb, affb, off, yw, cap, rows):
    NB, EB = posb.shape
    W = yw.shape[1]
    G, L, tt = SC_GATHER_ROWS, SC_LANES, MOE_TT
    n_workers = SC_CORES * SC_SUBCORES
    split = max(1, n_workers // NB)
    per = -(-NB * split // n_workers)
    mesh = plsc.VectorSubcoreMesh(core_axis_name="c", subcore_axis_name="s")
    off_pad = jnp.pad(off, (0, L))

    @pl.kernel(
        out_type=(jax.ShapeDtypeStruct((rows, W), I32), jax.ShapeDtypeStruct((rows,), I32),
                  jax.ShapeDtypeStruct((rows,), F32)),
        mesh=mesh,
        scratch_types=[pltpu.VMEM((EB,), I32), pltpu.VMEM((EB,), I32), pltpu.VMEM((EB,), F32),
                       pltpu.VMEM((EB,), I32), pltpu.VMEM((EB,), I32), pltpu.VMEM((EB,), F32),
                       pltpu.VMEM((G, W), I32), pltpu.VMEM((NB + 1 + L,), I32)],
        compiler_params=pltpu.CompilerParams(needs_layout_passes=False),
        name="sc_regroup",
    )
    def run(posb_hbm, rankb_hbm, affb_hbm, off_hbm, y_hbm, yg_hbm, tok_hbm, gate_hbm,
            pos_v, rank_v, aff_v, src_v, tok_v, gate_v, buf, off_v):
        wid = lax.axis_index("c") * SC_SUBCORES + lax.axis_index("s")
        pltpu.sync_copy(off_hbm, off_v)
        lane = lax.iota(I32, L)
        zi = jnp.zeros((L,), I32)
        zf = jnp.zeros((L,), F32)

        def regroup_block(j, part):
            pltpu.sync_copy(posb_hbm.at[j], pos_v)
            pltpu.sync_copy(rankb_hbm.at[j], rank_v)
            pltpu.sync_copy(affb_hbm.at[j], aff_v)
            lo = jnp.max(plsc.load_gather(off_v, [zi + j]))
            hi = jnp.max(plsc.load_gather(off_v, [zi + j + 1]))

            @pl.loop(0, EB // L)
            def _(i):
                o = pl.multiple_of(i * L, L)
                src_v[pl.ds(o, L)] = zi
                tok_v[pl.ds(o, L)] = zi
                gate_v[pl.ds(o, L)] = zf

            @pl.loop(0, EB // L)
            def _(i):
                o = pl.multiple_of(i * L, L)
                p = pos_v[pl.ds(o, L)]
                r = rank_v[pl.ds(o, L)]
                m = p >= 0
                e = i // (tt // L)
                t0 = j * tt + (i % (tt // L)) * L
                plsc.store_scatter(src_v, [r], p + e * cap, mask=m)
                plsc.store_scatter(tok_v, [r], lane + t0, mask=m)
                plsc.store_scatter(gate_v, [r], aff_v[pl.ds(o, L)], mask=m)

            @pl.loop(part, (hi - lo) // G, step=split)
            def _(g):
                o = pl.multiple_of(g * G, G)
                dst = pl.multiple_of(lo + o, G)
                pltpu.sync_copy(y_hbm.at[src_v.at[pl.ds(o, G)]], buf)
                pltpu.sync_copy(buf, yg_hbm.at[pl.ds(dst, G)])
                pltpu.sync_copy(tok_v.at[pl.ds(o, G)], tok_hbm.at[pl.ds(dst, G)])
                pltpu.sync_copy(gate_v.at[pl.ds(o, G)], gate_hbm.at[pl.ds(dst, G)])

        @pl.loop(0, per)
        def _(k):
            unit = wid * per + k

            @pl.when(unit < NB * split)
            def _():
                regroup_block(unit // split, unit % split)

    return run(posb, rankb, affb, off_pad, yw)


def _combine_kernel(pj_ref, pw_ref, pf_ref, plo_ref, phi_ref, tok_ref, gate_ref, yg_ref, h_ref, nf_ref,
                    o_ref, acc_ref, *, group):
    p = pl.program_id(0)
    flag = pf_ref[p]
    tw, tt = MOE_TW, MOE_TT
    half = D_MODEL // 2

    hw = group // 2

    def col_blocks():
        for n in range(D_MODEL // group):
            yield slice(n * hw, (n + 1) * hw), slice(n * group, n * group + hw)
            yield slice(half + n * hw, half + (n + 1) * hw), slice(n * group + hw, (n + 1) * group)

    @pl.when((flag & 1) != 0)
    def _():
        for packed, natural in col_blocks():
            acc_ref[:, packed] = h_ref[:, natural]

    @pl.when((flag & 4) != 0)
    def _():
        lo, hi = plo_ref[p], phi_ref[p]
        row0 = pw_ref[p] * tw
        rid = lax.broadcasted_iota(I32, (tw, 1), 0) + row0
        keep = (rid >= lo) & (rid < hi)
        y_lo, y_hi = _unpack_bf16_pairs(yg_ref[...])
        y_lo = jnp.where(keep, y_lo, 0.0).astype(BF16)
        y_hi = jnp.where(keep, y_hi, 0.0).astype(BF16)
        tid = lax.broadcasted_iota(I32, (tt, tw), 0) + pj_ref[p] * tt
        cid = lax.broadcasted_iota(I32, (tt, tw), 1) + row0
        hit = (tok_ref[...] == tid) & (cid >= lo) & (cid < hi)
        weights = jnp.where(hit, gate_ref[...], 0.0).astype(BF16)
        acc_ref[:, :half] += _dot(weights, y_lo)
        acc_ref[:, half:] += _dot(weights, y_hi)

    @pl.when((flag & 2) != 0)
    def _():
        y = acc_ref[...]
        scale = lax.rsqrt(jnp.mean(y * y, axis=-1, keepdims=True) + EPS)
        for packed, natural in col_blocks():
            o_ref[:, natural] = acc_ref[:, packed] * scale * nf_ref[:, natural]


def _combine(lists, tok, gate, yg, h, nfw, group):
    pj, pw, pf, plo, phi = lists
    T = h.shape[0]
    tw, tt = MOE_TW, MOE_TT
    nwin = yg.shape[0] // tw
    grid_spec = pltpu.PrefetchScalarGridSpec(
        num_scalar_prefetch=5,
        grid=(pj.shape[0],),
        in_specs=[
            pl.BlockSpec((None, 1, tw), lambda p, pj, pw, *_: (pw[p], 0, 0)),
            pl.BlockSpec((None, 1, tw), lambda p, pj, pw, *_: (pw[p], 0, 0)),
            pl.BlockSpec((tw, D_MODEL // 2), lambda p, pj, pw, *_: (pw[p], 0)),
            pl.BlockSpec((tt, D_MODEL), lambda p, pj, pw, *_: (pj[p], 0)),
            pl.BlockSpec((1, D_MODEL), lambda p, pj, pw, *_: (0, 0)),
        ],
        out_specs=pl.BlockSpec((tt, D_MODEL), lambda p, pj, pw, *_: (pj[p], 0)),
        scratch_shapes=[pltpu.VMEM((tt, D_MODEL), F32)],
    )
    return pl.pallas_call(
        functools.partial(_combine_kernel, group=group),
        grid_spec=grid_spec,
        out_shape=jax.ShapeDtypeStruct((T, D_MODEL), F32),
        compiler_params=_cparams(("arbitrary",), 48),
        name="combine",
    )(pj, pw, pf, plo, phi, tok.reshape(nwin, 1, tw), gate.reshape(nwin, 1, tw), yg, h, nfw)


def _rope_tables(seq_len):
    d = RET_DK
    inv = ROPE_BASE ** (-jnp.arange(0, d, 2, dtype=F32) / d)
    ang = jnp.arange(seq_len, dtype=F32)[:, None] * inv[None, :]
    return jnp.cos(ang), jnp.sin(ang)


def _chunk_tri(n, chunk, upper):
    r = np.arange(n)
    same = (r[:, None] // chunk) == (r[None, :] // chunk)
    tri = (r[:, None] <= r[None, :]) if upper else (r[:, None] >= r[None, :])
    return jnp.asarray(same & tri, BF16)


def _prep_params(norm1_w, w_in, ret_gn_w, gla_gate_up, gla_gate_bias, gla_gn_w, w_out, norm2_w, router_w,
                 normf_w):
    w = w_in[0]
    w_main = w[:, :IN_MAIN].astype(BF16)
    w_ga = jnp.pad(w[:, IN_MAIN:], ((0, 0), (0, LANE - 2 * GLA_RANK))).astype(BF16)
    cs = np.ones((1, IN_MAIN), np.float32)
    cs[:, _RQ:_RQ + RET_WIDTH] = RET_DK ** -0.5
    cs[:, _GQ:_GQ + GLA_KEY_WIDTH] = GLA_DK ** -0.5
    up = gla_gate_up[0].astype(F32)
    up_pad = jnp.zeros((LANE, 2 * GLA_KEY_WIDTH), F32)
    up_pad = up_pad.at[:GLA_RANK, :GLA_KEY_WIDTH].set(up[0])
    up_pad = up_pad.at[GLA_RANK:2 * GLA_RANK, GLA_KEY_WIDTH:].set(up[1])
    rt = router_w[0].T.astype(F32)
    r_hi = rt.astype(BF16)
    r_lo = (rt - r_hi.astype(F32)).astype(BF16)
    return dict(
        n1w=norm1_w[0].reshape(1, D_MODEL).astype(F32),
        w_main=w_main, w_ga=w_ga, colscale=jnp.asarray(cs),
        up_pad=up_pad.astype(BF16),
        bias=gla_gate_bias[0].reshape(1, 2 * GLA_KEY_WIDTH).astype(F32),
        lf=_chunk_tri(GATE_TM, GLA_CHUNK, upper=False),
        lb=_chunk_tri(GATE_TM, GLA_CHUNK, upper=True),
        ret_gn=ret_gn_w[0].reshape(1, RET_WIDTH).astype(F32),
        gla_gn=gla_gn_w[0].reshape(1, GLA_WIDTH).astype(F32),
        w_out=w_out[0].astype(BF16),
        n2w=norm2_w[0].reshape(1, D_MODEL).astype(F32),
        r_hi=r_hi, r_lo=r_lo,
        nfw=normf_w.reshape(1, D_MODEL).astype(F32),
    )


def _trunk_route(x, pp, decay_logit):
    B, L, _ = x.shape
    T = B * L
    x2d = x.reshape(T, D_MODEL)
    cos, sin = _rope_tables(L)
    proj, ga = _in_proj(x2d, pp["n1w"], pp["w_main"], pp["w_ga"], pp["colscale"], cos, sin, L)
    b_f, b_b = _gla_gates(ga, pp["up_pad"], pp["bias"], pp["lf"], pp["lb"])

    fwd = _mixer_scan(proj, decay_logit, b_f, B, L, reverse=False)
    mix_r, mix_g = _mixer_scan(proj, decay_logit, b_b, B, L, reverse=True, o_fwd=fwd,
                               gn_w=(pp["ret_gn"], pp["gla_gn"]))

    h, xn2, aff = _out_proj(mix_r, mix_g, pp["w_out"], x2d, pp["n2w"], pp["r_hi"], pp["r_lo"])

    cap = CAPACITY_FACTOR * T // N_EXPERTS
    pos, posb, rankb, affb = _select(aff, cap)
    off, c_lists = _combine_schedule(posb, T)
    xe = _dispatch(pos, xn2, cap).reshape(N_EXPERTS, cap, D_MODEL // 2)
    return dict(xe=xe, h=h, posb=posb, rankb=rankb, affb=affb, off=off, c_lists=c_lists)


def _trunk_regroup(route, ye):
    T = route["h"].shape[0]
    cap = CAPACITY_FACTOR * T // N_EXPERTS
    nb = T // MOE_TT
    flat = lambda a: a.reshape(nb, -1)
    yg, tok, gate = _regroup(flat(route["posb"]), flat(route["rankb"]), flat(route["affb"]), route["off"],
                             ye.reshape(N_EXPERTS * cap, D_MODEL // 2), cap, _regroup_rows(T))
    return route["c_lists"], tok, gate, yg, route["h"]


def _trunk_back(front, nfw, shape):
    c_lists, tok, gate, yg, h = front
    cap = CAPACITY_FACTOR * h.shape[0] // N_EXPERTS
    return _combine(c_lists, tok, gate, yg, h, nfw, _ffn_tile_width(cap)).reshape(shape)


def kernel(x_prompt, x_sample, norm1_w, w_in, ret_decay_logit, ret_gn_w, gla_gate_up, gla_gate_bias,
           gla_gn_w, w_out, norm2_w, router_w, w_gate, w_up, w_down, normf_w):
    pp = _prep_params(norm1_w, w_in, ret_gn_w, gla_gate_up, gla_gate_bias, gla_gn_w, w_out, norm2_w,
                      router_w, normf_w)
    decay_logit = ret_decay_logit[0].astype(F32)
    weights = (w_gate[0], w_up[0], w_down[0])
    route_p = _trunk_route(x_prompt, pp, decay_logit)
    route_s = _trunk_route(x_sample, pp, decay_logit)
    half = N_EXPERTS // 2
    ye_p = _ffn(route_p["xe"], *weights, experts=(0, half))
    ye_s = _ffn(route_s["xe"], *weights)
    xe_p, ye_p, ye_s = lax.optimization_barrier((route_p["xe"], ye_p, ye_s))
    ye_p = _ffn(xe_p, *weights, experts=(half, N_EXPERTS), earlier=ye_p)
    front_s = _trunk_regroup(route_s, ye_s)
    front_p = _trunk_regroup(route_p, ye_p)
    y_sample = _trunk_back(front_s, pp["nfw"], x_sample.shape)
    front_p, y_sample = lax.optimization_barrier((front_p, y_sample))
    y_prompt = _trunk_back(front_p, pp["nfw"], x_prompt.shape)
    return (y_prompt, y_sample)
```

```python
import functools

import numpy as np
import jax
import jax.numpy as jnp
from jax import lax
from jax.experimental import pallas as pl
from jax.experimental.pallas import tpu as pltpu
from jax.experimental.pallas import tpu_sc as plsc

F32, BF16, I32 = jnp.float32, jnp.bfloat16, jnp.int32

D_MODEL = 2048
RET_WIDTH = 1024
RET_HEADS = 4
RET_DK = 256
RET_DV = 256
GLA_WIDTH = 1024
GLA_HEADS = 4
GLA_DK = 128
GLA_DV = 256
GLA_KEY_WIDTH = 512
GLA_RANK = 16
GLA_TAU = 16.0
RET_CHUNK = 256
GLA_CHUNK = 64
ROPE_BASE = 10000.0
N_EXPERTS = 16
CAPACITY_FACTOR = 2
D_FF = 2048
EPS = 1e-6
LOG2_E = 1.4426950408889634
IN_MAIN = 4 * RET_WIDTH + 2 * GLA_KEY_WIDTH + 2 * GLA_WIDTH

_RQ, _RK, _RV, _RG = 0, 1024, 2048, 3072
_GQ, _GK, _GV, _GG = 4096, 4608, 5120, 6144

LANE = 128
MOE_TT = 512
MOE_TW = 512


def _cparams(sem, vmem_mb):
    return pltpu.CompilerParams(dimension_semantics=sem, vmem_limit_bytes=vmem_mb * 1024 * 1024)


def _log_sigmoid(z):
    return jnp.minimum(z, 0.0) - jnp.log(1.0 + jnp.exp(-jnp.abs(z)))


def _silu(g):
    return g * (1.0 / (1.0 + jnp.exp(-g)))


def _dot_nt(a, b):
    return lax.dot_general(a, b, (((1,), (1,)), ((), ())), preferred_element_type=F32)


def _dot_tn(a, b):
    return lax.dot_general(a, b, (((0,), (0,)), ((), ())), preferred_element_type=F32)


def _dot(a, b):
    return jnp.dot(a, b, preferred_element_type=F32)


def _pack_bf16_pairs(x):
    bits = pltpu.bitcast(x.astype(BF16).astype(F32), I32)
    w = x.shape[1] // 2
    return bits[:, w:] | lax.shift_right_logical(bits[:, :w], 16)


def _unpack_bf16_pairs(words):
    lo = pltpu.bitcast(lax.shift_left(words, 16), F32)
    hi = pltpu.bitcast(words & jnp.int32(-65536), F32)
    return lo, hi


IP_TM = 1024
IP_TN = 1024


def _in_proj_kernel(x_ref, n1_ref, w_ref, wga_ref, cs_ref, cos_ref, sin_ref, o_ref, ga_ref, xn_ref):
    j = pl.program_id(1)

    @pl.when(j == 0)
    def _():
        x = x_ref[...]
        ms = jnp.mean(x * x, axis=-1, keepdims=True)
        xn = (x * lax.rsqrt(ms + EPS) * n1_ref[...]).astype(BF16)
        xn_ref[...] = xn
        ga_ref[...] = _dot(xn, wga_ref[...])

    acc = _dot(xn_ref[...], w_ref[...]) * cs_ref[...]
    n_rope_blocks = 2 * RET_WIDTH // IP_TN

    @pl.when(j < n_rope_blocks)
    def _():
        cos = cos_ref[...]
        sin = sin_ref[...]
        for h in range(IP_TN // RET_DK):
            x1 = acc[:, 2 * h * LANE:(2 * h + 1) * LANE]
            x2 = acc[:, (2 * h + 1) * LANE:(2 * h + 2) * LANE]
            o_ref[2 * h] = (x1 * cos - x2 * sin).astype(BF16)
            o_ref[2 * h + 1] = (x1 * sin + x2 * cos).astype(BF16)

    @pl.when(j >= n_rope_blocks)
    def _():
        for c in range(IP_TN // LANE):
            o_ref[c] = acc[:, c * LANE:(c + 1) * LANE].astype(BF16)


def _in_proj(x2d, n1w, w_main, w_ga, colscale, cos, sin, seq_len):
    T = x2d.shape[0]
    tm, tn = IP_TM, IP_TN
    nlb = seq_len // tm
    return pl.pallas_call(
        _in_proj_kernel,
        grid=(T // tm, IN_MAIN // tn),
        in_specs=[
            pl.BlockSpec((tm, D_MODEL), lambda i, j: (i, 0)),
            pl.BlockSpec((1, D_MODEL), lambda i, j: (0, 0)),
            pl.BlockSpec((D_MODEL, tn), lambda i, j: (0, j)),
            pl.BlockSpec((D_MODEL, LANE), lambda i, j: (0, 0)),
            pl.BlockSpec((1, tn), lambda i, j: (0, j)),
            pl.BlockSpec((tm, LANE), lambda i, j: (i % nlb, 0)),
            pl.BlockSpec((tm, LANE), lambda i, j: (i % nlb, 0)),
        ],
        out_specs=[
            pl.BlockSpec((tn // LANE, tm, LANE), lambda i, j: (j, i, 0)),
            pl.BlockSpec((tm, LANE), lambda i, j: (i, 0)),
        ],
        out_shape=[
            jax.ShapeDtypeStruct((IN_MAIN // LANE, T, LANE), BF16),
            jax.ShapeDtypeStruct((T, LANE), F32),
        ],
        scratch_shapes=[pltpu.VMEM((tm, D_MODEL), BF16)],
        compiler_params=_cparams(("parallel", "arbitrary"), 48),
        name="in_proj",
    )(x2d, n1w, w_main, w_ga, colscale, cos, sin)


GATE_TM = 512


def _gates_kernel(ga_ref, up_ref, bias_ref, lf_ref, lb_ref, bf_ref, bb_ref):
    z = _dot(ga_ref[...].astype(BF16), up_ref[...]) + bias_ref[...]
    la = _log_sigmoid(z) * (LOG2_E / GLA_TAU)
    hi = la.astype(BF16)
    lo = (la - hi.astype(F32)).astype(BF16)
    kw = GLA_KEY_WIDTH
    bf_ref[...] = _dot(lf_ref[...], hi[:, :kw]) + _dot(lf_ref[...], lo[:, :kw])
    bb_ref[...] = _dot(lb_ref[...], hi[:, kw:]) + _dot(lb_ref[...], lo[:, kw:])


def _gla_gates(ga, up_pad, bias, lf, lb):
    T = ga.shape[0]
    tm = GATE_TM
    kw = GLA_KEY_WIDTH
    return pl.pallas_call(
        _gates_kernel,
        grid=(T // tm,),
        in_specs=[
            pl.BlockSpec((tm, LANE), lambda i: (i, 0)),
            pl.BlockSpec((LANE, 2 * kw), lambda i: (0, 0)),
            pl.BlockSpec((1, 2 * kw), lambda i: (0, 0)),
            pl.BlockSpec((tm, tm), lambda i: (0, 0)),
            pl.BlockSpec((tm, tm), lambda i: (0, 0)),
        ],
        out_specs=[pl.BlockSpec((tm, kw), lambda i: (i, 0)), pl.BlockSpec((tm, kw), lambda i: (i, 0))],
        out_shape=[jax.ShapeDtypeStruct((T, kw), F32), jax.ShapeDtypeStruct((T, kw), F32)],
        compiler_params=_cparams(("parallel",), 32),
        name="gla_gates",
    )(ga, up_pad, bias, lf, lb)


def _wide(ref, rows):
    return jnp.concatenate([ref[0, rows, :], ref[1, rows, :]], axis=1)


def _finish_heads(tot, gn, gate):
    ms = jnp.mean(tot * tot, axis=-1, keepdims=True)
    yn = tot * lax.rsqrt(ms + EPS) * gn
    return (yn * _silu(gate.astype(F32))).astype(BF16)


RET_TB = 1024


def _ret_kernel(dl_ref, q_ref, k_ref, v_ref, *rest, reverse):
    if reverse:
        g_ref, of_ref, gn_ref, o_ref, s_ref, intra_ref, qd_ref, kd_ref, cd_ref, p_ref, u_ref = rest
    else:
        o_ref, s_ref, intra_ref, qd_ref, kd_ref, cd_ref, p_ref, u_ref = rest
    h = pl.program_id(1)
    n = pl.program_id(2)
    C = RET_CHUNK

    @pl.when(n == 0)
    def _():
        s_ref[...] = jnp.zeros_like(s_ref)
        logit = dl_ref[1 if reverse else 0, h]
        lg = _log_sigmoid(jnp.full((C, RET_DV), logit, F32))
        lg_c = _log_sigmoid(jnp.full((C, C), logit, F32))
        lg_r = _log_sigmoid(jnp.full((1, RET_DV), logit, F32))
        ri = lax.broadcasted_iota(I32, (C, RET_DV), 0).astype(F32)
        rc = lax.broadcasted_iota(I32, (C, C), 0).astype(F32)
        cc = lax.broadcasted_iota(I32, (C, C), 1).astype(F32)
        diff = (cc - rc) if reverse else (rc - cc)
        intra_ref[...] = jnp.where(diff >= 0, jnp.exp(lg_c * diff), 0.0)
        if reverse:
            qd_ref[...] = jnp.exp(lg * (C - ri))
            kd_ref[...] = jnp.exp(lg * ri)
        else:
            qd_ref[...] = jnp.exp(lg * (ri + 1.0))
            kd_ref[...] = jnp.exp(lg * (C - 1.0 - ri))
        cd_ref[...] = jnp.exp(lg_r * C)

    nchunks = o_ref.shape[0] // C
    for c in range(nchunks):
        rows = slice(c * C, (c + 1) * C)
        k = _wide(k_ref, rows)
        p_ref[c] = (_dot_nt(_wide(q_ref, rows), k) * intra_ref[...]).astype(BF16)
        kd = (k.astype(F32) * kd_ref[...]).astype(BF16)
        u_ref[c] = _dot_tn(kd, _wide(v_ref, rows))
    order = range(nchunks - 1, -1, -1) if reverse else range(nchunks)
    for c in order:
        rows = slice(c * C, (c + 1) * C)
        state = s_ref[...]
        o = _dot(p_ref[c], _wide(v_ref, rows)) + _dot(_wide(q_ref, rows), state.astype(BF16)) * qd_ref[...]
        s_ref[...] = state * cd_ref[...] + u_ref[c]
        if reverse:
            tot = of_ref[rows, :].astype(F32) + o
            o_ref[rows, :] = _finish_heads(tot, gn_ref[...], _wide(g_ref, rows))
        else:
            o_ref[rows, :] = o.astype(BF16)


def _ret_parts(proj, decay_logit, rb, reverse, o_fwd=None, gn_w=None):
    T = proj.shape[1]
    tb = RET_TB
    dk, dv, C = RET_DK, RET_DV, RET_CHUNK

    def head(base):
        return pl.BlockSpec((dk // LANE, tb, LANE), lambda b, h, n: (base // dk + h, rb(b, n), 0))

    in_specs = [pl.BlockSpec(memory_space=pltpu.SMEM), head(_RQ), head(_RK), head(_RV)]
    args = [decay_logit, proj, proj, proj]
    if reverse:
        in_specs += [
            head(_RG),
            pl.BlockSpec((tb, dv), lambda b, h, n: (rb(b, n), h)),
            pl.BlockSpec((1, dv), lambda b, h, n: (0, h)),
        ]
        args += [proj, o_fwd, gn_w]
    out_spec = pl.BlockSpec((tb, dv), lambda b, h, n: (rb(b, n), h))
    out_shape = jax.ShapeDtypeStruct((T, RET_WIDTH), BF16)
    scratch = [
        pltpu.VMEM((dk, dv), F32),
        pltpu.VMEM((C, C), F32),
        pltpu.VMEM((C, dv), F32),
        pltpu.VMEM((C, dk), F32),
        pltpu.VMEM((1, dv), F32),
        pltpu.VMEM((tb // C, C, C), BF16),
        pltpu.VMEM((tb // C, dk, dv), F32),
    ]
    return in_specs, args, out_spec, out_shape, scratch


GLA_TB = 1024
GLA_UNROLL = 16


GLA_LEVELS = (32, 16, 8, 4, 2, 1)
SUBLANES = 8


def _gla_tables(reverse):
    C = GLA_CHUNK
    r = np.arange(C)
    masks = np.zeros((len(GLA_LEVELS) + 1, C, C), np.float32)
    for l, s in enumerate(GLA_LEVELS):
        upper = (r & s) != 0
        same = (r[:, None] // (2 * s)) == (r[None, :] // (2 * s))
        lhs_rows = ~upper if reverse else upper
        masks[l] = same & lhs_rows[:, None] & ~lhs_rows[None, :]
    masks[-1] = np.eye(C)
    return jnp.asarray(masks, F32)


def _gla_kernel(q_ref, k_ref, v_ref, b_ref, mask_ref, *rest, reverse):
    if reverse:
        g_ref, of_ref, gn_ref, o_ref, st_ref, sc_ref = rest
    else:
        o_ref, st_ref, sc_ref = rest
    n = pl.program_id(2)
    C = GLA_CHUNK

    @pl.when(n == 0)
    def _():
        st_ref[...] = jnp.zeros_like(st_ref)

    nchunks = q_ref.shape[0] // C
    sub_row = lax.broadcasted_iota(I32, (SUBLANES, GLA_DK), 0)
    zero_rows = jnp.zeros((SUBLANES, GLA_DK), F32)

    def chunk_scores(c, carry):
        c0 = pl.multiple_of(c * C, C)
        qb = q_ref[pl.ds(c0, C), :]
        kb = k_ref[pl.ds(c0, C), :]
        q = qb.astype(F32)
        k = kb.astype(F32)
        b = b_ref[pl.ds(c0, C), :]

        def mid_row(r):
            return jnp.broadcast_to(b[r:r + 1, :], (SUBLANES, GLA_DK))

        scores = mask_ref[len(GLA_LEVELS)] * _dot_nt(qb, kb)
        for l, s in enumerate(GLA_LEVELS):
            lhs, rhs = [], []
            for g in range(C // SUBLANES):
                r0 = g * SUBLANES
                rows = slice(r0, r0 + SUBLANES)
                if s >= SUBLANES:
                    m = mid_row((r0 // (2 * s)) * (2 * s) + s)
                    is_lhs = ((r0 & s) != 0) != reverse
                    if is_lhs:
                        lhs.append(q[rows] * jnp.exp2(b[rows] - m))
                        rhs.append(zero_rows)
                    else:
                        lhs.append(zero_rows)
                        rhs.append(k[rows] * jnp.exp2(m - b[rows]))
                else:
                    m = mid_row(r0 + SUBLANES - s)
                    for blk in range(SUBLANES // (2 * s) - 2, -1, -1):
                        m = jnp.where(sub_row < (blk + 1) * 2 * s, mid_row(r0 + blk * 2 * s + s), m)
                    upper = (sub_row & s) != 0
                    is_lhs = jnp.logical_not(upper) if reverse else upper
                    lhs.append(jnp.where(is_lhs, q[rows] * jnp.exp2(b[rows] - m), 0.0))
                    rhs.append(jnp.where(is_lhs, 0.0, k[rows] * jnp.exp2(m - b[rows])))
            lhs = jnp.concatenate(lhs, axis=0).astype(BF16)
            rhs = jnp.concatenate(rhs, axis=0).astype(BF16)
            scores = scores + mask_ref[l] * _dot_nt(lhs, rhs)
        sc_ref[c] = scores.astype(BF16)
        return carry

    lax.fori_loop(0, nchunks, chunk_scores, 0, unroll=GLA_UNROLL)

    def chunk(ci, carry):
        c = (nchunks - 1 - ci) if reverse else ci
        c0 = pl.multiple_of(c * C, C)
        q = q_ref[pl.ds(c0, C), :].astype(F32)
        k = k_ref[pl.ds(c0, C), :].astype(F32)
        v = _wide(v_ref, pl.ds(c0, C))
        b = b_ref[pl.ds(c0, C), :]
        b_end = b[0:1, :] if reverse else b[C - 1:C, :]

        st = st_ref[...]
        o = _dot_nt((q * jnp.exp2(b)).astype(BF16), st.astype(BF16))
        ke = (k * jnp.exp2(b_end - b)).astype(BF16)
        st_ref[...] = st * jnp.exp2(b_end) + _dot_tn(v, ke)
        o = o + _dot(sc_ref[c], v)
        if reverse:
            tot = of_ref[pl.ds(c0, C), :].astype(F32) + o
            o_ref[pl.ds(c0, C), :] = _finish_heads(tot, gn_ref[...], _wide(g_ref, pl.ds(c0, C)))
        else:
            o_ref[pl.ds(c0, C), :] = o.astype(BF16)
        return carry

    lax.fori_loop(0, nchunks, chunk, 0, unroll=GLA_UNROLL)


def _gla_parts(proj, bcum, rb, reverse, o_fwd=None, gn_w=None):
    T = proj.shape[1]
    tb = GLA_TB
    dk, dv = GLA_DK, GLA_DV
    masks = _gla_tables(reverse)

    def key_block(base):
        return pl.BlockSpec((None, tb, LANE), lambda b, h, n: (base // dk + h, rb(b, n), 0))

    def value_block(base):
        return pl.BlockSpec((dv // LANE, tb, LANE), lambda b, h, n: (base // dv + h, rb(b, n), 0))

    in_specs = [
        key_block(_GQ), key_block(_GK), value_block(_GV),
        pl.BlockSpec((tb, dk), lambda b, h, n: (rb(b, n), h)),
        pl.BlockSpec(masks.shape, lambda b, h, n: (0, 0, 0)),
    ]
    args = [proj, proj, proj, bcum, masks]
    if reverse:
        in_specs += [
            value_block(_GG),
            pl.BlockSpec((tb, dv), lambda b, h, n: (rb(b, n), h)),
            pl.BlockSpec((1, dv), lambda b, h, n: (0, h)),
        ]
        args += [proj, o_fwd, gn_w]
    out_spec = pl.BlockSpec((tb, dv), lambda b, h, n: (rb(b, n), h))
    out_shape = jax.ShapeDtypeStruct((T, GLA_WIDTH), BF16)
    scratch = [pltpu.VMEM((dv, dk), F32), pltpu.VMEM((tb // GLA_CHUNK, GLA_CHUNK, GLA_CHUNK), BF16)]
    return in_specs, args, out_spec, out_shape, scratch


def _mixer_kernel(*refs, reverse, n_ret_in, n_gla_in, n_ret_scratch):
    ret_in = refs[:n_ret_in]
    gla_in = refs[n_ret_in:n_ret_in + n_gla_in]
    ret_out, gla_out = refs[n_ret_in + n_gla_in:n_ret_in + n_gla_in + 2]
    scratch = refs[n_ret_in + n_gla_in + 2:]
    _ret_kernel(*ret_in, ret_out, *scratch[:n_ret_scratch], reverse=reverse)
    _gla_kernel(*gla_in, gla_out, *scratch[n_ret_scratch:], reverse=reverse)


def _mixer_scan(proj, decay_logit, bcum, batch, seq_len, reverse, o_fwd=(None, None), gn_w=(None, None)):
    assert RET_TB == GLA_TB and RET_HEADS == GLA_HEADS
    nb = seq_len // RET_TB

    def rb(b, n):
        return b * nb + ((nb - 1 - n) if reverse else n)

    r_specs, r_args, r_out, r_shape, r_scratch = _ret_parts(proj, decay_logit, rb, reverse, o_fwd[0], gn_w[0])
    g_specs, g_args, g_out, g_shape, g_scratch = _gla_parts(proj, bcum, rb, reverse, o_fwd[1], gn_w[1])
    return pl.pallas_call(
        functools.partial(_mixer_kernel, reverse=reverse, n_ret_in=len(r_specs), n_gla_in=len(g_specs),
                          n_ret_scratch=len(r_scratch)),
        grid=(batch, RET_HEADS, nb),
        in_specs=r_specs + g_specs,
        out_specs=[r_out, g_out],
        out_shape=[r_shape, g_shape],
        scratch_shapes=r_scratch + g_scratch,
        compiler_params=_cparams(("parallel", "parallel", "arbitrary"), 48),
        name="mixer_bwd" if reverse else "mixer_fwd",
    )(*r_args, *g_args)


OP_TM = 512


def _out_proj_kernel(mr_ref, mg_ref, w0_ref, w1_ref, x_ref, n2_ref, rh_ref, rl_ref, h_ref, xn_ref, aff_ref,
                     hs_ref):
    s = pl.program_id(0)
    slot = s % 2

    @pl.when(s == 0)
    def _():
        hs_ref[1] = jnp.zeros(hs_ref.shape[1:], F32)

    hp = hs_ref[1 - slot]
    ms = jnp.mean(hp * hp, axis=-1, keepdims=True)
    xn = hp * lax.rsqrt(ms + EPS) * n2_ref[...]
    xh = xn.astype(BF16)
    xn_ref[...] = _pack_bf16_pairs(xn)
    xl = (xn - xh.astype(F32)).astype(BF16)
    lt = _dot_nt(rh_ref[...], xh) + _dot_nt(rh_ref[...], xl) + _dot_nt(rl_ref[...], xh)
    m = jnp.max(lt, axis=0, keepdims=True)
    e = jnp.exp(lt - m)
    aff_ref[...] = e / jnp.sum(e, axis=0, keepdims=True)

    h = x_ref[...] + _dot(mr_ref[...], w0_ref[...]) + _dot(mg_ref[...], w1_ref[...])
    h_ref[...] = h
    hs_ref[slot] = h


def _out_proj(mix_r, mix_g, w_out, x2d, n2w, r_hi, r_lo):
    T = x2d.shape[0]
    tm = OP_TM
    half = RET_WIDTH
    nblk = T // tm

    def head(s):
        return jnp.minimum(s, nblk - 1)

    def tail(s):
        return jnp.maximum(s - 1, 0)

    return pl.pallas_call(
        _out_proj_kernel,
        grid=(nblk + 1,),
        in_specs=[
            pl.BlockSpec((tm, half), lambda s: (head(s), 0)),
            pl.BlockSpec((tm, half), lambda s: (head(s), 0)),
            pl.BlockSpec((half, D_MODEL), lambda s: (0, 0)),
            pl.BlockSpec((half, D_MODEL), lambda s: (1, 0)),
            pl.BlockSpec((tm, D_MODEL), lambda s: (head(s), 0)),
            pl.BlockSpec((1, D_MODEL), lambda s: (0, 0)),
            pl.BlockSpec((N_EXPERTS, D_MODEL), lambda s: (0, 0)),
            pl.BlockSpec((N_EXPERTS, D_MODEL), lambda s: (0, 0)),
        ],
        out_specs=[
            pl.BlockSpec((tm, D_MODEL), lambda s: (head(s), 0)),
            pl.BlockSpec((tm, D_MODEL // 2), lambda s: (tail(s), 0)),
            pl.BlockSpec((N_EXPERTS, tm), lambda s: (0, tail(s))),
        ],
        out_shape=[
            jax.ShapeDtypeStruct((T, D_MODEL), F32),
            jax.ShapeDtypeStruct((T, D_MODEL // 2), I32),
            jax.ShapeDtypeStruct((N_EXPERTS, T), F32),
        ],
        scratch_shapes=[pltpu.VMEM((2, tm, D_MODEL), F32)],
        compiler_params=_cparams(("arbitrary",), 60),
        name="out_proj",
    )(mix_r, mix_g, w_out, w_out, x2d, n2w, r_hi, r_lo)


def _select_kernel(a_ref, pos_ref, posb_ref, rankb_ref, affb_ref, *, cap):
    E, T = a_ref.shape
    tt = MOE_TT

    def count(pred):
        return jnp.sum(pred.astype(F32), axis=1, keepdims=True)

    def bisect(i, tau):
        cand = tau | jnp.left_shift(jnp.int32(1), 30 - i)
        bits = pltpu.bitcast(a_ref[...], I32)
        return jnp.where(count(bits >= cand) >= cap, cand, tau)

    tau = lax.fori_loop(0, 31, bisect, jnp.zeros((E, 1), I32))
    bits_all = pltpu.bitcast(a_ref[...], I32)
    quota = cap - count(bits_all > tau)

    before = (lax.broadcasted_iota(I32, (tt, tt), 0) < lax.broadcasted_iota(I32, (tt, tt), 1)).astype(BF16)
    below = (lax.broadcasted_iota(I32, (E, E), 1) < lax.broadcasted_iota(I32, (E, E), 0)).astype(BF16)

    def block(j, carry):
        c_eq, c_sel = carry
        off = pl.multiple_of(j * tt, tt)
        aff = a_ref[:, pl.ds(off, tt)]
        bits = pltpu.bitcast(aff, I32)
        eq = bits == tau
        eqf = eq.astype(F32)
        rank_eq = _dot(eqf.astype(BF16), before) + c_eq
        sel = (bits > tau) | (eq & (rank_eq < quota))
        self_ = sel.astype(F32)
        selb = self_.astype(BF16)
        slot = _dot(selb, before) + c_sel
        pos = jnp.where(sel, slot, -1.0).astype(I32)
        pos_ref[:, pl.ds(off, tt)] = pos
        per_tok = jnp.broadcast_to(jnp.sum(self_, axis=0, keepdims=True), (E, tt))
        rank = _dot(per_tok.astype(BF16), before) + _dot(below, selb)
        posb_ref[j] = pos
        rankb_ref[j] = jnp.where(sel, rank, -1.0).astype(I32)
        affb_ref[j] = aff
        return (c_eq + jnp.sum(eqf, axis=1, keepdims=True), c_sel + jnp.sum(self_, axis=1, keepdims=True))

    zero = jnp.zeros((E, 1), F32)
    lax.fori_loop(0, T // tt, block, (zero, zero))


def _select(aff, cap):
    E, T = aff.shape
    nb = T // MOE_TT
    blk = jax.ShapeDtypeStruct((nb, E, MOE_TT), I32)
    return pl.pallas_call(
        functools.partial(_select_kernel, cap=cap),
        out_shape=[jax.ShapeDtypeStruct((E, T), I32), blk, blk, jax.ShapeDtypeStruct((nb, E, MOE_TT), F32)],
        compiler_params=pltpu.CompilerParams(vmem_limit_bytes=40 * 1024 * 1024),
        name="select",
    )(aff)


def _regroup_rows(T):
    nb = T // MOE_TT
    rows = CAPACITY_FACTOR * T + SC_GATHER_ROWS * nb
    return -(-rows // MOE_TW) * MOE_TW


def _combine_schedule(posb, T):
    nb = posb.shape[0]
    tw, g = MOE_TW, SC_GATHER_ROWS
    n = jnp.sum((posb >= 0).reshape(nb, -1), axis=1).astype(I32)
    seg = (n + g - 1) // g * g
    hi = jnp.cumsum(seg)
    lo = hi - seg
    off = jnp.concatenate([jnp.zeros((1,), I32), hi])
    nwin_max = (N_EXPERTS * MOE_TT) // tw + 1
    w0 = lo // tw
    w1 = jnp.where(seg > 0, (hi - 1) // tw, w0)
    cand = jnp.arange(nwin_max, dtype=I32)
    win = w0[:, None] + cand[None, :]
    valid = (win <= w1[:, None]).reshape(-1)
    nwin_total = _regroup_rows(T) // tw
    pmax = nb + nwin_total
    jv = jnp.broadcast_to(jnp.arange(nb, dtype=I32)[:, None], win.shape).reshape(-1)
    wv = jnp.minimum(win, nwin_total - 1).reshape(-1)
    total = jnp.sum(valid.astype(I32))
    dst = jnp.where(valid, jnp.cumsum(valid.astype(I32)) - 1, pmax)
    pj, pw = (jnp.zeros((pmax,), I32).at[dst].set(a, mode="drop") for a in (jv, wv))
    real = jnp.arange(pmax, dtype=I32) < total
    pj, pw = (jnp.where(real, a, a[total - 1]) for a in (pj, pw))
    first = jnp.concatenate([jnp.ones((1,), bool), pj[1:] != pj[:-1]])
    last = jnp.concatenate([pj[1:] != pj[:-1], jnp.ones((1,), bool)]) | (jnp.arange(pmax, dtype=I32) == total - 1)
    flag = jnp.where(real, first.astype(I32) + 2 * last.astype(I32) + 4, 0)
    return off, (pj, pw, flag, lo[pj], hi[pj])


SC_LANES = 16
SC_CORES = 2
SC_SUBCORES = 16
SC_GATHER_ROWS = 32


def _dispatch(pos, xw, cap):
    E, T = pos.shape
    W = xw.shape[1]
    G = SC_GATHER_ROWS
    part_rows = cap // SC_CORES
    mesh = plsc.VectorSubcoreMesh(core_axis_name="c", subcore_axis_name="s")

    @pl.kernel(
        out_type=jax.ShapeDtypeStruct((E * cap, W), I32),
        mesh=mesh,
        scratch_types=[pltpu.VMEM((T,), I32), pltpu.VMEM((cap,), I32), pltpu.VMEM((G, W), I32)],
        compiler_params=pltpu.CompilerParams(needs_layout_passes=False),
        name="sc_dispatch",
    )
    def run(pos_hbm, x_hbm, xe_hbm, pos_v, idx_v, buf):
        e = lax.axis_index("s")
        part = lax.axis_index("c")
        pltpu.sync_copy(pos_hbm.at[e], pos_v)
        lane = lax.iota(I32, SC_LANES)

        @pl.loop(0, T // SC_LANES)
        def _(i):
            off = pl.multiple_of(i * SC_LANES, SC_LANES)
            p = pos_v[pl.ds(off, SC_LANES)]
            plsc.store_scatter(idx_v, [p], lane + off, mask=p >= 0)

        @pl.loop(0, part_rows // G)
        def _(g):
            o = pl.multiple_of(part * part_rows + g * G, G)
            pltpu.sync_copy(x_hbm.at[idx_v.at[pl.ds(o, G)]], buf)
            pltpu.sync_copy(buf, xe_hbm.at[pl.ds(e * cap + o, G)])

    return run(pos, xw)


FFN_TM = 2048
FFN_UNPACK_ROWS = 256


def _ffn_up_width(cap):
    return 512 if min(FFN_TM, cap) <= 1024 else 256


def _ffn_tile_width(cap):
    return 512


def _ffn_kernel(x_ref, wg_ref, wu_ref, wd_ref, *rest, tw):
    o_ref, xb_ref, hid_ref = rest[-3:]
    s = pl.program_id(2)
    n_up = D_FF // tw

    @pl.when(s == 0)
    def _():
        half = D_MODEL // 2

        def unpack_rows(i, carry):
            r = pl.multiple_of(i * FFN_UNPACK_ROWS, FFN_UNPACK_ROWS)
            lo, hi = _unpack_bf16_pairs(x_ref[pl.ds(r, FFN_UNPACK_ROWS), :])
            xb_ref[pl.ds(r, FFN_UNPACK_ROWS), :half] = lo.astype(BF16)
            xb_ref[pl.ds(r, FFN_UNPACK_ROWS), half:] = hi.astype(BF16)
            return carry

        lax.fori_loop(0, x_ref.shape[0] // FFN_UNPACK_ROWS, unpack_rows, 0)

    @pl.when(s < n_up)
    def _():
        x = xb_ref[...]
        g = _dot(x, wg_ref[...].astype(BF16))
        u = _dot(x, wu_ref[...].astype(BF16))
        col = pl.multiple_of(s * tw, tw)
        hid_ref[:, pl.ds(col, tw)] = (_silu(g) * u).astype(BF16)

    @pl.when(s >= n_up)
    def _():
        o_ref[...] = _pack_bf16_pairs(_dot(hid_ref[...], wd_ref[...].astype(BF16)))


def _ffn(xe, w_gate, w_up, w_down, experts=None, earlier=None):
    E, cap, _ = xe.shape
    e0, e1 = experts or (0, E)
    tm = min(FFN_TM, cap)
    tw, tg = _ffn_up_width(cap), _ffn_tile_width(cap)
    n_up, n_down = D_FF // tw, D_MODEL // tg

    def up(e, m, s):
        return (e + e0, 0, jnp.minimum(s, n_up - 1))

    def down(s):
        return jnp.maximum(s - n_up, 0)

    in_specs = [
        pl.BlockSpec((None, tm, D_MODEL // 2), lambda e, m, s: (e + e0, m, 0)),
        pl.BlockSpec((None, D_MODEL, tw), up),
        pl.BlockSpec((None, D_MODEL, tw), up),
        pl.BlockSpec((None, D_FF, tg), lambda e, m, s: (e + e0, 0, down(s))),
    ]
    args = [xe, w_gate, w_up, w_down]
    aliases = {}
    if earlier is not None:
        in_specs.append(pl.BlockSpec(memory_space=pl.ANY))
        args.append(earlier)
        aliases = {len(args) - 1: 0}
    return pl.pallas_call(
        functools.partial(_ffn_kernel, tw=tw),
        grid=(e1 - e0, cap // tm, n_up + n_down),
        in_specs=in_specs,
        out_specs=pl.BlockSpec((None, tm, tg // 2), lambda e, m, s: (e + e0, m, down(s))),
        out_shape=jax.ShapeDtypeStruct((E, cap, D_MODEL // 2), I32),
        scratch_shapes=[pltpu.VMEM((tm, D_MODEL), BF16), pltpu.VMEM((tm, D_FF), BF16)],
        input_output_aliases=aliases,
        compiler_params=_cparams(("parallel", "parallel", "arbitrary"), 60),
        name="ffn",
    )(*args)


def _regroup(posb, rankb, affb, off, yw, cap, rows):
    NB, EB = posb.shape
    W = yw.shape[1]
    G, L, tt = SC_GATHER_ROWS, SC_LANES, MOE_TT
    n_workers = SC_CORES * SC_SUBCORES
    split = max(1, n_workers // NB)
    per = -(-NB * split // n_workers)
    mesh = plsc.VectorSubcoreMesh(core_axis_name="c", subcore_axis_name="s")
    off_pad = jnp.pad(off, (0, L))

    @pl.kernel(
        out_type=(jax.ShapeDtypeStruct((rows, W), I32), jax.ShapeDtypeStruct((rows,), I32),
                  jax.ShapeDtypeStruct((rows,), F32)),
        mesh=mesh,
        scratch_types=[pltpu.VMEM((EB,), I32), pltpu.VMEM((EB,), I32), pltpu.VMEM((EB,), F32),
                       pltpu.VMEM((EB,), I32), pltpu.VMEM((EB,), I32), pltpu.VMEM((EB,), F32),
                       pltpu.VMEM((G, W), I32), pltpu.VMEM((NB + 1 + L,), I32)],
        compiler_params=pltpu.CompilerParams(needs_layout_passes=False),
        name="sc_regroup",
    )
    def run(posb_hbm, rankb_hbm, affb_hbm, off_hbm, y_hbm, yg_hbm, tok_hbm, gate_hbm,
            pos_v, rank_v, aff_v, src_v, tok_v, gate_v, buf, off_v):
        wid = lax.axis_index("c") * SC_SUBCORES + lax.axis_index("s")
        pltpu.sync_copy(off_hbm, off_v)
        lane = lax.iota(I32, L)
        zi = jnp.zeros((L,), I32)
        zf = jnp.zeros((L,), F32)

        def regroup_block(j, part):
            pltpu.sync_copy(posb_hbm.at[j], pos_v)
            pltpu.sync_copy(rankb_hbm.at[j], rank_v)
            pltpu.sync_copy(affb_hbm.at[j], aff_v)
            lo = jnp.max(plsc.load_gather(off_v, [zi + j]))
            hi = jnp.max(plsc.load_gather(off_v, [zi + j + 1]))

            @pl.loop(0, EB // L)
            def _(i):
                o = pl.multiple_of(i * L, L)
                src_v[pl.ds(o, L)] = zi
                tok_v[pl.ds(o, L)] = zi
                gate_v[pl.ds(o, L)] = zf

            @pl.loop(0, EB // L)
            def _(i):
                o = pl.multiple_of(i * L, L)
                p = pos_v[pl.ds(o, L)]
                r = rank_v[pl.ds(o, L)]
                m = p >= 0
                e = i // (tt // L)
                t0 = j * tt + (i % (tt // L)) * L
                plsc.store_scatter(src_v, [r], p + e * cap, mask=m)
                plsc.store_scatter(tok_v, [r], lane + t0, mask=m)
                plsc.store_scatter(gate_v, [r], aff_v[pl.ds(o, L)], mask=m)

            @pl.loop(part, (hi - lo) // G, step=split)
            def _(g):
                o = pl.multiple_of(g * G, G)
                dst = pl.multiple_of(lo + o, G)
                pltpu.sync_copy(y_hbm.at[src_v.at[pl.ds(o, G)]], buf)
                pltpu.sync_copy(buf, yg_hbm.at[pl.ds(dst, G)])
                pltpu.sync_copy(tok_v.at[pl.ds(o, G)], tok_hbm.at[pl.ds(dst, G)])
                pltpu.sync_copy(gate_v.at[pl.ds(o, G)], gate_hbm.at[pl.ds(dst, G)])

        @pl.loop(0, per)
        def _(k):
            unit = wid * per + k

            @pl.when(unit < NB * split)
            def _():
                regroup_block(unit // split, unit % split)

    return run(posb, rankb, affb, off_pad, yw)


def _combine_kernel(pj_ref, pw_ref, pf_ref, plo_ref, phi_ref, tok_ref, gate_ref, yg_ref, h_ref, nf_ref,
                    o_ref, acc_ref, *, group):
    p = pl.program_id(0)
    flag = pf_ref[p]
    tw, tt = MOE_TW, MOE_TT
    half = D_MODEL // 2

    hw = group // 2

    def col_blocks():
        for n in range(D_MODEL // group):
            yield slice(n * hw, (n + 1) * hw), slice(n * group, n * group + hw)
            yield slice(half + n * hw, half + (n + 1) * hw), slice(n * group + hw, (n + 1) * group)

    @pl.when((flag & 1) != 0)
    def _():
        for packed, natural in col_blocks():
            acc_ref[:, packed] = h_ref[:, natural]

    @pl.when((flag & 4) != 0)
    def _():
        lo, hi = plo_ref[p], phi_ref[p]
        row0 = pw_ref[p] * tw
        rid = lax.broadcasted_iota(I32, (tw, 1), 0) + row0
        keep = (rid >= lo) & (rid < hi)
        y_lo, y_hi = _unpack_bf16_pairs(yg_ref[...])
        y_lo = jnp.where(keep, y_lo, 0.0).astype(BF16)
        y_hi = jnp.where(keep, y_hi, 0.0).astype(BF16)
        tid = lax.broadcasted_iota(I32, (tt, tw), 0) + pj_ref[p] * tt
        cid = lax.broadcasted_iota(I32, (tt, tw), 1) + row0
        hit = (tok_ref[...] == tid) & (cid >= lo) & (cid < hi)
        weights = jnp.where(hit, gate_ref[...], 0.0).astype(BF16)
        acc_ref[:, :half] += _dot(weights, y_lo)
        acc_ref[:, half:] += _dot(weights, y_hi)

    @pl.when((flag & 2) != 0)
    def _():
        y = acc_ref[...]
        scale = lax.rsqrt(jnp.mean(y * y, axis=-1, keepdims=True) + EPS)
        for packed, natural in col_blocks():
            o_ref[:, natural] = acc_ref[:, packed] * scale * nf_ref[:, natural]


def _combine(lists, tok, gate, yg, h, nfw, group):
    pj, pw, pf, plo, phi = lists
    T = h.shape[0]
    tw, tt = MOE_TW, MOE_TT
    nwin = yg.shape[0] // tw
    grid_spec = pltpu.PrefetchScalarGridSpec(
        num_scalar_prefetch=5,
        grid=(pj.shape[0],),
        in_specs=[
            pl.BlockSpec((None, 1, tw), lambda p, pj, pw, *_: (pw[p], 0, 0)),
            pl.BlockSpec((None, 1, tw), lambda p, pj, pw, *_: (pw[p], 0, 0)),
            pl.BlockSpec((tw, D_MODEL // 2), lambda p, pj, pw, *_: (pw[p], 0)),
            pl.BlockSpec((tt, D_MODEL), lambda p, pj, pw, *_: (pj[p], 0)),
            pl.BlockSpec((1, D_MODEL), lambda p, pj, pw, *_: (0, 0)),
        ],
        out_specs=pl.BlockSpec((tt, D_MODEL), lambda p, pj, pw, *_: (pj[p], 0)),
        scratch_shapes=[pltpu.VMEM((tt, D_MODEL), F32)],
    )
    return pl.pallas_call(
        functools.partial(_combine_kernel, group=group),
        grid_spec=grid_spec,
        out_shape=jax.ShapeDtypeStruct((T, D_MODEL), F32),
        compiler_params=_cparams(("arbitrary",), 48),
        name="combine",
    )(pj, pw, pf, plo, phi, tok.reshape(nwin, 1, tw), gate.reshape(nwin, 1, tw), yg, h, nfw)


def _rope_tables(seq_len):
    d = RET_DK
    inv = ROPE_BASE ** (-jnp.arange(0, d, 2, dtype=F32) / d)
    ang = jnp.arange(seq_len, dtype=F32)[:, None] * inv[None, :]
    return jnp.cos(ang), jnp.sin(ang)


def _chunk_tri(n, chunk, upper):
    r = np.arange(n)
    same = (r[:, None] // chunk) == (r[None, :] // chunk)
    tri = (r[:, None] <= r[None, :]) if upper else (r[:, None] >= r[None, :])
    return jnp.asarray(same & tri, BF16)


def _prep_params(norm1_w, w_in, ret_gn_w, gla_gate_up, gla_gate_bias, gla_gn_w, w_out, norm2_w, router_w,
                 normf_w):
    w = w_in[0]
    w_main = w[:, :IN_MAIN].astype(BF16)
    w_ga = jnp.pad(w[:, IN_MAIN:], ((0, 0), (0, LANE - 2 * GLA_RANK))).astype(BF16)
    cs = np.ones((1, IN_MAIN), np.float32)
    cs[:, _RQ:_RQ + RET_WIDTH] = RET_DK ** -0.5
    cs[:, _GQ:_GQ + GLA_KEY_WIDTH] = GLA_DK ** -0.5
    up = gla_gate_up[0].astype(F32)
    up_pad = jnp.zeros((LANE, 2 * GLA_KEY_WIDTH), F32)
    up_pad = up_pad.at[:GLA_RANK, :GLA_KEY_WIDTH].set(up[0])
    up_pad = up_pad.at[GLA_RANK:2 * GLA_RANK, GLA_KEY_WIDTH:].set(up[1])
    rt = router_w[0].T.astype(F32)
    r_hi = rt.astype(BF16)
    r_lo = (rt - r_hi.astype(F32)).astype(BF16)
    return dict(
        n1w=norm1_w[0].reshape(1, D_MODEL).astype(F32),
        w_main=w_main, w_ga=w_ga, colscale=jnp.asarray(cs),
        up_pad=up_pad.astype(BF16),
        bias=gla_gate_bias[0].reshape(1, 2 * GLA_KEY_WIDTH).astype(F32),
        lf=_chunk_tri(GATE_TM, GLA_CHUNK, upper=False),
        lb=_chunk_tri(GATE_TM, GLA_CHUNK, upper=True),
        ret_gn=ret_gn_w[0].reshape(1, RET_WIDTH).astype(F32),
        gla_gn=gla_gn_w[0].reshape(1, GLA_WIDTH).astype(F32),
        w_out=w_out[0].astype(BF16),
        n2w=norm2_w[0].reshape(1, D_MODEL).astype(F32),
        r_hi=r_hi, r_lo=r_lo,
        nfw=normf_w.reshape(1, D_MODEL).astype(F32),
    )


def _trunk_route(x, pp, decay_logit):
    B, L, _ = x.shape
    T = B * L
    x2d = x.reshape(T, D_MODEL)
    cos, sin = _rope_tables(L)
    proj, ga = _in_proj(x2d, pp["n1w"], pp["w_main"], pp["w_ga"], pp["colscale"], cos, sin, L)
    b_f, b_b = _gla_gates(ga, pp["up_pad"], pp["bias"], pp["lf"], pp["lb"])

    fwd = _mixer_scan(proj, decay_logit, b_f, B, L, reverse=False)
    mix_r, mix_g = _mixer_scan(proj, decay_logit, b_b, B, L, reverse=True, o_fwd=fwd,
                               gn_w=(pp["ret_gn"], pp["gla_gn"]))

    h, xn2, aff = _out_proj(mix_r, mix_g, pp["w_out"], x2d, pp["n2w"], pp["r_hi"], pp["r_lo"])

    cap = CAPACITY_FACTOR * T // N_EXPERTS
    pos, posb, rankb, affb = _select(aff, cap)
    off, c_lists = _combine_schedule(posb, T)
    xe = _dispatch(pos, xn2, cap).reshape(N_EXPERTS, cap, D_MODEL // 2)
    return dict(xe=xe, h=h, posb=posb, rankb=rankb, affb=affb, off=off, c_lists=c_lists)


def _trunk_regroup(route, ye):
    T = route["h"].shape[0]
    cap = CAPACITY_FACTOR * T // N_EXPERTS
    nb = T // MOE_TT
    flat = lambda a: a.reshape(nb, -1)
    yg, tok, gate = _regroup(flat(route["posb"]), flat(route["rankb"]), flat(route["affb"]), route["off"],
                             ye.reshape(N_EXPERTS * cap, D_MODEL // 2), cap, _regroup_rows(T))
    return route["c_lists"], tok, gate, yg, route["h"]


def _trunk_back(front, nfw, shape):
    c_lists, tok, gate, yg, h = front
    cap = CAPACITY_FACTOR * h.shape[0] // N_EXPERTS
    return _combine(c_lists, tok, gate, yg, h, nfw, _ffn_tile_width(cap)).reshape(shape)


def kernel(x_prompt, x_sample, norm1_w, w_in, ret_decay_logit, ret_gn_w, gla_gate_up, gla_gate_bias,
           gla_gn_w, w_out, norm2_w, router_w, w_gate, w_up, w_down, normf_w):
    pp = _prep_params(norm1_w, w_in, ret_gn_w, gla_gate_up, gla_gate_bias, gla_gn_w, w_out, norm2_w,
                      router_w, normf_w)
    decay_logit = ret_decay_logit[0].astype(F32)
    weights = (w_gate[0], w_up[0], w_down[0])
    route_p = _trunk_route(x_prompt, pp, decay_logit)
    route_s = _trunk_route(x_sample, pp, decay_logit)
    half = N_EXPERTS // 2
    ye_p = _ffn(route_p["xe"], *weights, experts=(0, half))
    ye_s = _ffn(route_s["xe"], *weights)
    xe_p, ye_p, ye_s = lax.optimization_barrier((route_p["xe"], ye_p, ye_s))
    ye_p = _ffn(xe_p, *weights, experts=(half, N_EXPERTS), earlier=ye_p)
    front_s = _trunk_regroup(route_s, ye_s)
    front_p = _trunk_regroup(route_p, ye_p)
    y_sample = _trunk_back(front_s, pp["nfw"], x_sample.shape)
    front_p, y_sample = lax.optimization_barrier((front_p, y_sample))
    y_prompt = _trunk_back(front_p, pp["nfw"], x_prompt.shape)
    return (y_prompt, y_sample)
```

```python
import functools

import numpy as np
import jax
import jax.numpy as jnp
from jax import lax
from jax.experimental import pallas as pl
from jax.experimental.pallas import tpu as pltpu
from jax.experimental.pallas import tpu_sc as plsc

F32, BF16, I32 = jnp.float32, jnp.bfloat16, jnp.int32

D_MODEL = 2048
RET_WIDTH = 1024
RET_HEADS = 4
RET_DK = 256
RET_DV = 256
GLA_WIDTH = 1024
GLA_HEADS = 4
GLA_DK = 128
GLA_DV = 256
GLA_KEY_WIDTH = 512
GLA_RANK = 16
GLA_TAU = 16.0
RET_CHUNK = 256
GLA_CHUNK = 64
ROPE_BASE = 10000.0
N_EXPERTS = 16
CAPACITY_FACTOR = 2
D_FF = 2048
EPS = 1e-6
LOG2_E = 1.4426950408889634
IN_MAIN = 4 * RET_WIDTH + 2 * GLA_KEY_WIDTH + 2 * GLA_WIDTH

_RQ, _RK, _RV, _RG = 0, 1024, 2048, 3072
_GQ, _GK, _GV, _GG = 4096, 4608, 5120, 6144

LANE = 128
MOE_TT = 512
MOE_TW = 512


def _cparams(sem, vmem_mb):
    return pltpu.CompilerParams(dimension_semantics=sem, vmem_limit_bytes=vmem_mb * 1024 * 1024)


def _log_sigmoid(z):
    return jnp.minimum(z, 0.0) - jnp.log(1.0 + jnp.exp(-jnp.abs(z)))


def _silu(g):
    return g * (1.0 / (1.0 + jnp.exp(-g)))


def _dot_nt(a, b):
    return lax.dot_general(a, b, (((1,), (1,)), ((), ())), preferred_element_type=F32)


def _dot_tn(a, b):
    return lax.dot_general(a, b, (((0,), (0,)), ((), ())), preferred_element_type=F32)


def _dot(a, b):
    return jnp.dot(a, b, preferred_element_type=F32)


def _pack_bf16_pairs(x):
    bits = pltpu.bitcast(x.astype(BF16).astype(F32), I32)
    w = x.shape[1] // 2
    return bits[:, w:] | lax.shift_right_logical(bits[:, :w], 16)


def _unpack_bf16_pairs(words):
    lo = pltpu.bitcast(lax.shift_left(words, 16), F32)
    hi = pltpu.bitcast(words & jnp.int32(-65536), F32)
    return lo, hi


IP_TM = 1024
IP_TN = 1792


def _in_proj_kernel(x_ref, n1_ref, w_ref, wga_ref, cs_ref, cos_ref, sin_ref, o_ref, ga_ref, xn_ref):
    j = pl.program_id(1)

    @pl.when(j == 0)
    def _():
        x = x_ref[...]
        ms = jnp.mean(x * x, axis=-1, keepdims=True)
        xn = (x * lax.rsqrt(ms + EPS) * n1_ref[...]).astype(BF16)
        xn_ref[...] = xn
        ga_ref[...] = _dot(xn, wga_ref[...])

    acc = _dot(xn_ref[...], w_ref[...]) * cs_ref[...]
    heads_per_block = IP_TN // RET_DK
    n_rope_heads = 2 * RET_WIDTH // RET_DK

    def store(n_rotary):
        for h in range(heads_per_block):
            x1 = acc[:, 2 * h * LANE:(2 * h + 1) * LANE]
            x2 = acc[:, (2 * h + 1) * LANE:(2 * h + 2) * LANE]
            if h < n_rotary:
                cos = cos_ref[...]
                sin = sin_ref[...]
                o_ref[2 * h] = (x1 * cos - x2 * sin).astype(BF16)
                o_ref[2 * h + 1] = (x1 * sin + x2 * cos).astype(BF16)
            else:
                o_ref[2 * h] = x1.astype(BF16)
                o_ref[2 * h + 1] = x2.astype(BF16)

    for blk in range(IN_MAIN // IP_TN):
        n_rotary = min(max(n_rope_heads - blk * heads_per_block, 0), heads_per_block)
        if n_rotary > 0:
            pl.when(j == blk)(functools.partial(store, n_rotary))
        else:
            pl.when(j >= blk)(functools.partial(store, 0))
            break


def _in_proj(x2d, n1w, w_main, w_ga, colscale, cos, sin, seq_len):
    T = x2d.shape[0]
    tm, tn = IP_TM, IP_TN
    nlb = seq_len // tm
    return pl.pallas_call(
        _in_proj_kernel,
        grid=(T // tm, IN_MAIN // tn),
        in_specs=[
            pl.BlockSpec((tm, D_MODEL), lambda i, j: (i, 0)),
            pl.BlockSpec((1, D_MODEL), lambda i, j: (0, 0)),
            pl.BlockSpec((D_MODEL, tn), lambda i, j: (0, j)),
            pl.BlockSpec((D_MODEL, LANE), lambda i, j: (0, 0)),
            pl.BlockSpec((1, tn), lambda i, j: (0, j)),
            pl.BlockSpec((tm, LANE), lambda i, j: (i % nlb, 0)),
            pl.BlockSpec((tm, LANE), lambda i, j: (i % nlb, 0)),
        ],
        out_specs=[
            pl.BlockSpec((tn // LANE, tm, LANE), lambda i, j: (j, i, 0)),
            pl.BlockSpec((tm, LANE), lambda i, j: (i, 0)),
        ],
        out_shape=[
            jax.ShapeDtypeStruct((IN_MAIN // LANE, T, LANE), BF16),
            jax.ShapeDtypeStruct((T, LANE), F32),
        ],
        scratch_shapes=[pltpu.VMEM((tm, D_MODEL), BF16)],
        compiler_params=_cparams(("parallel", "arbitrary"), 60),
        name="in_proj",
    )(x2d, n1w, w_main, w_ga, colscale, cos, sin)


GATE_TM = 512


def _gates_kernel(ga_ref, up_ref, bias_ref, lf_ref, lb_ref, bf_ref, bb_ref):
    z = _dot(ga_ref[...].astype(BF16), up_ref[...]) + bias_ref[...]
    la = _log_sigmoid(z) * (LOG2_E / GLA_TAU)
    hi = la.astype(BF16)
    lo = (la - hi.astype(F32)).astype(BF16)
    kw = GLA_KEY_WIDTH
    bf_ref[...] = _dot(lf_ref[...], hi[:, :kw]) + _dot(lf_ref[...], lo[:, :kw])
    bb_ref[...] = _dot(lb_ref[...], hi[:, kw:]) + _dot(lb_ref[...], lo[:, kw:])


def _gla_gates(ga, up_pad, bias, lf, lb):
    T = ga.shape[0]
    tm = GATE_TM
    kw = GLA_KEY_WIDTH
    return pl.pallas_call(
        _gates_kernel,
        grid=(T // tm,),
        in_specs=[
            pl.BlockSpec((tm, LANE), lambda i: (i, 0)),
            pl.BlockSpec((LANE, 2 * kw), lambda i: (0, 0)),
            pl.BlockSpec((1, 2 * kw), lambda i: (0, 0)),
            pl.BlockSpec((tm, tm), lambda i: (0, 0)),
            pl.BlockSpec((tm, tm), lambda i: (0, 0)),
        ],
        out_specs=[pl.BlockSpec((tm, kw), lambda i: (i, 0)), pl.BlockSpec((tm, kw), lambda i: (i, 0))],
        out_shape=[jax.ShapeDtypeStruct((T, kw), F32), jax.ShapeDtypeStruct((T, kw), F32)],
        compiler_params=_cparams(("parallel",), 32),
        name="gla_gates",
    )(ga, up_pad, bias, lf, lb)


def _wide(ref, rows):
    return jnp.concatenate([ref[0, rows, :], ref[1, rows, :]], axis=1)


def _finish_heads(tot, gn, gate):
    ms = jnp.mean(tot * tot, axis=-1, keepdims=True)
    yn = tot * lax.rsqrt(ms + EPS) * gn
    return (yn * _silu(gate.astype(F32))).astype(BF16)


RET_TB = 1024


def _ret_kernel(dl_ref, q_ref, k_ref, v_ref, *rest, reverse):
    if reverse:
        g_ref, of_ref, gn_ref, o_ref, s_ref, intra_ref, qd_ref, kd_ref, cd_ref, p_ref, u_ref = rest
    else:
        o_ref, s_ref, intra_ref, qd_ref, kd_ref, cd_ref, p_ref, u_ref = rest
    h = pl.program_id(1)
    n = pl.program_id(2)
    C = RET_CHUNK

    @pl.when(n == 0)
    def _():
        s_ref[...] = jnp.zeros_like(s_ref)
        logit = dl_ref[1 if reverse else 0, h]
        lg = _log_sigmoid(jnp.full((C, RET_DV), logit, F32))
        lg_c = _log_sigmoid(jnp.full((C, C), logit, F32))
        lg_r = _log_sigmoid(jnp.full((1, RET_DV), logit, F32))
        ri = lax.broadcasted_iota(I32, (C, RET_DV), 0).astype(F32)
        rc = lax.broadcasted_iota(I32, (C, C), 0).astype(F32)
        cc = lax.broadcasted_iota(I32, (C, C), 1).astype(F32)
        diff = (cc - rc) if reverse else (rc - cc)
        intra_ref[...] = jnp.where(diff >= 0, jnp.exp(lg_c * diff), 0.0)
        if reverse:
            qd_ref[...] = jnp.exp(lg * (C - ri))
            kd_ref[...] = jnp.exp(lg * ri)
        else:
            qd_ref[...] = jnp.exp(lg * (ri + 1.0))
            kd_ref[...] = jnp.exp(lg * (C - 1.0 - ri))
        cd_ref[...] = jnp.exp(lg_r * C)

    nchunks = o_ref.shape[0] // C
    for c in range(nchunks):
        rows = slice(c * C, (c + 1) * C)
        k = _wide(k_ref, rows)
        p_ref[c] = (_dot_nt(_wide(q_ref, rows), k) * intra_ref[...]).astype(BF16)
        kd = (k.astype(F32) * kd_ref[...]).astype(BF16)
        u_ref[c] = _dot_tn(kd, _wide(v_ref, rows))
    order = range(nchunks - 1, -1, -1) if reverse else range(nchunks)
    for c in order:
        rows = slice(c * C, (c + 1) * C)
        state = s_ref[...]
        o = _dot(p_ref[c], _wide(v_ref, rows)) + _dot(_wide(q_ref, rows), state.astype(BF16)) * qd_ref[...]
        s_ref[...] = state * cd_ref[...] + u_ref[c]
        if reverse:
            tot = of_ref[rows, :].astype(F32) + o
            o_ref[rows, :] = _finish_heads(tot, gn_ref[...], _wide(g_ref, rows))
        else:
            o_ref[rows, :] = o.astype(BF16)


def _ret_parts(proj, decay_logit, rb, reverse, o_fwd=None, gn_w=None):
    T = proj.shape[1]
    tb = RET_TB
    dk, dv, C = RET_DK, RET_DV, RET_CHUNK

    def head(base):
        return pl.BlockSpec((dk // LANE, tb, LANE), lambda b, h, n: (base // dk + h, rb(b, n), 0))

    in_specs = [pl.BlockSpec(memory_space=pltpu.SMEM), head(_RQ), head(_RK), head(_RV)]
    args = [decay_logit, proj, proj, proj]
    if reverse:
        in_specs += [
            head(_RG),
            pl.BlockSpec((tb, dv), lambda b, h, n: (rb(b, n), h)),
            pl.BlockSpec((1, dv), lambda b, h, n: (0, h)),
        ]
        args += [proj, o_fwd, gn_w]
    out_spec = pl.BlockSpec((tb, dv), lambda b, h, n: (rb(b, n), h))
    out_shape = jax.ShapeDtypeStruct((T, RET_WIDTH), BF16)
    scratch = [
        pltpu.VMEM((dk, dv), F32),
        pltpu.VMEM((C, C), F32),
        pltpu.VMEM((C, dv), F32),
        pltpu.VMEM((C, dk), F32),
        pltpu.VMEM((1, dv), F32),
        pltpu.VMEM((tb // C, C, C), BF16),
        pltpu.VMEM((tb // C, dk, dv), F32),
    ]
    return in_specs, args, out_spec, out_shape, scratch


GLA_TB = 1024
GLA_UNROLL = 16


GLA_LEVELS = (32, 16, 8, 4, 2, 1)
SUBLANES = 8


def _gla_tables(reverse):
    C = GLA_CHUNK
    r = np.arange(C)
    masks = np.zeros((len(GLA_LEVELS) + 1, C, C), np.float32)
    for l, s in enumerate(GLA_LEVELS):
        upper = (r & s) != 0
        same = (r[:, None] // (2 * s)) == (r[None, :] // (2 * s))
        lhs_rows = ~upper if reverse else upper
        masks[l] = same & lhs_rows[:, None] & ~lhs_rows[None, :]
    masks[-1] = np.eye(C)
    return jnp.asarray(masks, F32)


def _gla_kernel(q_ref, k_ref, v_ref, b_ref, mask_ref, *rest, reverse):
    if reverse:
        g_ref, of_ref, gn_ref, o_ref, st_ref, sc_ref = rest
    else:
        o_ref, st_ref, sc_ref = rest
    n = pl.program_id(2)
    C = GLA_CHUNK

    @pl.when(n == 0)
    def _():
        st_ref[...] = jnp.zeros_like(st_ref)

    nchunks = q_ref.shape[0] // C
    sub_row = lax.broadcasted_iota(I32, (SUBLANES, GLA_DK), 0)
    zero_rows = jnp.zeros((SUBLANES, GLA_DK), F32)

    def chunk_scores(c, carry):
        c0 = pl.multiple_of(c * C, C)
        qb = q_ref[pl.ds(c0, C), :]
        kb = k_ref[pl.ds(c0, C), :]
        q = qb.astype(F32)
        k = kb.astype(F32)
        b = b_ref[pl.ds(c0, C), :]

        def mid_row(r):
            return jnp.broadcast_to(b[r:r + 1, :], (SUBLANES, GLA_DK))

        scores = mask_ref[len(GLA_LEVELS)] * _dot_nt(qb, kb)
        for l, s in enumerate(GLA_LEVELS):
            lhs, rhs = [], []
            for g in range(C // SUBLANES):
                r0 = g * SUBLANES
                rows = slice(r0, r0 + SUBLANES)
                if s >= SUBLANES:
                    m = mid_row((r0 // (2 * s)) * (2 * s) + s)
                    is_lhs = ((r0 & s) != 0) != reverse
                    if is_lhs:
                        lhs.append(q[rows] * jnp.exp2(b[rows] - m))
                        rhs.append(zero_rows)
                    else:
                        lhs.append(zero_rows)
                        rhs.append(k[rows] * jnp.exp2(m - b[rows]))
                else:
                    m = mid_row(r0 + SUBLANES - s)
                    for blk in range(SUBLANES // (2 * s) - 2, -1, -1):
                        m = jnp.where(sub_row < (blk + 1) * 2 * s, mid_row(r0 + blk * 2 * s + s), m)
                    upper = (sub_row & s) != 0
                    is_lhs = jnp.logical_not(upper) if reverse else upper
                    lhs.append(jnp.where(is_lhs, q[rows] * jnp.exp2(b[rows] - m), 0.0))
                    rhs.append(jnp.where(is_lhs, 0.0, k[rows] * jnp.exp2(m - b[rows])))
            lhs = jnp.concatenate(lhs, axis=0).astype(BF16)
            rhs = jnp.concatenate(rhs, axis=0).astype(BF16)
            scores = scores + mask_ref[l] * _dot_nt(lhs, rhs)
        sc_ref[c] = scores.astype(BF16)
        return carry

    lax.fori_loop(0, nchunks, chunk_scores, 0, unroll=GLA_UNROLL)

    def chunk(ci, carry):
        c = (nchunks - 1 - ci) if reverse else ci
        c0 = pl.multiple_of(c * C, C)
        q = q_ref[pl.ds(c0, C), :].astype(F32)
        k = k_ref[pl.ds(c0, C), :].astype(F32)
        v = _wide(v_ref, pl.ds(c0, C))
        b = b_ref[pl.ds(c0, C), :]
        b_end = b[0:1, :] if reverse else b[C - 1:C, :]

        st = st_ref[...]
        o = _dot_nt((q * jnp.exp2(b)).astype(BF16), st.astype(BF16))
        ke = (k * jnp.exp2(b_end - b)).astype(BF16)
        st_ref[...] = st * jnp.exp2(b_end) + _dot_tn(v, ke)
        o = o + _dot(sc_ref[c], v)
        if reverse:
            tot = of_ref[pl.ds(c0, C), :].astype(F32) + o
            o_ref[pl.ds(c0, C), :] = _finish_heads(tot, gn_ref[...], _wide(g_ref, pl.ds(c0, C)))
        else:
            o_ref[pl.ds(c0, C), :] = o.astype(BF16)
        return carry

    lax.fori_loop(0, nchunks, chunk, 0, unroll=GLA_UNROLL)


def _gla_parts(proj, bcum, rb, reverse, o_fwd=None, gn_w=None):
    T = proj.shape[1]
    tb = GLA_TB
    dk, dv = GLA_DK, GLA_DV
    masks = _gla_tables(reverse)

    def key_block(base):
        return pl.BlockSpec((None, tb, LANE), lambda b, h, n: (base // dk + h, rb(b, n), 0))

    def value_block(base):
        return pl.BlockSpec((dv // LANE, tb, LANE), lambda b, h, n: (base // dv + h, rb(b, n), 0))

    in_specs = [
        key_block(_GQ), key_block(_GK), value_block(_GV),
        pl.BlockSpec((tb, dk), lambda b, h, n: (rb(b, n), h)),
        pl.BlockSpec(masks.shape, lambda b, h, n: (0, 0, 0)),
    ]
    args = [proj, proj, proj, bcum, masks]
    if reverse:
        in_specs += [
            value_block(_GG),
            pl.BlockSpec((tb, dv), lambda b, h, n: (rb(b, n), h)),
            pl.BlockSpec((1, dv), lambda b, h, n: (0, h)),
        ]
        args += [proj, o_fwd, gn_w]
    out_spec = pl.BlockSpec((tb, dv), lambda b, h, n: (rb(b, n), h))
    out_shape = jax.ShapeDtypeStruct((T, GLA_WIDTH), BF16)
    scratch = [pltpu.VMEM((dv, dk), F32), pltpu.VMEM((tb // GLA_CHUNK, GLA_CHUNK, GLA_CHUNK), BF16)]
    return in_specs, args, out_spec, out_shape, scratch


def _mixer_kernel(*refs, reverse, n_ret_in, n_gla_in, n_ret_scratch):
    ret_in = refs[:n_ret_in]
    gla_in = refs[n_ret_in:n_ret_in + n_gla_in]
    ret_out, gla_out = refs[n_ret_in + n_gla_in:n_ret_in + n_gla_in + 2]
    scratch = refs[n_ret_in + n_gla_in + 2:]
    _ret_kernel(*ret_in, ret_out, *scratch[:n_ret_scratch], reverse=reverse)
    _gla_kernel(*gla_in, gla_out, *scratch[n_ret_scratch:], reverse=reverse)


def _mixer_scan(proj, decay_logit, bcum, batch, seq_len, reverse, o_fwd=(None, None), gn_w=(None, None)):
    assert RET_TB == GLA_TB and RET_HEADS == GLA_HEADS
    nb = seq_len // RET_TB

    def rb(b, n):
        return b * nb + ((nb - 1 - n) if reverse else n)

    r_specs, r_args, r_out, r_shape, r_scratch = _ret_parts(proj, decay_logit, rb, reverse, o_fwd[0], gn_w[0])
    g_specs, g_args, g_out, g_shape, g_scratch = _gla_parts(proj, bcum, rb, reverse, o_fwd[1], gn_w[1])
    return pl.pallas_call(
        functools.partial(_mixer_kernel, reverse=reverse, n_ret_in=len(r_specs), n_gla_in=len(g_specs),
                          n_ret_scratch=len(r_scratch)),
        grid=(batch, RET_HEADS, nb),
        in_specs=r_specs + g_specs,
        out_specs=[r_out, g_out],
        out_shape=[r_shape, g_shape],
        scratch_shapes=r_scratch + g_scratch,
        compiler_params=_cparams(("parallel", "parallel", "arbitrary"), 48),
        name="mixer_bwd" if reverse else "mixer_fwd",
    )(*r_args, *g_args)


OP_TM = 512


def _out_proj_kernel(mr_ref, mg_ref, w0_ref, w1_ref, x_ref, n2_ref, rh_ref, rl_ref, h_ref, xn_ref, aff_ref,
                     hs_ref):
    s = pl.program_id(0)
    slot = s % 2

    @pl.when(s == 0)
    def _():
        hs_ref[1] = jnp.zeros(hs_ref.shape[1:], F32)

    hp = hs_ref[1 - slot]
    ms = jnp.mean(hp * hp, axis=-1, keepdims=True)
    xn = hp * lax.rsqrt(ms + EPS) * n2_ref[...]
    xh = xn.astype(BF16)
    xn_ref[...] = _pack_bf16_pairs(xn)
    xl = (xn - xh.astype(F32)).astype(BF16)
    lt = _dot_nt(rh_ref[...], xh) + _dot_nt(rh_ref[...], xl) + _dot_nt(rl_ref[...], xh)
    m = jnp.max(lt, axis=0, keepdims=True)
    e = jnp.exp(lt - m)
    aff_ref[...] = e / jnp.sum(e, axis=0, keepdims=True)

    h = x_ref[...] + _dot(mr_ref[...], w0_ref[...]) + _dot(mg_ref[...], w1_ref[...])
    h_ref[...] = h
    hs_ref[slot] = h


def _out_proj(mix_r, mix_g, w_out, x2d, n2w, r_hi, r_lo):
    T = x2d.shape[0]
    tm = OP_TM
    half = RET_WIDTH
    nblk = T // tm

    def head(s):
        return jnp.minimum(s, nblk - 1)

    def tail(s):
        return jnp.maximum(s - 1, 0)

    return pl.pallas_call(
        _out_proj_kernel,
        grid=(nblk + 1,),
        in_specs=[
            pl.BlockSpec((tm, half), lambda s: (head(s), 0)),
            pl.BlockSpec((tm, half), lambda s: (head(s), 0)),
            pl.BlockSpec((half, D_MODEL), lambda s: (0, 0)),
            pl.BlockSpec((half, D_MODEL), lambda s: (1, 0)),
            pl.BlockSpec((tm, D_MODEL), lambda s: (head(s), 0)),
            pl.BlockSpec((1, D_MODEL), lambda s: (0, 0)),
            pl.BlockSpec((N_EXPERTS, D_MODEL), lambda s: (0, 0)),
            pl.BlockSpec((N_EXPERTS, D_MODEL), lambda s: (0, 0)),
        ],
        out_specs=[
            pl.BlockSpec((tm, D_MODEL), lambda s: (head(s), 0)),
            pl.BlockSpec((tm, D_MODEL // 2), lambda s: (tail(s), 0)),
            pl.BlockSpec((N_EXPERTS, tm), lambda s: (0, tail(s))),
        ],
        out_shape=[
            jax.ShapeDtypeStruct((T, D_MODEL), F32),
            jax.ShapeDtypeStruct((T, D_MODEL // 2), I32),
            jax.ShapeDtypeStruct((N_EXPERTS, T), F32),
        ],
        scratch_shapes=[pltpu.VMEM((2, tm, D_MODEL), F32)],
        compiler_params=_cparams(("arbitrary",), 60),
        name="out_proj",
    )(mix_r, mix_g, w_out, w_out, x2d, n2w, r_hi, r_lo)


def _select_kernel(a_ref, pos_ref, posb_ref, rankb_ref, affb_ref, *, cap):
    E, T = a_ref.shape
    tt = MOE_TT

    def count(pred):
        return jnp.sum(pred.astype(F32), axis=1, keepdims=True)

    def bisect(i, tau):
        cand = tau | jnp.left_shift(jnp.int32(1), 30 - i)
        bits = pltpu.bitcast(a_ref[...], I32)
        return jnp.where(count(bits >= cand) >= cap, cand, tau)

    tau = lax.fori_loop(0, 31, bisect, jnp.zeros((E, 1), I32))
    bits_all = pltpu.bitcast(a_ref[...], I32)
    quota = cap - count(bits_all > tau)

    before = (lax.broadcasted_iota(I32, (tt, tt), 0) < lax.broadcasted_iota(I32, (tt, tt), 1)).astype(BF16)
    below = (lax.broadcasted_iota(I32, (E, E), 1) < lax.broadcasted_iota(I32, (E, E), 0)).astype(BF16)

    def block(j, carry):
        c_eq, c_sel = carry
        off = pl.multiple_of(j * tt, tt)
        aff = a_ref[:, pl.ds(off, tt)]
        bits = pltpu.bitcast(aff, I32)
        eq = bits == tau
        eqf = eq.astype(F32)
        rank_eq = _dot(eqf.astype(BF16), before) + c_eq
        sel = (bits > tau) | (eq & (rank_eq < quota))
        self_ = sel.astype(F32)
        selb = self_.astype(BF16)
        slot = _dot(selb, before) + c_sel
        pos = jnp.where(sel, slot, -1.0).astype(I32)
        pos_ref[:, pl.ds(off, tt)] = pos
        per_tok = jnp.broadcast_to(jnp.sum(self_, axis=0, keepdims=True), (E, tt))
        rank = _dot(per_tok.astype(BF16), before) + _dot(below, selb)
        posb_ref[j] = pos
        rankb_ref[j] = jnp.where(sel, rank, -1.0).astype(I32)
        affb_ref[j] = aff
        return (c_eq + jnp.sum(eqf, axis=1, keepdims=True), c_sel + jnp.sum(self_, axis=1, keepdims=True))

    zero = jnp.zeros((E, 1), F32)
    lax.fori_loop(0, T // tt, block, (zero, zero))


def _select(aff, cap):
    E, T = aff.shape
    nb = T // MOE_TT
    blk = jax.ShapeDtypeStruct((nb, E, MOE_TT), I32)
    return pl.pallas_call(
        functools.partial(_select_kernel, cap=cap),
        out_shape=[jax.ShapeDtypeStruct((E, T), I32), blk, blk, jax.ShapeDtypeStruct((nb, E, MOE_TT), F32)],
        compiler_params=pltpu.CompilerParams(vmem_limit_bytes=40 * 1024 * 1024),
        name="select",
    )(aff)


def _regroup_rows(T):
    nb = T // MOE_TT
    rows = CAPACITY_FACTOR * T + SC_GATHER_ROWS * nb
    return -(-rows // MOE_TW) * MOE_TW


def _combine_schedule(posb, T):
    nb = posb.shape[0]
    tw, g = MOE_TW, SC_GATHER_ROWS
    n = jnp.sum((posb >= 0).reshape(nb, -1), axis=1).astype(I32)
    seg = (n + g - 1) // g * g
    hi = jnp.cumsum(seg)
    lo = hi - seg
    off = jnp.concatenate([jnp.zeros((1,), I32), hi])
    nwin_max = (N_EXPERTS * MOE_TT) // tw + 1
    w0 = lo // tw
    w1 = jnp.where(seg > 0, (hi - 1) // tw, w0)
    cand = jnp.arange(nwin_max, dtype=I32)
    win = w0[:, None] + cand[None, :]
    valid = (win <= w1[:, None]).reshape(-1)
    nwin_total = _regroup_rows(T) // tw
    pmax = nb + nwin_total
    jv = jnp.broadcast_to(jnp.arange(nb, dtype=I32)[:, None], win.shape).reshape(-1)
    wv = jnp.minimum(win, nwin_total - 1).reshape(-1)
    total = jnp.sum(valid.astype(I32))
    dst = jnp.where(valid, jnp.cumsum(valid.astype(I32)) - 1, pmax)
    pj, pw = (jnp.zeros((pmax,), I32).at[dst].set(a, mode="drop") for a in (jv, wv))
    real = jnp.arange(pmax, dtype=I32) < total
    pj, pw = (jnp.where(real, a, a[total - 1]) for a in (pj, pw))
    first = jnp.concatenate([jnp.ones((1,), bool), pj[1:] != pj[:-1]])
    last = jnp.concatenate([pj[1:] != pj[:-1], jnp.ones((1,), bool)]) | (jnp.arange(pmax, dtype=I32) == total - 1)
    flag = jnp.where(real, first.astype(I32) + 2 * last.astype(I32) + 4, 0)
    return off, (pj, pw, flag, lo[pj], hi[pj])


SC_LANES = 16
SC_CORES = 2
SC_SUBCORES = 16
SC_GATHER_ROWS = 32


def _dispatch(pos, xw, cap):
    E, T = pos.shape
    W = xw.shape[1]
    G = SC_GATHER_ROWS
    part_rows = cap // SC_CORES
    mesh = plsc.VectorSubcoreMesh(core_axis_name="c", subcore_axis_name="s")

    @pl.kernel(
        out_type=jax.ShapeDtypeStruct((E * cap, W), I32),
        mesh=mesh,
        scratch_types=[pltpu.VMEM((T,), I32), pltpu.VMEM((cap,), I32), pltpu.VMEM((G, W), I32)],
        compiler_params=pltpu.CompilerParams(needs_layout_passes=False),
        name="sc_dispatch",
    )
    def run(pos_hbm, x_hbm, xe_hbm, pos_v, idx_v, buf):
        e = lax.axis_index("s")
        part = lax.axis_index("c")
        pltpu.sync_copy(pos_hbm.at[e], pos_v)
        lane = lax.iota(I32, SC_LANES)

        @pl.loop(0, T // SC_LANES)
        def _(i):
            off = pl.multiple_of(i * SC_LANES, SC_LANES)
            p = pos_v[pl.ds(off, SC_LANES)]
            plsc.store_scatter(idx_v, [p], lane + off, mask=p >= 0)

        @pl.loop(0, part_rows // G)
        def _(g):
            o = pl.multiple_of(part * part_rows + g * G, G)
            pltpu.sync_copy(x_hbm.at[idx_v.at[pl.ds(o, G)]], buf)
            pltpu.sync_copy(buf, xe_hbm.at[pl.ds(e * cap + o, G)])

    return run(pos, xw)


FFN_TM = 2048
FFN_UNPACK_ROWS = 256


def _ffn_up_width(cap):
    return 512 if min(FFN_TM, cap) <= 1024 else 256


def _ffn_tile_width(cap):
    return 512


def _ffn_kernel(x_ref, wg_ref, wu_ref, wd_ref, *rest, tw):
    o_ref, xb_ref, hid_ref = rest[-3:]
    s = pl.program_id(2)
    n_up = D_FF // tw

    @pl.when(s == 0)
    def _():
        half = D_MODEL // 2

        def unpack_rows(i, carry):
            r = pl.multiple_of(i * FFN_UNPACK_ROWS, FFN_UNPACK_ROWS)
            lo, hi = _unpack_bf16_pairs(x_ref[pl.ds(r, FFN_UNPACK_ROWS), :])
            xb_ref[pl.ds(r, FFN_UNPACK_ROWS), :half] = lo.astype(BF16)
            xb_ref[pl.ds(r, FFN_UNPACK_ROWS), half:] = hi.astype(BF16)
            return carry

        lax.fori_loop(0, x_ref.shape[0] // FFN_UNPACK_ROWS, unpack_rows, 0)

    @pl.when(s < n_up)
    def _():
        x = xb_ref[...]
        g = _dot(x, wg_ref[...].astype(BF16))
        u = _dot(x, wu_ref[...].astype(BF16))
        col = pl.multiple_of(s * tw, tw)
        hid_ref[:, pl.ds(col, tw)] = (_silu(g) * u).astype(BF16)

    @pl.when(s >= n_up)
    def _():
        o_ref[...] = _pack_bf16_pairs(_dot(hid_ref[...], wd_ref[...].astype(BF16)))


def _ffn(xe, w_gate, w_up, w_down, experts=None, earlier=None):
    E, cap, _ = xe.shape
    e0, e1 = experts or (0, E)
    tm = min(FFN_TM, cap)
    tw, tg = _ffn_up_width(cap), _ffn_tile_width(cap)
    n_up, n_down = D_FF // tw, D_MODEL // tg

    def up(e, m, s):
        return (e + e0, 0, jnp.minimum(s, n_up - 1))

    def down(s):
        return jnp.maximum(s - n_up, 0)

    in_specs = [
        pl.BlockSpec((None, tm, D_MODEL // 2), lambda e, m, s: (e + e0, m, 0)),
        pl.BlockSpec((None, D_MODEL, tw), up),
        pl.BlockSpec((None, D_MODEL, tw), up),
        pl.BlockSpec((None, D_FF, tg), lambda e, m, s: (e + e0, 0, down(s))),
    ]
    args = [xe, w_gate, w_up, w_down]
    aliases = {}
    if earlier is not None:
        in_specs.append(pl.BlockSpec(memory_space=pl.ANY))
        args.append(earlier)
        aliases = {len(args) - 1: 0}
    return pl.pallas_call(
        functools.partial(_ffn_kernel, tw=tw),
        grid=(e1 - e0, cap // tm, n_up + n_down),
        in_specs=in_specs,
        out_specs=pl.BlockSpec((None, tm, tg // 2), lambda e, m, s: (e + e0, m, down(s))),
        out_shape=jax.ShapeDtypeStruct((E, cap, D_MODEL // 2), I32),
        scratch_shapes=[pltpu.VMEM((tm, D_MODEL), BF16), pltpu.VMEM((tm, D_FF), BF16)],
        input_output_aliases=aliases,
        compiler_params=_cparams(("parallel", "parallel", "arbitrary"), 60),
        name="ffn",
    )(*args)


def _regroup(posb, rankb, affb, off, yw, cap, rows):
    NB, EB = posb.shape
    W = yw.shape[1]
    G, L, tt = SC_GATHER_ROWS, SC_LANES, MOE_TT
    n_workers = SC_CORES * SC_SUBCORES
    split = max(1, n_workers // NB)
    per = -(-NB * split // n_workers)
    mesh = plsc.VectorSubcoreMesh(core_axis_name="c", subcore_axis_name="s")
    off_pad = jnp.pad(off, (0, L))

    @pl.kernel(
        out_type=(jax.ShapeDtypeStruct((rows, W), I32), jax.ShapeDtypeStruct((rows,), I32),
                  jax.ShapeDtypeStruct((rows,), F32)),
        mesh=mesh,
        scratch_types=[pltpu.VMEM((EB,), I32), pltpu.VMEM((EB,), I32), pltpu.VMEM((EB,), F32),
                       pltpu.VMEM((EB,), I32), pltpu.VMEM((EB,), I32), pltpu.VMEM((EB,), F32),
                       pltpu.VMEM((G, W), I32), pltpu.VMEM((NB + 1 + L,), I32)],
        compiler_params=pltpu.CompilerParams(needs_layout_passes=False),
        name="sc_regroup",
    )
    def run(posb_hbm, rankb_hbm, affb_hbm, off_hbm, y_hbm, yg_hbm, tok_hbm, gate_hbm,
            pos_v, rank_v, aff_v, src_v, tok_v, gate_v, buf, off_v):
        wid = lax.axis_index("c") * SC_SUBCORES + lax.axis_index("s")
        pltpu.sync_copy(off_hbm, off_v)
        lane = lax.iota(I32, L)
        zi = jnp.zeros((L,), I32)
        zf = jnp.zeros((L,), F32)

        def regroup_block(j, part):
            pltpu.sync_copy(posb_hbm.at[j], pos_v)
            pltpu.sync_copy(rankb_hbm.at[j], rank_v)
            pltpu.sync_copy(affb_hbm.at[j], aff_v)
            lo = jnp.max(plsc.load_gather(off_v, [zi + j]))
            hi = jnp.max(plsc.load_gather(off_v, [zi + j + 1]))

            @pl.loop(0, EB // L)
            def _(i):
                o = pl.multiple_of(i * L, L)
                src_v[pl.ds(o, L)] = zi
                tok_v[pl.ds(o, L)] = zi
                gate_v[pl.ds(o, L)] = zf

            @pl.loop(0, EB // L)
            def _(i):
                o = pl.multiple_of(i * L, L)
                p = pos_v[pl.ds(o, L)]
                r = rank_v[pl.ds(o, L)]
                m = p >= 0
                e = i // (tt // L)
                t0 = j * tt + (i % (tt // L)) * L
                plsc.store_scatter(src_v, [r], p + e * cap, mask=m)
                plsc.store_scatter(tok_v, [r], lane + t0, mask=m)
                plsc.store_scatter(gate_v, [r], aff_v[pl.ds(o, L)], mask=m)

            @pl.loop(part, (hi - lo) // G, step=split)
            def _(g):
                o = pl.multiple_of(g * G, G)
                dst = pl.multiple_of(lo + o, G)
                pltpu.sync_copy(y_hbm.at[src_v.at[pl.ds(o, G)]], buf)
                pltpu.sync_copy(buf, yg_hbm.at[pl.ds(dst, G)])
                pltpu.sync_copy(tok_v.at[pl.ds(o, G)], tok_hbm.at[pl.ds(dst, G)])
                pltpu.sync_copy(gate_v.at[pl.ds(o, G)], gate_hbm.at[pl.ds(dst, G)])

        @pl.loop(0, per)
        def _(k):
            unit = wid * per + k

            @pl.when(unit < NB * split)
            def _():
                regroup_block(unit // split, unit % split)

    return run(posb, rankb, affb, off_pad, yw)


def _combine_kernel(pj_ref, pw_ref, pf_ref, plo_ref, phi_ref, tok_ref, gate_ref, yg_ref, h_ref, nf_ref,
                    o_ref, acc_ref, *, group):
    p = pl.program_id(0)
    flag = pf_ref[p]
    tw, tt = MOE_TW, MOE_TT
    half = D_MODEL // 2

    hw = group // 2

    def col_blocks():
        for n in range(D_MODEL // group):
            yield slice(n * hw, (n + 1) * hw), slice(n * group, n * group + hw)
            yield slice(half + n * hw, half + (n + 1) * hw), slice(n * group + hw, (n + 1) * group)

    @pl.when((flag & 1) != 0)
    def _():
        for packed, natural in col_blocks():
            acc_ref[:, packed] = h_ref[:, natural]

    @pl.when((flag & 4) != 0)
    def _():
        lo, hi = plo_ref[p], phi_ref[p]
        row0 = pw_ref[p] * tw
        rid = lax.broadcasted_iota(I32, (tw, 1), 0) + row0
        keep = (rid >= lo) & (rid < hi)
        y_lo, y_hi = _unpack_bf16_pairs(yg_ref[...])
        y_lo = jnp.where(keep, y_lo, 0.0).astype(BF16)
        y_hi = jnp.where(keep, y_hi, 0.0).astype(BF16)
        tid = lax.broadcasted_iota(I32, (tt, tw), 0) + pj_ref[p] * tt
        cid = lax.broadcasted_iota(I32, (tt, tw), 1) + row0
        hit = (tok_ref[...] == tid) & (cid >= lo) & (cid < hi)
        weights = jnp.where(hit, gate_ref[...], 0.0).astype(BF16)
        acc_ref[:, :half] += _dot(weights, y_lo)
        acc_ref[:, half:] += _dot(weights, y_hi)

    @pl.when((flag & 2) != 0)
    def _():
        y = acc_ref[...]
        scale = lax.rsqrt(jnp.mean(y * y, axis=-1, keepdims=True) + EPS)
        for packed, natural in col_blocks():
            o_ref[:, natural] = acc_ref[:, packed] * scale * nf_ref[:, natural]


def _combine(lists, tok, gate, yg, h, nfw, group):
    pj, pw, pf, plo, phi = lists
    T = h.shape[0]
    tw, tt = MOE_TW, MOE_TT
    nwin = yg.shape[0] // tw
    grid_spec = pltpu.PrefetchScalarGridSpec(
        num_scalar_prefetch=5,
        grid=(pj.shape[0],),
        in_specs=[
            pl.BlockSpec((None, 1, tw), lambda p, pj, pw, *_: (pw[p], 0, 0)),
            pl.BlockSpec((None, 1, tw), lambda p, pj, pw, *_: (pw[p], 0, 0)),
            pl.BlockSpec((tw, D_MODEL // 2), lambda p, pj, pw, *_: (pw[p], 0)),
            pl.BlockSpec((tt, D_MODEL), lambda p, pj, pw, *_: (pj[p], 0)),
            pl.BlockSpec((1, D_MODEL), lambda p, pj, pw, *_: (0, 0)),
        ],
        out_specs=pl.BlockSpec((tt, D_MODEL), lambda p, pj, pw, *_: (pj[p], 0)),
        scratch_shapes=[pltpu.VMEM((tt, D_MODEL), F32)],
    )
    return pl.pallas_call(
        functools.partial(_combine_kernel, group=group),
        grid_spec=grid_spec,
        out_shape=jax.ShapeDtypeStruct((T, D_MODEL), F32),
        compiler_params=_cparams(("arbitrary",), 48),
        name="combine",
    )(pj, pw, pf, plo, phi, tok.reshape(nwin, 1, tw), gate.reshape(nwin, 1, tw), yg, h, nfw)


def _rope_tables(seq_len):
    d = RET_DK
    inv = ROPE_BASE ** (-jnp.arange(0, d, 2, dtype=F32) / d)
    ang = jnp.arange(seq_len, dtype=F32)[:, None] * inv[None, :]
    return jnp.cos(ang), jnp.sin(ang)


def _chunk_tri(n, chunk, upper):
    r = np.arange(n)
    same = (r[:, None] // chunk) == (r[None, :] // chunk)
    tri = (r[:, None] <= r[None, :]) if upper else (r[:, None] >= r[None, :])
    return jnp.asarray(same & tri, BF16)


def _prep_params(norm1_w, w_in, ret_gn_w, gla_gate_up, gla_gate_bias, gla_gn_w, w_out, norm2_w, router_w,
                 normf_w):
    w = w_in[0]
    w_main = w[:, :IN_MAIN].astype(BF16)
    w_ga = jnp.pad(w[:, IN_MAIN:], ((0, 0), (0, LANE - 2 * GLA_RANK))).astype(BF16)
    cs = np.ones((1, IN_MAIN), np.float32)
    cs[:, _RQ:_RQ + RET_WIDTH] = RET_DK ** -0.5
    cs[:, _GQ:_GQ + GLA_KEY_WIDTH] = GLA_DK ** -0.5
    up = gla_gate_up[0].astype(F32)
    up_pad = jnp.zeros((LANE, 2 * GLA_KEY_WIDTH), F32)
    up_pad = up_pad.at[:GLA_RANK, :GLA_KEY_WIDTH].set(up[0])
    up_pad = up_pad.at[GLA_RANK:2 * GLA_RANK, GLA_KEY_WIDTH:].set(up[1])
    rt = router_w[0].T.astype(F32)
    r_hi = rt.astype(BF16)
    r_lo = (rt - r_hi.astype(F32)).astype(BF16)
    return dict(
        n1w=norm1_w[0].reshape(1, D_MODEL).astype(F32),
        w_main=w_main, w_ga=w_ga, colscale=jnp.asarray(cs),
        up_pad=up_pad.astype(BF16),
        bias=gla_gate_bias[0].reshape(1, 2 * GLA_KEY_WIDTH).astype(F32),
        lf=_chunk_tri(GATE_TM, GLA_CHUNK, upper=False),
        lb=_chunk_tri(GATE_TM, GLA_CHUNK, upper=True),
        ret_gn=ret_gn_w[0].reshape(1, RET_WIDTH).astype(F32),
        gla_gn=gla_gn_w[0].reshape(1, GLA_WIDTH).astype(F32),
        w_out=w_out[0].astype(BF16),
        n2w=norm2_w[0].reshape(1, D_MODEL).astype(F32),
        r_hi=r_hi, r_lo=r_lo,
        nfw=normf_w.reshape(1, D_MODEL).astype(F32),
    )


def _trunk_route(x, pp, decay_logit):
    B, L, _ = x.shape
    T = B * L
    x2d = x.reshape(T, D_MODEL)
    cos, sin = _rope_tables(L)
    proj, ga = _in_proj(x2d, pp["n1w"], pp["w_main"], pp["w_ga"], pp["colscale"], cos, sin, L)
    b_f, b_b = _gla_gates(ga, pp["up_pad"], pp["bias"], pp["lf"], pp["lb"])

    fwd = _mixer_scan(proj, decay_logit, b_f, B, L, reverse=False)
    mix_r, mix_g = _mixer_scan(proj, decay_logit, b_b, B, L, reverse=True, o_fwd=fwd,
                               gn_w=(pp["ret_gn"], pp["gla_gn"]))

    h, xn2, aff = _out_proj(mix_r, mix_g, pp["w_out"], x2d, pp["n2w"], pp["r_hi"], pp["r_lo"])

    cap = CAPACITY_FACTOR * T // N_EXPERTS
    pos, posb, rankb, affb = _select(aff, cap)
    off, c_lists = _combine_schedule(posb, T)
    xe = _dispatch(pos, xn2, cap).reshape(N_EXPERTS, cap, D_MODEL // 2)
    return dict(xe=xe, h=h, posb=posb, rankb=rankb, affb=affb, off=off, c_lists=c_lists)


def _trunk_regroup(route, ye):
    T = route["h"].shape[0]
    cap = CAPACITY_FACTOR * T // N_EXPERTS
    nb = T // MOE_TT
    flat = lambda a: a.reshape(nb, -1)
    yg, tok, gate = _regroup(flat(route["posb"]), flat(route["rankb"]), flat(route["affb"]), route["off"],
                             ye.reshape(N_EXPERTS * cap, D_MODEL // 2), cap, _regroup_rows(T))
    return route["c_lists"], tok, gate, yg, route["h"]


def _trunk_back(front, nfw, shape):
    c_lists, tok, gate, yg, h = front
    cap = CAPACITY_FACTOR * h.shape[0] // N_EXPERTS
    return _combine(c_lists, tok, gate, yg, h, nfw, _ffn_tile_width(cap)).reshape(shape)


def kernel(x_prompt, x_sample, norm1_w, w_in, ret_decay_logit, ret_gn_w, gla_gate_up, gla_gate_bias,
           gla_gn_w, w_out, norm2_w, router_w, w_gate, w_up, w_down, normf_w):
    pp = _prep_params(norm1_w, w_in, ret_gn_w, gla_gate_up, gla_gate_bias, gla_gn_w, w_out, norm2_w,
                      router_w, normf_w)
    decay_logit = ret_decay_logit[0].astype(F32)
    weights = (w_gate[0], w_up[0], w_down[0])
    route_p = _trunk_route(x_prompt, pp, decay_logit)
    route_s = _trunk_route(x_sample, pp, decay_logit)
    half = N_EXPERTS // 2
    ye_p = _ffn(route_p["xe"], *weights, experts=(0, half))
    ye_s = _ffn(route_s["xe"], *weights)
    xe_p, ye_p, ye_s = lax.optimization_barrier((route_p["xe"], ye_p, ye_s))
    ye_p = _ffn(xe_p, *weights, experts=(half, N_EXPERTS), earlier=ye_p)
    front_s = _trunk_regroup(route_s, ye_s)
    front_p = _trunk_regroup(route_p, ye_p)
    y_sample = _trunk_back(front_s, pp["nfw"], x_sample.shape)
    front_p, y_sample = lax.optimization_barrier((front_p, y_sample))
    y_prompt = _trunk_back(front_p, pp["nfw"], x_prompt.shape)
    return (y_prompt, y_sample)
```

```python
import functools

import numpy as np
import jax
import jax.numpy as jnp
from jax import lax
from jax.experimental import pallas as pl
from jax.experimental.pallas import tpu as pltpu
from jax.experimental.pallas import tpu_sc as plsc

F32, BF16, I32 = jnp.float32, jnp.bfloat16, jnp.int32

D_MODEL = 2048
RET_WIDTH = 1024
RET_HEADS = 4
RET_DK = 256
RET_DV = 256
GLA_WIDTH = 1024
GLA_HEADS = 4
GLA_DK = 128
GLA_DV = 256
GLA_KEY_WIDTH = 512
GLA_RANK = 16
GLA_TAU = 16.0
RET_CHUNK = 256
GLA_CHUNK = 64
ROPE_BASE = 10000.0
N_EXPERTS = 16
CAPACITY_FACTOR = 2
D_FF = 2048
EPS = 1e-6
LOG2_E = 1.4426950408889634
IN_MAIN = 4 * RET_WIDTH + 2 * GLA_KEY_WIDTH + 2 * GLA_WIDTH

_RQ, _RK, _RV, _RG = 0, 1024, 2048, 3072
_GQ, _GK, _GV, _GG = 4096, 4608, 5120, 6144

LANE = 128
MOE_TT = 512
MOE_TW = 512


def _cparams(sem, vmem_mb):
    return pltpu.CompilerParams(dimension_semantics=sem, vmem_limit_bytes=vmem_mb * 1024 * 1024)


def _log_sigmoid(z):
    return jnp.minimum(z, 0.0) - jnp.log(1.0 + jnp.exp(-jnp.abs(z)))


def _silu(g):
    return g * (1.0 / (1.0 + jnp.exp(-g)))


def _dot_nt(a, b):
    return lax.dot_general(a, b, (((1,), (1,)), ((), ())), preferred_element_type=F32)


def _dot_tn(a, b):
    return lax.dot_general(a, b, (((0,), (0,)), ((), ())), preferred_element_type=F32)


def _dot(a, b):
    return jnp.dot(a, b, preferred_element_type=F32)


def _pack_bf16_pairs(x):
    bits = pltpu.bitcast(x.astype(BF16).astype(F32), I32)
    w = x.shape[1] // 2
    return bits[:, w:] | lax.shift_right_logical(bits[:, :w], 16)


def _unpack_bf16_pairs(words):
    lo = pltpu.bitcast(lax.shift_left(words, 16), F32)
    hi = pltpu.bitcast(words & jnp.int32(-65536), F32)
    return lo, hi


IP_TM = 1024
IP_TN = 1792


def _in_proj_kernel(x_ref, n1_ref, w_ref, wga_ref, cs_ref, cos_ref, sin_ref, o_ref, ga_ref, xn_ref):
    j = pl.program_id(1)

    @pl.when(j == 0)
    def _():
        x = x_ref[...]
        ms = jnp.mean(x * x, axis=-1, keepdims=True)
        xn = (x * lax.rsqrt(ms + EPS) * n1_ref[...]).astype(BF16)
        xn_ref[...] = xn
        ga_ref[...] = _dot(xn, wga_ref[...])

    acc = _dot(xn_ref[...], w_ref[...]) * cs_ref[...]
    heads_per_block = IP_TN // RET_DK
    n_rope_heads = 2 * RET_WIDTH // RET_DK

    def store(n_rotary):
        for h in range(heads_per_block):
            x1 = acc[:, 2 * h * LANE:(2 * h + 1) * LANE]
            x2 = acc[:, (2 * h + 1) * LANE:(2 * h + 2) * LANE]
            if h < n_rotary:
                cos = cos_ref[...]
                sin = sin_ref[...]
                o_ref[2 * h] = (x1 * cos - x2 * sin).astype(BF16)
                o_ref[2 * h + 1] = (x1 * sin + x2 * cos).astype(BF16)
            else:
                o_ref[2 * h] = x1.astype(BF16)
                o_ref[2 * h + 1] = x2.astype(BF16)

    for blk in range(IN_MAIN // IP_TN):
        n_rotary = min(max(n_rope_heads - blk * heads_per_block, 0), heads_per_block)
        if n_rotary > 0:
            pl.when(j == blk)(functools.partial(store, n_rotary))
        else:
            pl.when(j >= blk)(functools.partial(store, 0))
            break


def _in_proj(x2d, n1w, w_main, w_ga, colscale, cos, sin, seq_len):
    T = x2d.shape[0]
    tm, tn = IP_TM, IP_TN
    nlb = seq_len // tm
    return pl.pallas_call(
        _in_proj_kernel,
        grid=(T // tm, IN_MAIN // tn),
        in_specs=[
            pl.BlockSpec((tm, D_MODEL), lambda i, j: (i, 0)),
            pl.BlockSpec((1, D_MODEL), lambda i, j: (0, 0)),
            pl.BlockSpec((D_MODEL, tn), lambda i, j: (0, j)),
            pl.BlockSpec((D_MODEL, LANE), lambda i, j: (0, 0)),
            pl.BlockSpec((1, tn), lambda i, j: (0, j)),
            pl.BlockSpec((tm, LANE), lambda i, j: (i % nlb, 0)),
            pl.BlockSpec((tm, LANE), lambda i, j: (i % nlb, 0)),
        ],
        out_specs=[
            pl.BlockSpec((tn // LANE, tm, LANE), lambda i, j: (j, i, 0)),
            pl.BlockSpec((tm, LANE), lambda i, j: (i, 0)),
        ],
        out_shape=[
            jax.ShapeDtypeStruct((IN_MAIN // LANE, T, LANE), BF16),
            jax.ShapeDtypeStruct((T, LANE), F32),
        ],
        scratch_shapes=[pltpu.VMEM((tm, D_MODEL), BF16)],
        compiler_params=_cparams(("parallel", "arbitrary"), 60),
        name="in_proj",
    )(x2d, n1w, w_main, w_ga, colscale, cos, sin)


GATE_TM = 512
GATE_SUB = 256


def _gates_kernel(ga_ref, up_ref, bias_ref, lf_ref, lb_ref, bf_ref, bb_ref):
    z = _dot(ga_ref[...].astype(BF16), up_ref[...]) + bias_ref[...]
    la = _log_sigmoid(z) * (LOG2_E / GLA_TAU)
    hi = la.astype(BF16)
    lo = (la - hi.astype(F32)).astype(BF16)
    kw = GLA_KEY_WIDTH
    for r in range(GATE_TM // GATE_SUB):
        rows = slice(r * GATE_SUB, (r + 1) * GATE_SUB)
        bf_ref[rows, :] = _dot(lf_ref[...], hi[rows, :kw]) + _dot(lf_ref[...], lo[rows, :kw])
        bb_ref[rows, :] = _dot(lb_ref[...], hi[rows, kw:]) + _dot(lb_ref[...], lo[rows, kw:])


def _gla_gates(ga, up_pad, bias, lf, lb):
    T = ga.shape[0]
    tm = GATE_TM
    kw = GLA_KEY_WIDTH
    return pl.pallas_call(
        _gates_kernel,
        grid=(T // tm,),
        in_specs=[
            pl.BlockSpec((tm, LANE), lambda i: (i, 0)),
            pl.BlockSpec((LANE, 2 * kw), lambda i: (0, 0)),
            pl.BlockSpec((1, 2 * kw), lambda i: (0, 0)),
            pl.BlockSpec((GATE_SUB, GATE_SUB), lambda i: (0, 0)),
            pl.BlockSpec((GATE_SUB, GATE_SUB), lambda i: (0, 0)),
        ],
        out_specs=[pl.BlockSpec((tm, kw), lambda i: (i, 0)), pl.BlockSpec((tm, kw), lambda i: (i, 0))],
        out_shape=[jax.ShapeDtypeStruct((T, kw), F32), jax.ShapeDtypeStruct((T, kw), F32)],
        compiler_params=_cparams(("parallel",), 32),
        name="gla_gates",
    )(ga, up_pad, bias, lf, lb)


def _wide(ref, rows):
    return jnp.concatenate([ref[0, rows, :], ref[1, rows, :]], axis=1)


def _finish_heads(tot, gn, gate):
    ms = jnp.mean(tot * tot, axis=-1, keepdims=True)
    yn = tot * lax.rsqrt(ms + EPS) * gn
    return (yn * _silu(gate.astype(F32))).astype(BF16)


RET_TB = 1024


def _ret_kernel(dl_ref, q_ref, k_ref, v_ref, *rest, reverse):
    if reverse:
        g_ref, of_ref, gn_ref, o_ref, s_ref, intra_ref, qd_ref, kd_ref, cd_ref, p_ref, u_ref = rest
    else:
        o_ref, s_ref, intra_ref, qd_ref, kd_ref, cd_ref, p_ref, u_ref = rest
    h = pl.program_id(1)
    n = pl.program_id(2)
    C = RET_CHUNK

    @pl.when(n == 0)
    def _():
        s_ref[...] = jnp.zeros_like(s_ref)
        logit = dl_ref[1 if reverse else 0, h]
        lg = _log_sigmoid(jnp.full((C, RET_DV), logit, F32))
        lg_c = _log_sigmoid(jnp.full((C, C), logit, F32))
        lg_r = _log_sigmoid(jnp.full((1, RET_DV), logit, F32))
        ri = lax.broadcasted_iota(I32, (C, RET_DV), 0).astype(F32)
        rc = lax.broadcasted_iota(I32, (C, C), 0).astype(F32)
        cc = lax.broadcasted_iota(I32, (C, C), 1).astype(F32)
        diff = (cc - rc) if reverse else (rc - cc)
        intra_ref[...] = jnp.where(diff >= 0, jnp.exp(lg_c * diff), 0.0)
        if reverse:
            qd_ref[...] = jnp.exp(lg * (C - ri))
            kd_ref[...] = jnp.exp(lg * ri)
        else:
            qd_ref[...] = jnp.exp(lg * (ri + 1.0))
            kd_ref[...] = jnp.exp(lg * (C - 1.0 - ri))
        cd_ref[...] = jnp.exp(lg_r * C)

    nchunks = o_ref.shape[0] // C
    for c in range(nchunks):
        rows = slice(c * C, (c + 1) * C)
        k = _wide(k_ref, rows)
        p_ref[c] = (_dot_nt(_wide(q_ref, rows), k) * intra_ref[...]).astype(BF16)
        kd = (k.astype(F32) * kd_ref[...]).astype(BF16)
        u_ref[c] = _dot_tn(kd, _wide(v_ref, rows))
    order = range(nchunks - 1, -1, -1) if reverse else range(nchunks)
    for c in order:
        rows = slice(c * C, (c + 1) * C)
        state = s_ref[...]
        o = _dot(p_ref[c], _wide(v_ref, rows)) + _dot(_wide(q_ref, rows), state.astype(BF16)) * qd_ref[...]
        s_ref[...] = state * cd_ref[...] + u_ref[c]
        if reverse:
            tot = of_ref[rows, :].astype(F32) + o
            o_ref[rows, :] = _finish_heads(tot, gn_ref[...], _wide(g_ref, rows))
        else:
            o_ref[rows, :] = o.astype(BF16)


def _ret_parts(proj, decay_logit, rb, reverse, o_fwd=None, gn_w=None):
    T = proj.shape[1]
    tb = RET_TB
    dk, dv, C = RET_DK, RET_DV, RET_CHUNK

    def head(base):
        return pl.BlockSpec((dk // LANE, tb, LANE), lambda b, h, n: (base // dk + h, rb(b, n), 0))

    in_specs = [pl.BlockSpec(memory_space=pltpu.SMEM), head(_RQ), head(_RK), head(_RV)]
    args = [decay_logit, proj, proj, proj]
    if reverse:
        in_specs += [
            head(_RG),
            pl.BlockSpec((tb, dv), lambda b, h, n: (rb(b, n), h)),
            pl.BlockSpec((1, dv), lambda b, h, n: (0, h)),
        ]
        args += [proj, o_fwd, gn_w]
    out_spec = pl.BlockSpec((tb, dv), lambda b, h, n: (rb(b, n), h))
    out_shape = jax.ShapeDtypeStruct((T, RET_WIDTH), BF16)
    scratch = [
        pltpu.VMEM((dk, dv), F32),
        pltpu.VMEM((C, C), F32),
        pltpu.VMEM((C, dv), F32),
        pltpu.VMEM((C, dk), F32),
        pltpu.VMEM((1, dv), F32),
        pltpu.VMEM((tb // C, C, C), BF16),
        pltpu.VMEM((tb // C, dk, dv), F32),
    ]
    return in_specs, args, out_spec, out_shape, scratch


GLA_TB = 1024
GLA_UNROLL = 16


GLA_LEVELS = (32, 16, 8, 4, 2, 1)
SUBLANES = 8


def _gla_tables(reverse):
    C = GLA_CHUNK
    r = np.arange(C)
    masks = np.zeros((len(GLA_LEVELS) + 1, C, C), np.float32)
    for l, s in enumerate(GLA_LEVELS):
        upper = (r & s) != 0
        same = (r[:, None] // (2 * s)) == (r[None, :] // (2 * s))
        lhs_rows = ~upper if reverse else upper
        masks[l] = same & lhs_rows[:, None] & ~lhs_rows[None, :]
    masks[-1] = np.eye(C)
    return jnp.asarray(masks, F32)


def _gla_kernel(q_ref, k_ref, v_ref, b_ref, mask_ref, *rest, reverse):
    if reverse:
        g_ref, of_ref, gn_ref, o_ref, st_ref, sc_ref = rest
    else:
        o_ref, st_ref, sc_ref = rest
    n = pl.program_id(2)
    C = GLA_CHUNK

    @pl.when(n == 0)
    def _():
        st_ref[...] = jnp.zeros_like(st_ref)

    nchunks = q_ref.shape[0] // C
    sub_row = lax.broadcasted_iota(I32, (SUBLANES, GLA_DK), 0)
    zero_rows = jnp.zeros((SUBLANES, GLA_DK), F32)

    def chunk_scores(c, carry):
        c0 = pl.multiple_of(c * C, C)
        qb = q_ref[pl.ds(c0, C), :]
        kb = k_ref[pl.ds(c0, C), :]
        q = qb.astype(F32)
        k = kb.astype(F32)
        b = b_ref[pl.ds(c0, C), :]

        def mid_row(r):
            return jnp.broadcast_to(b[r:r + 1, :], (SUBLANES, GLA_DK))

        scores = mask_ref[len(GLA_LEVELS)] * _dot_nt(qb, kb)
        for l, s in enumerate(GLA_LEVELS):
            lhs, rhs = [], []
            for g in range(C // SUBLANES):
                r0 = g * SUBLANES
                rows = slice(r0, r0 + SUBLANES)
                if s >= SUBLANES:
                    m = mid_row((r0 // (2 * s)) * (2 * s) + s)
                    is_lhs = ((r0 & s) != 0) != reverse
                    if is_lhs:
                        lhs.append(q[rows] * jnp.exp2(b[rows] - m))
                        rhs.append(zero_rows)
                    else:
                        lhs.append(zero_rows)
                        rhs.append(k[rows] * jnp.exp2(m - b[rows]))
                else:
                    m = mid_row(r0 + SUBLANES - s)
                    for blk in range(SUBLANES // (2 * s) - 2, -1, -1):
                        m = jnp.where(sub_row < (blk + 1) * 2 * s, mid_row(r0 + blk * 2 * s + s), m)
                    upper = (sub_row & s) != 0
                    is_lhs = jnp.logical_not(upper) if reverse else upper
                    lhs.append(jnp.where(is_lhs, q[rows] * jnp.exp2(b[rows] - m), 0.0))
                    rhs.append(jnp.where(is_lhs, 0.0, k[rows] * jnp.exp2(m - b[rows])))
            lhs = jnp.concatenate(lhs, axis=0).astype(BF16)
            rhs = jnp.concatenate(rhs, axis=0).astype(BF16)
            scores = scores + mask_ref[l] * _dot_nt(lhs, rhs)
        sc_ref[c] = scores.astype(BF16)
        return carry

    lax.fori_loop(0, nchunks, chunk_scores, 0, unroll=GLA_UNROLL)

    def chunk(ci, carry):
        c = (nchunks - 1 - ci) if reverse else ci
        c0 = pl.multiple_of(c * C, C)
        q = q_ref[pl.ds(c0, C), :].astype(F32)
        k = k_ref[pl.ds(c0, C), :].astype(F32)
        v = _wide(v_ref, pl.ds(c0, C))
        b = b_ref[pl.ds(c0, C), :]
        b_end = b[0:1, :] if reverse else b[C - 1:C, :]

        st = st_ref[...]
        o = _dot_nt((q * jnp.exp2(b)).astype(BF16), st.astype(BF16))
        ke = (k * jnp.exp2(b_end - b)).astype(BF16)
        st_ref[...] = st * jnp.exp2(b_end) + _dot_tn(v, ke)
        o = o + _dot(sc_ref[c], v)
        if reverse:
            tot = of_ref[pl.ds(c0, C), :].astype(F32) + o
            o_ref[pl.ds(c0, C), :] = _finish_heads(tot, gn_ref[...], _wide(g_ref, pl.ds(c0, C)))
        else:
            o_ref[pl.ds(c0, C), :] = o.astype(BF16)
        return carry

    lax.fori_loop(0, nchunks, chunk, 0, unroll=GLA_UNROLL)


def _gla_parts(proj, bcum, rb, reverse, o_fwd=None, gn_w=None):
    T = proj.shape[1]
    tb = GLA_TB
    dk, dv = GLA_DK, GLA_DV
    masks = _gla_tables(reverse)

    def key_block(base):
        return pl.BlockSpec((None, tb, LANE), lambda b, h, n: (base // dk + h, rb(b, n), 0))

    def value_block(base):
        return pl.BlockSpec((dv // LANE, tb, LANE), lambda b, h, n: (base // dv + h, rb(b, n), 0))

    in_specs = [
        key_block(_GQ), key_block(_GK), value_block(_GV),
        pl.BlockSpec((tb, dk), lambda b, h, n: (rb(b, n), h)),
        pl.BlockSpec(masks.shape, lambda b, h, n: (0, 0, 0)),
    ]
    args = [proj, proj, proj, bcum, masks]
    if reverse:
        in_specs += [
            value_block(_GG),
            pl.BlockSpec((tb, dv), lambda b, h, n: (rb(b, n), h)),
            pl.BlockSpec((1, dv), lambda b, h, n: (0, h)),
        ]
        args += [proj, o_fwd, gn_w]
    out_spec = pl.BlockSpec((tb, dv), lambda b, h, n: (rb(b, n), h))
    out_shape = jax.ShapeDtypeStruct((T, GLA_WIDTH), BF16)
    scratch = [pltpu.VMEM((dv, dk), F32), pltpu.VMEM((tb // GLA_CHUNK, GLA_CHUNK, GLA_CHUNK), BF16)]
    return in_specs, args, out_spec, out_shape, scratch


def _mixer_kernel(*refs, reverse, n_ret_in, n_gla_in, n_ret_scratch):
    ret_in = refs[:n_ret_in]
    gla_in = refs[n_ret_in:n_ret_in + n_gla_in]
    ret_out, gla_out = refs[n_ret_in + n_gla_in:n_ret_in + n_gla_in + 2]
    scratch = refs[n_ret_in + n_gla_in + 2:]
    _ret_kernel(*ret_in, ret_out, *scratch[:n_ret_scratch], reverse=reverse)
    _gla_kernel(*gla_in, gla_out, *scratch[n_ret_scratch:], reverse=reverse)


def _mixer_scan(proj, decay_logit, bcum, batch, seq_len, reverse, o_fwd=(None, None), gn_w=(None, None)):
    assert RET_TB == GLA_TB and RET_HEADS == GLA_HEADS
    nb = seq_len // RET_TB

    def rb(b, n):
        return b * nb + ((nb - 1 - n) if reverse else n)

    r_specs, r_args, r_out, r_shape, r_scratch = _ret_parts(proj, decay_logit, rb, reverse, o_fwd[0], gn_w[0])
    g_specs, g_args, g_out, g_shape, g_scratch = _gla_parts(proj, bcum, rb, reverse, o_fwd[1], gn_w[1])
    return pl.pallas_call(
        functools.partial(_mixer_kernel, reverse=reverse, n_ret_in=len(r_specs), n_gla_in=len(g_specs),
                          n_ret_scratch=len(r_scratch)),
        grid=(batch, RET_HEADS, nb),
        in_specs=r_specs + g_specs,
        out_specs=[r_out, g_out],
        out_shape=[r_shape, g_shape],
        scratch_shapes=r_scratch + g_scratch,
        compiler_params=_cparams(("parallel", "parallel", "arbitrary"), 48),
        name="mixer_bwd" if reverse else "mixer_fwd",
    )(*r_args, *g_args)


OP_TM = 512


def _out_proj_kernel(mr_ref, mg_ref, w0_ref, w1_ref, x_ref, n2_ref, rh_ref, rl_ref, h_ref, xn_ref, aff_ref,
                     hs_ref):
    s = pl.program_id(0)
    slot = s % 2

    @pl.when(s == 0)
    def _():
        hs_ref[1] = jnp.zeros(hs_ref.shape[1:], F32)

    hp = hs_ref[1 - slot]
    ms = jnp.mean(hp * hp, axis=-1, keepdims=True)
    xn = hp * lax.rsqrt(ms + EPS) * n2_ref[...]
    xh = xn.astype(BF16)
    xn_ref[...] = _pack_bf16_pairs(xn)
    xl = (xn - xh.astype(F32)).astype(BF16)
    lt = _dot_nt(rh_ref[...], xh) + _dot_nt(rh_ref[...], xl) + _dot_nt(rl_ref[...], xh)
    m = jnp.max(lt, axis=0, keepdims=True)
    e = jnp.exp(lt - m)
    aff_ref[...] = e / jnp.sum(e, axis=0, keepdims=True)

    h = x_ref[...] + _dot(mr_ref[...], w0_ref[...]) + _dot(mg_ref[...], w1_ref[...])
    h_ref[...] = h
    hs_ref[slot] = h


def _out_proj(mix_r, mix_g, w_out, x2d, n2w, r_hi, r_lo):
    T = x2d.shape[0]
    tm = OP_TM
    half = RET_WIDTH
    nblk = T // tm

    def head(s):
        return jnp.minimum(s, nblk - 1)

    def tail(s):
        return jnp.maximum(s - 1, 0)

    return pl.pallas_call(
        _out_proj_kernel,
        grid=(nblk + 1,),
        in_specs=[
            pl.BlockSpec((tm, half), lambda s: (head(s), 0)),
            pl.BlockSpec((tm, half), lambda s: (head(s), 0)),
            pl.BlockSpec((half, D_MODEL), lambda s: (0, 0)),
            pl.BlockSpec((half, D_MODEL), lambda s: (1, 0)),
            pl.BlockSpec((tm, D_MODEL), lambda s: (head(s), 0)),
            pl.BlockSpec((1, D_MODEL), lambda s: (0, 0)),
            pl.BlockSpec((N_EXPERTS, D_MODEL), lambda s: (0, 0)),
            pl.BlockSpec((N_EXPERTS, D_MODEL), lambda s: (0, 0)),
        ],
        out_specs=[
            pl.BlockSpec((tm, D_MODEL), lambda s: (head(s), 0)),
            pl.BlockSpec((tm, D_MODEL // 2), lambda s: (tail(s), 0)),
            pl.BlockSpec((N_EXPERTS, tm), lambda s: (0, tail(s))),
        ],
        out_shape=[
            jax.ShapeDtypeStruct((T, D_MODEL), F32),
            jax.ShapeDtypeStruct((T, D_MODEL // 2), I32),
            jax.ShapeDtypeStruct((N_EXPERTS, T), F32),
        ],
        scratch_shapes=[pltpu.VMEM((2, tm, D_MODEL), F32)],
        compiler_params=_cparams(("arbitrary",), 60),
        name="out_proj",
    )(mix_r, mix_g, w_out, w_out, x2d, n2w, r_hi, r_lo)


def _select_kernel(a_ref, pos_ref, posb_ref, rankb_ref, affb_ref, *, cap):
    E, T = a_ref.shape
    tt = MOE_TT

    def count(pred):
        return jnp.sum(pred.astype(F32), axis=1, keepdims=True)

    def bisect(i, tau):
        cand = tau | jnp.left_shift(jnp.int32(1), 30 - i)
        bits = pltpu.bitcast(a_ref[...], I32)
        return jnp.where(count(bits >= cand) >= cap, cand, tau)

    tau = lax.fori_loop(0, 31, bisect, jnp.zeros((E, 1), I32))
    bits_all = pltpu.bitcast(a_ref[...], I32)
    quota = cap - count(bits_all > tau)

    before = (lax.broadcasted_iota(I32, (tt, tt), 0) < lax.broadcasted_iota(I32, (tt, tt), 1)).astype(BF16)
    below = (lax.broadcasted_iota(I32, (E, E), 1) < lax.broadcasted_iota(I32, (E, E), 0)).astype(BF16)

    def block(j, carry):
        c_eq, c_sel = carry
        off = pl.multiple_of(j * tt, tt)
        aff = a_ref[:, pl.ds(off, tt)]
        bits = pltpu.bitcast(aff, I32)
        eq = bits == tau
        eqf = eq.astype(F32)
        rank_eq = _dot(eqf.astype(BF16), before) + c_eq
        sel = (bits > tau) | (eq & (rank_eq < quota))
        self_ = sel.astype(F32)
        selb = self_.astype(BF16)
        slot = _dot(selb, before) + c_sel
        pos = jnp.where(sel, slot, -1.0).astype(I32)
        pos_ref[:, pl.ds(off, tt)] = pos
        per_tok = jnp.broadcast_to(jnp.sum(self_, axis=0, keepdims=True), (E, tt))
        rank = _dot(per_tok.astype(BF16), before) + _dot(below, selb)
        posb_ref[j] = pos
        rankb_ref[j] = jnp.where(sel, rank, -1.0).astype(I32)
        affb_ref[j] = aff
        return (c_eq + jnp.sum(eqf, axis=1, keepdims=True), c_sel + jnp.sum(self_, axis=1, keepdims=True))

    zero = jnp.zeros((E, 1), F32)
    lax.fori_loop(0, T // tt, block, (zero, zero))


def _select(aff, cap):
    E, T = aff.shape
    nb = T // MOE_TT
    blk = jax.ShapeDtypeStruct((nb, E, MOE_TT), I32)
    return pl.pallas_call(
        functools.partial(_select_kernel, cap=cap),
        out_shape=[jax.ShapeDtypeStruct((E, T), I32), blk, blk, jax.ShapeDtypeStruct((nb, E, MOE_TT), F32)],
        compiler_params=pltpu.CompilerParams(vmem_limit_bytes=40 * 1024 * 1024),
        name="select",
    )(aff)


def _regroup_rows(T):
    nb = T // MOE_TT
    rows = CAPACITY_FACTOR * T + SC_GATHER_ROWS * nb
    return -(-rows // MOE_TW) * MOE_TW


def _combine_schedule(posb, T):
    nb = posb.shape[0]
    tw, g = MOE_TW, SC_GATHER_ROWS
    n = jnp.sum((posb >= 0).reshape(nb, -1), axis=1).astype(I32)
    seg = (n + g - 1) // g * g
    hi = jnp.cumsum(seg)
    lo = hi - seg
    off = jnp.concatenate([jnp.zeros((1,), I32), hi])
    nwin_max = (N_EXPERTS * MOE_TT) // tw + 1
    w0 = lo // tw
    w1 = jnp.where(seg > 0, (hi - 1) // tw, w0)
    cand = jnp.arange(nwin_max, dtype=I32)
    win = w0[:, None] + cand[None, :]
    valid = (win <= w1[:, None]).reshape(-1)
    nwin_total = _regroup_rows(T) // tw
    pmax = nb + nwin_total
    jv = jnp.broadcast_to(jnp.arange(nb, dtype=I32)[:, None], win.shape).reshape(-1)
    wv = jnp.minimum(win, nwin_total - 1).reshape(-1)
    total = jnp.sum(valid.astype(I32))
    dst = jnp.where(valid, jnp.cumsum(valid.astype(I32)) - 1, pmax)
    pj, pw = (jnp.zeros((pmax,), I32).at[dst].set(a, mode="drop") for a in (jv, wv))
    real = jnp.arange(pmax, dtype=I32) < total
    pj, pw = (jnp.where(real, a, a[total - 1]) for a in (pj, pw))
    first = jnp.concatenate([jnp.ones((1,), bool), pj[1:] != pj[:-1]])
    last = jnp.concatenate([pj[1:] != pj[:-1], jnp.ones((1,), bool)]) | (jnp.arange(pmax, dtype=I32) == total - 1)
    flag = jnp.where(real, first.astype(I32) + 2 * last.astype(I32) + 4, 0)
    return off, (pj, pw, flag, lo[pj], hi[pj])


SC_LANES = 16
SC_CORES = 2
SC_SUBCORES = 16
SC_GATHER_ROWS = 32


def _dispatch(pos, xw, cap):
    E, T = pos.shape
    W = xw.shape[1]
    G = SC_GATHER_ROWS
    part_rows = cap // SC_CORES
    mesh = plsc.VectorSubcoreMesh(core_axis_name="c", subcore_axis_name="s")

    @pl.kernel(
        out_type=jax.ShapeDtypeStruct((E * cap, W), I32),
        mesh=mesh,
        scratch_types=[pltpu.VMEM((T,), I32), pltpu.VMEM((cap,), I32), pltpu.VMEM((G, W), I32)],
        compiler_params=pltpu.CompilerParams(needs_layout_passes=False),
        name="sc_dispatch",
    )
    def run(pos_hbm, x_hbm, xe_hbm, pos_v, idx_v, buf):
        e = lax.axis_index("s")
        part = lax.axis_index("c")
        pltpu.sync_copy(pos_hbm.at[e], pos_v)
        lane = lax.iota(I32, SC_LANES)

        @pl.loop(0, T // SC_LANES)
        def _(i):
            off = pl.multiple_of(i * SC_LANES, SC_LANES)
            p = pos_v[pl.ds(off, SC_LANES)]
            plsc.store_scatter(idx_v, [p], lane + off, mask=p >= 0)

        @pl.loop(0, part_rows // G)
        def _(g):
            o = pl.multiple_of(part * part_rows + g * G, G)
            pltpu.sync_copy(x_hbm.at[idx_v.at[pl.ds(o, G)]], buf)
            pltpu.sync_copy(buf, xe_hbm.at[pl.ds(e * cap + o, G)])

    return run(pos, xw)


FFN_TM = 2048
FFN_UNPACK_ROWS = 256


def _ffn_up_width(cap):
    return 512 if min(FFN_TM, cap) <= 1024 else 256


def _ffn_tile_width(cap):
    return 512


def _ffn_kernel(x_ref, wg_ref, wu_ref, wd_ref, *rest, tw):
    o_ref, xb_ref, hid_ref = rest[-3:]
    s = pl.program_id(2)
    n_up = D_FF // tw

    @pl.when(s == 0)
    def _():
        half = D_MODEL // 2

        def unpack_rows(i, carry):
            r = pl.multiple_of(i * FFN_UNPACK_ROWS, FFN_UNPACK_ROWS)
            lo, hi = _unpack_bf16_pairs(x_ref[pl.ds(r, FFN_UNPACK_ROWS), :])
            xb_ref[pl.ds(r, FFN_UNPACK_ROWS), :half] = lo.astype(BF16)
            xb_ref[pl.ds(r, FFN_UNPACK_ROWS), half:] = hi.astype(BF16)
            return carry

        lax.fori_loop(0, x_ref.shape[0] // FFN_UNPACK_ROWS, unpack_rows, 0)

    @pl.when(s < n_up)
    def _():
        x = xb_ref[...]
        g = _dot(x, wg_ref[...].astype(BF16))
        u = _dot(x, wu_ref[...].astype(BF16))
        col = pl.multiple_of(s * tw, tw)
        hid_ref[:, pl.ds(col, tw)] = (_silu(g) * u).astype(BF16)

    @pl.when(s >= n_up)
    def _():
        o_ref[...] = _pack_bf16_pairs(_dot(hid_ref[...], wd_ref[...].astype(BF16)))


def _ffn(xe, w_gate, w_up, w_down, experts=None, earlier=None):
    E, cap, _ = xe.shape
    e0, e1 = experts or (0, E)
    tm = min(FFN_TM, cap)
    tw, tg = _ffn_up_width(cap), _ffn_tile_width(cap)
    n_up, n_down = D_FF // tw, D_MODEL // tg

    def up(e, m, s):
        return (e + e0, 0, jnp.minimum(s, n_up - 1))

    def down(s):
        return jnp.maximum(s - n_up, 0)

    in_specs = [
        pl.BlockSpec((None, tm, D_MODEL // 2), lambda e, m, s: (e + e0, m, 0)),
        pl.BlockSpec((None, D_MODEL, tw), up),
        pl.BlockSpec((None, D_MODEL, tw), up),
        pl.BlockSpec((None, D_FF, tg), lambda e, m, s: (e + e0, 0, down(s))),
    ]
    args = [xe, w_gate, w_up, w_down]
    aliases = {}
    if earlier is not None:
        in_specs.append(pl.BlockSpec(memory_space=pl.ANY))
        args.append(earlier)
        aliases = {len(args) - 1: 0}
    return pl.pallas_call(
        functools.partial(_ffn_kernel, tw=tw),
        grid=(e1 - e0, cap // tm, n_up + n_down),
        in_specs=in_specs,
        out_specs=pl.BlockSpec((None, tm, tg // 2), lambda e, m, s: (e + e0, m, down(s))),
        out_shape=jax.ShapeDtypeStruct((E, cap, D_MODEL // 2), I32),
        scratch_shapes=[pltpu.VMEM((tm, D_MODEL), BF16), pltpu.VMEM((tm, D_FF), BF16)],
        input_output_aliases=aliases,
        compiler_params=_cparams(("parallel", "parallel", "arbitrary"), 60),
        name="ffn",
    )(*args)


def _regroup(posb, rankb, affb, off, yw, cap, rows):
    NB, EB = posb.shape
    W = yw.shape[1]
    G, L, tt = SC_GATHER_ROWS, SC_LANES, MOE_TT
    n_workers = SC_CORES * SC_SUBCORES
    split = max(1, n_workers // NB)
    per = -(-NB * split // n_workers)
    mesh = plsc.VectorSubcoreMesh(core_axis_name="c", subcore_axis_name="s")
    off_pad = jnp.pad(off, (0, L))

    @pl.kernel(
        out_type=(jax.ShapeDtypeStruct((rows, W), I32), jax.ShapeDtypeStruct((rows,), I32),
                  jax.ShapeDtypeStruct((rows,), F32)),
        mesh=mesh,
        scratch_types=[pltpu.VMEM((EB,), I32), pltpu.VMEM((EB,), I32), pltpu.VMEM((EB,), F32),
                       pltpu.VMEM((EB,), I32), pltpu.VMEM((EB,), I32), pltpu.VMEM((EB,), F32),
                       pltpu.VMEM((G, W), I32), pltpu.VMEM((NB + 1 + L,), I32)],
        compiler_params=pltpu.CompilerParams(needs_layout_passes=False),
        name="sc_regroup",
    )
    def run(posb_hbm, rankb_hbm, affb_hbm, off_hbm, y_hbm, yg_hbm, tok_hbm, gate_hbm,
            pos_v, rank_v, aff_v, src_v, tok_v, gate_v, buf, off_v):
        wid = lax.axis_index("c") * SC_SUBCORES + lax.axis_index("s")
        pltpu.sync_copy(off_hbm, off_v)
        lane = lax.iota(I32, L)
        zi = jnp.zeros((L,), I32)
        zf = jnp.zeros((L,), F32)

        def regroup_block(j, part):
            pltpu.sync_copy(posb_hbm.at[j], pos_v)
            pltpu.sync_copy(rankb_hbm.at[j], rank_v)
            pltpu.sync_copy(affb_hbm.at[j], aff_v)
            lo = jnp.max(plsc.load_gather(off_v, [zi + j]))
            hi = jnp.max(plsc.load_gather(off_v, [zi + j + 1]))

            @pl.loop(0, EB // L)
            def _(i):
                o = pl.multiple_of(i * L, L)
                src_v[pl.ds(o, L)] = zi
                tok_v[pl.ds(o, L)] = zi
                gate_v[pl.ds(o, L)] = zf

            @pl.loop(0, EB // L)
            def _(i):
                o = pl.multiple_of(i * L, L)
                p = pos_v[pl.ds(o, L)]
                r = rank_v[pl.ds(o, L)]
                m = p >= 0
                e = i // (tt // L)
                t0 = j * tt + (i % (tt // L)) * L
                plsc.store_scatter(src_v, [r], p + e * cap, mask=m)
                plsc.store_scatter(tok_v, [r], lane + t0, mask=m)
                plsc.store_scatter(gate_v, [r], aff_v[pl.ds(o, L)], mask=m)

            @pl.loop(part, (hi - lo) // G, step=split)
            def _(g):
                o = pl.multiple_of(g * G, G)
                dst = pl.multiple_of(lo + o, G)
                pltpu.sync_copy(y_hbm.at[src_v.at[pl.ds(o, G)]], buf)
                pltpu.sync_copy(buf, yg_hbm.at[pl.ds(dst, G)])
                pltpu.sync_copy(tok_v.at[pl.ds(o, G)], tok_hbm.at[pl.ds(dst, G)])
                pltpu.sync_copy(gate_v.at[pl.ds(o, G)], gate_hbm.at[pl.ds(dst, G)])

        @pl.loop(0, per)
        def _(k):
            unit = wid * per + k

            @pl.when(unit < NB * split)
            def _():
                regroup_block(unit // split, unit % split)

    return run(posb, rankb, affb, off_pad, yw)


def _combine_kernel(pj_ref, pw_ref, pf_ref, plo_ref, phi_ref, tok_ref, gate_ref, yg_ref, h_ref, nf_ref,
                    o_ref, acc_ref, *, group):
    p = pl.program_id(0)
    flag = pf_ref[p]
    tw, tt = MOE_TW, MOE_TT
    half = D_MODEL // 2

    hw = group // 2

    def col_blocks():
        for n in range(D_MODEL // group):
            yield slice(n * hw, (n + 1) * hw), slice(n * group, n * group + hw)
            yield slice(half + n * hw, half + (n + 1) * hw), slice(n * group + hw, (n + 1) * group)

    @pl.when((flag & 1) != 0)
    def _():
        for packed, natural in col_blocks():
            acc_ref[:, packed] = h_ref[:, natural]

    @pl.when((flag & 4) != 0)
    def _():
        lo, hi = plo_ref[p], phi_ref[p]
        row0 = pw_ref[p] * tw
        rid = lax.broadcasted_iota(I32, (tw, 1), 0) + row0
        keep = (rid >= lo) & (rid < hi)
        y_lo, y_hi = _unpack_bf16_pairs(yg_ref[...])
        y_lo = jnp.where(keep, y_lo, 0.0).astype(BF16)
        y_hi = jnp.where(keep, y_hi, 0.0).astype(BF16)
        tid = lax.broadcasted_iota(I32, (tt, tw), 0) + pj_ref[p] * tt
        cid = lax.broadcasted_iota(I32, (tt, tw), 1) + row0
        hit = (tok_ref[...] == tid) & (cid >= lo) & (cid < hi)
        weights = jnp.where(hit, gate_ref[...], 0.0).astype(BF16)
        acc_ref[:, :half] += _dot(weights, y_lo)
        acc_ref[:, half:] += _dot(weights, y_hi)

    @pl.when((flag & 2) != 0)
    def _():
        y = acc_ref[...]
        scale = lax.rsqrt(jnp.mean(y * y, axis=-1, keepdims=True) + EPS)
        for packed, natural in col_blocks():
            o_ref[:, natural] = acc_ref[:, packed] * scale * nf_ref[:, natural]


def _combine(lists, tok, gate, yg, h, nfw, group):
    pj, pw, pf, plo, phi = lists
    T = h.shape[0]
    tw, tt = MOE_TW, MOE_TT
    nwin = yg.shape[0] // tw
    grid_spec = pltpu.PrefetchScalarGridSpec(
        num_scalar_prefetch=5,
        grid=(pj.shape[0],),
        in_specs=[
            pl.BlockSpec((None, 1, tw), lambda p, pj, pw, *_: (pw[p], 0, 0)),
            pl.BlockSpec((None, 1, tw), lambda p, pj, pw, *_: (pw[p], 0, 0)),
            pl.BlockSpec((tw, D_MODEL // 2), lambda p, pj, pw, *_: (pw[p], 0)),
            pl.BlockSpec((tt, D_MODEL), lambda p, pj, pw, *_: (pj[p], 0)),
            pl.BlockSpec((1, D_MODEL), lambda p, pj, pw, *_: (0, 0)),
        ],
        out_specs=pl.BlockSpec((tt, D_MODEL), lambda p, pj, pw, *_: (pj[p], 0)),
        scratch_shapes=[pltpu.VMEM((tt, D_MODEL), F32)],
    )
    return pl.pallas_call(
        functools.partial(_combine_kernel, group=group),
        grid_spec=grid_spec,
        out_shape=jax.ShapeDtypeStruct((T, D_MODEL), F32),
        compiler_params=_cparams(("arbitrary",), 48),
        name="combine",
    )(pj, pw, pf, plo, phi, tok.reshape(nwin, 1, tw), gate.reshape(nwin, 1, tw), yg, h, nfw)


def _rope_tables(seq_len):
    d = RET_DK
    inv = ROPE_BASE ** (-jnp.arange(0, d, 2, dtype=F32) / d)
    ang = jnp.arange(seq_len, dtype=F32)[:, None] * inv[None, :]
    return jnp.cos(ang), jnp.sin(ang)


def _chunk_tri(n, chunk, upper):
    r = np.arange(n)
    same = (r[:, None] // chunk) == (r[None, :] // chunk)
    tri = (r[:, None] <= r[None, :]) if upper else (r[:, None] >= r[None, :])
    return jnp.asarray(same & tri, BF16)


def _prep_params(norm1_w, w_in, ret_gn_w, gla_gate_up, gla_gate_bias, gla_gn_w, w_out, norm2_w, router_w,
                 normf_w):
    w = w_in[0]
    w_main = w[:, :IN_MAIN].astype(BF16)
    w_ga = jnp.pad(w[:, IN_MAIN:], ((0, 0), (0, LANE - 2 * GLA_RANK))).astype(BF16)
    cs = np.ones((1, IN_MAIN), np.float32)
    cs[:, _RQ:_RQ + RET_WIDTH] = RET_DK ** -0.5
    cs[:, _GQ:_GQ + GLA_KEY_WIDTH] = GLA_DK ** -0.5
    up = gla_gate_up[0].astype(F32)
    up_pad = jnp.zeros((LANE, 2 * GLA_KEY_WIDTH), F32)
    up_pad = up_pad.at[:GLA_RANK, :GLA_KEY_WIDTH].set(up[0])
    up_pad = up_pad.at[GLA_RANK:2 * GLA_RANK, GLA_KEY_WIDTH:].set(up[1])
    rt = router_w[0].T.astype(F32)
    r_hi = rt.astype(BF16)
    r_lo = (rt - r_hi.astype(F32)).astype(BF16)
    return dict(
        n1w=norm1_w[0].reshape(1, D_MODEL).astype(F32),
        w_main=w_main, w_ga=w_ga, colscale=jnp.asarray(cs),
        up_pad=up_pad.astype(BF16),
        bias=gla_gate_bias[0].reshape(1, 2 * GLA_KEY_WIDTH).astype(F32),
        lf=_chunk_tri(GATE_SUB, GLA_CHUNK, upper=False),
        lb=_chunk_tri(GATE_SUB, GLA_CHUNK, upper=True),
        ret_gn=ret_gn_w[0].reshape(1, RET_WIDTH).astype(F32),
        gla_gn=gla_gn_w[0].reshape(1, GLA_WIDTH).astype(F32),
        w_out=w_out[0].astype(BF16),
        n2w=norm2_w[0].reshape(1, D_MODEL).astype(F32),
        r_hi=r_hi, r_lo=r_lo,
        nfw=normf_w.reshape(1, D_MODEL).astype(F32),
    )


def _trunk_route(x, pp, decay_logit):
    B, L, _ = x.shape
    T = B * L
    x2d = x.reshape(T, D_MODEL)
    cos, sin = _rope_tables(L)
    proj, ga = _in_proj(x2d, pp["n1w"], pp["w_main"], pp["w_ga"], pp["colscale"], cos, sin, L)
    b_f, b_b = _gla_gates(ga, pp["up_pad"], pp["bias"], pp["lf"], pp["lb"])

    fwd = _mixer_scan(proj, decay_logit, b_f, B, L, reverse=False)
    mix_r, mix_g = _mixer_scan(proj, decay_logit, b_b, B, L, reverse=True, o_fwd=fwd,
                               gn_w=(pp["ret_gn"], pp["gla_gn"]))

    h, xn2, aff = _out_proj(mix_r, mix_g, pp["w_out"], x2d, pp["n2w"], pp["r_hi"], pp["r_lo"])

    cap = CAPACITY_FACTOR * T // N_EXPERTS
    pos, posb, rankb, affb = _select(aff, cap)
    off, c_lists = _combine_schedule(posb, T)
    xe = _dispatch(pos, xn2, cap).reshape(N_EXPERTS, cap, D_MODEL // 2)
    return dict(xe=xe, h=h, posb=posb, rankb=rankb, affb=affb, off=off, c_lists=c_lists)


def _trunk_regroup(route, ye):
    T = route["h"].shape[0]
    cap = CAPACITY_FACTOR * T // N_EXPERTS
    nb = T // MOE_TT
    flat = lambda a: a.reshape(nb, -1)
    yg, tok, gate = _regroup(flat(route["posb"]), flat(route["rankb"]), flat(route["affb"]), route["off"],
                             ye.reshape(N_EXPERTS * cap, D_MODEL // 2), cap, _regroup_rows(T))
    return route["c_lists"], tok, gate, yg, route["h"]


def _trunk_back(front, nfw, shape):
    c_lists, tok, gate, yg, h = front
    cap = CAPACITY_FACTOR * h.shape[0] // N_EXPERTS
    return _combine(c_lists, tok, gate, yg, h, nfw, _ffn_tile_width(cap)).reshape(shape)


def kernel(x_prompt, x_sample, norm1_w, w_in, ret_decay_logit, ret_gn_w, gla_gate_up, gla_gate_bias,
           gla_gn_w, w_out, norm2_w, router_w, w_gate, w_up, w_down, normf_w):
    pp = _prep_params(norm1_w, w_in, ret_gn_w, gla_gate_up, gla_gate_bias, gla_gn_w, w_out, norm2_w,
                      router_w, normf_w)
    decay_logit = ret_decay_logit[0].astype(F32)
    weights = (w_gate[0], w_up[0], w_down[0])
    route_p = _trunk_route(x_prompt, pp, decay_logit)
    route_s = _trunk_route(x_sample, pp, decay_logit)
    half = N_EXPERTS // 2
    ye_p = _ffn(route_p["xe"], *weights, experts=(0, half))
    ye_s = _ffn(route_s["xe"], *weights)
    xe_p, ye_p, ye_s = lax.optimization_barrier((route_p["xe"], ye_p, ye_s))
    ye_p = _ffn(xe_p, *weights, experts=(half, N_EXPERTS), earlier=ye_p)
    front_s = _trunk_regroup(route_s, ye_s)
    front_p = _trunk_regroup(route_p, ye_p)
    y_sample = _trunk_back(front_s, pp["nfw"], x_sample.shape)
    front_p, y_sample = lax.optimization_barrier((front_p, y_sample))
    y_prompt = _trunk_back(front_p, pp["nfw"], x_prompt.shape)
    return (y_prompt, y_sample)
```
